```python
import math
import jax, jax.numpy as jnp
from jax import lax
import numpy as np

D_MODEL = 1024
BATCH = 8
SEQ = 16384
DEPTH = 1

N_META = 16
D_MIX = D_MODEL
EPS = 1e-6
MLA_HEADS = 8
QK_NOPE = 64
QK_ROPE = 32
V_HEAD = 64
Q_LORA = 256
KV_LORA = 128
D_ATTN = MLA_HEADS * V_HEAD
ROPE_THETA = 10000.0
Q_BLOCK = 128
D_SSM = D_MIX - D_ATTN
SSM_GROUP = 16
N_SSM_GROUPS = D_SSM // SSM_GROUP
SSM_STATE = 64
DT_MIN = 1e-3
DT_MAX = 1e-1
IN_SPLITS = (Q_LORA, KV_LORA, QK_ROPE, D_ATTN, D_SSM, D_SSM)
D_IN = sum(IN_SPLITS)

kernel_name = "hymba_mla_s5_bidir_block"


def rmsnorm(x, w):
    xf = x.astype(jnp.float32)
    y = xf * lax.rsqrt(jnp.mean(xf * xf, axis=-1, keepdims=True) + EPS)
    return (y * w.astype(jnp.float32)).astype(x.dtype)


def rope(x, pos):
    d = x.shape[-1]
    half = d // 2
    inv = ROPE_THETA ** (-jnp.arange(half, dtype=jnp.float32) / half)
    ang = pos.astype(jnp.float32)[:, None] * inv[None, :]
    cos = jnp.cos(ang)[None, :, None, :]
    sin = jnp.sin(ang)[None, :, None, :]
    xf = x.astype(jnp.float32)
    x1, x2 = xf[..., :half], xf[..., half:]
    out = jnp.concatenate([x1 * cos - x2 * sin, x1 * sin + x2 * cos], axis=-1)
    return out.astype(x.dtype)


def block_attention(q, k, v):
    b, h, L, dqk = q.shape
    dv = v.shape[-1]
    n_blk = -(-L // Q_BLOCK)
    pad = n_blk * Q_BLOCK - L
    qp = jnp.pad(q, ((0, 0), (0, 0), (0, pad), (0, 0)))
    qb = qp.reshape(b, h, n_blk, Q_BLOCK, dqk).transpose(2, 0, 1, 3, 4)
    scale = 1.0 / math.sqrt(dqk)

    def one_block(qblk):
        s = jnp.einsum('bhqd,bhkd->bhqk', qblk, k).astype(jnp.float32) * scale
        p = jax.nn.softmax(s, axis=-1)
        return jnp.einsum('bhqk,bhkd->bhqd', p.astype(v.dtype), v)

    out = lax.map(one_block, qb)
    out = out.transpose(1, 0, 3, 2, 4).reshape(b, n_blk * Q_BLOCK, h * dv)
    return out[:, :L]


def s5_direction(u, a_re, a_im, log_dt, b_re, b_im, c_re, c_im, reverse):
    dt = jnp.exp(log_dt.astype(jnp.float32))[:, None]
    a_re = a_re.astype(jnp.float32)
    a_im = a_im.astype(jnp.float32)
    mag = jnp.exp(a_re * dt)
    abar_re = mag * jnp.cos(a_im * dt)
    abar_im = mag * jnp.sin(a_im * dt)
    num_re = abar_re - 1.0
    num_im = abar_im
    den = a_re * a_re + a_im * a_im
    coef_re = (num_re * a_re + num_im * a_im) / den
    coef_im = (num_im * a_re - num_re * a_im) / den
    b_re = b_re.astype(jnp.float32)
    b_im = b_im.astype(jnp.float32)
    bbar_re = coef_re[..., None] * b_re - coef_im[..., None] * b_im
    bbar_im = coef_re[..., None] * b_im + coef_im[..., None] * b_re
    bu_re = jnp.einsum('blgh,gph->blgp', u, bbar_re)
    bu_im = jnp.einsum('blgh,gph->blgp', u, bbar_im)
    L = u.shape[1]
    g, p = abar_re.shape
    as_re = jnp.broadcast_to(abar_re[None, None], (1, L, g, p))
    as_im = jnp.broadcast_to(abar_im[None, None], (1, L, g, p))

    def combine(left, right):
        ar1, ai1, br1, bi1 = left
        ar2, ai2, br2, bi2 = right
        ar = ar2 * ar1 - ai2 * ai1
        ai = ar2 * ai1 + ai2 * ar1
        br = ar2 * br1 - ai2 * bi1 + br2
        bi = ar2 * bi1 + ai2 * br1 + bi2
        return (ar, ai, br, bi)

    _, _, x_re, x_im = lax.associative_scan(
        combine, (as_re, as_im, bu_re, bu_im), axis=1, reverse=reverse)
    return (jnp.einsum('blgp,ghp->blgh', x_re, c_re.astype(jnp.float32))
            - jnp.einsum('blgp,ghp->blgh', x_im, c_im.astype(jnp.float32)))


def hybrid_layer(h, pos, pre_norm_w, post_norm_w, w_in, q_norm_w, w_q_up, kv_norm_w,
                 w_kv_up, attn_out_norm_w, ssm_a_re, ssm_a_im, ssm_log_dt, ssm_b_re,
                 ssm_b_im, ssm_c_re, ssm_c_im, ssm_d, w_glu, b_glu, ssm_out_norm_w, w_out):
    b, L, _ = h.shape
    xn = rmsnorm(h, pre_norm_w)
    proj = jnp.einsum('bld,de->ble', xn, w_in)
    offs = np.cumsum(IN_SPLITS)[:-1].tolist()
    q_lat, kv_lat, k_rope, attn_gate, ssm_u, ssm_gate = jnp.split(proj, offs, axis=-1)

    q = jnp.einsum('blr,re->ble', rmsnorm(q_lat, q_norm_w), w_q_up)
    q = q.reshape(b, L, MLA_HEADS, QK_NOPE + QK_ROPE)
    q_nope, q_rope = q[..., :QK_NOPE], rope(q[..., QK_NOPE:], pos)
    kv = jnp.einsum('blr,re->ble', rmsnorm(kv_lat, kv_norm_w), w_kv_up)
    kv = kv.reshape(b, L, MLA_HEADS, QK_NOPE + V_HEAD)
    k_nope, v = kv[..., :QK_NOPE], kv[..., QK_NOPE:]
    k_r = rope(k_rope[:, :, None, :], pos)
    k_r = jnp.broadcast_to(k_r, (b, L, MLA_HEADS, QK_ROPE))
    qf = jnp.concatenate([q_nope, q_rope], axis=-1).transpose(0, 2, 1, 3)
    kf = jnp.concatenate([k_nope, k_r], axis=-1).transpose(0, 2, 1, 3)
    vf = v.transpose(0, 2, 1, 3)
    y_attn = block_attention(qf, kf, vf)
    y_attn = rmsnorm(y_attn * jax.nn.silu(attn_gate), attn_out_norm_w)

    u = ssm_u.astype(jnp.float32).reshape(b, L, N_SSM_GROUPS, SSM_GROUP)
    y_f = s5_direction(u, ssm_a_re[0], ssm_a_im[0], ssm_log_dt[0], ssm_b_re[0],
                       ssm_b_im[0], ssm_c_re[0], ssm_c_im[0], reverse=False)
    y_b = s5_direction(u, ssm_a_re[1], ssm_a_im[1], ssm_log_dt[1], ssm_b_re[1],
                       ssm_b_im[1], ssm_c_re[1], ssm_c_im[1], reverse=True)
    y_ssm = (y_f + y_b).reshape(b, L, D_SSM) + ssm_d.astype(jnp.float32) * ssm_u.astype(jnp.float32)
    y_ssm = jax.nn.gelu(y_ssm).astype(h.dtype)
    glu = jnp.einsum('ble,ef->blf', y_ssm, w_glu) + b_glu
    y_ssm = glu[..., :D_SSM] * jax.nn.sigmoid(glu[..., D_SSM:])
    y_ssm = rmsnorm(y_ssm * jax.nn.silu(ssm_gate), ssm_out_norm_w)

    y = jnp.concatenate([y_attn, y_ssm], axis=-1)
    y = jnp.einsum('ble,ed->bld', y, w_out)
    return h + rmsnorm(y, post_norm_w)


def _fwd_setup_inputs(seed: int = 0) -> dict:
    key = jax.random.key(seed)
    ks = jax.random.split(key, 24)
    f32 = jnp.float32

    def nrm(k, shape, fan_in):
        return jax.random.normal(k, shape, f32) * (fan_in ** -0.5)

    def gain(k, shape):
        return 1.0 + 0.02 * jax.random.normal(k, shape, f32)

    G, P, H = N_SSM_GROUPS, SSM_STATE, SSM_GROUP
    a_re = -0.5 + 0.01 * jax.random.normal(ks[10], (DEPTH, 2, G, P), f32)
    a_im = (jnp.pi * jnp.arange(P, dtype=f32))[None, None, None, :] \
        + 0.01 * jax.random.normal(ks[11], (DEPTH, 2, G, P), f32)
    log_dt = jax.random.uniform(ks[12], (DEPTH, 2, G), f32,
                                minval=math.log(DT_MIN), maxval=math.log(DT_MAX))
    return {
        "x": jax.random.normal(ks[0], (BATCH, SEQ, D_MODEL), f32),
        "meta_tokens": jax.random.normal(ks[1], (N_META, D_MODEL), f32),
        "pre_norm_w": gain(ks[2], (DEPTH, D_MODEL)),
        "post_norm_w": gain(ks[3], (DEPTH, D_MODEL)),
        "w_in": nrm(ks[4], (DEPTH, D_MODEL, D_IN), D_MODEL),
        "q_norm_w": gain(ks[5], (DEPTH, Q_LORA)),
        "w_q_up": nrm(ks[6], (DEPTH, Q_LORA, MLA_HEADS * (QK_NOPE + QK_ROPE)), Q_LORA),
        "kv_norm_w": gain(ks[7], (DEPTH, KV_LORA)),
        "w_kv_up": nrm(ks[8], (DEPTH, KV_LORA, MLA_HEADS * (QK_NOPE + V_HEAD)), KV_LORA),
        "attn_out_norm_w": gain(ks[9], (DEPTH, D_ATTN)),
        "ssm_a_re": a_re,
        "ssm_a_im": a_im,
        "ssm_log_dt": log_dt,
        "ssm_b_re": nrm(ks[13], (DEPTH, 2, G, P, H), 2 * H),
        "ssm_b_im": nrm(ks[14], (DEPTH, 2, G, P, H), 2 * H),
        "ssm_c_re": nrm(ks[15], (DEPTH, 2, G, H, P), 2 * P),
        "ssm_c_im": nrm(ks[16], (DEPTH, 2, G, H, P), 2 * P),
        "ssm_d": jax.random.normal(ks[17], (DEPTH, D_SSM), f32),
        "w_glu": nrm(ks[18], (DEPTH, D_SSM, 2 * D_SSM), D_SSM),
        "b_glu": 0.01 * jax.random.normal(ks[19], (DEPTH, 2 * D_SSM), f32),
        "ssm_out_norm_w": gain(ks[20], (DEPTH, D_SSM)),
        "w_out": nrm(ks[21], (DEPTH, D_MIX, D_MODEL), D_MIX),
    }


def _fwd_reference(x, meta_tokens, pre_norm_w, post_norm_w, w_in, q_norm_w, w_q_up, kv_norm_w,
              w_kv_up, attn_out_norm_w, ssm_a_re, ssm_a_im, ssm_log_dt, ssm_b_re, ssm_b_im,
              ssm_c_re, ssm_c_im, ssm_d, w_glu, b_glu, ssm_out_norm_w, w_out):
    b = x.shape[0]
    meta = jnp.broadcast_to(meta_tokens[None].astype(x.dtype), (b, N_META, x.shape[-1]))
    h = jnp.concatenate([meta, x], axis=1)
    pos = jnp.arange(h.shape[1], dtype=jnp.int32)
    for i in range(DEPTH):
        h = hybrid_layer(h, pos, pre_norm_w[i], post_norm_w[i], w_in[i], q_norm_w[i],
                         w_q_up[i], kv_norm_w[i], w_kv_up[i], attn_out_norm_w[i],
                         ssm_a_re[i], ssm_a_im[i], ssm_log_dt[i], ssm_b_re[i], ssm_b_im[i],
                         ssm_c_re[i], ssm_c_im[i], ssm_d[i], w_glu[i], b_glu[i],
                         ssm_out_norm_w[i], w_out[i])
    return h[:, N_META:]


import jax as _jax
import jax.numpy as _jnp

TWIN_FORMAT = 'train_step'
FWD_PARAMS = ['x', 'meta_tokens', 'pre_norm_w', 'post_norm_w', 'w_in', 'q_norm_w', 'w_q_up', 'kv_norm_w', 'w_kv_up', 'attn_out_norm_w', 'ssm_a_re', 'ssm_a_im', 'ssm_log_dt', 'ssm_b_re', 'ssm_b_im', 'ssm_c_re', 'ssm_c_im', 'ssm_d', 'w_glu', 'b_glu', 'ssm_out_norm_w', 'w_out']
TWIN_WEIGHTS = ['meta_tokens', 'pre_norm_w', 'post_norm_w', 'w_in', 'q_norm_w', 'w_q_up', 'kv_norm_w', 'w_kv_up', 'attn_out_norm_w', 'ssm_a_re', 'ssm_a_im', 'ssm_log_dt', 'ssm_b_re', 'ssm_b_im', 'ssm_c_re', 'ssm_c_im', 'ssm_d', 'w_glu', 'b_glu', 'ssm_out_norm_w', 'w_out']
TWIN_DIFF_INPUT = 'x'
TWIN_INPUTS = ['x', 'meta_tokens', 'pre_norm_w', 'post_norm_w', 'w_in', 'q_norm_w', 'w_q_up', 'kv_norm_w', 'w_kv_up', 'attn_out_norm_w', 'ssm_a_re', 'ssm_a_im', 'ssm_log_dt', 'ssm_b_re', 'ssm_b_im', 'ssm_c_re', 'ssm_c_im', 'ssm_d', 'w_glu', 'b_glu', 'ssm_out_norm_w', 'w_out', 'loss_target', 'm_meta_tokens', 'm_pre_norm_w', 'm_post_norm_w', 'm_w_in', 'm_q_norm_w', 'm_w_q_up', 'm_kv_norm_w', 'm_w_kv_up', 'm_attn_out_norm_w', 'm_ssm_a_re', 'm_ssm_a_im', 'm_ssm_log_dt', 'm_ssm_b_re', 'm_ssm_b_im', 'm_ssm_c_re', 'm_ssm_c_im', 'm_ssm_d', 'm_w_glu', 'm_b_glu', 'm_ssm_out_norm_w', 'm_w_out', 'v_meta_tokens', 'v_pre_norm_w', 'v_post_norm_w', 'v_w_in', 'v_q_norm_w', 'v_w_q_up', 'v_kv_norm_w', 'v_w_kv_up', 'v_attn_out_norm_w', 'v_ssm_a_re', 'v_ssm_a_im', 'v_ssm_log_dt', 'v_ssm_b_re', 'v_ssm_b_im', 'v_ssm_c_re', 'v_ssm_c_im', 'v_ssm_d', 'v_w_glu', 'v_b_glu', 'v_ssm_out_norm_w', 'v_w_out']
TWIN_OUTPUTS = ['loss', 'grad_x', 'grad_meta_tokens', 'grad_pre_norm_w', 'grad_post_norm_w', 'grad_w_in', 'grad_q_norm_w', 'grad_w_q_up', 'grad_kv_norm_w', 'grad_w_kv_up', 'grad_attn_out_norm_w', 'grad_ssm_a_re', 'grad_ssm_a_im', 'grad_ssm_log_dt', 'grad_ssm_b_re', 'grad_ssm_b_im', 'grad_ssm_c_re', 'grad_ssm_c_im', 'grad_ssm_d', 'grad_w_glu', 'grad_b_glu', 'grad_ssm_out_norm_w', 'grad_w_out', 'delta_meta_tokens', 'delta_pre_norm_w', 'delta_post_norm_w', 'delta_w_in', 'delta_q_norm_w', 'delta_w_q_up', 'delta_kv_norm_w', 'delta_w_kv_up', 'delta_attn_out_norm_w', 'delta_ssm_a_re', 'delta_ssm_a_im', 'delta_ssm_log_dt', 'delta_ssm_b_re', 'delta_ssm_b_im', 'delta_ssm_c_re', 'delta_ssm_c_im', 'delta_ssm_d', 'delta_w_glu', 'delta_b_glu', 'delta_ssm_out_norm_w', 'delta_w_out', 'new_m_meta_tokens', 'new_m_pre_norm_w', 'new_m_post_norm_w', 'new_m_w_in', 'new_m_q_norm_w', 'new_m_w_q_up', 'new_m_kv_norm_w', 'new_m_w_kv_up', 'new_m_attn_out_norm_w', 'new_m_ssm_a_re', 'new_m_ssm_a_im', 'new_m_ssm_log_dt', 'new_m_ssm_b_re', 'new_m_ssm_b_im', 'new_m_ssm_c_re', 'new_m_ssm_c_im', 'new_m_ssm_d', 'new_m_w_glu', 'new_m_b_glu', 'new_m_ssm_out_norm_w', 'new_m_w_out', 'new_v_meta_tokens', 'new_v_pre_norm_w', 'new_v_post_norm_w', 'new_v_w_in', 'new_v_q_norm_w', 'new_v_w_q_up', 'new_v_kv_norm_w', 'new_v_w_kv_up', 'new_v_attn_out_norm_w', 'new_v_ssm_a_re', 'new_v_ssm_a_im', 'new_v_ssm_log_dt', 'new_v_ssm_b_re', 'new_v_ssm_b_im', 'new_v_ssm_c_re', 'new_v_ssm_c_im', 'new_v_ssm_d', 'new_v_w_glu', 'new_v_b_glu', 'new_v_ssm_out_norm_w', 'new_v_w_out']
TWIN_LEAF_KINDS = {'loss': 'loss', 'grad_x': 'grad_x', 'grad_meta_tokens': 'grad_w', 'grad_pre_norm_w': 'grad_w', 'grad_post_norm_w': 'grad_w', 'grad_w_in': 'grad_w', 'grad_q_norm_w': 'grad_w', 'grad_w_q_up': 'grad_w', 'grad_kv_norm_w': 'grad_w', 'grad_w_kv_up': 'grad_w', 'grad_attn_out_norm_w': 'grad_w', 'grad_ssm_a_re': 'grad_w', 'grad_ssm_a_im': 'grad_w', 'grad_ssm_log_dt': 'grad_w', 'grad_ssm_b_re': 'grad_w', 'grad_ssm_b_im': 'grad_w', 'grad_ssm_c_re': 'grad_w', 'grad_ssm_c_im': 'grad_w', 'grad_ssm_d': 'grad_w', 'grad_w_glu': 'grad_w', 'grad_b_glu': 'grad_w', 'grad_ssm_out_norm_w': 'grad_w', 'grad_w_out': 'grad_w', 'delta_meta_tokens': 'delta_w', 'delta_pre_norm_w': 'delta_w', 'delta_post_norm_w': 'delta_w', 'delta_w_in': 'delta_w', 'delta_q_norm_w': 'delta_w', 'delta_w_q_up': 'delta_w', 'delta_kv_norm_w': 'delta_w', 'delta_w_kv_up': 'delta_w', 'delta_attn_out_norm_w': 'delta_w', 'delta_ssm_a_re': 'delta_w', 'delta_ssm_a_im': 'delta_w', 'delta_ssm_log_dt': 'delta_w', 'delta_ssm_b_re': 'delta_w', 'delta_ssm_b_im': 'delta_w', 'delta_ssm_c_re': 'delta_w', 'delta_ssm_c_im': 'delta_w', 'delta_ssm_d': 'delta_w', 'delta_w_glu': 'delta_w', 'delta_b_glu': 'delta_w', 'delta_ssm_out_norm_w': 'delta_w', 'delta_w_out': 'delta_w', 'new_m_meta_tokens': 'new_m', 'new_m_pre_norm_w': 'new_m', 'new_m_post_norm_w': 'new_m', 'new_m_w_in': 'new_m', 'new_m_q_norm_w': 'new_m', 'new_m_w_q_up': 'new_m', 'new_m_kv_norm_w': 'new_m', 'new_m_w_kv_up': 'new_m', 'new_m_attn_out_norm_w': 'new_m', 'new_m_ssm_a_re': 'new_m', 'new_m_ssm_a_im': 'new_m', 'new_m_ssm_log_dt': 'new_m', 'new_m_ssm_b_re': 'new_m', 'new_m_ssm_b_im': 'new_m', 'new_m_ssm_c_re': 'new_m', 'new_m_ssm_c_im': 'new_m', 'new_m_ssm_d': 'new_m', 'new_m_w_glu': 'new_m', 'new_m_b_glu': 'new_m', 'new_m_ssm_out_norm_w': 'new_m', 'new_m_w_out': 'new_m', 'new_v_meta_tokens': 'new_v', 'new_v_pre_norm_w': 'new_v', 'new_v_post_norm_w': 'new_v', 'new_v_w_in': 'new_v', 'new_v_q_norm_w': 'new_v', 'new_v_w_q_up': 'new_v', 'new_v_kv_norm_w': 'new_v', 'new_v_w_kv_up': 'new_v', 'new_v_attn_out_norm_w': 'new_v', 'new_v_ssm_a_re': 'new_v', 'new_v_ssm_a_im': 'new_v', 'new_v_ssm_log_dt': 'new_v', 'new_v_ssm_b_re': 'new_v', 'new_v_ssm_b_im': 'new_v', 'new_v_ssm_c_re': 'new_v', 'new_v_ssm_c_im': 'new_v', 'new_v_ssm_d': 'new_v', 'new_v_w_glu': 'new_v', 'new_v_b_glu': 'new_v', 'new_v_ssm_out_norm_w': 'new_v', 'new_v_w_out': 'new_v'}


def _forward(args):
    return _fwd_reference(*[args[k] for k in FWD_PARAMS])


def _output_shape():
    def fwd():
        inp = _fwd_setup_inputs(0)
        return _fwd_reference(*[inp[k] for k in FWD_PARAMS])
    out = _jax.eval_shape(fwd)
    return out.shape, out.dtype

N_MICROBATCH = 1
ADAM_LR = 0.001
ADAM_B1 = 0.9
ADAM_B2 = 0.999
ADAM_EPS = 1e-08
ADAM_WD = 0.01
ADAM_STEP = 10
PER_EXAMPLE_BATCH_AXIS = {'x': 0, 'loss_target': 0}
SHARED_INPUTS = []
_WEIGHT_DTYPES = {'meta_tokens': _jnp.float32, 'pre_norm_w': _jnp.float32, 'post_norm_w': _jnp.float32, 'w_in': _jnp.float32, 'q_norm_w': _jnp.float32, 'w_q_up': _jnp.float32, 'kv_norm_w': _jnp.float32, 'w_kv_up': _jnp.float32, 'attn_out_norm_w': _jnp.float32, 'ssm_a_re': _jnp.float32, 'ssm_a_im': _jnp.float32, 'ssm_log_dt': _jnp.float32, 'ssm_b_re': _jnp.float32, 'ssm_b_im': _jnp.float32, 'ssm_c_re': _jnp.float32, 'ssm_c_im': _jnp.float32, 'ssm_d': _jnp.float32, 'w_glu': _jnp.float32, 'b_glu': _jnp.float32, 'ssm_out_norm_w': _jnp.float32, 'w_out': _jnp.float32}
MOMENT_SCALE = {'meta_tokens': 2.538238e-02, 'pre_norm_w': 1.052202e+00, 'post_norm_w': 1.280919e+02, 'w_in': 7.054801e-01, 'q_norm_w': 1.027567e+00, 'w_q_up': 6.166566e-01, 'kv_norm_w': 3.802857e+00, 'w_kv_up': 7.694510e-01, 'attn_out_norm_w': 6.989436e-01, 'ssm_a_re': 3.287350e-02, 'ssm_a_im': 3.647832e-02, 'ssm_log_dt': 2.119916e+01, 'ssm_b_re': 2.040140e-02, 'ssm_b_im': 2.054406e-02, 'ssm_c_re': 4.200059e-02, 'ssm_c_im': 4.295784e-02, 'ssm_d': 1.174546e+00, 'w_glu': 7.691764e-01, 'b_glu': 2.593848e+00, 'ssm_out_norm_w': 9.956098e-01, 'w_out': 8.898773e-01}


def _to_microbatches(a, axis):
    t = _jnp.moveaxis(a, axis, 0)
    t = t.reshape((N_MICROBATCH, t.shape[0] // N_MICROBATCH) + t.shape[1:])
    return _jnp.moveaxis(t, 1, axis + 1)


def setup_inputs(seed: int = 0) -> dict:
    inp = _fwd_setup_inputs(seed)
    key = _jax.random.fold_in(_jax.random.key(seed), 7919)
    shape, _ = _output_shape()
    out = dict(inp)
    out["loss_target"] = _jax.random.normal(_jax.random.fold_in(key, 0), shape, _jnp.float32)
    for i, name in enumerate(TWIN_WEIGHTS):
        w = inp[name].astype(_jnp.float32)
        if MOMENT_SCALE is None:
            s = _jnp.sqrt(_jnp.mean(_jnp.square(w)) + 1e-30)
        else:
            s = MOMENT_SCALE[name]
        km, kv = _jax.random.split(_jax.random.fold_in(key, i + 1))
        out[name] = w
        out["m_" + name] = s * _jax.random.normal(km, w.shape, _jnp.float32)
        out["v_" + name] = (s * s) * _jax.random.uniform(kv, w.shape, _jnp.float32, 0.5, 1.5)
    if N_MICROBATCH > 1:
        for name, axis in PER_EXAMPLE_BATCH_AXIS.items():
            out[name] = _to_microbatches(out[name], axis)
    return {'x': out['x'], 'meta_tokens': out['meta_tokens'], 'pre_norm_w': out['pre_norm_w'], 'post_norm_w': out['post_norm_w'], 'w_in': out['w_in'], 'q_norm_w': out['q_norm_w'], 'w_q_up': out['w_q_up'], 'kv_norm_w': out['kv_norm_w'], 'w_kv_up': out['w_kv_up'], 'attn_out_norm_w': out['attn_out_norm_w'], 'ssm_a_re': out['ssm_a_re'], 'ssm_a_im': out['ssm_a_im'], 'ssm_log_dt': out['ssm_log_dt'], 'ssm_b_re': out['ssm_b_re'], 'ssm_b_im': out['ssm_b_im'], 'ssm_c_re': out['ssm_c_re'], 'ssm_c_im': out['ssm_c_im'], 'ssm_d': out['ssm_d'], 'w_glu': out['w_glu'], 'b_glu': out['b_glu'], 'ssm_out_norm_w': out['ssm_out_norm_w'], 'w_out': out['w_out'], 'loss_target': out['loss_target'], 'm_meta_tokens': out['m_meta_tokens'], 'm_pre_norm_w': out['m_pre_norm_w'], 'm_post_norm_w': out['m_post_norm_w'], 'm_w_in': out['m_w_in'], 'm_q_norm_w': out['m_q_norm_w'], 'm_w_q_up': out['m_w_q_up'], 'm_kv_norm_w': out['m_kv_norm_w'], 'm_w_kv_up': out['m_w_kv_up'], 'm_attn_out_norm_w': out['m_attn_out_norm_w'], 'm_ssm_a_re': out['m_ssm_a_re'], 'm_ssm_a_im': out['m_ssm_a_im'], 'm_ssm_log_dt': out['m_ssm_log_dt'], 'm_ssm_b_re': out['m_ssm_b_re'], 'm_ssm_b_im': out['m_ssm_b_im'], 'm_ssm_c_re': out['m_ssm_c_re'], 'm_ssm_c_im': out['m_ssm_c_im'], 'm_ssm_d': out['m_ssm_d'], 'm_w_glu': out['m_w_glu'], 'm_b_glu': out['m_b_glu'], 'm_ssm_out_norm_w': out['m_ssm_out_norm_w'], 'm_w_out': out['m_w_out'], 'v_meta_tokens': out['v_meta_tokens'], 'v_pre_norm_w': out['v_pre_norm_w'], 'v_post_norm_w': out['v_post_norm_w'], 'v_w_in': out['v_w_in'], 'v_q_norm_w': out['v_q_norm_w'], 'v_w_q_up': out['v_w_q_up'], 'v_kv_norm_w': out['v_kv_norm_w'], 'v_w_kv_up': out['v_w_kv_up'], 'v_attn_out_norm_w': out['v_attn_out_norm_w'], 'v_ssm_a_re': out['v_ssm_a_re'], 'v_ssm_a_im': out['v_ssm_a_im'], 'v_ssm_log_dt': out['v_ssm_log_dt'], 'v_ssm_b_re': out['v_ssm_b_re'], 'v_ssm_b_im': out['v_ssm_b_im'], 'v_ssm_c_re': out['v_ssm_c_re'], 'v_ssm_c_im': out['v_ssm_c_im'], 'v_ssm_d': out['v_ssm_d'], 'v_w_glu': out['v_w_glu'], 'v_b_glu': out['v_b_glu'], 'v_ssm_out_norm_w': out['v_ssm_out_norm_w'], 'v_w_out': out['v_w_out']}


def _loss(weights, diff, rest, loss_target):
    with _jax.named_scope("forward"):
        args = {**rest, TWIN_DIFF_INPUT: diff, **{k: w.astype(_WEIGHT_DTYPES[k]) for k, w in weights.items()}}
        y = _forward(args)
    with _jax.named_scope("loss_head"):
        err = _jnp.square(y.astype(_jnp.float32) - loss_target)
        return 0.5 * _jnp.sum(_jnp.mean(err, axis=-1)) if err.ndim else 0.5 * err


def _adamw(w, g, m, v):
    m = ADAM_B1 * m + (1.0 - ADAM_B1) * g
    v = ADAM_B2 * v + (1.0 - ADAM_B2) * _jnp.square(g)
    m_hat = m / (1.0 - ADAM_B1 ** ADAM_STEP)
    v_hat = v / (1.0 - ADAM_B2 ** ADAM_STEP)
    delta = -ADAM_LR * (m_hat / (_jnp.sqrt(v_hat) + ADAM_EPS) + ADAM_WD * w)
    return delta, m, v


def reference(x, meta_tokens, pre_norm_w, post_norm_w, w_in, q_norm_w, w_q_up, kv_norm_w, w_kv_up, attn_out_norm_w, ssm_a_re, ssm_a_im, ssm_log_dt, ssm_b_re, ssm_b_im, ssm_c_re, ssm_c_im, ssm_d, w_glu, b_glu, ssm_out_norm_w, w_out, loss_target, m_meta_tokens, m_pre_norm_w, m_post_norm_w, m_w_in, m_q_norm_w, m_w_q_up, m_kv_norm_w, m_w_kv_up, m_attn_out_norm_w, m_ssm_a_re, m_ssm_a_im, m_ssm_log_dt, m_ssm_b_re, m_ssm_b_im, m_ssm_c_re, m_ssm_c_im, m_ssm_d, m_w_glu, m_b_glu, m_ssm_out_norm_w, m_w_out, v_meta_tokens, v_pre_norm_w, v_post_norm_w, v_w_in, v_q_norm_w, v_w_q_up, v_kv_norm_w, v_w_kv_up, v_attn_out_norm_w, v_ssm_a_re, v_ssm_a_im, v_ssm_log_dt, v_ssm_b_re, v_ssm_b_im, v_ssm_c_re, v_ssm_c_im, v_ssm_d, v_w_glu, v_b_glu, v_ssm_out_norm_w, v_w_out):
    given = dict(x=x, meta_tokens=meta_tokens, pre_norm_w=pre_norm_w, post_norm_w=post_norm_w, w_in=w_in, q_norm_w=q_norm_w, w_q_up=w_q_up, kv_norm_w=kv_norm_w, w_kv_up=w_kv_up, attn_out_norm_w=attn_out_norm_w, ssm_a_re=ssm_a_re, ssm_a_im=ssm_a_im, ssm_log_dt=ssm_log_dt, ssm_b_re=ssm_b_re, ssm_b_im=ssm_b_im, ssm_c_re=ssm_c_re, ssm_c_im=ssm_c_im, ssm_d=ssm_d, w_glu=w_glu, b_glu=b_glu, ssm_out_norm_w=ssm_out_norm_w, w_out=w_out, loss_target=loss_target, m_meta_tokens=m_meta_tokens, m_pre_norm_w=m_pre_norm_w, m_post_norm_w=m_post_norm_w, m_w_in=m_w_in, m_q_norm_w=m_q_norm_w, m_w_q_up=m_w_q_up, m_kv_norm_w=m_kv_norm_w, m_w_kv_up=m_w_kv_up, m_attn_out_norm_w=m_attn_out_norm_w, m_ssm_a_re=m_ssm_a_re, m_ssm_a_im=m_ssm_a_im, m_ssm_log_dt=m_ssm_log_dt, m_ssm_b_re=m_ssm_b_re, m_ssm_b_im=m_ssm_b_im, m_ssm_c_re=m_ssm_c_re, m_ssm_c_im=m_ssm_c_im, m_ssm_d=m_ssm_d, m_w_glu=m_w_glu, m_b_glu=m_b_glu, m_ssm_out_norm_w=m_ssm_out_norm_w, m_w_out=m_w_out, v_meta_tokens=v_meta_tokens, v_pre_norm_w=v_pre_norm_w, v_post_norm_w=v_post_norm_w, v_w_in=v_w_in, v_q_norm_w=v_q_norm_w, v_w_q_up=v_w_q_up, v_kv_norm_w=v_kv_norm_w, v_w_kv_up=v_w_kv_up, v_attn_out_norm_w=v_attn_out_norm_w, v_ssm_a_re=v_ssm_a_re, v_ssm_a_im=v_ssm_a_im, v_ssm_log_dt=v_ssm_log_dt, v_ssm_b_re=v_ssm_b_re, v_ssm_b_im=v_ssm_b_im, v_ssm_c_re=v_ssm_c_re, v_ssm_c_im=v_ssm_c_im, v_ssm_d=v_ssm_d, v_w_glu=v_w_glu, v_b_glu=v_b_glu, v_ssm_out_norm_w=v_ssm_out_norm_w, v_w_out=v_w_out)
    weights = {n: given[n] for n in TWIN_WEIGHTS}
    shared = {n: given[n] for n in SHARED_INPUTS}
    per_example = {n: given[n] for n in ['x']}
    grad_fn = _jax.value_and_grad(_loss, argnums=(0, 1))

    def one_microbatch(ex, loss_target):
        ex = dict(ex)
        diff = ex.pop(TWIN_DIFF_INPUT)
        return grad_fn(weights, diff, {**shared, **ex}, loss_target)

    if N_MICROBATCH == 1:
        loss, (grad_w, grad_x) = one_microbatch(per_example, given["loss_target"])
    else:
        def body(carry, xs):
            loss_sum, grad_sum = carry
            l_k, (gw_k, gx_k) = one_microbatch(xs[0], xs[1])
            with _jax.named_scope("update"):
                return (loss_sum + l_k, _jax.tree.map(_jnp.add, grad_sum, gw_k)), gx_k

        init = (_jnp.zeros((), _jnp.float32), _jax.tree.map(_jnp.zeros_like, weights))
        (loss, grad_w), grad_x = _jax.lax.scan(body, init, (per_example, given["loss_target"]))
    with _jax.named_scope("update"):
        delta_w, new_m, new_v = {}, {}, {}
        for n in TWIN_WEIGHTS:
            delta_w[n], new_m[n], new_v[n] = _adamw(weights[n], grad_w[n], given["m_" + n], given["v_" + n])
    return (loss, grad_x, *[grad_w[n] for n in TWIN_WEIGHTS], *[delta_w[n] for n in TWIN_WEIGHTS],
            *[new_m[n] for n in TWIN_WEIGHTS], *[new_v[n] for n in TWIN_WEIGHTS])
```

```python
import functools
import math

import numpy as np
import jax
import jax.numpy as jnp
from jax import lax
from jax.experimental import pallas as pl
from jax.experimental.pallas import tpu as pltpu

F32 = jnp.float32
BF16 = jnp.bfloat16

D_MODEL = 1024
N_META = 16
EPS = 1e-6
HEADS = 8
QK_NOPE = 64
QK_ROPE = 32
HALF_ROPE = QK_ROPE // 2
QK_DIM = QK_NOPE + QK_ROPE
V_HEAD = 64
Q_LORA = 256
KV_LORA = 128
D_ATTN = HEADS * V_HEAD
D_SSM = 512
SSM_GROUP = 16
N_GROUPS = D_SSM // SSM_GROUP
SSM_STATE = 64
N_STATES = N_GROUPS * SSM_STATE
ROPE_THETA = 10000.0
D_IN = Q_LORA + KV_LORA + QK_ROPE + D_ATTN + 2 * D_SSM
D_IN_PAD = 2048
N_DEV = 8
N_SEG = 8
COL_BLK = 512
LANES = 128

ADAM_LR = 0.001
ADAM_B1 = 0.9
ADAM_B2 = 0.999
ADAM_EPS = 1e-08
ADAM_WD = 0.01
ADAM_STEP = 10

VMEM_LIMIT_V7X = 56 * 1024 * 1024

WEIGHTS = ['meta_tokens', 'pre_norm_w', 'post_norm_w', 'w_in', 'q_norm_w', 'w_q_up', 'kv_norm_w', 'w_kv_up',
           'attn_out_norm_w', 'ssm_a_re', 'ssm_a_im', 'ssm_log_dt', 'ssm_b_re', 'ssm_b_im', 'ssm_c_re', 'ssm_c_im',
           'ssm_d', 'w_glu', 'b_glu', 'ssm_out_norm_w', 'w_out']
SHARDED = ['w_in', 'w_q_up', 'w_kv_up', 'w_glu', 'w_out', 'meta_tokens']

def _cols_in(w):
    return jnp.concatenate([w[:, 0:384], w[:, 416:D_IN], w[:, 384:416]], axis=1)


def _cols_in_inv(w):
    return jnp.concatenate([w[:, 0:384], w[:, D_IN - QK_ROPE:D_IN], w[:, 384:D_IN - QK_ROPE]], axis=1)


def _cols_q(w):
    t = w.reshape(w.shape[0], HEADS, QK_DIM)
    return jnp.concatenate([t[:, :, 0:64].reshape(-1, 512), t[:, :, 64:80].reshape(-1, 128),
                            t[:, :, 80:96].reshape(-1, 128)], axis=1)


def _cols_q_inv(w):
    r = w.shape[0]
    return jnp.concatenate([w[:, 0:512].reshape(r, HEADS, 64), w[:, 512:640].reshape(r, HEADS, 16),
                            w[:, 640:768].reshape(r, HEADS, 16)], axis=2).reshape(r, HEADS * QK_DIM)


def _cols_kv(w):
    t = w.reshape(w.shape[0], HEADS, 128)
    return jnp.concatenate([t[:, :, 0:64].reshape(-1, 512), t[:, :, 64:128].reshape(-1, 512)], axis=1)


def _cols_kv_inv(w):
    r = w.shape[0]
    return jnp.concatenate([w[:, 0:512].reshape(r, HEADS, 64), w[:, 512:1024].reshape(r, HEADS, 64)],
                           axis=2).reshape(r, HEADS * 128)


def _pick(n, cands):
    for c in cands:
        if n % c == 0:
            return c
    raise ValueError(f"no tile for {n}")


def _cparams(*sem):
    return pltpu.CompilerParams(dimension_semantics=sem, vmem_limit_bytes=VMEM_LIMIT_V7X)


def _mm(a, b):
    return jnp.dot(a.astype(BF16), b.astype(BF16), preferred_element_type=F32)


def _mm_nt(a, b):
    return lax.dot_general(a.astype(BF16), b.astype(BF16), (((1,), (1,)), ((), ())), preferred_element_type=F32)


def _mm_tn(a, b):
    return lax.dot_general(a.astype(BF16), b.astype(BF16), (((0,), (0,)), ((), ())), preferred_element_type=F32)


def _mm_exact(a, b):
    return jnp.dot(a, b, precision=lax.Precision.HIGHEST, preferred_element_type=F32)


def _rms(x):
    return lax.rsqrt(jnp.mean(x * x, axis=-1, keepdims=True) + EPS)


def _rms_bwd(dy, x, r, w):
    xh = x * r
    g = dy * w
    dx = r * (g - xh * jnp.mean(g * xh, axis=-1, keepdims=True))
    dw = jnp.sum(dy * xh, axis=0, keepdims=True)
    return dx, dw


def _sigmoid(z):
    return 1.0 / (1.0 + jnp.exp(-z))


def _silu_and_grad(z):
    s = _sigmoid(z)
    return z * s, s * (1.0 + z * (1.0 - s))


_GELU_C = math.sqrt(2.0 / math.pi)


def _gelu_and_grad(x):
    x2 = x * x
    t = jnp.tanh(_GELU_C * (x + 0.044715 * x * x2))
    val = 0.5 * x * (1.0 + t)
    grad = 0.5 * (1.0 + t) + 0.5 * x * (1.0 - t * t) * _GELU_C * (1.0 + 3.0 * 0.044715 * x2)
    return val, grad


def _acc(ref, val, first):
    @pl.when(first)
    def _():
        ref[...] = val

    @pl.when(jnp.logical_not(first))
    def _():
        ref[...] += val


def _rows_call(name, body, tr, row_ins, full_ins, row_outs, acc_outs):
    lp = row_ins[0].shape[0]
    in_specs = [pl.BlockSpec((tr, a.shape[1]), lambda i: (i, 0)) for a in row_ins]
    in_specs += [pl.BlockSpec(a.shape, lambda i, n=a.ndim: (0,) * n) for a in full_ins]
    out_specs = [pl.BlockSpec((tr, c), lambda i: (i, 0)) for c, _ in row_outs]
    out_specs += [pl.BlockSpec(s, lambda i, n=len(s): (0,) * n) for s, _ in acc_outs]
    out_shape = [jax.ShapeDtypeStruct((lp, c), dt) for c, dt in row_outs]
    out_shape += [jax.ShapeDtypeStruct(s, dt) for s, dt in acc_outs]
    return pl.pallas_call(
        body, name=name, grid=(lp // tr,), in_specs=in_specs, out_specs=out_specs, out_shape=out_shape,
        compiler_params=_cparams("arbitrary"))(*row_ins, *full_ins)


def _inproj(h, pre_w, w_in_b, tr):
    def body(h_ref, pw_ref, w_ref, ql, kvl, ag, su, sg, kr):
        x = h_ref[...]
        xn = x * _rms(x) * pw_ref[...]
        pr = _mm(xn, w_ref[...])
        ql[...] = pr[:, 0:256]
        kvl[...] = pr[:, 256:384]
        ag[...] = pr[:, 384:896]
        su[...] = pr[:, 896:1408]
        sg[...] = pr[:, 1408:1920]
        kr[...] = pr[:, 1920:1952]

    return _rows_call("inproj", body, tr, [h], [pre_w, w_in_b],
                      [(256, F32), (128, F32), (512, F32), (512, F32), (512, F32), (32, F32)], [])


def _qkv_up(ql, kvl, kr, cos8, sin8, c32, s32, qw, kvw, wq_b, wkv_b, p32, tr):
    def body(ql_ref, kvl_ref, kr_ref, cos_ref, sin_ref, c32_ref, s32_ref, qw_ref, kvw_ref, wq_ref, wkv_ref, p_ref,
             qn_o, qr1_o, qr2_o, kn_o, v_o, kr_o):
        x = ql_ref[...]
        q = _mm(x * _rms(x) * qw_ref[...], wq_ref[...])
        r1, r2 = q[:, 512:640], q[:, 640:768]
        cs, sn = cos_ref[...], sin_ref[...]
        qn_o[...] = q[:, 0:512].astype(BF16)
        qr1_o[...] = (r1 * cs - r2 * sn).astype(BF16)
        qr2_o[...] = (r2 * cs + r1 * sn).astype(BF16)
        x = kvl_ref[...]
        kv = _mm(x * _rms(x) * kvw_ref[...], wkv_ref[...])
        kn_o[...] = kv[:, 0:512].astype(BF16)
        v_o[...] = kv[:, 512:1024].astype(BF16)
        x = kr_ref[...]
        kr_o[...] = (x * c32_ref[...] + _mm_exact(x, p_ref[...]) * s32_ref[...]).astype(BF16)

    return _rows_call("qkv_up", body, tr, [ql, kvl, kr, cos8, sin8, c32, s32], [qw, kvw, wq_b, wkv_b, p32],
                      [(512, BF16), (128, BF16), (128, BF16), (512, BF16), (512, BF16), (32, BF16)], [])


def _attn_fwd(q, k, v, n_valid, tq, tk):
    lp = q.shape[1]
    nq, nk = lp // tq, lp // tk
    scale = 1.0 / math.sqrt(QK_DIM)

    def body(q_ref, k_ref, v_ref, o_ref, lse_ref, m_s, l_s, acc_s):
        ki = pl.program_id(2)

        @pl.when(ki == 0)
        def _():
            m_s[...] = jnp.full(m_s.shape, -1e30, F32)
            l_s[...] = jnp.zeros(l_s.shape, F32)
            acc_s[...] = jnp.zeros(acc_s.shape, F32)

        s = _mm_nt(q_ref[0], k_ref[0]) * scale
        col = ki * tk + lax.broadcasted_iota(jnp.int32, (tq, tk), 1)
        s = jnp.where(col < n_valid, s, -1e30)
        m_old = m_s[...]
        m_new = jnp.maximum(m_old, jnp.max(s, axis=1, keepdims=True))
        alpha = jnp.exp(m_old - m_new)
        p = jnp.exp(s - m_new)
        l_s[...] = alpha * l_s[...] + jnp.sum(p, axis=1, keepdims=True)
        acc_s[...] = alpha * acc_s[...] + _mm(p, v_ref[0])
        m_s[...] = m_new

        @pl.when(ki == nk - 1)
        def _():
            o_ref[0] = acc_s[...] / l_s[...]
            lse_ref[0] = m_s[...] + jnp.log(l_s[...])

    return pl.pallas_call(
        body, name="attn_fwd", grid=(HEADS, nq, nk),
        in_specs=[pl.BlockSpec((1, tq, QK_DIM), lambda h, i, j: (h, i, 0)),
                  pl.BlockSpec((1, tk, QK_DIM), lambda h, i, j: (h, j, 0)),
                  pl.BlockSpec((1, tk, V_HEAD), lambda h, i, j: (h, j, 0))],
        out_specs=[pl.BlockSpec((1, tq, V_HEAD), lambda h, i, j: (h, i, 0)),
                   pl.BlockSpec((1, tq, 1), lambda h, i, j: (h, i, 0))],
        out_shape=[jax.ShapeDtypeStruct((HEADS, lp, V_HEAD), F32), jax.ShapeDtypeStruct((HEADS, lp, 1), F32)],
        scratch_shapes=[pltpu.VMEM((tq, 1), F32), pltpu.VMEM((tq, 1), F32), pltpu.VMEM((tq, V_HEAD), F32)],
        compiler_params=_cparams("arbitrary", "arbitrary", "arbitrary"))(q, k, v)


def _attn_post(o_flat, ag, aw, tr):
    def body(o_ref, g_ref, w_ref, ya):
        t = o_ref[...] * _silu_and_grad(g_ref[...])[0]
        ya[...] = t * _rms(t) * w_ref[...]

    return _rows_call("attn_post", body, tr, [o_flat, ag], [aw], [(512, F32)], [])[0]


def _seg_tiles(lp):
    lseg = lp // N_SEG
    trs = _pick(lseg, [208, 104, 48, 32, 16, 8])
    return lseg, trs, lseg // trs


def _seg_mm(name, urows, w_b):
    lp = urows.shape[0]
    lseg, trs, nts = _seg_tiles(lp)

    def body(u_ref, w_ref, re_o, im_o):
        r = _mm(u_ref[...], w_ref[...])
        re_o[...] = r[:, :N_STATES]
        im_o[...] = r[:, N_STATES:]

    outs = pl.pallas_call(
        body, name=name, grid=(N_SEG, nts),
        in_specs=[pl.BlockSpec((trs, D_SSM), lambda j, i: (j * nts + i, 0)),
                  pl.BlockSpec(w_b.shape, lambda j, i: (0, 0))],
        out_specs=[pl.BlockSpec((trs, N_STATES), lambda j, i: (i, j))] * 2,
        out_shape=[jax.ShapeDtypeStruct((lseg, N_SEG * N_STATES), F32)] * 2,
        compiler_params=_cparams("arbitrary", "arbitrary"))(urows, w_b)
    return [o.reshape(lp, N_STATES) for o in outs]


def _seg_nt(name, xs, wt_b, rows_add, dvec):
    lp = rows_add.shape[0]
    lseg, trs, nts = _seg_tiles(lp)
    n = len(xs)

    def body(*refs):
        x_refs, w_ref, a_ref, d_ref, y_o = refs[:n], refs[n], refs[n + 1], refs[n + 2], refs[n + 3]
        y = a_ref[...] * d_ref[...]
        for p in range(n):
            y = y + _mm_nt(x_refs[p][...], w_ref[:, p * N_STATES:(p + 1) * N_STATES])
        y_o[...] = y

    xv = [x.reshape(lseg, N_SEG * N_STATES) for x in xs]
    return pl.pallas_call(
        body, name=name, grid=(N_SEG, nts),
        in_specs=[pl.BlockSpec((trs, N_STATES), lambda j, i: (i, j))] * n
        + [pl.BlockSpec(wt_b.shape, lambda j, i: (0, 0)),
           pl.BlockSpec((trs, D_SSM), lambda j, i: (j * nts + i, 0)),
           pl.BlockSpec(dvec.shape, lambda j, i: (0, 0))],
        out_specs=pl.BlockSpec((trs, D_SSM), lambda j, i: (j * nts + i, 0)),
        out_shape=jax.ShapeDtypeStruct((lp, D_SSM), F32),
        compiler_params=_cparams("arbitrary", "arbitrary"))(*xv, wt_b, rows_add, dvec)


def _seg_tn(name, urows, xre, xim):
    lp = urows.shape[0]
    lseg, trs, nts = _seg_tiles(lp)

    def body(u_ref, re_ref, im_ref, dre_o, dim_o):
        first = jnp.logical_and(pl.program_id(0) == 0, pl.program_id(1) == 0)
        u = u_ref[...]
        _acc(dre_o, _mm_tn(u, re_ref[...]), first)
        _acc(dim_o, _mm_tn(u, im_ref[...]), first)

    xv = [x.reshape(lseg, N_SEG * N_STATES) for x in (xre, xim)]
    return pl.pallas_call(
        body, name=name, grid=(N_SEG, nts),
        in_specs=[pl.BlockSpec((trs, D_SSM), lambda j, i: (j * nts + i, 0))]
        + [pl.BlockSpec((trs, N_STATES), lambda j, i: (i, j))] * 2,
        out_specs=[pl.BlockSpec((D_SSM, N_STATES), lambda j, i: (0, 0))] * 2,
        out_shape=[jax.ShapeDtypeStruct((D_SSM, N_STATES), F32)] * 2,
        compiler_params=_cparams("arbitrary", "arbitrary"))(urows, *xv)


def _scan_tiles(lp):
    lseg = lp // N_SEG
    tt = _pick(lseg, [104, 48, 32, 16, 8, 4, 2, 1])
    return lseg, tt, lseg // tt


def _cmul(ar, ai, br, bi):
    return ar * br - ai * bi, ar * bi + ai * br


def _scan_local(name, bre, bim, ar8, ai8, forward):
    lp = bre.shape[0]
    lseg, tt, nt = _scan_tiles(lp)
    ncb = N_STATES // COL_BLK

    def tile(t):
        return t if forward else nt - 1 - t

    def body(bre_ref, bim_ref, ar_ref, ai_ref, xre_o, xim_o, ere_o, eim_o, cr_s, ci_s):
        t = pl.program_id(1)

        @pl.when(t == 0)
        def _():
            cr_s[...] = jnp.zeros(cr_s.shape, F32)
            ci_s[...] = jnp.zeros(ci_s.shape, F32)

        ar, ai = ar_ref[...], ai_ref[...]

        def step(s, carry):
            cr, ci = carry
            r0 = pl.multiple_of((s if forward else tt - 1 - s) * N_SEG, N_SEG)
            pr, pi = _cmul(ar, ai, cr, ci)
            xr = pr + bre_ref[pl.ds(r0, N_SEG), :]
            xi = pi + bim_ref[pl.ds(r0, N_SEG), :]
            xre_o[pl.ds(r0, N_SEG), :] = xr
            xim_o[pl.ds(r0, N_SEG), :] = xi
            return xr, xi

        cr, ci = lax.fori_loop(0, tt, step, (cr_s[...], ci_s[...]), unroll=4 if tt % 4 == 0 else 1)
        cr_s[...] = cr
        ci_s[...] = ci

        @pl.when(t == nt - 1)
        def _():
            ere_o[...] = cr
            eim_o[...] = ci

    blk = pl.BlockSpec((tt * N_SEG, COL_BLK), lambda cb, t: (tile(t), cb))
    small = pl.BlockSpec((N_SEG, COL_BLK), lambda cb, t: (0, cb))
    return pl.pallas_call(
        body, name=name, grid=(ncb, nt), in_specs=[blk, blk, small, small], out_specs=[blk, blk, small, small],
        out_shape=[jax.ShapeDtypeStruct((lp, N_STATES), F32)] * 2 + [jax.ShapeDtypeStruct((N_SEG, N_STATES), F32)] * 2,
        scratch_shapes=[pltpu.VMEM((N_SEG, COL_BLK), F32)] * 2,
        compiler_params=_cparams("arbitrary", "arbitrary"))(bre, bim, ar8, ai8)


def _scan_fix(name, lre, lim, ar8, ai8, ere, eim, forward, states=None):
    lp = lre.shape[0]
    lseg, tt, nt = _scan_tiles(lp)
    ncb = N_STATES // COL_BLK
    adj = states is not None

    def tile(t):
        return t if forward else nt - 1 - t

    def body(*refs):
        lre_ref, lim_ref, ar_ref, ai_ref, ere_ref, eim_ref = refs[:6]
        k = 6
        if adj:
            sre_ref, sim_ref = refs[k:k + 2]
            k += 2
        xre_o, xim_o = refs[k:k + 2]
        k += 2
        if adj:
            dre_o, dim_o = refs[k:k + 2]
            k += 2
        s_re, s_im, pw_re, pw_im = refs[k:k + 4]
        k += 4
        if adj:
            gp_re, gp_im = refs[k:k + 2]
        t = pl.program_id(1)
        ar, ai = ar_ref[...], ai_ref[...]

        @pl.when(t == 0)
        def _():
            a1r, a1i = ar[0:1, :], ai[0:1, :]
            pr, pi = jnp.ones_like(a1r), jnp.zeros_like(a1i)
            br, bi = a1r, a1i
            n = lseg
            while n:
                if n & 1:
                    pr, pi = _cmul(pr, pi, br, bi)
                n >>= 1
                if n:
                    br, bi = _cmul(br, bi, br, bi)
            cr, ci = jnp.zeros_like(a1r), jnp.zeros_like(a1i)
            order = range(N_SEG) if forward else range(N_SEG - 1, -1, -1)
            for j in order:
                s_re[j:j + 1, :] = cr
                s_im[j:j + 1, :] = ci
                nr, ni = _cmul(pr, pi, cr, ci)
                cr = nr + ere_ref[j:j + 1, :]
                ci = ni + eim_ref[j:j + 1, :]
            pw_re[...] = ar
            pw_im[...] = ai
            if adj:
                gp_re[...] = s_re[...]
                gp_im[...] = s_im[...]
                dre_o[...] = jnp.zeros(dre_o.shape, F32)
                dim_o[...] = jnp.zeros(dim_o.shape, F32)

        sr, si = s_re[...], s_im[...]

        def step(s, carry):
            r0 = pl.multiple_of((s if forward else tt - 1 - s) * N_SEG, N_SEG)
            pr, pi = carry[0], carry[1]
            cr, ci = _cmul(pr, pi, sr, si)
            xr = lre_ref[pl.ds(r0, N_SEG), :] + cr
            xi = lim_ref[pl.ds(r0, N_SEG), :] + ci
            xre_o[pl.ds(r0, N_SEG), :] = xr
            xim_o[pl.ds(r0, N_SEG), :] = xi
            npr, npi = _cmul(pr, pi, ar, ai)
            if not adj:
                return npr, npi
            gr, gi, dr, di = carry[2:]
            fr = sre_ref[pl.ds(r0, N_SEG), :]
            fi = sim_ref[pl.ds(r0, N_SEG), :]
            dr = dr + fr * gr + fi * gi
            di = di + fr * gi - fi * gr
            return npr, npi, xr, xi, dr, di

        init = (pw_re[...], pw_im[...])
        if adj:
            init = init + (gp_re[...], gp_im[...], dre_o[...], dim_o[...])
        out = lax.fori_loop(0, tt, step, init, unroll=4 if tt % 4 == 0 else 1)
        pw_re[...] = out[0]
        pw_im[...] = out[1]
        if adj:
            gp_re[...] = out[2]
            gp_im[...] = out[3]
            dre_o[...] = out[4]
            dim_o[...] = out[5]

    blk = pl.BlockSpec((tt * N_SEG, COL_BLK), lambda cb, t: (tile(t), cb))
    small = pl.BlockSpec((N_SEG, COL_BLK), lambda cb, t: (0, cb))
    ins = [lre, lim, ar8, ai8, ere, eim] + (list(states) if adj else [])
    in_specs = [blk, blk, small, small, small, small] + ([blk, blk] if adj else [])
    out_specs = [blk, blk] + ([small, small] if adj else [])
    out_shape = [jax.ShapeDtypeStruct((lp, N_STATES), F32)] * 2
    if adj:
        out_shape += [jax.ShapeDtypeStruct((N_SEG, N_STATES), F32)] * 2
    scratch = [pltpu.VMEM((N_SEG, COL_BLK), F32)] * (6 if adj else 4)
    return pl.pallas_call(
        body, name=name, grid=(ncb, nt), in_specs=in_specs, out_specs=out_specs, out_shape=out_shape,
        scratch_shapes=scratch, compiler_params=_cparams("arbitrary", "arbitrary"))(*ins)


def _scan(name, bre, bim, ar8, ai8, forward, states=None):
    lre, lim, ere, eim = _scan_local(name + "_local", bre, bim, ar8, ai8, forward)
    return _scan_fix(name + "_fix", lre, lim, ar8, ai8, ere, eim, forward, states)


def _ssm_post(ypre, sg, wglu_b, bglu, sw, tr):
    def body(y_ref, g_ref, w_ref, b_ref, sw_ref, glu_o, ysn_o):
        glu = _mm(_gelu_and_grad(y_ref[...])[0], w_ref[...]) + b_ref[...]
        glu_o[...] = glu
        t = glu[:, :D_SSM] * _sigmoid(glu[:, D_SSM:]) * _silu_and_grad(g_ref[...])[0]
        ysn_o[...] = t * _rms(t) * sw_ref[...]

    return _rows_call("ssm_post", body, tr, [ypre, sg], [wglu_b, bglu, sw], [(1024, F32), (512, F32)], [])


def _out_loss(ya, ysn, h, tgt, wo_b, post_w, n_valid, tr):
    def body(ya_ref, ys_ref, h_ref, t_ref, w_ref, pw_ref, dy_o, dout_o, loss_o, dpw_o):
        i = pl.program_id(0)
        y = _mm(ya_ref[...], w_ref[0:D_ATTN, :]) + _mm(ys_ref[...], w_ref[D_ATTN:, :])
        r = _rms(y)
        pw = pw_ref[...]
        out = h_ref[...] + y * r * pw
        row = i * tr + lax.broadcasted_iota(jnp.int32, (tr, 1), 0)
        valid = jnp.logical_and(row >= N_META, row < n_valid)
        diff = jnp.where(valid, out - t_ref[...], 0.0)
        dout = diff * (1.0 / D_MODEL)
        dy, dpw = _rms_bwd(dout, y, r, pw)
        dy_o[...] = dy
        dout_o[...] = dout
        _acc(loss_o, 0.5 * jnp.sum(jnp.sum(diff * diff, axis=1, keepdims=True), axis=0, keepdims=True)
             * (1.0 / D_MODEL), i == 0)
        _acc(dpw_o, dpw, i == 0)

    return _rows_call("out_loss", body, tr, [ya, ysn, h, tgt], [wo_b, post_w], [(1024, F32), (1024, F32)],
                      [((1, 1), F32), ((1, D_MODEL), F32)])


def _out_bwd(dy, ya, ysn, o_flat, ag, wo_b, aw, tr):
    def body(dy_ref, ya_ref, ys_ref, o_ref, g_ref, w_ref, aw_ref, do_o, dag_o, dysn_o, dwo_o, daw_o):
        i = pl.program_id(0)
        dy = dy_ref[...]
        dcat = _mm_nt(dy, w_ref[...])
        cat = jnp.concatenate([ya_ref[...], ys_ref[...]], axis=1)
        _acc(dwo_o, _mm_tn(cat, dy), i == 0)
        dysn_o[...] = dcat[:, D_ATTN:]
        o = o_ref[...]
        sl, dsl = _silu_and_grad(g_ref[...])
        t = o * sl
        dt, daw = _rms_bwd(dcat[:, :D_ATTN], t, _rms(t), aw_ref[...])
        _acc(daw_o, daw, i == 0)
        do_o[...] = dt * sl
        dag_o[...] = dt * o * dsl

    return _rows_call("out_bwd", body, tr, [dy, ya, ysn, o_flat, ag], [wo_b, aw],
                      [(512, F32), (512, F32), (512, F32)], [((D_MODEL, D_MODEL), F32), ((1, D_ATTN), F32)])


def _ssm_post_bwd(dysn, glu, sg, ypre, u, wglu_b, sw, dvec, tr):
    def body(d_ref, glu_ref, sg_ref, y_ref, u_ref, w_ref, sw_ref, dv_ref,
             dyp_o, dsg_o, dwg_o, dbg_o, dsw_o, dd_o):
        i = pl.program_id(0)
        glu = glu_ref[...]
        a, b = glu[:, :D_SSM], glu[:, D_SSM:]
        sb = _sigmoid(b)
        ys = a * sb
        sl, dsl = _silu_and_grad(sg_ref[...])
        t = ys * sl
        dt, dsw = _rms_bwd(d_ref[...], t, _rms(t), sw_ref[...])
        _acc(dsw_o, dsw, i == 0)
        dsg_o[...] = dt * ys * dsl
        dys = dt * sl
        dglu = jnp.concatenate([dys * sb, dys * a * sb * (1.0 - sb)], axis=1)
        _acc(dbg_o, jnp.sum(dglu, axis=0, keepdims=True), i == 0)
        gel, dgel = _gelu_and_grad(y_ref[...])
        _acc(dwg_o, _mm_tn(gel, dglu), i == 0)
        dyp = _mm_nt(dglu, w_ref[...]) * dgel
        dyp_o[...] = dyp
        _acc(dd_o, jnp.sum(dyp * u_ref[...], axis=0, keepdims=True), i == 0)

    return _rows_call("ssm_post_bwd", body, tr, [dysn, glu, sg, ypre, u], [wglu_b, sw, dvec],
                      [(512, F32), (512, F32)],
                      [((D_SSM, 2 * D_SSM), F32), ((1, 2 * D_SSM), F32), ((1, D_SSM), F32), ((1, D_SSM), F32)])


def _attn_bwd_dq(q, k, v, do, o, lse, n_valid, tq, tk):
    lp = q.shape[1]
    nq, nk = lp // tq, lp // tk
    scale = 1.0 / math.sqrt(QK_DIM)

    def body(q_ref, k_ref, v_ref, do_ref, o_ref, lse_ref, dq_o, delta_o, acc_s, dl_s):
        ki = pl.program_id(2)

        @pl.when(ki == 0)
        def _():
            acc_s[...] = jnp.zeros(acc_s.shape, F32)
            dl_s[...] = jnp.sum(do_ref[0] * o_ref[0], axis=1, keepdims=True)

        kk = k_ref[0]
        s = _mm_nt(q_ref[0], kk) * scale
        col = ki * tk + lax.broadcasted_iota(jnp.int32, (tq, tk), 1)
        p = jnp.where(col < n_valid, jnp.exp(s - lse_ref[0]), 0.0)
        dp = _mm_nt(do_ref[0], v_ref[0])
        ds = p * (dp - dl_s[...]) * scale
        acc_s[...] += _mm(ds, kk)

        @pl.when(ki == nk - 1)
        def _():
            dq_o[0] = acc_s[...]
            delta_o[0] = dl_s[...]

    qspec = lambda w: pl.BlockSpec((1, tq, w), lambda h, i, j: (h, i, 0))
    kspec = lambda w: pl.BlockSpec((1, tk, w), lambda h, i, j: (h, j, 0))
    return pl.pallas_call(
        body, name="attn_bwd_dq", grid=(HEADS, nq, nk),
        in_specs=[qspec(QK_DIM), kspec(QK_DIM), kspec(V_HEAD), qspec(V_HEAD), qspec(V_HEAD), qspec(1)],
        out_specs=[qspec(QK_DIM), qspec(1)],
        out_shape=[jax.ShapeDtypeStruct((HEADS, lp, QK_DIM), F32), jax.ShapeDtypeStruct((HEADS, lp, 1), F32)],
        scratch_shapes=[pltpu.VMEM((tq, QK_DIM), F32), pltpu.VMEM((tq, 1), F32)],
        compiler_params=_cparams("arbitrary", "arbitrary", "arbitrary"))(q, k, v, do, o, lse)


def _attn_bwd_dkv(q, k, v, do, lse_t, delta_t, n_valid, tq, tk):
    lp = q.shape[1]
    nq, nk = lp // tq, lp // tk
    scale = 1.0 / math.sqrt(QK_DIM)

    def body(q_ref, k_ref, v_ref, do_ref, lse_ref, dl_ref, dk_o, dv_o, dk_s, dv_s):
        qi = pl.program_id(2)
        ki = pl.program_id(1)

        @pl.when(qi == 0)
        def _():
            dk_s[...] = jnp.zeros(dk_s.shape, F32)
            dv_s[...] = jnp.zeros(dv_s.shape, F32)

        qq = q_ref[0]
        dd = do_ref[0]
        st = _mm_nt(k_ref[0], qq) * scale
        row = ki * tk + lax.broadcasted_iota(jnp.int32, (tk, tq), 0)
        pt = jnp.where(row < n_valid, jnp.exp(st - lse_ref[0]), 0.0)
        dv_s[...] += _mm(pt, dd)
        dpt = _mm_nt(v_ref[0], dd)
        dst = pt * (dpt - dl_ref[0]) * scale
        dk_s[...] += _mm(dst, qq)

        @pl.when(qi == nq - 1)
        def _():
            dk_o[0] = dk_s[...]
            dv_o[0] = dv_s[...]

    qspec = lambda w: pl.BlockSpec((1, tq, w), lambda h, j, i: (h, i, 0))
    kspec = lambda w: pl.BlockSpec((1, tk, w), lambda h, j, i: (h, j, 0))
    tspec = pl.BlockSpec((1, 1, tq), lambda h, j, i: (h, 0, i))
    return pl.pallas_call(
        body, name="attn_bwd_dkv", grid=(HEADS, nk, nq),
        in_specs=[qspec(QK_DIM), kspec(QK_DIM), kspec(V_HEAD), qspec(V_HEAD), tspec, tspec],
        out_specs=[kspec(QK_DIM), kspec(V_HEAD)],
        out_shape=[jax.ShapeDtypeStruct((HEADS, lp, QK_DIM), F32), jax.ShapeDtypeStruct((HEADS, lp, V_HEAD), F32)],
        scratch_shapes=[pltpu.VMEM((tk, QK_DIM), F32), pltpu.VMEM((tk, V_HEAD), F32)],
        compiler_params=_cparams("arbitrary", "arbitrary", "arbitrary"))(q, k, v, do, lse_t, delta_t)


def _qkv_up_bwd(dqn, dr1, dr2, dkn, dv, dkr8, ql, kvl, cos8, sin8, c32, s32, qw, kvw, wq_b, wkv_b, p32, sum8, tr):
    def body(dqn_ref, dr1_ref, dr2_ref, dkn_ref, dv_ref, dkr_ref, ql_ref, kvl_ref, cos_ref, sin_ref, c32_ref,
             s32_ref, qw_ref, kvw_ref, wq_ref, wkv_ref, p_ref, s8_ref,
             dql_o, dkvl_o, dkrr_o, dwq_o, dwkv_o, dqw_o, dkvw_o):
        i = pl.program_id(0)
        cs, sn = cos_ref[...], sin_ref[...]
        d1, d2 = dr1_ref[...], dr2_ref[...]
        dq = jnp.concatenate([dqn_ref[...], d1 * cs + d2 * sn, d2 * cs - d1 * sn], axis=1)
        x = ql_ref[...]
        r = _rms(x)
        qw = qw_ref[...]
        _acc(dwq_o, _mm_tn(x * r * qw, dq), i == 0)
        dx, dw = _rms_bwd(_mm_nt(dq, wq_ref[...]), x, r, qw)
        dql_o[...] = dx
        _acc(dqw_o, dw, i == 0)
        dkv = jnp.concatenate([dkn_ref[...], dv_ref[...]], axis=1)
        x = kvl_ref[...]
        r = _rms(x)
        kvw = kvw_ref[...]
        _acc(dwkv_o, _mm_tn(x * r * kvw, dkv), i == 0)
        dx, dw = _rms_bwd(_mm_nt(dkv, wkv_ref[...]), x, r, kvw)
        dkvl_o[...] = dx
        _acc(dkvw_o, dw, i == 0)
        dkr = _mm_exact(dkr_ref[...], s8_ref[...])
        dkrr_o[...] = dkr * c32_ref[...] + _mm_exact(dkr * s32_ref[...], p_ref[...])

    return _rows_call("qkv_up_bwd", body, tr, [dqn, dr1, dr2, dkn, dv, dkr8, ql, kvl, cos8, sin8, c32, s32],
                      [qw, kvw, wq_b, wkv_b, p32, sum8], [(256, F32), (128, F32), (32, F32)],
                      [((Q_LORA, 768), F32), ((KV_LORA, 1024), F32), ((1, Q_LORA), F32), ((1, KV_LORA), F32)])


def _inproj_bwd(dql, dkvl, dag, du, dsg, dkr, h, dout, pre_w, w_in_b, tr):
    def body(dql_ref, dkvl_ref, dag_ref, du_ref, dsg_ref, dkr_ref, h_ref, dout_ref, pw_ref, w_ref,
             dh_o, dwin_o, dpw_o):
        i = pl.program_id(0)
        dproj = jnp.concatenate([dql_ref[...], dkvl_ref[...], dag_ref[...], du_ref[...], dsg_ref[...],
                                 dkr_ref[...], jnp.zeros((tr, D_IN_PAD - D_IN), F32)], axis=1)
        x = h_ref[...]
        r = _rms(x)
        pw = pw_ref[...]
        _acc(dwin_o, _mm_tn(x * r * pw, dproj), i == 0)
        dx, dw = _rms_bwd(_mm_nt(dproj, w_ref[...]), x, r, pw)
        _acc(dpw_o, dw, i == 0)
        dh_o[...] = dout_ref[...] + dx

    return _rows_call("inproj_bwd", body, tr, [dql, dkvl, dag, du, dsg, dkr, h, dout], [pre_w, w_in_b],
                      [(1024, F32)], [((D_MODEL, D_IN_PAD), F32), ((1, D_MODEL), F32)])


def _disc_terms(a_re, a_im, ldt):
    dt = jnp.exp(ldt)
    mag = jnp.exp(a_re * dt)
    th = a_im * dt
    cs, sn = jnp.cos(th), jnp.sin(th)
    abar_re, abar_im = mag * cs, mag * sn
    num_re, num_im = abar_re - 1.0, abar_im
    den = a_re * a_re + a_im * a_im
    coef_re = (num_re * a_re + num_im * a_im) / den
    coef_im = (num_im * a_re - num_re * a_im) / den
    return dt, mag, cs, sn, abar_re, abar_im, num_re, num_im, den, coef_re, coef_im


def _ssm_disc(a_re, a_im, ldt, bt_re, bt_im):
    def body(ar_ref, ai_ref, l_ref, br_ref, bi_ref, abr_o, abi_o, bbr_o, bbi_o):
        t = _disc_terms(ar_ref[...], ai_ref[...], l_ref[...])
        abr_o[...] = t[4]
        abi_o[...] = t[5]
        cr, ci = t[9], t[10]
        br, bi = br_ref[...], bi_ref[...]
        bbr_o[...] = cr * br - ci * bi
        bbi_o[...] = cr * bi + ci * br

    ng = a_re.shape[0]
    return pl.pallas_call(
        body, name="ssm_disc",
        out_shape=[jax.ShapeDtypeStruct((ng, 1, SSM_STATE), F32)] * 2
        + [jax.ShapeDtypeStruct((ng, SSM_GROUP, SSM_STATE), F32)] * 2)(a_re, a_im, ldt, bt_re, bt_im)


def _ssm_disc_bwd(a_re, a_im, ldt, bt_re, bt_im, da8_re, da8_im, dbb_re, dbb_im):
    def body(ar_ref, ai_ref, l_ref, br_ref, bi_ref, dar_ref, dai_ref, dbr_ref, dbi_ref,
             gar_o, gai_o, gl_o, gbr_o, gbi_o):
        a_re, a_im = ar_ref[...], ai_ref[...]
        dt, mag, cs, sn, abar_re, abar_im, num_re, num_im, den, cr, ci = _disc_terms(a_re, a_im, l_ref[...])
        br, bi = br_ref[...], bi_ref[...]
        dbr, dbi = dbr_ref[...], dbi_ref[...]
        gbr_o[...] = cr * dbr + ci * dbi
        gbi_o[...] = cr * dbi - ci * dbr
        dcr = jnp.sum(br * dbr + bi * dbi, axis=1, keepdims=True)
        dci = jnp.sum(br * dbi - bi * dbr, axis=1, keepdims=True)
        dnum_re = (dcr * a_re - dci * a_im) / den
        dnum_im = (dcr * a_im + dci * a_re) / den
        dden = -(dcr * cr + dci * ci) / den
        g_are = (dcr * num_re + dci * num_im) / den + dden * 2.0 * a_re
        g_aim = (dcr * num_im - dci * num_re) / den + dden * 2.0 * a_im
        d_abr = jnp.sum(dar_ref[...], axis=1, keepdims=True) + dnum_re
        d_abi = jnp.sum(dai_ref[...], axis=1, keepdims=True) + dnum_im
        dmag = d_abr * cs + d_abi * sn
        dth = d_abi * abar_re - d_abr * abar_im
        g_are = g_are + dmag * mag * dt
        g_aim = g_aim + dth * dt
        ddt = jnp.sum(dmag * mag * a_re + dth * a_im, axis=2, keepdims=True)
        gar_o[...] = g_are
        gai_o[...] = g_aim
        gl_o[...] = ddt * dt

    ng = a_re.shape[0]
    return pl.pallas_call(
        body, name="ssm_disc_bwd",
        out_shape=[jax.ShapeDtypeStruct((ng, 1, SSM_STATE), F32)] * 2 + [jax.ShapeDtypeStruct((ng, 1, 1), F32)]
        + [jax.ShapeDtypeStruct((ng, SSM_GROUP, SSM_STATE), F32)] * 2)(
            a_re, a_im, ldt, bt_re, bt_im, da8_re, da8_im, dbb_re, dbb_im)


def _exchange(name, buf, all_to_all):
    rows = buf.shape[-2]

    def body(in_ref, out_ref, send_sems, recv_sems, local_sem):
        x, y, c = lax.axis_index("x"), lax.axis_index("y"), lax.axis_index("c")
        me = 4 * x + 2 * y + c
        copies = []
        for k in range(1, N_DEV):
            px = 1 - x if (k >> 2) & 1 else x
            py = 1 - y if (k >> 1) & 1 else y
            pc = 1 - c if k & 1 else c
            src = in_ref.at[4 * px + 2 * py + pc] if all_to_all else in_ref
            copies.append(pltpu.make_async_remote_copy(
                src_ref=src, dst_ref=out_ref.at[me], send_sem=send_sems.at[k - 1], recv_sem=recv_sems.at[k - 1],
                device_id=(px, py, pc), device_id_type=pl.DeviceIdType.MESH))
        mine = pltpu.make_async_copy(in_ref.at[me] if all_to_all else in_ref, out_ref.at[me], local_sem)
        mine.start()
        for cp in copies:
            cp.start()
        for cp in copies:
            cp.wait()
        mine.wait()

    return pl.pallas_call(
        body, name=name, out_shape=jax.ShapeDtypeStruct((N_DEV, rows, LANES), F32),
        in_specs=[pl.BlockSpec(memory_space=pl.ANY)], out_specs=pl.BlockSpec(memory_space=pl.ANY),
        scratch_shapes=[pltpu.SemaphoreType.DMA((N_DEV - 1,)), pltpu.SemaphoreType.DMA((N_DEV - 1,)),
                        pltpu.SemaphoreType.DMA(())])(buf)


def _adamw(recv, w, m, v, tr):
    rows = w.shape[0]
    c1 = 1.0 - ADAM_B1 ** ADAM_STEP
    c2 = 1.0 - ADAM_B2 ** ADAM_STEP

    def body(r_ref, w_ref, m_ref, v_ref, g_o, d_o, m_o, v_o):
        g = r_ref[0]
        for k in range(1, N_DEV):
            g = g + r_ref[k]
        mm = ADAM_B1 * m_ref[...] + (1.0 - ADAM_B1) * g
        vv = ADAM_B2 * v_ref[...] + (1.0 - ADAM_B2) * (g * g)
        g_o[...] = g
        m_o[...] = mm
        v_o[...] = vv
        d_o[...] = -ADAM_LR * ((mm / c1) / (jnp.sqrt(vv / c2) + ADAM_EPS) + ADAM_WD * w_ref[...])

    spec = pl.BlockSpec((tr, LANES), lambda i: (i, 0))
    return pl.pallas_call(
        body, name="adamw", grid=(rows // tr,),
        in_specs=[pl.BlockSpec((N_DEV, tr, LANES), lambda i: (0, i, 0)), spec, spec, spec],
        out_specs=[spec] * 4, out_shape=[jax.ShapeDtypeStruct((rows, LANES), F32)] * 4,
        compiler_params=_cparams("arbitrary"))(recv, w, m, v)


def _to_rows(a):
    flat = a.reshape(-1)
    pad = (-flat.shape[0]) % LANES
    if pad:
        flat = jnp.concatenate([flat, jnp.zeros((pad,), flat.dtype)])
    return flat.reshape(-1, LANES)


def _n_rows(shape):
    return -(-int(np.prod(shape)) // LANES)


def _pack(arrays, total_rows):
    rows = [_to_rows(a) for a in arrays]
    used = sum(r.shape[0] for r in rows)
    if total_rows > used:
        rows.append(jnp.zeros((total_rows - used, LANES), F32))
    return jnp.concatenate(rows, axis=0)


def _unpack(buf, shapes):
    out, r0 = [], 0
    for s in shapes:
        n = int(np.prod(s))
        nr = _n_rows(s)
        out.append(buf[r0:r0 + nr].reshape(-1)[:n].reshape(s))
        r0 += nr
    return out


def _shard_views(name, full):
    if name == 'w_out':
        return full.reshape(N_DEV, full.shape[0] // N_DEV, full.shape[1])
    r, ccols = full.shape
    return full.reshape(r, N_DEV, ccols // N_DEV).transpose(1, 0, 2)


def _from_shards(name, stacked):
    if name == 'w_out':
        return stacked.reshape(-1, stacked.shape[-1])
    n, r, cc = stacked.shape
    return stacked.transpose(1, 0, 2).reshape(r, n * cc)


def _block_diag(t):
    eye = jnp.eye(N_GROUPS, dtype=t.dtype)
    return (t[:, :, None, :] * eye[:, None, :, None]).reshape(D_SSM, N_STATES)


def _diag_blocks(mat):
    return jnp.stack([mat[g * SSM_GROUP:(g + 1) * SSM_GROUP, g * SSM_STATE:(g + 1) * SSM_STATE]
                      for g in range(N_GROUPS)])


def _step(x, loss_target, wts, moms, vels):
    seq = x.shape[1]
    n_valid = N_META + seq
    lp = -(-n_valid // 256) * 256
    tr = _pick(lp, [640, 256])
    tr_small = 256
    tq = _pick(lp, [640, 256])
    tk = tq

    shard_shapes = [wts[n].shape[-2:] for n in SHARDED]
    n_shard_rows = sum(_n_rows(s) for s in shard_shapes)
    gathered = _exchange("gather_weights", _pack([wts[n].reshape(wts[n].shape[-2:]) for n in SHARDED],
                                                 n_shard_rows), all_to_all=False)
    parts = [[] for _ in SHARDED]
    for dev in range(N_DEV):
        for i, a in enumerate(_unpack(gathered[dev], shard_shapes)):
            parts[i].append(a)
    full = {n: _from_shards(n, jnp.stack(parts[i])) for i, n in enumerate(SHARDED)}

    w_in_b = jnp.concatenate([_cols_in(full['w_in']), jnp.zeros((D_MODEL, D_IN_PAD - D_IN), F32)],
                             axis=1).astype(BF16)
    wq_b = _cols_q(full['w_q_up']).astype(BF16)
    wkv_b = _cols_kv(full['w_kv_up']).astype(BF16)
    wglu_b = full['w_glu'].astype(BF16)
    wo_b = full['w_out'].astype(BF16)
    pre_w, post_w = wts['pre_norm_w'], wts['post_norm_w']
    qw, kvw, aw, sw = wts['q_norm_w'], wts['kv_norm_w'], wts['attn_out_norm_w'], wts['ssm_out_norm_w']
    bglu, dvec = wts['b_glu'], wts['ssm_d']

    pos = jnp.arange(lp, dtype=jnp.int32)
    inv = ROPE_THETA ** (-jnp.arange(HALF_ROPE, dtype=F32) / HALF_ROPE)
    ang = pos.astype(F32)[:, None] * inv[None, :]
    cos, sin = jnp.cos(ang), jnp.sin(ang)
    cos8, sin8 = jnp.tile(cos, (1, HEADS)), jnp.tile(sin, (1, HEADS))
    c32 = jnp.concatenate([cos, cos], axis=1)
    s32 = jnp.concatenate([-sin, sin], axis=1)
    p32 = jnp.asarray(np.roll(np.eye(QK_ROPE, dtype=np.float32), HALF_ROPE, axis=1))
    sum8 = jnp.asarray(np.tile(np.eye(QK_ROPE, dtype=np.float32), (HEADS, 1)))

    ng = 2 * N_GROUPS
    a_re3 = wts['ssm_a_re'].reshape(ng, 1, SSM_STATE)
    a_im3 = wts['ssm_a_im'].reshape(ng, 1, SSM_STATE)
    ldt3 = wts['ssm_log_dt'].reshape(ng, 1, 1)
    bt_re = wts['ssm_b_re'].reshape(2, N_GROUPS, SSM_STATE, SSM_GROUP).transpose(0, 1, 3, 2).reshape(
        ng, SSM_GROUP, SSM_STATE)
    bt_im = wts['ssm_b_im'].reshape(2, N_GROUPS, SSM_STATE, SSM_GROUP).transpose(0, 1, 3, 2).reshape(
        ng, SSM_GROUP, SSM_STATE)
    c_re = wts['ssm_c_re'].reshape(ng, SSM_GROUP, SSM_STATE)
    c_im = wts['ssm_c_im'].reshape(ng, SSM_GROUP, SSM_STATE)
    abar_re, abar_im, bbt_re, bbt_im = _ssm_disc(a_re3, a_im3, ldt3, bt_re, bt_im)

    def direction(t, d):
        return t[d * N_GROUPS:(d + 1) * N_GROUPS]

    def slab(t, d, sign=1.0):
        return jnp.broadcast_to(sign * direction(t, d).reshape(1, N_STATES), (N_SEG, N_STATES))

    wb = [jnp.concatenate([_block_diag(direction(bbt_re, d)), _block_diag(direction(bbt_im, d))],
                          axis=1).astype(BF16) for d in range(2)]
    ct = [jnp.concatenate([_block_diag(direction(c_re, d)), _block_diag(-direction(c_im, d))],
                          axis=1).astype(BF16) for d in range(2)]
    wb_all = jnp.concatenate(wb, axis=1)
    ct_all = jnp.concatenate(ct, axis=1)

    pad = jnp.zeros((lp - n_valid, D_MODEL), F32)
    h = jnp.concatenate([full['meta_tokens'], x[0], pad], axis=0)
    tgt = jnp.concatenate([jnp.zeros((N_META, D_MODEL), F32), loss_target[0], pad], axis=0)

    ql, kvl, ag, su, sg, kr = _inproj(h, pre_w, w_in_b, tr)
    qn_b, qr1_b, qr2_b, kn_b, v_b, kr_b = _qkv_up(ql, kvl, kr, cos8, sin8, c32, s32, qw, kvw, wq_b, wkv_b, p32, tr)

    def heads(a, w):
        return a.reshape(lp, HEADS, w)

    q_h = jnp.concatenate([heads(qn_b, 64), heads(qr1_b, 16), heads(qr2_b, 16)], axis=-1).transpose(1, 0, 2)
    k_h = jnp.concatenate([heads(kn_b, 64), jnp.broadcast_to(kr_b[:, None, :], (lp, HEADS, QK_ROPE))],
                          axis=-1).transpose(1, 0, 2)
    v_h = heads(v_b, 64).transpose(1, 0, 2)
    o_h, lse = _attn_fwd(q_h, k_h, v_h, n_valid, tq, tk)
    o_flat = o_h.transpose(1, 0, 2).reshape(lp, D_ATTN)
    ya = _attn_post(o_flat, ag, aw, tr)

    bu = [_seg_mm(f"ssm_bu{d}", su, wb[d]) for d in range(2)]
    xs = []
    for d in range(2):
        xs += _scan(f"scan{d}", bu[d][0], bu[d][1], slab(abar_re, d), slab(abar_im, d), forward=(d == 0))
    ypre = _seg_nt("ssm_y", xs, ct_all, su, dvec)
    glu, ysn = _ssm_post(ypre, sg, wglu_b, bglu, sw, tr)

    dy, dout, loss, d_post = _out_loss(ya, ysn, h, tgt, wo_b, post_w, n_valid, tr_small)

    do_flat, dag, dysn, d_wo, d_aw = _out_bwd(dy, ya, ysn, o_flat, ag, wo_b, aw, tr_small)
    dypre, dsg, d_wglu, d_bglu, d_sw, d_dvec = _ssm_post_bwd(dysn, glu, sg, ypre, su, wglu_b, sw, dvec, tr_small)

    gs, d_ct, d_wb, d_a8 = [], [], [], []
    for d in range(2):
        dx_re, dx_im = _seg_mm(f"ssm_dx{d}", dypre, ct[d])
        d_ct.append(_seg_tn(f"ssm_dc{d}", dypre, xs[2 * d], xs[2 * d + 1]))
        g_re, g_im, da_re, da_im = _scan(f"scan_adj{d}", dx_re, dx_im, slab(abar_re, d), slab(abar_im, d, -1.0),
                                         forward=(d != 0), states=(xs[2 * d], xs[2 * d + 1]))
        gs += [g_re, g_im]
        d_a8.append((da_re, da_im))
        d_wb.append(_seg_tn(f"ssm_db{d}", su, g_re, g_im))
    du = _seg_nt("ssm_du", gs, wb_all, dypre, dvec)

    do_h = do_flat.reshape(lp, HEADS, V_HEAD).transpose(1, 0, 2)
    dq_h, delta = _attn_bwd_dq(q_h, k_h, v_h, do_h, o_h, lse, n_valid, tq, tk)
    dk_h, dv_h = _attn_bwd_dkv(q_h, k_h, v_h, do_h, lse.reshape(HEADS, 1, lp), delta.reshape(HEADS, 1, lp),
                               n_valid, tq, tk)
    dq_t = dq_h.transpose(1, 0, 2)
    dk_t = dk_h.transpose(1, 0, 2)
    dqn = dq_t[:, :, :64].reshape(lp, 512)
    dr1 = dq_t[:, :, 64:80].reshape(lp, 128)
    dr2 = dq_t[:, :, 80:96].reshape(lp, 128)
    dkn = dk_t[:, :, :64].reshape(lp, 512)
    dkr8 = dk_t[:, :, 64:].reshape(lp, HEADS * QK_ROPE)
    dvf = dv_h.transpose(1, 0, 2).reshape(lp, 512)
    dql, dkvl, dkrr, d_wq, d_wkv, d_qw, d_kvw = _qkv_up_bwd(
        dqn, dr1, dr2, dkn, dvf, dkr8, ql, kvl, cos8, sin8, c32, s32, qw, kvw, wq_b, wkv_b, p32, sum8, tr_small)
    dh, d_win, d_pre = _inproj_bwd(dql, dkvl, dag, du, dsg, dkrr, h, dout, pre_w, w_in_b, tr_small)

    def seg_sums(t):
        return t.reshape(N_SEG, N_GROUPS, SSM_STATE).transpose(1, 0, 2)

    da8_re = jnp.concatenate([seg_sums(d_a8[d][0]) for d in range(2)], axis=0)
    da8_im = jnp.concatenate([seg_sums(d_a8[d][1]) for d in range(2)], axis=0)
    dbb_re = jnp.concatenate([_diag_blocks(d_wb[d][0]) for d in range(2)], axis=0)
    dbb_im = jnp.concatenate([_diag_blocks(d_wb[d][1]) for d in range(2)], axis=0)
    g_are, g_aim, g_ldt, g_bt_re, g_bt_im = _ssm_disc_bwd(a_re3, a_im3, ldt3, bt_re, bt_im, da8_re, da8_im,
                                                          dbb_re, dbb_im)
    g_c_re = jnp.concatenate([_diag_blocks(d_ct[d][0]) for d in range(2)], axis=0)
    g_c_im = jnp.concatenate([-_diag_blocks(d_ct[d][1]) for d in range(2)], axis=0)

    def b_layout(t):
        return t.reshape(2, N_GROUPS, SSM_GROUP, SSM_STATE).transpose(0, 1, 3, 2)

    local = {
        'meta_tokens': dh[:N_META],
        'pre_norm_w': d_pre, 'post_norm_w': d_post,
        'w_in': _cols_in_inv(d_win[:, :D_IN]),
        'q_norm_w': d_qw, 'w_q_up': _cols_q_inv(d_wq),
        'kv_norm_w': d_kvw, 'w_kv_up': _cols_kv_inv(d_wkv),
        'attn_out_norm_w': d_aw,
        'ssm_a_re': g_are, 'ssm_a_im': g_aim, 'ssm_log_dt': g_ldt,
        'ssm_b_re': b_layout(g_bt_re), 'ssm_b_im': b_layout(g_bt_im), 'ssm_c_re': g_c_re, 'ssm_c_im': g_c_im,
        'ssm_d': d_dvec, 'w_glu': d_wglu, 'b_glu': d_bglu, 'ssm_out_norm_w': d_sw, 'w_out': d_wo,
    }

    order = SHARDED + [n for n in WEIGHTS if n not in SHARDED]
    shapes = [wts[n].shape for n in order]
    used_rows = sum(_n_rows(s) for s in shapes)
    tr_adam = 512
    total_rows = -(-used_rows // tr_adam) * tr_adam
    send = []
    for dev in range(N_DEV):
        arrs = [_shard_views(n, local[n])[dev] if n in SHARDED else local[n] for n in order]
        send.append(_pack(arrs, total_rows))
    recv = _exchange("exchange_grads", jnp.stack(send), all_to_all=True)
    packed = [_pack([src[n] for n in order], total_rows) for src in (wts, moms, vels)]
    g_p, d_p, m_p, v_p = _adamw(recv, *packed, tr_adam)
    grads, deltas, new_m, new_v = (dict(zip(order, _unpack(b, shapes))) for b in (g_p, d_p, m_p, v_p))

    loss = lax.psum(loss[0, 0], ("x", "y", "c"))
    grad_x = dh[N_META:n_valid][None]
    return (loss, grad_x, *[grads[n] for n in WEIGHTS], *[deltas[n] for n in WEIGHTS],
            *[new_m[n] for n in WEIGHTS], *[new_v[n] for n in WEIGHTS])


def kernel(x, meta_tokens, pre_norm_w, post_norm_w, w_in, q_norm_w, w_q_up, kv_norm_w, w_kv_up, attn_out_norm_w, ssm_a_re, ssm_a_im, ssm_log_dt, ssm_b_re, ssm_b_im, ssm_c_re, ssm_c_im, ssm_d, w_glu, b_glu, ssm_out_norm_w, w_out, loss_target, m_meta_tokens, m_pre_norm_w, m_post_norm_w, m_w_in, m_q_norm_w, m_w_q_up, m_kv_norm_w, m_w_kv_up, m_attn_out_norm_w, m_ssm_a_re, m_ssm_a_im, m_ssm_log_dt, m_ssm_b_re, m_ssm_b_im, m_ssm_c_re, m_ssm_c_im, m_ssm_d, m_w_glu, m_b_glu, m_ssm_out_norm_w, m_w_out, v_meta_tokens, v_pre_norm_w, v_post_norm_w, v_w_in, v_q_norm_w, v_w_q_up, v_kv_norm_w, v_w_kv_up, v_attn_out_norm_w, v_ssm_a_re, v_ssm_a_im, v_ssm_log_dt, v_ssm_b_re, v_ssm_b_im, v_ssm_c_re, v_ssm_c_im, v_ssm_d, v_w_glu, v_b_glu, v_ssm_out_norm_w, v_w_out):
    wts = dict(zip(WEIGHTS, (meta_tokens, pre_norm_w, post_norm_w, w_in, q_norm_w, w_q_up, kv_norm_w, w_kv_up,
                             attn_out_norm_w, ssm_a_re, ssm_a_im, ssm_log_dt, ssm_b_re, ssm_b_im, ssm_c_re,
                             ssm_c_im, ssm_d, w_glu, b_glu, ssm_out_norm_w, w_out)))
    moms = dict(zip(WEIGHTS, (m_meta_tokens, m_pre_norm_w, m_post_norm_w, m_w_in, m_q_norm_w, m_w_q_up,
                              m_kv_norm_w, m_w_kv_up, m_attn_out_norm_w, m_ssm_a_re, m_ssm_a_im, m_ssm_log_dt,
                              m_ssm_b_re, m_ssm_b_im, m_ssm_c_re, m_ssm_c_im, m_ssm_d, m_w_glu, m_b_glu,
                              m_ssm_out_norm_w, m_w_out)))
    vels = dict(zip(WEIGHTS, (v_meta_tokens, v_pre_norm_w, v_post_norm_w, v_w_in, v_q_norm_w, v_w_q_up,
                              v_kv_norm_w, v_w_kv_up, v_attn_out_norm_w, v_ssm_a_re, v_ssm_a_im, v_ssm_log_dt,
                              v_ssm_b_re, v_ssm_b_im, v_ssm_c_re, v_ssm_c_im, v_ssm_d, v_w_glu, v_b_glu,
                              v_ssm_out_norm_w, v_w_out)))
    return _step(x, loss_target, wts, moms, vels)
```

```python
import functools
import math

import numpy as np
import jax
import jax.numpy as jnp
from jax import lax
from jax.experimental import pallas as pl
from jax.experimental.pallas import tpu as pltpu

F32 = jnp.float32
BF16 = jnp.bfloat16

D_MODEL = 1024
N_META = 16
EPS = 1e-6
HEADS = 8
QK_NOPE = 64
QK_ROPE = 32
HALF_ROPE = QK_ROPE // 2
QK_DIM = QK_NOPE + QK_ROPE
V_HEAD = 64
Q_LORA = 256
KV_LORA = 128
D_ATTN = HEADS * V_HEAD
D_SSM = 512
SSM_GROUP = 16
N_GROUPS = D_SSM // SSM_GROUP
SSM_STATE = 64
N_STATES = N_GROUPS * SSM_STATE
ROPE_THETA = 10000.0
D_IN = Q_LORA + KV_LORA + QK_ROPE + D_ATTN + 2 * D_SSM
D_IN_PAD = 2048
N_DEV = 8
N_SEG = 8
COL_BLK = 512
LANES = 128

ADAM_LR = 0.001
ADAM_B1 = 0.9
ADAM_B2 = 0.999
ADAM_EPS = 1e-08
ADAM_WD = 0.01
ADAM_STEP = 10

VMEM_LIMIT_V7X = 56 * 1024 * 1024
LOG2E = 1.0 / math.log(2.0)
Q_SCALE = LOG2E / math.sqrt(QK_DIM)

WEIGHTS = ['meta_tokens', 'pre_norm_w', 'post_norm_w', 'w_in', 'q_norm_w', 'w_q_up', 'kv_norm_w', 'w_kv_up',
           'attn_out_norm_w', 'ssm_a_re', 'ssm_a_im', 'ssm_log_dt', 'ssm_b_re', 'ssm_b_im', 'ssm_c_re', 'ssm_c_im',
           'ssm_d', 'w_glu', 'b_glu', 'ssm_out_norm_w', 'w_out']
SHARDED = ['w_in', 'w_q_up', 'w_kv_up', 'w_glu', 'w_out', 'meta_tokens']

def _cols_in(w):
    return jnp.concatenate([w[:, 0:384], w[:, 416:D_IN], w[:, 384:416]], axis=1)


def _cols_in_inv(w):
    return jnp.concatenate([w[:, 0:384], w[:, D_IN - QK_ROPE:D_IN], w[:, 384:D_IN - QK_ROPE]], axis=1)


def _cols_q(w):
    t = w.reshape(w.shape[0], HEADS, QK_DIM)
    return jnp.concatenate([t[:, :, 0:64].reshape(-1, 512), t[:, :, 64:80].reshape(-1, 128),
                            t[:, :, 80:96].reshape(-1, 128)], axis=1)


def _cols_q_inv(w):
    r = w.shape[0]
    return jnp.concatenate([w[:, 0:512].reshape(r, HEADS, 64), w[:, 512:640].reshape(r, HEADS, 16),
                            w[:, 640:768].reshape(r, HEADS, 16)], axis=2).reshape(r, HEADS * QK_DIM)


def _cols_kv(w):
    t = w.reshape(w.shape[0], HEADS, 128)
    return jnp.concatenate([t[:, :, 0:64].reshape(-1, 512), t[:, :, 64:128].reshape(-1, 512)], axis=1)


def _cols_kv_inv(w):
    r = w.shape[0]
    return jnp.concatenate([w[:, 0:512].reshape(r, HEADS, 64), w[:, 512:1024].reshape(r, HEADS, 64)],
                           axis=2).reshape(r, HEADS * 128)


def _pick(n, cands):
    for c in cands:
        if n % c == 0:
            return c
    raise ValueError(f"no tile for {n}")


def _cparams(*sem):
    return pltpu.CompilerParams(dimension_semantics=sem, vmem_limit_bytes=VMEM_LIMIT_V7X)


def _mm(a, b):
    return jnp.dot(a.astype(BF16), b.astype(BF16), preferred_element_type=F32)


def _mm_nt(a, b):
    return lax.dot_general(a.astype(BF16), b.astype(BF16), (((1,), (1,)), ((), ())), preferred_element_type=F32)


def _mm_tn(a, b):
    return lax.dot_general(a.astype(BF16), b.astype(BF16), (((0,), (0,)), ((), ())), preferred_element_type=F32)


def _mm_exact(a, b):
    return jnp.dot(a, b, precision=lax.Precision.HIGHEST, preferred_element_type=F32)


def _rms(x):
    return lax.rsqrt(jnp.mean(x * x, axis=-1, keepdims=True) + EPS)


def _rms_bwd(dy, x, r, w):
    xh = x * r
    g = dy * w
    dx = r * (g - xh * jnp.mean(g * xh, axis=-1, keepdims=True))
    dw = jnp.sum(dy * xh, axis=0, keepdims=True)
    return dx, dw


def _sigmoid(z):
    return 1.0 / (1.0 + jnp.exp(-z))


def _silu_and_grad(z):
    s = _sigmoid(z)
    return z * s, s * (1.0 + z * (1.0 - s))


_GELU_C = math.sqrt(2.0 / math.pi)


def _gelu_and_grad(x):
    x2 = x * x
    t = jnp.tanh(_GELU_C * (x + 0.044715 * x * x2))
    val = 0.5 * x * (1.0 + t)
    grad = 0.5 * (1.0 + t) + 0.5 * x * (1.0 - t * t) * _GELU_C * (1.0 + 3.0 * 0.044715 * x2)
    return val, grad


def _acc(ref, val, first):
    @pl.when(first)
    def _():
        ref[...] = val

    @pl.when(jnp.logical_not(first))
    def _():
        ref[...] += val


def _rows_call(name, body, tr, row_ins, full_ins, row_outs, acc_outs):
    lp = row_ins[0].shape[0]
    in_specs = [pl.BlockSpec((tr, a.shape[1]), lambda i: (i, 0)) for a in row_ins]
    in_specs += [pl.BlockSpec(a.shape, lambda i, n=a.ndim: (0,) * n) for a in full_ins]
    out_specs = [pl.BlockSpec((tr, c), lambda i: (i, 0)) for c, _ in row_outs]
    out_specs += [pl.BlockSpec(s, lambda i, n=len(s): (0,) * n) for s, _ in acc_outs]
    out_shape = [jax.ShapeDtypeStruct((lp, c), dt) for c, dt in row_outs]
    out_shape += [jax.ShapeDtypeStruct(s, dt) for s, dt in acc_outs]
    return pl.pallas_call(
        body, name=name, grid=(lp // tr,), in_specs=in_specs, out_specs=out_specs, out_shape=out_shape,
        compiler_params=_cparams("arbitrary"))(*row_ins, *full_ins)


def _inproj(h, pre_w, w_in_b, tr):
    def body(h_ref, pw_ref, w_ref, ql, kvl, ag, su, sg, kr):
        x = h_ref[...]
        xn = x * _rms(x) * pw_ref[...]
        pr = _mm(xn, w_ref[...])
        ql[...] = pr[:, 0:256]
        kvl[...] = pr[:, 256:384]
        ag[...] = pr[:, 384:896]
        su[...] = pr[:, 896:1408]
        sg[...] = pr[:, 1408:1920]
        kr[...] = pr[:, 1920:1952]

    return _rows_call("inproj", body, tr, [h], [pre_w, w_in_b],
                      [(256, F32), (128, F32), (512, F32), (512, F32), (512, F32), (32, F32)], [])


def _qkv_up(ql, kvl, kr, cos8, sin8, c32, s32, qw, kvw, wq_b, wkv_b, p32, tr):
    def body(ql_ref, kvl_ref, kr_ref, cos_ref, sin_ref, c32_ref, s32_ref, qw_ref, kvw_ref, wq_ref, wkv_ref, p_ref,
             qn_o, qr1_o, qr2_o, kn_o, v_o, kr_o):
        x = ql_ref[...]
        q = _mm(x * _rms(x) * qw_ref[...], wq_ref[...]) * Q_SCALE
        r1, r2 = q[:, 512:640], q[:, 640:768]
        cs, sn = cos_ref[...], sin_ref[...]
        qn_o[...] = q[:, 0:512].astype(BF16)
        qr1_o[...] = (r1 * cs - r2 * sn).astype(BF16)
        qr2_o[...] = (r2 * cs + r1 * sn).astype(BF16)
        x = kvl_ref[...]
        kv = _mm(x * _rms(x) * kvw_ref[...], wkv_ref[...])
        kn_o[...] = kv[:, 0:512].astype(BF16)
        v_o[...] = kv[:, 512:1024].astype(BF16)
        x = kr_ref[...]
        kr_o[...] = (x * c32_ref[...] + _mm_exact(x, p_ref[...]) * s32_ref[...]).astype(BF16)

    return _rows_call("qkv_up", body, tr, [ql, kvl, kr, cos8, sin8, c32, s32], [qw, kvw, wq_b, wkv_b, p32],
                      [(512, BF16), (128, BF16), (128, BF16), (512, BF16), (512, BF16), (32, BF16)], [])


def _row_position(row, lseg):
    return (row & (N_SEG - 1)) * lseg + (row >> 3)


def _first_padded_tile(n_valid, lp, tile):
    lseg = lp // N_SEG
    t0 = n_valid - (N_SEG - 1) * lseg
    return (t0 * N_SEG + N_SEG - 1) // tile if n_valid < lp else lp // tile


def _attn_fwd(q, k, vx, n_valid, tq, tk):
    lp = q.shape[1]
    nq, nk = lp // tq, lp // tk
    lseg = lp // N_SEG
    first_padded = _first_padded_tile(n_valid, lp, tk)

    def body(q_ref, k_ref, v_ref, o_ref, lse_ref, m_s, acc_s):
        ki = pl.program_id(2)

        @pl.when(ki == 0)
        def _():
            m_s[...] = jnp.full(m_s.shape, -1e30, F32)
            acc_s[...] = jnp.zeros(acc_s.shape, F32)

        def update(padded):
            s = _mm_nt(q_ref[0], k_ref[0])
            if padded:
                col = ki * tk + lax.broadcasted_iota(jnp.int32, (tq, tk), 1)
                s = jnp.where(_row_position(col, lseg) < n_valid, s, -1e30)
            m_old = m_s[...]
            m_new = jnp.maximum(m_old, jnp.max(s, axis=1, keepdims=True))
            p = jnp.exp2(s - m_new)
            acc_s[...] = jnp.exp2(m_old - m_new) * acc_s[...] + _mm(p, v_ref[0])
            m_s[...] = m_new

        pl.when(ki < first_padded)(functools.partial(update, False))
        pl.when(ki >= first_padded)(functools.partial(update, True))

        @pl.when(ki == nk - 1)
        def _():
            acc = acc_s[...]
            l = acc[:, V_HEAD:V_HEAD + 1]
            o_ref[0] = acc[:, :V_HEAD] / l
            lse_ref[0] = m_s[...] + jnp.log2(l)

    return pl.pallas_call(
        body, name="attn_fwd", grid=(HEADS, nq, nk),
        in_specs=[pl.BlockSpec((1, tq, QK_DIM), lambda h, i, j: (h, i, 0)),
                  pl.BlockSpec((1, tk, QK_DIM), lambda h, i, j: (h, j, 0)),
                  pl.BlockSpec((1, tk, LANES), lambda h, i, j: (h, j, 0))],
        out_specs=[pl.BlockSpec((1, tq, V_HEAD), lambda h, i, j: (h, i, 0)),
                   pl.BlockSpec((1, tq, 1), lambda h, i, j: (h, i, 0))],
        out_shape=[jax.ShapeDtypeStruct((HEADS, lp, V_HEAD), F32), jax.ShapeDtypeStruct((HEADS, lp, 1), F32)],
        scratch_shapes=[pltpu.VMEM((tq, 1), F32), pltpu.VMEM((tq, LANES), F32)],
        compiler_params=_cparams("arbitrary", "arbitrary", "arbitrary"))(q, k, vx)


def _attn_post(o_flat, ag, aw, tr):
    def body(o_ref, g_ref, w_ref, ya):
        t = o_ref[...] * _silu_and_grad(g_ref[...])[0]
        ya[...] = t * _rms(t) * w_ref[...]

    return _rows_call("attn_post", body, tr, [o_flat, ag], [aw], [(512, F32)], [])[0]


def _ssm_in(name, urows, w_b, tr):
    def body(u_ref, w_ref, re_o, im_o):
        r = _mm(u_ref[...], w_ref[...])
        re_o[...] = r[:, :N_STATES]
        im_o[...] = r[:, N_STATES:]

    return _rows_call(name, body, tr, [urows], [w_b], [(N_STATES, F32), (N_STATES, F32)], [])


def _ssm_out(name, xs, wt_b, rows_add, dvec, tr):
    n = len(xs)

    def body(*refs):
        x_refs, a_ref, w_ref, d_ref, y_o = refs[:n], refs[n], refs[n + 1], refs[n + 2], refs[n + 3]
        y = a_ref[...] * d_ref[...]
        for p in range(n):
            y = y + _mm_nt(x_refs[p][...], w_ref[:, p * N_STATES:(p + 1) * N_STATES])
        y_o[...] = y

    return _rows_call(name, body, tr, list(xs) + [rows_add], [wt_b, dvec], [(D_SSM, F32)], [])[0]


def _ssm_wgrad(name, urows, xre, xim, tr):
    def body(u_ref, re_ref, im_ref, dre_o, dim_o):
        first = pl.program_id(0) == 0
        u = u_ref[...]
        _acc(dre_o, _mm_tn(u, re_ref[...]), first)
        _acc(dim_o, _mm_tn(u, im_ref[...]), first)

    return _rows_call(name, body, tr, [urows, xre, xim], [], [],
                      [((D_SSM, N_STATES), F32), ((D_SSM, N_STATES), F32)])


def _scan_tiles(lp):
    lseg = lp // N_SEG
    tt = _pick(lseg, [104, 48, 32, 16, 8, 4, 2, 1])
    return lseg, tt, lseg // tt


def _cmul(ar, ai, br, bi):
    return ar * br - ai * bi, ar * bi + ai * br


def _scan_local(name, bre, bim, ar8, ai8, forward):
    lp = bre.shape[0]
    lseg, tt, nt = _scan_tiles(lp)
    ncb = N_STATES // COL_BLK

    def tile(t):
        return t if forward else nt - 1 - t

    def body(bre_ref, bim_ref, ar_ref, ai_ref, xre_o, xim_o, ere_o, eim_o, cr_s, ci_s):
        t = pl.program_id(1)

        @pl.when(t == 0)
        def _():
            cr_s[...] = jnp.zeros(cr_s.shape, F32)
            ci_s[...] = jnp.zeros(ci_s.shape, F32)

        ar, ai = ar_ref[...], ai_ref[...]

        def step(s, carry):
            cr, ci = carry
            r0 = pl.multiple_of((s if forward else tt - 1 - s) * N_SEG, N_SEG)
            pr, pi = _cmul(ar, ai, cr, ci)
            xr = pr + bre_ref[pl.ds(r0, N_SEG), :]
            xi = pi + bim_ref[pl.ds(r0, N_SEG), :]
            xre_o[pl.ds(r0, N_SEG), :] = xr
            xim_o[pl.ds(r0, N_SEG), :] = xi
            return xr, xi

        cr, ci = lax.fori_loop(0, tt, step, (cr_s[...], ci_s[...]), unroll=4 if tt % 4 == 0 else 1)
        cr_s[...] = cr
        ci_s[...] = ci

        @pl.when(t == nt - 1)
        def _():
            ere_o[...] = cr
            eim_o[...] = ci

    blk = pl.BlockSpec((tt * N_SEG, COL_BLK), lambda cb, t: (tile(t), cb))
    small = pl.BlockSpec((N_SEG, COL_BLK), lambda cb, t: (0, cb))
    return pl.pallas_call(
        body, name=name, grid=(ncb, nt), in_specs=[blk, blk, small, small], out_specs=[blk, blk, small, small],
        out_shape=[jax.ShapeDtypeStruct((lp, N_STATES), F32)] * 2 + [jax.ShapeDtypeStruct((N_SEG, N_STATES), F32)] * 2,
        scratch_shapes=[pltpu.VMEM((N_SEG, COL_BLK), F32)] * 2,
        compiler_params=_cparams("arbitrary", "arbitrary"))(bre, bim, ar8, ai8)


def _scan_fix(name, lre, lim, ar8, ai8, ere, eim, forward, states=None):
    lp = lre.shape[0]
    lseg, tt, nt = _scan_tiles(lp)
    ncb = N_STATES // COL_BLK
    adj = states is not None

    def tile(t):
        return t if forward else nt - 1 - t

    def body(*refs):
        lre_ref, lim_ref, ar_ref, ai_ref, ere_ref, eim_ref = refs[:6]
        k = 6
        if adj:
            sre_ref, sim_ref = refs[k:k + 2]
            k += 2
        xre_o, xim_o = refs[k:k + 2]
        k += 2
        if adj:
            dre_o, dim_o = refs[k:k + 2]
            k += 2
        s_re, s_im, pw_re, pw_im = refs[k:k + 4]
        k += 4
        if adj:
            gp_re, gp_im = refs[k:k + 2]
        t = pl.program_id(1)
        ar, ai = ar_ref[...], ai_ref[...]

        @pl.when(t == 0)
        def _():
            a1r, a1i = ar[0:1, :], ai[0:1, :]
            pr, pi = jnp.ones_like(a1r), jnp.zeros_like(a1i)
            br, bi = a1r, a1i
            n = lseg
            while n:
                if n & 1:
                    pr, pi = _cmul(pr, pi, br, bi)
                n >>= 1
                if n:
                    br, bi = _cmul(br, bi, br, bi)
            cr, ci = jnp.zeros_like(a1r), jnp.zeros_like(a1i)
            order = range(N_SEG) if forward else range(N_SEG - 1, -1, -1)
            for j in order:
                s_re[j:j + 1, :] = cr
                s_im[j:j + 1, :] = ci
                nr, ni = _cmul(pr, pi, cr, ci)
                cr = nr + ere_ref[j:j + 1, :]
                ci = ni + eim_ref[j:j + 1, :]
            pw_re[...] = ar
            pw_im[...] = ai
            if adj:
                gp_re[...] = s_re[...]
                gp_im[...] = s_im[...]
                dre_o[...] = jnp.zeros(dre_o.shape, F32)
                dim_o[...] = jnp.zeros(dim_o.shape, F32)

        sr, si = s_re[...], s_im[...]

        def step(s, carry):
            r0 = pl.multiple_of((s if forward else tt - 1 - s) * N_SEG, N_SEG)
            pr, pi = carry[0], carry[1]
            cr, ci = _cmul(pr, pi, sr, si)
            xr = lre_ref[pl.ds(r0, N_SEG), :] + cr
            xi = lim_ref[pl.ds(r0, N_SEG), :] + ci
            xre_o[pl.ds(r0, N_SEG), :] = xr
            xim_o[pl.ds(r0, N_SEG), :] = xi
            npr, npi = _cmul(pr, pi, ar, ai)
            if not adj:
                return npr, npi
            gr, gi, dr, di = carry[2:]
            fr = sre_ref[pl.ds(r0, N_SEG), :]
            fi = sim_ref[pl.ds(r0, N_SEG), :]
            dr = dr + fr * gr + fi * gi
            di = di + fr * gi - fi * gr
            return npr, npi, xr, xi, dr, di

        init = (pw_re[...], pw_im[...])
        if adj:
            init = init + (gp_re[...], gp_im[...], dre_o[...], dim_o[...])
        out = lax.fori_loop(0, tt, step, init, unroll=4 if tt % 4 == 0 else 1)
        pw_re[...] = out[0]
        pw_im[...] = out[1]
        if adj:
            gp_re[...] = out[2]
            gp_im[...] = out[3]
            dre_o[...] = out[4]
            dim_o[...] = out[5]

    blk = pl.BlockSpec((tt * N_SEG, COL_BLK), lambda cb, t: (tile(t), cb))
    small = pl.BlockSpec((N_SEG, COL_BLK), lambda cb, t: (0, cb))
    ins = [lre, lim, ar8, ai8, ere, eim] + (list(states) if adj else [])
    in_specs = [blk, blk, small, small, small, small] + ([blk, blk] if adj else [])
    out_specs = [blk, blk] + ([small, small] if adj else [])
    out_shape = [jax.ShapeDtypeStruct((lp, N_STATES), F32)] * 2
    if adj:
        out_shape += [jax.ShapeDtypeStruct((N_SEG, N_STATES), F32)] * 2
    scratch = [pltpu.VMEM((N_SEG, COL_BLK), F32)] * (6 if adj else 4)
    return pl.pallas_call(
        body, name=name, grid=(ncb, nt), in_specs=in_specs, out_specs=out_specs, out_shape=out_shape,
        scratch_shapes=scratch, compiler_params=_cparams("arbitrary", "arbitrary"))(*ins)


def _scan(name, bre, bim, ar8, ai8, forward, states=None):
    lre, lim, ere, eim = _scan_local(name + "_local", bre, bim, ar8, ai8, forward)
    return _scan_fix(name + "_fix", lre, lim, ar8, ai8, ere, eim, forward, states)


def _ssm_post(ypre, sg, wglu_b, bglu, sw, tr):
    def body(y_ref, g_ref, w_ref, b_ref, sw_ref, glu_o, ysn_o):
        glu = _mm(_gelu_and_grad(y_ref[...])[0], w_ref[...]) + b_ref[...]
        glu_o[...] = glu
        t = glu[:, :D_SSM] * _sigmoid(glu[:, D_SSM:]) * _silu_and_grad(g_ref[...])[0]
        ysn_o[...] = t * _rms(t) * sw_ref[...]

    return _rows_call("ssm_post", body, tr, [ypre, sg], [wglu_b, bglu, sw], [(1024, F32), (512, F32)], [])


def _out_loss(ya, ysn, h, tgt, wo_b, post_w, n_valid, tr):
    lseg = h.shape[0] // N_SEG

    def body(ya_ref, ys_ref, h_ref, t_ref, w_ref, pw_ref, dy_o, dout_o, loss_o, dpw_o):
        i = pl.program_id(0)
        y = _mm(ya_ref[...], w_ref[0:D_ATTN, :]) + _mm(ys_ref[...], w_ref[D_ATTN:, :])
        r = _rms(y)
        pw = pw_ref[...]
        out = h_ref[...] + y * r * pw
        pos = _row_position(i * tr + lax.broadcasted_iota(jnp.int32, (tr, 1), 0), lseg)
        valid = jnp.logical_and(pos >= N_META, pos < n_valid)
        diff = jnp.where(valid, out - t_ref[...], 0.0)
        dout = diff * (1.0 / D_MODEL)
        dy, dpw = _rms_bwd(dout, y, r, pw)
        dy_o[...] = dy
        dout_o[...] = dout
        _acc(loss_o, 0.5 * jnp.sum(jnp.sum(diff * diff, axis=1, keepdims=True), axis=0, keepdims=True)
             * (1.0 / D_MODEL), i == 0)
        _acc(dpw_o, dpw, i == 0)

    return _rows_call("out_loss", body, tr, [ya, ysn, h, tgt], [wo_b, post_w], [(1024, F32), (1024, F32)],
                      [((1, 1), F32), ((1, D_MODEL), F32)])


def _out_bwd(dy, ya, ysn, o_flat, ag, wo_b, aw, head_sum, tr):
    def body(dy_ref, ya_ref, ys_ref, o_ref, g_ref, w_ref, aw_ref, hs_ref, do_o, dag_o, dysn_o, dl_o, dwo_o, daw_o):
        i = pl.program_id(0)
        dy = dy_ref[...]
        dcat = _mm_nt(dy, w_ref[...])
        cat = jnp.concatenate([ya_ref[...], ys_ref[...]], axis=1)
        _acc(dwo_o, _mm_tn(cat, dy), i == 0)
        dysn_o[...] = dcat[:, D_ATTN:]
        o = o_ref[...]
        sl, dsl = _silu_and_grad(g_ref[...])
        t = o * sl
        dt, daw = _rms_bwd(dcat[:, :D_ATTN], t, _rms(t), aw_ref[...])
        _acc(daw_o, daw, i == 0)
        do = dt * sl
        do_o[...] = do
        dag_o[...] = dt * o * dsl
        dl_o[...] = _mm_exact(do * o, hs_ref[...])

    return _rows_call("out_bwd", body, tr, [dy, ya, ysn, o_flat, ag], [wo_b, aw, head_sum],
                      [(512, F32), (512, F32), (512, F32), (HEADS, F32)],
                      [((D_MODEL, D_MODEL), F32), ((1, D_ATTN), F32)])


def _ssm_post_bwd(dysn, glu, sg, ypre, u, wglu_b, sw, dvec, tr):
    def body(d_ref, glu_ref, sg_ref, y_ref, u_ref, w_ref, sw_ref, dv_ref,
             dyp_o, dsg_o, dwg_o, dbg_o, dsw_o, dd_o):
        i = pl.program_id(0)
        glu = glu_ref[...]
        a, b = glu[:, :D_SSM], glu[:, D_SSM:]
        sb = _sigmoid(b)
        ys = a * sb
        sl, dsl = _silu_and_grad(sg_ref[...])
        t = ys * sl
        dt, dsw = _rms_bwd(d_ref[...], t, _rms(t), sw_ref[...])
        _acc(dsw_o, dsw, i == 0)
        dsg_o[...] = dt * ys * dsl
        dys = dt * sl
        dglu = jnp.concatenate([dys * sb, dys * a * sb * (1.0 - sb)], axis=1)
        _acc(dbg_o, jnp.sum(dglu, axis=0, keepdims=True), i == 0)
        gel, dgel = _gelu_and_grad(y_ref[...])
        _acc(dwg_o, _mm_tn(gel, dglu), i == 0)
        dyp = _mm_nt(dglu, w_ref[...]) * dgel
        dyp_o[...] = dyp
        _acc(dd_o, jnp.sum(dyp * u_ref[...], axis=0, keepdims=True), i == 0)

    return _rows_call("ssm_post_bwd", body, tr, [dysn, glu, sg, ypre, u], [wglu_b, sw, dvec],
                      [(512, F32), (512, F32)],
                      [((D_SSM, 2 * D_SSM), F32), ((1, 2 * D_SSM), F32), ((1, D_SSM), F32), ((1, D_SSM), F32)])


def _attn_bwd(q, k, v, do, lse_t, delta_t, n_valid, tq, tk):
    lp = q.shape[1]
    nq, nk = lp // tq, lp // tk
    lseg = lp // N_SEG
    first_padded = _first_padded_tile(n_valid, lp, tk)

    def body(q_ref, k_ref, v_ref, do_ref, lse_ref, dl_ref, dq_o, dk_o, dv_o, dk_s, dv_s):
        ki = pl.program_id(1)
        qi = pl.program_id(2)

        @pl.when(jnp.logical_and(ki == 0, qi == 0))
        def _():
            dq_o[...] = jnp.zeros(dq_o.shape, F32)

        @pl.when(qi == 0)
        def _():
            dk_s[...] = jnp.zeros(dk_s.shape, F32)
            dv_s[...] = jnp.zeros(dv_s.shape, F32)

        def update(padded):
            qq = q_ref[0]
            kk = k_ref[0]
            dd = do_ref[0]
            pt = jnp.exp2(_mm_nt(kk, qq) - lse_ref[0])
            if padded:
                row = ki * tk + lax.broadcasted_iota(jnp.int32, (tk, tq), 0)
                pt = jnp.where(_row_position(row, lseg) < n_valid, pt, 0.0)
            dv_s[...] += _mm(pt, dd)
            dst = (pt * (_mm_nt(v_ref[0], dd) - dl_ref[0])).astype(BF16)
            dk_s[...] += _mm(dst, qq)
            r0 = pl.multiple_of(qi * tq, tq)
            dq_o[0, pl.ds(r0, tq), :] += _mm_tn(dst, kk)

        pl.when(ki < first_padded)(functools.partial(update, False))
        pl.when(ki >= first_padded)(functools.partial(update, True))

        @pl.when(qi == nq - 1)
        def _():
            dk_o[0] = dk_s[...]
            dv_o[0] = dv_s[...]

    qspec = lambda w: pl.BlockSpec((1, tq, w), lambda h, j, i: (h, i, 0))
    kspec = lambda w: pl.BlockSpec((1, tk, w), lambda h, j, i: (h, j, 0))
    tspec = pl.BlockSpec((1, 1, tq), lambda h, j, i: (h, 0, i))
    return pl.pallas_call(
        body, name="attn_bwd", grid=(HEADS, nk, nq),
        in_specs=[qspec(QK_DIM), kspec(QK_DIM), kspec(V_HEAD), qspec(V_HEAD), tspec, tspec],
        out_specs=[pl.BlockSpec((1, lp, QK_DIM), lambda h, j, i: (h, 0, 0)), kspec(QK_DIM), kspec(V_HEAD)],
        out_shape=[jax.ShapeDtypeStruct((HEADS, lp, QK_DIM), F32), jax.ShapeDtypeStruct((HEADS, lp, QK_DIM), F32),
                   jax.ShapeDtypeStruct((HEADS, lp, V_HEAD), F32)],
        scratch_shapes=[pltpu.VMEM((tk, QK_DIM), F32), pltpu.VMEM((tk, V_HEAD), F32)],
        compiler_params=_cparams("arbitrary", "arbitrary", "arbitrary"))(q, k, v, do, lse_t, delta_t)


def _qkv_up_bwd(dqn, dr1, dr2, dkn, dv, dkr8, ql, kvl, cos8, sin8, c32, s32, qw, kvw, wq_b, wkv_b, p32, sum8, tr):
    def body(dqn_ref, dr1_ref, dr2_ref, dkn_ref, dv_ref, dkr_ref, ql_ref, kvl_ref, cos_ref, sin_ref, c32_ref,
             s32_ref, qw_ref, kvw_ref, wq_ref, wkv_ref, p_ref, s8_ref,
             dql_o, dkvl_o, dkrr_o, dwq_o, dwkv_o, dqw_o, dkvw_o):
        i = pl.program_id(0)
        cs, sn = cos_ref[...], sin_ref[...]
        d1, d2 = dr1_ref[...], dr2_ref[...]
        dq = jnp.concatenate([dqn_ref[...], d1 * cs + d2 * sn, d2 * cs - d1 * sn], axis=1) * (Q_SCALE / LOG2E)
        x = ql_ref[...]
        r = _rms(x)
        qw = qw_ref[...]
        _acc(dwq_o, _mm_tn(x * r * qw, dq), i == 0)
        dx, dw = _rms_bwd(_mm_nt(dq, wq_ref[...]), x, r, qw)
        dql_o[...] = dx
        _acc(dqw_o, dw, i == 0)
        dkv = jnp.concatenate([dkn_ref[...] * (1.0 / LOG2E), dv_ref[...]], axis=1)
        x = kvl_ref[...]
        r = _rms(x)
        kvw = kvw_ref[...]
        _acc(dwkv_o, _mm_tn(x * r * kvw, dkv), i == 0)
        dx, dw = _rms_bwd(_mm_nt(dkv, wkv_ref[...]), x, r, kvw)
        dkvl_o[...] = dx
        _acc(dkvw_o, dw, i == 0)
        dkr = _mm_exact(dkr_ref[...], s8_ref[...]) * (1.0 / LOG2E)
        dkrr_o[...] = dkr * c32_ref[...] + _mm_exact(dkr * s32_ref[...], p_ref[...])

    return _rows_call("qkv_up_bwd", body, tr, [dqn, dr1, dr2, dkn, dv, dkr8, ql, kvl, cos8, sin8, c32, s32],
                      [qw, kvw, wq_b, wkv_b, p32, sum8], [(256, F32), (128, F32), (32, F32)],
                      [((Q_LORA, 768), F32), ((KV_LORA, 1024), F32), ((1, Q_LORA), F32), ((1, KV_LORA), F32)])


def _inproj_bwd(dql, dkvl, dag, du, dsg, dkr, h, dout, pre_w, w_in_b, tr):
    def body(dql_ref, dkvl_ref, dag_ref, du_ref, dsg_ref, dkr_ref, h_ref, dout_ref, pw_ref, w_ref,
             dh_o, dwin_o, dpw_o):
        i = pl.program_id(0)
        dproj = jnp.concatenate([dql_ref[...], dkvl_ref[...], dag_ref[...], du_ref[...], dsg_ref[...],
                                 dkr_ref[...], jnp.zeros((tr, D_IN_PAD - D_IN), F32)], axis=1)
        x = h_ref[...]
        r = _rms(x)
        pw = pw_ref[...]
        _acc(dwin_o, _mm_tn(x * r * pw, dproj), i == 0)
        dx, dw = _rms_bwd(_mm_nt(dproj, w_ref[...]), x, r, pw)
        _acc(dpw_o, dw, i == 0)
        dh_o[...] = dout_ref[...] + dx

    return _rows_call("inproj_bwd", body, tr, [dql, dkvl, dag, du, dsg, dkr, h, dout], [pre_w, w_in_b],
                      [(1024, F32)], [((D_MODEL, D_IN_PAD), F32), ((1, D_MODEL), F32)])


def _disc_terms(a_re, a_im, ldt):
    dt = jnp.exp(ldt)
    mag = jnp.exp(a_re * dt)
    th = a_im * dt
    cs, sn = jnp.cos(th), jnp.sin(th)
    abar_re, abar_im = mag * cs, mag * sn
    num_re, num_im = abar_re - 1.0, abar_im
    den = a_re * a_re + a_im * a_im
    coef_re = (num_re * a_re + num_im * a_im) / den
    coef_im = (num_im * a_re - num_re * a_im) / den
    return dt, mag, cs, sn, abar_re, abar_im, num_re, num_im, den, coef_re, coef_im


def _ssm_disc(a_re, a_im, ldt, bt_re, bt_im):
    def body(ar_ref, ai_ref, l_ref, br_ref, bi_ref, abr_o, abi_o, bbr_o, bbi_o):
        t = _disc_terms(ar_ref[...], ai_ref[...], l_ref[...])
        abr_o[...] = t[4]
        abi_o[...] = t[5]
        cr, ci = t[9], t[10]
        br, bi = br_ref[...], bi_ref[...]
        bbr_o[...] = cr * br - ci * bi
        bbi_o[...] = cr * bi + ci * br

    ng = a_re.shape[0]
    return pl.pallas_call(
        body, name="ssm_disc",
        out_shape=[jax.ShapeDtypeStruct((ng, 1, SSM_STATE), F32)] * 2
        + [jax.ShapeDtypeStruct((ng, SSM_GROUP, SSM_STATE), F32)] * 2)(a_re, a_im, ldt, bt_re, bt_im)


def _ssm_disc_bwd(a_re, a_im, ldt, bt_re, bt_im, da8_re, da8_im, dbb_re, dbb_im):
    def body(ar_ref, ai_ref, l_ref, br_ref, bi_ref, dar_ref, dai_ref, dbr_ref, dbi_ref,
             gar_o, gai_o, gl_o, gbr_o, gbi_o):
        a_re, a_im = ar_ref[...], ai_ref[...]
        dt, mag, cs, sn, abar_re, abar_im, num_re, num_im, den, cr, ci = _disc_terms(a_re, a_im, l_ref[...])
        br, bi = br_ref[...], bi_ref[...]
        dbr, dbi = dbr_ref[...], dbi_ref[...]
        gbr_o[...] = cr * dbr + ci * dbi
        gbi_o[...] = cr * dbi - ci * dbr
        dcr = jnp.sum(br * dbr + bi * dbi, axis=1, keepdims=True)
        dci = jnp.sum(br * dbi - bi * dbr, axis=1, keepdims=True)
        dnum_re = (dcr * a_re - dci * a_im) / den
        dnum_im = (dcr * a_im + dci * a_re) / den
        dden = -(dcr * cr + dci * ci) / den
        g_are = (dcr * num_re + dci * num_im) / den + dden * 2.0 * a_re
        g_aim = (dcr * num_im - dci * num_re) / den + dden * 2.0 * a_im
        d_abr = jnp.sum(dar_ref[...], axis=1, keepdims=True) + dnum_re
        d_abi = jnp.sum(dai_ref[...], axis=1, keepdims=True) + dnum_im
        dmag = d_abr * cs + d_abi * sn
        dth = d_abi * abar_re - d_abr * abar_im
        g_are = g_are + dmag * mag * dt
        g_aim = g_aim + dth * dt
        ddt = jnp.sum(dmag * mag * a_re + dth * a_im, axis=2, keepdims=True)
        gar_o[...] = g_are
        gai_o[...] = g_aim
        gl_o[...] = ddt * dt

    ng = a_re.shape[0]
    return pl.pallas_call(
        body, name="ssm_disc_bwd",
        out_shape=[jax.ShapeDtypeStruct((ng, 1, SSM_STATE), F32)] * 2 + [jax.ShapeDtypeStruct((ng, 1, 1), F32)]
        + [jax.ShapeDtypeStruct((ng, SSM_GROUP, SSM_STATE), F32)] * 2)(
            a_re, a_im, ldt, bt_re, bt_im, da8_re, da8_im, dbb_re, dbb_im)


def _exchange(name, buf, all_to_all):
    rows = buf.shape[-2]

    def body(in_ref, out_ref, send_sems, recv_sems, local_sem):
        x, y, c = lax.axis_index("x"), lax.axis_index("y"), lax.axis_index("c")
        me = 4 * x + 2 * y + c
        copies = []
        for k in range(1, N_DEV):
            px = 1 - x if (k >> 2) & 1 else x
            py = 1 - y if (k >> 1) & 1 else y
            pc = 1 - c if k & 1 else c
            src = in_ref.at[4 * px + 2 * py + pc] if all_to_all else in_ref
            copies.append(pltpu.make_async_remote_copy(
                src_ref=src, dst_ref=out_ref.at[me], send_sem=send_sems.at[k - 1], recv_sem=recv_sems.at[k - 1],
                device_id=(px, py, pc), device_id_type=pl.DeviceIdType.MESH))
        mine = pltpu.make_async_copy(in_ref.at[me] if all_to_all else in_ref, out_ref.at[me], local_sem)
        mine.start()
        for cp in copies:
            cp.start()
        for cp in copies:
            cp.wait()
        mine.wait()

    return pl.pallas_call(
        body, name=name, out_shape=jax.ShapeDtypeStruct((N_DEV, rows, LANES), F32),
        in_specs=[pl.BlockSpec(memory_space=pl.ANY)], out_specs=pl.BlockSpec(memory_space=pl.ANY),
        scratch_shapes=[pltpu.SemaphoreType.DMA((N_DEV - 1,)), pltpu.SemaphoreType.DMA((N_DEV - 1,)),
                        pltpu.SemaphoreType.DMA(())])(buf)


def _adamw(recv, w, m, v, tr):
    rows = w.shape[0]
    c1 = 1.0 - ADAM_B1 ** ADAM_STEP
    c2 = 1.0 - ADAM_B2 ** ADAM_STEP

    def body(r_ref, w_ref, m_ref, v_ref, g_o, d_o, m_o, v_o):
        g = r_ref[0]
        for k in range(1, N_DEV):
            g = g + r_ref[k]
        mm = ADAM_B1 * m_ref[...] + (1.0 - ADAM_B1) * g
        vv = ADAM_B2 * v_ref[...] + (1.0 - ADAM_B2) * (g * g)
        g_o[...] = g
        m_o[...] = mm
        v_o[...] = vv
        d_o[...] = -ADAM_LR * ((mm / c1) / (jnp.sqrt(vv / c2) + ADAM_EPS) + ADAM_WD * w_ref[...])

    spec = pl.BlockSpec((tr, LANES), lambda i: (i, 0))
    return pl.pallas_call(
        body, name="adamw", grid=(rows // tr,),
        in_specs=[pl.BlockSpec((N_DEV, tr, LANES), lambda i: (0, i, 0)), spec, spec, spec],
        out_specs=[spec] * 4, out_shape=[jax.ShapeDtypeStruct((rows, LANES), F32)] * 4,
        compiler_params=_cparams("arbitrary"))(recv, w, m, v)


def _to_rows(a):
    flat = a.reshape(-1)
    pad = (-flat.shape[0]) % LANES
    if pad:
        flat = jnp.concatenate([flat, jnp.zeros((pad,), flat.dtype)])
    return flat.reshape(-1, LANES)


def _n_rows(shape):
    return -(-int(np.prod(shape)) // LANES)


def _pack(arrays, total_rows):
    rows = [_to_rows(a) for a in arrays]
    used = sum(r.shape[0] for r in rows)
    if total_rows > used:
        rows.append(jnp.zeros((total_rows - used, LANES), F32))
    return jnp.concatenate(rows, axis=0)


def _unpack(buf, shapes):
    out, r0 = [], 0
    for s in shapes:
        n = int(np.prod(s))
        nr = _n_rows(s)
        out.append(buf[r0:r0 + nr].reshape(-1)[:n].reshape(s))
        r0 += nr
    return out


def _shard_views(name, full):
    if name == 'w_out':
        return full.reshape(N_DEV, full.shape[0] // N_DEV, full.shape[1])
    r, ccols = full.shape
    return full.reshape(r, N_DEV, ccols // N_DEV).transpose(1, 0, 2)


def _from_shards(name, stacked):
    if name == 'w_out':
        return stacked.reshape(-1, stacked.shape[-1])
    n, r, cc = stacked.shape
    return stacked.transpose(1, 0, 2).reshape(r, n * cc)


def _block_diag(t):
    eye = jnp.eye(N_GROUPS, dtype=t.dtype)
    return (t[:, :, None, :] * eye[:, None, :, None]).reshape(D_SSM, N_STATES)


def _diag_blocks(mat):
    return jnp.stack([mat[g * SSM_GROUP:(g + 1) * SSM_GROUP, g * SSM_STATE:(g + 1) * SSM_STATE]
                      for g in range(N_GROUPS)])


def _step(x, loss_target, wts, moms, vels):
    seq = x.shape[1]
    n_valid = N_META + seq
    lp = -(-n_valid // 256) * 256
    tr = _pick(lp, [640, 256])
    tr_small = 256
    tq = _pick(lp, [640, 256])
    tk = tq

    shard_shapes = [wts[n].shape[-2:] for n in SHARDED]
    n_shard_rows = sum(_n_rows(s) for s in shard_shapes)
    gathered = _exchange("gather_weights", _pack([wts[n].reshape(wts[n].shape[-2:]) for n in SHARDED],
                                                 n_shard_rows), all_to_all=False)
    parts = [[] for _ in SHARDED]
    for dev in range(N_DEV):
        for i, a in enumerate(_unpack(gathered[dev], shard_shapes)):
            parts[i].append(a)
    full = {n: _from_shards(n, jnp.stack(parts[i])) for i, n in enumerate(SHARDED)}

    w_in_b = jnp.concatenate([_cols_in(full['w_in']), jnp.zeros((D_MODEL, D_IN_PAD - D_IN), F32)],
                             axis=1).astype(BF16)
    wq_b = _cols_q(full['w_q_up']).astype(BF16)
    wkv_b = _cols_kv(full['w_kv_up']).astype(BF16)
    wglu_b = full['w_glu'].astype(BF16)
    wo_b = full['w_out'].astype(BF16)
    pre_w, post_w = wts['pre_norm_w'], wts['post_norm_w']
    qw, kvw, aw, sw = wts['q_norm_w'], wts['kv_norm_w'], wts['attn_out_norm_w'], wts['ssm_out_norm_w']
    bglu, dvec = wts['b_glu'], wts['ssm_d']

    lseg = lp // N_SEG
    pos = _row_position(jnp.arange(lp, dtype=jnp.int32), lseg)
    inv = ROPE_THETA ** (-jnp.arange(HALF_ROPE, dtype=F32) / HALF_ROPE)
    ang = pos.astype(F32)[:, None] * inv[None, :]
    cos, sin = jnp.cos(ang), jnp.sin(ang)
    cos8, sin8 = jnp.tile(cos, (1, HEADS)), jnp.tile(sin, (1, HEADS))
    c32 = jnp.concatenate([cos, cos], axis=1)
    s32 = jnp.concatenate([-sin, sin], axis=1)
    p32 = jnp.asarray(np.roll(np.eye(QK_ROPE, dtype=np.float32), HALF_ROPE, axis=1))
    sum8 = jnp.asarray(np.tile(np.eye(QK_ROPE, dtype=np.float32), (HEADS, 1)))
    head_sum = jnp.asarray(np.repeat(np.eye(HEADS, dtype=np.float32), V_HEAD, axis=0))

    ng = 2 * N_GROUPS
    a_re3 = wts['ssm_a_re'].reshape(ng, 1, SSM_STATE)
    a_im3 = wts['ssm_a_im'].reshape(ng, 1, SSM_STATE)
    ldt3 = wts['ssm_log_dt'].reshape(ng, 1, 1)
    bt_re = wts['ssm_b_re'].reshape(2, N_GROUPS, SSM_STATE, SSM_GROUP).transpose(0, 1, 3, 2).reshape(
        ng, SSM_GROUP, SSM_STATE)
    bt_im = wts['ssm_b_im'].reshape(2, N_GROUPS, SSM_STATE, SSM_GROUP).transpose(0, 1, 3, 2).reshape(
        ng, SSM_GROUP, SSM_STATE)
    c_re = wts['ssm_c_re'].reshape(ng, SSM_GROUP, SSM_STATE)
    c_im = wts['ssm_c_im'].reshape(ng, SSM_GROUP, SSM_STATE)
    abar_re, abar_im, bbt_re, bbt_im = _ssm_disc(a_re3, a_im3, ldt3, bt_re, bt_im)

    def direction(t, d):
        return t[d * N_GROUPS:(d + 1) * N_GROUPS]

    def slab(t, d, sign=1.0):
        return jnp.broadcast_to(sign * direction(t, d).reshape(1, N_STATES), (N_SEG, N_STATES))

    wb = [jnp.concatenate([_block_diag(direction(bbt_re, d)), _block_diag(direction(bbt_im, d))],
                          axis=1).astype(BF16) for d in range(2)]
    ct = [jnp.concatenate([_block_diag(direction(c_re, d)), _block_diag(-direction(c_im, d))],
                          axis=1).astype(BF16) for d in range(2)]
    wb_all = jnp.concatenate(wb, axis=1)
    ct_all = jnp.concatenate(ct, axis=1)

    def to_rows(a):
        return a.reshape(N_SEG, lseg, a.shape[-1]).transpose(1, 0, 2).reshape(lp, a.shape[-1])

    def to_tokens(a):
        return a.reshape(lseg, N_SEG, a.shape[-1]).transpose(1, 0, 2).reshape(lp, a.shape[-1])

    pad = jnp.zeros((lp - n_valid, D_MODEL), F32)
    h = to_rows(jnp.concatenate([full['meta_tokens'], x[0], pad], axis=0))
    tgt = to_rows(jnp.concatenate([jnp.zeros((N_META, D_MODEL), F32), loss_target[0], pad], axis=0))

    ql, kvl, ag, su, sg, kr = _inproj(h, pre_w, w_in_b, tr)
    qn_b, qr1_b, qr2_b, kn_b, v_b, kr_b = _qkv_up(ql, kvl, kr, cos8, sin8, c32, s32, qw, kvw, wq_b, wkv_b, p32, tr)

    def heads(a, w):
        return a.reshape(lp, HEADS, w)

    q_h = jnp.concatenate([heads(qn_b, 64), heads(qr1_b, 16), heads(qr2_b, 16)], axis=-1).transpose(1, 0, 2)
    k_h = jnp.concatenate([heads(kn_b, 64), jnp.broadcast_to(kr_b[:, None, :], (lp, HEADS, QK_ROPE))],
                          axis=-1).transpose(1, 0, 2)
    v_h = heads(v_b, 64).transpose(1, 0, 2)
    vx_h = jnp.concatenate([v_h, jnp.ones((HEADS, lp, 1), BF16), jnp.zeros((HEADS, lp, LANES - V_HEAD - 1), BF16)],
                           axis=-1)
    o_h, lse = _attn_fwd(q_h, k_h, vx_h, n_valid, tq, tk)
    o_flat = o_h.transpose(1, 0, 2).reshape(lp, D_ATTN)
    ya = _attn_post(o_flat, ag, aw, tr)

    bu = [_ssm_in(f"ssm_bu{d}", su, wb[d], tr_small) for d in range(2)]
    xs = []
    for d in range(2):
        xs += _scan(f"scan{d}", bu[d][0], bu[d][1], slab(abar_re, d), slab(abar_im, d), forward=(d == 0))
    ypre = _ssm_out("ssm_y", xs, ct_all, su, dvec, tr_small)
    glu, ysn = _ssm_post(ypre, sg, wglu_b, bglu, sw, tr)

    dy, dout, loss, d_post = _out_loss(ya, ysn, h, tgt, wo_b, post_w, n_valid, tr_small)

    do_flat, dag, dysn, delta8, d_wo, d_aw = _out_bwd(dy, ya, ysn, o_flat, ag, wo_b, aw, head_sum, tr_small)
    dypre, dsg, d_wglu, d_bglu, d_sw, d_dvec = _ssm_post_bwd(dysn, glu, sg, ypre, su, wglu_b, sw, dvec, tr_small)

    gs, d_ct, d_wb, d_a8 = [], [], [], []
    for d in range(2):
        dx_re, dx_im = _ssm_in(f"ssm_dx{d}", dypre, ct[d], tr_small)
        d_ct.append(_ssm_wgrad(f"ssm_dc{d}", dypre, xs[2 * d], xs[2 * d + 1], tr_small))
        g_re, g_im, da_re, da_im = _scan(f"scan_adj{d}", dx_re, dx_im, slab(abar_re, d), slab(abar_im, d, -1.0),
                                         forward=(d != 0), states=(xs[2 * d], xs[2 * d + 1]))
        gs += [g_re, g_im]
        d_a8.append((da_re, da_im))
        d_wb.append(_ssm_wgrad(f"ssm_db{d}", su, g_re, g_im, tr_small))
    du = _ssm_out("ssm_du", gs, wb_all, dypre, dvec, tr_small)

    do_h = do_flat.astype(BF16).reshape(lp, HEADS, V_HEAD).transpose(1, 0, 2)
    dq_h, dk_h, dv_h = _attn_bwd(q_h, k_h, v_h, do_h, lse.reshape(HEADS, 1, lp),
                                 delta8.T.reshape(HEADS, 1, lp), n_valid, tq, tk)
    dq_t = dq_h.transpose(1, 0, 2)
    dk_t = dk_h.transpose(1, 0, 2)
    dqn = dq_t[:, :, :64].reshape(lp, 512)
    dr1 = dq_t[:, :, 64:80].reshape(lp, 128)
    dr2 = dq_t[:, :, 80:96].reshape(lp, 128)
    dkn = dk_t[:, :, :64].reshape(lp, 512)
    dkr8 = dk_t[:, :, 64:].reshape(lp, HEADS * QK_ROPE)
    dvf = dv_h.transpose(1, 0, 2).reshape(lp, 512)
    dql, dkvl, dkrr, d_wq, d_wkv, d_qw, d_kvw = _qkv_up_bwd(
        dqn, dr1, dr2, dkn, dvf, dkr8, ql, kvl, cos8, sin8, c32, s32, qw, kvw, wq_b, wkv_b, p32, sum8, tr_small)
    dh, d_win, d_pre = _inproj_bwd(dql, dkvl, dag, du, dsg, dkrr, h, dout, pre_w, w_in_b, tr_small)
    dh = to_tokens(dh)

    def seg_sums(t):
        return t.reshape(N_SEG, N_GROUPS, SSM_STATE).transpose(1, 0, 2)

    da8_re = jnp.concatenate([seg_sums(d_a8[d][0]) for d in range(2)], axis=0)
    da8_im = jnp.concatenate([seg_sums(d_a8[d][1]) for d in range(2)], axis=0)
    dbb_re = jnp.concatenate([_diag_blocks(d_wb[d][0]) for d in range(2)], axis=0)
    dbb_im = jnp.concatenate([_diag_blocks(d_wb[d][1]) for d in range(2)], axis=0)
    g_are, g_aim, g_ldt, g_bt_re, g_bt_im = _ssm_disc_bwd(a_re3, a_im3, ldt3, bt_re, bt_im, da8_re, da8_im,
                                                          dbb_re, dbb_im)
    g_c_re = jnp.concatenate([_diag_blocks(d_ct[d][0]) for d in range(2)], axis=0)
    g_c_im = jnp.concatenate([-_diag_blocks(d_ct[d][1]) for d in range(2)], axis=0)

    def b_layout(t):
        return t.reshape(2, N_GROUPS, SSM_GROUP, SSM_STATE).transpose(0, 1, 3, 2)

    local = {
        'meta_tokens': dh[:N_META],
        'pre_norm_w': d_pre, 'post_norm_w': d_post,
        'w_in': _cols_in_inv(d_win[:, :D_IN]),
        'q_norm_w': d_qw, 'w_q_up': _cols_q_inv(d_wq),
        'kv_norm_w': d_kvw, 'w_kv_up': _cols_kv_inv(d_wkv),
        'attn_out_norm_w': d_aw,
        'ssm_a_re': g_are, 'ssm_a_im': g_aim, 'ssm_log_dt': g_ldt,
        'ssm_b_re': b_layout(g_bt_re), 'ssm_b_im': b_layout(g_bt_im), 'ssm_c_re': g_c_re, 'ssm_c_im': g_c_im,
        'ssm_d': d_dvec, 'w_glu': d_wglu, 'b_glu': d_bglu, 'ssm_out_norm_w': d_sw, 'w_out': d_wo,
    }

    order = SHARDED + [n for n in WEIGHTS if n not in SHARDED]
    shapes = [wts[n].shape for n in order]
    used_rows = sum(_n_rows(s) for s in shapes)
    tr_adam = 512
    total_rows = -(-used_rows // tr_adam) * tr_adam
    send = []
    for dev in range(N_DEV):
        arrs = [_shard_views(n, local[n])[dev] if n in SHARDED else local[n] for n in order]
        send.append(_pack(arrs, total_rows))
    recv = _exchange("exchange_grads", jnp.stack(send), all_to_all=True)
    packed = [_pack([src[n] for n in order], total_rows) for src in (wts, moms, vels)]
    g_p, d_p, m_p, v_p = _adamw(recv, *packed, tr_adam)
    grads, deltas, new_m, new_v = (dict(zip(order, _unpack(b, shapes))) for b in (g_p, d_p, m_p, v_p))

    loss = lax.psum(loss[0, 0], ("x", "y", "c"))
    grad_x = dh[N_META:n_valid][None]
    return (loss, grad_x, *[grads[n] for n in WEIGHTS], *[deltas[n] for n in WEIGHTS],
            *[new_m[n] for n in WEIGHTS], *[new_v[n] for n in WEIGHTS])


def kernel(x, meta_tokens, pre_norm_w, post_norm_w, w_in, q_norm_w, w_q_up, kv_norm_w, w_kv_up, attn_out_norm_w, ssm_a_re, ssm_a_im, ssm_log_dt, ssm_b_re, ssm_b_im, ssm_c_re, ssm_c_im, ssm_d, w_glu, b_glu, ssm_out_norm_w, w_out, loss_target, m_meta_tokens, m_pre_norm_w, m_post_norm_w, m_w_in, m_q_norm_w, m_w_q_up, m_kv_norm_w, m_w_kv_up, m_attn_out_norm_w, m_ssm_a_re, m_ssm_a_im, m_ssm_log_dt, m_ssm_b_re, m_ssm_b_im, m_ssm_c_re, m_ssm_c_im, m_ssm_d, m_w_glu, m_b_glu, m_ssm_out_norm_w, m_w_out, v_meta_tokens, v_pre_norm_w, v_post_norm_w, v_w_in, v_q_norm_w, v_w_q_up, v_kv_norm_w, v_w_kv_up, v_attn_out_norm_w, v_ssm_a_re, v_ssm_a_im, v_ssm_log_dt, v_ssm_b_re, v_ssm_b_im, v_ssm_c_re, v_ssm_c_im, v_ssm_d, v_w_glu, v_b_glu, v_ssm_out_norm_w, v_w_out):
    wts = dict(zip(WEIGHTS, (meta_tokens, pre_norm_w, post_norm_w, w_in, q_norm_w, w_q_up, kv_norm_w, w_kv_up,
                             attn_out_norm_w, ssm_a_re, ssm_a_im, ssm_log_dt, ssm_b_re, ssm_b_im, ssm_c_re,
                             ssm_c_im, ssm_d, w_glu, b_glu, ssm_out_norm_w, w_out)))
    moms = dict(zip(WEIGHTS, (m_meta_tokens, m_pre_norm_w, m_post_norm_w, m_w_in, m_q_norm_w, m_w_q_up,
                              m_kv_norm_w, m_w_kv_up, m_attn_out_norm_w, m_ssm_a_re, m_ssm_a_im, m_ssm_log_dt,
                              m_ssm_b_re, m_ssm_b_im, m_ssm_c_re, m_ssm_c_im, m_ssm_d, m_w_glu, m_b_glu,
                              m_ssm_out_norm_w, m_w_out)))
    vels = dict(zip(WEIGHTS, (v_meta_tokens, v_pre_norm_w, v_post_norm_w, v_w_in, v_q_norm_w, v_w_q_up,
                              v_kv_norm_w, v_w_kv_up, v_attn_out_norm_w, v_ssm_a_re, v_ssm_a_im, v_ssm_log_dt,
                              v_ssm_b_re, v_ssm_b_im, v_ssm_c_re, v_ssm_c_im, v_ssm_d, v_w_glu, v_b_glu,
                              v_ssm_out_norm_w, v_w_out)))
    return _step(x, loss_target, wts, moms, vels)
```

```python
import functools
import math

import numpy as np
import jax
import jax.numpy as jnp
from jax import lax
from jax.experimental import pallas as pl
from jax.experimental.pallas import tpu as pltpu

F32 = jnp.float32
BF16 = jnp.bfloat16

D_MODEL = 1024
N_META = 16
EPS = 1e-6
HEADS = 8
QK_NOPE = 64
QK_ROPE = 32
HALF_ROPE = QK_ROPE // 2
QK_DIM = QK_NOPE + QK_ROPE
V_HEAD = 64
Q_LORA = 256
KV_LORA = 128
D_ATTN = HEADS * V_HEAD
D_SSM = 512
SSM_GROUP = 16
N_GROUPS = D_SSM // SSM_GROUP
SSM_STATE = 64
N_STATES = N_GROUPS * SSM_STATE
ROPE_THETA = 10000.0
D_IN = Q_LORA + KV_LORA + QK_ROPE + D_ATTN + 2 * D_SSM
D_IN_PAD = 2048
N_DEV = 8
N_SEG = 8
COL_BLK = 512
LANES = 128

ADAM_LR = 0.001
ADAM_B1 = 0.9
ADAM_B2 = 0.999
ADAM_EPS = 1e-08
ADAM_WD = 0.01
ADAM_STEP = 10

VMEM_LIMIT_V7X = 56 * 1024 * 1024
LOG2E = 1.0 / math.log(2.0)
Q_SCALE = LOG2E / math.sqrt(QK_DIM)

WEIGHTS = ['meta_tokens', 'pre_norm_w', 'post_norm_w', 'w_in', 'q_norm_w', 'w_q_up', 'kv_norm_w', 'w_kv_up',
           'attn_out_norm_w', 'ssm_a_re', 'ssm_a_im', 'ssm_log_dt', 'ssm_b_re', 'ssm_b_im', 'ssm_c_re', 'ssm_c_im',
           'ssm_d', 'w_glu', 'b_glu', 'ssm_out_norm_w', 'w_out']
SHARDED = ['w_in', 'w_q_up', 'w_kv_up', 'w_glu', 'w_out', 'meta_tokens']

def _cols_in(w):
    return jnp.concatenate([w[:, 0:384], w[:, 416:D_IN], w[:, 384:416]], axis=1)


def _cols_in_inv(w):
    return jnp.concatenate([w[:, 0:384], w[:, D_IN - QK_ROPE:D_IN], w[:, 384:D_IN - QK_ROPE]], axis=1)


def _cols_q(w):
    t = w.reshape(w.shape[0], HEADS, QK_DIM)
    return jnp.concatenate([t[:, :, 0:64].reshape(-1, 512), t[:, :, 64:80].reshape(-1, 128),
                            t[:, :, 80:96].reshape(-1, 128)], axis=1)


def _cols_q_inv(w):
    r = w.shape[0]
    return jnp.concatenate([w[:, 0:512].reshape(r, HEADS, 64), w[:, 512:640].reshape(r, HEADS, 16),
                            w[:, 640:768].reshape(r, HEADS, 16)], axis=2).reshape(r, HEADS * QK_DIM)


def _cols_kv(w):
    t = w.reshape(w.shape[0], HEADS, 128)
    return jnp.concatenate([t[:, :, 0:64].reshape(-1, 512), t[:, :, 64:128].reshape(-1, 512)], axis=1)


def _cols_kv_inv(w):
    r = w.shape[0]
    return jnp.concatenate([w[:, 0:512].reshape(r, HEADS, 64), w[:, 512:1024].reshape(r, HEADS, 64)],
                           axis=2).reshape(r, HEADS * 128)


def _pick(n, cands):
    for c in cands:
        if n % c == 0:
            return c
    raise ValueError(f"no tile for {n}")


def _cparams(*sem):
    return pltpu.CompilerParams(dimension_semantics=sem, vmem_limit_bytes=VMEM_LIMIT_V7X)


def _mm(a, b):
    return jnp.dot(a.astype(BF16), b.astype(BF16), preferred_element_type=F32)


def _mm_nt(a, b):
    return lax.dot_general(a.astype(BF16), b.astype(BF16), (((1,), (1,)), ((), ())), preferred_element_type=F32)


def _mm_tn(a, b):
    return lax.dot_general(a.astype(BF16), b.astype(BF16), (((0,), (0,)), ((), ())), preferred_element_type=F32)


def _mm_exact(a, b):
    return jnp.dot(a, b, precision=lax.Precision.HIGHEST, preferred_element_type=F32)


def _rms(x):
    return lax.rsqrt(jnp.mean(x * x, axis=-1, keepdims=True) + EPS)


def _rms_bwd(dy, x, r, w):
    xh = x * r
    g = dy * w
    dx = r * (g - xh * jnp.mean(g * xh, axis=-1, keepdims=True))
    dw = jnp.sum(dy * xh, axis=0, keepdims=True)
    return dx, dw


def _sigmoid(z):
    return 1.0 / (1.0 + jnp.exp(-z))


def _silu_and_grad(z):
    s = _sigmoid(z)
    return z * s, s * (1.0 + z * (1.0 - s))


_GELU_C = math.sqrt(2.0 / math.pi)


def _gelu_and_grad(x):
    x2 = x * x
    t = jnp.tanh(_GELU_C * (x + 0.044715 * x * x2))
    val = 0.5 * x * (1.0 + t)
    grad = 0.5 * (1.0 + t) + 0.5 * x * (1.0 - t * t) * _GELU_C * (1.0 + 3.0 * 0.044715 * x2)
    return val, grad


def _acc(ref, val, first):
    @pl.when(first)
    def _():
        ref[...] = val

    @pl.when(jnp.logical_not(first))
    def _():
        ref[...] += val


def _rows_call(name, body, tr, row_ins, full_ins, row_outs, acc_outs):
    lp = row_ins[0].shape[0]
    in_specs = [pl.BlockSpec((tr, a.shape[1]), lambda i: (i, 0)) for a in row_ins]
    in_specs += [pl.BlockSpec(a.shape, lambda i, n=a.ndim: (0,) * n) for a in full_ins]
    out_specs = [pl.BlockSpec((tr, c), lambda i: (i, 0)) for c, _ in row_outs]
    out_specs += [pl.BlockSpec(s, lambda i, n=len(s): (0,) * n) for s, _ in acc_outs]
    out_shape = [jax.ShapeDtypeStruct((lp, c), dt) for c, dt in row_outs]
    out_shape += [jax.ShapeDtypeStruct(s, dt) for s, dt in acc_outs]
    return pl.pallas_call(
        body, name=name, grid=(lp // tr,), in_specs=in_specs, out_specs=out_specs, out_shape=out_shape,
        compiler_params=_cparams("arbitrary"))(*row_ins, *full_ins)


def _inproj(h, pre_w, w_in_b, tr):
    def body(h_ref, pw_ref, w_ref, ql, kvl, ag, su, sg, kr):
        x = h_ref[...]
        xn = x * _rms(x) * pw_ref[...]
        pr = _mm(xn, w_ref[...])
        ql[...] = pr[:, 0:256]
        kvl[...] = pr[:, 256:384]
        ag[...] = pr[:, 384:896]
        su[...] = pr[:, 896:1408]
        sg[...] = pr[:, 1408:1920]
        kr[...] = pr[:, 1920:1952]

    return _rows_call("inproj", body, tr, [h], [pre_w, w_in_b],
                      [(256, F32), (128, F32), (512, F32), (512, F32), (512, F32), (32, F32)], [])


def _qkv_up(ql, kvl, kr, cos8, sin8, c32, s32, qw, kvw, wq_b, wkv_b, p32, tr):
    def body(ql_ref, kvl_ref, kr_ref, cos_ref, sin_ref, c32_ref, s32_ref, qw_ref, kvw_ref, wq_ref, wkv_ref, p_ref,
             qn_o, qr1_o, qr2_o, kn_o, v_o, kr_o):
        x = ql_ref[...]
        q = _mm(x * _rms(x) * qw_ref[...], wq_ref[...]) * Q_SCALE
        r1, r2 = q[:, 512:640], q[:, 640:768]
        cs, sn = cos_ref[...], sin_ref[...]
        qn_o[...] = q[:, 0:512].astype(BF16)
        qr1_o[...] = (r1 * cs - r2 * sn).astype(BF16)
        qr2_o[...] = (r2 * cs + r1 * sn).astype(BF16)
        x = kvl_ref[...]
        kv = _mm(x * _rms(x) * kvw_ref[...], wkv_ref[...])
        kn_o[...] = kv[:, 0:512].astype(BF16)
        v_o[...] = kv[:, 512:1024].astype(BF16)
        x = kr_ref[...]
        kr_o[...] = (x * c32_ref[...] + _mm_exact(x, p_ref[...]) * s32_ref[...]).astype(BF16)

    return _rows_call("qkv_up", body, tr, [ql, kvl, kr, cos8, sin8, c32, s32], [qw, kvw, wq_b, wkv_b, p32],
                      [(512, BF16), (128, BF16), (128, BF16), (512, BF16), (512, BF16), (32, BF16)], [])


def _row_position(row, lseg):
    return (row & (N_SEG - 1)) * lseg + (row >> 3)


def _first_padded_tile(n_valid, lp, tile):
    lseg = lp // N_SEG
    t0 = n_valid - (N_SEG - 1) * lseg
    return (t0 * N_SEG + N_SEG - 1) // tile if n_valid < lp else lp // tile


def _attn_fwd(q, k, vx, n_valid, tq, tk):
    lp = q.shape[1]
    nq, nk = lp // tq, lp // tk
    lseg = lp // N_SEG
    n_plain = max(0, min(nk, _first_padded_tile(n_valid, lp, tk)))

    def body(q_ref, k_ref, v_ref, o_ref, lse_ref, m_s, acc_s):
        m_s[...] = jnp.full(m_s.shape, -1e30, F32)
        acc_s[...] = jnp.zeros(acc_s.shape, F32)
        qq = q_ref[0]

        def chunk(c, padded):
            r0 = pl.multiple_of(c * tk, tk)
            s = _mm_nt(qq, k_ref[0, pl.ds(r0, tk), :])
            if padded:
                col = r0 + lax.broadcasted_iota(jnp.int32, (tq, tk), 1)
                s = jnp.where(_row_position(col, lseg) < n_valid, s, -1e30)
            m_old = m_s[...]
            m_new = jnp.maximum(m_old, jnp.max(s, axis=1, keepdims=True))
            p = jnp.exp2(s - m_new)
            acc_s[...] = jnp.exp2(m_old - m_new) * acc_s[...] + _mm(p, v_ref[0, pl.ds(r0, tk), :])
            m_s[...] = m_new

        def plain(c, carry):
            chunk(c, False)
            return carry

        n_even = n_plain - n_plain % 2
        if n_even:
            lax.fori_loop(0, n_even, plain, 0, unroll=2)
        for c in range(n_even, nk):
            chunk(c, c >= n_plain)
        acc = acc_s[...]
        l = acc[:, V_HEAD:V_HEAD + 1]
        o_ref[0] = acc[:, :V_HEAD] / l
        lse_ref[0] = m_s[...] + jnp.log2(l)

    return pl.pallas_call(
        body, name="attn_fwd", grid=(HEADS, nq),
        in_specs=[pl.BlockSpec((1, tq, QK_DIM), lambda h, i: (h, i, 0)),
                  pl.BlockSpec((1, lp, QK_DIM), lambda h, i: (h, 0, 0)),
                  pl.BlockSpec((1, lp, LANES), lambda h, i: (h, 0, 0))],
        out_specs=[pl.BlockSpec((1, tq, V_HEAD), lambda h, i: (h, i, 0)),
                   pl.BlockSpec((1, tq, 1), lambda h, i: (h, i, 0))],
        out_shape=[jax.ShapeDtypeStruct((HEADS, lp, V_HEAD), F32), jax.ShapeDtypeStruct((HEADS, lp, 1), F32)],
        scratch_shapes=[pltpu.VMEM((tq, 1), F32), pltpu.VMEM((tq, LANES), F32)],
        compiler_params=_cparams("arbitrary", "arbitrary"))(q, k, vx)


def _attn_post(o_flat, ag, aw, tr):
    def body(o_ref, g_ref, w_ref, ya):
        t = o_ref[...] * _silu_and_grad(g_ref[...])[0]
        ya[...] = t * _rms(t) * w_ref[...]

    return _rows_call("attn_post", body, tr, [o_flat, ag], [aw], [(512, F32)], [])[0]


def _ssm_in(name, urows, w_b, tr):
    def body(u_ref, w_ref, re_o, im_o):
        r = _mm(u_ref[...], w_ref[...])
        re_o[...] = r[:, :N_STATES]
        im_o[...] = r[:, N_STATES:]

    return _rows_call(name, body, tr, [urows], [w_b], [(N_STATES, F32), (N_STATES, F32)], [])


def _ssm_out(name, xs, wt_b, rows_add, dvec, tr):
    n = len(xs)

    def body(*refs):
        x_refs, a_ref, w_ref, d_ref, y_o = refs[:n], refs[n], refs[n + 1], refs[n + 2], refs[n + 3]
        y = a_ref[...] * d_ref[...]
        for p in range(n):
            y = y + _mm_nt(x_refs[p][...], w_ref[:, p * N_STATES:(p + 1) * N_STATES])
        y_o[...] = y

    return _rows_call(name, body, tr, list(xs) + [rows_add], [wt_b, dvec], [(D_SSM, F32)], [])[0]


def _ssm_wgrad(name, urows, xre, xim, tr):
    def body(u_ref, re_ref, im_ref, dre_o, dim_o):
        first = pl.program_id(0) == 0
        u = u_ref[...]
        _acc(dre_o, _mm_tn(u, re_ref[...]), first)
        _acc(dim_o, _mm_tn(u, im_ref[...]), first)

    return _rows_call(name, body, tr, [urows, xre, xim], [], [],
                      [((D_SSM, N_STATES), F32), ((D_SSM, N_STATES), F32)])


def _scan_tiles(lp):
    lseg = lp // N_SEG
    tt = _pick(lseg, [104, 48, 32, 16, 8, 4, 2, 1])
    return lseg, tt, lseg // tt


def _cmul(ar, ai, br, bi):
    return ar * br - ai * bi, ar * bi + ai * br


def _scan_local(name, bre, bim, ar8, ai8, forward):
    lp = bre.shape[0]
    lseg, tt, nt = _scan_tiles(lp)
    ncb = N_STATES // COL_BLK

    def tile(t):
        return t if forward else nt - 1 - t

    def body(bre_ref, bim_ref, ar_ref, ai_ref, xre_o, xim_o, ere_o, eim_o, cr_s, ci_s):
        t = pl.program_id(1)

        @pl.when(t == 0)
        def _():
            cr_s[...] = jnp.zeros(cr_s.shape, F32)
            ci_s[...] = jnp.zeros(ci_s.shape, F32)

        ar, ai = ar_ref[...], ai_ref[...]

        def step(s, carry):
            cr, ci = carry
            r0 = pl.multiple_of((s if forward else tt - 1 - s) * N_SEG, N_SEG)
            pr, pi = _cmul(ar, ai, cr, ci)
            xr = pr + bre_ref[pl.ds(r0, N_SEG), :]
            xi = pi + bim_ref[pl.ds(r0, N_SEG), :]
            xre_o[pl.ds(r0, N_SEG), :] = xr
            xim_o[pl.ds(r0, N_SEG), :] = xi
            return xr, xi

        cr, ci = lax.fori_loop(0, tt, step, (cr_s[...], ci_s[...]), unroll=4 if tt % 4 == 0 else 1)
        cr_s[...] = cr
        ci_s[...] = ci

        @pl.when(t == nt - 1)
        def _():
            ere_o[...] = cr
            eim_o[...] = ci

    blk = pl.BlockSpec((tt * N_SEG, COL_BLK), lambda cb, t: (tile(t), cb))
    small = pl.BlockSpec((N_SEG, COL_BLK), lambda cb, t: (0, cb))
    return pl.pallas_call(
        body, name=name, grid=(ncb, nt), in_specs=[blk, blk, small, small], out_specs=[blk, blk, small, small],
        out_shape=[jax.ShapeDtypeStruct((lp, N_STATES), F32)] * 2 + [jax.ShapeDtypeStruct((N_SEG, N_STATES), F32)] * 2,
        scratch_shapes=[pltpu.VMEM((N_SEG, COL_BLK), F32)] * 2,
        compiler_params=_cparams("arbitrary", "arbitrary"))(bre, bim, ar8, ai8)


def _scan_fix(name, lre, lim, ar8, ai8, ere, eim, forward, states=None):
    lp = lre.shape[0]
    lseg, tt, nt = _scan_tiles(lp)
    ncb = N_STATES // COL_BLK
    adj = states is not None

    def tile(t):
        return t if forward else nt - 1 - t

    def body(*refs):
        lre_ref, lim_ref, ar_ref, ai_ref, ere_ref, eim_ref = refs[:6]
        k = 6
        if adj:
            sre_ref, sim_ref = refs[k:k + 2]
            k += 2
        xre_o, xim_o = refs[k:k + 2]
        k += 2
        if adj:
            dre_o, dim_o = refs[k:k + 2]
            k += 2
        s_re, s_im, pw_re, pw_im = refs[k:k + 4]
        k += 4
        if adj:
            gp_re, gp_im = refs[k:k + 2]
        t = pl.program_id(1)
        ar, ai = ar_ref[...], ai_ref[...]

        @pl.when(t == 0)
        def _():
            a1r, a1i = ar[0:1, :], ai[0:1, :]
            pr, pi = jnp.ones_like(a1r), jnp.zeros_like(a1i)
            br, bi = a1r, a1i
            n = lseg
            while n:
                if n & 1:
                    pr, pi = _cmul(pr, pi, br, bi)
                n >>= 1
                if n:
                    br, bi = _cmul(br, bi, br, bi)
            cr, ci = jnp.zeros_like(a1r), jnp.zeros_like(a1i)
            order = range(N_SEG) if forward else range(N_SEG - 1, -1, -1)
            for j in order:
                s_re[j:j + 1, :] = cr
                s_im[j:j + 1, :] = ci
                nr, ni = _cmul(pr, pi, cr, ci)
                cr = nr + ere_ref[j:j + 1, :]
                ci = ni + eim_ref[j:j + 1, :]
            pw_re[...] = ar
            pw_im[...] = ai
            if adj:
                gp_re[...] = s_re[...]
                gp_im[...] = s_im[...]
                dre_o[...] = jnp.zeros(dre_o.shape, F32)
                dim_o[...] = jnp.zeros(dim_o.shape, F32)

        sr, si = s_re[...], s_im[...]

        def step(s, carry):
            r0 = pl.multiple_of((s if forward else tt - 1 - s) * N_SEG, N_SEG)
            pr, pi = carry[0], carry[1]
            cr, ci = _cmul(pr, pi, sr, si)
            xr = lre_ref[pl.ds(r0, N_SEG), :] + cr
            xi = lim_ref[pl.ds(r0, N_SEG), :] + ci
            xre_o[pl.ds(r0, N_SEG), :] = xr
            xim_o[pl.ds(r0, N_SEG), :] = xi
            npr, npi = _cmul(pr, pi, ar, ai)
            if not adj:
                return npr, npi
            gr, gi, dr, di = carry[2:]
            fr = sre_ref[pl.ds(r0, N_SEG), :]
            fi = sim_ref[pl.ds(r0, N_SEG), :]
            dr = dr + fr * gr + fi * gi
            di = di + fr * gi - fi * gr
            return npr, npi, xr, xi, dr, di

        init = (pw_re[...], pw_im[...])
        if adj:
            init = init + (gp_re[...], gp_im[...], dre_o[...], dim_o[...])
        out = lax.fori_loop(0, tt, step, init, unroll=4 if tt % 4 == 0 else 1)
        pw_re[...] = out[0]
        pw_im[...] = out[1]
        if adj:
            gp_re[...] = out[2]
            gp_im[...] = out[3]
            dre_o[...] = out[4]
            dim_o[...] = out[5]

    blk = pl.BlockSpec((tt * N_SEG, COL_BLK), lambda cb, t: (tile(t), cb))
    small = pl.BlockSpec((N_SEG, COL_BLK), lambda cb, t: (0, cb))
    ins = [lre, lim, ar8, ai8, ere, eim] + (list(states) if adj else [])
    in_specs = [blk, blk, small, small, small, small] + ([blk, blk] if adj else [])
    out_specs = [blk, blk] + ([small, small] if adj else [])
    out_shape = [jax.ShapeDtypeStruct((lp, N_STATES), F32)] * 2
    if adj:
        out_shape += [jax.ShapeDtypeStruct((N_SEG, N_STATES), F32)] * 2
    scratch = [pltpu.VMEM((N_SEG, COL_BLK), F32)] * (6 if adj else 4)
    return pl.pallas_call(
        body, name=name, grid=(ncb, nt), in_specs=in_specs, out_specs=out_specs, out_shape=out_shape,
        scratch_shapes=scratch, compiler_params=_cparams("arbitrary", "arbitrary"))(*ins)


def _scan(name, bre, bim, ar8, ai8, forward, states=None):
    lre, lim, ere, eim = _scan_local(name + "_local", bre, bim, ar8, ai8, forward)
    return _scan_fix(name + "_fix", lre, lim, ar8, ai8, ere, eim, forward, states)


def _ssm_post(ypre, sg, wglu_b, bglu, sw, tr):
    def body(y_ref, g_ref, w_ref, b_ref, sw_ref, glu_o, ysn_o):
        glu = _mm(_gelu_and_grad(y_ref[...])[0], w_ref[...]) + b_ref[...]
        glu_o[...] = glu
        t = glu[:, :D_SSM] * _sigmoid(glu[:, D_SSM:]) * _silu_and_grad(g_ref[...])[0]
        ysn_o[...] = t * _rms(t) * sw_ref[...]

    return _rows_call("ssm_post", body, tr, [ypre, sg], [wglu_b, bglu, sw], [(1024, F32), (512, F32)], [])


def _out_loss(ya, ysn, h, tgt, wo_b, post_w, n_valid, tr):
    lseg = h.shape[0] // N_SEG

    def body(ya_ref, ys_ref, h_ref, t_ref, w_ref, pw_ref, dy_o, dout_o, loss_o, dpw_o):
        i = pl.program_id(0)
        y = _mm(ya_ref[...], w_ref[0:D_ATTN, :]) + _mm(ys_ref[...], w_ref[D_ATTN:, :])
        r = _rms(y)
        pw = pw_ref[...]
        out = h_ref[...] + y * r * pw
        pos = _row_position(i * tr + lax.broadcasted_iota(jnp.int32, (tr, 1), 0), lseg)
        valid = jnp.logical_and(pos >= N_META, pos < n_valid)
        diff = jnp.where(valid, out - t_ref[...], 0.0)
        dout = diff * (1.0 / D_MODEL)
        dy, dpw = _rms_bwd(dout, y, r, pw)
        dy_o[...] = dy
        dout_o[...] = dout
        _acc(loss_o, 0.5 * jnp.sum(jnp.sum(diff * diff, axis=1, keepdims=True), axis=0, keepdims=True)
             * (1.0 / D_MODEL), i == 0)
        _acc(dpw_o, dpw, i == 0)

    return _rows_call("out_loss", body, tr, [ya, ysn, h, tgt], [wo_b, post_w], [(1024, F32), (1024, F32)],
                      [((1, 1), F32), ((1, D_MODEL), F32)])


def _out_bwd(dy, ya, ysn, o_flat, ag, wo_b, aw, head_sum, tr):
    def body(dy_ref, ya_ref, ys_ref, o_ref, g_ref, w_ref, aw_ref, hs_ref, do_o, dag_o, dysn_o, dl_o, dwo_o, daw_o):
        i = pl.program_id(0)
        dy = dy_ref[...]
        dcat = _mm_nt(dy, w_ref[...])
        cat = jnp.concatenate([ya_ref[...], ys_ref[...]], axis=1)
        _acc(dwo_o, _mm_tn(cat, dy), i == 0)
        dysn_o[...] = dcat[:, D_ATTN:]
        o = o_ref[...]
        sl, dsl = _silu_and_grad(g_ref[...])
        t = o * sl
        dt, daw = _rms_bwd(dcat[:, :D_ATTN], t, _rms(t), aw_ref[...])
        _acc(daw_o, daw, i == 0)
        do = dt * sl
        do_o[...] = do
        dag_o[...] = dt * o * dsl
        dl_o[...] = _mm_exact(do * o, hs_ref[...])

    return _rows_call("out_bwd", body, tr, [dy, ya, ysn, o_flat, ag], [wo_b, aw, head_sum],
                      [(512, F32), (512, F32), (512, F32), (HEADS, F32)],
                      [((D_MODEL, D_MODEL), F32), ((1, D_ATTN), F32)])


def _ssm_post_bwd(dysn, glu, sg, ypre, u, wglu_b, sw, dvec, tr):
    def body(d_ref, glu_ref, sg_ref, y_ref, u_ref, w_ref, sw_ref, dv_ref,
             dyp_o, dsg_o, dwg_o, dbg_o, dsw_o, dd_o):
        i = pl.program_id(0)
        glu = glu_ref[...]
        a, b = glu[:, :D_SSM], glu[:, D_SSM:]
        sb = _sigmoid(b)
        ys = a * sb
        sl, dsl = _silu_and_grad(sg_ref[...])
        t = ys * sl
        dt, dsw = _rms_bwd(d_ref[...], t, _rms(t), sw_ref[...])
        _acc(dsw_o, dsw, i == 0)
        dsg_o[...] = dt * ys * dsl
        dys = dt * sl
        dglu = jnp.concatenate([dys * sb, dys * a * sb * (1.0 - sb)], axis=1)
        _acc(dbg_o, jnp.sum(dglu, axis=0, keepdims=True), i == 0)
        gel, dgel = _gelu_and_grad(y_ref[...])
        _acc(dwg_o, _mm_tn(gel, dglu), i == 0)
        dyp = _mm_nt(dglu, w_ref[...]) * dgel
        dyp_o[...] = dyp
        _acc(dd_o, jnp.sum(dyp * u_ref[...], axis=0, keepdims=True), i == 0)

    return _rows_call("ssm_post_bwd", body, tr, [dysn, glu, sg, ypre, u], [wglu_b, sw, dvec],
                      [(512, F32), (512, F32)],
                      [((D_SSM, 2 * D_SSM), F32), ((1, 2 * D_SSM), F32), ((1, D_SSM), F32), ((1, D_SSM), F32)])


def _attn_bwd(q, k, v, do, lse_t, delta_t, tq, tk):
    lp = q.shape[1]
    nq, nk = lp // tq, lp // tk
    assert lse_t.shape == (HEADS, nq, 1, tq) and delta_t.shape == (HEADS, nq, 1, tq)

    def body(q_ref, k_ref, v_ref, do_ref, lse_ref, dl_ref, dq_o, dk_o, dv_o, dk_s, dv_s):
        @pl.when(pl.program_id(1) == 0)
        def _():
            dq_o[...] = jnp.zeros(dq_o.shape, F32)

        dk_s[...] = jnp.zeros(dk_s.shape, F32)
        dv_s[...] = jnp.zeros(dv_s.shape, F32)
        kk = k_ref[0]
        vv = v_ref[0]

        def chunk(c, carry):
            r0 = pl.multiple_of(c * tq, tq)
            qq = q_ref[0, pl.ds(r0, tq), :]
            dd = do_ref[0, pl.ds(r0, tq), :]
            pt = jnp.exp2(_mm_nt(kk, qq) - lse_ref[0, c])
            dv_s[...] += _mm(pt, dd)
            dst = (pt * (_mm_nt(vv, dd) - dl_ref[0, c])).astype(BF16)
            dk_s[...] += _mm(dst, qq)
            dq_o[0, pl.ds(r0, tq), :] += _mm_tn(dst, kk)
            return carry

        n_even = nq - nq % 2
        if n_even:
            lax.fori_loop(0, n_even, chunk, 0, unroll=2)
        for c in range(n_even, nq):
            chunk(c, 0)
        dk_o[0] = dk_s[...]
        dv_o[0] = dv_s[...]

    head = lambda w: pl.BlockSpec((1, lp, w), lambda h, j: (h, 0, 0))
    kspec = lambda w: pl.BlockSpec((1, tk, w), lambda h, j: (h, j, 0))
    stat = pl.BlockSpec((1, nq, 1, tq), lambda h, j: (h, 0, 0, 0))
    return pl.pallas_call(
        body, name="attn_bwd", grid=(HEADS, nk),
        in_specs=[head(QK_DIM), kspec(QK_DIM), kspec(V_HEAD), head(V_HEAD), stat, stat],
        out_specs=[head(QK_DIM), kspec(QK_DIM), kspec(V_HEAD)],
        out_shape=[jax.ShapeDtypeStruct((HEADS, lp, QK_DIM), F32), jax.ShapeDtypeStruct((HEADS, lp, QK_DIM), F32),
                   jax.ShapeDtypeStruct((HEADS, lp, V_HEAD), F32)],
        scratch_shapes=[pltpu.VMEM((tk, QK_DIM), F32), pltpu.VMEM((tk, V_HEAD), F32)],
        compiler_params=_cparams("arbitrary", "arbitrary"))(q, k, v, do, lse_t, delta_t)


def _qkv_up_bwd(dqn, dr1, dr2, dkn, dv, dkr8, ql, kvl, cos8, sin8, c32, s32, qw, kvw, wq_b, wkv_b, p32, sum8, tr):
    def body(dqn_ref, dr1_ref, dr2_ref, dkn_ref, dv_ref, dkr_ref, ql_ref, kvl_ref, cos_ref, sin_ref, c32_ref,
             s32_ref, qw_ref, kvw_ref, wq_ref, wkv_ref, p_ref, s8_ref,
             dql_o, dkvl_o, dkrr_o, dwq_o, dwkv_o, dqw_o, dkvw_o):
        i = pl.program_id(0)
        cs, sn = cos_ref[...], sin_ref[...]
        d1, d2 = dr1_ref[...], dr2_ref[...]
        dq = jnp.concatenate([dqn_ref[...], d1 * cs + d2 * sn, d2 * cs - d1 * sn], axis=1) * (Q_SCALE / LOG2E)
        x = ql_ref[...]
        r = _rms(x)
        qw = qw_ref[...]
        _acc(dwq_o, _mm_tn(x * r * qw, dq), i == 0)
        dx, dw = _rms_bwd(_mm_nt(dq, wq_ref[...]), x, r, qw)
        dql_o[...] = dx
        _acc(dqw_o, dw, i == 0)
        dkv = jnp.concatenate([dkn_ref[...] * (1.0 / LOG2E), dv_ref[...]], axis=1)
        x = kvl_ref[...]
        r = _rms(x)
        kvw = kvw_ref[...]
        _acc(dwkv_o, _mm_tn(x * r * kvw, dkv), i == 0)
        dx, dw = _rms_bwd(_mm_nt(dkv, wkv_ref[...]), x, r, kvw)
        dkvl_o[...] = dx
        _acc(dkvw_o, dw, i == 0)
        dkr = _mm_exact(dkr_ref[...], s8_ref[...]) * (1.0 / LOG2E)
        dkrr_o[...] = dkr * c32_ref[...] + _mm_exact(dkr * s32_ref[...], p_ref[...])

    return _rows_call("qkv_up_bwd", body, tr, [dqn, dr1, dr2, dkn, dv, dkr8, ql, kvl, cos8, sin8, c32, s32],
                      [qw, kvw, wq_b, wkv_b, p32, sum8], [(256, F32), (128, F32), (32, F32)],
                      [((Q_LORA, 768), F32), ((KV_LORA, 1024), F32), ((1, Q_LORA), F32), ((1, KV_LORA), F32)])


def _inproj_bwd(dql, dkvl, dag, du, dsg, dkr, h, dout, pre_w, w_in_b, tr):
    def body(dql_ref, dkvl_ref, dag_ref, du_ref, dsg_ref, dkr_ref, h_ref, dout_ref, pw_ref, w_ref,
             dh_o, dwin_o, dpw_o):
        i = pl.program_id(0)
        dproj = jnp.concatenate([dql_ref[...], dkvl_ref[...], dag_ref[...], du_ref[...], dsg_ref[...],
                                 dkr_ref[...], jnp.zeros((tr, D_IN_PAD - D_IN), F32)], axis=1)
        x = h_ref[...]
        r = _rms(x)
        pw = pw_ref[...]
        _acc(dwin_o, _mm_tn(x * r * pw, dproj), i == 0)
        dx, dw = _rms_bwd(_mm_nt(dproj, w_ref[...]), x, r, pw)
        _acc(dpw_o, dw, i == 0)
        dh_o[...] = dout_ref[...] + dx

    return _rows_call("inproj_bwd", body, tr, [dql, dkvl, dag, du, dsg, dkr, h, dout], [pre_w, w_in_b],
                      [(1024, F32)], [((D_MODEL, D_IN_PAD), F32), ((1, D_MODEL), F32)])


def _disc_terms(a_re, a_im, ldt):
    dt = jnp.exp(ldt)
    mag = jnp.exp(a_re * dt)
    th = a_im * dt
    cs, sn = jnp.cos(th), jnp.sin(th)
    abar_re, abar_im = mag * cs, mag * sn
    num_re, num_im = abar_re - 1.0, abar_im
    den = a_re * a_re + a_im * a_im
    coef_re = (num_re * a_re + num_im * a_im) / den
    coef_im = (num_im * a_re - num_re * a_im) / den
    return dt, mag, cs, sn, abar_re, abar_im, num_re, num_im, den, coef_re, coef_im


def _ssm_disc(a_re, a_im, ldt, bt_re, bt_im):
    def body(ar_ref, ai_ref, l_ref, br_ref, bi_ref, abr_o, abi_o, bbr_o, bbi_o):
        t = _disc_terms(ar_ref[...], ai_ref[...], l_ref[...])
        abr_o[...] = t[4]
        abi_o[...] = t[5]
        cr, ci = t[9], t[10]
        br, bi = br_ref[...], bi_ref[...]
        bbr_o[...] = cr * br - ci * bi
        bbi_o[...] = cr * bi + ci * br

    ng = a_re.shape[0]
    return pl.pallas_call(
        body, name="ssm_disc",
        out_shape=[jax.ShapeDtypeStruct((ng, 1, SSM_STATE), F32)] * 2
        + [jax.ShapeDtypeStruct((ng, SSM_GROUP, SSM_STATE), F32)] * 2)(a_re, a_im, ldt, bt_re, bt_im)


def _ssm_disc_bwd(a_re, a_im, ldt, bt_re, bt_im, da8_re, da8_im, dbb_re, dbb_im):
    def body(ar_ref, ai_ref, l_ref, br_ref, bi_ref, dar_ref, dai_ref, dbr_ref, dbi_ref,
             gar_o, gai_o, gl_o, gbr_o, gbi_o):
        a_re, a_im = ar_ref[...], ai_ref[...]
        dt, mag, cs, sn, abar_re, abar_im, num_re, num_im, den, cr, ci = _disc_terms(a_re, a_im, l_ref[...])
        br, bi = br_ref[...], bi_ref[...]
        dbr, dbi = dbr_ref[...], dbi_ref[...]
        gbr_o[...] = cr * dbr + ci * dbi
        gbi_o[...] = cr * dbi - ci * dbr
        dcr = jnp.sum(br * dbr + bi * dbi, axis=1, keepdims=True)
        dci = jnp.sum(br * dbi - bi * dbr, axis=1, keepdims=True)
        dnum_re = (dcr * a_re - dci * a_im) / den
        dnum_im = (dcr * a_im + dci * a_re) / den
        dden = -(dcr * cr + dci * ci) / den
        g_are = (dcr * num_re + dci * num_im) / den + dden * 2.0 * a_re
        g_aim = (dcr * num_im - dci * num_re) / den + dden * 2.0 * a_im
        d_abr = jnp.sum(dar_ref[...], axis=1, keepdims=True) + dnum_re
        d_abi = jnp.sum(dai_ref[...], axis=1, keepdims=True) + dnum_im
        dmag = d_abr * cs + d_abi * sn
        dth = d_abi * abar_re - d_abr * abar_im
        g_are = g_are + dmag * mag * dt
        g_aim = g_aim + dth * dt
        ddt = jnp.sum(dmag * mag * a_re + dth * a_im, axis=2, keepdims=True)
        gar_o[...] = g_are
        gai_o[...] = g_aim
        gl_o[...] = ddt * dt

    ng = a_re.shape[0]
    return pl.pallas_call(
        body, name="ssm_disc_bwd",
        out_shape=[jax.ShapeDtypeStruct((ng, 1, SSM_STATE), F32)] * 2 + [jax.ShapeDtypeStruct((ng, 1, 1), F32)]
        + [jax.ShapeDtypeStruct((ng, SSM_GROUP, SSM_STATE), F32)] * 2)(
            a_re, a_im, ldt, bt_re, bt_im, da8_re, da8_im, dbb_re, dbb_im)


def _exchange(name, buf, all_to_all):
    rows = buf.shape[-2]

    def body(in_ref, out_ref, send_sems, recv_sems, local_sem):
        x, y, c = lax.axis_index("x"), lax.axis_index("y"), lax.axis_index("c")
        me = 4 * x + 2 * y + c
        copies = []
        for k in range(1, N_DEV):
            px = 1 - x if (k >> 2) & 1 else x
            py = 1 - y if (k >> 1) & 1 else y
            pc = 1 - c if k & 1 else c
            src = in_ref.at[4 * px + 2 * py + pc] if all_to_all else in_ref
            copies.append(pltpu.make_async_remote_copy(
                src_ref=src, dst_ref=out_ref.at[me], send_sem=send_sems.at[k - 1], recv_sem=recv_sems.at[k - 1],
                device_id=(px, py, pc), device_id_type=pl.DeviceIdType.MESH))
        mine = pltpu.make_async_copy(in_ref.at[me] if all_to_all else in_ref, out_ref.at[me], local_sem)
        mine.start()
        for cp in copies:
            cp.start()
        for cp in copies:
            cp.wait()
        mine.wait()

    return pl.pallas_call(
        body, name=name, out_shape=jax.ShapeDtypeStruct((N_DEV, rows, LANES), F32),
        in_specs=[pl.BlockSpec(memory_space=pl.ANY)], out_specs=pl.BlockSpec(memory_space=pl.ANY),
        scratch_shapes=[pltpu.SemaphoreType.DMA((N_DEV - 1,)), pltpu.SemaphoreType.DMA((N_DEV - 1,)),
                        pltpu.SemaphoreType.DMA(())])(buf)


def _adamw(recv, w, m, v, tr):
    rows = w.shape[0]
    c1 = 1.0 - ADAM_B1 ** ADAM_STEP
    c2 = 1.0 - ADAM_B2 ** ADAM_STEP

    def body(r_ref, w_ref, m_ref, v_ref, g_o, d_o, m_o, v_o):
        g = r_ref[0]
        for k in range(1, N_DEV):
            g = g + r_ref[k]
        mm = ADAM_B1 * m_ref[...] + (1.0 - ADAM_B1) * g
        vv = ADAM_B2 * v_ref[...] + (1.0 - ADAM_B2) * (g * g)
        g_o[...] = g
        m_o[...] = mm
        v_o[...] = vv
        d_o[...] = -ADAM_LR * ((mm / c1) / (jnp.sqrt(vv / c2) + ADAM_EPS) + ADAM_WD * w_ref[...])

    spec = pl.BlockSpec((tr, LANES), lambda i: (i, 0))
    return pl.pallas_call(
        body, name="adamw", grid=(rows // tr,),
        in_specs=[pl.BlockSpec((N_DEV, tr, LANES), lambda i: (0, i, 0)), spec, spec, spec],
        out_specs=[spec] * 4, out_shape=[jax.ShapeDtypeStruct((rows, LANES), F32)] * 4,
        compiler_params=_cparams("arbitrary"))(recv, w, m, v)


def _to_rows(a):
    flat = a.reshape(-1)
    pad = (-flat.shape[0]) % LANES
    if pad:
        flat = jnp.concatenate([flat, jnp.zeros((pad,), flat.dtype)])
    return flat.reshape(-1, LANES)


def _n_rows(shape):
    return -(-int(np.prod(shape)) // LANES)


def _pack(arrays, total_rows):
    rows = [_to_rows(a) for a in arrays]
    used = sum(r.shape[0] for r in rows)
    if total_rows > used:
        rows.append(jnp.zeros((total_rows - used, LANES), F32))
    return jnp.concatenate(rows, axis=0)


def _unpack(buf, shapes):
    out, r0 = [], 0
    for s in shapes:
        n = int(np.prod(s))
        nr = _n_rows(s)
        out.append(buf[r0:r0 + nr].reshape(-1)[:n].reshape(s))
        r0 += nr
    return out


def _shard_views(name, full):
    if name == 'w_out':
        return full.reshape(N_DEV, full.shape[0] // N_DEV, full.shape[1])
    r, ccols = full.shape
    return full.reshape(r, N_DEV, ccols // N_DEV).transpose(1, 0, 2)


def _from_shards(name, stacked):
    if name == 'w_out':
        return stacked.reshape(-1, stacked.shape[-1])
    n, r, cc = stacked.shape
    return stacked.transpose(1, 0, 2).reshape(r, n * cc)


def _block_diag(t):
    eye = jnp.eye(N_GROUPS, dtype=t.dtype)
    return (t[:, :, None, :] * eye[:, None, :, None]).reshape(D_SSM, N_STATES)


def _diag_blocks(mat):
    return jnp.stack([mat[g * SSM_GROUP:(g + 1) * SSM_GROUP, g * SSM_STATE:(g + 1) * SSM_STATE]
                      for g in range(N_GROUPS)])


def _step(x, loss_target, wts, moms, vels):
    seq = x.shape[1]
    n_valid = N_META + seq
    lp = -(-n_valid // 256) * 256
    tr = _pick(lp, [640, 256])
    tr_small = 256
    tq = _pick(lp, [640, 256])
    tk = tq

    shard_shapes = [wts[n].shape[-2:] for n in SHARDED]
    n_shard_rows = sum(_n_rows(s) for s in shard_shapes)
    gathered = _exchange("gather_weights", _pack([wts[n].reshape(wts[n].shape[-2:]) for n in SHARDED],
                                                 n_shard_rows), all_to_all=False)
    parts = [[] for _ in SHARDED]
    for dev in range(N_DEV):
        for i, a in enumerate(_unpack(gathered[dev], shard_shapes)):
            parts[i].append(a)
    full = {n: _from_shards(n, jnp.stack(parts[i])) for i, n in enumerate(SHARDED)}

    w_in_b = jnp.concatenate([_cols_in(full['w_in']), jnp.zeros((D_MODEL, D_IN_PAD - D_IN), F32)],
                             axis=1).astype(BF16)
    wq_b = _cols_q(full['w_q_up']).astype(BF16)
    wkv_b = _cols_kv(full['w_kv_up']).astype(BF16)
    wglu_b = full['w_glu'].astype(BF16)
    wo_b = full['w_out'].astype(BF16)
    pre_w, post_w = wts['pre_norm_w'], wts['post_norm_w']
    qw, kvw, aw, sw = wts['q_norm_w'], wts['kv_norm_w'], wts['attn_out_norm_w'], wts['ssm_out_norm_w']
    bglu, dvec = wts['b_glu'], wts['ssm_d']

    lseg = lp // N_SEG
    pos = _row_position(jnp.arange(lp, dtype=jnp.int32), lseg)
    inv = ROPE_THETA ** (-jnp.arange(HALF_ROPE, dtype=F32) / HALF_ROPE)
    ang = pos.astype(F32)[:, None] * inv[None, :]
    cos, sin = jnp.cos(ang), jnp.sin(ang)
    cos8, sin8 = jnp.tile(cos, (1, HEADS)), jnp.tile(sin, (1, HEADS))
    c32 = jnp.concatenate([cos, cos], axis=1)
    s32 = jnp.concatenate([-sin, sin], axis=1)
    p32 = jnp.asarray(np.roll(np.eye(QK_ROPE, dtype=np.float32), HALF_ROPE, axis=1))
    sum8 = jnp.asarray(np.tile(np.eye(QK_ROPE, dtype=np.float32), (HEADS, 1)))
    head_sum = jnp.asarray(np.repeat(np.eye(HEADS, dtype=np.float32), V_HEAD, axis=0))

    ng = 2 * N_GROUPS
    a_re3 = wts['ssm_a_re'].reshape(ng, 1, SSM_STATE)
    a_im3 = wts['ssm_a_im'].reshape(ng, 1, SSM_STATE)
    ldt3 = wts['ssm_log_dt'].reshape(ng, 1, 1)
    bt_re = wts['ssm_b_re'].reshape(2, N_GROUPS, SSM_STATE, SSM_GROUP).transpose(0, 1, 3, 2).reshape(
        ng, SSM_GROUP, SSM_STATE)
    bt_im = wts['ssm_b_im'].reshape(2, N_GROUPS, SSM_STATE, SSM_GROUP).transpose(0, 1, 3, 2).reshape(
        ng, SSM_GROUP, SSM_STATE)
    c_re = wts['ssm_c_re'].reshape(ng, SSM_GROUP, SSM_STATE)
    c_im = wts['ssm_c_im'].reshape(ng, SSM_GROUP, SSM_STATE)
    abar_re, abar_im, bbt_re, bbt_im = _ssm_disc(a_re3, a_im3, ldt3, bt_re, bt_im)

    def direction(t, d):
        return t[d * N_GROUPS:(d + 1) * N_GROUPS]

    def slab(t, d, sign=1.0):
        return jnp.broadcast_to(sign * direction(t, d).reshape(1, N_STATES), (N_SEG, N_STATES))

    wb = [jnp.concatenate([_block_diag(direction(bbt_re, d)), _block_diag(direction(bbt_im, d))],
                          axis=1).astype(BF16) for d in range(2)]
    ct = [jnp.concatenate([_block_diag(direction(c_re, d)), _block_diag(-direction(c_im, d))],
                          axis=1).astype(BF16) for d in range(2)]
    wb_all = jnp.concatenate(wb, axis=1)
    ct_all = jnp.concatenate(ct, axis=1)

    def to_rows(a):
        return a.reshape(N_SEG, lseg, a.shape[-1]).transpose(1, 0, 2).reshape(lp, a.shape[-1])

    def to_tokens(a):
        return a.reshape(lseg, N_SEG, a.shape[-1]).transpose(1, 0, 2).reshape(lp, a.shape[-1])

    pad = jnp.zeros((lp - n_valid, D_MODEL), F32)
    h = to_rows(jnp.concatenate([full['meta_tokens'], x[0], pad], axis=0))
    tgt = to_rows(jnp.concatenate([jnp.zeros((N_META, D_MODEL), F32), loss_target[0], pad], axis=0))

    ql, kvl, ag, su, sg, kr = _inproj(h, pre_w, w_in_b, tr)
    qn_b, qr1_b, qr2_b, kn_b, v_b, kr_b = _qkv_up(ql, kvl, kr, cos8, sin8, c32, s32, qw, kvw, wq_b, wkv_b, p32, tr)

    def heads(a, w):
        return a.reshape(lp, HEADS, w)

    q_h = jnp.concatenate([heads(qn_b, 64), heads(qr1_b, 16), heads(qr2_b, 16)], axis=-1).transpose(1, 0, 2)
    k_h = jnp.concatenate([heads(kn_b, 64), jnp.broadcast_to(kr_b[:, None, :], (lp, HEADS, QK_ROPE))],
                          axis=-1).transpose(1, 0, 2)
    v_h = heads(v_b, 64).transpose(1, 0, 2)
    vx_h = jnp.concatenate([v_h, jnp.ones((HEADS, lp, 1), BF16), jnp.zeros((HEADS, lp, LANES - V_HEAD - 1), BF16)],
                           axis=-1)
    o_h, lse = _attn_fwd(q_h, k_h, vx_h, n_valid, tq, tk)
    o_flat = o_h.transpose(1, 0, 2).reshape(lp, D_ATTN)
    ya = _attn_post(o_flat, ag, aw, tr)

    bu = [_ssm_in(f"ssm_bu{d}", su, wb[d], tr_small) for d in range(2)]
    xs = []
    for d in range(2):
        xs += _scan(f"scan{d}", bu[d][0], bu[d][1], slab(abar_re, d), slab(abar_im, d), forward=(d == 0))
    ypre = _ssm_out("ssm_y", xs, ct_all, su, dvec, tr_small)
    glu, ysn = _ssm_post(ypre, sg, wglu_b, bglu, sw, tr)

    dy, dout, loss, d_post = _out_loss(ya, ysn, h, tgt, wo_b, post_w, n_valid, tr_small)

    do_flat, dag, dysn, delta8, d_wo, d_aw = _out_bwd(dy, ya, ysn, o_flat, ag, wo_b, aw, head_sum, tr_small)
    dypre, dsg, d_wglu, d_bglu, d_sw, d_dvec = _ssm_post_bwd(dysn, glu, sg, ypre, su, wglu_b, sw, dvec, tr_small)

    gs, d_ct, d_wb, d_a8 = [], [], [], []
    for d in range(2):
        dx_re, dx_im = _ssm_in(f"ssm_dx{d}", dypre, ct[d], tr_small)
        d_ct.append(_ssm_wgrad(f"ssm_dc{d}", dypre, xs[2 * d], xs[2 * d + 1], tr_small))
        g_re, g_im, da_re, da_im = _scan(f"scan_adj{d}", dx_re, dx_im, slab(abar_re, d), slab(abar_im, d, -1.0),
                                         forward=(d != 0), states=(xs[2 * d], xs[2 * d + 1]))
        gs += [g_re, g_im]
        d_a8.append((da_re, da_im))
        d_wb.append(_ssm_wgrad(f"ssm_db{d}", su, g_re, g_im, tr_small))
    du = _ssm_out("ssm_du", gs, wb_all, dypre, dvec, tr_small)

    do_h = do_flat.astype(BF16).reshape(lp, HEADS, V_HEAD).transpose(1, 0, 2)
    dq_h, dk_h, dv_h = _attn_bwd(q_h, k_h, v_h, do_h, lse.reshape(HEADS, lp // tq, 1, tq),
                                 delta8.T.reshape(HEADS, lp // tq, 1, tq), tq, tk)
    dq_t = dq_h.transpose(1, 0, 2)
    dk_t = dk_h.transpose(1, 0, 2)
    dqn = dq_t[:, :, :64].reshape(lp, 512)
    dr1 = dq_t[:, :, 64:80].reshape(lp, 128)
    dr2 = dq_t[:, :, 80:96].reshape(lp, 128)
    dkn = dk_t[:, :, :64].reshape(lp, 512)
    dkr8 = dk_t[:, :, 64:].reshape(lp, HEADS * QK_ROPE)
    dvf = dv_h.transpose(1, 0, 2).reshape(lp, 512)
    dql, dkvl, dkrr, d_wq, d_wkv, d_qw, d_kvw = _qkv_up_bwd(
        dqn, dr1, dr2, dkn, dvf, dkr8, ql, kvl, cos8, sin8, c32, s32, qw, kvw, wq_b, wkv_b, p32, sum8, tr_small)
    dh, d_win, d_pre = _inproj_bwd(dql, dkvl, dag, du, dsg, dkrr, h, dout, pre_w, w_in_b, tr_small)
    dh = to_tokens(dh)

    def seg_sums(t):
        return t.reshape(N_SEG, N_GROUPS, SSM_STATE).transpose(1, 0, 2)

    da8_re = jnp.concatenate([seg_sums(d_a8[d][0]) for d in range(2)], axis=0)
    da8_im = jnp.concatenate([seg_sums(d_a8[d][1]) for d in range(2)], axis=0)
    dbb_re = jnp.concatenate([_diag_blocks(d_wb[d][0]) for d in range(2)], axis=0)
    dbb_im = jnp.concatenate([_diag_blocks(d_wb[d][1]) for d in range(2)], axis=0)
    g_are, g_aim, g_ldt, g_bt_re, g_bt_im = _ssm_disc_bwd(a_re3, a_im3, ldt3, bt_re, bt_im, da8_re, da8_im,
                                                          dbb_re, dbb_im)
    g_c_re = jnp.concatenate([_diag_blocks(d_ct[d][0]) for d in range(2)], axis=0)
    g_c_im = jnp.concatenate([-_diag_blocks(d_ct[d][1]) for d in range(2)], axis=0)

    def b_layout(t):
        return t.reshape(2, N_GROUPS, SSM_GROUP, SSM_STATE).transpose(0, 1, 3, 2)

    local = {
        'meta_tokens': dh[:N_META],
        'pre_norm_w': d_pre, 'post_norm_w': d_post,
        'w_in': _cols_in_inv(d_win[:, :D_IN]),
        'q_norm_w': d_qw, 'w_q_up': _cols_q_inv(d_wq),
        'kv_norm_w': d_kvw, 'w_kv_up': _cols_kv_inv(d_wkv),
        'attn_out_norm_w': d_aw,
        'ssm_a_re': g_are, 'ssm_a_im': g_aim, 'ssm_log_dt': g_ldt,
        'ssm_b_re': b_layout(g_bt_re), 'ssm_b_im': b_layout(g_bt_im), 'ssm_c_re': g_c_re, 'ssm_c_im': g_c_im,
        'ssm_d': d_dvec, 'w_glu': d_wglu, 'b_glu': d_bglu, 'ssm_out_norm_w': d_sw, 'w_out': d_wo,
    }

    order = SHARDED + [n for n in WEIGHTS if n not in SHARDED]
    shapes = [wts[n].shape for n in order]
    used_rows = sum(_n_rows(s) for s in shapes)
    tr_adam = 512
    total_rows = -(-used_rows // tr_adam) * tr_adam
    send = []
    for dev in range(N_DEV):
        arrs = [_shard_views(n, local[n])[dev] if n in SHARDED else local[n] for n in order]
        send.append(_pack(arrs, total_rows))
    recv = _exchange("exchange_grads", jnp.stack(send), all_to_all=True)
    packed = [_pack([src[n] for n in order], total_rows) for src in (wts, moms, vels)]
    g_p, d_p, m_p, v_p = _adamw(recv, *packed, tr_adam)
    grads, deltas, new_m, new_v = (dict(zip(order, _unpack(b, shapes))) for b in (g_p, d_p, m_p, v_p))

    loss = lax.psum(loss[0, 0], ("x", "y", "c"))
    grad_x = dh[N_META:n_valid][None]
    return (loss, grad_x, *[grads[n] for n in WEIGHTS], *[deltas[n] for n in WEIGHTS],
            *[new_m[n] for n in WEIGHTS], *[new_v[n] for n in WEIGHTS])


def kernel(x, meta_tokens, pre_norm_w, post_norm_w, w_in, q_norm_w, w_q_up, kv_norm_w, w_kv_up, attn_out_norm_w, ssm_a_re, ssm_a_im, ssm_log_dt, ssm_b_re, ssm_b_im, ssm_c_re, ssm_c_im, ssm_d, w_glu, b_glu, ssm_out_norm_w, w_out, loss_target, m_meta_tokens, m_pre_norm_w, m_post_norm_w, m_w_in, m_q_norm_w, m_w_q_up, m_kv_norm_w, m_w_kv_up, m_attn_out_norm_w, m_ssm_a_re, m_ssm_a_im, m_ssm_log_dt, m_ssm_b_re, m_ssm_b_im, m_ssm_c_re, m_ssm_c_im, m_ssm_d, m_w_glu, m_b_glu, m_ssm_out_norm_w, m_w_out, v_meta_tokens, v_pre_norm_w, v_post_norm_w, v_w_in, v_q_norm_w, v_w_q_up, v_kv_norm_w, v_w_kv_up, v_attn_out_norm_w, v_ssm_a_re, v_ssm_a_im, v_ssm_log_dt, v_ssm_b_re, v_ssm_b_im, v_ssm_c_re, v_ssm_c_im, v_ssm_d, v_w_glu, v_b_glu, v_ssm_out_norm_w, v_w_out):
    wts = dict(zip(WEIGHTS, (meta_tokens, pre_norm_w, post_norm_w, w_in, q_norm_w, w_q_up, kv_norm_w, w_kv_up,
                             attn_out_norm_w, ssm_a_re, ssm_a_im, ssm_log_dt, ssm_b_re, ssm_b_im, ssm_c_re,
                             ssm_c_im, ssm_d, w_glu, b_glu, ssm_out_norm_w, w_out)))
    moms = dict(zip(WEIGHTS, (m_meta_tokens, m_pre_norm_w, m_post_norm_w, m_w_in, m_q_norm_w, m_w_q_up,
                              m_kv_norm_w, m_w_kv_up, m_attn_out_norm_w, m_ssm_a_re, m_ssm_a_im, m_ssm_log_dt,
                              m_ssm_b_re, m_ssm_b_im, m_ssm_c_re, m_ssm_c_im, m_ssm_d, m_w_glu, m_b_glu,
                              m_ssm_out_norm_w, m_w_out)))
    vels = dict(zip(WEIGHTS, (v_meta_tokens, v_pre_norm_w, v_post_norm_w, v_w_in, v_q_norm_w, v_w_q_up,
                              v_kv_norm_w, v_w_kv_up, v_attn_out_norm_w, v_ssm_a_re, v_ssm_a_im, v_ssm_log_dt,
                              v_ssm_b_re, v_ssm_b_im, v_ssm_c_re, v_ssm_c_im, v_ssm_d, v_w_glu, v_b_glu,
                              v_ssm_out_norm_w, v_w_out)))
    return _step(x, loss_target, wts, moms, vels)
```

```python
import functools
import math

import numpy as np
import jax
import jax.numpy as jnp
from jax import lax
from jax.experimental import pallas as pl
from jax.experimental.pallas import tpu as pltpu

F32 = jnp.float32
BF16 = jnp.bfloat16

D_MODEL = 1024
N_META = 16
EPS = 1e-6
HEADS = 8
QK_NOPE = 64
QK_ROPE = 32
HALF_ROPE = QK_ROPE // 2
QK_DIM = QK_NOPE + QK_ROPE
V_HEAD = 64
Q_LORA = 256
KV_LORA = 128
D_ATTN = HEADS * V_HEAD
D_SSM = 512
SSM_GROUP = 16
N_GROUPS = D_SSM // SSM_GROUP
SSM_STATE = 64
N_STATES = N_GROUPS * SSM_STATE
ROPE_THETA = 10000.0
D_IN = Q_LORA + KV_LORA + QK_ROPE + D_ATTN + 2 * D_SSM
D_IN_PAD = 2048
N_DEV = 8
N_SEG = 8
COL_BLK = 512
LANES = 128

ADAM_LR = 0.001
ADAM_B1 = 0.9
ADAM_B2 = 0.999
ADAM_EPS = 1e-08
ADAM_WD = 0.01
ADAM_STEP = 10

VMEM_LIMIT_V7X = 56 * 1024 * 1024
LOG2E = 1.0 / math.log(2.0)
Q_SCALE = LOG2E / math.sqrt(QK_DIM)
ATTN_UNROLL = 4
ATTN_BWD_UNROLL = 2

WEIGHTS = ['meta_tokens', 'pre_norm_w', 'post_norm_w', 'w_in', 'q_norm_w', 'w_q_up', 'kv_norm_w', 'w_kv_up',
           'attn_out_norm_w', 'ssm_a_re', 'ssm_a_im', 'ssm_log_dt', 'ssm_b_re', 'ssm_b_im', 'ssm_c_re', 'ssm_c_im',
           'ssm_d', 'w_glu', 'b_glu', 'ssm_out_norm_w', 'w_out']
SHARDED = ['w_in', 'w_q_up', 'w_kv_up', 'w_glu', 'w_out', 'meta_tokens']

def _cols_in(w):
    return jnp.concatenate([w[:, 0:384], w[:, 416:D_IN], w[:, 384:416]], axis=1)


def _cols_in_inv(w):
    return jnp.concatenate([w[:, 0:384], w[:, D_IN - QK_ROPE:D_IN], w[:, 384:D_IN - QK_ROPE]], axis=1)


def _cols_q(w):
    t = w.reshape(w.shape[0], HEADS, QK_DIM)
    return jnp.concatenate([t[:, :, 0:64].reshape(-1, 512), t[:, :, 64:80].reshape(-1, 128),
                            t[:, :, 80:96].reshape(-1, 128)], axis=1)


def _cols_q_inv(w):
    r = w.shape[0]
    return jnp.concatenate([w[:, 0:512].reshape(r, HEADS, 64), w[:, 512:640].reshape(r, HEADS, 16),
                            w[:, 640:768].reshape(r, HEADS, 16)], axis=2).reshape(r, HEADS * QK_DIM)


def _cols_kv(w):
    t = w.reshape(w.shape[0], HEADS, 128)
    return jnp.concatenate([t[:, :, 0:64].reshape(-1, 512), t[:, :, 64:128].reshape(-1, 512)], axis=1)


def _cols_kv_inv(w):
    r = w.shape[0]
    return jnp.concatenate([w[:, 0:512].reshape(r, HEADS, 64), w[:, 512:1024].reshape(r, HEADS, 64)],
                           axis=2).reshape(r, HEADS * 128)


def _pick(n, cands):
    for c in cands:
        if n % c == 0:
            return c
    raise ValueError(f"no tile for {n}")


def _cparams(*sem):
    return pltpu.CompilerParams(dimension_semantics=sem, vmem_limit_bytes=VMEM_LIMIT_V7X)


def _mm(a, b):
    return jnp.dot(a.astype(BF16), b.astype(BF16), preferred_element_type=F32)


def _mm_nt(a, b):
    return lax.dot_general(a.astype(BF16), b.astype(BF16), (((1,), (1,)), ((), ())), preferred_element_type=F32)


def _mm_tn(a, b):
    return lax.dot_general(a.astype(BF16), b.astype(BF16), (((0,), (0,)), ((), ())), preferred_element_type=F32)


def _mm_exact(a, b):
    return jnp.dot(a, b, precision=lax.Precision.HIGHEST, preferred_element_type=F32)


def _rms(x):
    return lax.rsqrt(jnp.mean(x * x, axis=-1, keepdims=True) + EPS)


def _rms_bwd(dy, x, r, w):
    xh = x * r
    g = dy * w
    dx = r * (g - xh * jnp.mean(g * xh, axis=-1, keepdims=True))
    dw = jnp.sum(dy * xh, axis=0, keepdims=True)
    return dx, dw


def _sigmoid(z):
    return 1.0 / (1.0 + jnp.exp(-z))


def _silu_and_grad(z):
    s = _sigmoid(z)
    return z * s, s * (1.0 + z * (1.0 - s))


_GELU_C = math.sqrt(2.0 / math.pi)


def _gelu_and_grad(x):
    x2 = x * x
    t = jnp.tanh(_GELU_C * (x + 0.044715 * x * x2))
    val = 0.5 * x * (1.0 + t)
    grad = 0.5 * (1.0 + t) + 0.5 * x * (1.0 - t * t) * _GELU_C * (1.0 + 3.0 * 0.044715 * x2)
    return val, grad


def _acc(ref, val, first):
    @pl.when(first)
    def _():
        ref[...] = val

    @pl.when(jnp.logical_not(first))
    def _():
        ref[...] += val


def _rows_call(name, body, tr, row_ins, full_ins, row_outs, acc_outs):
    lp = row_ins[0].shape[0]
    in_specs = [pl.BlockSpec((tr, a.shape[1]), lambda i: (i, 0)) for a in row_ins]
    in_specs += [pl.BlockSpec(a.shape, lambda i, n=a.ndim: (0,) * n) for a in full_ins]
    out_specs = [pl.BlockSpec((tr, c), lambda i: (i, 0)) for c, _ in row_outs]
    out_specs += [pl.BlockSpec(s, lambda i, n=len(s): (0,) * n) for s, _ in acc_outs]
    out_shape = [jax.ShapeDtypeStruct((lp, c), dt) for c, dt in row_outs]
    out_shape += [jax.ShapeDtypeStruct(s, dt) for s, dt in acc_outs]
    return pl.pallas_call(
        body, name=name, grid=(lp // tr,), in_specs=in_specs, out_specs=out_specs, out_shape=out_shape,
        compiler_params=_cparams("arbitrary"))(*row_ins, *full_ins)


def _inproj(h, pre_w, w_in_b, tr):
    def body(h_ref, pw_ref, w_ref, ql, kvl, ag, su, sg, kr):
        x = h_ref[...]
        xn = x * _rms(x) * pw_ref[...]
        pr = _mm(xn, w_ref[...])
        ql[...] = pr[:, 0:256]
        kvl[...] = pr[:, 256:384]
        ag[...] = pr[:, 384:896]
        su[...] = pr[:, 896:1408]
        sg[...] = pr[:, 1408:1920]
        kr[...] = pr[:, 1920:1952]

    return _rows_call("inproj", body, tr, [h], [pre_w, w_in_b],
                      [(256, F32), (128, F32), (512, F32), (512, F32), (512, F32), (32, F32)], [])


def _qkv_up(ql, kvl, kr, cos8, sin8, c32, s32, qw, kvw, wq_b, wkv_b, p32, tr):
    def body(ql_ref, kvl_ref, kr_ref, cos_ref, sin_ref, c32_ref, s32_ref, qw_ref, kvw_ref, wq_ref, wkv_ref, p_ref,
             qn_o, qr1_o, qr2_o, kn_o, v_o, kr_o):
        x = ql_ref[...]
        q = _mm(x * _rms(x) * qw_ref[...], wq_ref[...]) * Q_SCALE
        r1, r2 = q[:, 512:640], q[:, 640:768]
        cs, sn = cos_ref[...], sin_ref[...]
        qn_o[...] = q[:, 0:512].astype(BF16)
        qr1_o[...] = (r1 * cs - r2 * sn).astype(BF16)
        qr2_o[...] = (r2 * cs + r1 * sn).astype(BF16)
        x = kvl_ref[...]
        kv = _mm(x * _rms(x) * kvw_ref[...], wkv_ref[...])
        kn_o[...] = kv[:, 0:512].astype(BF16)
        v_o[...] = kv[:, 512:1024].astype(BF16)
        x = kr_ref[...]
        kr_o[...] = (x * c32_ref[...] + _mm_exact(x, p_ref[...]) * s32_ref[...]).astype(BF16)

    return _rows_call("qkv_up", body, tr, [ql, kvl, kr, cos8, sin8, c32, s32], [qw, kvw, wq_b, wkv_b, p32],
                      [(512, BF16), (128, BF16), (128, BF16), (512, BF16), (512, BF16), (32, BF16)], [])


def _row_position(row, lseg):
    return (row & (N_SEG - 1)) * lseg + (row >> 3)


def _first_padded_tile(n_valid, lp, tile):
    lseg = lp // N_SEG
    t0 = n_valid - (N_SEG - 1) * lseg
    return (t0 * N_SEG + N_SEG - 1) // tile if n_valid < lp else lp // tile


def _attn_fwd(q, k, vx, n_valid, tq, tk):
    lp = q.shape[1]
    nq, nk = lp // tq, lp // tk
    lseg = lp // N_SEG
    n_plain = max(0, min(nk, _first_padded_tile(n_valid, lp, tk)))

    def body(q_ref, k_ref, v_ref, o_ref, lse_ref, m_s, acc_s):
        m_s[...] = jnp.full(m_s.shape, -1e30, F32)
        acc_s[...] = jnp.zeros(acc_s.shape, F32)
        qq = q_ref[0]

        def chunk(c, padded):
            r0 = pl.multiple_of(c * tk, tk)
            s = _mm_nt(qq, k_ref[0, pl.ds(r0, tk), :])
            if padded:
                col = r0 + lax.broadcasted_iota(jnp.int32, (tq, tk), 1)
                s = jnp.where(_row_position(col, lseg) < n_valid, s, -1e30)
            m_old = m_s[...]
            m_new = jnp.maximum(m_old, jnp.max(s, axis=1, keepdims=True))
            p = jnp.exp2(s - m_new)
            acc_s[...] = jnp.exp2(m_old - m_new) * acc_s[...] + _mm(p, v_ref[0, pl.ds(r0, tk), :])
            m_s[...] = m_new

        def plain(c, carry):
            chunk(c, False)
            return carry

        n_loop = n_plain - n_plain % ATTN_UNROLL
        if n_loop:
            lax.fori_loop(0, n_loop, plain, 0, unroll=ATTN_UNROLL)
        for c in range(n_loop, nk):
            chunk(c, c >= n_plain)
        acc = acc_s[...]
        l = acc[:, V_HEAD:V_HEAD + 1]
        o_ref[0] = acc[:, :V_HEAD] / l
        lse_ref[0] = m_s[...] + jnp.log2(l)

    return pl.pallas_call(
        body, name="attn_fwd", grid=(HEADS, nq),
        in_specs=[pl.BlockSpec((1, tq, QK_DIM), lambda h, i: (h, i, 0)),
                  pl.BlockSpec((1, lp, QK_DIM), lambda h, i: (h, 0, 0)),
                  pl.BlockSpec((1, lp, LANES), lambda h, i: (h, 0, 0))],
        out_specs=[pl.BlockSpec((1, tq, V_HEAD), lambda h, i: (h, i, 0)),
                   pl.BlockSpec((1, tq, 1), lambda h, i: (h, i, 0))],
        out_shape=[jax.ShapeDtypeStruct((HEADS, lp, V_HEAD), F32), jax.ShapeDtypeStruct((HEADS, lp, 1), F32)],
        scratch_shapes=[pltpu.VMEM((tq, 1), F32), pltpu.VMEM((tq, LANES), F32)],
        compiler_params=_cparams("arbitrary", "arbitrary"))(q, k, vx)


def _attn_post(o_flat, ag, aw, tr):
    def body(o_ref, g_ref, w_ref, ya):
        t = o_ref[...] * _silu_and_grad(g_ref[...])[0]
        ya[...] = t * _rms(t) * w_ref[...]

    return _rows_call("attn_post", body, tr, [o_flat, ag], [aw], [(512, F32)], [])[0]


def _scan_tiles(lp):
    lseg = lp // N_SEG
    tt = _pick(lseg, [104, 48, 32, 16, 8, 4, 2, 1])
    return lseg, tt, lseg // tt


def _cmul(ar, ai, br, bi):
    return ar * br - ai * bi, ar * bi + ai * br


N_COL_BLK = N_STATES // COL_BLK
CH_BLK = D_SSM // N_COL_BLK


def _scan_steps(tt, forward, bre_ref, bim_ref, ar, ai, carry, visit):
    def step(s, c):
        r0 = pl.multiple_of((s if forward else tt - 1 - s) * N_SEG, N_SEG)
        pr, pi = _cmul(ar, ai, c[0], c[1])
        xr = pr + bre_ref[pl.ds(r0, N_SEG), :]
        xi = pi + bim_ref[pl.ds(r0, N_SEG), :]
        return (xr, xi) + tuple(visit(r0, (xr, xi), (c[0], c[1]), c[2:]))

    return lax.fori_loop(0, tt, step, carry, unroll=4 if tt % 4 == 0 else 1)


def _segment_starts(lseg, forward, ar, ai, ere_ref, eim_ref, s_re, s_im):
    a1r, a1i = ar[0:1, :], ai[0:1, :]
    pr, pi = jnp.ones_like(a1r), jnp.zeros_like(a1i)
    br, bi = a1r, a1i
    n = lseg
    while n:
        if n & 1:
            pr, pi = _cmul(pr, pi, br, bi)
        n >>= 1
        if n:
            br, bi = _cmul(br, bi, br, bi)
    cr, ci = jnp.zeros_like(a1r), jnp.zeros_like(a1i)
    for j in (range(N_SEG) if forward else range(N_SEG - 1, -1, -1)):
        s_re[j:j + 1, :] = cr
        s_im[j:j + 1, :] = ci
        nr, ni = _cmul(pr, pi, cr, ci)
        cr = nr + ere_ref[j:j + 1, :]
        ci = ni + eim_ref[j:j + 1, :]


def _scan_specs(lp, forward):
    lseg, tt, nt = _scan_tiles(lp)

    def tile(t):
        return t if forward else nt - 1 - t

    rows = lambda w: pl.BlockSpec((tt * N_SEG, w), lambda cb, t: (tile(t), cb))
    proj = pl.BlockSpec((1, CH_BLK, COL_BLK), lambda cb, t: (cb, 0, 0))
    slab = pl.BlockSpec((N_SEG, COL_BLK), lambda cb, t: (0, cb))
    return lseg, tt, nt, rows, proj, slab


def _scan_ends(name, urows, wre4, wim4, ar8, ai8, forward):
    lp = urows.shape[0]
    lseg, tt, nt, rows, proj, slab = _scan_specs(lp, forward)

    def body(u_ref, wre_ref, wim_ref, ar_ref, ai_ref, ere_o, eim_o, bre_s, bim_s, cr_s, ci_s):
        t = pl.program_id(1)

        @pl.when(t == 0)
        def _():
            cr_s[...] = jnp.zeros(cr_s.shape, F32)
            ci_s[...] = jnp.zeros(ci_s.shape, F32)

        u = u_ref[...]
        bre_s[...] = _mm(u, wre_ref[0])
        bim_s[...] = _mm(u, wim_ref[0])
        cr, ci = _scan_steps(tt, forward, bre_s, bim_s, ar_ref[...], ai_ref[...], (cr_s[...], ci_s[...]),
                             lambda r0, x, x_prev, extra: ())
        cr_s[...] = cr
        ci_s[...] = ci

        @pl.when(t == nt - 1)
        def _():
            ere_o[...] = cr
            eim_o[...] = ci

    return pl.pallas_call(
        body, name=name, grid=(N_COL_BLK, nt), in_specs=[rows(CH_BLK), proj, proj, slab, slab],
        out_specs=[slab, slab], out_shape=[jax.ShapeDtypeStruct((N_SEG, N_STATES), F32)] * 2,
        scratch_shapes=[pltpu.VMEM((tt * N_SEG, COL_BLK), F32)] * 2 + [pltpu.VMEM((N_SEG, COL_BLK), F32)] * 2,
        compiler_params=_cparams("arbitrary", "arbitrary"))(urows, wre4, wim4, ar8, ai8)


def _scan_fwd(name, urows, wre4, wim4, ar8, ai8, ere, eim, cre4, cim4, forward):
    lp = urows.shape[0]
    lseg, tt, nt, rows, proj, slab = _scan_specs(lp, forward)

    def body(u_ref, wre_ref, wim_ref, ar_ref, ai_ref, ere_ref, eim_ref, cre_ref, cim_ref,
             xre_o, xim_o, y_o, bre_s, bim_s, cr_s, ci_s):
        ar, ai = ar_ref[...], ai_ref[...]

        @pl.when(pl.program_id(1) == 0)
        def _():
            _segment_starts(lseg, forward, ar, ai, ere_ref, eim_ref, cr_s, ci_s)

        u = u_ref[...]
        bre_s[...] = _mm(u, wre_ref[0])
        bim_s[...] = _mm(u, wim_ref[0])

        def visit(r0, x, x_prev, extra):
            xre_o[pl.ds(r0, N_SEG), :] = x[0]
            xim_o[pl.ds(r0, N_SEG), :] = x[1]
            return ()

        cr, ci = _scan_steps(tt, forward, bre_s, bim_s, ar, ai, (cr_s[...], ci_s[...]), visit)
        cr_s[...] = cr
        ci_s[...] = ci
        y_o[...] = _mm_nt(xre_o[...], cre_ref[0]) + _mm_nt(xim_o[...], cim_ref[0])

    return pl.pallas_call(
        body, name=name, grid=(N_COL_BLK, nt),
        in_specs=[rows(CH_BLK), proj, proj, slab, slab, slab, slab, proj, proj],
        out_specs=[rows(COL_BLK), rows(COL_BLK), rows(CH_BLK)],
        out_shape=[jax.ShapeDtypeStruct((lp, N_STATES), F32)] * 2 + [jax.ShapeDtypeStruct((lp, D_SSM), F32)],
        scratch_shapes=[pltpu.VMEM((tt * N_SEG, COL_BLK), F32)] * 2 + [pltpu.VMEM((N_SEG, COL_BLK), F32)] * 2,
        compiler_params=_cparams("arbitrary", "arbitrary"))(urows, wre4, wim4, ar8, ai8, ere, eim, cre4, cim4)


def _scan_bwd(name, dyrows, cre4, cim4, ar8, ai8, ere, eim, urows, wre4, wim4, xre, xim, forward):
    lp = urows.shape[0]
    lseg, tt, nt, rows, proj, slab = _scan_specs(lp, forward)

    def body(dy_ref, cre_ref, cim_ref, ar_ref, ai_ref, ere_ref, eim_ref, u_ref, wre_ref, wim_ref, xre_ref, xim_ref,
             du_o, dwre_o, dwim_o, dcre_o, dcim_o, dare_o, daim_o, bre_s, bim_s, gre_s, gim_s, cr_s, ci_s):
        t = pl.program_id(1)
        ar, ai = ar_ref[...], ai_ref[...]

        @pl.when(t == 0)
        def _():
            _segment_starts(lseg, forward, ar, ai, ere_ref, eim_ref, cr_s, ci_s)
            dare_o[...] = jnp.zeros(dare_o.shape, F32)
            daim_o[...] = jnp.zeros(daim_o.shape, F32)

        dy = dy_ref[...]
        bre_s[...] = _mm(dy, cre_ref[0])
        bim_s[...] = _mm(dy, cim_ref[0])

        def visit(r0, g, g_prev, sums):
            gre_s[pl.ds(r0, N_SEG), :] = g[0]
            gim_s[pl.ds(r0, N_SEG), :] = g[1]
            fr = xre_ref[pl.ds(r0, N_SEG), :]
            fi = xim_ref[pl.ds(r0, N_SEG), :]
            pr, pi = g_prev
            return sums[0] + fr * pr + fi * pi, sums[1] + fr * pi - fi * pr

        out = _scan_steps(tt, forward, bre_s, bim_s, ar, ai, (cr_s[...], ci_s[...], dare_o[...], daim_o[...]), visit)
        cr_s[...] = out[0]
        ci_s[...] = out[1]
        dare_o[...] = out[2]
        daim_o[...] = out[3]
        gre, gim = gre_s[...], gim_s[...]
        du_o[...] = _mm_nt(gre, wre_ref[0]) + _mm_nt(gim, wim_ref[0])
        u = u_ref[...]
        first = t == 0
        _acc(dwre_o, _mm_tn(u, gre)[None], first)
        _acc(dwim_o, _mm_tn(u, gim)[None], first)
        _acc(dcre_o, _mm_tn(dy, xre_ref[...])[None], first)
        _acc(dcim_o, _mm_tn(dy, xim_ref[...])[None], first)

    big = pltpu.VMEM((tt * N_SEG, COL_BLK), F32)
    small = pltpu.VMEM((N_SEG, COL_BLK), F32)
    return pl.pallas_call(
        body, name=name, grid=(N_COL_BLK, nt),
        in_specs=[rows(CH_BLK), proj, proj, slab, slab, slab, slab, rows(CH_BLK), proj, proj,
                  rows(COL_BLK), rows(COL_BLK)],
        out_specs=[rows(CH_BLK), proj, proj, proj, proj, slab, slab],
        out_shape=[jax.ShapeDtypeStruct((lp, D_SSM), F32)]
        + [jax.ShapeDtypeStruct((N_COL_BLK, CH_BLK, COL_BLK), F32)] * 4
        + [jax.ShapeDtypeStruct((N_SEG, N_STATES), F32)] * 2,
        scratch_shapes=[big, big, big, big, small, small],
        compiler_params=_cparams("arbitrary", "arbitrary"))(
            dyrows, cre4, cim4, ar8, ai8, ere, eim, urows, wre4, wim4, xre, xim)


def _ssm_post(yf, yb, u, sg, wglu_b, bglu, sw, dvec, tr):
    def body(yf_ref, yb_ref, u_ref, g_ref, w_ref, b_ref, sw_ref, d_ref, ypre_o, glu_o, ysn_o):
        ypre = yf_ref[...] + yb_ref[...] + d_ref[...] * u_ref[...]
        ypre_o[...] = ypre
        glu = _mm(_gelu_and_grad(ypre)[0], w_ref[...]) + b_ref[...]
        glu_o[...] = glu
        t = glu[:, :D_SSM] * _sigmoid(glu[:, D_SSM:]) * _silu_and_grad(g_ref[...])[0]
        ysn_o[...] = t * _rms(t) * sw_ref[...]

    return _rows_call("ssm_post", body, tr, [yf, yb, u, sg], [wglu_b, bglu, sw, dvec],
                      [(512, F32), (1024, F32), (512, F32)], [])


def _out_loss(ya, ysn, h, tgt, wo_b, post_w, n_valid, tr):
    lseg = h.shape[0] // N_SEG

    def body(ya_ref, ys_ref, h_ref, t_ref, w_ref, pw_ref, dy_o, dout_o, loss_o, dpw_o):
        i = pl.program_id(0)
        y = _mm(ya_ref[...], w_ref[0:D_ATTN, :]) + _mm(ys_ref[...], w_ref[D_ATTN:, :])
        r = _rms(y)
        pw = pw_ref[...]
        out = h_ref[...] + y * r * pw
        pos = _row_position(i * tr + lax.broadcasted_iota(jnp.int32, (tr, 1), 0), lseg)
        valid = jnp.logical_and(pos >= N_META, pos < n_valid)
        diff = jnp.where(valid, out - t_ref[...], 0.0)
        dout = diff * (1.0 / D_MODEL)
        dy, dpw = _rms_bwd(dout, y, r, pw)
        dy_o[...] = dy
        dout_o[...] = dout
        _acc(loss_o, 0.5 * jnp.sum(jnp.sum(diff * diff, axis=1, keepdims=True), axis=0, keepdims=True)
             * (1.0 / D_MODEL), i == 0)
        _acc(dpw_o, dpw, i == 0)

    return _rows_call("out_loss", body, tr, [ya, ysn, h, tgt], [wo_b, post_w], [(1024, F32), (1024, F32)],
                      [((1, 1), F32), ((1, D_MODEL), F32)])


def _out_bwd(dy, ya, ysn, o_flat, ag, wo_b, aw, head_sum, tr):
    def body(dy_ref, ya_ref, ys_ref, o_ref, g_ref, w_ref, aw_ref, hs_ref, do_o, dag_o, dysn_o, dl_o, dwo_o, daw_o):
        i = pl.program_id(0)
        dy = dy_ref[...]
        dcat = _mm_nt(dy, w_ref[...])
        cat = jnp.concatenate([ya_ref[...], ys_ref[...]], axis=1)
        _acc(dwo_o, _mm_tn(cat, dy), i == 0)
        dysn_o[...] = dcat[:, D_ATTN:]
        o = o_ref[...]
        sl, dsl = _silu_and_grad(g_ref[...])
        t = o * sl
        dt, daw = _rms_bwd(dcat[:, :D_ATTN], t, _rms(t), aw_ref[...])
        _acc(daw_o, daw, i == 0)
        do = dt * sl
        do_o[...] = do
        dag_o[...] = dt * o * dsl
        dl_o[...] = _mm_exact(do * o, hs_ref[...])

    return _rows_call("out_bwd", body, tr, [dy, ya, ysn, o_flat, ag], [wo_b, aw, head_sum],
                      [(512, F32), (512, F32), (512, F32), (HEADS, F32)],
                      [((D_MODEL, D_MODEL), F32), ((1, D_ATTN), F32)])


def _ssm_post_bwd(dysn, glu, sg, ypre, u, wglu_b, sw, dvec, tr):
    def body(d_ref, glu_ref, sg_ref, y_ref, u_ref, w_ref, sw_ref, dv_ref,
             dyp_o, dsg_o, dwg_o, dbg_o, dsw_o, dd_o):
        i = pl.program_id(0)
        glu = glu_ref[...]
        a, b = glu[:, :D_SSM], glu[:, D_SSM:]
        sb = _sigmoid(b)
        ys = a * sb
        sl, dsl = _silu_and_grad(sg_ref[...])
        t = ys * sl
        dt, dsw = _rms_bwd(d_ref[...], t, _rms(t), sw_ref[...])
        _acc(dsw_o, dsw, i == 0)
        dsg_o[...] = dt * ys * dsl
        dys = dt * sl
        dglu = jnp.concatenate([dys * sb, dys * a * sb * (1.0 - sb)], axis=1)
        _acc(dbg_o, jnp.sum(dglu, axis=0, keepdims=True), i == 0)
        gel, dgel = _gelu_and_grad(y_ref[...])
        _acc(dwg_o, _mm_tn(gel, dglu), i == 0)
        dyp = _mm_nt(dglu, w_ref[...]) * dgel
        dyp_o[...] = dyp
        _acc(dd_o, jnp.sum(dyp * u_ref[...], axis=0, keepdims=True), i == 0)

    return _rows_call("ssm_post_bwd", body, tr, [dysn, glu, sg, ypre, u], [wglu_b, sw, dvec],
                      [(512, F32), (512, F32)],
                      [((D_SSM, 2 * D_SSM), F32), ((1, 2 * D_SSM), F32), ((1, D_SSM), F32), ((1, D_SSM), F32)])


def _attn_bwd(q, k, v, do, lse_t, delta_t, tq, tk):
    lp = q.shape[1]
    nq, nk = lp // tq, lp // tk
    assert lse_t.shape == (HEADS, nq, 1, tq) and delta_t.shape == (HEADS, nq, 1, tq)

    def body(q_ref, k_ref, v_ref, do_ref, lse_ref, dl_ref, dq_o, dk_o, dv_o, dk_s, dv_s):
        @pl.when(pl.program_id(1) == 0)
        def _():
            dq_o[...] = jnp.zeros(dq_o.shape, F32)

        dk_s[...] = jnp.zeros(dk_s.shape, F32)
        dv_s[...] = jnp.zeros(dv_s.shape, F32)
        kk = k_ref[0]
        vv = v_ref[0]

        def chunk(c, carry):
            r0 = pl.multiple_of(c * tq, tq)
            qq = q_ref[0, pl.ds(r0, tq), :]
            dd = do_ref[0, pl.ds(r0, tq), :]
            pt = jnp.exp2(_mm_nt(kk, qq) - lse_ref[0, c])
            dv_s[...] += _mm(pt, dd)
            dst = (pt * (_mm_nt(vv, dd) - dl_ref[0, c])).astype(BF16)
            dk_s[...] += _mm(dst, qq)
            dq_o[0, pl.ds(r0, tq), :] += _mm_tn(dst, kk)
            return carry

        n_loop = nq - nq % ATTN_BWD_UNROLL
        if n_loop:
            lax.fori_loop(0, n_loop, chunk, 0, unroll=ATTN_BWD_UNROLL)
        for c in range(n_loop, nq):
            chunk(c, 0)
        dk_o[0] = dk_s[...]
        dv_o[0] = dv_s[...]

    head = lambda w: pl.BlockSpec((1, lp, w), lambda h, j: (h, 0, 0))
    kspec = lambda w: pl.BlockSpec((1, tk, w), lambda h, j: (h, j, 0))
    stat = pl.BlockSpec((1, nq, 1, tq), lambda h, j: (h, 0, 0, 0))
    return pl.pallas_call(
        body, name="attn_bwd", grid=(HEADS, nk),
        in_specs=[head(QK_DIM), kspec(QK_DIM), kspec(V_HEAD), head(V_HEAD), stat, stat],
        out_specs=[head(QK_DIM), kspec(QK_DIM), kspec(V_HEAD)],
        out_shape=[jax.ShapeDtypeStruct((HEADS, lp, QK_DIM), F32), jax.ShapeDtypeStruct((HEADS, lp, QK_DIM), F32),
                   jax.ShapeDtypeStruct((HEADS, lp, V_HEAD), F32)],
        scratch_shapes=[pltpu.VMEM((tk, QK_DIM), F32), pltpu.VMEM((tk, V_HEAD), F32)],
        compiler_params=_cparams("arbitrary", "arbitrary"))(q, k, v, do, lse_t, delta_t)


def _qkv_up_bwd(dqn, dr1, dr2, dkn, dv, dkr8, ql, kvl, cos8, sin8, c32, s32, qw, kvw, wq_b, wkv_b, p32, sum8, tr):
    def body(dqn_ref, dr1_ref, dr2_ref, dkn_ref, dv_ref, dkr_ref, ql_ref, kvl_ref, cos_ref, sin_ref, c32_ref,
             s32_ref, qw_ref, kvw_ref, wq_ref, wkv_ref, p_ref, s8_ref,
             dql_o, dkvl_o, dkrr_o, dwq_o, dwkv_o, dqw_o, dkvw_o):
        i = pl.program_id(0)
        cs, sn = cos_ref[...], sin_ref[...]
        d1, d2 = dr1_ref[...], dr2_ref[...]
        dq = jnp.concatenate([dqn_ref[...], d1 * cs + d2 * sn, d2 * cs - d1 * sn], axis=1) * (Q_SCALE / LOG2E)
        x = ql_ref[...]
        r = _rms(x)
        qw = qw_ref[...]
        _acc(dwq_o, _mm_tn(x * r * qw, dq), i == 0)
        dx, dw = _rms_bwd(_mm_nt(dq, wq_ref[...]), x, r, qw)
        dql_o[...] = dx
        _acc(dqw_o, dw, i == 0)
        dkv = jnp.concatenate([dkn_ref[...] * (1.0 / LOG2E), dv_ref[...]], axis=1)
        x = kvl_ref[...]
        r = _rms(x)
        kvw = kvw_ref[...]
        _acc(dwkv_o, _mm_tn(x * r * kvw, dkv), i == 0)
        dx, dw = _rms_bwd(_mm_nt(dkv, wkv_ref[...]), x, r, kvw)
        dkvl_o[...] = dx
        _acc(dkvw_o, dw, i == 0)
        dkr = _mm_exact(dkr_ref[...], s8_ref[...]) * (1.0 / LOG2E)
        dkrr_o[...] = dkr * c32_ref[...] + _mm_exact(dkr * s32_ref[...], p_ref[...])

    return _rows_call("qkv_up_bwd", body, tr, [dqn, dr1, dr2, dkn, dv, dkr8, ql, kvl, cos8, sin8, c32, s32],
                      [qw, kvw, wq_b, wkv_b, p32, sum8], [(256, F32), (128, F32), (32, F32)],
                      [((Q_LORA, 768), F32), ((KV_LORA, 1024), F32), ((1, Q_LORA), F32), ((1, KV_LORA), F32)])


def _inproj_bwd(dql, dkvl, dag, du_f, du_b, dypre, dsg, dkr, h, dout, pre_w, w_in_b, dvec, tr):
    def body(dql_ref, dkvl_ref, dag_ref, duf_ref, dub_ref, dyp_ref, dsg_ref, dkr_ref, h_ref, dout_ref,
             pw_ref, w_ref, dv_ref, dh_o, dwin_o, dpw_o):
        i = pl.program_id(0)
        du = duf_ref[...] + dub_ref[...] + dv_ref[...] * dyp_ref[...]
        dproj = jnp.concatenate([dql_ref[...], dkvl_ref[...], dag_ref[...], du, dsg_ref[...],
                                 dkr_ref[...], jnp.zeros((tr, D_IN_PAD - D_IN), F32)], axis=1)
        x = h_ref[...]
        r = _rms(x)
        pw = pw_ref[...]
        _acc(dwin_o, _mm_tn(x * r * pw, dproj), i == 0)
        dx, dw = _rms_bwd(_mm_nt(dproj, w_ref[...]), x, r, pw)
        _acc(dpw_o, dw, i == 0)
        dh_o[...] = dout_ref[...] + dx

    return _rows_call("inproj_bwd", body, tr, [dql, dkvl, dag, du_f, du_b, dypre, dsg, dkr, h, dout],
                      [pre_w, w_in_b, dvec], [(1024, F32)], [((D_MODEL, D_IN_PAD), F32), ((1, D_MODEL), F32)])


def _disc_terms(a_re, a_im, ldt):
    dt = jnp.exp(ldt)
    mag = jnp.exp(a_re * dt)
    th = a_im * dt
    cs, sn = jnp.cos(th), jnp.sin(th)
    abar_re, abar_im = mag * cs, mag * sn
    num_re, num_im = abar_re - 1.0, abar_im
    den = a_re * a_re + a_im * a_im
    coef_re = (num_re * a_re + num_im * a_im) / den
    coef_im = (num_im * a_re - num_re * a_im) / den
    return dt, mag, cs, sn, abar_re, abar_im, num_re, num_im, den, coef_re, coef_im


def _ssm_disc(a_re, a_im, ldt, bt_re, bt_im):
    def body(ar_ref, ai_ref, l_ref, br_ref, bi_ref, abr_o, abi_o, bbr_o, bbi_o):
        t = _disc_terms(ar_ref[...], ai_ref[...], l_ref[...])
        abr_o[...] = t[4]
        abi_o[...] = t[5]
        cr, ci = t[9], t[10]
        br, bi = br_ref[...], bi_ref[...]
        bbr_o[...] = cr * br - ci * bi
        bbi_o[...] = cr * bi + ci * br

    ng = a_re.shape[0]
    return pl.pallas_call(
        body, name="ssm_disc",
        out_shape=[jax.ShapeDtypeStruct((ng, 1, SSM_STATE), F32)] * 2
        + [jax.ShapeDtypeStruct((ng, SSM_GROUP, SSM_STATE), F32)] * 2)(a_re, a_im, ldt, bt_re, bt_im)


def _ssm_disc_bwd(a_re, a_im, ldt, bt_re, bt_im, da8_re, da8_im, dbb_re, dbb_im):
    def body(ar_ref, ai_ref, l_ref, br_ref, bi_ref, dar_ref, dai_ref, dbr_ref, dbi_ref,
             gar_o, gai_o, gl_o, gbr_o, gbi_o):
        a_re, a_im = ar_ref[...], ai_ref[...]
        dt, mag, cs, sn, abar_re, abar_im, num_re, num_im, den, cr, ci = _disc_terms(a_re, a_im, l_ref[...])
        br, bi = br_ref[...], bi_ref[...]
        dbr, dbi = dbr_ref[...], dbi_ref[...]
        gbr_o[...] = cr * dbr + ci * dbi
        gbi_o[...] = cr * dbi - ci * dbr
        dcr = jnp.sum(br * dbr + bi * dbi, axis=1, keepdims=True)
        dci = jnp.sum(br * dbi - bi * dbr, axis=1, keepdims=True)
        dnum_re = (dcr * a_re - dci * a_im) / den
        dnum_im = (dcr * a_im + dci * a_re) / den
        dden = -(dcr * cr + dci * ci) / den
        g_are = (dcr * num_re + dci * num_im) / den + dden * 2.0 * a_re
        g_aim = (dcr * num_im - dci * num_re) / den + dden * 2.0 * a_im
        d_abr = jnp.sum(dar_ref[...], axis=1, keepdims=True) + dnum_re
        d_abi = jnp.sum(dai_ref[...], axis=1, keepdims=True) + dnum_im
        dmag = d_abr * cs + d_abi * sn
        dth = d_abi * abar_re - d_abr * abar_im
        g_are = g_are + dmag * mag * dt
        g_aim = g_aim + dth * dt
        ddt = jnp.sum(dmag * mag * a_re + dth * a_im, axis=2, keepdims=True)
        gar_o[...] = g_are
        gai_o[...] = g_aim
        gl_o[...] = ddt * dt

    ng = a_re.shape[0]
    return pl.pallas_call(
        body, name="ssm_disc_bwd",
        out_shape=[jax.ShapeDtypeStruct((ng, 1, SSM_STATE), F32)] * 2 + [jax.ShapeDtypeStruct((ng, 1, 1), F32)]
        + [jax.ShapeDtypeStruct((ng, SSM_GROUP, SSM_STATE), F32)] * 2)(
            a_re, a_im, ldt, bt_re, bt_im, da8_re, da8_im, dbb_re, dbb_im)


def _exchange(name, buf, all_to_all):
    rows = buf.shape[-2]

    def body(in_ref, out_ref, send_sems, recv_sems, local_sem):
        x, y, c = lax.axis_index("x"), lax.axis_index("y"), lax.axis_index("c")
        me = 4 * x + 2 * y + c
        copies = []
        for k in range(1, N_DEV):
            px = 1 - x if (k >> 2) & 1 else x
            py = 1 - y if (k >> 1) & 1 else y
            pc = 1 - c if k & 1 else c
            src = in_ref.at[4 * px + 2 * py + pc] if all_to_all else in_ref
            copies.append(pltpu.make_async_remote_copy(
                src_ref=src, dst_ref=out_ref.at[me], send_sem=send_sems.at[k - 1], recv_sem=recv_sems.at[k - 1],
                device_id=(px, py, pc), device_id_type=pl.DeviceIdType.MESH))
        mine = pltpu.make_async_copy(in_ref.at[me] if all_to_all else in_ref, out_ref.at[me], local_sem)
        mine.start()
        for cp in copies:
            cp.start()
        for cp in copies:
            cp.wait()
        mine.wait()

    return pl.pallas_call(
        body, name=name, out_shape=jax.ShapeDtypeStruct((N_DEV, rows, LANES), F32),
        in_specs=[pl.BlockSpec(memory_space=pl.ANY)], out_specs=pl.BlockSpec(memory_space=pl.ANY),
        scratch_shapes=[pltpu.SemaphoreType.DMA((N_DEV - 1,)), pltpu.SemaphoreType.DMA((N_DEV - 1,)),
                        pltpu.SemaphoreType.DMA(())])(buf)


def _adamw(recv, w, m, v, tr):
    rows = w.shape[0]
    c1 = 1.0 - ADAM_B1 ** ADAM_STEP
    c2 = 1.0 - ADAM_B2 ** ADAM_STEP

    def body(r_ref, w_ref, m_ref, v_ref, g_o, d_o, m_o, v_o):
        g = r_ref[0]
        for k in range(1, N_DEV):
            g = g + r_ref[k]
        mm = ADAM_B1 * m_ref[...] + (1.0 - ADAM_B1) * g
        vv = ADAM_B2 * v_ref[...] + (1.0 - ADAM_B2) * (g * g)
        g_o[...] = g
        m_o[...] = mm
        v_o[...] = vv
        d_o[...] = -ADAM_LR * ((mm / c1) / (jnp.sqrt(vv / c2) + ADAM_EPS) + ADAM_WD * w_ref[...])

    spec = pl.BlockSpec((tr, LANES), lambda i: (i, 0))
    return pl.pallas_call(
        body, name="adamw", grid=(rows // tr,),
        in_specs=[pl.BlockSpec((N_DEV, tr, LANES), lambda i: (0, i, 0)), spec, spec, spec],
        out_specs=[spec] * 4, out_shape=[jax.ShapeDtypeStruct((rows, LANES), F32)] * 4,
        compiler_params=_cparams("arbitrary"))(recv, w, m, v)


def _to_rows(a):
    flat = a.reshape(-1)
    pad = (-flat.shape[0]) % LANES
    if pad:
        flat = jnp.concatenate([flat, jnp.zeros((pad,), flat.dtype)])
    return flat.reshape(-1, LANES)


def _n_rows(shape):
    return -(-int(np.prod(shape)) // LANES)


def _pack(arrays, total_rows):
    rows = [_to_rows(a) for a in arrays]
    used = sum(r.shape[0] for r in rows)
    if total_rows > used:
        rows.append(jnp.zeros((total_rows - used, LANES), F32))
    return jnp.concatenate(rows, axis=0)


def _unpack(buf, shapes):
    out, r0 = [], 0
    for s in shapes:
        n = int(np.prod(s))
        nr = _n_rows(s)
        out.append(buf[r0:r0 + nr].reshape(-1)[:n].reshape(s))
        r0 += nr
    return out


def _shard_views(name, full):
    if name == 'w_out':
        return full.reshape(N_DEV, full.shape[0] // N_DEV, full.shape[1])
    r, ccols = full.shape
    return full.reshape(r, N_DEV, ccols // N_DEV).transpose(1, 0, 2)


def _from_shards(name, stacked):
    if name == 'w_out':
        return stacked.reshape(-1, stacked.shape[-1])
    n, r, cc = stacked.shape
    return stacked.transpose(1, 0, 2).reshape(r, n * cc)


GROUPS_PER_BLK = N_GROUPS // N_COL_BLK


def _block_diag(t):
    eye = jnp.eye(GROUPS_PER_BLK, dtype=t.dtype)
    t4 = t.reshape(N_COL_BLK, GROUPS_PER_BLK, SSM_GROUP, SSM_STATE)
    return (t4[:, :, :, None, :] * eye[None, :, None, :, None]).reshape(N_COL_BLK, CH_BLK, COL_BLK)


def _diag_blocks(mat4):
    return jnp.stack([mat4[c, g * SSM_GROUP:(g + 1) * SSM_GROUP, g * SSM_STATE:(g + 1) * SSM_STATE]
                      for c in range(N_COL_BLK) for g in range(GROUPS_PER_BLK)])


def _step(x, loss_target, wts, moms, vels):
    seq = x.shape[1]
    n_valid = N_META + seq
    lp = -(-n_valid // 256) * 256
    tr = _pick(lp, [640, 256])
    tr_small = 256
    tq = _pick(lp, [640, 256])
    tk = tq
    tq_fwd = _pick(lp, [640, 256])

    shard_shapes = [wts[n].shape[-2:] for n in SHARDED]
    n_shard_rows = sum(_n_rows(s) for s in shard_shapes)
    gathered = _exchange("gather_weights", _pack([wts[n].reshape(wts[n].shape[-2:]) for n in SHARDED],
                                                 n_shard_rows), all_to_all=False)
    parts = [[] for _ in SHARDED]
    for dev in range(N_DEV):
        for i, a in enumerate(_unpack(gathered[dev], shard_shapes)):
            parts[i].append(a)
    full = {n: _from_shards(n, jnp.stack(parts[i])) for i, n in enumerate(SHARDED)}

    w_in_b = jnp.concatenate([_cols_in(full['w_in']), jnp.zeros((D_MODEL, D_IN_PAD - D_IN), F32)],
                             axis=1).astype(BF16)
    wq_b = _cols_q(full['w_q_up']).astype(BF16)
    wkv_b = _cols_kv(full['w_kv_up']).astype(BF16)
    wglu_b = full['w_glu'].astype(BF16)
    wo_b = full['w_out'].astype(BF16)
    pre_w, post_w = wts['pre_norm_w'], wts['post_norm_w']
    qw, kvw, aw, sw = wts['q_norm_w'], wts['kv_norm_w'], wts['attn_out_norm_w'], wts['ssm_out_norm_w']
    bglu, dvec = wts['b_glu'], wts['ssm_d']

    lseg = lp // N_SEG
    pos = _row_position(jnp.arange(lp, dtype=jnp.int32), lseg)
    inv = ROPE_THETA ** (-jnp.arange(HALF_ROPE, dtype=F32) / HALF_ROPE)
    ang = pos.astype(F32)[:, None] * inv[None, :]
    cos, sin = jnp.cos(ang), jnp.sin(ang)
    cos8, sin8 = jnp.tile(cos, (1, HEADS)), jnp.tile(sin, (1, HEADS))
    c32 = jnp.concatenate([cos, cos], axis=1)
    s32 = jnp.concatenate([-sin, sin], axis=1)
    p32 = jnp.asarray(np.roll(np.eye(QK_ROPE, dtype=np.float32), HALF_ROPE, axis=1))
    sum8 = jnp.asarray(np.tile(np.eye(QK_ROPE, dtype=np.float32), (HEADS, 1)))
    head_sum = jnp.asarray(np.repeat(np.eye(HEADS, dtype=np.float32), V_HEAD, axis=0))

    ng = 2 * N_GROUPS
    a_re3 = wts['ssm_a_re'].reshape(ng, 1, SSM_STATE)
    a_im3 = wts['ssm_a_im'].reshape(ng, 1, SSM_STATE)
    ldt3 = wts['ssm_log_dt'].reshape(ng, 1, 1)
    bt_re = wts['ssm_b_re'].reshape(2, N_GROUPS, SSM_STATE, SSM_GROUP).transpose(0, 1, 3, 2).reshape(
        ng, SSM_GROUP, SSM_STATE)
    bt_im = wts['ssm_b_im'].reshape(2, N_GROUPS, SSM_STATE, SSM_GROUP).transpose(0, 1, 3, 2).reshape(
        ng, SSM_GROUP, SSM_STATE)
    c_re = wts['ssm_c_re'].reshape(ng, SSM_GROUP, SSM_STATE)
    c_im = wts['ssm_c_im'].reshape(ng, SSM_GROUP, SSM_STATE)
    abar_re, abar_im, bbt_re, bbt_im = _ssm_disc(a_re3, a_im3, ldt3, bt_re, bt_im)

    def direction(t, d):
        return t[d * N_GROUPS:(d + 1) * N_GROUPS]

    def slab(t, d, sign=1.0):
        return jnp.broadcast_to(sign * direction(t, d).reshape(1, N_STATES), (N_SEG, N_STATES))

    w_re = [_block_diag(direction(bbt_re, d)).astype(BF16) for d in range(2)]
    w_im = [_block_diag(direction(bbt_im, d)).astype(BF16) for d in range(2)]
    cb_re = [_block_diag(direction(c_re, d)).astype(BF16) for d in range(2)]
    cb_im = [_block_diag(-direction(c_im, d)).astype(BF16) for d in range(2)]

    def to_rows(a):
        return a.reshape(N_SEG, lseg, a.shape[-1]).transpose(1, 0, 2).reshape(lp, a.shape[-1])

    def to_tokens(a):
        return a.reshape(lseg, N_SEG, a.shape[-1]).transpose(1, 0, 2).reshape(lp, a.shape[-1])

    pad = jnp.zeros((lp - n_valid, D_MODEL), F32)
    h = to_rows(jnp.concatenate([full['meta_tokens'], x[0], pad], axis=0))
    tgt = to_rows(jnp.concatenate([jnp.zeros((N_META, D_MODEL), F32), loss_target[0], pad], axis=0))

    ql, kvl, ag, su, sg, kr = _inproj(h, pre_w, w_in_b, tr)
    qn_b, qr1_b, qr2_b, kn_b, v_b, kr_b = _qkv_up(ql, kvl, kr, cos8, sin8, c32, s32, qw, kvw, wq_b, wkv_b, p32, tr)

    def heads(a, w):
        return a.reshape(lp, HEADS, w)

    q_h = jnp.concatenate([heads(qn_b, 64), heads(qr1_b, 16), heads(qr2_b, 16)], axis=-1).transpose(1, 0, 2)
    k_h = jnp.concatenate([heads(kn_b, 64), jnp.broadcast_to(kr_b[:, None, :], (lp, HEADS, QK_ROPE))],
                          axis=-1).transpose(1, 0, 2)
    v_h = heads(v_b, 64).transpose(1, 0, 2)
    vx_h = jnp.concatenate([v_h, jnp.ones((HEADS, lp, 1), BF16), jnp.zeros((HEADS, lp, LANES - V_HEAD - 1), BF16)],
                           axis=-1)
    o_h, lse = _attn_fwd(q_h, k_h, vx_h, n_valid, tq_fwd, tk)
    o_flat = o_h.transpose(1, 0, 2).reshape(lp, D_ATTN)
    ya = _attn_post(o_flat, ag, aw, tr)

    xs, ys = [], []
    for d in range(2):
        ar8, ai8 = slab(abar_re, d), slab(abar_im, d)
        ere, eim = _scan_ends(f"scan{d}_ends", su, w_re[d], w_im[d], ar8, ai8, d == 0)
        x_re, x_im, y_d = _scan_fwd(f"scan{d}", su, w_re[d], w_im[d], ar8, ai8, ere, eim, cb_re[d], cb_im[d],
                                    d == 0)
        xs += [x_re, x_im]
        ys.append(y_d)
    ypre, glu, ysn = _ssm_post(ys[0], ys[1], su, sg, wglu_b, bglu, sw, dvec, tr)

    dy, dout, loss, d_post = _out_loss(ya, ysn, h, tgt, wo_b, post_w, n_valid, tr_small)

    do_flat, dag, dysn, delta8, d_wo, d_aw = _out_bwd(dy, ya, ysn, o_flat, ag, wo_b, aw, head_sum, tr_small)
    dypre, dsg, d_wglu, d_bglu, d_sw, d_dvec = _ssm_post_bwd(dysn, glu, sg, ypre, su, wglu_b, sw, dvec, tr_small)

    dus, d_ct, d_wb, d_a8 = [], [], [], []
    for d in range(2):
        ar8, ai8c = slab(abar_re, d), slab(abar_im, d, -1.0)
        ere, eim = _scan_ends(f"scan_adj{d}_ends", dypre, cb_re[d], cb_im[d], ar8, ai8c, d != 0)
        du_d, dw_re, dw_im, dc_re, dc_im, da_re, da_im = _scan_bwd(
            f"scan_adj{d}", dypre, cb_re[d], cb_im[d], ar8, ai8c, ere, eim, su, w_re[d], w_im[d],
            xs[2 * d], xs[2 * d + 1], d != 0)
        dus.append(du_d)
        d_ct.append((dc_re, dc_im))
        d_wb.append((dw_re, dw_im))
        d_a8.append((da_re, da_im))

    do_h = do_flat.astype(BF16).reshape(lp, HEADS, V_HEAD).transpose(1, 0, 2)
    dq_h, dk_h, dv_h = _attn_bwd(q_h, k_h, v_h, do_h, lse.reshape(HEADS, lp // tq, 1, tq),
                                 delta8.T.reshape(HEADS, lp // tq, 1, tq), tq, tk)
    dq_t = dq_h.transpose(1, 0, 2)
    dk_t = dk_h.transpose(1, 0, 2)
    dqn = dq_t[:, :, :64].reshape(lp, 512)
    dr1 = dq_t[:, :, 64:80].reshape(lp, 128)
    dr2 = dq_t[:, :, 80:96].reshape(lp, 128)
    dkn = dk_t[:, :, :64].reshape(lp, 512)
    dkr8 = dk_t[:, :, 64:].reshape(lp, HEADS * QK_ROPE)
    dvf = dv_h.transpose(1, 0, 2).reshape(lp, 512)
    dql, dkvl, dkrr, d_wq, d_wkv, d_qw, d_kvw = _qkv_up_bwd(
        dqn, dr1, dr2, dkn, dvf, dkr8, ql, kvl, cos8, sin8, c32, s32, qw, kvw, wq_b, wkv_b, p32, sum8, tr_small)
    dh, d_win, d_pre = _inproj_bwd(dql, dkvl, dag, dus[0], dus[1], dypre, dsg, dkrr, h, dout, pre_w, w_in_b, dvec,
                                   tr_small)
    dh = to_tokens(dh)

    def seg_sums(t):
        return t.reshape(N_SEG, N_GROUPS, SSM_STATE).transpose(1, 0, 2)

    da8_re = jnp.concatenate([seg_sums(d_a8[d][0]) for d in range(2)], axis=0)
    da8_im = jnp.concatenate([seg_sums(d_a8[d][1]) for d in range(2)], axis=0)
    dbb_re = jnp.concatenate([_diag_blocks(d_wb[d][0]) for d in range(2)], axis=0)
    dbb_im = jnp.concatenate([_diag_blocks(d_wb[d][1]) for d in range(2)], axis=0)
    g_are, g_aim, g_ldt, g_bt_re, g_bt_im = _ssm_disc_bwd(a_re3, a_im3, ldt3, bt_re, bt_im, da8_re, da8_im,
                                                          dbb_re, dbb_im)
    g_c_re = jnp.concatenate([_diag_blocks(d_ct[d][0]) for d in range(2)], axis=0)
    g_c_im = jnp.concatenate([-_diag_blocks(d_ct[d][1]) for d in range(2)], axis=0)

    def b_layout(t):
        return t.reshape(2, N_GROUPS, SSM_GROUP, SSM_STATE).transpose(0, 1, 3, 2)

    local = {
        'meta_tokens': dh[:N_META],
        'pre_norm_w': d_pre, 'post_norm_w': d_post,
        'w_in': _cols_in_inv(d_win[:, :D_IN]),
        'q_norm_w': d_qw, 'w_q_up': _cols_q_inv(d_wq),
        'kv_norm_w': d_kvw, 'w_kv_up': _cols_kv_inv(d_wkv),
        'attn_out_norm_w': d_aw,
        'ssm_a_re': g_are, 'ssm_a_im': g_aim, 'ssm_log_dt': g_ldt,
        'ssm_b_re': b_layout(g_bt_re), 'ssm_b_im': b_layout(g_bt_im), 'ssm_c_re': g_c_re, 'ssm_c_im': g_c_im,
        'ssm_d': d_dvec, 'w_glu': d_wglu, 'b_glu': d_bglu, 'ssm_out_norm_w': d_sw, 'w_out': d_wo,
    }

    order = SHARDED + [n for n in WEIGHTS if n not in SHARDED]
    shapes = [wts[n].shape for n in order]
    used_rows = sum(_n_rows(s) for s in shapes)
    tr_adam = 512
    total_rows = -(-used_rows // tr_adam) * tr_adam
    send = []
    for dev in range(N_DEV):
        arrs = [_shard_views(n, local[n])[dev] if n in SHARDED else local[n] for n in order]
        send.append(_pack(arrs, total_rows))
    recv = _exchange("exchange_grads", jnp.stack(send), all_to_all=True)
    packed = [_pack([src[n] for n in order], total_rows) for src in (wts, moms, vels)]
    g_p, d_p, m_p, v_p = _adamw(recv, *packed, tr_adam)
    grads, deltas, new_m, new_v = (dict(zip(order, _unpack(b, shapes))) for b in (g_p, d_p, m_p, v_p))

    loss = lax.psum(loss[0, 0], ("x", "y", "c"))
    grad_x = dh[N_META:n_valid][None]
    return (loss, grad_x, *[grads[n] for n in WEIGHTS], *[deltas[n] for n in WEIGHTS],
            *[new_m[n] for n in WEIGHTS], *[new_v[n] for n in WEIGHTS])


def kernel(x, meta_tokens, pre_norm_w, post_norm_w, w_in, q_norm_w, w_q_up, kv_norm_w, w_kv_up, attn_out_norm_w, ssm_a_re, ssm_a_im, ssm_log_dt, ssm_b_re, ssm_b_im, ssm_c_re, ssm_c_im, ssm_d, w_glu, b_glu, ssm_out_norm_w, w_out, loss_target, m_meta_tokens, m_pre_norm_w, m_post_norm_w, m_w_in, m_q_norm_w, m_w_q_up, m_kv_norm_w, m_w_kv_up, m_attn_out_norm_w, m_ssm_a_re, m_ssm_a_im, m_ssm_log_dt, m_ssm_b_re, m_ssm_b_im, m_ssm_c_re, m_ssm_c_im, m_ssm_d, m_w_glu, m_b_glu, m_ssm_out_norm_w, m_w_out, v_meta_tokens, v_pre_norm_w, v_post_norm_w, v_w_in, v_q_norm_w, v_w_q_up, v_kv_norm_w, v_w_kv_up, v_attn_out_norm_w, v_ssm_a_re, v_ssm_a_im, v_ssm_log_dt, v_ssm_b_re, v_ssm_b_im, v_ssm_c_re, v_ssm_c_im, v_ssm_d, v_w_glu, v_b_glu, v_ssm_out_norm_w, v_w_out):
    wts = dict(zip(WEIGHTS, (meta_tokens, pre_norm_w, post_norm_w, w_in, q_norm_w, w_q_up, kv_norm_w, w_kv_up,
                             attn_out_norm_w, ssm_a_re, ssm_a_im, ssm_log_dt, ssm_b_re, ssm_b_im, ssm_c_re,
                             ssm_c_im, ssm_d, w_glu, b_glu, ssm_out_norm_w, w_out)))
    moms = dict(zip(WEIGHTS, (m_meta_tokens, m_pre_norm_w, m_post_norm_w, m_w_in, m_q_norm_w, m_w_q_up,
                              m_kv_norm_w, m_w_kv_up, m_attn_out_norm_w, m_ssm_a_re, m_ssm_a_im, m_ssm_log_dt,
                              m_ssm_b_re, m_ssm_b_im, m_ssm_c_re, m_ssm_c_im, m_ssm_d, m_w_glu, m_b_glu,
                              m_ssm_out_norm_w, m_w_out)))
    vels = dict(zip(WEIGHTS, (v_meta_tokens, v_pre_norm_w, v_post_norm_w, v_w_in, v_q_norm_w, v_w_q_up,
                              v_kv_norm_w, v_w_kv_up, v_attn_out_norm_w, v_ssm_a_re, v_ssm_a_im, v_ssm_log_dt,
                              v_ssm_b_re, v_ssm_b_im, v_ssm_c_re, v_ssm_c_im, v_ssm_d, v_w_glu, v_b_glu,
                              v_ssm_out_norm_w, v_w_out)))
    return _step(x, loss_target, wts, moms, vels)
```

```python
import functools
import math

import numpy as np
import jax
import jax.numpy as jnp
from jax import lax
from jax.experimental import pallas as pl
from jax.experimental.pallas import tpu as pltpu

F32 = jnp.float32
BF16 = jnp.bfloat16

D_MODEL = 1024
N_META = 16
EPS = 1e-6
HEADS = 8
QK_NOPE = 64
QK_ROPE = 32
HALF_ROPE = QK_ROPE // 2
QK_DIM = QK_NOPE + QK_ROPE
V_HEAD = 64
Q_LORA = 256
KV_LORA = 128
D_ATTN = HEADS * V_HEAD
D_SSM = 512
SSM_GROUP = 16
N_GROUPS = D_SSM // SSM_GROUP
SSM_STATE = 64
N_STATES = N_GROUPS * SSM_STATE
ROPE_THETA = 10000.0
D_IN = Q_LORA + KV_LORA + QK_ROPE + D_ATTN + 2 * D_SSM
D_IN_PAD = 2048
N_DEV = 8
N_SEG = 8
COL_BLK = 512
LANES = 128

ADAM_LR = 0.001
ADAM_B1 = 0.9
ADAM_B2 = 0.999
ADAM_EPS = 1e-08
ADAM_WD = 0.01
ADAM_STEP = 10

VMEM_LIMIT_V7X = 56 * 1024 * 1024
LOG2E = 1.0 / math.log(2.0)
Q_SCALE = LOG2E / math.sqrt(QK_DIM)
ATTN_UNROLL = 4
ATTN_BWD_UNROLL = 4

WEIGHTS = ['meta_tokens', 'pre_norm_w', 'post_norm_w', 'w_in', 'q_norm_w', 'w_q_up', 'kv_norm_w', 'w_kv_up',
           'attn_out_norm_w', 'ssm_a_re', 'ssm_a_im', 'ssm_log_dt', 'ssm_b_re', 'ssm_b_im', 'ssm_c_re', 'ssm_c_im',
           'ssm_d', 'w_glu', 'b_glu', 'ssm_out_norm_w', 'w_out']
SHARDED = ['w_in', 'w_q_up', 'w_kv_up', 'w_glu', 'w_out', 'meta_tokens']

def _cols_in(w):
    return jnp.concatenate([w[:, 0:384], w[:, 416:D_IN], w[:, 384:416]], axis=1)


def _cols_in_inv(w):
    return jnp.concatenate([w[:, 0:384], w[:, D_IN - QK_ROPE:D_IN], w[:, 384:D_IN - QK_ROPE]], axis=1)


def _cols_q(w):
    t = w.reshape(w.shape[0], HEADS, QK_DIM)
    return jnp.concatenate([t[:, :, 0:64].reshape(-1, 512), t[:, :, 64:80].reshape(-1, 128),
                            t[:, :, 80:96].reshape(-1, 128)], axis=1)


def _cols_q_inv(w):
    r = w.shape[0]
    return jnp.concatenate([w[:, 0:512].reshape(r, HEADS, 64), w[:, 512:640].reshape(r, HEADS, 16),
                            w[:, 640:768].reshape(r, HEADS, 16)], axis=2).reshape(r, HEADS * QK_DIM)


def _cols_kv(w):
    t = w.reshape(w.shape[0], HEADS, 128)
    return jnp.concatenate([t[:, :, 0:64].reshape(-1, 512), t[:, :, 64:128].reshape(-1, 512)], axis=1)


def _cols_kv_inv(w):
    r = w.shape[0]
    return jnp.concatenate([w[:, 0:512].reshape(r, HEADS, 64), w[:, 512:1024].reshape(r, HEADS, 64)],
                           axis=2).reshape(r, HEADS * 128)


def _pick(n, cands):
    for c in cands:
        if n % c == 0:
            return c
    raise ValueError(f"no tile for {n}")


def _cparams(*sem):
    return pltpu.CompilerParams(dimension_semantics=sem, vmem_limit_bytes=VMEM_LIMIT_V7X)


def _mm(a, b):
    return jnp.dot(a.astype(BF16), b.astype(BF16), preferred_element_type=F32)


def _mm_nt(a, b):
    return lax.dot_general(a.astype(BF16), b.astype(BF16), (((1,), (1,)), ((), ())), preferred_element_type=F32)


def _mm_tn(a, b):
    return lax.dot_general(a.astype(BF16), b.astype(BF16), (((0,), (0,)), ((), ())), preferred_element_type=F32)


def _mm_exact(a, b):
    return jnp.dot(a, b, precision=lax.Precision.HIGHEST, preferred_element_type=F32)


def _rms(x):
    return lax.rsqrt(jnp.mean(x * x, axis=-1, keepdims=True) + EPS)


def _rms_bwd(dy, x, r, w):
    xh = x * r
    g = dy * w
    dx = r * (g - xh * jnp.mean(g * xh, axis=-1, keepdims=True))
    dw = jnp.sum(dy * xh, axis=0, keepdims=True)
    return dx, dw


def _sigmoid(z):
    return 1.0 / (1.0 + jnp.exp(-z))


def _silu_and_grad(z):
    s = _sigmoid(z)
    return z * s, s * (1.0 + z * (1.0 - s))


_GELU_C = math.sqrt(2.0 / math.pi)


def _gelu_and_grad(x):
    x2 = x * x
    t = jnp.tanh(_GELU_C * (x + 0.044715 * x * x2))
    val = 0.5 * x * (1.0 + t)
    grad = 0.5 * (1.0 + t) + 0.5 * x * (1.0 - t * t) * _GELU_C * (1.0 + 3.0 * 0.044715 * x2)
    return val, grad


def _acc(ref, val, first):
    @pl.when(first)
    def _():
        ref[...] = val

    @pl.when(jnp.logical_not(first))
    def _():
        ref[...] += val


def _rows_call(name, body, tr, row_ins, full_ins, row_outs, acc_outs):
    lp = row_ins[0].shape[0]
    in_specs = [pl.BlockSpec((tr, a.shape[1]), lambda i: (i, 0)) for a in row_ins]
    in_specs += [pl.BlockSpec(a.shape, lambda i, n=a.ndim: (0,) * n) for a in full_ins]
    out_specs = [pl.BlockSpec((tr, c), lambda i: (i, 0)) for c, _ in row_outs]
    out_specs += [pl.BlockSpec(s, lambda i, n=len(s): (0,) * n) for s, _ in acc_outs]
    out_shape = [jax.ShapeDtypeStruct((lp, c), dt) for c, dt in row_outs]
    out_shape += [jax.ShapeDtypeStruct(s, dt) for s, dt in acc_outs]
    return pl.pallas_call(
        body, name=name, grid=(lp // tr,), in_specs=in_specs, out_specs=out_specs, out_shape=out_shape,
        compiler_params=_cparams("arbitrary"))(*row_ins, *full_ins)


def _inproj(h, pre_w, w_in_b, tr):
    def body(h_ref, pw_ref, w_ref, ql, kvl, ag, su, sg, kr):
        x = h_ref[...]
        xn = x * _rms(x) * pw_ref[...]
        pr = _mm(xn, w_ref[...])
        ql[...] = pr[:, 0:256]
        kvl[...] = pr[:, 256:384]
        ag[...] = pr[:, 384:896]
        su[...] = pr[:, 896:1408]
        sg[...] = pr[:, 1408:1920]
        kr[...] = pr[:, 1920:1952]

    return _rows_call("inproj", body, tr, [h], [pre_w, w_in_b],
                      [(256, F32), (128, F32), (512, F32), (512, F32), (512, F32), (32, F32)], [])


def _qkv_up(ql, kvl, kr, cos8, sin8, c32, s32, qw, kvw, wq_b, wkv_b, p32, tr):
    def body(ql_ref, kvl_ref, kr_ref, cos_ref, sin_ref, c32_ref, s32_ref, qw_ref, kvw_ref, wq_ref, wkv_ref, p_ref,
             qn_o, qr1_o, qr2_o, kn_o, v_o, kr_o):
        x = ql_ref[...]
        q = _mm(x * _rms(x) * qw_ref[...], wq_ref[...]) * Q_SCALE
        r1, r2 = q[:, 512:640], q[:, 640:768]
        cs, sn = cos_ref[...], sin_ref[...]
        qn_o[...] = q[:, 0:512].astype(BF16)
        qr1_o[...] = (r1 * cs - r2 * sn).astype(BF16)
        qr2_o[...] = (r2 * cs + r1 * sn).astype(BF16)
        x = kvl_ref[...]
        kv = _mm(x * _rms(x) * kvw_ref[...], wkv_ref[...])
        kn_o[...] = kv[:, 0:512].astype(BF16)
        v_o[...] = kv[:, 512:1024].astype(BF16)
        x = kr_ref[...]
        kr_o[...] = (x * c32_ref[...] + _mm_exact(x, p_ref[...]) * s32_ref[...]).astype(BF16)

    return _rows_call("qkv_up", body, tr, [ql, kvl, kr, cos8, sin8, c32, s32], [qw, kvw, wq_b, wkv_b, p32],
                      [(512, BF16), (128, BF16), (128, BF16), (512, BF16), (512, BF16), (32, BF16)], [])


def _row_position(row, lseg):
    return (row & (N_SEG - 1)) * lseg + (row >> 3)


def _first_padded_tile(n_valid, lp, tile):
    lseg = lp // N_SEG
    t0 = n_valid - (N_SEG - 1) * lseg
    return (t0 * N_SEG + N_SEG - 1) // tile if n_valid < lp else lp // tile


def _attn_fwd(qt, k, vxt, n_valid):
    _, nq, _, tq = qt.shape
    _, nk, _, tk = vxt.shape
    lp = k.shape[1]
    lseg = lp // N_SEG
    n_plain = max(0, min(nk, _first_padded_tile(n_valid, lp, tk)))

    def body(q_ref, k_ref, v_ref, o_ref, lse_ref, m_s, acc_s):
        m_s[...] = jnp.full(m_s.shape, -1e30, F32)
        acc_s[...] = jnp.zeros(acc_s.shape, F32)
        qq = q_ref[0, 0]

        def chunk(c, padded):
            r0 = pl.multiple_of(c * tk, tk)
            st = _mm(k_ref[0, pl.ds(r0, tk), :], qq)
            if padded:
                row = r0 + lax.broadcasted_iota(jnp.int32, (tk, tq), 0)
                st = jnp.where(_row_position(row, lseg) < n_valid, st, -1e30)
            m_old = m_s[...]
            m_new = jnp.maximum(m_old, jnp.max(st, axis=0, keepdims=True))
            pt = jnp.exp2(st - m_new)
            acc_s[...] = jnp.exp2(m_old - m_new) * acc_s[...] + _mm(v_ref[0, c], pt)
            m_s[...] = m_new

        def plain(c, carry):
            chunk(c, False)
            return carry

        n_loop = n_plain - n_plain % ATTN_UNROLL
        if n_loop:
            lax.fori_loop(0, n_loop, plain, 0, unroll=ATTN_UNROLL)
        for c in range(n_loop, nk):
            chunk(c, c >= n_plain)
        acc = acc_s[...]
        l = acc[V_HEAD:V_HEAD + 1, :]
        o_ref[0] = acc[:V_HEAD, :] / l
        lse_ref[0, 0] = m_s[...] + jnp.log2(l)

    return pl.pallas_call(
        body, name="attn_fwd", grid=(HEADS, nq),
        in_specs=[pl.BlockSpec((1, 1, QK_DIM, tq), lambda h, i: (h, i, 0, 0)),
                  pl.BlockSpec((1, lp, QK_DIM), lambda h, i: (h, 0, 0)),
                  pl.BlockSpec((1, nk, LANES, tk), lambda h, i: (h, 0, 0, 0))],
        out_specs=[pl.BlockSpec((1, V_HEAD, tq), lambda h, i: (h, 0, i)),
                   pl.BlockSpec((1, 1, 1, tq), lambda h, i: (h, i, 0, 0))],
        out_shape=[jax.ShapeDtypeStruct((HEADS, V_HEAD, lp), F32), jax.ShapeDtypeStruct((HEADS, nq, 1, tq), F32)],
        scratch_shapes=[pltpu.VMEM((1, tq), F32), pltpu.VMEM((LANES, tq), F32)],
        compiler_params=_cparams("arbitrary", "arbitrary"))(qt, k, vxt)


def _attn_post(o_flat, ag, aw, tr):
    def body(o_ref, g_ref, w_ref, ya):
        t = o_ref[...] * _silu_and_grad(g_ref[...])[0]
        ya[...] = t * _rms(t) * w_ref[...]

    return _rows_call("attn_post", body, tr, [o_flat, ag], [aw], [(512, F32)], [])[0]


def _scan_tiles(lp):
    lseg = lp // N_SEG
    tt = _pick(lseg, [104, 48, 32, 16, 8, 4, 2, 1])
    return lseg, tt, lseg // tt


def _cmul(ar, ai, br, bi):
    return ar * br - ai * bi, ar * bi + ai * br


N_COL_BLK = N_STATES // COL_BLK
CH_BLK = D_SSM // N_COL_BLK


def _scan_steps(tt, forward, bre_ref, bim_ref, ar, ai, carry, visit):
    def step(s, c):
        r0 = pl.multiple_of((s if forward else tt - 1 - s) * N_SEG, N_SEG)
        pr, pi = _cmul(ar, ai, c[0], c[1])
        xr = pr + bre_ref[pl.ds(r0, N_SEG), :]
        xi = pi + bim_ref[pl.ds(r0, N_SEG), :]
        return (xr, xi) + tuple(visit(r0, (xr, xi), (c[0], c[1]), c[2:]))

    return lax.fori_loop(0, tt, step, carry, unroll=4 if tt % 4 == 0 else 1)


def _segment_starts(lseg, forward, ar, ai, ere_ref, eim_ref, s_re, s_im):
    a1r, a1i = ar[0:1, :], ai[0:1, :]
    pr, pi = jnp.ones_like(a1r), jnp.zeros_like(a1i)
    br, bi = a1r, a1i
    n = lseg
    while n:
        if n & 1:
            pr, pi = _cmul(pr, pi, br, bi)
        n >>= 1
        if n:
            br, bi = _cmul(br, bi, br, bi)
    cr, ci = jnp.zeros_like(a1r), jnp.zeros_like(a1i)
    for j in (range(N_SEG) if forward else range(N_SEG - 1, -1, -1)):
        s_re[j:j + 1, :] = cr
        s_im[j:j + 1, :] = ci
        nr, ni = _cmul(pr, pi, cr, ci)
        cr = nr + ere_ref[j:j + 1, :]
        ci = ni + eim_ref[j:j + 1, :]


def _scan_specs(lp, forward):
    lseg, tt, nt = _scan_tiles(lp)

    def tile(t):
        return t if forward else nt - 1 - t

    rows = lambda w: pl.BlockSpec((tt * N_SEG, w), lambda cb, t: (tile(t), cb))
    proj = pl.BlockSpec((1, CH_BLK, COL_BLK), lambda cb, t: (cb, 0, 0))
    slab = pl.BlockSpec((N_SEG, COL_BLK), lambda cb, t: (0, cb))
    return lseg, tt, nt, rows, proj, slab


def _scan_ends(name, urows, wre4, wim4, ar8, ai8, forward):
    lp = urows.shape[0]
    lseg, tt, nt, rows, proj, slab = _scan_specs(lp, forward)

    def body(u_ref, wre_ref, wim_ref, ar_ref, ai_ref, ere_o, eim_o, bre_s, bim_s, cr_s, ci_s):
        t = pl.program_id(1)

        @pl.when(t == 0)
        def _():
            cr_s[...] = jnp.zeros(cr_s.shape, F32)
            ci_s[...] = jnp.zeros(ci_s.shape, F32)

        u = u_ref[...]
        bre_s[...] = _mm(u, wre_ref[0])
        bim_s[...] = _mm(u, wim_ref[0])
        cr, ci = _scan_steps(tt, forward, bre_s, bim_s, ar_ref[...], ai_ref[...], (cr_s[...], ci_s[...]),
                             lambda r0, x, x_prev, extra: ())
        cr_s[...] = cr
        ci_s[...] = ci

        @pl.when(t == nt - 1)
        def _():
            ere_o[...] = cr
            eim_o[...] = ci

    return pl.pallas_call(
        body, name=name, grid=(N_COL_BLK, nt), in_specs=[rows(CH_BLK), proj, proj, slab, slab],
        out_specs=[slab, slab], out_shape=[jax.ShapeDtypeStruct((N_SEG, N_STATES), F32)] * 2,
        scratch_shapes=[pltpu.VMEM((tt * N_SEG, COL_BLK), F32)] * 2 + [pltpu.VMEM((N_SEG, COL_BLK), F32)] * 2,
        compiler_params=_cparams("arbitrary", "arbitrary"))(urows, wre4, wim4, ar8, ai8)


def _scan_fwd(name, urows, wre4, wim4, ar8, ai8, ere, eim, cre4, cim4, forward):
    lp = urows.shape[0]
    lseg, tt, nt, rows, proj, slab = _scan_specs(lp, forward)

    def body(u_ref, wre_ref, wim_ref, ar_ref, ai_ref, ere_ref, eim_ref, cre_ref, cim_ref,
             xre_o, xim_o, y_o, bre_s, bim_s, cr_s, ci_s):
        ar, ai = ar_ref[...], ai_ref[...]

        @pl.when(pl.program_id(1) == 0)
        def _():
            _segment_starts(lseg, forward, ar, ai, ere_ref, eim_ref, cr_s, ci_s)

        u = u_ref[...]
        bre_s[...] = _mm(u, wre_ref[0])
        bim_s[...] = _mm(u, wim_ref[0])

        def visit(r0, x, x_prev, extra):
            xre_o[pl.ds(r0, N_SEG), :] = x[0]
            xim_o[pl.ds(r0, N_SEG), :] = x[1]
            return ()

        cr, ci = _scan_steps(tt, forward, bre_s, bim_s, ar, ai, (cr_s[...], ci_s[...]), visit)
        cr_s[...] = cr
        ci_s[...] = ci
        y_o[...] = _mm_nt(xre_o[...], cre_ref[0]) + _mm_nt(xim_o[...], cim_ref[0])

    return pl.pallas_call(
        body, name=name, grid=(N_COL_BLK, nt),
        in_specs=[rows(CH_BLK), proj, proj, slab, slab, slab, slab, proj, proj],
        out_specs=[rows(COL_BLK), rows(COL_BLK), rows(CH_BLK)],
        out_shape=[jax.ShapeDtypeStruct((lp, N_STATES), F32)] * 2 + [jax.ShapeDtypeStruct((lp, D_SSM), F32)],
        scratch_shapes=[pltpu.VMEM((tt * N_SEG, COL_BLK), F32)] * 2 + [pltpu.VMEM((N_SEG, COL_BLK), F32)] * 2,
        compiler_params=_cparams("arbitrary", "arbitrary"))(urows, wre4, wim4, ar8, ai8, ere, eim, cre4, cim4)


def _scan_bwd(name, dyrows, cre4, cim4, ar8, ai8, ere, eim, urows, wre4, wim4, xre, xim, forward):
    lp = urows.shape[0]
    lseg, tt, nt, rows, proj, slab = _scan_specs(lp, forward)

    def body(dy_ref, cre_ref, cim_ref, ar_ref, ai_ref, ere_ref, eim_ref, u_ref, wre_ref, wim_ref, xre_ref, xim_ref,
             du_o, dwre_o, dwim_o, dcre_o, dcim_o, dare_o, daim_o, bre_s, bim_s, gre_s, gim_s, cr_s, ci_s):
        t = pl.program_id(1)
        ar, ai = ar_ref[...], ai_ref[...]

        @pl.when(t == 0)
        def _():
            _segment_starts(lseg, forward, ar, ai, ere_ref, eim_ref, cr_s, ci_s)
            dare_o[...] = jnp.zeros(dare_o.shape, F32)
            daim_o[...] = jnp.zeros(daim_o.shape, F32)

        dy = dy_ref[...]
        bre_s[...] = _mm(dy, cre_ref[0])
        bim_s[...] = _mm(dy, cim_ref[0])

        def visit(r0, g, g_prev, sums):
            gre_s[pl.ds(r0, N_SEG), :] = g[0]
            gim_s[pl.ds(r0, N_SEG), :] = g[1]
            fr = xre_ref[pl.ds(r0, N_SEG), :]
            fi = xim_ref[pl.ds(r0, N_SEG), :]
            pr, pi = g_prev
            return sums[0] + fr * pr + fi * pi, sums[1] + fr * pi - fi * pr

        out = _scan_steps(tt, forward, bre_s, bim_s, ar, ai, (cr_s[...], ci_s[...], dare_o[...], daim_o[...]), visit)
        cr_s[...] = out[0]
        ci_s[...] = out[1]
        dare_o[...] = out[2]
        daim_o[...] = out[3]
        gre, gim = gre_s[...], gim_s[...]
        du_o[...] = _mm_nt(gre, wre_ref[0]) + _mm_nt(gim, wim_ref[0])
        u = u_ref[...]
        first = t == 0
        _acc(dwre_o, _mm_tn(u, gre)[None], first)
        _acc(dwim_o, _mm_tn(u, gim)[None], first)
        _acc(dcre_o, _mm_tn(dy, xre_ref[...])[None], first)
        _acc(dcim_o, _mm_tn(dy, xim_ref[...])[None], first)

    big = pltpu.VMEM((tt * N_SEG, COL_BLK), F32)
    small = pltpu.VMEM((N_SEG, COL_BLK), F32)
    return pl.pallas_call(
        body, name=name, grid=(N_COL_BLK, nt),
        in_specs=[rows(CH_BLK), proj, proj, slab, slab, slab, slab, rows(CH_BLK), proj, proj,
                  rows(COL_BLK), rows(COL_BLK)],
        out_specs=[rows(CH_BLK), proj, proj, proj, proj, slab, slab],
        out_shape=[jax.ShapeDtypeStruct((lp, D_SSM), F32)]
        + [jax.ShapeDtypeStruct((N_COL_BLK, CH_BLK, COL_BLK), F32)] * 4
        + [jax.ShapeDtypeStruct((N_SEG, N_STATES), F32)] * 2,
        scratch_shapes=[big, big, big, big, small, small],
        compiler_params=_cparams("arbitrary", "arbitrary"))(
            dyrows, cre4, cim4, ar8, ai8, ere, eim, urows, wre4, wim4, xre, xim)


def _ssm_post(yf, yb, u, sg, wglu_b, bglu, sw, dvec, tr):
    def body(yf_ref, yb_ref, u_ref, g_ref, w_ref, b_ref, sw_ref, d_ref, ypre_o, glu_o, ysn_o):
        ypre = yf_ref[...] + yb_ref[...] + d_ref[...] * u_ref[...]
        ypre_o[...] = ypre
        glu = _mm(_gelu_and_grad(ypre)[0], w_ref[...]) + b_ref[...]
        glu_o[...] = glu
        t = glu[:, :D_SSM] * _sigmoid(glu[:, D_SSM:]) * _silu_and_grad(g_ref[...])[0]
        ysn_o[...] = t * _rms(t) * sw_ref[...]

    return _rows_call("ssm_post", body, tr, [yf, yb, u, sg], [wglu_b, bglu, sw, dvec],
                      [(512, F32), (1024, F32), (512, F32)], [])


def _out_loss(ya, ysn, h, tgt, wo_b, post_w, n_valid, tr):
    lseg = h.shape[0] // N_SEG

    def body(ya_ref, ys_ref, h_ref, t_ref, w_ref, pw_ref, dy_o, dout_o, loss_o, dpw_o):
        i = pl.program_id(0)
        y = _mm(ya_ref[...], w_ref[0:D_ATTN, :]) + _mm(ys_ref[...], w_ref[D_ATTN:, :])
        r = _rms(y)
        pw = pw_ref[...]
        out = h_ref[...] + y * r * pw
        pos = _row_position(i * tr + lax.broadcasted_iota(jnp.int32, (tr, 1), 0), lseg)
        valid = jnp.logical_and(pos >= N_META, pos < n_valid)
        diff = jnp.where(valid, out - t_ref[...], 0.0)
        dout = diff * (1.0 / D_MODEL)
        dy, dpw = _rms_bwd(dout, y, r, pw)
        dy_o[...] = dy
        dout_o[...] = dout
        _acc(loss_o, 0.5 * jnp.sum(jnp.sum(diff * diff, axis=1, keepdims=True), axis=0, keepdims=True)
             * (1.0 / D_MODEL), i == 0)
        _acc(dpw_o, dpw, i == 0)

    return _rows_call("out_loss", body, tr, [ya, ysn, h, tgt], [wo_b, post_w], [(1024, F32), (1024, F32)],
                      [((1, 1), F32), ((1, D_MODEL), F32)])


def _out_bwd(dy, ya, ysn, o_flat, ag, wo_b, aw, head_sum, tr):
    def body(dy_ref, ya_ref, ys_ref, o_ref, g_ref, w_ref, aw_ref, hs_ref, do_o, dag_o, dysn_o, dl_o, dwo_o, daw_o):
        i = pl.program_id(0)
        dy = dy_ref[...]
        dcat = _mm_nt(dy, w_ref[...])
        cat = jnp.concatenate([ya_ref[...], ys_ref[...]], axis=1)
        _acc(dwo_o, _mm_tn(cat, dy), i == 0)
        dysn_o[...] = dcat[:, D_ATTN:]
        o = o_ref[...]
        sl, dsl = _silu_and_grad(g_ref[...])
        t = o * sl
        dt, daw = _rms_bwd(dcat[:, :D_ATTN], t, _rms(t), aw_ref[...])
        _acc(daw_o, daw, i == 0)
        do = dt * sl
        do_o[...] = do
        dag_o[...] = dt * o * dsl
        dl_o[...] = _mm_exact(do * o, hs_ref[...])

    return _rows_call("out_bwd", body, tr, [dy, ya, ysn, o_flat, ag], [wo_b, aw, head_sum],
                      [(512, F32), (512, F32), (512, F32), (HEADS, F32)],
                      [((D_MODEL, D_MODEL), F32), ((1, D_ATTN), F32)])


def _ssm_post_bwd(dysn, glu, sg, ypre, u, wglu_b, sw, dvec, tr):
    def body(d_ref, glu_ref, sg_ref, y_ref, u_ref, w_ref, sw_ref, dv_ref,
             dyp_o, dsg_o, dwg_o, dbg_o, dsw_o, dd_o):
        i = pl.program_id(0)
        glu = glu_ref[...]
        a, b = glu[:, :D_SSM], glu[:, D_SSM:]
        sb = _sigmoid(b)
        ys = a * sb
        sl, dsl = _silu_and_grad(sg_ref[...])
        t = ys * sl
        dt, dsw = _rms_bwd(d_ref[...], t, _rms(t), sw_ref[...])
        _acc(dsw_o, dsw, i == 0)
        dsg_o[...] = dt * ys * dsl
        dys = dt * sl
        dglu = jnp.concatenate([dys * sb, dys * a * sb * (1.0 - sb)], axis=1)
        _acc(dbg_o, jnp.sum(dglu, axis=0, keepdims=True), i == 0)
        gel, dgel = _gelu_and_grad(y_ref[...])
        _acc(dwg_o, _mm_tn(gel, dglu), i == 0)
        dyp = _mm_nt(dglu, w_ref[...]) * dgel
        dyp_o[...] = dyp
        _acc(dd_o, jnp.sum(dyp * u_ref[...], axis=0, keepdims=True), i == 0)

    return _rows_call("ssm_post_bwd", body, tr, [dysn, glu, sg, ypre, u], [wglu_b, sw, dvec],
                      [(512, F32), (512, F32)],
                      [((D_SSM, 2 * D_SSM), F32), ((1, 2 * D_SSM), F32), ((1, D_SSM), F32), ((1, D_SSM), F32)])


def _attn_bwd(qt, k, kt, v, dot, lse_t, delta_t):
    _, nq, _, tq = qt.shape
    lp = k.shape[1]
    tk = tq
    nk = lp // tk
    assert lse_t.shape == (HEADS, nq, 1, tq) and delta_t.shape == (HEADS, nq, 1, tq)

    def body(q_ref, k_ref, kt_ref, v_ref, do_ref, lse_ref, dl_ref, dq_o, dk_o, dv_o, dk_s, dv_s):
        @pl.when(pl.program_id(1) == 0)
        def _():
            dq_o[...] = jnp.zeros(dq_o.shape, F32)

        dk_s[...] = jnp.zeros(dk_s.shape, F32)
        dv_s[...] = jnp.zeros(dv_s.shape, F32)
        kk = k_ref[0]
        kkt = kt_ref[0]
        vv = v_ref[0]

        def chunk(c, carry):
            qq = q_ref[0, c]
            dd = do_ref[0, c]
            pt = jnp.exp2(_mm(kk, qq) - lse_ref[0, c])
            dv_s[...] += _mm_nt(dd, pt)
            dst = (pt * (_mm(vv, dd) - dl_ref[0, c])).astype(BF16)
            dk_s[...] += _mm_nt(qq, dst)
            dq_o[0, c] += _mm(kkt, dst)
            return carry

        n_loop = nq - nq % ATTN_BWD_UNROLL
        if n_loop:
            lax.fori_loop(0, n_loop, chunk, 0, unroll=ATTN_BWD_UNROLL)
        for c in range(n_loop, nq):
            chunk(c, 0)
        dk_o[0] = dk_s[...]
        dv_o[0] = dv_s[...]

    head = lambda w: pl.BlockSpec((1, nq, w, tq), lambda h, j: (h, 0, 0, 0))
    rows = lambda w: pl.BlockSpec((1, tk, w), lambda h, j: (h, j, 0))
    cols = lambda w: pl.BlockSpec((1, w, tk), lambda h, j: (h, 0, j))
    return pl.pallas_call(
        body, name="attn_bwd", grid=(HEADS, nk),
        in_specs=[head(QK_DIM), rows(QK_DIM), cols(QK_DIM), rows(V_HEAD), head(V_HEAD), head(1), head(1)],
        out_specs=[head(QK_DIM), cols(QK_DIM), cols(V_HEAD)],
        out_shape=[jax.ShapeDtypeStruct((HEADS, nq, QK_DIM, tq), F32), jax.ShapeDtypeStruct((HEADS, QK_DIM, lp), F32),
                   jax.ShapeDtypeStruct((HEADS, V_HEAD, lp), F32)],
        scratch_shapes=[pltpu.VMEM((QK_DIM, tk), F32), pltpu.VMEM((V_HEAD, tk), F32)],
        compiler_params=_cparams("arbitrary", "arbitrary"))(qt, k, kt, v, dot, lse_t, delta_t)


def _qkv_up_bwd(dqn, dr1, dr2, dkn, dv, dkr8, ql, kvl, cos8, sin8, c32, s32, qw, kvw, wq_b, wkv_b, p32, sum8, tr):
    def body(dqn_ref, dr1_ref, dr2_ref, dkn_ref, dv_ref, dkr_ref, ql_ref, kvl_ref, cos_ref, sin_ref, c32_ref,
             s32_ref, qw_ref, kvw_ref, wq_ref, wkv_ref, p_ref, s8_ref,
             dql_o, dkvl_o, dkrr_o, dwq_o, dwkv_o, dqw_o, dkvw_o):
        i = pl.program_id(0)
        cs, sn = cos_ref[...], sin_ref[...]
        d1, d2 = dr1_ref[...], dr2_ref[...]
        dq = jnp.concatenate([dqn_ref[...], d1 * cs + d2 * sn, d2 * cs - d1 * sn], axis=1) * (Q_SCALE / LOG2E)
        x = ql_ref[...]
        r = _rms(x)
        qw = qw_ref[...]
        _acc(dwq_o, _mm_tn(x * r * qw, dq), i == 0)
        dx, dw = _rms_bwd(_mm_nt(dq, wq_ref[...]), x, r, qw)
        dql_o[...] = dx
        _acc(dqw_o, dw, i == 0)
        dkv = jnp.concatenate([dkn_ref[...] * (1.0 / LOG2E), dv_ref[...]], axis=1)
        x = kvl_ref[...]
        r = _rms(x)
        kvw = kvw_ref[...]
        _acc(dwkv_o, _mm_tn(x * r * kvw, dkv), i == 0)
        dx, dw = _rms_bwd(_mm_nt(dkv, wkv_ref[...]), x, r, kvw)
        dkvl_o[...] = dx
        _acc(dkvw_o, dw, i == 0)
        dkr = _mm_exact(dkr_ref[...], s8_ref[...]) * (1.0 / LOG2E)
        dkrr_o[...] = dkr * c32_ref[...] + _mm_exact(dkr * s32_ref[...], p_ref[...])

    return _rows_call("qkv_up_bwd", body, tr, [dqn, dr1, dr2, dkn, dv, dkr8, ql, kvl, cos8, sin8, c32, s32],
                      [qw, kvw, wq_b, wkv_b, p32, sum8], [(256, F32), (128, F32), (32, F32)],
                      [((Q_LORA, 768), F32), ((KV_LORA, 1024), F32), ((1, Q_LORA), F32), ((1, KV_LORA), F32)])


def _inproj_bwd(dql, dkvl, dag, du_f, du_b, dypre, dsg, dkr, h, dout, pre_w, w_in_b, dvec, tr):
    def body(dql_ref, dkvl_ref, dag_ref, duf_ref, dub_ref, dyp_ref, dsg_ref, dkr_ref, h_ref, dout_ref,
             pw_ref, w_ref, dv_ref, dh_o, dwin_o, dpw_o):
        i = pl.program_id(0)
        du = duf_ref[...] + dub_ref[...] + dv_ref[...] * dyp_ref[...]
        dproj = jnp.concatenate([dql_ref[...], dkvl_ref[...], dag_ref[...], du, dsg_ref[...],
                                 dkr_ref[...], jnp.zeros((tr, D_IN_PAD - D_IN), F32)], axis=1)
        x = h_ref[...]
        r = _rms(x)
        pw = pw_ref[...]
        _acc(dwin_o, _mm_tn(x * r * pw, dproj), i == 0)
        dx, dw = _rms_bwd(_mm_nt(dproj, w_ref[...]), x, r, pw)
        _acc(dpw_o, dw, i == 0)
        dh_o[...] = dout_ref[...] + dx

    return _rows_call("inproj_bwd", body, tr, [dql, dkvl, dag, du_f, du_b, dypre, dsg, dkr, h, dout],
                      [pre_w, w_in_b, dvec], [(1024, F32)], [((D_MODEL, D_IN_PAD), F32), ((1, D_MODEL), F32)])


def _disc_terms(a_re, a_im, ldt):
    dt = jnp.exp(ldt)
    mag = jnp.exp(a_re * dt)
    th = a_im * dt
    cs, sn = jnp.cos(th), jnp.sin(th)
    abar_re, abar_im = mag * cs, mag * sn
    num_re, num_im = abar_re - 1.0, abar_im
    den = a_re * a_re + a_im * a_im
    coef_re = (num_re * a_re + num_im * a_im) / den
    coef_im = (num_im * a_re - num_re * a_im) / den
    return dt, mag, cs, sn, abar_re, abar_im, num_re, num_im, den, coef_re, coef_im


def _ssm_disc(a_re, a_im, ldt, bt_re, bt_im):
    def body(ar_ref, ai_ref, l_ref, br_ref, bi_ref, abr_o, abi_o, bbr_o, bbi_o):
        t = _disc_terms(ar_ref[...], ai_ref[...], l_ref[...])
        abr_o[...] = t[4]
        abi_o[...] = t[5]
        cr, ci = t[9], t[10]
        br, bi = br_ref[...], bi_ref[...]
        bbr_o[...] = cr * br - ci * bi
        bbi_o[...] = cr * bi + ci * br

    ng = a_re.shape[0]
    return pl.pallas_call(
        body, name="ssm_disc",
        out_shape=[jax.ShapeDtypeStruct((ng, 1, SSM_STATE), F32)] * 2
        + [jax.ShapeDtypeStruct((ng, SSM_GROUP, SSM_STATE), F32)] * 2)(a_re, a_im, ldt, bt_re, bt_im)


def _ssm_disc_bwd(a_re, a_im, ldt, bt_re, bt_im, da8_re, da8_im, dbb_re, dbb_im):
    def body(ar_ref, ai_ref, l_ref, br_ref, bi_ref, dar_ref, dai_ref, dbr_ref, dbi_ref,
             gar_o, gai_o, gl_o, gbr_o, gbi_o):
        a_re, a_im = ar_ref[...], ai_ref[...]
        dt, mag, cs, sn, abar_re, abar_im, num_re, num_im, den, cr, ci = _disc_terms(a_re, a_im, l_ref[...])
        br, bi = br_ref[...], bi_ref[...]
        dbr, dbi = dbr_ref[...], dbi_ref[...]
        gbr_o[...] = cr * dbr + ci * dbi
        gbi_o[...] = cr * dbi - ci * dbr
        dcr = jnp.sum(br * dbr + bi * dbi, axis=1, keepdims=True)
        dci = jnp.sum(br * dbi - bi * dbr, axis=1, keepdims=True)
        dnum_re = (dcr * a_re - dci * a_im) / den
        dnum_im = (dcr * a_im + dci * a_re) / den
        dden = -(dcr * cr + dci * ci) / den
        g_are = (dcr * num_re + dci * num_im) / den + dden * 2.0 * a_re
        g_aim = (dcr * num_im - dci * num_re) / den + dden * 2.0 * a_im
        d_abr = jnp.sum(dar_ref[...], axis=1, keepdims=True) + dnum_re
        d_abi = jnp.sum(dai_ref[...], axis=1, keepdims=True) + dnum_im
        dmag = d_abr * cs + d_abi * sn
        dth = d_abi * abar_re - d_abr * abar_im
        g_are = g_are + dmag * mag * dt
        g_aim = g_aim + dth * dt
        ddt = jnp.sum(dmag * mag * a_re + dth * a_im, axis=2, keepdims=True)
        gar_o[...] = g_are
        gai_o[...] = g_aim
        gl_o[...] = ddt * dt

    ng = a_re.shape[0]
    return pl.pallas_call(
        body, name="ssm_disc_bwd",
        out_shape=[jax.ShapeDtypeStruct((ng, 1, SSM_STATE), F32)] * 2 + [jax.ShapeDtypeStruct((ng, 1, 1), F32)]
        + [jax.ShapeDtypeStruct((ng, SSM_GROUP, SSM_STATE), F32)] * 2)(
            a_re, a_im, ldt, bt_re, bt_im, da8_re, da8_im, dbb_re, dbb_im)


def _exchange(name, buf, all_to_all):
    rows = buf.shape[-2]

    def body(in_ref, out_ref, send_sems, recv_sems, local_sem):
        x, y, c = lax.axis_index("x"), lax.axis_index("y"), lax.axis_index("c")
        me = 4 * x + 2 * y + c
        copies = []
        for k in range(1, N_DEV):
            px = 1 - x if (k >> 2) & 1 else x
            py = 1 - y if (k >> 1) & 1 else y
            pc = 1 - c if k & 1 else c
            src = in_ref.at[4 * px + 2 * py + pc] if all_to_all else in_ref
            copies.append(pltpu.make_async_remote_copy(
                src_ref=src, dst_ref=out_ref.at[me], send_sem=send_sems.at[k - 1], recv_sem=recv_sems.at[k - 1],
                device_id=(px, py, pc), device_id_type=pl.DeviceIdType.MESH))
        mine = pltpu.make_async_copy(in_ref.at[me] if all_to_all else in_ref, out_ref.at[me], local_sem)
        mine.start()
        for cp in copies:
            cp.start()
        for cp in copies:
            cp.wait()
        mine.wait()

    return pl.pallas_call(
        body, name=name, out_shape=jax.ShapeDtypeStruct((N_DEV, rows, LANES), F32),
        in_specs=[pl.BlockSpec(memory_space=pl.ANY)], out_specs=pl.BlockSpec(memory_space=pl.ANY),
        scratch_shapes=[pltpu.SemaphoreType.DMA((N_DEV - 1,)), pltpu.SemaphoreType.DMA((N_DEV - 1,)),
                        pltpu.SemaphoreType.DMA(())])(buf)


def _adamw(recv, w, m, v, tr):
    rows = w.shape[0]
    c1 = 1.0 - ADAM_B1 ** ADAM_STEP
    c2 = 1.0 - ADAM_B2 ** ADAM_STEP

    def body(r_ref, w_ref, m_ref, v_ref, g_o, d_o, m_o, v_o):
        g = r_ref[0]
        for k in range(1, N_DEV):
            g = g + r_ref[k]
        mm = ADAM_B1 * m_ref[...] + (1.0 - ADAM_B1) * g
        vv = ADAM_B2 * v_ref[...] + (1.0 - ADAM_B2) * (g * g)
        g_o[...] = g
        m_o[...] = mm
        v_o[...] = vv
        d_o[...] = -ADAM_LR * ((mm / c1) / (jnp.sqrt(vv / c2) + ADAM_EPS) + ADAM_WD * w_ref[...])

    spec = pl.BlockSpec((tr, LANES), lambda i: (i, 0))
    return pl.pallas_call(
        body, name="adamw", grid=(rows // tr,),
        in_specs=[pl.BlockSpec((N_DEV, tr, LANES), lambda i: (0, i, 0)), spec, spec, spec],
        out_specs=[spec] * 4, out_shape=[jax.ShapeDtypeStruct((rows, LANES), F32)] * 4,
        compiler_params=_cparams("arbitrary"))(recv, w, m, v)


def _to_rows(a):
    flat = a.reshape(-1)
    pad = (-flat.shape[0]) % LANES
    if pad:
        flat = jnp.concatenate([flat, jnp.zeros((pad,), flat.dtype)])
    return flat.reshape(-1, LANES)


def _n_rows(shape):
    return -(-int(np.prod(shape)) // LANES)


def _pack(arrays, total_rows):
    rows = [_to_rows(a) for a in arrays]
    used = sum(r.shape[0] for r in rows)
    if total_rows > used:
        rows.append(jnp.zeros((total_rows - used, LANES), F32))
    return jnp.concatenate(rows, axis=0)


def _unpack(buf, shapes):
    out, r0 = [], 0
    for s in shapes:
        n = int(np.prod(s))
        nr = _n_rows(s)
        out.append(buf[r0:r0 + nr].reshape(-1)[:n].reshape(s))
        r0 += nr
    return out


def _shard_views(name, full):
    if name == 'w_out':
        return full.reshape(N_DEV, full.shape[0] // N_DEV, full.shape[1])
    r, ccols = full.shape
    return full.reshape(r, N_DEV, ccols // N_DEV).transpose(1, 0, 2)


def _from_shards(name, stacked):
    if name == 'w_out':
        return stacked.reshape(-1, stacked.shape[-1])
    n, r, cc = stacked.shape
    return stacked.transpose(1, 0, 2).reshape(r, n * cc)


GROUPS_PER_BLK = N_GROUPS // N_COL_BLK


def _block_diag(t):
    eye = jnp.eye(GROUPS_PER_BLK, dtype=t.dtype)
    t4 = t.reshape(N_COL_BLK, GROUPS_PER_BLK, SSM_GROUP, SSM_STATE)
    return (t4[:, :, :, None, :] * eye[None, :, None, :, None]).reshape(N_COL_BLK, CH_BLK, COL_BLK)


def _diag_blocks(mat4):
    return jnp.stack([mat4[c, g * SSM_GROUP:(g + 1) * SSM_GROUP, g * SSM_STATE:(g + 1) * SSM_STATE]
                      for c in range(N_COL_BLK) for g in range(GROUPS_PER_BLK)])


def _step(x, loss_target, wts, moms, vels):
    seq = x.shape[1]
    n_valid = N_META + seq
    lp = -(-n_valid // 256) * 256
    tr = _pick(lp, [640, 256])
    tr_small = 256
    tq = _pick(lp, [640, 256])
    tk = tq

    shard_shapes = [wts[n].shape[-2:] for n in SHARDED]
    n_shard_rows = sum(_n_rows(s) for s in shard_shapes)
    gathered = _exchange("gather_weights", _pack([wts[n].reshape(wts[n].shape[-2:]) for n in SHARDED],
                                                 n_shard_rows), all_to_all=False)
    parts = [[] for _ in SHARDED]
    for dev in range(N_DEV):
        for i, a in enumerate(_unpack(gathered[dev], shard_shapes)):
            parts[i].append(a)
    full = {n: _from_shards(n, jnp.stack(parts[i])) for i, n in enumerate(SHARDED)}

    w_in_b = jnp.concatenate([_cols_in(full['w_in']), jnp.zeros((D_MODEL, D_IN_PAD - D_IN), F32)],
                             axis=1).astype(BF16)
    wq_b = _cols_q(full['w_q_up']).astype(BF16)
    wkv_b = _cols_kv(full['w_kv_up']).astype(BF16)
    wglu_b = full['w_glu'].astype(BF16)
    wo_b = full['w_out'].astype(BF16)
    pre_w, post_w = wts['pre_norm_w'], wts['post_norm_w']
    qw, kvw, aw, sw = wts['q_norm_w'], wts['kv_norm_w'], wts['attn_out_norm_w'], wts['ssm_out_norm_w']
    bglu, dvec = wts['b_glu'], wts['ssm_d']

    lseg = lp // N_SEG
    pos = _row_position(jnp.arange(lp, dtype=jnp.int32), lseg)
    inv = ROPE_THETA ** (-jnp.arange(HALF_ROPE, dtype=F32) / HALF_ROPE)
    ang = pos.astype(F32)[:, None] * inv[None, :]
    cos, sin = jnp.cos(ang), jnp.sin(ang)
    cos8, sin8 = jnp.tile(cos, (1, HEADS)), jnp.tile(sin, (1, HEADS))
    c32 = jnp.concatenate([cos, cos], axis=1)
    s32 = jnp.concatenate([-sin, sin], axis=1)
    p32 = jnp.asarray(np.roll(np.eye(QK_ROPE, dtype=np.float32), HALF_ROPE, axis=1))
    sum8 = jnp.asarray(np.tile(np.eye(QK_ROPE, dtype=np.float32), (HEADS, 1)))
    head_sum = jnp.asarray(np.repeat(np.eye(HEADS, dtype=np.float32), V_HEAD, axis=0))

    ng = 2 * N_GROUPS
    a_re3 = wts['ssm_a_re'].reshape(ng, 1, SSM_STATE)
    a_im3 = wts['ssm_a_im'].reshape(ng, 1, SSM_STATE)
    ldt3 = wts['ssm_log_dt'].reshape(ng, 1, 1)
    bt_re = wts['ssm_b_re'].reshape(2, N_GROUPS, SSM_STATE, SSM_GROUP).transpose(0, 1, 3, 2).reshape(
        ng, SSM_GROUP, SSM_STATE)
    bt_im = wts['ssm_b_im'].reshape(2, N_GROUPS, SSM_STATE, SSM_GROUP).transpose(0, 1, 3, 2).reshape(
        ng, SSM_GROUP, SSM_STATE)
    c_re = wts['ssm_c_re'].reshape(ng, SSM_GROUP, SSM_STATE)
    c_im = wts['ssm_c_im'].reshape(ng, SSM_GROUP, SSM_STATE)
    abar_re, abar_im, bbt_re, bbt_im = _ssm_disc(a_re3, a_im3, ldt3, bt_re, bt_im)

    def direction(t, d):
        return t[d * N_GROUPS:(d + 1) * N_GROUPS]

    def slab(t, d, sign=1.0):
        return jnp.broadcast_to(sign * direction(t, d).reshape(1, N_STATES), (N_SEG, N_STATES))

    w_re = [_block_diag(direction(bbt_re, d)).astype(BF16) for d in range(2)]
    w_im = [_block_diag(direction(bbt_im, d)).astype(BF16) for d in range(2)]
    cb_re = [_block_diag(direction(c_re, d)).astype(BF16) for d in range(2)]
    cb_im = [_block_diag(-direction(c_im, d)).astype(BF16) for d in range(2)]

    def to_rows(a):
        return a.reshape(N_SEG, lseg, a.shape[-1]).transpose(1, 0, 2).reshape(lp, a.shape[-1])

    def to_tokens(a):
        return a.reshape(lseg, N_SEG, a.shape[-1]).transpose(1, 0, 2).reshape(lp, a.shape[-1])

    pad = jnp.zeros((lp - n_valid, D_MODEL), F32)
    h = to_rows(jnp.concatenate([full['meta_tokens'], x[0], pad], axis=0))
    tgt = to_rows(jnp.concatenate([jnp.zeros((N_META, D_MODEL), F32), loss_target[0], pad], axis=0))

    ql, kvl, ag, su, sg, kr = _inproj(h, pre_w, w_in_b, tr)
    qn_b, qr1_b, qr2_b, kn_b, v_b, kr_b = _qkv_up(ql, kvl, kr, cos8, sin8, c32, s32, qw, kvw, wq_b, wkv_b, p32, tr)

    def heads(a, w):
        return a.reshape(lp, HEADS, w)

    nq, nk = lp // tq, lp // tk
    q_t = jnp.concatenate([heads(qn_b, 64), heads(qr1_b, 16), heads(qr2_b, 16)], axis=-1)
    k_t = jnp.concatenate([heads(kn_b, 64), jnp.broadcast_to(kr_b[:, None, :], (lp, HEADS, QK_ROPE))], axis=-1)
    v_t = heads(v_b, 64)
    vx_t = jnp.concatenate([v_t, jnp.ones((lp, HEADS, 1), BF16), jnp.zeros((lp, HEADS, LANES - V_HEAD - 1), BF16)],
                           axis=-1)
    qt4 = q_t.reshape(nq, tq, HEADS, QK_DIM).transpose(2, 0, 3, 1)
    vxt4 = vx_t.reshape(nk, tk, HEADS, LANES).transpose(2, 0, 3, 1)
    k_h = k_t.transpose(1, 0, 2)
    kt_h = k_t.transpose(1, 2, 0)
    v_h = v_t.transpose(1, 0, 2)
    ot_h, lse4 = _attn_fwd(qt4, k_h, vxt4, n_valid)
    o_flat = ot_h.transpose(2, 0, 1).reshape(lp, D_ATTN)
    ya = _attn_post(o_flat, ag, aw, tr)

    xs, ys = [], []
    for d in range(2):
        ar8, ai8 = slab(abar_re, d), slab(abar_im, d)
        ere, eim = _scan_ends(f"scan{d}_ends", su, w_re[d], w_im[d], ar8, ai8, d == 0)
        x_re, x_im, y_d = _scan_fwd(f"scan{d}", su, w_re[d], w_im[d], ar8, ai8, ere, eim, cb_re[d], cb_im[d],
                                    d == 0)
        xs += [x_re, x_im]
        ys.append(y_d)
    ypre, glu, ysn = _ssm_post(ys[0], ys[1], su, sg, wglu_b, bglu, sw, dvec, tr)

    dy, dout, loss, d_post = _out_loss(ya, ysn, h, tgt, wo_b, post_w, n_valid, tr_small)

    do_flat, dag, dysn, delta8, d_wo, d_aw = _out_bwd(dy, ya, ysn, o_flat, ag, wo_b, aw, head_sum, tr_small)
    dypre, dsg, d_wglu, d_bglu, d_sw, d_dvec = _ssm_post_bwd(dysn, glu, sg, ypre, su, wglu_b, sw, dvec, tr_small)

    dus, d_ct, d_wb, d_a8 = [], [], [], []
    for d in range(2):
        ar8, ai8c = slab(abar_re, d), slab(abar_im, d, -1.0)
        ere, eim = _scan_ends(f"scan_adj{d}_ends", dypre, cb_re[d], cb_im[d], ar8, ai8c, d != 0)
        du_d, dw_re, dw_im, dc_re, dc_im, da_re, da_im = _scan_bwd(
            f"scan_adj{d}", dypre, cb_re[d], cb_im[d], ar8, ai8c, ere, eim, su, w_re[d], w_im[d],
            xs[2 * d], xs[2 * d + 1], d != 0)
        dus.append(du_d)
        d_ct.append((dc_re, dc_im))
        d_wb.append((dw_re, dw_im))
        d_a8.append((da_re, da_im))

    dot4 = do_flat.astype(BF16).reshape(nq, tq, HEADS, V_HEAD).transpose(2, 0, 3, 1)
    dqt4, dkt_h, dvt_h = _attn_bwd(qt4, k_h, kt_h, v_h, dot4, lse4, delta8.T.reshape(HEADS, nq, 1, tq))
    dq_t = dqt4.transpose(1, 3, 0, 2).reshape(lp, HEADS, QK_DIM)
    dk_t = dkt_h.transpose(2, 0, 1)
    dqn = dq_t[:, :, :64].reshape(lp, 512)
    dr1 = dq_t[:, :, 64:80].reshape(lp, 128)
    dr2 = dq_t[:, :, 80:96].reshape(lp, 128)
    dkn = dk_t[:, :, :64].reshape(lp, 512)
    dkr8 = dk_t[:, :, 64:].reshape(lp, HEADS * QK_ROPE)
    dvf = dvt_h.transpose(2, 0, 1).reshape(lp, 512)
    dql, dkvl, dkrr, d_wq, d_wkv, d_qw, d_kvw = _qkv_up_bwd(
        dqn, dr1, dr2, dkn, dvf, dkr8, ql, kvl, cos8, sin8, c32, s32, qw, kvw, wq_b, wkv_b, p32, sum8, tr_small)
    dh, d_win, d_pre = _inproj_bwd(dql, dkvl, dag, dus[0], dus[1], dypre, dsg, dkrr, h, dout, pre_w, w_in_b, dvec,
                                   tr_small)
    dh = to_tokens(dh)

    def seg_sums(t):
        return t.reshape(N_SEG, N_GROUPS, SSM_STATE).transpose(1, 0, 2)

    da8_re = jnp.concatenate([seg_sums(d_a8[d][0]) for d in range(2)], axis=0)
    da8_im = jnp.concatenate([seg_sums(d_a8[d][1]) for d in range(2)], axis=0)
    dbb_re = jnp.concatenate([_diag_blocks(d_wb[d][0]) for d in range(2)], axis=0)
    dbb_im = jnp.concatenate([_diag_blocks(d_wb[d][1]) for d in range(2)], axis=0)
    g_are, g_aim, g_ldt, g_bt_re, g_bt_im = _ssm_disc_bwd(a_re3, a_im3, ldt3, bt_re, bt_im, da8_re, da8_im,
                                                          dbb_re, dbb_im)
    g_c_re = jnp.concatenate([_diag_blocks(d_ct[d][0]) for d in range(2)], axis=0)
    g_c_im = jnp.concatenate([-_diag_blocks(d_ct[d][1]) for d in range(2)], axis=0)

    def b_layout(t):
        return t.reshape(2, N_GROUPS, SSM_GROUP, SSM_STATE).transpose(0, 1, 3, 2)

    local = {
        'meta_tokens': dh[:N_META],
        'pre_norm_w': d_pre, 'post_norm_w': d_post,
        'w_in': _cols_in_inv(d_win[:, :D_IN]),
        'q_norm_w': d_qw, 'w_q_up': _cols_q_inv(d_wq),
        'kv_norm_w': d_kvw, 'w_kv_up': _cols_kv_inv(d_wkv),
        'attn_out_norm_w': d_aw,
        'ssm_a_re': g_are, 'ssm_a_im': g_aim, 'ssm_log_dt': g_ldt,
        'ssm_b_re': b_layout(g_bt_re), 'ssm_b_im': b_layout(g_bt_im), 'ssm_c_re': g_c_re, 'ssm_c_im': g_c_im,
        'ssm_d': d_dvec, 'w_glu': d_wglu, 'b_glu': d_bglu, 'ssm_out_norm_w': d_sw, 'w_out': d_wo,
    }

    order = SHARDED + [n for n in WEIGHTS if n not in SHARDED]
    shapes = [wts[n].shape for n in order]
    used_rows = sum(_n_rows(s) for s in shapes)
    tr_adam = 512
    total_rows = -(-used_rows // tr_adam) * tr_adam
    send = []
    for dev in range(N_DEV):
        arrs = [_shard_views(n, local[n])[dev] if n in SHARDED else local[n] for n in order]
        send.append(_pack(arrs, total_rows))
    recv = _exchange("exchange_grads", jnp.stack(send), all_to_all=True)
    packed = [_pack([src[n] for n in order], total_rows) for src in (wts, moms, vels)]
    g_p, d_p, m_p, v_p = _adamw(recv, *packed, tr_adam)
    grads, deltas, new_m, new_v = (dict(zip(order, _unpack(b, shapes))) for b in (g_p, d_p, m_p, v_p))

    loss = lax.psum(loss[0, 0], ("x", "y", "c"))
    grad_x = dh[N_META:n_valid][None]
    return (loss, grad_x, *[grads[n] for n in WEIGHTS], *[deltas[n] for n in WEIGHTS],
            *[new_m[n] for n in WEIGHTS], *[new_v[n] for n in WEIGHTS])


def kernel(x, meta_tokens, pre_norm_w, post_norm_w, w_in, q_norm_w, w_q_up, kv_norm_w, w_kv_up, attn_out_norm_w, ssm_a_re, ssm_a_im, ssm_log_dt, ssm_b_re, ssm_b_im, ssm_c_re, ssm_c_im, ssm_d, w_glu, b_glu, ssm_out_norm_w, w_out, loss_target, m_meta_tokens, m_pre_norm_w, m_post_norm_w, m_w_in, m_q_norm_w, m_w_q_up, m_kv_norm_w, m_w_kv_up, m_attn_out_norm_w, m_ssm_a_re, m_ssm_a_im, m_ssm_log_dt, m_ssm_b_re, m_ssm_b_im, m_ssm_c_re, m_ssm_c_im, m_ssm_d, m_w_glu, m_b_glu, m_ssm_out_norm_w, m_w_out, v_meta_tokens, v_pre_norm_w, v_post_norm_w, v_w_in, v_q_norm_w, v_w_q_up, v_kv_norm_w, v_w_kv_up, v_attn_out_norm_w, v_ssm_a_re, v_ssm_a_im, v_ssm_log_dt, v_ssm_b_re, v_ssm_b_im, v_ssm_c_re, v_ssm_c_im, v_ssm_d, v_w_glu, v_b_glu, v_ssm_out_norm_w, v_w_out):
    wts = dict(zip(WEIGHTS, (meta_tokens, pre_norm_w, post_norm_w, w_in, q_norm_w, w_q_up, kv_norm_w, w_kv_up,
                             attn_out_norm_w, ssm_a_re, ssm_a_im, ssm_log_dt, ssm_b_re, ssm_b_im, ssm_c_re,
                             ssm_c_im, ssm_d, w_glu, b_glu, ssm_out_norm_w, w_out)))
    moms = dict(zip(WEIGHTS, (m_meta_tokens, m_pre_norm_w, m_post_norm_w, m_w_in, m_q_norm_w, m_w_q_up,
                              m_kv_norm_w, m_w_kv_up, m_attn_out_norm_w, m_ssm_a_re, m_ssm_a_im, m_ssm_log_dt,
                              m_ssm_b_re, m_ssm_b_im, m_ssm_c_re, m_ssm_c_im, m_ssm_d, m_w_glu, m_b_glu,
                              m_ssm_out_norm_w, m_w_out)))
    vels = dict(zip(WEIGHTS, (v_meta_tokens, v_pre_norm_w, v_post_norm_w, v_w_in, v_q_norm_w, v_w_q_up,
                              v_kv_norm_w, v_w_kv_up, v_attn_out_norm_w, v_ssm_a_re, v_ssm_a_im, v_ssm_log_dt,
                              v_ssm_b_re, v_ssm_b_im, v_ssm_c_re, v_ssm_c_im, v_ssm_d, v_w_glu, v_b_glu,
                              v_ssm_out_norm_w, v_w_out)))
    return _step(x, loss_target, wts, moms, vels)
```

```python
import functools
import math

import numpy as np
import jax
import jax.numpy as jnp
from jax import lax
from jax.experimental import pallas as pl
from jax.experimental.pallas import tpu as pltpu

F32 = jnp.float32
BF16 = jnp.bfloat16

D_MODEL = 1024
N_META = 16
EPS = 1e-6
HEADS = 8
QK_NOPE = 64
QK_ROPE = 32
HALF_ROPE = QK_ROPE // 2
QK_DIM = QK_NOPE + QK_ROPE
V_HEAD = 64
Q_LORA = 256
KV_LORA = 128
D_ATTN = HEADS * V_HEAD
D_SSM = 512
SSM_GROUP = 16
N_GROUPS = D_SSM // SSM_GROUP
SSM_STATE = 64
N_STATES = N_GROUPS * SSM_STATE
ROPE_THETA = 10000.0
D_IN = Q_LORA + KV_LORA + QK_ROPE + D_ATTN + 2 * D_SSM
D_IN_PAD = 2048
N_DEV = 8
N_SEG = 8
COL_BLK = 512
LANES = 128

ADAM_LR = 0.001
ADAM_B1 = 0.9
ADAM_B2 = 0.999
ADAM_EPS = 1e-08
ADAM_WD = 0.01
ADAM_STEP = 10

VMEM_LIMIT_V7X = 56 * 1024 * 1024
LOG2E = 1.0 / math.log(2.0)
Q_SCALE = LOG2E / math.sqrt(QK_DIM)
ATTN_UNROLL = 4
ATTN_BWD_UNROLL = 4

WEIGHTS = ['meta_tokens', 'pre_norm_w', 'post_norm_w', 'w_in', 'q_norm_w', 'w_q_up', 'kv_norm_w', 'w_kv_up',
           'attn_out_norm_w', 'ssm_a_re', 'ssm_a_im', 'ssm_log_dt', 'ssm_b_re', 'ssm_b_im', 'ssm_c_re', 'ssm_c_im',
           'ssm_d', 'w_glu', 'b_glu', 'ssm_out_norm_w', 'w_out']
SHARDED = ['w_in', 'w_q_up', 'w_kv_up', 'w_glu', 'w_out', 'meta_tokens']

def _cols_in(w):
    return jnp.concatenate([w[:, 0:384], w[:, 416:D_IN], w[:, 384:416]], axis=1)


def _cols_in_inv(w):
    return jnp.concatenate([w[:, 0:384], w[:, D_IN - QK_ROPE:D_IN], w[:, 384:D_IN - QK_ROPE]], axis=1)


def _cols_q(w):
    t = w.reshape(w.shape[0], HEADS, QK_DIM)
    return jnp.concatenate([t[:, :, 0:64].reshape(-1, 512), t[:, :, 64:80].reshape(-1, 128),
                            t[:, :, 80:96].reshape(-1, 128)], axis=1)


def _cols_q_inv(w):
    r = w.shape[0]
    return jnp.concatenate([w[:, 0:512].reshape(r, HEADS, 64), w[:, 512:640].reshape(r, HEADS, 16),
                            w[:, 640:768].reshape(r, HEADS, 16)], axis=2).reshape(r, HEADS * QK_DIM)


def _cols_kv(w):
    t = w.reshape(w.shape[0], HEADS, 128)
    return jnp.concatenate([t[:, :, 0:64].reshape(-1, 512), t[:, :, 64:128].reshape(-1, 512)], axis=1)


def _cols_kv_inv(w):
    r = w.shape[0]
    return jnp.concatenate([w[:, 0:512].reshape(r, HEADS, 64), w[:, 512:1024].reshape(r, HEADS, 64)],
                           axis=2).reshape(r, HEADS * 128)


def _pick(n, cands):
    for c in cands:
        if n % c == 0:
            return c
    raise ValueError(f"no tile for {n}")


def _cparams(*sem):
    return pltpu.CompilerParams(dimension_semantics=sem, vmem_limit_bytes=VMEM_LIMIT_V7X)


def _mm(a, b):
    return jnp.dot(a.astype(BF16), b.astype(BF16), preferred_element_type=F32)


def _mm_nt(a, b):
    return lax.dot_general(a.astype(BF16), b.astype(BF16), (((1,), (1,)), ((), ())), preferred_element_type=F32)


def _mm_tn(a, b):
    return lax.dot_general(a.astype(BF16), b.astype(BF16), (((0,), (0,)), ((), ())), preferred_element_type=F32)


def _mm_exact(a, b):
    return jnp.dot(a, b, precision=lax.Precision.HIGHEST, preferred_element_type=F32)


def _rms(x):
    return lax.rsqrt(jnp.mean(x * x, axis=-1, keepdims=True) + EPS)


def _rms_bwd(dy, x, r, w):
    xh = x * r
    g = dy * w
    dx = r * (g - xh * jnp.mean(g * xh, axis=-1, keepdims=True))
    dw = jnp.sum(dy * xh, axis=0, keepdims=True)
    return dx, dw


def _sigmoid(z):
    return 1.0 / (1.0 + jnp.exp(-z))


def _silu_and_grad(z):
    s = _sigmoid(z)
    return z * s, s * (1.0 + z * (1.0 - s))


_GELU_C = math.sqrt(2.0 / math.pi)


def _gelu_and_grad(x):
    x2 = x * x
    t = jnp.tanh(_GELU_C * (x + 0.044715 * x * x2))
    val = 0.5 * x * (1.0 + t)
    grad = 0.5 * (1.0 + t) + 0.5 * x * (1.0 - t * t) * _GELU_C * (1.0 + 3.0 * 0.044715 * x2)
    return val, grad


def _acc(ref, val, first):
    @pl.when(first)
    def _():
        ref[...] = val

    @pl.when(jnp.logical_not(first))
    def _():
        ref[...] += val


def _rows_call(name, body, tr, row_ins, full_ins, row_outs, acc_outs):
    lp = row_ins[0].shape[0]
    in_specs = [pl.BlockSpec((tr, a.shape[1]), lambda i: (i, 0)) for a in row_ins]
    in_specs += [pl.BlockSpec(a.shape, lambda i, n=a.ndim: (0,) * n) for a in full_ins]
    out_specs = [pl.BlockSpec((tr, c), lambda i: (i, 0)) for c, _ in row_outs]
    out_specs += [pl.BlockSpec(s, lambda i, n=len(s): (0,) * n) for s, _ in acc_outs]
    out_shape = [jax.ShapeDtypeStruct((lp, c), dt) for c, dt in row_outs]
    out_shape += [jax.ShapeDtypeStruct(s, dt) for s, dt in acc_outs]
    return pl.pallas_call(
        body, name=name, grid=(lp // tr,), in_specs=in_specs, out_specs=out_specs, out_shape=out_shape,
        compiler_params=_cparams("arbitrary"))(*row_ins, *full_ins)


def _inproj(h, pre_w, w_in_b, tr):
    def body(h_ref, pw_ref, w_ref, ql, kvl, ag, su, sg, kr):
        x = h_ref[...]
        xn = x * _rms(x) * pw_ref[...]
        pr = _mm(xn, w_ref[...])
        ql[...] = pr[:, 0:256]
        kvl[...] = pr[:, 256:384]
        ag[...] = pr[:, 384:896]
        su[...] = pr[:, 896:1408]
        sg[...] = pr[:, 1408:1920]
        kr[...] = pr[:, 1920:1952]

    return _rows_call("inproj", body, tr, [h], [pre_w, w_in_b],
                      [(256, F32), (128, F32), (512, F32), (512, F32), (512, F32), (32, F32)], [])


def _qkv_up(ql, kvl, kr, cos8, sin8, c32, s32, qw, kvw, wq_b, wkv_b, p32, tr):
    def body(ql_ref, kvl_ref, kr_ref, cos_ref, sin_ref, c32_ref, s32_ref, qw_ref, kvw_ref, wq_ref, wkv_ref, p_ref,
             qn_o, qr1_o, qr2_o, kn_o, v_o, kr_o):
        x = ql_ref[...]
        q = _mm(x * _rms(x) * qw_ref[...], wq_ref[...]) * Q_SCALE
        r1, r2 = q[:, 512:640], q[:, 640:768]
        cs, sn = cos_ref[...], sin_ref[...]
        qn_o[...] = q[:, 0:512].astype(BF16)
        qr1_o[...] = (r1 * cs - r2 * sn).astype(BF16)
        qr2_o[...] = (r2 * cs + r1 * sn).astype(BF16)
        x = kvl_ref[...]
        kv = _mm(x * _rms(x) * kvw_ref[...], wkv_ref[...])
        kn_o[...] = kv[:, 0:512].astype(BF16)
        v_o[...] = kv[:, 512:1024].astype(BF16)
        x = kr_ref[...]
        kr_o[...] = (x * c32_ref[...] + _mm_exact(x, p_ref[...]) * s32_ref[...]).astype(BF16)

    return _rows_call("qkv_up", body, tr, [ql, kvl, kr, cos8, sin8, c32, s32], [qw, kvw, wq_b, wkv_b, p32],
                      [(512, BF16), (128, BF16), (128, BF16), (512, BF16), (512, BF16), (32, BF16)], [])


def _row_position(row, lseg):
    return (row & (N_SEG - 1)) * lseg + (row >> 3)


def _first_padded_tile(n_valid, lp, tile):
    lseg = lp // N_SEG
    t0 = n_valid - (N_SEG - 1) * lseg
    return (t0 * N_SEG + N_SEG - 1) // tile if n_valid < lp else lp // tile


def _attn_fwd(qt, k, vxt, n_valid):
    _, nq, _, tq = qt.shape
    _, nk, _, tk = vxt.shape
    lp = k.shape[1]
    lseg = lp // N_SEG
    n_plain = max(0, min(nk, _first_padded_tile(n_valid, lp, tk)))

    def body(q_ref, k_ref, v_ref, o_ref, lse_ref, m_s, acc_s):
        m_s[...] = jnp.full(m_s.shape, -1e30, F32)
        acc_s[...] = jnp.zeros(acc_s.shape, F32)
        qq = q_ref[0, 0]

        def chunk(c, padded):
            r0 = pl.multiple_of(c * tk, tk)
            st = _mm(k_ref[0, pl.ds(r0, tk), :], qq)
            if padded:
                row = r0 + lax.broadcasted_iota(jnp.int32, (tk, tq), 0)
                st = jnp.where(_row_position(row, lseg) < n_valid, st, -1e30)
            m_old = m_s[...]
            m_new = jnp.maximum(m_old, jnp.max(st, axis=0, keepdims=True))
            pt = jnp.exp2(st - m_new)
            acc_s[...] = jnp.exp2(m_old - m_new) * acc_s[...] + _mm(v_ref[0, c], pt)
            m_s[...] = m_new

        def plain(c, carry):
            chunk(c, False)
            return carry

        n_loop = n_plain - n_plain % ATTN_UNROLL
        if n_loop:
            lax.fori_loop(0, n_loop, plain, 0, unroll=ATTN_UNROLL)
        for c in range(n_loop, nk):
            chunk(c, c >= n_plain)
        acc = acc_s[...]
        l = acc[V_HEAD:V_HEAD + 1, :]
        o_ref[0] = acc[:V_HEAD, :] / l
        lse_ref[0, 0] = m_s[...] + jnp.log2(l)

    return pl.pallas_call(
        body, name="attn_fwd", grid=(HEADS, nq),
        in_specs=[pl.BlockSpec((1, 1, QK_DIM, tq), lambda h, i: (h, i, 0, 0)),
                  pl.BlockSpec((1, lp, QK_DIM), lambda h, i: (h, 0, 0)),
                  pl.BlockSpec((1, nk, LANES, tk), lambda h, i: (h, 0, 0, 0))],
        out_specs=[pl.BlockSpec((1, V_HEAD, tq), lambda h, i: (h, 0, i)),
                   pl.BlockSpec((1, 1, 1, tq), lambda h, i: (h, i, 0, 0))],
        out_shape=[jax.ShapeDtypeStruct((HEADS, V_HEAD, lp), F32), jax.ShapeDtypeStruct((HEADS, nq, 1, tq), F32)],
        scratch_shapes=[pltpu.VMEM((1, tq), F32), pltpu.VMEM((LANES, tq), F32)],
        compiler_params=_cparams("arbitrary", "arbitrary"))(qt, k, vxt)


def _attn_post(o_flat, ag, aw, tr):
    def body(o_ref, g_ref, w_ref, ya):
        t = o_ref[...] * _silu_and_grad(g_ref[...])[0]
        ya[...] = t * _rms(t) * w_ref[...]

    return _rows_call("attn_post", body, tr, [o_flat, ag], [aw], [(512, F32)], [])[0]


def _scan_tiles(lp):
    lseg = lp // N_SEG
    tt = _pick(lseg, [208, 48, 32, 16, 8, 4, 2, 1])
    return lseg, tt, lseg // tt


def _cmul(ar, ai, br, bi):
    return ar * br - ai * bi, ar * bi + ai * br


N_COL_BLK = N_STATES // COL_BLK
CH_BLK = D_SSM // N_COL_BLK


def _scan_steps(tt, forward, bre_ref, bim_ref, ar, ai, carry, visit):
    def step(s, c):
        r0 = pl.multiple_of((s if forward else tt - 1 - s) * N_SEG, N_SEG)
        pr, pi = _cmul(ar, ai, c[0], c[1])
        xr = pr + bre_ref[pl.ds(r0, N_SEG), :]
        xi = pi + bim_ref[pl.ds(r0, N_SEG), :]
        return (xr, xi) + tuple(visit(r0, (xr, xi), (c[0], c[1]), c[2:]))

    return lax.fori_loop(0, tt, step, carry, unroll=4 if tt % 4 == 0 else 1)


def _segment_starts(lseg, forward, ar, ai, ere_ref, eim_ref, s_re, s_im):
    a1r, a1i = ar[0:1, :], ai[0:1, :]
    pr, pi = jnp.ones_like(a1r), jnp.zeros_like(a1i)
    br, bi = a1r, a1i
    n = lseg
    while n:
        if n & 1:
            pr, pi = _cmul(pr, pi, br, bi)
        n >>= 1
        if n:
            br, bi = _cmul(br, bi, br, bi)
    cr, ci = jnp.zeros_like(a1r), jnp.zeros_like(a1i)
    for j in (range(N_SEG) if forward else range(N_SEG - 1, -1, -1)):
        s_re[j:j + 1, :] = cr
        s_im[j:j + 1, :] = ci
        nr, ni = _cmul(pr, pi, cr, ci)
        cr = nr + ere_ref[j:j + 1, :]
        ci = ni + eim_ref[j:j + 1, :]


def _scan_specs(lp, forward):
    lseg, tt, nt = _scan_tiles(lp)

    def tile(t):
        return t if forward else nt - 1 - t

    rows = lambda w: pl.BlockSpec((tt * N_SEG, w), lambda cb, t: (tile(t), cb))
    proj = pl.BlockSpec((1, CH_BLK, COL_BLK), lambda cb, t: (cb, 0, 0))
    slab = pl.BlockSpec((N_SEG, COL_BLK), lambda cb, t: (0, cb))
    return lseg, tt, nt, rows, proj, slab


def _scan_ends(name, urows, wre4, wim4, ar8, ai8, forward):
    lp = urows.shape[0]
    lseg, tt, nt, rows, proj, slab = _scan_specs(lp, forward)

    def body(u_ref, wre_ref, wim_ref, ar_ref, ai_ref, ere_o, eim_o, bre_s, bim_s, cr_s, ci_s):
        t = pl.program_id(1)

        @pl.when(t == 0)
        def _():
            cr_s[...] = jnp.zeros(cr_s.shape, F32)
            ci_s[...] = jnp.zeros(ci_s.shape, F32)

        u = u_ref[...]
        bre_s[...] = _mm(u, wre_ref[0])
        bim_s[...] = _mm(u, wim_ref[0])
        cr, ci = _scan_steps(tt, forward, bre_s, bim_s, ar_ref[...], ai_ref[...], (cr_s[...], ci_s[...]),
                             lambda r0, x, x_prev, extra: ())
        cr_s[...] = cr
        ci_s[...] = ci

        @pl.when(t == nt - 1)
        def _():
            ere_o[...] = cr
            eim_o[...] = ci

    return pl.pallas_call(
        body, name=name, grid=(N_COL_BLK, nt), in_specs=[rows(CH_BLK), proj, proj, slab, slab],
        out_specs=[slab, slab], out_shape=[jax.ShapeDtypeStruct((N_SEG, N_STATES), F32)] * 2,
        scratch_shapes=[pltpu.VMEM((tt * N_SEG, COL_BLK), F32)] * 2 + [pltpu.VMEM((N_SEG, COL_BLK), F32)] * 2,
        compiler_params=_cparams("arbitrary", "arbitrary"))(urows, wre4, wim4, ar8, ai8)


def _scan_fwd(name, urows, wre4, wim4, ar8, ai8, ere, eim, cre4, cim4, forward):
    lp = urows.shape[0]
    lseg, tt, nt, rows, proj, slab = _scan_specs(lp, forward)

    def body(u_ref, wre_ref, wim_ref, ar_ref, ai_ref, ere_ref, eim_ref, cre_ref, cim_ref,
             xre_o, xim_o, y_o, bre_s, bim_s, cr_s, ci_s):
        ar, ai = ar_ref[...], ai_ref[...]

        @pl.when(pl.program_id(1) == 0)
        def _():
            _segment_starts(lseg, forward, ar, ai, ere_ref, eim_ref, cr_s, ci_s)

        u = u_ref[...]
        bre_s[...] = _mm(u, wre_ref[0])
        bim_s[...] = _mm(u, wim_ref[0])

        def visit(r0, x, x_prev, extra):
            xre_o[pl.ds(r0, N_SEG), :] = x[0]
            xim_o[pl.ds(r0, N_SEG), :] = x[1]
            return ()

        cr, ci = _scan_steps(tt, forward, bre_s, bim_s, ar, ai, (cr_s[...], ci_s[...]), visit)
        cr_s[...] = cr
        ci_s[...] = ci
        y_o[...] = _mm_nt(xre_o[...], cre_ref[0]) + _mm_nt(xim_o[...], cim_ref[0])

    return pl.pallas_call(
        body, name=name, grid=(N_COL_BLK, nt),
        in_specs=[rows(CH_BLK), proj, proj, slab, slab, slab, slab, proj, proj],
        out_specs=[rows(COL_BLK), rows(COL_BLK), rows(CH_BLK)],
        out_shape=[jax.ShapeDtypeStruct((lp, N_STATES), F32)] * 2 + [jax.ShapeDtypeStruct((lp, D_SSM), F32)],
        scratch_shapes=[pltpu.VMEM((tt * N_SEG, COL_BLK), F32)] * 2 + [pltpu.VMEM((N_SEG, COL_BLK), F32)] * 2,
        compiler_params=_cparams("arbitrary", "arbitrary"))(urows, wre4, wim4, ar8, ai8, ere, eim, cre4, cim4)


def _scan_bwd(name, dyrows, cre4, cim4, ar8, ai8, ere, eim, urows, wre4, wim4, xre, xim, forward):
    lp = urows.shape[0]
    lseg, tt, nt, rows, proj, slab = _scan_specs(lp, forward)

    def body(dy_ref, cre_ref, cim_ref, ar_ref, ai_ref, ere_ref, eim_ref, u_ref, wre_ref, wim_ref, xre_ref, xim_ref,
             du_o, dwre_o, dwim_o, dcre_o, dcim_o, dare_o, daim_o, bre_s, bim_s, gre_s, gim_s, cr_s, ci_s):
        t = pl.program_id(1)
        ar, ai = ar_ref[...], ai_ref[...]

        @pl.when(t == 0)
        def _():
            _segment_starts(lseg, forward, ar, ai, ere_ref, eim_ref, cr_s, ci_s)
            dare_o[...] = jnp.zeros(dare_o.shape, F32)
            daim_o[...] = jnp.zeros(daim_o.shape, F32)

        dy = dy_ref[...]
        bre_s[...] = _mm(dy, cre_ref[0])
        bim_s[...] = _mm(dy, cim_ref[0])

        def visit(r0, g, g_prev, sums):
            gre_s[pl.ds(r0, N_SEG), :] = g[0]
            gim_s[pl.ds(r0, N_SEG), :] = g[1]
            fr = xre_ref[pl.ds(r0, N_SEG), :]
            fi = xim_ref[pl.ds(r0, N_SEG), :]
            pr, pi = g_prev
            return sums[0] + fr * pr + fi * pi, sums[1] + fr * pi - fi * pr

        out = _scan_steps(tt, forward, bre_s, bim_s, ar, ai, (cr_s[...], ci_s[...], dare_o[...], daim_o[...]), visit)
        cr_s[...] = out[0]
        ci_s[...] = out[1]
        dare_o[...] = out[2]
        daim_o[...] = out[3]
        gre, gim = gre_s[...], gim_s[...]
        du_o[...] = _mm_nt(gre, wre_ref[0]) + _mm_nt(gim, wim_ref[0])
        u = u_ref[...]
        first = t == 0
        _acc(dwre_o, _mm_tn(u, gre)[None], first)
        _acc(dwim_o, _mm_tn(u, gim)[None], first)
        _acc(dcre_o, _mm_tn(dy, xre_ref[...])[None], first)
        _acc(dcim_o, _mm_tn(dy, xim_ref[...])[None], first)

    big = pltpu.VMEM((tt * N_SEG, COL_BLK), F32)
    small = pltpu.VMEM((N_SEG, COL_BLK), F32)
    return pl.pallas_call(
        body, name=name, grid=(N_COL_BLK, nt),
        in_specs=[rows(CH_BLK), proj, proj, slab, slab, slab, slab, rows(CH_BLK), proj, proj,
                  rows(COL_BLK), rows(COL_BLK)],
        out_specs=[rows(CH_BLK), proj, proj, proj, proj, slab, slab],
        out_shape=[jax.ShapeDtypeStruct((lp, D_SSM), F32)]
        + [jax.ShapeDtypeStruct((N_COL_BLK, CH_BLK, COL_BLK), F32)] * 4
        + [jax.ShapeDtypeStruct((N_SEG, N_STATES), F32)] * 2,
        scratch_shapes=[big, big, big, big, small, small],
        compiler_params=_cparams("arbitrary", "arbitrary"))(
            dyrows, cre4, cim4, ar8, ai8, ere, eim, urows, wre4, wim4, xre, xim)


def _ssm_post(yf, yb, u, sg, wglu_b, bglu, sw, dvec, tr):
    def body(yf_ref, yb_ref, u_ref, g_ref, w_ref, b_ref, sw_ref, d_ref, ypre_o, glu_o, ysn_o):
        ypre = yf_ref[...] + yb_ref[...] + d_ref[...] * u_ref[...]
        ypre_o[...] = ypre
        glu = _mm(_gelu_and_grad(ypre)[0], w_ref[...]) + b_ref[...]
        glu_o[...] = glu
        t = glu[:, :D_SSM] * _sigmoid(glu[:, D_SSM:]) * _silu_and_grad(g_ref[...])[0]
        ysn_o[...] = t * _rms(t) * sw_ref[...]

    return _rows_call("ssm_post", body, tr, [yf, yb, u, sg], [wglu_b, bglu, sw, dvec],
                      [(512, F32), (1024, F32), (512, F32)], [])


def _out_loss(ya, ysn, h, tgt, wo_b, post_w, n_valid, tr):
    lseg = h.shape[0] // N_SEG

    def body(ya_ref, ys_ref, h_ref, t_ref, w_ref, pw_ref, dy_o, dout_o, loss_o, dpw_o):
        i = pl.program_id(0)
        y = _mm(ya_ref[...], w_ref[0:D_ATTN, :]) + _mm(ys_ref[...], w_ref[D_ATTN:, :])
        r = _rms(y)
        pw = pw_ref[...]
        out = h_ref[...] + y * r * pw
        pos = _row_position(i * tr + lax.broadcasted_iota(jnp.int32, (tr, 1), 0), lseg)
        valid = jnp.logical_and(pos >= N_META, pos < n_valid)
        diff = jnp.where(valid, out - t_ref[...], 0.0)
        dout = diff * (1.0 / D_MODEL)
        dy, dpw = _rms_bwd(dout, y, r, pw)
        dy_o[...] = dy
        dout_o[...] = dout
        _acc(loss_o, 0.5 * jnp.sum(jnp.sum(diff * diff, axis=1, keepdims=True), axis=0, keepdims=True)
             * (1.0 / D_MODEL), i == 0)
        _acc(dpw_o, dpw, i == 0)

    return _rows_call("out_loss", body, tr, [ya, ysn, h, tgt], [wo_b, post_w], [(1024, F32), (1024, F32)],
                      [((1, 1), F32), ((1, D_MODEL), F32)])


def _out_bwd(dy, ya, ysn, o_flat, ag, wo_b, aw, head_sum, tr):
    def body(dy_ref, ya_ref, ys_ref, o_ref, g_ref, w_ref, aw_ref, hs_ref, do_o, dag_o, dysn_o, dl_o, dwo_o, daw_o):
        i = pl.program_id(0)
        dy = dy_ref[...]
        dcat = _mm_nt(dy, w_ref[...])
        cat = jnp.concatenate([ya_ref[...], ys_ref[...]], axis=1)
        _acc(dwo_o, _mm_tn(cat, dy), i == 0)
        dysn_o[...] = dcat[:, D_ATTN:]
        o = o_ref[...]
        sl, dsl = _silu_and_grad(g_ref[...])
        t = o * sl
        dt, daw = _rms_bwd(dcat[:, :D_ATTN], t, _rms(t), aw_ref[...])
        _acc(daw_o, daw, i == 0)
        do = dt * sl
        do_o[...] = do
        dag_o[...] = dt * o * dsl
        dl_o[...] = _mm_exact(do * o, hs_ref[...])

    return _rows_call("out_bwd", body, tr, [dy, ya, ysn, o_flat, ag], [wo_b, aw, head_sum],
                      [(512, F32), (512, F32), (512, F32), (HEADS, F32)],
                      [((D_MODEL, D_MODEL), F32), ((1, D_ATTN), F32)])


def _ssm_post_bwd(dysn, glu, sg, ypre, u, wglu_b, sw, dvec, tr):
    def body(d_ref, glu_ref, sg_ref, y_ref, u_ref, w_ref, sw_ref, dv_ref,
             dyp_o, dsg_o, dwg_o, dbg_o, dsw_o, dd_o):
        i = pl.program_id(0)
        glu = glu_ref[...]
        a, b = glu[:, :D_SSM], glu[:, D_SSM:]
        sb = _sigmoid(b)
        ys = a * sb
        sl, dsl = _silu_and_grad(sg_ref[...])
        t = ys * sl
        dt, dsw = _rms_bwd(d_ref[...], t, _rms(t), sw_ref[...])
        _acc(dsw_o, dsw, i == 0)
        dsg_o[...] = dt * ys * dsl
        dys = dt * sl
        dglu = jnp.concatenate([dys * sb, dys * a * sb * (1.0 - sb)], axis=1)
        _acc(dbg_o, jnp.sum(dglu, axis=0, keepdims=True), i == 0)
        gel, dgel = _gelu_and_grad(y_ref[...])
        _acc(dwg_o, _mm_tn(gel, dglu), i == 0)
        dyp = _mm_nt(dglu, w_ref[...]) * dgel
        dyp_o[...] = dyp
        _acc(dd_o, jnp.sum(dyp * u_ref[...], axis=0, keepdims=True), i == 0)

    return _rows_call("ssm_post_bwd", body, tr, [dysn, glu, sg, ypre, u], [wglu_b, sw, dvec],
                      [(512, F32), (512, F32)],
                      [((D_SSM, 2 * D_SSM), F32), ((1, 2 * D_SSM), F32), ((1, D_SSM), F32), ((1, D_SSM), F32)])


def _attn_bwd(qt, k, kt, v, dot, lse_t, delta_t, tk):
    _, nq, _, tq = qt.shape
    lp = k.shape[1]
    nk = lp // tk
    assert lse_t.shape == (HEADS, nq, 1, tq) and delta_t.shape == (HEADS, nq, 1, tq)

    def body(q_ref, k_ref, kt_ref, v_ref, do_ref, lse_ref, dl_ref, dq_o, dk_o, dv_o, dk_s, dv_s):
        @pl.when(pl.program_id(1) == 0)
        def _():
            dq_o[...] = jnp.zeros(dq_o.shape, F32)

        dk_s[...] = jnp.zeros(dk_s.shape, F32)
        dv_s[...] = jnp.zeros(dv_s.shape, F32)
        kk = k_ref[0]
        kkt = kt_ref[0]
        vv = v_ref[0]

        def chunk(c, carry):
            qq = q_ref[0, c]
            dd = do_ref[0, c]
            pt = jnp.exp2(_mm(kk, qq) - lse_ref[0, c])
            dv_s[...] += _mm_nt(dd, pt)
            dst = (pt * (_mm(vv, dd) - dl_ref[0, c])).astype(BF16)
            dk_s[...] += _mm_nt(qq, dst)
            dq_o[0, c] += _mm(kkt, dst)
            return carry

        n_loop = nq - nq % ATTN_BWD_UNROLL
        if n_loop:
            lax.fori_loop(0, n_loop, chunk, 0, unroll=ATTN_BWD_UNROLL)
        for c in range(n_loop, nq):
            chunk(c, 0)
        dk_o[0] = dk_s[...]
        dv_o[0] = dv_s[...]

    head = lambda w: pl.BlockSpec((1, nq, w, tq), lambda h, j: (h, 0, 0, 0))
    rows = lambda w: pl.BlockSpec((1, tk, w), lambda h, j: (h, j, 0))
    cols = lambda w: pl.BlockSpec((1, w, tk), lambda h, j: (h, 0, j))
    return pl.pallas_call(
        body, name="attn_bwd", grid=(HEADS, nk),
        in_specs=[head(QK_DIM), rows(QK_DIM), cols(QK_DIM), rows(V_HEAD), head(V_HEAD), head(1), head(1)],
        out_specs=[head(QK_DIM), cols(QK_DIM), cols(V_HEAD)],
        out_shape=[jax.ShapeDtypeStruct((HEADS, nq, QK_DIM, tq), F32), jax.ShapeDtypeStruct((HEADS, QK_DIM, lp), F32),
                   jax.ShapeDtypeStruct((HEADS, V_HEAD, lp), F32)],
        scratch_shapes=[pltpu.VMEM((QK_DIM, tk), F32), pltpu.VMEM((V_HEAD, tk), F32)],
        compiler_params=_cparams("arbitrary", "arbitrary"))(qt, k, kt, v, dot, lse_t, delta_t)


def _qkv_up_bwd(dqn, dr1, dr2, dkn, dv, dkr8, ql, kvl, cos8, sin8, c32, s32, qw, kvw, wq_b, wkv_b, p32, sum8, tr):
    def body(dqn_ref, dr1_ref, dr2_ref, dkn_ref, dv_ref, dkr_ref, ql_ref, kvl_ref, cos_ref, sin_ref, c32_ref,
             s32_ref, qw_ref, kvw_ref, wq_ref, wkv_ref, p_ref, s8_ref,
             dql_o, dkvl_o, dkrr_o, dwq_o, dwkv_o, dqw_o, dkvw_o):
        i = pl.program_id(0)
        cs, sn = cos_ref[...], sin_ref[...]
        d1, d2 = dr1_ref[...], dr2_ref[...]
        dq = jnp.concatenate([dqn_ref[...], d1 * cs + d2 * sn, d2 * cs - d1 * sn], axis=1) * (Q_SCALE / LOG2E)
        x = ql_ref[...]
        r = _rms(x)
        qw = qw_ref[...]
        _acc(dwq_o, _mm_tn(x * r * qw, dq), i == 0)
        dx, dw = _rms_bwd(_mm_nt(dq, wq_ref[...]), x, r, qw)
        dql_o[...] = dx
        _acc(dqw_o, dw, i == 0)
        dkv = jnp.concatenate([dkn_ref[...] * (1.0 / LOG2E), dv_ref[...]], axis=1)
        x = kvl_ref[...]
        r = _rms(x)
        kvw = kvw_ref[...]
        _acc(dwkv_o, _mm_tn(x * r * kvw, dkv), i == 0)
        dx, dw = _rms_bwd(_mm_nt(dkv, wkv_ref[...]), x, r, kvw)
        dkvl_o[...] = dx
        _acc(dkvw_o, dw, i == 0)
        dkr = _mm_exact(dkr_ref[...], s8_ref[...]) * (1.0 / LOG2E)
        dkrr_o[...] = dkr * c32_ref[...] + _mm_exact(dkr * s32_ref[...], p_ref[...])

    return _rows_call("qkv_up_bwd", body, tr, [dqn, dr1, dr2, dkn, dv, dkr8, ql, kvl, cos8, sin8, c32, s32],
                      [qw, kvw, wq_b, wkv_b, p32, sum8], [(256, F32), (128, F32), (32, F32)],
                      [((Q_LORA, 768), F32), ((KV_LORA, 1024), F32), ((1, Q_LORA), F32), ((1, KV_LORA), F32)])


def _inproj_bwd(dql, dkvl, dag, du_f, du_b, dypre, dsg, dkr, h, dout, pre_w, w_in_b, dvec, tr):
    def body(dql_ref, dkvl_ref, dag_ref, duf_ref, dub_ref, dyp_ref, dsg_ref, dkr_ref, h_ref, dout_ref,
             pw_ref, w_ref, dv_ref, dh_o, dwin_o, dpw_o):
        i = pl.program_id(0)
        du = duf_ref[...] + dub_ref[...] + dv_ref[...] * dyp_ref[...]
        dproj = jnp.concatenate([dql_ref[...], dkvl_ref[...], dag_ref[...], du, dsg_ref[...],
                                 dkr_ref[...], jnp.zeros((tr, D_IN_PAD - D_IN), F32)], axis=1)
        x = h_ref[...]
        r = _rms(x)
        pw = pw_ref[...]
        _acc(dwin_o, _mm_tn(x * r * pw, dproj), i == 0)
        dx, dw = _rms_bwd(_mm_nt(dproj, w_ref[...]), x, r, pw)
        _acc(dpw_o, dw, i == 0)
        dh_o[...] = dout_ref[...] + dx

    return _rows_call("inproj_bwd", body, tr, [dql, dkvl, dag, du_f, du_b, dypre, dsg, dkr, h, dout],
                      [pre_w, w_in_b, dvec], [(1024, F32)], [((D_MODEL, D_IN_PAD), F32), ((1, D_MODEL), F32)])


def _disc_terms(a_re, a_im, ldt):
    dt = jnp.exp(ldt)
    mag = jnp.exp(a_re * dt)
    th = a_im * dt
    cs, sn = jnp.cos(th), jnp.sin(th)
    abar_re, abar_im = mag * cs, mag * sn
    num_re, num_im = abar_re - 1.0, abar_im
    den = a_re * a_re + a_im * a_im
    coef_re = (num_re * a_re + num_im * a_im) / den
    coef_im = (num_im * a_re - num_re * a_im) / den
    return dt, mag, cs, sn, abar_re, abar_im, num_re, num_im, den, coef_re, coef_im


def _ssm_disc(a_re, a_im, ldt, bt_re, bt_im):
    def body(ar_ref, ai_ref, l_ref, br_ref, bi_ref, abr_o, abi_o, bbr_o, bbi_o):
        t = _disc_terms(ar_ref[...], ai_ref[...], l_ref[...])
        abr_o[...] = t[4]
        abi_o[...] = t[5]
        cr, ci = t[9], t[10]
        br, bi = br_ref[...], bi_ref[...]
        bbr_o[...] = cr * br - ci * bi
        bbi_o[...] = cr * bi + ci * br

    ng = a_re.shape[0]
    return pl.pallas_call(
        body, name="ssm_disc",
        out_shape=[jax.ShapeDtypeStruct((ng, 1, SSM_STATE), F32)] * 2
        + [jax.ShapeDtypeStruct((ng, SSM_GROUP, SSM_STATE), F32)] * 2)(a_re, a_im, ldt, bt_re, bt_im)


def _ssm_disc_bwd(a_re, a_im, ldt, bt_re, bt_im, da8_re, da8_im, dbb_re, dbb_im):
    def body(ar_ref, ai_ref, l_ref, br_ref, bi_ref, dar_ref, dai_ref, dbr_ref, dbi_ref,
             gar_o, gai_o, gl_o, gbr_o, gbi_o):
        a_re, a_im = ar_ref[...], ai_ref[...]
        dt, mag, cs, sn, abar_re, abar_im, num_re, num_im, den, cr, ci = _disc_terms(a_re, a_im, l_ref[...])
        br, bi = br_ref[...], bi_ref[...]
        dbr, dbi = dbr_ref[...], dbi_ref[...]
        gbr_o[...] = cr * dbr + ci * dbi
        gbi_o[...] = cr * dbi - ci * dbr
        dcr = jnp.sum(br * dbr + bi * dbi, axis=1, keepdims=True)
        dci = jnp.sum(br * dbi - bi * dbr, axis=1, keepdims=True)
        dnum_re = (dcr * a_re - dci * a_im) / den
        dnum_im = (dcr * a_im + dci * a_re) / den
        dden = -(dcr * cr + dci * ci) / den
        g_are = (dcr * num_re + dci * num_im) / den + dden * 2.0 * a_re
        g_aim = (dcr * num_im - dci * num_re) / den + dden * 2.0 * a_im
        d_abr = jnp.sum(dar_ref[...], axis=1, keepdims=True) + dnum_re
        d_abi = jnp.sum(dai_ref[...], axis=1, keepdims=True) + dnum_im
        dmag = d_abr * cs + d_abi * sn
        dth = d_abi * abar_re - d_abr * abar_im
        g_are = g_are + dmag * mag * dt
        g_aim = g_aim + dth * dt
        ddt = jnp.sum(dmag * mag * a_re + dth * a_im, axis=2, keepdims=True)
        gar_o[...] = g_are
        gai_o[...] = g_aim
        gl_o[...] = ddt * dt

    ng = a_re.shape[0]
    return pl.pallas_call(
        body, name="ssm_disc_bwd",
        out_shape=[jax.ShapeDtypeStruct((ng, 1, SSM_STATE), F32)] * 2 + [jax.ShapeDtypeStruct((ng, 1, 1), F32)]
        + [jax.ShapeDtypeStruct((ng, SSM_GROUP, SSM_STATE), F32)] * 2)(
            a_re, a_im, ldt, bt_re, bt_im, da8_re, da8_im, dbb_re, dbb_im)


def _exchange(name, buf, all_to_all):
    rows = buf.shape[-2]

    def body(in_ref, out_ref, send_sems, recv_sems, local_sem):
        x, y, c = lax.axis_index("x"), lax.axis_index("y"), lax.axis_index("c")
        me = 4 * x + 2 * y + c
        copies = []
        for k in range(1, N_DEV):
            px = 1 - x if (k >> 2) & 1 else x
            py = 1 - y if (k >> 1) & 1 else y
            pc = 1 - c if k & 1 else c
            src = in_ref.at[4 * px + 2 * py + pc] if all_to_all else in_ref
            copies.append(pltpu.make_async_remote_copy(
                src_ref=src, dst_ref=out_ref.at[me], send_sem=send_sems.at[k - 1], recv_sem=recv_sems.at[k - 1],
                device_id=(px, py, pc), device_id_type=pl.DeviceIdType.MESH))
        mine = pltpu.make_async_copy(in_ref.at[me] if all_to_all else in_ref, out_ref.at[me], local_sem)
        mine.start()
        for cp in copies:
            cp.start()
        for cp in copies:
            cp.wait()
        mine.wait()

    return pl.pallas_call(
        body, name=name, out_shape=jax.ShapeDtypeStruct((N_DEV, rows, LANES), F32),
        in_specs=[pl.BlockSpec(memory_space=pl.ANY)], out_specs=pl.BlockSpec(memory_space=pl.ANY),
        scratch_shapes=[pltpu.SemaphoreType.DMA((N_DEV - 1,)), pltpu.SemaphoreType.DMA((N_DEV - 1,)),
                        pltpu.SemaphoreType.DMA(())])(buf)


def _adamw(recv, w, m, v, tr):
    rows = w.shape[0]
    c1 = 1.0 - ADAM_B1 ** ADAM_STEP
    c2 = 1.0 - ADAM_B2 ** ADAM_STEP

    def body(r_ref, w_ref, m_ref, v_ref, g_o, d_o, m_o, v_o):
        g = r_ref[0]
        for k in range(1, N_DEV):
            g = g + r_ref[k]
        mm = ADAM_B1 * m_ref[...] + (1.0 - ADAM_B1) * g
        vv = ADAM_B2 * v_ref[...] + (1.0 - ADAM_B2) * (g * g)
        g_o[...] = g
        m_o[...] = mm
        v_o[...] = vv
        d_o[...] = -ADAM_LR * ((mm / c1) / (jnp.sqrt(vv / c2) + ADAM_EPS) + ADAM_WD * w_ref[...])

    spec = pl.BlockSpec((tr, LANES), lambda i: (i, 0))
    return pl.pallas_call(
        body, name="adamw", grid=(rows // tr,),
        in_specs=[pl.BlockSpec((N_DEV, tr, LANES), lambda i: (0, i, 0)), spec, spec, spec],
        out_specs=[spec] * 4, out_shape=[jax.ShapeDtypeStruct((rows, LANES), F32)] * 4,
        compiler_params=_cparams("arbitrary"))(recv, w, m, v)


def _to_rows(a):
    flat = a.reshape(-1)
    pad = (-flat.shape[0]) % LANES
    if pad:
        flat = jnp.concatenate([flat, jnp.zeros((pad,), flat.dtype)])
    return flat.reshape(-1, LANES)


def _n_rows(shape):
    return -(-int(np.prod(shape)) // LANES)


def _pack(arrays, total_rows):
    rows = [_to_rows(a) for a in arrays]
    used = sum(r.shape[0] for r in rows)
    if total_rows > used:
        rows.append(jnp.zeros((total_rows - used, LANES), F32))
    return jnp.concatenate(rows, axis=0)


def _unpack(buf, shapes):
    lead = buf.shape[:-2]
    out, r0 = [], 0
    for s in shapes:
        n = int(np.prod(s))
        nr = _n_rows(s)
        out.append(buf[..., r0:r0 + nr, :].reshape(lead + (-1,))[..., :n].reshape(lead + tuple(s)))
        r0 += nr
    return out


def _pack_per_device(arrays, total_rows):
    rows = []
    for a in arrays:
        flat = a.reshape(N_DEV, -1)
        pad = (-flat.shape[1]) % LANES
        if pad:
            flat = jnp.concatenate([flat, jnp.zeros((N_DEV, pad), flat.dtype)], axis=1)
        rows.append(flat.reshape(N_DEV, -1, LANES))
    used = sum(r.shape[1] for r in rows)
    if total_rows > used:
        rows.append(jnp.zeros((N_DEV, total_rows - used, LANES), F32))
    return jnp.concatenate(rows, axis=1)


def _shard_views(name, full):
    if name == 'w_out':
        return full.reshape(N_DEV, full.shape[0] // N_DEV, full.shape[1])
    r, ccols = full.shape
    return full.reshape(r, N_DEV, ccols // N_DEV).transpose(1, 0, 2)


def _from_shards(name, stacked):
    if name == 'w_out':
        return stacked.reshape(-1, stacked.shape[-1])
    n, r, cc = stacked.shape
    return stacked.transpose(1, 0, 2).reshape(r, n * cc)


GROUPS_PER_BLK = N_GROUPS // N_COL_BLK


def _block_diag(t):
    eye = jnp.eye(GROUPS_PER_BLK, dtype=t.dtype)
    t4 = t.reshape(N_COL_BLK, GROUPS_PER_BLK, SSM_GROUP, SSM_STATE)
    return (t4[:, :, :, None, :] * eye[None, :, None, :, None]).reshape(N_COL_BLK, CH_BLK, COL_BLK)


def _diag_blocks(mat4):
    eye = jnp.eye(GROUPS_PER_BLK, dtype=mat4.dtype)
    m6 = mat4.reshape(N_COL_BLK, GROUPS_PER_BLK, SSM_GROUP, GROUPS_PER_BLK, SSM_STATE)
    return (m6 * eye[None, :, None, :, None]).sum(axis=3).reshape(N_GROUPS, SSM_GROUP, SSM_STATE)


def _step(x, loss_target, wts, moms, vels):
    seq = x.shape[1]
    n_valid = N_META + seq
    lp = -(-n_valid // 256) * 256
    tr = _pick(lp, [640, 256])
    tr_small = 256
    tq = _pick(lp, [1280, 256])
    tk = _pick(lp, [640, 256])

    shard_shapes = [wts[n].shape[-2:] for n in SHARDED]
    n_shard_rows = sum(_n_rows(s) for s in shard_shapes)
    gathered = _exchange("gather_weights", _pack([wts[n].reshape(wts[n].shape[-2:]) for n in SHARDED],
                                                 n_shard_rows), all_to_all=False)
    full = {n: _from_shards(n, a) for n, a in zip(SHARDED, _unpack(gathered, shard_shapes))}

    w_in_b = jnp.concatenate([_cols_in(full['w_in']), jnp.zeros((D_MODEL, D_IN_PAD - D_IN), F32)],
                             axis=1).astype(BF16)
    wq_b = _cols_q(full['w_q_up']).astype(BF16)
    wkv_b = _cols_kv(full['w_kv_up']).astype(BF16)
    wglu_b = full['w_glu'].astype(BF16)
    wo_b = full['w_out'].astype(BF16)
    pre_w, post_w = wts['pre_norm_w'], wts['post_norm_w']
    qw, kvw, aw, sw = wts['q_norm_w'], wts['kv_norm_w'], wts['attn_out_norm_w'], wts['ssm_out_norm_w']
    bglu, dvec = wts['b_glu'], wts['ssm_d']

    lseg = lp // N_SEG
    pos = _row_position(jnp.arange(lp, dtype=jnp.int32), lseg)
    inv = ROPE_THETA ** (-jnp.arange(HALF_ROPE, dtype=F32) / HALF_ROPE)
    ang = pos.astype(F32)[:, None] * inv[None, :]
    cos, sin = jnp.cos(ang), jnp.sin(ang)
    cos8, sin8 = jnp.tile(cos, (1, HEADS)), jnp.tile(sin, (1, HEADS))
    c32 = jnp.concatenate([cos, cos], axis=1)
    s32 = jnp.concatenate([-sin, sin], axis=1)
    p32 = jnp.asarray(np.roll(np.eye(QK_ROPE, dtype=np.float32), HALF_ROPE, axis=1))
    sum8 = jnp.asarray(np.tile(np.eye(QK_ROPE, dtype=np.float32), (HEADS, 1)))
    head_sum = jnp.asarray(np.repeat(np.eye(HEADS, dtype=np.float32), V_HEAD, axis=0))

    ng = 2 * N_GROUPS
    a_re3 = wts['ssm_a_re'].reshape(ng, 1, SSM_STATE)
    a_im3 = wts['ssm_a_im'].reshape(ng, 1, SSM_STATE)
    ldt3 = wts['ssm_log_dt'].reshape(ng, 1, 1)
    bt_re = wts['ssm_b_re'].reshape(2, N_GROUPS, SSM_STATE, SSM_GROUP).transpose(0, 1, 3, 2).reshape(
        ng, SSM_GROUP, SSM_STATE)
    bt_im = wts['ssm_b_im'].reshape(2, N_GROUPS, SSM_STATE, SSM_GROUP).transpose(0, 1, 3, 2).reshape(
        ng, SSM_GROUP, SSM_STATE)
    c_re = wts['ssm_c_re'].reshape(ng, SSM_GROUP, SSM_STATE)
    c_im = wts['ssm_c_im'].reshape(ng, SSM_GROUP, SSM_STATE)
    abar_re, abar_im, bbt_re, bbt_im = _ssm_disc(a_re3, a_im3, ldt3, bt_re, bt_im)

    def direction(t, d):
        return t[d * N_GROUPS:(d + 1) * N_GROUPS]

    def slab(t, d, sign=1.0):
        return jnp.broadcast_to(sign * direction(t, d).reshape(1, N_STATES), (N_SEG, N_STATES))

    w_re = [_block_diag(direction(bbt_re, d)).astype(BF16) for d in range(2)]
    w_im = [_block_diag(direction(bbt_im, d)).astype(BF16) for d in range(2)]
    cb_re = [_block_diag(direction(c_re, d)).astype(BF16) for d in range(2)]
    cb_im = [_block_diag(-direction(c_im, d)).astype(BF16) for d in range(2)]

    def to_rows(a):
        return a.reshape(N_SEG, lseg, a.shape[-1]).transpose(1, 0, 2).reshape(lp, a.shape[-1])

    def to_tokens(a):
        return a.reshape(lseg, N_SEG, a.shape[-1]).transpose(1, 0, 2).reshape(lp, a.shape[-1])

    pad = jnp.zeros((lp - n_valid, D_MODEL), F32)
    h = to_rows(jnp.concatenate([full['meta_tokens'], x[0], pad], axis=0))
    tgt = to_rows(jnp.concatenate([jnp.zeros((N_META, D_MODEL), F32), loss_target[0], pad], axis=0))

    ql, kvl, ag, su, sg, kr = _inproj(h, pre_w, w_in_b, tr)
    qn_b, qr1_b, qr2_b, kn_b, v_b, kr_b = _qkv_up(ql, kvl, kr, cos8, sin8, c32, s32, qw, kvw, wq_b, wkv_b, p32, tr)

    def heads(a, w):
        return a.reshape(lp, HEADS, w)

    nq, nk = lp // tq, lp // tk
    q_t = jnp.concatenate([heads(qn_b, 64), heads(qr1_b, 16), heads(qr2_b, 16)], axis=-1)
    k_t = jnp.concatenate([heads(kn_b, 64), jnp.broadcast_to(kr_b[:, None, :], (lp, HEADS, QK_ROPE))], axis=-1)
    v_t = heads(v_b, 64)
    vx_t = jnp.concatenate([v_t, jnp.ones((lp, HEADS, 1), BF16), jnp.zeros((lp, HEADS, LANES - V_HEAD - 1), BF16)],
                           axis=-1)
    qt4 = q_t.reshape(nq, tq, HEADS, QK_DIM).transpose(2, 0, 3, 1)
    tk_fwd = _pick(lp, [1280, 256])
    vxt4 = vx_t.reshape(lp // tk_fwd, tk_fwd, HEADS, LANES).transpose(2, 0, 3, 1)
    k_h = k_t.transpose(1, 0, 2)
    kt_h = k_t.transpose(1, 2, 0)
    v_h = v_t.transpose(1, 0, 2)
    ot_h, lse4 = _attn_fwd(qt4, k_h, vxt4, n_valid)
    o_flat = ot_h.transpose(2, 0, 1).reshape(lp, D_ATTN)
    ya = _attn_post(o_flat, ag, aw, tr)

    xs, ys = [], []
    for d in range(2):
        ar8, ai8 = slab(abar_re, d), slab(abar_im, d)
        ere, eim = _scan_ends(f"scan{d}_ends", su, w_re[d], w_im[d], ar8, ai8, d == 0)
        x_re, x_im, y_d = _scan_fwd(f"scan{d}", su, w_re[d], w_im[d], ar8, ai8, ere, eim, cb_re[d], cb_im[d],
                                    d == 0)
        xs += [x_re, x_im]
        ys.append(y_d)
    ypre, glu, ysn = _ssm_post(ys[0], ys[1], su, sg, wglu_b, bglu, sw, dvec, tr)

    dy, dout, loss, d_post = _out_loss(ya, ysn, h, tgt, wo_b, post_w, n_valid, tr_small)

    do_flat, dag, dysn, delta8, d_wo, d_aw = _out_bwd(dy, ya, ysn, o_flat, ag, wo_b, aw, head_sum, tr_small)
    dypre, dsg, d_wglu, d_bglu, d_sw, d_dvec = _ssm_post_bwd(dysn, glu, sg, ypre, su, wglu_b, sw, dvec, tr_small)

    dus, d_ct, d_wb, d_a8 = [], [], [], []
    for d in range(2):
        ar8, ai8c = slab(abar_re, d), slab(abar_im, d, -1.0)
        ere, eim = _scan_ends(f"scan_adj{d}_ends", dypre, cb_re[d], cb_im[d], ar8, ai8c, d != 0)
        du_d, dw_re, dw_im, dc_re, dc_im, da_re, da_im = _scan_bwd(
            f"scan_adj{d}", dypre, cb_re[d], cb_im[d], ar8, ai8c, ere, eim, su, w_re[d], w_im[d],
            xs[2 * d], xs[2 * d + 1], d != 0)
        dus.append(du_d)
        d_ct.append((dc_re, dc_im))
        d_wb.append((dw_re, dw_im))
        d_a8.append((da_re, da_im))

    dot4 = do_flat.astype(BF16).reshape(nq, tq, HEADS, V_HEAD).transpose(2, 0, 3, 1)
    dqt4, dkt_h, dvt_h = _attn_bwd(qt4, k_h, kt_h, v_h, dot4, lse4, delta8.T.reshape(HEADS, nq, 1, tq), tk)
    dq_t = dqt4.transpose(1, 3, 0, 2).reshape(lp, HEADS, QK_DIM)
    dk_t = dkt_h.transpose(2, 0, 1)
    dqn = dq_t[:, :, :64].reshape(lp, 512)
    dr1 = dq_t[:, :, 64:80].reshape(lp, 128)
    dr2 = dq_t[:, :, 80:96].reshape(lp, 128)
    dkn = dk_t[:, :, :64].reshape(lp, 512)
    dkr8 = dk_t[:, :, 64:].reshape(lp, HEADS * QK_ROPE)
    dvf = dvt_h.transpose(2, 0, 1).reshape(lp, 512)
    dql, dkvl, dkrr, d_wq, d_wkv, d_qw, d_kvw = _qkv_up_bwd(
        dqn, dr1, dr2, dkn, dvf, dkr8, ql, kvl, cos8, sin8, c32, s32, qw, kvw, wq_b, wkv_b, p32, sum8, tr_small)
    dh, d_win, d_pre = _inproj_bwd(dql, dkvl, dag, dus[0], dus[1], dypre, dsg, dkrr, h, dout, pre_w, w_in_b, dvec,
                                   tr_small)
    dh = to_tokens(dh)

    def seg_sums(t):
        return t.reshape(N_SEG, N_GROUPS, SSM_STATE).transpose(1, 0, 2)

    da8_re = jnp.concatenate([seg_sums(d_a8[d][0]) for d in range(2)], axis=0)
    da8_im = jnp.concatenate([seg_sums(d_a8[d][1]) for d in range(2)], axis=0)
    dbb_re = jnp.concatenate([_diag_blocks(d_wb[d][0]) for d in range(2)], axis=0)
    dbb_im = jnp.concatenate([_diag_blocks(d_wb[d][1]) for d in range(2)], axis=0)
    g_are, g_aim, g_ldt, g_bt_re, g_bt_im = _ssm_disc_bwd(a_re3, a_im3, ldt3, bt_re, bt_im, da8_re, da8_im,
                                                          dbb_re, dbb_im)
    g_c_re = jnp.concatenate([_diag_blocks(d_ct[d][0]) for d in range(2)], axis=0)
    g_c_im = jnp.concatenate([-_diag_blocks(d_ct[d][1]) for d in range(2)], axis=0)

    def b_layout(t):
        return t.reshape(2, N_GROUPS, SSM_GROUP, SSM_STATE).transpose(0, 1, 3, 2)

    local = {
        'meta_tokens': dh[:N_META],
        'pre_norm_w': d_pre, 'post_norm_w': d_post,
        'w_in': _cols_in_inv(d_win[:, :D_IN]),
        'q_norm_w': d_qw, 'w_q_up': _cols_q_inv(d_wq),
        'kv_norm_w': d_kvw, 'w_kv_up': _cols_kv_inv(d_wkv),
        'attn_out_norm_w': d_aw,
        'ssm_a_re': g_are, 'ssm_a_im': g_aim, 'ssm_log_dt': g_ldt,
        'ssm_b_re': b_layout(g_bt_re), 'ssm_b_im': b_layout(g_bt_im), 'ssm_c_re': g_c_re, 'ssm_c_im': g_c_im,
        'ssm_d': d_dvec, 'w_glu': d_wglu, 'b_glu': d_bglu, 'ssm_out_norm_w': d_sw, 'w_out': d_wo,
    }

    order = SHARDED + [n for n in WEIGHTS if n not in SHARDED]
    shapes = [wts[n].shape for n in order]
    used_rows = sum(_n_rows(s) for s in shapes)
    tr_adam = 512
    total_rows = -(-used_rows // tr_adam) * tr_adam
    per_dev = [_shard_views(n, local[n]) if n in SHARDED
               else jnp.broadcast_to(local[n].reshape(1, -1), (N_DEV, local[n].size)) for n in order]
    recv = _exchange("exchange_grads", _pack_per_device(per_dev, total_rows), all_to_all=True)
    packed = [_pack([src[n] for n in order], total_rows) for src in (wts, moms, vels)]
    g_p, d_p, m_p, v_p = _adamw(recv, *packed, tr_adam)
    grads, deltas, new_m, new_v = (dict(zip(order, _unpack(b, shapes))) for b in (g_p, d_p, m_p, v_p))

    loss = lax.psum(loss[0, 0], ("x", "y", "c"))
    grad_x = dh[N_META:n_valid][None]
    return (loss, grad_x, *[grads[n] for n in WEIGHTS], *[deltas[n] for n in WEIGHTS],
            *[new_m[n] for n in WEIGHTS], *[new_v[n] for n in WEIGHTS])


def kernel(x, meta_tokens, pre_norm_w, post_norm_w, w_in, q_norm_w, w_q_up, kv_norm_w, w_kv_up, attn_out_norm_w, ssm_a_re, ssm_a_im, ssm_log_dt, ssm_b_re, ssm_b_im, ssm_c_re, ssm_c_im, ssm_d, w_glu, b_glu, ssm_out_norm_w, w_out, loss_target, m_meta_tokens, m_pre_norm_w, m_post_norm_w, m_w_in, m_q_norm_w, m_w_q_up, m_kv_norm_w, m_w_kv_up, m_attn_out_norm_w, m_ssm_a_re, m_ssm_a_im, m_ssm_log_dt, m_ssm_b_re, m_ssm_b_im, m_ssm_c_re, m_ssm_c_im, m_ssm_d, m_w_glu, m_b_glu, m_ssm_out_norm_w, m_w_out, v_meta_tokens, v_pre_norm_w, v_post_norm_w, v_w_in, v_q_norm_w, v_w_q_up, v_kv_norm_w, v_w_kv_up, v_attn_out_norm_w, v_ssm_a_re, v_ssm_a_im, v_ssm_log_dt, v_ssm_b_re, v_ssm_b_im, v_ssm_c_re, v_ssm_c_im, v_ssm_d, v_w_glu, v_b_glu, v_ssm_out_norm_w, v_w_out):
    wts = dict(zip(WEIGHTS, (meta_tokens, pre_norm_w, post_norm_w, w_in, q_norm_w, w_q_up, kv_norm_w, w_kv_up,
                             attn_out_norm_w, ssm_a_re, ssm_a_im, ssm_log_dt, ssm_b_re, ssm_b_im, ssm_c_re,
                             ssm_c_im, ssm_d, w_glu, b_glu, ssm_out_norm_w, w_out)))
    moms = dict(zip(WEIGHTS, (m_meta_tokens, m_pre_norm_w, m_post_norm_w, m_w_in, m_q_norm_w, m_w_q_up,
                              m_kv_norm_w, m_w_kv_up, m_attn_out_norm_w, m_ssm_a_re, m_ssm_a_im, m_ssm_log_dt,
                              m_ssm_b_re, m_ssm_b_im, m_ssm_c_re, m_ssm_c_im, m_ssm_d, m_w_glu, m_b_glu,
                              m_ssm_out_norm_w, m_w_out)))
    vels = dict(zip(WEIGHTS, (v_meta_tokens, v_pre_norm_w, v_post_norm_w, v_w_in, v_q_norm_w, v_w_q_up,
                              v_kv_norm_w, v_w_kv_up, v_attn_out_norm_w, v_ssm_a_re, v_ssm_a_im, v_ssm_log_dt,
                              v_ssm_b_re, v_ssm_b_im, v_ssm_c_re, v_ssm_c_im, v_ssm_d, v_w_glu, v_b_glu,
                              v_ssm_out_norm_w, v_w_out)))
    return _step(x, loss_target, wts, moms, vels)
```

```python
import functools
import math

import numpy as np
import jax
import jax.numpy as jnp
from jax import lax
from jax.experimental import pallas as pl
from jax.experimental.pallas import tpu as pltpu

F32 = jnp.float32
BF16 = jnp.bfloat16

D_MODEL = 1024
N_META = 16
EPS = 1e-6
HEADS = 8
QK_NOPE = 64
QK_ROPE = 32
HALF_ROPE = QK_ROPE // 2
QK_DIM = QK_NOPE + QK_ROPE
V_HEAD = 64
Q_LORA = 256
KV_LORA = 128
D_ATTN = HEADS * V_HEAD
D_SSM = 512
SSM_GROUP = 16
N_GROUPS = D_SSM // SSM_GROUP
SSM_STATE = 64
N_STATES = N_GROUPS * SSM_STATE
ROPE_THETA = 10000.0
D_IN = Q_LORA + KV_LORA + QK_ROPE + D_ATTN + 2 * D_SSM
D_IN_PAD = 2048
N_DEV = 8
N_SEG = 8
COL_BLK = 512
LANES = 128

ADAM_LR = 0.001
ADAM_B1 = 0.9
ADAM_B2 = 0.999
ADAM_EPS = 1e-08
ADAM_WD = 0.01
ADAM_STEP = 10

VMEM_LIMIT_V7X = 56 * 1024 * 1024
LOG2E = 1.0 / math.log(2.0)
Q_SCALE = LOG2E / math.sqrt(QK_DIM)
ATTN_UNROLL = 4
ATTN_BWD_UNROLL = 4

WEIGHTS = ['meta_tokens', 'pre_norm_w', 'post_norm_w', 'w_in', 'q_norm_w', 'w_q_up', 'kv_norm_w', 'w_kv_up',
           'attn_out_norm_w', 'ssm_a_re', 'ssm_a_im', 'ssm_log_dt', 'ssm_b_re', 'ssm_b_im', 'ssm_c_re', 'ssm_c_im',
           'ssm_d', 'w_glu', 'b_glu', 'ssm_out_norm_w', 'w_out']
SHARDED = ['w_in', 'w_q_up', 'w_kv_up', 'w_glu', 'w_out', 'meta_tokens']

def _cols_in(w):
    return jnp.concatenate([w[:, 0:384], w[:, 416:D_IN], w[:, 384:416]], axis=1)


def _cols_in_inv(w):
    return jnp.concatenate([w[:, 0:384], w[:, D_IN - QK_ROPE:D_IN], w[:, 384:D_IN - QK_ROPE]], axis=1)


def _cols_q(w):
    t = w.reshape(w.shape[0], HEADS, QK_DIM)
    return jnp.concatenate([t[:, :, 0:64].reshape(-1, 512), t[:, :, 64:80].reshape(-1, 128),
                            t[:, :, 80:96].reshape(-1, 128)], axis=1)


def _cols_q_inv(w):
    r = w.shape[0]
    return jnp.concatenate([w[:, 0:512].reshape(r, HEADS, 64), w[:, 512:640].reshape(r, HEADS, 16),
                            w[:, 640:768].reshape(r, HEADS, 16)], axis=2).reshape(r, HEADS * QK_DIM)


def _cols_kv(w):
    t = w.reshape(w.shape[0], HEADS, 128)
    return jnp.concatenate([t[:, :, 0:64].reshape(-1, 512), t[:, :, 64:128].reshape(-1, 512)], axis=1)


def _cols_kv_inv(w):
    r = w.shape[0]
    return jnp.concatenate([w[:, 0:512].reshape(r, HEADS, 64), w[:, 512:1024].reshape(r, HEADS, 64)],
                           axis=2).reshape(r, HEADS * 128)


def _pick(n, cands):
    for c in cands:
        if n % c == 0:
            return c
    raise ValueError(f"no tile for {n}")


def _cparams(*sem):
    return pltpu.CompilerParams(dimension_semantics=sem, vmem_limit_bytes=VMEM_LIMIT_V7X)


def _mm(a, b):
    return jnp.dot(a.astype(BF16), b.astype(BF16), preferred_element_type=F32)


def _mm_nt(a, b):
    return lax.dot_general(a.astype(BF16), b.astype(BF16), (((1,), (1,)), ((), ())), preferred_element_type=F32)


def _mm_tn(a, b):
    return lax.dot_general(a.astype(BF16), b.astype(BF16), (((0,), (0,)), ((), ())), preferred_element_type=F32)


def _mm_exact(a, b):
    return jnp.dot(a, b, precision=lax.Precision.HIGHEST, preferred_element_type=F32)


def _rms(x):
    return lax.rsqrt(jnp.mean(x * x, axis=-1, keepdims=True) + EPS)


def _rms_bwd(dy, x, r, w):
    xh = x * r
    g = dy * w
    dx = r * (g - xh * jnp.mean(g * xh, axis=-1, keepdims=True))
    dw = jnp.sum(dy * xh, axis=0, keepdims=True)
    return dx, dw


def _sigmoid(z):
    return 1.0 / (1.0 + jnp.exp(-z))


def _silu_and_grad(z):
    s = _sigmoid(z)
    return z * s, s * (1.0 + z * (1.0 - s))


_GELU_C = math.sqrt(2.0 / math.pi)


def _gelu_and_grad(x):
    x2 = x * x
    t = jnp.tanh(_GELU_C * (x + 0.044715 * x * x2))
    val = 0.5 * x * (1.0 + t)
    grad = 0.5 * (1.0 + t) + 0.5 * x * (1.0 - t * t) * _GELU_C * (1.0 + 3.0 * 0.044715 * x2)
    return val, grad


def _acc(ref, val, first):
    @pl.when(first)
    def _():
        ref[...] = val

    @pl.when(jnp.logical_not(first))
    def _():
        ref[...] += val


def _rows_call(name, body, tr, row_ins, full_ins, row_outs, acc_outs):
    lp = row_ins[0].shape[0]
    in_specs = [pl.BlockSpec((tr, a.shape[1]), lambda i: (i, 0)) for a in row_ins]
    in_specs += [pl.BlockSpec(a.shape, lambda i, n=a.ndim: (0,) * n) for a in full_ins]
    out_specs = [pl.BlockSpec((tr, c), lambda i: (i, 0)) for c, _ in row_outs]
    out_specs += [pl.BlockSpec(s, lambda i, n=len(s): (0,) * n) for s, _ in acc_outs]
    out_shape = [jax.ShapeDtypeStruct((lp, c), dt) for c, dt in row_outs]
    out_shape += [jax.ShapeDtypeStruct(s, dt) for s, dt in acc_outs]
    return pl.pallas_call(
        body, name=name, grid=(lp // tr,), in_specs=in_specs, out_specs=out_specs, out_shape=out_shape,
        compiler_params=_cparams("arbitrary"))(*row_ins, *full_ins)


def _inproj(h, pre_w, w_in_b, tr):
    def body(h_ref, pw_ref, w_ref, ql, kvl, ag, su, sg, kr):
        x = h_ref[...]
        xn = x * _rms(x) * pw_ref[...]
        pr = _mm(xn, w_ref[...])
        ql[...] = pr[:, 0:256]
        kvl[...] = pr[:, 256:384]
        ag[...] = pr[:, 384:896]
        su[...] = pr[:, 896:1408]
        sg[...] = pr[:, 1408:1920]
        kr[...] = pr[:, 1920:1952]

    return _rows_call("inproj", body, tr, [h], [pre_w, w_in_b],
                      [(256, F32), (128, F32), (512, F32), (512, F32), (512, F32), (32, F32)], [])


def _qkv_up(ql, kvl, kr, cos8, sin8, c32, s32, qw, kvw, wq_b, wkv_b, p32, tr):
    def body(ql_ref, kvl_ref, kr_ref, cos_ref, sin_ref, c32_ref, s32_ref, qw_ref, kvw_ref, wq_ref, wkv_ref, p_ref,
             qn_o, qr1_o, qr2_o, kn_o, v_o, kr_o):
        x = ql_ref[...]
        q = _mm(x * _rms(x) * qw_ref[...], wq_ref[...]) * Q_SCALE
        r1, r2 = q[:, 512:640], q[:, 640:768]
        cs, sn = cos_ref[...], sin_ref[...]
        qn_o[...] = q[:, 0:512].astype(BF16)
        qr1_o[...] = (r1 * cs - r2 * sn).astype(BF16)
        qr2_o[...] = (r2 * cs + r1 * sn).astype(BF16)
        x = kvl_ref[...]
        kv = _mm(x * _rms(x) * kvw_ref[...], wkv_ref[...])
        kn_o[...] = kv[:, 0:512].astype(BF16)
        v_o[...] = kv[:, 512:1024].astype(BF16)
        x = kr_ref[...]
        kr_o[...] = (x * c32_ref[...] + _mm_exact(x, p_ref[...]) * s32_ref[...]).astype(BF16)

    return _rows_call("qkv_up", body, tr, [ql, kvl, kr, cos8, sin8, c32, s32], [qw, kvw, wq_b, wkv_b, p32],
                      [(512, BF16), (128, BF16), (128, BF16), (512, BF16), (512, BF16), (32, BF16)], [])


def _row_position(row, lseg):
    return (row & (N_SEG - 1)) * lseg + (row >> 3)


def _first_padded_tile(n_valid, lp, tile):
    lseg = lp // N_SEG
    t0 = n_valid - (N_SEG - 1) * lseg
    return (t0 * N_SEG + N_SEG - 1) // tile if n_valid < lp else lp // tile


def _attn_fwd(qt, k, vxt, n_valid):
    _, nq, _, tq = qt.shape
    _, nk, _, tk = vxt.shape
    lp = k.shape[1]
    lseg = lp // N_SEG
    n_plain = max(0, min(nk, _first_padded_tile(n_valid, lp, tk)))

    def body(q_ref, k_ref, v_ref, o_ref, lse_ref, m_s, acc_s):
        m_s[...] = jnp.full(m_s.shape, -1e30, F32)
        acc_s[...] = jnp.zeros(acc_s.shape, F32)
        qq = q_ref[0, 0]

        def chunk(c, padded):
            r0 = pl.multiple_of(c * tk, tk)
            st = _mm(k_ref[0, pl.ds(r0, tk), :], qq)
            if padded:
                row = r0 + lax.broadcasted_iota(jnp.int32, (tk, tq), 0)
                st = jnp.where(_row_position(row, lseg) < n_valid, st, -1e30)
            m_old = m_s[...]
            m_new = jnp.maximum(m_old, jnp.max(st, axis=0, keepdims=True))
            pt = jnp.exp2(st - m_new)
            acc_s[...] = jnp.exp2(m_old - m_new) * acc_s[...] + _mm(v_ref[0, c], pt)
            m_s[...] = m_new

        def plain(c, carry):
            chunk(c, False)
            return carry

        n_loop = n_plain - n_plain % ATTN_UNROLL
        if n_loop:
            lax.fori_loop(0, n_loop, plain, 0, unroll=ATTN_UNROLL)
        for c in range(n_loop, nk):
            chunk(c, c >= n_plain)
        acc = acc_s[...]
        l = acc[V_HEAD:V_HEAD + 1, :]
        o_ref[0] = acc[:V_HEAD, :] / l
        lse_ref[0, 0] = m_s[...] + jnp.log2(l)

    return pl.pallas_call(
        body, name="attn_fwd", grid=(HEADS, nq),
        in_specs=[pl.BlockSpec((1, 1, QK_DIM, tq), lambda h, i: (h, i, 0, 0)),
                  pl.BlockSpec((1, lp, QK_DIM), lambda h, i: (h, 0, 0)),
                  pl.BlockSpec((1, nk, LANES, tk), lambda h, i: (h, 0, 0, 0))],
        out_specs=[pl.BlockSpec((1, V_HEAD, tq), lambda h, i: (h, 0, i)),
                   pl.BlockSpec((1, 1, 1, tq), lambda h, i: (h, i, 0, 0))],
        out_shape=[jax.ShapeDtypeStruct((HEADS, V_HEAD, lp), F32), jax.ShapeDtypeStruct((HEADS, nq, 1, tq), F32)],
        scratch_shapes=[pltpu.VMEM((1, tq), F32), pltpu.VMEM((LANES, tq), F32)],
        compiler_params=_cparams("arbitrary", "arbitrary"))(qt, k, vxt)


def _attn_post(o_flat, ag, aw, tr):
    def body(o_ref, g_ref, w_ref, ya):
        t = o_ref[...] * _silu_and_grad(g_ref[...])[0]
        ya[...] = t * _rms(t) * w_ref[...]

    return _rows_call("attn_post", body, tr, [o_flat, ag], [aw], [(512, F32)], [])[0]


def _scan_tiles(lp):
    lseg = lp // N_SEG
    tt = _pick(lseg, [208, 48, 32, 16, 8, 4, 2, 1])
    return lseg, tt, lseg // tt


def _cmul(ar, ai, br, bi):
    return ar * br - ai * bi, ar * bi + ai * br


N_COL_BLK = N_STATES // COL_BLK
CH_BLK = D_SSM // N_COL_BLK


def _scan_steps(tt, forward, bre_ref, bim_ref, ar, ai, carry, visit):
    def step(s, c):
        r0 = pl.multiple_of((s if forward else tt - 1 - s) * N_SEG, N_SEG)
        pr, pi = _cmul(ar, ai, c[0], c[1])
        xr = pr + bre_ref[pl.ds(r0, N_SEG), :]
        xi = pi + bim_ref[pl.ds(r0, N_SEG), :]
        return (xr, xi) + tuple(visit(r0, (xr, xi), (c[0], c[1]), c[2:]))

    return lax.fori_loop(0, tt, step, carry, unroll=4 if tt % 4 == 0 else 1)


def _segment_starts(lseg, forward, ar, ai, ere_ref, eim_ref, s_re, s_im):
    a1r, a1i = ar[0:1, :], ai[0:1, :]
    pr, pi = jnp.ones_like(a1r), jnp.zeros_like(a1i)
    br, bi = a1r, a1i
    n = lseg
    while n:
        if n & 1:
            pr, pi = _cmul(pr, pi, br, bi)
        n >>= 1
        if n:
            br, bi = _cmul(br, bi, br, bi)
    cr, ci = jnp.zeros_like(a1r), jnp.zeros_like(a1i)
    for j in (range(N_SEG) if forward else range(N_SEG - 1, -1, -1)):
        s_re[j:j + 1, :] = cr
        s_im[j:j + 1, :] = ci
        nr, ni = _cmul(pr, pi, cr, ci)
        cr = nr + ere_ref[j:j + 1, :]
        ci = ni + eim_ref[j:j + 1, :]


def _scan_specs(lp, forward):
    lseg, tt, nt = _scan_tiles(lp)

    def tile(t):
        return t if forward else nt - 1 - t

    rows = lambda w: pl.BlockSpec((tt * N_SEG, w), lambda cb, t: (tile(t), cb))
    proj = pl.BlockSpec((1, CH_BLK, COL_BLK), lambda cb, t: (cb, 0, 0))
    slab = pl.BlockSpec((N_SEG, COL_BLK), lambda cb, t: (0, cb))
    return lseg, tt, nt, rows, proj, slab


def _scan_ends(name, urows, wre4, wim4, ar8, ai8, forward):
    lp = urows.shape[0]
    lseg, tt, nt, rows, proj, slab = _scan_specs(lp, forward)

    def body(u_ref, wre_ref, wim_ref, ar_ref, ai_ref, ere_o, eim_o, bre_s, bim_s, cr_s, ci_s):
        t = pl.program_id(1)

        @pl.when(t == 0)
        def _():
            cr_s[...] = jnp.zeros(cr_s.shape, F32)
            ci_s[...] = jnp.zeros(ci_s.shape, F32)

        u = u_ref[...]
        bre_s[...] = _mm(u, wre_ref[0])
        bim_s[...] = _mm(u, wim_ref[0])
        cr, ci = _scan_steps(tt, forward, bre_s, bim_s, ar_ref[...], ai_ref[...], (cr_s[...], ci_s[...]),
                             lambda r0, x, x_prev, extra: ())
        cr_s[...] = cr
        ci_s[...] = ci

        @pl.when(t == nt - 1)
        def _():
            ere_o[...] = cr
            eim_o[...] = ci

    return pl.pallas_call(
        body, name=name, grid=(N_COL_BLK, nt), in_specs=[rows(CH_BLK), proj, proj, slab, slab],
        out_specs=[slab, slab], out_shape=[jax.ShapeDtypeStruct((N_SEG, N_STATES), F32)] * 2,
        scratch_shapes=[pltpu.VMEM((tt * N_SEG, COL_BLK), F32)] * 2 + [pltpu.VMEM((N_SEG, COL_BLK), F32)] * 2,
        compiler_params=_cparams("arbitrary", "arbitrary"))(urows, wre4, wim4, ar8, ai8)


def _scan_fwd(name, urows, wre4, wim4, ar8, ai8, ere, eim, cre4, cim4, forward):
    lp = urows.shape[0]
    lseg, tt, nt, rows, proj, slab = _scan_specs(lp, forward)

    def body(u_ref, wre_ref, wim_ref, ar_ref, ai_ref, ere_ref, eim_ref, cre_ref, cim_ref,
             xre_o, xim_o, y_o, bre_s, bim_s, cr_s, ci_s):
        ar, ai = ar_ref[...], ai_ref[...]

        @pl.when(pl.program_id(1) == 0)
        def _():
            _segment_starts(lseg, forward, ar, ai, ere_ref, eim_ref, cr_s, ci_s)

        u = u_ref[...]
        bre_s[...] = _mm(u, wre_ref[0])
        bim_s[...] = _mm(u, wim_ref[0])

        def visit(r0, x, x_prev, extra):
            xre_o[pl.ds(r0, N_SEG), :] = x[0]
            xim_o[pl.ds(r0, N_SEG), :] = x[1]
            return ()

        cr, ci = _scan_steps(tt, forward, bre_s, bim_s, ar, ai, (cr_s[...], ci_s[...]), visit)
        cr_s[...] = cr
        ci_s[...] = ci
        y_o[...] = _mm_nt(xre_o[...], cre_ref[0]) + _mm_nt(xim_o[...], cim_ref[0])

    return pl.pallas_call(
        body, name=name, grid=(N_COL_BLK, nt),
        in_specs=[rows(CH_BLK), proj, proj, slab, slab, slab, slab, proj, proj],
        out_specs=[rows(COL_BLK), rows(COL_BLK), rows(CH_BLK)],
        out_shape=[jax.ShapeDtypeStruct((lp, N_STATES), F32)] * 2 + [jax.ShapeDtypeStruct((lp, D_SSM), F32)],
        scratch_shapes=[pltpu.VMEM((tt * N_SEG, COL_BLK), F32)] * 2 + [pltpu.VMEM((N_SEG, COL_BLK), F32)] * 2,
        compiler_params=_cparams("arbitrary", "arbitrary"))(urows, wre4, wim4, ar8, ai8, ere, eim, cre4, cim4)


def _scan_bwd(name, dyrows, cre4, cim4, ar8, ai8, ere, eim, urows, wre4, wim4, xre, xim, forward):
    lp = urows.shape[0]
    lseg, tt, nt, rows, proj, slab = _scan_specs(lp, forward)

    def body(dy_ref, cre_ref, cim_ref, ar_ref, ai_ref, ere_ref, eim_ref, u_ref, wre_ref, wim_ref, xre_ref, xim_ref,
             du_o, dwre_o, dwim_o, dcre_o, dcim_o, dare_o, daim_o, bre_s, bim_s, gre_s, gim_s, cr_s, ci_s):
        t = pl.program_id(1)
        ar, ai = ar_ref[...], ai_ref[...]

        @pl.when(t == 0)
        def _():
            _segment_starts(lseg, forward, ar, ai, ere_ref, eim_ref, cr_s, ci_s)
            dare_o[...] = jnp.zeros(dare_o.shape, F32)
            daim_o[...] = jnp.zeros(daim_o.shape, F32)

        dy = dy_ref[...]
        bre_s[...] = _mm(dy, cre_ref[0])
        bim_s[...] = _mm(dy, cim_ref[0])

        def visit(r0, g, g_prev, sums):
            gre_s[pl.ds(r0, N_SEG), :] = g[0]
            gim_s[pl.ds(r0, N_SEG), :] = g[1]
            fr = xre_ref[pl.ds(r0, N_SEG), :]
            fi = xim_ref[pl.ds(r0, N_SEG), :]
            pr, pi = g_prev
            return sums[0] + fr * pr + fi * pi, sums[1] + fr * pi - fi * pr

        out = _scan_steps(tt, forward, bre_s, bim_s, ar, ai, (cr_s[...], ci_s[...], dare_o[...], daim_o[...]), visit)
        cr_s[...] = out[0]
        ci_s[...] = out[1]
        dare_o[...] = out[2]
        daim_o[...] = out[3]
        gre, gim = gre_s[...], gim_s[...]
        du_o[...] = _mm_nt(gre, wre_ref[0]) + _mm_nt(gim, wim_ref[0])
        u = u_ref[...]
        first = t == 0
        _acc(dwre_o, _mm_tn(u, gre)[None], first)
        _acc(dwim_o, _mm_tn(u, gim)[None], first)
        _acc(dcre_o, _mm_tn(dy, xre_ref[...])[None], first)
        _acc(dcim_o, _mm_tn(dy, xim_ref[...])[None], first)

    big = pltpu.VMEM((tt * N_SEG, COL_BLK), F32)
    small = pltpu.VMEM((N_SEG, COL_BLK), F32)
    return pl.pallas_call(
        body, name=name, grid=(N_COL_BLK, nt),
        in_specs=[rows(CH_BLK), proj, proj, slab, slab, slab, slab, rows(CH_BLK), proj, proj,
                  rows(COL_BLK), rows(COL_BLK)],
        out_specs=[rows(CH_BLK), proj, proj, proj, proj, slab, slab],
        out_shape=[jax.ShapeDtypeStruct((lp, D_SSM), F32)]
        + [jax.ShapeDtypeStruct((N_COL_BLK, CH_BLK, COL_BLK), F32)] * 4
        + [jax.ShapeDtypeStruct((N_SEG, N_STATES), F32)] * 2,
        scratch_shapes=[big, big, big, big, small, small],
        compiler_params=_cparams("arbitrary", "arbitrary"))(
            dyrows, cre4, cim4, ar8, ai8, ere, eim, urows, wre4, wim4, xre, xim)


def _ssm_post(yf, yb, u, sg, wglu_b, bglu, sw, dvec, tr):
    def body(yf_ref, yb_ref, u_ref, g_ref, w_ref, b_ref, sw_ref, d_ref, ypre_o, glu_o, ysn_o):
        ypre = yf_ref[...] + yb_ref[...] + d_ref[...] * u_ref[...]
        ypre_o[...] = ypre
        glu = _mm(_gelu_and_grad(ypre)[0], w_ref[...]) + b_ref[...]
        glu_o[...] = glu
        t = glu[:, :D_SSM] * _sigmoid(glu[:, D_SSM:]) * _silu_and_grad(g_ref[...])[0]
        ysn_o[...] = t * _rms(t) * sw_ref[...]

    return _rows_call("ssm_post", body, tr, [yf, yb, u, sg], [wglu_b, bglu, sw, dvec],
                      [(512, F32), (1024, F32), (512, F32)], [])


def _out_loss(ya, ysn, h, tgt, wo_b, post_w, n_valid, tr):
    lseg = h.shape[0] // N_SEG

    def body(ya_ref, ys_ref, h_ref, t_ref, w_ref, pw_ref, dy_o, dout_o, loss_o, dpw_o):
        i = pl.program_id(0)
        y = _mm(ya_ref[...], w_ref[0:D_ATTN, :]) + _mm(ys_ref[...], w_ref[D_ATTN:, :])
        r = _rms(y)
        pw = pw_ref[...]
        out = h_ref[...] + y * r * pw
        pos = _row_position(i * tr + lax.broadcasted_iota(jnp.int32, (tr, 1), 0), lseg)
        valid = jnp.logical_and(pos >= N_META, pos < n_valid)
        diff = jnp.where(valid, out - t_ref[...], 0.0)
        dout = diff * (1.0 / D_MODEL)
        dy, dpw = _rms_bwd(dout, y, r, pw)
        dy_o[...] = dy
        dout_o[...] = dout
        _acc(loss_o, 0.5 * jnp.sum(jnp.sum(diff * diff, axis=1, keepdims=True), axis=0, keepdims=True)
             * (1.0 / D_MODEL), i == 0)
        _acc(dpw_o, dpw, i == 0)

    return _rows_call("out_loss", body, tr, [ya, ysn, h, tgt], [wo_b, post_w], [(1024, F32), (1024, F32)],
                      [((1, 1), F32), ((1, D_MODEL), F32)])


def _out_bwd(dy, ya, ysn, o_flat, ag, wo_b, aw, head_sum, tr):
    def body(dy_ref, ya_ref, ys_ref, o_ref, g_ref, w_ref, aw_ref, hs_ref, do_o, dag_o, dysn_o, dl_o, dwo_o, daw_o):
        i = pl.program_id(0)
        dy = dy_ref[...]
        dcat = _mm_nt(dy, w_ref[...])
        cat = jnp.concatenate([ya_ref[...], ys_ref[...]], axis=1)
        _acc(dwo_o, _mm_tn(cat, dy), i == 0)
        dysn_o[...] = dcat[:, D_ATTN:]
        o = o_ref[...]
        sl, dsl = _silu_and_grad(g_ref[...])
        t = o * sl
        dt, daw = _rms_bwd(dcat[:, :D_ATTN], t, _rms(t), aw_ref[...])
        _acc(daw_o, daw, i == 0)
        do = dt * sl
        do_o[...] = do
        dag_o[...] = dt * o * dsl
        dl_o[...] = _mm_exact(do * o, hs_ref[...])

    return _rows_call("out_bwd", body, tr, [dy, ya, ysn, o_flat, ag], [wo_b, aw, head_sum],
                      [(512, F32), (512, F32), (512, F32), (HEADS, F32)],
                      [((D_MODEL, D_MODEL), F32), ((1, D_ATTN), F32)])


def _ssm_post_bwd(dysn, glu, sg, ypre, u, wglu_b, sw, dvec, tr):
    def body(d_ref, glu_ref, sg_ref, y_ref, u_ref, w_ref, sw_ref, dv_ref,
             dyp_o, dsg_o, dwg_o, dbg_o, dsw_o, dd_o):
        i = pl.program_id(0)
        glu = glu_ref[...]
        a, b = glu[:, :D_SSM], glu[:, D_SSM:]
        sb = _sigmoid(b)
        ys = a * sb
        sl, dsl = _silu_and_grad(sg_ref[...])
        t = ys * sl
        dt, dsw = _rms_bwd(d_ref[...], t, _rms(t), sw_ref[...])
        _acc(dsw_o, dsw, i == 0)
        dsg_o[...] = dt * ys * dsl
        dys = dt * sl
        dglu = jnp.concatenate([dys * sb, dys * a * sb * (1.0 - sb)], axis=1)
        _acc(dbg_o, jnp.sum(dglu, axis=0, keepdims=True), i == 0)
        gel, dgel = _gelu_and_grad(y_ref[...])
        _acc(dwg_o, _mm_tn(gel, dglu), i == 0)
        dyp = _mm_nt(dglu, w_ref[...]) * dgel
        dyp_o[...] = dyp
        _acc(dd_o, jnp.sum(dyp * u_ref[...], axis=0, keepdims=True), i == 0)

    return _rows_call("ssm_post_bwd", body, tr, [dysn, glu, sg, ypre, u], [wglu_b, sw, dvec],
                      [(512, F32), (512, F32)],
                      [((D_SSM, 2 * D_SSM), F32), ((1, 2 * D_SSM), F32), ((1, D_SSM), F32), ((1, D_SSM), F32)])


def _attn_bwd(qt, k, kt, v, dot, lse_t, delta_t, tk):
    _, nq, _, tq = qt.shape
    lp = k.shape[1]
    nk = lp // tk
    assert lse_t.shape == (HEADS, nq, 1, tq) and delta_t.shape == (HEADS, nq, 1, tq)

    def body(q_ref, k_ref, kt_ref, v_ref, do_ref, lse_ref, dl_ref, dq_o, dk_o, dv_o, dk_s, dv_s):
        @pl.when(pl.program_id(1) == 0)
        def _():
            dq_o[...] = jnp.zeros(dq_o.shape, F32)

        dk_s[...] = jnp.zeros(dk_s.shape, F32)
        dv_s[...] = jnp.zeros(dv_s.shape, F32)
        kk = k_ref[0]
        kkt = kt_ref[0]
        vv = v_ref[0]

        def chunk(c, carry):
            qq = q_ref[0, c]
            dd = do_ref[0, c]
            pt = jnp.exp2(_mm(kk, qq) - lse_ref[0, c])
            dv_s[...] += _mm_nt(dd, pt)
            dst = (pt * (_mm(vv, dd) - dl_ref[0, c])).astype(BF16)
            dk_s[...] += _mm_nt(qq, dst)
            dq_o[0, c] += _mm(kkt, dst)
            return carry

        n_loop = nq - nq % ATTN_BWD_UNROLL
        if n_loop:
            lax.fori_loop(0, n_loop, chunk, 0, unroll=ATTN_BWD_UNROLL)
        for c in range(n_loop, nq):
            chunk(c, 0)
        dk_o[0] = dk_s[...]
        dv_o[0] = dv_s[...]

    head = lambda w: pl.BlockSpec((1, nq, w, tq), lambda h, j: (h, 0, 0, 0))
    rows = lambda w: pl.BlockSpec((1, tk, w), lambda h, j: (h, j, 0))
    cols = lambda w: pl.BlockSpec((1, w, tk), lambda h, j: (h, 0, j))
    return pl.pallas_call(
        body, name="attn_bwd", grid=(HEADS, nk),
        in_specs=[head(QK_DIM), rows(QK_DIM), cols(QK_DIM), rows(V_HEAD), head(V_HEAD), head(1), head(1)],
        out_specs=[head(QK_DIM), cols(QK_DIM), cols(V_HEAD)],
        out_shape=[jax.ShapeDtypeStruct((HEADS, nq, QK_DIM, tq), F32), jax.ShapeDtypeStruct((HEADS, QK_DIM, lp), F32),
                   jax.ShapeDtypeStruct((HEADS, V_HEAD, lp), F32)],
        scratch_shapes=[pltpu.VMEM((QK_DIM, tk), F32), pltpu.VMEM((V_HEAD, tk), F32)],
        compiler_params=_cparams("arbitrary", "arbitrary"))(qt, k, kt, v, dot, lse_t, delta_t)


def _qkv_up_bwd(dqn, dr1, dr2, dkn, dv, dkr8, ql, kvl, cos8, sin8, c32, s32, qw, kvw, wq_b, wkv_b, p32, sum8, tr):
    def body(dqn_ref, dr1_ref, dr2_ref, dkn_ref, dv_ref, dkr_ref, ql_ref, kvl_ref, cos_ref, sin_ref, c32_ref,
             s32_ref, qw_ref, kvw_ref, wq_ref, wkv_ref, p_ref, s8_ref,
             dql_o, dkvl_o, dkrr_o, dwq_o, dwkv_o, dqw_o, dkvw_o):
        i = pl.program_id(0)
        cs, sn = cos_ref[...], sin_ref[...]
        d1, d2 = dr1_ref[...], dr2_ref[...]
        dq = jnp.concatenate([dqn_ref[...], d1 * cs + d2 * sn, d2 * cs - d1 * sn], axis=1) * (Q_SCALE / LOG2E)
        x = ql_ref[...]
        r = _rms(x)
        qw = qw_ref[...]
        _acc(dwq_o, _mm_tn(x * r * qw, dq), i == 0)
        dx, dw = _rms_bwd(_mm_nt(dq, wq_ref[...]), x, r, qw)
        dql_o[...] = dx
        _acc(dqw_o, dw, i == 0)
        dkv = jnp.concatenate([dkn_ref[...] * (1.0 / LOG2E), dv_ref[...]], axis=1)
        x = kvl_ref[...]
        r = _rms(x)
        kvw = kvw_ref[...]
        _acc(dwkv_o, _mm_tn(x * r * kvw, dkv), i == 0)
        dx, dw = _rms_bwd(_mm_nt(dkv, wkv_ref[...]), x, r, kvw)
        dkvl_o[...] = dx
        _acc(dkvw_o, dw, i == 0)
        dkr = _mm_exact(dkr_ref[...], s8_ref[...]) * (1.0 / LOG2E)
        dkrr_o[...] = dkr * c32_ref[...] + _mm_exact(dkr * s32_ref[...], p_ref[...])

    return _rows_call("qkv_up_bwd", body, tr, [dqn, dr1, dr2, dkn, dv, dkr8, ql, kvl, cos8, sin8, c32, s32],
                      [qw, kvw, wq_b, wkv_b, p32, sum8], [(256, F32), (128, F32), (32, F32)],
                      [((Q_LORA, 768), F32), ((KV_LORA, 1024), F32), ((1, Q_LORA), F32), ((1, KV_LORA), F32)])


def _inproj_bwd(dql, dkvl, dag, du_f, du_b, dypre, dsg, dkr, h, dout, pre_w, w_in_b, dvec, tr):
    def body(dql_ref, dkvl_ref, dag_ref, duf_ref, dub_ref, dyp_ref, dsg_ref, dkr_ref, h_ref, dout_ref,
             pw_ref, w_ref, dv_ref, dh_o, dwin_o, dpw_o):
        i = pl.program_id(0)
        du = duf_ref[...] + dub_ref[...] + dv_ref[...] * dyp_ref[...]
        dproj = jnp.concatenate([dql_ref[...], dkvl_ref[...], dag_ref[...], du, dsg_ref[...],
                                 dkr_ref[...], jnp.zeros((tr, D_IN_PAD - D_IN), F32)], axis=1)
        x = h_ref[...]
        r = _rms(x)
        pw = pw_ref[...]
        _acc(dwin_o, _mm_tn(x * r * pw, dproj), i == 0)
        dx, dw = _rms_bwd(_mm_nt(dproj, w_ref[...]), x, r, pw)
        _acc(dpw_o, dw, i == 0)
        dh_o[...] = dout_ref[...] + dx

    return _rows_call("inproj_bwd", body, tr, [dql, dkvl, dag, du_f, du_b, dypre, dsg, dkr, h, dout],
                      [pre_w, w_in_b, dvec], [(1024, F32)], [((D_MODEL, D_IN_PAD), F32), ((1, D_MODEL), F32)])


def _disc_terms(a_re, a_im, ldt):
    dt = jnp.exp(ldt)
    mag = jnp.exp(a_re * dt)
    th = a_im * dt
    cs, sn = jnp.cos(th), jnp.sin(th)
    abar_re, abar_im = mag * cs, mag * sn
    num_re, num_im = abar_re - 1.0, abar_im
    den = a_re * a_re + a_im * a_im
    coef_re = (num_re * a_re + num_im * a_im) / den
    coef_im = (num_im * a_re - num_re * a_im) / den
    return dt, mag, cs, sn, abar_re, abar_im, num_re, num_im, den, coef_re, coef_im


def _ssm_disc(a_re, a_im, ldt, bt_re, bt_im):
    def body(ar_ref, ai_ref, l_ref, br_ref, bi_ref, abr_o, abi_o, bbr_o, bbi_o):
        t = _disc_terms(ar_ref[...], ai_ref[...], l_ref[...])
        abr_o[...] = t[4]
        abi_o[...] = t[5]
        cr, ci = t[9], t[10]
        br, bi = br_ref[...], bi_ref[...]
        bbr_o[...] = cr * br - ci * bi
        bbi_o[...] = cr * bi + ci * br

    ng = a_re.shape[0]
    return pl.pallas_call(
        body, name="ssm_disc",
        out_shape=[jax.ShapeDtypeStruct((ng, 1, SSM_STATE), F32)] * 2
        + [jax.ShapeDtypeStruct((ng, SSM_GROUP, SSM_STATE), F32)] * 2)(a_re, a_im, ldt, bt_re, bt_im)


def _ssm_disc_bwd(a_re, a_im, ldt, bt_re, bt_im, da8_re, da8_im, dbb_re, dbb_im):
    def body(ar_ref, ai_ref, l_ref, br_ref, bi_ref, dar_ref, dai_ref, dbr_ref, dbi_ref,
             gar_o, gai_o, gl_o, gbr_o, gbi_o):
        a_re, a_im = ar_ref[...], ai_ref[...]
        dt, mag, cs, sn, abar_re, abar_im, num_re, num_im, den, cr, ci = _disc_terms(a_re, a_im, l_ref[...])
        br, bi = br_ref[...], bi_ref[...]
        dbr, dbi = dbr_ref[...], dbi_ref[...]
        gbr_o[...] = cr * dbr + ci * dbi
        gbi_o[...] = cr * dbi - ci * dbr
        dcr = jnp.sum(br * dbr + bi * dbi, axis=1, keepdims=True)
        dci = jnp.sum(br * dbi - bi * dbr, axis=1, keepdims=True)
        dnum_re = (dcr * a_re - dci * a_im) / den
        dnum_im = (dcr * a_im + dci * a_re) / den
        dden = -(dcr * cr + dci * ci) / den
        g_are = (dcr * num_re + dci * num_im) / den + dden * 2.0 * a_re
        g_aim = (dcr * num_im - dci * num_re) / den + dden * 2.0 * a_im
        d_abr = jnp.sum(dar_ref[...], axis=1, keepdims=True) + dnum_re
        d_abi = jnp.sum(dai_ref[...], axis=1, keepdims=True) + dnum_im
        dmag = d_abr * cs + d_abi * sn
        dth = d_abi * abar_re - d_abr * abar_im
        g_are = g_are + dmag * mag * dt
        g_aim = g_aim + dth * dt
        ddt = jnp.sum(dmag * mag * a_re + dth * a_im, axis=2, keepdims=True)
        gar_o[...] = g_are
        gai_o[...] = g_aim
        gl_o[...] = ddt * dt

    ng = a_re.shape[0]
    return pl.pallas_call(
        body, name="ssm_disc_bwd",
        out_shape=[jax.ShapeDtypeStruct((ng, 1, SSM_STATE), F32)] * 2 + [jax.ShapeDtypeStruct((ng, 1, 1), F32)]
        + [jax.ShapeDtypeStruct((ng, SSM_GROUP, SSM_STATE), F32)] * 2)(
            a_re, a_im, ldt, bt_re, bt_im, da8_re, da8_im, dbb_re, dbb_im)


def _exchange(name, per_peer, shared):
    parts = [a for a in (per_peer, shared) if a is not None]
    rp = per_peer.shape[1] if per_peer is not None else 0
    rs = shared.shape[0] if shared is not None else 0
    n = len(parts)

    def body(*refs):
        in_refs, out_ref, send_sems, recv_sems, local_sems = refs[:n], refs[n], refs[n + 1], refs[n + 2], refs[n + 3]
        x, y, c = lax.axis_index("x"), lax.axis_index("y"), lax.axis_index("c")
        me = 4 * x + 2 * y + c

        def pieces(peer):
            out = []
            if per_peer is not None:
                out.append((in_refs[0].at[peer], out_ref.at[me, pl.ds(0, rp), :]))
            if shared is not None:
                out.append((in_refs[-1], out_ref.at[me, pl.ds(rp, rs), :]))
            return out

        copies = []
        for k in range(1, N_DEV):
            px = 1 - x if (k >> 2) & 1 else x
            py = 1 - y if (k >> 1) & 1 else y
            pc = 1 - c if k & 1 else c
            for j, (src, dst) in enumerate(pieces(4 * px + 2 * py + pc)):
                s = (k - 1) * n + j
                copies.append(pltpu.make_async_remote_copy(
                    src_ref=src, dst_ref=dst, send_sem=send_sems.at[s], recv_sem=recv_sems.at[s],
                    device_id=(px, py, pc), device_id_type=pl.DeviceIdType.MESH))
        mine = [pltpu.make_async_copy(src, dst, local_sems.at[j]) for j, (src, dst) in enumerate(pieces(me))]
        for cp in mine + copies:
            cp.start()
        for cp in copies + mine:
            cp.wait()

    n_sem = (N_DEV - 1) * n
    return pl.pallas_call(
        body, name=name, out_shape=jax.ShapeDtypeStruct((N_DEV, rp + rs, LANES), F32),
        in_specs=[pl.BlockSpec(memory_space=pl.ANY)] * n, out_specs=pl.BlockSpec(memory_space=pl.ANY),
        scratch_shapes=[pltpu.SemaphoreType.DMA((n_sem,)), pltpu.SemaphoreType.DMA((n_sem,)),
                        pltpu.SemaphoreType.DMA((n,))])(*parts)


def _adamw(recv, w, m, v, tr):
    rows = w.shape[0]
    c1 = 1.0 - ADAM_B1 ** ADAM_STEP
    c2 = 1.0 - ADAM_B2 ** ADAM_STEP

    def body(r_ref, w_ref, m_ref, v_ref, g_o, d_o, m_o, v_o):
        g = r_ref[0]
        for k in range(1, N_DEV):
            g = g + r_ref[k]
        mm = ADAM_B1 * m_ref[...] + (1.0 - ADAM_B1) * g
        vv = ADAM_B2 * v_ref[...] + (1.0 - ADAM_B2) * (g * g)
        g_o[...] = g
        m_o[...] = mm
        v_o[...] = vv
        d_o[...] = -ADAM_LR * ((mm / c1) / (jnp.sqrt(vv / c2) + ADAM_EPS) + ADAM_WD * w_ref[...])

    spec = pl.BlockSpec((tr, LANES), lambda i: (i, 0))
    return pl.pallas_call(
        body, name="adamw", grid=(rows // tr,),
        in_specs=[pl.BlockSpec((N_DEV, tr, LANES), lambda i: (0, i, 0)), spec, spec, spec],
        out_specs=[spec] * 4, out_shape=[jax.ShapeDtypeStruct((rows, LANES), F32)] * 4,
        compiler_params=_cparams("arbitrary"))(recv, w, m, v)


def _to_rows(a):
    flat = a.reshape(-1)
    pad = (-flat.shape[0]) % LANES
    if pad:
        flat = jnp.concatenate([flat, jnp.zeros((pad,), flat.dtype)])
    return flat.reshape(-1, LANES)


def _n_rows(shape):
    return -(-int(np.prod(shape)) // LANES)


def _pack(arrays, total_rows):
    rows = [_to_rows(a) for a in arrays]
    used = sum(r.shape[0] for r in rows)
    if total_rows > used:
        rows.append(jnp.zeros((total_rows - used, LANES), F32))
    return jnp.concatenate(rows, axis=0)


def _unpack(buf, shapes):
    lead = buf.shape[:-2]
    out, r0 = [], 0
    for s in shapes:
        n = int(np.prod(s))
        nr = _n_rows(s)
        out.append(buf[..., r0:r0 + nr, :].reshape(lead + (-1,))[..., :n].reshape(lead + tuple(s)))
        r0 += nr
    return out


def _pack_per_device(arrays, total_rows):
    rows = []
    for a in arrays:
        flat = a.reshape(N_DEV, -1)
        pad = (-flat.shape[1]) % LANES
        if pad:
            flat = jnp.concatenate([flat, jnp.zeros((N_DEV, pad), flat.dtype)], axis=1)
        rows.append(flat.reshape(N_DEV, -1, LANES))
    used = sum(r.shape[1] for r in rows)
    if total_rows > used:
        rows.append(jnp.zeros((N_DEV, total_rows - used, LANES), F32))
    return jnp.concatenate(rows, axis=1)


def _shard_views(name, full):
    if name == 'w_out':
        return full.reshape(N_DEV, full.shape[0] // N_DEV, full.shape[1])
    r, ccols = full.shape
    return full.reshape(r, N_DEV, ccols // N_DEV).transpose(1, 0, 2)


def _from_shards(name, stacked):
    if name == 'w_out':
        return stacked.reshape(-1, stacked.shape[-1])
    n, r, cc = stacked.shape
    return stacked.transpose(1, 0, 2).reshape(r, n * cc)


GROUPS_PER_BLK = N_GROUPS // N_COL_BLK


def _block_diag(t):
    eye = jnp.eye(GROUPS_PER_BLK, dtype=t.dtype)
    t4 = t.reshape(N_COL_BLK, GROUPS_PER_BLK, SSM_GROUP, SSM_STATE)
    return (t4[:, :, :, None, :] * eye[None, :, None, :, None]).reshape(N_COL_BLK, CH_BLK, COL_BLK)


def _diag_blocks(mat4):
    eye = jnp.eye(GROUPS_PER_BLK, dtype=mat4.dtype)
    m6 = mat4.reshape(N_COL_BLK, GROUPS_PER_BLK, SSM_GROUP, GROUPS_PER_BLK, SSM_STATE)
    return (m6 * eye[None, :, None, :, None]).sum(axis=3).reshape(N_GROUPS, SSM_GROUP, SSM_STATE)


def _step(x, loss_target, wts, moms, vels):
    seq = x.shape[1]
    n_valid = N_META + seq
    lp = -(-n_valid // 256) * 256
    tr = _pick(lp, [640, 256])
    tr_mid = _pick(lp, [320, 256])
    tq = _pick(lp, [1280, 256])
    tk = _pick(lp, [640, 256])

    shard_shapes = [wts[n].shape[-2:] for n in SHARDED]
    n_shard_rows = sum(_n_rows(s) for s in shard_shapes)
    gathered = _exchange("gather_weights", None,
                         _pack([wts[n].reshape(wts[n].shape[-2:]) for n in SHARDED], n_shard_rows))
    full = {n: _from_shards(n, a) for n, a in zip(SHARDED, _unpack(gathered, shard_shapes))}

    w_in_b = jnp.concatenate([_cols_in(full['w_in']), jnp.zeros((D_MODEL, D_IN_PAD - D_IN), F32)],
                             axis=1).astype(BF16)
    wq_b = _cols_q(full['w_q_up']).astype(BF16)
    wkv_b = _cols_kv(full['w_kv_up']).astype(BF16)
    wglu_b = full['w_glu'].astype(BF16)
    wo_b = full['w_out'].astype(BF16)
    pre_w, post_w = wts['pre_norm_w'], wts['post_norm_w']
    qw, kvw, aw, sw = wts['q_norm_w'], wts['kv_norm_w'], wts['attn_out_norm_w'], wts['ssm_out_norm_w']
    bglu, dvec = wts['b_glu'], wts['ssm_d']

    lseg = lp // N_SEG
    pos = _row_position(jnp.arange(lp, dtype=jnp.int32), lseg)
    inv = ROPE_THETA ** (-jnp.arange(HALF_ROPE, dtype=F32) / HALF_ROPE)
    ang = pos.astype(F32)[:, None] * inv[None, :]
    cos, sin = jnp.cos(ang), jnp.sin(ang)
    cos8, sin8 = jnp.tile(cos, (1, HEADS)), jnp.tile(sin, (1, HEADS))
    c32 = jnp.concatenate([cos, cos], axis=1)
    s32 = jnp.concatenate([-sin, sin], axis=1)
    p32 = jnp.asarray(np.roll(np.eye(QK_ROPE, dtype=np.float32), HALF_ROPE, axis=1))
    sum8 = jnp.asarray(np.tile(np.eye(QK_ROPE, dtype=np.float32), (HEADS, 1)))
    head_sum = jnp.asarray(np.repeat(np.eye(HEADS, dtype=np.float32), V_HEAD, axis=0))

    ng = 2 * N_GROUPS
    a_re3 = wts['ssm_a_re'].reshape(ng, 1, SSM_STATE)
    a_im3 = wts['ssm_a_im'].reshape(ng, 1, SSM_STATE)
    ldt3 = wts['ssm_log_dt'].reshape(ng, 1, 1)
    bt_re = wts['ssm_b_re'].reshape(2, N_GROUPS, SSM_STATE, SSM_GROUP).transpose(0, 1, 3, 2).reshape(
        ng, SSM_GROUP, SSM_STATE)
    bt_im = wts['ssm_b_im'].reshape(2, N_GROUPS, SSM_STATE, SSM_GROUP).transpose(0, 1, 3, 2).reshape(
        ng, SSM_GROUP, SSM_STATE)
    c_re = wts['ssm_c_re'].reshape(ng, SSM_GROUP, SSM_STATE)
    c_im = wts['ssm_c_im'].reshape(ng, SSM_GROUP, SSM_STATE)
    abar_re, abar_im, bbt_re, bbt_im = _ssm_disc(a_re3, a_im3, ldt3, bt_re, bt_im)

    def direction(t, d):
        return t[d * N_GROUPS:(d + 1) * N_GROUPS]

    def slab(t, d, sign=1.0):
        return jnp.broadcast_to(sign * direction(t, d).reshape(1, N_STATES), (N_SEG, N_STATES))

    w_re = [_block_diag(direction(bbt_re, d)).astype(BF16) for d in range(2)]
    w_im = [_block_diag(direction(bbt_im, d)).astype(BF16) for d in range(2)]
    cb_re = [_block_diag(direction(c_re, d)).astype(BF16) for d in range(2)]
    cb_im = [_block_diag(-direction(c_im, d)).astype(BF16) for d in range(2)]

    def to_rows(a):
        return a.reshape(N_SEG, lseg, a.shape[-1]).transpose(1, 0, 2).reshape(lp, a.shape[-1])

    def to_tokens(a):
        return a.reshape(lseg, N_SEG, a.shape[-1]).transpose(1, 0, 2).reshape(lp, a.shape[-1])

    pad = jnp.zeros((lp - n_valid, D_MODEL), F32)
    h = to_rows(jnp.concatenate([full['meta_tokens'], x[0], pad], axis=0))
    tgt = to_rows(jnp.concatenate([jnp.zeros((N_META, D_MODEL), F32), loss_target[0], pad], axis=0))

    ql, kvl, ag, su, sg, kr = _inproj(h, pre_w, w_in_b, tr)
    qn_b, qr1_b, qr2_b, kn_b, v_b, kr_b = _qkv_up(ql, kvl, kr, cos8, sin8, c32, s32, qw, kvw, wq_b, wkv_b, p32, tr)

    def heads(a, w):
        return a.reshape(lp, HEADS, w)

    nq, nk = lp // tq, lp // tk
    q_t = jnp.concatenate([heads(qn_b, 64), heads(qr1_b, 16), heads(qr2_b, 16)], axis=-1)
    k_t = jnp.concatenate([heads(kn_b, 64), jnp.broadcast_to(kr_b[:, None, :], (lp, HEADS, QK_ROPE))], axis=-1)
    v_t = heads(v_b, 64)
    vx_t = jnp.concatenate([v_t, jnp.ones((lp, HEADS, 1), BF16), jnp.zeros((lp, HEADS, LANES - V_HEAD - 1), BF16)],
                           axis=-1)
    qt4 = q_t.reshape(nq, tq, HEADS, QK_DIM).transpose(2, 0, 3, 1)
    tk_fwd = _pick(lp, [1280, 256])
    vxt4 = vx_t.reshape(lp // tk_fwd, tk_fwd, HEADS, LANES).transpose(2, 0, 3, 1)
    k_h = k_t.transpose(1, 0, 2)
    kt_h = k_t.transpose(1, 2, 0)
    v_h = v_t.transpose(1, 0, 2)
    ot_h, lse4 = _attn_fwd(qt4, k_h, vxt4, n_valid)
    o_flat = ot_h.transpose(2, 0, 1).reshape(lp, D_ATTN)
    ya = _attn_post(o_flat, ag, aw, tr)

    xs, ys = [], []
    for d in range(2):
        ar8, ai8 = slab(abar_re, d), slab(abar_im, d)
        ere, eim = _scan_ends(f"scan{d}_ends", su, w_re[d], w_im[d], ar8, ai8, d == 0)
        x_re, x_im, y_d = _scan_fwd(f"scan{d}", su, w_re[d], w_im[d], ar8, ai8, ere, eim, cb_re[d], cb_im[d],
                                    d == 0)
        xs += [x_re, x_im]
        ys.append(y_d)
    ypre, glu, ysn = _ssm_post(ys[0], ys[1], su, sg, wglu_b, bglu, sw, dvec, tr)

    dy, dout, loss, d_post = _out_loss(ya, ysn, h, tgt, wo_b, post_w, n_valid, tr)

    do_flat, dag, dysn, delta8, d_wo, d_aw = _out_bwd(dy, ya, ysn, o_flat, ag, wo_b, aw, head_sum, tr)
    dypre, dsg, d_wglu, d_bglu, d_sw, d_dvec = _ssm_post_bwd(dysn, glu, sg, ypre, su, wglu_b, sw, dvec, tr)

    dus, d_ct, d_wb, d_a8 = [], [], [], []
    for d in range(2):
        ar8, ai8c = slab(abar_re, d), slab(abar_im, d, -1.0)
        ere, eim = _scan_ends(f"scan_adj{d}_ends", dypre, cb_re[d], cb_im[d], ar8, ai8c, d != 0)
        du_d, dw_re, dw_im, dc_re, dc_im, da_re, da_im = _scan_bwd(
            f"scan_adj{d}", dypre, cb_re[d], cb_im[d], ar8, ai8c, ere, eim, su, w_re[d], w_im[d],
            xs[2 * d], xs[2 * d + 1], d != 0)
        dus.append(du_d)
        d_ct.append((dc_re, dc_im))
        d_wb.append((dw_re, dw_im))
        d_a8.append((da_re, da_im))

    dot4 = do_flat.astype(BF16).reshape(nq, tq, HEADS, V_HEAD).transpose(2, 0, 3, 1)
    dqt4, dkt_h, dvt_h = _attn_bwd(qt4, k_h, kt_h, v_h, dot4, lse4, delta8.T.reshape(HEADS, nq, 1, tq), tk)
    dq_t = dqt4.transpose(1, 3, 0, 2).reshape(lp, HEADS, QK_DIM)
    dk_t = dkt_h.transpose(2, 0, 1)
    dqn = dq_t[:, :, :64].reshape(lp, 512)
    dr1 = dq_t[:, :, 64:80].reshape(lp, 128)
    dr2 = dq_t[:, :, 80:96].reshape(lp, 128)
    dkn = dk_t[:, :, :64].reshape(lp, 512)
    dkr8 = dk_t[:, :, 64:].reshape(lp, HEADS * QK_ROPE)
    dvf = dvt_h.transpose(2, 0, 1).reshape(lp, 512)
    dql, dkvl, dkrr, d_wq, d_wkv, d_qw, d_kvw = _qkv_up_bwd(
        dqn, dr1, dr2, dkn, dvf, dkr8, ql, kvl, cos8, sin8, c32, s32, qw, kvw, wq_b, wkv_b, p32, sum8, tr)
    dh, d_win, d_pre = _inproj_bwd(dql, dkvl, dag, dus[0], dus[1], dypre, dsg, dkrr, h, dout, pre_w, w_in_b, dvec,
                                   tr_mid)
    dh = to_tokens(dh)

    def seg_sums(t):
        return t.reshape(N_SEG, N_GROUPS, SSM_STATE).transpose(1, 0, 2)

    da8_re = jnp.concatenate([seg_sums(d_a8[d][0]) for d in range(2)], axis=0)
    da8_im = jnp.concatenate([seg_sums(d_a8[d][1]) for d in range(2)], axis=0)
    dbb_re = jnp.concatenate([_diag_blocks(d_wb[d][0]) for d in range(2)], axis=0)
    dbb_im = jnp.concatenate([_diag_blocks(d_wb[d][1]) for d in range(2)], axis=0)
    g_are, g_aim, g_ldt, g_bt_re, g_bt_im = _ssm_disc_bwd(a_re3, a_im3, ldt3, bt_re, bt_im, da8_re, da8_im,
                                                          dbb_re, dbb_im)
    g_c_re = jnp.concatenate([_diag_blocks(d_ct[d][0]) for d in range(2)], axis=0)
    g_c_im = jnp.concatenate([-_diag_blocks(d_ct[d][1]) for d in range(2)], axis=0)

    def b_layout(t):
        return t.reshape(2, N_GROUPS, SSM_GROUP, SSM_STATE).transpose(0, 1, 3, 2)

    local = {
        'meta_tokens': dh[:N_META],
        'pre_norm_w': d_pre, 'post_norm_w': d_post,
        'w_in': _cols_in_inv(d_win[:, :D_IN]),
        'q_norm_w': d_qw, 'w_q_up': _cols_q_inv(d_wq),
        'kv_norm_w': d_kvw, 'w_kv_up': _cols_kv_inv(d_wkv),
        'attn_out_norm_w': d_aw,
        'ssm_a_re': g_are, 'ssm_a_im': g_aim, 'ssm_log_dt': g_ldt,
        'ssm_b_re': b_layout(g_bt_re), 'ssm_b_im': b_layout(g_bt_im), 'ssm_c_re': g_c_re, 'ssm_c_im': g_c_im,
        'ssm_d': d_dvec, 'w_glu': d_wglu, 'b_glu': d_bglu, 'ssm_out_norm_w': d_sw, 'w_out': d_wo,
    }

    replicated = [n for n in WEIGHTS if n not in SHARDED]
    order = SHARDED + replicated
    shapes = [wts[n].shape for n in order] + [(1, 1)]
    tr_adam = 512
    total_rows = -(-sum(_n_rows(s) for s in shapes) // tr_adam) * tr_adam
    recv = _exchange("exchange_grads",
                     _pack_per_device([_shard_views(n, local[n]) for n in SHARDED], n_shard_rows),
                     _pack([local[n] for n in replicated] + [loss], total_rows - n_shard_rows))
    zero = jnp.zeros((1, 1), F32)
    packed = [_pack([src[n] for n in order] + [zero], total_rows) for src in (wts, moms, vels)]
    g_p, d_p, m_p, v_p = _adamw(recv, *packed, tr_adam)
    sums = _unpack(g_p, shapes)
    grads = dict(zip(order, sums))
    deltas, new_m, new_v = (dict(zip(order, _unpack(b, shapes))) for b in (d_p, m_p, v_p))

    grad_x = dh[N_META:n_valid][None]
    return (sums[-1][0, 0], grad_x, *[grads[n] for n in WEIGHTS], *[deltas[n] for n in WEIGHTS],
            *[new_m[n] for n in WEIGHTS], *[new_v[n] for n in WEIGHTS])


def kernel(x, meta_tokens, pre_norm_w, post_norm_w, w_in, q_norm_w, w_q_up, kv_norm_w, w_kv_up, attn_out_norm_w, ssm_a_re, ssm_a_im, ssm_log_dt, ssm_b_re, ssm_b_im, ssm_c_re, ssm_c_im, ssm_d, w_glu, b_glu, ssm_out_norm_w, w_out, loss_target, m_meta_tokens, m_pre_norm_w, m_post_norm_w, m_w_in, m_q_norm_w, m_w_q_up, m_kv_norm_w, m_w_kv_up, m_attn_out_norm_w, m_ssm_a_re, m_ssm_a_im, m_ssm_log_dt, m_ssm_b_re, m_ssm_b_im, m_ssm_c_re, m_ssm_c_im, m_ssm_d, m_w_glu, m_b_glu, m_ssm_out_norm_w, m_w_out, v_meta_tokens, v_pre_norm_w, v_post_norm_w, v_w_in, v_q_norm_w, v_w_q_up, v_kv_norm_w, v_w_kv_up, v_attn_out_norm_w, v_ssm_a_re, v_ssm_a_im, v_ssm_log_dt, v_ssm_b_re, v_ssm_b_im, v_ssm_c_re, v_ssm_c_im, v_ssm_d, v_w_glu, v_b_glu, v_ssm_out_norm_w, v_w_out):
    wts = dict(zip(WEIGHTS, (meta_tokens, pre_norm_w, post_norm_w, w_in, q_norm_w, w_q_up, kv_norm_w, w_kv_up,
                             attn_out_norm_w, ssm_a_re, ssm_a_im, ssm_log_dt, ssm_b_re, ssm_b_im, ssm_c_re,
                             ssm_c_im, ssm_d, w_glu, b_glu, ssm_out_norm_w, w_out)))
    moms = dict(zip(WEIGHTS, (m_meta_tokens, m_pre_norm_w, m_post_norm_w, m_w_in, m_q_norm_w, m_w_q_up,
                              m_kv_norm_w, m_w_kv_up, m_attn_out_norm_w, m_ssm_a_re, m_ssm_a_im, m_ssm_log_dt,
                              m_ssm_b_re, m_ssm_b_im, m_ssm_c_re, m_ssm_c_im, m_ssm_d, m_w_glu, m_b_glu,
                              m_ssm_out_norm_w, m_w_out)))
    vels = dict(zip(WEIGHTS, (v_meta_tokens, v_pre_norm_w, v_post_norm_w, v_w_in, v_q_norm_w, v_w_q_up,
                              v_kv_norm_w, v_w_kv_up, v_attn_out_norm_w, v_ssm_a_re, v_ssm_a_im, v_ssm_log_dt,
                              v_ssm_b_re, v_ssm_b_im, v_ssm_c_re, v_ssm_c_im, v_ssm_d, v_w_glu, v_b_glu,
                              v_ssm_out_norm_w, v_w_out)))
    return _step(x, loss_target, wts, moms, vels)
```

```python
import functools
import math

import numpy as np
import jax
import jax.numpy as jnp
from jax import lax
from jax.experimental import pallas as pl
from jax.experimental.pallas import tpu as pltpu

F32 = jnp.float32
BF16 = jnp.bfloat16

D_MODEL = 1024
N_META = 16
EPS = 1e-6
HEADS = 8
QK_NOPE = 64
QK_ROPE = 32
HALF_ROPE = QK_ROPE // 2
QK_DIM = QK_NOPE + QK_ROPE
V_HEAD = 64
Q_LORA = 256
KV_LORA = 128
D_ATTN = HEADS * V_HEAD
D_SSM = 512
SSM_GROUP = 16
N_GROUPS = D_SSM // SSM_GROUP
SSM_STATE = 64
N_STATES = N_GROUPS * SSM_STATE
ROPE_THETA = 10000.0
D_IN = Q_LORA + KV_LORA + QK_ROPE + D_ATTN + 2 * D_SSM
D_IN_PAD = 2048
N_DEV = 8
N_SEG = 8
COL_BLK = 512
LANES = 128

ADAM_LR = 0.001
ADAM_B1 = 0.9
ADAM_B2 = 0.999
ADAM_EPS = 1e-08
ADAM_WD = 0.01
ADAM_STEP = 10

VMEM_LIMIT_V7X = 56 * 1024 * 1024
LOG2E = 1.0 / math.log(2.0)
Q_SCALE = LOG2E / math.sqrt(QK_DIM)
ATTN_UNROLL = 4
ATTN_BWD_UNROLL = 4

WEIGHTS = ['meta_tokens', 'pre_norm_w', 'post_norm_w', 'w_in', 'q_norm_w', 'w_q_up', 'kv_norm_w', 'w_kv_up',
           'attn_out_norm_w', 'ssm_a_re', 'ssm_a_im', 'ssm_log_dt', 'ssm_b_re', 'ssm_b_im', 'ssm_c_re', 'ssm_c_im',
           'ssm_d', 'w_glu', 'b_glu', 'ssm_out_norm_w', 'w_out']
SHARDED = ['w_in', 'w_q_up', 'w_kv_up', 'w_glu', 'w_out', 'meta_tokens']

def _cols_in(w):
    return jnp.concatenate([w[:, 0:384], w[:, 416:D_IN], w[:, 384:416]], axis=1)


def _cols_in_inv(w):
    return jnp.concatenate([w[:, 0:384], w[:, D_IN - QK_ROPE:D_IN], w[:, 384:D_IN - QK_ROPE]], axis=1)


def _cols_q(w):
    t = w.reshape(w.shape[0], HEADS, QK_DIM)
    return jnp.concatenate([t[:, :, 0:64].reshape(-1, 512), t[:, :, 64:80].reshape(-1, 128),
                            t[:, :, 80:96].reshape(-1, 128)], axis=1)


def _cols_q_inv(w):
    r = w.shape[0]
    return jnp.concatenate([w[:, 0:512].reshape(r, HEADS, 64), w[:, 512:640].reshape(r, HEADS, 16),
                            w[:, 640:768].reshape(r, HEADS, 16)], axis=2).reshape(r, HEADS * QK_DIM)


def _cols_kv(w):
    t = w.reshape(w.shape[0], HEADS, 128)
    return jnp.concatenate([t[:, :, 0:64].reshape(-1, 512), t[:, :, 64:128].reshape(-1, 512)], axis=1)


def _cols_kv_inv(w):
    r = w.shape[0]
    return jnp.concatenate([w[:, 0:512].reshape(r, HEADS, 64), w[:, 512:1024].reshape(r, HEADS, 64)],
                           axis=2).reshape(r, HEADS * 128)


def _pick(n, cands):
    for c in cands:
        if n % c == 0:
            return c
    raise ValueError(f"no tile for {n}")


def _cparams(*sem):
    return pltpu.CompilerParams(dimension_semantics=sem, vmem_limit_bytes=VMEM_LIMIT_V7X)


def _mm(a, b):
    return jnp.dot(a.astype(BF16), b.astype(BF16), preferred_element_type=F32)


def _mm_nt(a, b):
    return lax.dot_general(a.astype(BF16), b.astype(BF16), (((1,), (1,)), ((), ())), preferred_element_type=F32)


def _mm_tn(a, b):
    return lax.dot_general(a.astype(BF16), b.astype(BF16), (((0,), (0,)), ((), ())), preferred_element_type=F32)


def _mm_exact(a, b):
    return jnp.dot(a, b, precision=lax.Precision.HIGHEST, preferred_element_type=F32)


def _rms(x):
    return lax.rsqrt(jnp.mean(x * x, axis=-1, keepdims=True) + EPS)


def _rms_bwd(dy, x, r, w):
    xh = x * r
    g = dy * w
    dx = r * (g - xh * jnp.mean(g * xh, axis=-1, keepdims=True))
    dw = jnp.sum(dy * xh, axis=0, keepdims=True)
    return dx, dw


def _sigmoid(z):
    return 1.0 / (1.0 + jnp.exp(-z))


def _silu_and_grad(z):
    s = _sigmoid(z)
    return z * s, s * (1.0 + z * (1.0 - s))


_GELU_C = math.sqrt(2.0 / math.pi)


def _gelu_and_grad(x):
    x2 = x * x
    t = jnp.tanh(_GELU_C * (x + 0.044715 * x * x2))
    val = 0.5 * x * (1.0 + t)
    grad = 0.5 * (1.0 + t) + 0.5 * x * (1.0 - t * t) * _GELU_C * (1.0 + 3.0 * 0.044715 * x2)
    return val, grad


def _acc(ref, val, first):
    @pl.when(first)
    def _():
        ref[...] = val

    @pl.when(jnp.logical_not(first))
    def _():
        ref[...] += val


def _rows_call(name, body, tr, row_ins, full_ins, row_outs, acc_outs):
    lp = row_ins[0].shape[0]
    in_specs = [pl.BlockSpec((tr, a.shape[1]), lambda i: (i, 0)) for a in row_ins]
    in_specs += [pl.BlockSpec(a.shape, lambda i, n=a.ndim: (0,) * n) for a in full_ins]
    out_specs = [pl.BlockSpec((tr, c), lambda i: (i, 0)) for c, _ in row_outs]
    out_specs += [pl.BlockSpec(s, lambda i, n=len(s): (0,) * n) for s, _ in acc_outs]
    out_shape = [jax.ShapeDtypeStruct((lp, c), dt) for c, dt in row_outs]
    out_shape += [jax.ShapeDtypeStruct(s, dt) for s, dt in acc_outs]
    return pl.pallas_call(
        body, name=name, grid=(lp // tr,), in_specs=in_specs, out_specs=out_specs, out_shape=out_shape,
        compiler_params=_cparams("arbitrary"))(*row_ins, *full_ins)


def _inproj(h, pre_w, w_in_b, tr):
    def body(h_ref, pw_ref, w_ref, ql, kvl, ag, su, sg, kr):
        x = h_ref[...]
        xn = x * _rms(x) * pw_ref[...]
        pr = _mm(xn, w_ref[...])
        ql[...] = pr[:, 0:256]
        kvl[...] = pr[:, 256:384]
        ag[...] = pr[:, 384:896]
        su[...] = pr[:, 896:1408]
        sg[...] = pr[:, 1408:1920]
        kr[...] = pr[:, 1920:1952]

    return _rows_call("inproj", body, tr, [h], [pre_w, w_in_b],
                      [(256, F32), (128, F32), (512, F32), (512, F32), (512, F32), (32, F32)], [])


def _qkv_up(ql, kvl, kr, cos8, sin8, c32, s32, qw, kvw, wq_b, wkv_b, p32, tr):
    def body(ql_ref, kvl_ref, kr_ref, cos_ref, sin_ref, c32_ref, s32_ref, qw_ref, kvw_ref, wq_ref, wkv_ref, p_ref,
             qn_o, qr1_o, qr2_o, kn_o, v_o, kr_o):
        x = ql_ref[...]
        q = _mm(x * _rms(x) * qw_ref[...], wq_ref[...]) * Q_SCALE
        r1, r2 = q[:, 512:640], q[:, 640:768]
        cs, sn = cos_ref[...], sin_ref[...]
        qn_o[...] = q[:, 0:512].astype(BF16)
        qr1_o[...] = (r1 * cs - r2 * sn).astype(BF16)
        qr2_o[...] = (r2 * cs + r1 * sn).astype(BF16)
        x = kvl_ref[...]
        kv = _mm(x * _rms(x) * kvw_ref[...], wkv_ref[...])
        kn_o[...] = kv[:, 0:512].astype(BF16)
        v_o[...] = kv[:, 512:1024].astype(BF16)
        x = kr_ref[...]
        kr_o[...] = (x * c32_ref[...] + _mm_exact(x, p_ref[...]) * s32_ref[...]).astype(BF16)

    return _rows_call("qkv_up", body, tr, [ql, kvl, kr, cos8, sin8, c32, s32], [qw, kvw, wq_b, wkv_b, p32],
                      [(512, BF16), (128, BF16), (128, BF16), (512, BF16), (512, BF16), (32, BF16)], [])


def _row_position(row, lseg):
    return (row & (N_SEG - 1)) * lseg + (row >> 3)


def _first_padded_tile(n_valid, lp, tile):
    lseg = lp // N_SEG
    t0 = n_valid - (N_SEG - 1) * lseg
    return (t0 * N_SEG + N_SEG - 1) // tile if n_valid < lp else lp // tile


def _attn_fwd(qt, k, vxt, n_valid):
    _, nq, _, tq = qt.shape
    _, nk, _, tk = vxt.shape
    lp = k.shape[1]
    lseg = lp // N_SEG
    n_plain = max(0, min(nk, _first_padded_tile(n_valid, lp, tk)))

    def body(q_ref, k_ref, v_ref, o_ref, lse_ref, m_s, acc_s):
        m_s[...] = jnp.full(m_s.shape, -1e30, F32)
        acc_s[...] = jnp.zeros(acc_s.shape, F32)
        qq = q_ref[0, 0]

        def chunk(c, padded):
            r0 = pl.multiple_of(c * tk, tk)
            st = _mm(k_ref[0, pl.ds(r0, tk), :], qq)
            if padded:
                row = r0 + lax.broadcasted_iota(jnp.int32, (tk, tq), 0)
                st = jnp.where(_row_position(row, lseg) < n_valid, st, -1e30)
            m_old = m_s[...]
            m_new = jnp.maximum(m_old, jnp.max(st, axis=0, keepdims=True))
            pt = jnp.exp2(st - m_new)
            acc_s[...] = jnp.exp2(m_old - m_new) * acc_s[...] + _mm(v_ref[0, c], pt)
            m_s[...] = m_new

        def plain(c, carry):
            chunk(c, False)
            return carry

        n_loop = n_plain - n_plain % ATTN_UNROLL
        if n_loop:
            lax.fori_loop(0, n_loop, plain, 0, unroll=ATTN_UNROLL)
        for c in range(n_loop, nk):
            chunk(c, c >= n_plain)
        acc = acc_s[...]
        l = acc[V_HEAD:V_HEAD + 1, :]
        o_ref[0] = acc[:V_HEAD, :] / l
        lse_ref[0, 0] = m_s[...] + jnp.log2(l)

    return pl.pallas_call(
        body, name="attn_fwd", grid=(HEADS, nq),
        in_specs=[pl.BlockSpec((1, 1, QK_DIM, tq), lambda h, i: (h, i, 0, 0)),
                  pl.BlockSpec((1, lp, QK_DIM), lambda h, i: (h, 0, 0)),
                  pl.BlockSpec((1, nk, LANES, tk), lambda h, i: (h, 0, 0, 0))],
        out_specs=[pl.BlockSpec((1, V_HEAD, tq), lambda h, i: (h, 0, i)),
                   pl.BlockSpec((1, 1, 1, tq), lambda h, i: (h, i, 0, 0))],
        out_shape=[jax.ShapeDtypeStruct((HEADS, V_HEAD, lp), F32), jax.ShapeDtypeStruct((HEADS, nq, 1, tq), F32)],
        scratch_shapes=[pltpu.VMEM((1, tq), F32), pltpu.VMEM((LANES, tq), F32)],
        compiler_params=_cparams("arbitrary", "arbitrary"))(qt, k, vxt)


def _attn_post(o_flat, ag, aw, tr):
    def body(o_ref, g_ref, w_ref, ya):
        t = o_ref[...] * _silu_and_grad(g_ref[...])[0]
        ya[...] = t * _rms(t) * w_ref[...]

    return _rows_call("attn_post", body, tr, [o_flat, ag], [aw], [(512, F32)], [])[0]


def _scan_tiles(lp):
    lseg = lp // N_SEG
    tt = _pick(lseg, [208, 48, 32, 16, 8, 4, 2, 1])
    return lseg, tt, lseg // tt


def _cmul(ar, ai, br, bi):
    return ar * br - ai * bi, ar * bi + ai * br


N_COL_BLK = N_STATES // COL_BLK
CH_BLK = D_SSM // N_COL_BLK


def _scan_steps(tt, forward, bre_ref, bim_ref, ar, ai, carry, visit):
    def step(s, c):
        r0 = pl.multiple_of((s if forward else tt - 1 - s) * N_SEG, N_SEG)
        pr, pi = _cmul(ar, ai, c[0], c[1])
        xr = pr + bre_ref[pl.ds(r0, N_SEG), :]
        xi = pi + bim_ref[pl.ds(r0, N_SEG), :]
        return (xr, xi) + tuple(visit(r0, (xr, xi), (c[0], c[1]), c[2:]))

    return lax.fori_loop(0, tt, step, carry, unroll=4 if tt % 4 == 0 else 1)


def _segment_starts(lseg, forward, ar, ai, ere_ref, eim_ref, s_re, s_im):
    a1r, a1i = ar[0:1, :], ai[0:1, :]
    pr, pi = jnp.ones_like(a1r), jnp.zeros_like(a1i)
    br, bi = a1r, a1i
    n = lseg
    while n:
        if n & 1:
            pr, pi = _cmul(pr, pi, br, bi)
        n >>= 1
        if n:
            br, bi = _cmul(br, bi, br, bi)
    cr, ci = jnp.zeros_like(a1r), jnp.zeros_like(a1i)
    for j in (range(N_SEG) if forward else range(N_SEG - 1, -1, -1)):
        s_re[j:j + 1, :] = cr
        s_im[j:j + 1, :] = ci
        nr, ni = _cmul(pr, pi, cr, ci)
        cr = nr + ere_ref[j:j + 1, :]
        ci = ni + eim_ref[j:j + 1, :]


def _scan_specs(lp, forward):
    lseg, tt, nt = _scan_tiles(lp)

    def tile(t):
        return t if forward else nt - 1 - t

    rows = lambda w: pl.BlockSpec((tt * N_SEG, w), lambda cb, t: (tile(t), cb))
    proj = pl.BlockSpec((1, CH_BLK, COL_BLK), lambda cb, t: (cb, 0, 0))
    slab = pl.BlockSpec((N_SEG, COL_BLK), lambda cb, t: (0, cb))
    return lseg, tt, nt, rows, proj, slab


def _scan_ends(name, urows, wre4, wim4, ar8, ai8, forward):
    lp = urows.shape[0]
    lseg, tt, nt, rows, proj, slab = _scan_specs(lp, forward)

    def body(u_ref, wre_ref, wim_ref, ar_ref, ai_ref, ere_o, eim_o, bre_s, bim_s, cr_s, ci_s):
        t = pl.program_id(1)

        @pl.when(t == 0)
        def _():
            cr_s[...] = jnp.zeros(cr_s.shape, F32)
            ci_s[...] = jnp.zeros(ci_s.shape, F32)

        u = u_ref[...]
        bre_s[...] = _mm(u, wre_ref[0])
        bim_s[...] = _mm(u, wim_ref[0])
        cr, ci = _scan_steps(tt, forward, bre_s, bim_s, ar_ref[...], ai_ref[...], (cr_s[...], ci_s[...]),
                             lambda r0, x, x_prev, extra: ())
        cr_s[...] = cr
        ci_s[...] = ci

        @pl.when(t == nt - 1)
        def _():
            ere_o[...] = cr
            eim_o[...] = ci

    return pl.pallas_call(
        body, name=name, grid=(N_COL_BLK, nt), in_specs=[rows(CH_BLK), proj, proj, slab, slab],
        out_specs=[slab, slab], out_shape=[jax.ShapeDtypeStruct((N_SEG, N_STATES), F32)] * 2,
        scratch_shapes=[pltpu.VMEM((tt * N_SEG, COL_BLK), F32)] * 2 + [pltpu.VMEM((N_SEG, COL_BLK), F32)] * 2,
        compiler_params=_cparams("arbitrary", "arbitrary"))(urows, wre4, wim4, ar8, ai8)


def _scan_fwd(name, urows, wre4, wim4, ar8, ai8, ere, eim, cre4, cim4, forward):
    lp = urows.shape[0]
    lseg, tt, nt, rows, proj, slab = _scan_specs(lp, forward)

    def body(u_ref, wre_ref, wim_ref, ar_ref, ai_ref, ere_ref, eim_ref, cre_ref, cim_ref,
             xre_o, xim_o, y_o, bre_s, bim_s, cr_s, ci_s):
        ar, ai = ar_ref[...], ai_ref[...]

        @pl.when(pl.program_id(1) == 0)
        def _():
            _segment_starts(lseg, forward, ar, ai, ere_ref, eim_ref, cr_s, ci_s)

        u = u_ref[...]
        bre_s[...] = _mm(u, wre_ref[0])
        bim_s[...] = _mm(u, wim_ref[0])

        def visit(r0, x, x_prev, extra):
            xre_o[pl.ds(r0, N_SEG), :] = x[0]
            xim_o[pl.ds(r0, N_SEG), :] = x[1]
            return ()

        cr, ci = _scan_steps(tt, forward, bre_s, bim_s, ar, ai, (cr_s[...], ci_s[...]), visit)
        cr_s[...] = cr
        ci_s[...] = ci
        y_o[...] = _mm_nt(xre_o[...], cre_ref[0]) + _mm_nt(xim_o[...], cim_ref[0])

    return pl.pallas_call(
        body, name=name, grid=(N_COL_BLK, nt),
        in_specs=[rows(CH_BLK), proj, proj, slab, slab, slab, slab, proj, proj],
        out_specs=[rows(COL_BLK), rows(COL_BLK), rows(CH_BLK)],
        out_shape=[jax.ShapeDtypeStruct((lp, N_STATES), F32)] * 2 + [jax.ShapeDtypeStruct((lp, D_SSM), F32)],
        scratch_shapes=[pltpu.VMEM((tt * N_SEG, COL_BLK), F32)] * 2 + [pltpu.VMEM((N_SEG, COL_BLK), F32)] * 2,
        compiler_params=_cparams("arbitrary", "arbitrary"))(urows, wre4, wim4, ar8, ai8, ere, eim, cre4, cim4)


def _scan_bwd(name, dyrows, cre4, cim4, ar8, ai8, ere, eim, urows, wre4, wim4, xre, xim, forward):
    lp = urows.shape[0]
    lseg, tt, nt, rows, proj, slab = _scan_specs(lp, forward)

    def body(dy_ref, cre_ref, cim_ref, ar_ref, ai_ref, ere_ref, eim_ref, u_ref, wre_ref, wim_ref, xre_ref, xim_ref,
             du_o, dwre_o, dwim_o, dcre_o, dcim_o, dare_o, daim_o, bre_s, bim_s, gre_s, gim_s, cr_s, ci_s):
        t = pl.program_id(1)
        ar, ai = ar_ref[...], ai_ref[...]

        @pl.when(t == 0)
        def _():
            _segment_starts(lseg, forward, ar, ai, ere_ref, eim_ref, cr_s, ci_s)
            dare_o[...] = jnp.zeros(dare_o.shape, F32)
            daim_o[...] = jnp.zeros(daim_o.shape, F32)

        dy = dy_ref[...]
        bre_s[...] = _mm(dy, cre_ref[0])
        bim_s[...] = _mm(dy, cim_ref[0])

        def visit(r0, g, g_prev, sums):
            gre_s[pl.ds(r0, N_SEG), :] = g[0]
            gim_s[pl.ds(r0, N_SEG), :] = g[1]
            fr = xre_ref[pl.ds(r0, N_SEG), :]
            fi = xim_ref[pl.ds(r0, N_SEG), :]
            pr, pi = g_prev
            return sums[0] + fr * pr + fi * pi, sums[1] + fr * pi - fi * pr

        out = _scan_steps(tt, forward, bre_s, bim_s, ar, ai, (cr_s[...], ci_s[...], dare_o[...], daim_o[...]), visit)
        cr_s[...] = out[0]
        ci_s[...] = out[1]
        dare_o[...] = out[2]
        daim_o[...] = out[3]
        gre, gim = gre_s[...], gim_s[...]
        du_o[...] = _mm_nt(gre, wre_ref[0]) + _mm_nt(gim, wim_ref[0])
        u = u_ref[...]
        first = t == 0
        _acc(dwre_o, _mm_tn(u, gre)[None], first)
        _acc(dwim_o, _mm_tn(u, gim)[None], first)
        _acc(dcre_o, _mm_tn(dy, xre_ref[...])[None], first)
        _acc(dcim_o, _mm_tn(dy, xim_ref[...])[None], first)

    big = pltpu.VMEM((tt * N_SEG, COL_BLK), F32)
    small = pltpu.VMEM((N_SEG, COL_BLK), F32)
    return pl.pallas_call(
        body, name=name, grid=(N_COL_BLK, nt),
        in_specs=[rows(CH_BLK), proj, proj, slab, slab, slab, slab, rows(CH_BLK), proj, proj,
                  rows(COL_BLK), rows(COL_BLK)],
        out_specs=[rows(CH_BLK), proj, proj, proj, proj, slab, slab],
        out_shape=[jax.ShapeDtypeStruct((lp, D_SSM), F32)]
        + [jax.ShapeDtypeStruct((N_COL_BLK, CH_BLK, COL_BLK), F32)] * 4
        + [jax.ShapeDtypeStruct((N_SEG, N_STATES), F32)] * 2,
        scratch_shapes=[big, big, big, big, small, small],
        compiler_params=_cparams("arbitrary", "arbitrary"))(
            dyrows, cre4, cim4, ar8, ai8, ere, eim, urows, wre4, wim4, xre, xim)


def _ssm_post(yf, yb, u, sg, wglu_b, bglu, sw, dvec, tr):
    def body(yf_ref, yb_ref, u_ref, g_ref, w_ref, b_ref, sw_ref, d_ref, ypre_o, glu_o, ysn_o):
        ypre = yf_ref[...] + yb_ref[...] + d_ref[...] * u_ref[...]
        ypre_o[...] = ypre
        glu = _mm(_gelu_and_grad(ypre)[0], w_ref[...]) + b_ref[...]
        glu_o[...] = glu
        t = glu[:, :D_SSM] * _sigmoid(glu[:, D_SSM:]) * _silu_and_grad(g_ref[...])[0]
        ysn_o[...] = t * _rms(t) * sw_ref[...]

    return _rows_call("ssm_post", body, tr, [yf, yb, u, sg], [wglu_b, bglu, sw, dvec],
                      [(512, F32), (1024, F32), (512, F32)], [])


def _out_loss(ya, ysn, h, tgt, wo_b, post_w, n_valid, tr):
    lseg = h.shape[0] // N_SEG

    def body(ya_ref, ys_ref, h_ref, t_ref, w_ref, pw_ref, dy_o, dout_o, loss_o, dpw_o):
        i = pl.program_id(0)
        y = _mm(ya_ref[...], w_ref[0:D_ATTN, :]) + _mm(ys_ref[...], w_ref[D_ATTN:, :])
        r = _rms(y)
        pw = pw_ref[...]
        out = h_ref[...] + y * r * pw
        pos = _row_position(i * tr + lax.broadcasted_iota(jnp.int32, (tr, 1), 0), lseg)
        valid = jnp.logical_and(pos >= N_META, pos < n_valid)
        diff = jnp.where(valid, out - t_ref[...], 0.0)
        dout = diff * (1.0 / D_MODEL)
        dy, dpw = _rms_bwd(dout, y, r, pw)
        dy_o[...] = dy
        dout_o[...] = dout
        _acc(loss_o, 0.5 * jnp.sum(jnp.sum(diff * diff, axis=1, keepdims=True), axis=0, keepdims=True)
             * (1.0 / D_MODEL), i == 0)
        _acc(dpw_o, dpw, i == 0)

    return _rows_call("out_loss", body, tr, [ya, ysn, h, tgt], [wo_b, post_w], [(1024, F32), (1024, F32)],
                      [((1, 1), F32), ((1, D_MODEL), F32)])


def _out_bwd(dy, ya, ysn, o_flat, ag, wo_b, aw, head_sum, tr):
    def body(dy_ref, ya_ref, ys_ref, o_ref, g_ref, w_ref, aw_ref, hs_ref, do_o, dag_o, dysn_o, dl_o, dwo_o, daw_o):
        i = pl.program_id(0)
        dy = dy_ref[...]
        dcat = _mm_nt(dy, w_ref[...])
        cat = jnp.concatenate([ya_ref[...], ys_ref[...]], axis=1)
        _acc(dwo_o, _mm_tn(cat, dy), i == 0)
        dysn_o[...] = dcat[:, D_ATTN:]
        o = o_ref[...]
        sl, dsl = _silu_and_grad(g_ref[...])
        t = o * sl
        dt, daw = _rms_bwd(dcat[:, :D_ATTN], t, _rms(t), aw_ref[...])
        _acc(daw_o, daw, i == 0)
        do = dt * sl
        do_o[...] = do
        dag_o[...] = dt * o * dsl
        dl_o[...] = _mm_exact(do * o, hs_ref[...])

    return _rows_call("out_bwd", body, tr, [dy, ya, ysn, o_flat, ag], [wo_b, aw, head_sum],
                      [(512, F32), (512, F32), (512, F32), (HEADS, F32)],
                      [((D_MODEL, D_MODEL), F32), ((1, D_ATTN), F32)])


def _ssm_post_bwd(dysn, glu, sg, ypre, u, wglu_b, sw, dvec, tr):
    def body(d_ref, glu_ref, sg_ref, y_ref, u_ref, w_ref, sw_ref, dv_ref,
             dyp_o, dsg_o, dwg_o, dbg_o, dsw_o, dd_o):
        i = pl.program_id(0)
        glu = glu_ref[...]
        a, b = glu[:, :D_SSM], glu[:, D_SSM:]
        sb = _sigmoid(b)
        ys = a * sb
        sl, dsl = _silu_and_grad(sg_ref[...])
        t = ys * sl
        dt, dsw = _rms_bwd(d_ref[...], t, _rms(t), sw_ref[...])
        _acc(dsw_o, dsw, i == 0)
        dsg_o[...] = dt * ys * dsl
        dys = dt * sl
        dglu = jnp.concatenate([dys * sb, dys * a * sb * (1.0 - sb)], axis=1)
        _acc(dbg_o, jnp.sum(dglu, axis=0, keepdims=True), i == 0)
        gel, dgel = _gelu_and_grad(y_ref[...])
        _acc(dwg_o, _mm_tn(gel, dglu), i == 0)
        dyp = _mm_nt(dglu, w_ref[...]) * dgel
        dyp_o[...] = dyp
        _acc(dd_o, jnp.sum(dyp * u_ref[...], axis=0, keepdims=True), i == 0)

    return _rows_call("ssm_post_bwd", body, tr, [dysn, glu, sg, ypre, u], [wglu_b, sw, dvec],
                      [(512, F32), (512, F32)],
                      [((D_SSM, 2 * D_SSM), F32), ((1, 2 * D_SSM), F32), ((1, D_SSM), F32), ((1, D_SSM), F32)])


def _attn_bwd(qt, k, kt, v, dot, lse_t, delta_t, tk):
    _, nq, _, tq = qt.shape
    lp = k.shape[1]
    nk = lp // tk
    assert lse_t.shape == (HEADS, nq, 1, tq) and delta_t.shape == (HEADS, nq, 1, tq)

    def body(q_ref, k_ref, kt_ref, v_ref, do_ref, lse_ref, dl_ref, dq_o, dk_o, dv_o, dk_s, dv_s):
        @pl.when(pl.program_id(1) == 0)
        def _():
            dq_o[...] = jnp.zeros(dq_o.shape, F32)

        dk_s[...] = jnp.zeros(dk_s.shape, F32)
        dv_s[...] = jnp.zeros(dv_s.shape, F32)
        kk = k_ref[0]
        kkt = kt_ref[0]
        vv = v_ref[0]

        def chunk(c, carry):
            qq = q_ref[0, c]
            dd = do_ref[0, c]
            pt = jnp.exp2(_mm(kk, qq) - lse_ref[0, c])
            dv_s[...] += _mm_nt(dd, pt)
            dst = (pt * (_mm(vv, dd) - dl_ref[0, c])).astype(BF16)
            dk_s[...] += _mm_nt(qq, dst)
            dq_o[0, c] += _mm(kkt, dst)
            return carry

        n_loop = nq - nq % ATTN_BWD_UNROLL
        if n_loop:
            lax.fori_loop(0, n_loop, chunk, 0, unroll=ATTN_BWD_UNROLL)
        for c in range(n_loop, nq):
            chunk(c, 0)
        dk_o[0] = dk_s[...]
        dv_o[0] = dv_s[...]

    head = lambda w: pl.BlockSpec((1, nq, w, tq), lambda h, j: (h, 0, 0, 0))
    rows = lambda w: pl.BlockSpec((1, tk, w), lambda h, j: (h, j, 0))
    cols = lambda w: pl.BlockSpec((1, w, tk), lambda h, j: (h, 0, j))
    return pl.pallas_call(
        body, name="attn_bwd", grid=(HEADS, nk),
        in_specs=[head(QK_DIM), rows(QK_DIM), cols(QK_DIM), rows(V_HEAD), head(V_HEAD), head(1), head(1)],
        out_specs=[head(QK_DIM), cols(QK_DIM), cols(V_HEAD)],
        out_shape=[jax.ShapeDtypeStruct((HEADS, nq, QK_DIM, tq), F32), jax.ShapeDtypeStruct((HEADS, QK_DIM, lp), F32),
                   jax.ShapeDtypeStruct((HEADS, V_HEAD, lp), F32)],
        scratch_shapes=[pltpu.VMEM((QK_DIM, tk), F32), pltpu.VMEM((V_HEAD, tk), F32)],
        compiler_params=_cparams("arbitrary", "arbitrary"))(qt, k, kt, v, dot, lse_t, delta_t)


def _qkv_up_bwd(dqn, dr1, dr2, dkn, dv, dkr8, ql, kvl, cos8, sin8, c32, s32, qw, kvw, wq_b, wkv_b, p32, sum8, tr):
    def body(dqn_ref, dr1_ref, dr2_ref, dkn_ref, dv_ref, dkr_ref, ql_ref, kvl_ref, cos_ref, sin_ref, c32_ref,
             s32_ref, qw_ref, kvw_ref, wq_ref, wkv_ref, p_ref, s8_ref,
             dql_o, dkvl_o, dkrr_o, dwq_o, dwkv_o, dqw_o, dkvw_o):
        i = pl.program_id(0)
        cs, sn = cos_ref[...], sin_ref[...]
        d1, d2 = dr1_ref[...], dr2_ref[...]
        dq = jnp.concatenate([dqn_ref[...], d1 * cs + d2 * sn, d2 * cs - d1 * sn], axis=1) * (Q_SCALE / LOG2E)
        x = ql_ref[...]
        r = _rms(x)
        qw = qw_ref[...]
        _acc(dwq_o, _mm_tn(x * r * qw, dq), i == 0)
        dx, dw = _rms_bwd(_mm_nt(dq, wq_ref[...]), x, r, qw)
        dql_o[...] = dx
        _acc(dqw_o, dw, i == 0)
        dkv = jnp.concatenate([dkn_ref[...] * (1.0 / LOG2E), dv_ref[...]], axis=1)
        x = kvl_ref[...]
        r = _rms(x)
        kvw = kvw_ref[...]
        _acc(dwkv_o, _mm_tn(x * r * kvw, dkv), i == 0)
        dx, dw = _rms_bwd(_mm_nt(dkv, wkv_ref[...]), x, r, kvw)
        dkvl_o[...] = dx
        _acc(dkvw_o, dw, i == 0)
        dkr = _mm_exact(dkr_ref[...], s8_ref[...]) * (1.0 / LOG2E)
        dkrr_o[...] = dkr * c32_ref[...] + _mm_exact(dkr * s32_ref[...], p_ref[...])

    return _rows_call("qkv_up_bwd", body, tr, [dqn, dr1, dr2, dkn, dv, dkr8, ql, kvl, cos8, sin8, c32, s32],
                      [qw, kvw, wq_b, wkv_b, p32, sum8], [(256, F32), (128, F32), (32, F32)],
                      [((Q_LORA, 768), F32), ((KV_LORA, 1024), F32), ((1, Q_LORA), F32), ((1, KV_LORA), F32)])


def _inproj_bwd(dql, dkvl, dag, du_f, du_b, dypre, dsg, dkr, h, dout, pre_w, w_in_b, dvec, tr):
    def body(dql_ref, dkvl_ref, dag_ref, duf_ref, dub_ref, dyp_ref, dsg_ref, dkr_ref, h_ref, dout_ref,
             pw_ref, w_ref, dv_ref, dh_o, dwin_o, dpw_o):
        i = pl.program_id(0)
        du = duf_ref[...] + dub_ref[...] + dv_ref[...] * dyp_ref[...]
        dproj = jnp.concatenate([dql_ref[...], dkvl_ref[...], dag_ref[...], du, dsg_ref[...],
                                 dkr_ref[...], jnp.zeros((tr, D_IN_PAD - D_IN), F32)], axis=1)
        x = h_ref[...]
        r = _rms(x)
        pw = pw_ref[...]
        _acc(dwin_o, _mm_tn(x * r * pw, dproj), i == 0)
        dx, dw = _rms_bwd(_mm_nt(dproj, w_ref[...]), x, r, pw)
        _acc(dpw_o, dw, i == 0)
        dh_o[...] = dout_ref[...] + dx

    return _rows_call("inproj_bwd", body, tr, [dql, dkvl, dag, du_f, du_b, dypre, dsg, dkr, h, dout],
                      [pre_w, w_in_b, dvec], [(1024, F32)], [((D_MODEL, D_IN_PAD), F32), ((1, D_MODEL), F32)])


def _disc_terms(a_re, a_im, ldt):
    dt = jnp.exp(ldt)
    mag = jnp.exp(a_re * dt)
    th = a_im * dt
    cs, sn = jnp.cos(th), jnp.sin(th)
    abar_re, abar_im = mag * cs, mag * sn
    num_re, num_im = abar_re - 1.0, abar_im
    den = a_re * a_re + a_im * a_im
    coef_re = (num_re * a_re + num_im * a_im) / den
    coef_im = (num_im * a_re - num_re * a_im) / den
    return dt, mag, cs, sn, abar_re, abar_im, num_re, num_im, den, coef_re, coef_im


def _ssm_disc(a_re, a_im, ldt, bt_re, bt_im):
    def body(ar_ref, ai_ref, l_ref, br_ref, bi_ref, abr_o, abi_o, bbr_o, bbi_o):
        t = _disc_terms(ar_ref[...], ai_ref[...], l_ref[...])
        abr_o[...] = t[4]
        abi_o[...] = t[5]
        cr, ci = t[9], t[10]
        br, bi = br_ref[...], bi_ref[...]
        bbr_o[...] = cr * br - ci * bi
        bbi_o[...] = cr * bi + ci * br

    ng = a_re.shape[0]
    return pl.pallas_call(
        body, name="ssm_disc",
        out_shape=[jax.ShapeDtypeStruct((ng, 1, SSM_STATE), F32)] * 2
        + [jax.ShapeDtypeStruct((ng, SSM_GROUP, SSM_STATE), F32)] * 2)(a_re, a_im, ldt, bt_re, bt_im)


def _ssm_disc_bwd(a_re, a_im, ldt, bt_re, bt_im, da8_re, da8_im, dbb_re, dbb_im):
    def body(ar_ref, ai_ref, l_ref, br_ref, bi_ref, dar_ref, dai_ref, dbr_ref, dbi_ref,
             gar_o, gai_o, gl_o, gbr_o, gbi_o):
        a_re, a_im = ar_ref[...], ai_ref[...]
        dt, mag, cs, sn, abar_re, abar_im, num_re, num_im, den, cr, ci = _disc_terms(a_re, a_im, l_ref[...])
        br, bi = br_ref[...], bi_ref[...]
        dbr, dbi = dbr_ref[...], dbi_ref[...]
        gbr_o[...] = cr * dbr + ci * dbi
        gbi_o[...] = cr * dbi - ci * dbr
        dcr = jnp.sum(br * dbr + bi * dbi, axis=1, keepdims=True)
        dci = jnp.sum(br * dbi - bi * dbr, axis=1, keepdims=True)
        dnum_re = (dcr * a_re - dci * a_im) / den
        dnum_im = (dcr * a_im + dci * a_re) / den
        dden = -(dcr * cr + dci * ci) / den
        g_are = (dcr * num_re + dci * num_im) / den + dden * 2.0 * a_re
        g_aim = (dcr * num_im - dci * num_re) / den + dden * 2.0 * a_im
        d_abr = jnp.sum(dar_ref[...], axis=1, keepdims=True) + dnum_re
        d_abi = jnp.sum(dai_ref[...], axis=1, keepdims=True) + dnum_im
        dmag = d_abr * cs + d_abi * sn
        dth = d_abi * abar_re - d_abr * abar_im
        g_are = g_are + dmag * mag * dt
        g_aim = g_aim + dth * dt
        ddt = jnp.sum(dmag * mag * a_re + dth * a_im, axis=2, keepdims=True)
        gar_o[...] = g_are
        gai_o[...] = g_aim
        gl_o[...] = ddt * dt

    ng = a_re.shape[0]
    return pl.pallas_call(
        body, name="ssm_disc_bwd",
        out_shape=[jax.ShapeDtypeStruct((ng, 1, SSM_STATE), F32)] * 2 + [jax.ShapeDtypeStruct((ng, 1, 1), F32)]
        + [jax.ShapeDtypeStruct((ng, SSM_GROUP, SSM_STATE), F32)] * 2)(
            a_re, a_im, ldt, bt_re, bt_im, da8_re, da8_im, dbb_re, dbb_im)


def _exchange(name, per_peer, shared):
    parts = [a for a in (per_peer, shared) if a is not None]
    rp = per_peer.shape[1] if per_peer is not None else 0
    rs = shared.shape[0] if shared is not None else 0
    n = len(parts)

    def body(*refs):
        in_refs, out_ref, send_sems, recv_sems, local_sems = refs[:n], refs[n], refs[n + 1], refs[n + 2], refs[n + 3]
        x, y, c = lax.axis_index("x"), lax.axis_index("y"), lax.axis_index("c")
        me = 4 * x + 2 * y + c

        def pieces(peer):
            out = []
            if per_peer is not None:
                out.append((in_refs[0].at[peer], out_ref.at[me, pl.ds(0, rp), :]))
            if shared is not None:
                out.append((in_refs[-1], out_ref.at[me, pl.ds(rp, rs), :]))
            return out

        copies = []
        for k in range(1, N_DEV):
            px = 1 - x if (k >> 2) & 1 else x
            py = 1 - y if (k >> 1) & 1 else y
            pc = 1 - c if k & 1 else c
            for j, (src, dst) in enumerate(pieces(4 * px + 2 * py + pc)):
                s = (k - 1) * n + j
                copies.append(pltpu.make_async_remote_copy(
                    src_ref=src, dst_ref=dst, send_sem=send_sems.at[s], recv_sem=recv_sems.at[s],
                    device_id=(px, py, pc), device_id_type=pl.DeviceIdType.MESH))
        mine = [pltpu.make_async_copy(src, dst, local_sems.at[j]) for j, (src, dst) in enumerate(pieces(me))]
        for cp in mine + copies:
            cp.start()
        for cp in copies + mine:
            cp.wait()

    n_sem = (N_DEV - 1) * n
    return pl.pallas_call(
        body, name=name, out_shape=jax.ShapeDtypeStruct((N_DEV, rp + rs, LANES), F32),
        in_specs=[pl.BlockSpec(memory_space=pl.ANY)] * n, out_specs=pl.BlockSpec(memory_space=pl.ANY),
        scratch_shapes=[pltpu.SemaphoreType.DMA((n_sem,)), pltpu.SemaphoreType.DMA((n_sem,)),
                        pltpu.SemaphoreType.DMA((n,))])(*parts)


def _adamw(recv, w, m, v, tr):
    rows = w.shape[0]
    c1 = 1.0 - ADAM_B1 ** ADAM_STEP
    c2 = 1.0 - ADAM_B2 ** ADAM_STEP

    def body(r_ref, w_ref, m_ref, v_ref, g_o, d_o, m_o, v_o):
        g = r_ref[0]
        for k in range(1, N_DEV):
            g = g + r_ref[k]
        mm = ADAM_B1 * m_ref[...] + (1.0 - ADAM_B1) * g
        vv = ADAM_B2 * v_ref[...] + (1.0 - ADAM_B2) * (g * g)
        g_o[...] = g
        m_o[...] = mm
        v_o[...] = vv
        d_o[...] = -ADAM_LR * ((mm / c1) / (jnp.sqrt(vv / c2) + ADAM_EPS) + ADAM_WD * w_ref[...])

    spec = pl.BlockSpec((tr, LANES), lambda i: (i, 0))
    return pl.pallas_call(
        body, name="adamw", grid=(rows // tr,),
        in_specs=[pl.BlockSpec((N_DEV, tr, LANES), lambda i: (0, i, 0)), spec, spec, spec],
        out_specs=[spec] * 4, out_shape=[jax.ShapeDtypeStruct((rows, LANES), F32)] * 4,
        compiler_params=_cparams("arbitrary"))(recv, w, m, v)


def _to_rows(a):
    flat = a.reshape(-1)
    pad = (-flat.shape[0]) % LANES
    if pad:
        flat = jnp.concatenate([flat, jnp.zeros((pad,), flat.dtype)])
    return flat.reshape(-1, LANES)


def _n_rows(shape):
    return -(-int(np.prod(shape)) // LANES)


def _pack(arrays, total_rows):
    rows = [_to_rows(a) for a in arrays]
    used = sum(r.shape[0] for r in rows)
    if total_rows > used:
        rows.append(jnp.zeros((total_rows - used, LANES), F32))
    return jnp.concatenate(rows, axis=0)


def _unpack(buf, shapes):
    lead = buf.shape[:-2]
    out, r0 = [], 0
    for s in shapes:
        n = int(np.prod(s))
        nr = _n_rows(s)
        out.append(buf[..., r0:r0 + nr, :].reshape(lead + (-1,))[..., :n].reshape(lead + tuple(s)))
        r0 += nr
    return out


def _pack_per_device(arrays, total_rows):
    rows = []
    for a in arrays:
        flat = a.reshape(N_DEV, -1)
        pad = (-flat.shape[1]) % LANES
        if pad:
            flat = jnp.concatenate([flat, jnp.zeros((N_DEV, pad), flat.dtype)], axis=1)
        rows.append(flat.reshape(N_DEV, -1, LANES))
    used = sum(r.shape[1] for r in rows)
    if total_rows > used:
        rows.append(jnp.zeros((N_DEV, total_rows - used, LANES), F32))
    return jnp.concatenate(rows, axis=1)


def _shard_views(name, full):
    if name == 'w_out':
        return full.reshape(N_DEV, full.shape[0] // N_DEV, full.shape[1])
    r, ccols = full.shape
    return full.reshape(r, N_DEV, ccols // N_DEV).transpose(1, 0, 2)


def _from_shards(name, stacked):
    if name == 'w_out':
        return stacked.reshape(-1, stacked.shape[-1])
    n, r, cc = stacked.shape
    return stacked.transpose(1, 0, 2).reshape(r, n * cc)


GROUPS_PER_BLK = N_GROUPS // N_COL_BLK


def _block_diag(t):
    eye = jnp.eye(GROUPS_PER_BLK, dtype=t.dtype)
    t4 = t.reshape(N_COL_BLK, GROUPS_PER_BLK, SSM_GROUP, SSM_STATE)
    return (t4[:, :, :, None, :] * eye[None, :, None, :, None]).reshape(N_COL_BLK, CH_BLK, COL_BLK)


def _diag_blocks(mat4):
    eye = jnp.eye(GROUPS_PER_BLK, dtype=mat4.dtype)
    m6 = mat4.reshape(N_COL_BLK, GROUPS_PER_BLK, SSM_GROUP, GROUPS_PER_BLK, SSM_STATE)
    return (m6 * eye[None, :, None, :, None]).sum(axis=3).reshape(N_GROUPS, SSM_GROUP, SSM_STATE)


def _step(x, loss_target, wts, moms, vels):
    seq = x.shape[1]
    n_valid = N_META + seq
    lp = -(-n_valid // 256) * 256
    tr = _pick(lp, [640, 256])
    tr_mid = 256
    tq = _pick(lp, [1280, 256])
    tk = _pick(lp, [640, 256])

    shard_shapes = [wts[n].shape[-2:] for n in SHARDED]
    n_shard_rows = sum(_n_rows(s) for s in shard_shapes)
    per_word = 4 // jnp.dtype(BF16).itemsize
    word_shapes = [(int(np.prod(s)) // per_word,) if n != 'meta_tokens' else s for n, s in zip(SHARDED, shard_shapes)]

    def as_words(n):
        a = wts[n].reshape(wts[n].shape[-2:])
        if n == 'meta_tokens' or per_word == 1:
            return a
        return lax.bitcast_convert_type(a.astype(BF16).reshape(-1, per_word), F32)

    def from_words(n, a, s):
        if n == 'meta_tokens' or per_word == 1:
            return a.reshape((N_DEV,) + tuple(s))
        return lax.bitcast_convert_type(a, BF16).reshape((N_DEV,) + tuple(s))

    gathered = _exchange("gather_weights", None,
                         _pack([as_words(n) for n in SHARDED], sum(_n_rows(s) for s in word_shapes)))
    full = {n: _from_shards(n, from_words(n, a, s))
            for n, a, s in zip(SHARDED, _unpack(gathered, word_shapes), shard_shapes)}

    w_in_b = jnp.concatenate([_cols_in(full['w_in']).astype(BF16), jnp.zeros((D_MODEL, D_IN_PAD - D_IN), BF16)],
                             axis=1)
    wq_b = _cols_q(full['w_q_up']).astype(BF16)
    wkv_b = _cols_kv(full['w_kv_up']).astype(BF16)
    wglu_b = full['w_glu'].astype(BF16)
    wo_b = full['w_out'].astype(BF16)
    pre_w, post_w = wts['pre_norm_w'], wts['post_norm_w']
    qw, kvw, aw, sw = wts['q_norm_w'], wts['kv_norm_w'], wts['attn_out_norm_w'], wts['ssm_out_norm_w']
    bglu, dvec = wts['b_glu'], wts['ssm_d']

    lseg = lp // N_SEG
    pos = _row_position(jnp.arange(lp, dtype=jnp.int32), lseg)
    inv = ROPE_THETA ** (-jnp.arange(HALF_ROPE, dtype=F32) / HALF_ROPE)
    ang = pos.astype(F32)[:, None] * inv[None, :]
    cos, sin = jnp.cos(ang), jnp.sin(ang)
    cos8, sin8 = jnp.tile(cos, (1, HEADS)), jnp.tile(sin, (1, HEADS))
    c32 = jnp.concatenate([cos, cos], axis=1)
    s32 = jnp.concatenate([-sin, sin], axis=1)
    p32 = jnp.asarray(np.roll(np.eye(QK_ROPE, dtype=np.float32), HALF_ROPE, axis=1))
    sum8 = jnp.asarray(np.tile(np.eye(QK_ROPE, dtype=np.float32), (HEADS, 1)))
    head_sum = jnp.asarray(np.repeat(np.eye(HEADS, dtype=np.float32), V_HEAD, axis=0))

    ng = 2 * N_GROUPS
    a_re3 = wts['ssm_a_re'].reshape(ng, 1, SSM_STATE)
    a_im3 = wts['ssm_a_im'].reshape(ng, 1, SSM_STATE)
    ldt3 = wts['ssm_log_dt'].reshape(ng, 1, 1)
    bt_re = wts['ssm_b_re'].reshape(2, N_GROUPS, SSM_STATE, SSM_GROUP).transpose(0, 1, 3, 2).reshape(
        ng, SSM_GROUP, SSM_STATE)
    bt_im = wts['ssm_b_im'].reshape(2, N_GROUPS, SSM_STATE, SSM_GROUP).transpose(0, 1, 3, 2).reshape(
        ng, SSM_GROUP, SSM_STATE)
    c_re = wts['ssm_c_re'].reshape(ng, SSM_GROUP, SSM_STATE)
    c_im = wts['ssm_c_im'].reshape(ng, SSM_GROUP, SSM_STATE)
    abar_re, abar_im, bbt_re, bbt_im = _ssm_disc(a_re3, a_im3, ldt3, bt_re, bt_im)

    def direction(t, d):
        return t[d * N_GROUPS:(d + 1) * N_GROUPS]

    def slab(t, d, sign=1.0):
        return jnp.broadcast_to(sign * direction(t, d).reshape(1, N_STATES), (N_SEG, N_STATES))

    w_re = [_block_diag(direction(bbt_re, d)).astype(BF16) for d in range(2)]
    w_im = [_block_diag(direction(bbt_im, d)).astype(BF16) for d in range(2)]
    cb_re = [_block_diag(direction(c_re, d)).astype(BF16) for d in range(2)]
    cb_im = [_block_diag(-direction(c_im, d)).astype(BF16) for d in range(2)]

    def to_rows(a):
        return a.reshape(N_SEG, lseg, a.shape[-1]).transpose(1, 0, 2).reshape(lp, a.shape[-1])

    def to_tokens(a):
        return a.reshape(lseg, N_SEG, a.shape[-1]).transpose(1, 0, 2).reshape(lp, a.shape[-1])

    pad = jnp.zeros((lp - n_valid, D_MODEL), F32)
    h = to_rows(jnp.concatenate([full['meta_tokens'], x[0], pad], axis=0))
    tgt = to_rows(jnp.concatenate([jnp.zeros((N_META, D_MODEL), F32), loss_target[0], pad], axis=0))

    ql, kvl, ag, su, sg, kr = _inproj(h, pre_w, w_in_b, tr)
    qn_b, qr1_b, qr2_b, kn_b, v_b, kr_b = _qkv_up(ql, kvl, kr, cos8, sin8, c32, s32, qw, kvw, wq_b, wkv_b, p32, tr)

    def heads(a, w):
        return a.reshape(lp, HEADS, w)

    nq, nk = lp // tq, lp // tk
    q_t = jnp.concatenate([heads(qn_b, 64), heads(qr1_b, 16), heads(qr2_b, 16)], axis=-1)
    k_t = jnp.concatenate([heads(kn_b, 64), jnp.broadcast_to(kr_b[:, None, :], (lp, HEADS, QK_ROPE))], axis=-1)
    v_t = heads(v_b, 64)
    vx_t = jnp.concatenate([v_t, jnp.ones((lp, HEADS, 1), BF16), jnp.zeros((lp, HEADS, LANES - V_HEAD - 1), BF16)],
                           axis=-1)
    qt4 = q_t.reshape(nq, tq, HEADS, QK_DIM).transpose(2, 0, 3, 1)
    tk_fwd = _pick(lp, [1280, 256])
    vxt4 = vx_t.reshape(lp // tk_fwd, tk_fwd, HEADS, LANES).transpose(2, 0, 3, 1)
    k_h = k_t.transpose(1, 0, 2)
    kt_h = k_t.transpose(1, 2, 0)
    v_h = v_t.transpose(1, 0, 2)
    ot_h, lse4 = _attn_fwd(qt4, k_h, vxt4, n_valid)
    o_flat = ot_h.transpose(2, 0, 1).reshape(lp, D_ATTN)
    ya = _attn_post(o_flat, ag, aw, tr)

    xs, ys = [], []
    for d in range(2):
        ar8, ai8 = slab(abar_re, d), slab(abar_im, d)
        ere, eim = _scan_ends(f"scan{d}_ends", su, w_re[d], w_im[d], ar8, ai8, d == 0)
        x_re, x_im, y_d = _scan_fwd(f"scan{d}", su, w_re[d], w_im[d], ar8, ai8, ere, eim, cb_re[d], cb_im[d],
                                    d == 0)
        xs += [x_re, x_im]
        ys.append(y_d)
    ypre, glu, ysn = _ssm_post(ys[0], ys[1], su, sg, wglu_b, bglu, sw, dvec, tr)

    dy, dout, loss, d_post = _out_loss(ya, ysn, h, tgt, wo_b, post_w, n_valid, tr)

    do_flat, dag, dysn, delta8, d_wo, d_aw = _out_bwd(dy, ya, ysn, o_flat, ag, wo_b, aw, head_sum, tr)
    dypre, dsg, d_wglu, d_bglu, d_sw, d_dvec = _ssm_post_bwd(dysn, glu, sg, ypre, su, wglu_b, sw, dvec, tr)

    dus, d_ct, d_wb, d_a8 = [], [], [], []
    for d in range(2):
        ar8, ai8c = slab(abar_re, d), slab(abar_im, d, -1.0)
        ere, eim = _scan_ends(f"scan_adj{d}_ends", dypre, cb_re[d], cb_im[d], ar8, ai8c, d != 0)
        du_d, dw_re, dw_im, dc_re, dc_im, da_re, da_im = _scan_bwd(
            f"scan_adj{d}", dypre, cb_re[d], cb_im[d], ar8, ai8c, ere, eim, su, w_re[d], w_im[d],
            xs[2 * d], xs[2 * d + 1], d != 0)
        dus.append(du_d)
        d_ct.append((dc_re, dc_im))
        d_wb.append((dw_re, dw_im))
        d_a8.append((da_re, da_im))

    dot4 = do_flat.astype(BF16).reshape(nq, tq, HEADS, V_HEAD).transpose(2, 0, 3, 1)
    dqt4, dkt_h, dvt_h = _attn_bwd(qt4, k_h, kt_h, v_h, dot4, lse4, delta8.T.reshape(HEADS, nq, 1, tq), tk)
    dq_t = dqt4.transpose(1, 3, 0, 2).reshape(lp, HEADS, QK_DIM)
    dk_t = dkt_h.transpose(2, 0, 1)
    dqn = dq_t[:, :, :64].reshape(lp, 512)
    dr1 = dq_t[:, :, 64:80].reshape(lp, 128)
    dr2 = dq_t[:, :, 80:96].reshape(lp, 128)
    dkn = dk_t[:, :, :64].reshape(lp, 512)
    dkr8 = dk_t[:, :, 64:].reshape(lp, HEADS * QK_ROPE)
    dvf = dvt_h.transpose(2, 0, 1).reshape(lp, 512)
    dql, dkvl, dkrr, d_wq, d_wkv, d_qw, d_kvw = _qkv_up_bwd(
        dqn, dr1, dr2, dkn, dvf, dkr8, ql, kvl, cos8, sin8, c32, s32, qw, kvw, wq_b, wkv_b, p32, sum8, tr)
    dh, d_win, d_pre = _inproj_bwd(dql, dkvl, dag, dus[0], dus[1], dypre, dsg, dkrr, h, dout, pre_w, w_in_b, dvec,
                                   tr_mid)
    dh = to_tokens(dh)

    def seg_sums(t):
        return t.reshape(N_SEG, N_GROUPS, SSM_STATE).transpose(1, 0, 2)

    da8_re = jnp.concatenate([seg_sums(d_a8[d][0]) for d in range(2)], axis=0)
    da8_im = jnp.concatenate([seg_sums(d_a8[d][1]) for d in range(2)], axis=0)
    dbb_re = jnp.concatenate([_diag_blocks(d_wb[d][0]) for d in range(2)], axis=0)
    dbb_im = jnp.concatenate([_diag_blocks(d_wb[d][1]) for d in range(2)], axis=0)
    g_are, g_aim, g_ldt, g_bt_re, g_bt_im = _ssm_disc_bwd(a_re3, a_im3, ldt3, bt_re, bt_im, da8_re, da8_im,
                                                          dbb_re, dbb_im)
    g_c_re = jnp.concatenate([_diag_blocks(d_ct[d][0]) for d in range(2)], axis=0)
    g_c_im = jnp.concatenate([-_diag_blocks(d_ct[d][1]) for d in range(2)], axis=0)

    def b_layout(t):
        return t.reshape(2, N_GROUPS, SSM_GROUP, SSM_STATE).transpose(0, 1, 3, 2)

    local = {
        'meta_tokens': dh[:N_META],
        'pre_norm_w': d_pre, 'post_norm_w': d_post,
        'w_in': _cols_in_inv(d_win[:, :D_IN]),
        'q_norm_w': d_qw, 'w_q_up': _cols_q_inv(d_wq),
        'kv_norm_w': d_kvw, 'w_kv_up': _cols_kv_inv(d_wkv),
        'attn_out_norm_w': d_aw,
        'ssm_a_re': g_are, 'ssm_a_im': g_aim, 'ssm_log_dt': g_ldt,
        'ssm_b_re': b_layout(g_bt_re), 'ssm_b_im': b_layout(g_bt_im), 'ssm_c_re': g_c_re, 'ssm_c_im': g_c_im,
        'ssm_d': d_dvec, 'w_glu': d_wglu, 'b_glu': d_bglu, 'ssm_out_norm_w': d_sw, 'w_out': d_wo,
    }

    replicated = [n for n in WEIGHTS if n not in SHARDED]
    order = SHARDED + replicated
    shapes = [wts[n].shape for n in order] + [(1, 1)]
    tr_adam = 512
    total_rows = -(-sum(_n_rows(s) for s in shapes) // tr_adam) * tr_adam
    recv = _exchange("exchange_grads",
                     _pack_per_device([_shard_views(n, local[n]) for n in SHARDED], n_shard_rows),
                     _pack([local[n] for n in replicated] + [loss], total_rows - n_shard_rows))
    zero = jnp.zeros((1, 1), F32)
    packed = [_pack([src[n] for n in order] + [zero], total_rows) for src in (wts, moms, vels)]
    g_p, d_p, m_p, v_p = _adamw(recv, *packed, tr_adam)
    sums = _unpack(g_p, shapes)
    grads = dict(zip(order, sums))
    deltas, new_m, new_v = (dict(zip(order, _unpack(b, shapes))) for b in (d_p, m_p, v_p))

    grad_x = dh[N_META:n_valid][None]
    return (sums[-1][0, 0], grad_x, *[grads[n] for n in WEIGHTS], *[deltas[n] for n in WEIGHTS],
            *[new_m[n] for n in WEIGHTS], *[new_v[n] for n in WEIGHTS])


def kernel(x, meta_tokens, pre_norm_w, post_norm_w, w_in, q_norm_w, w_q_up, kv_norm_w, w_kv_up, attn_out_norm_w, ssm_a_re, ssm_a_im, ssm_log_dt, ssm_b_re, ssm_b_im, ssm_c_re, ssm_c_im, ssm_d, w_glu, b_glu, ssm_out_norm_w, w_out, loss_target, m_meta_tokens, m_pre_norm_w, m_post_norm_w, m_w_in, m_q_norm_w, m_w_q_up, m_kv_norm_w, m_w_kv_up, m_attn_out_norm_w, m_ssm_a_re, m_ssm_a_im, m_ssm_log_dt, m_ssm_b_re, m_ssm_b_im, m_ssm_c_re, m_ssm_c_im, m_ssm_d, m_w_glu, m_b_glu, m_ssm_out_norm_w, m_w_out, v_meta_tokens, v_pre_norm_w, v_post_norm_w, v_w_in, v_q_norm_w, v_w_q_up, v_kv_norm_w, v_w_kv_up, v_attn_out_norm_w, v_ssm_a_re, v_ssm_a_im, v_ssm_log_dt, v_ssm_b_re, v_ssm_b_im, v_ssm_c_re, v_ssm_c_im, v_ssm_d, v_w_glu, v_b_glu, v_ssm_out_norm_w, v_w_out):
    wts = dict(zip(WEIGHTS, (meta_tokens, pre_norm_w, post_norm_w, w_in, q_norm_w, w_q_up, kv_norm_w, w_kv_up,
                             attn_out_norm_w, ssm_a_re, ssm_a_im, ssm_log_dt, ssm_b_re, ssm_b_im, ssm_c_re,
                             ssm_c_im, ssm_d, w_glu, b_glu, ssm_out_norm_w, w_out)))
    moms = dict(zip(WEIGHTS, (m_meta_tokens, m_pre_norm_w, m_post_norm_w, m_w_in, m_q_norm_w, m_w_q_up,
                              m_kv_norm_w, m_w_kv_up, m_attn_out_norm_w, m_ssm_a_re, m_ssm_a_im, m_ssm_log_dt,
                              m_ssm_b_re, m_ssm_b_im, m_ssm_c_re, m_ssm_c_im, m_ssm_d, m_w_glu, m_b_glu,
                              m_ssm_out_norm_w, m_w_out)))
    vels = dict(zip(WEIGHTS, (v_meta_tokens, v_pre_norm_w, v_post_norm_w, v_w_in, v_q_norm_w, v_w_q_up,
                              v_kv_norm_w, v_w_kv_up, v_attn_out_norm_w, v_ssm_a_re, v_ssm_a_im, v_ssm_log_dt,
                              v_ssm_b_re, v_ssm_b_im, v_ssm_c_re, v_ssm_c_im, v_ssm_d, v_w_glu, v_b_glu,
                              v_ssm_out_norm_w, v_w_out)))
    return _step(x, loss_target, wts, moms, vels)
```

```python
import functools
import math

import numpy as np
import jax
import jax.numpy as jnp
from jax import lax
from jax.experimental import pallas as pl
from jax.experimental.pallas import tpu as pltpu

F32 = jnp.float32
BF16 = jnp.bfloat16

D_MODEL = 1024
N_META = 16
EPS = 1e-6
HEADS = 8
QK_NOPE = 64
QK_ROPE = 32
HALF_ROPE = QK_ROPE // 2
QK_DIM = QK_NOPE + QK_ROPE
V_HEAD = 64
Q_LORA = 256
KV_LORA = 128
D_ATTN = HEADS * V_HEAD
D_SSM = 512
SSM_GROUP = 16
N_GROUPS = D_SSM // SSM_GROUP
SSM_STATE = 64
N_STATES = N_GROUPS * SSM_STATE
ROPE_THETA = 10000.0
D_IN = Q_LORA + KV_LORA + QK_ROPE + D_ATTN + 2 * D_SSM
D_IN_PAD = 2048
N_DEV = 8
N_SEG = 8
COL_BLK = 512
LANES = 128

ADAM_LR = 0.001
ADAM_B1 = 0.9
ADAM_B2 = 0.999
ADAM_EPS = 1e-08
ADAM_WD = 0.01
ADAM_STEP = 10

VMEM_LIMIT_V7X = 56 * 1024 * 1024
LOG2E = 1.0 / math.log(2.0)
Q_SCALE = LOG2E / math.sqrt(QK_DIM)
ATTN_UNROLL = 4
ATTN_BWD_UNROLL = 4

WEIGHTS = ['meta_tokens', 'pre_norm_w', 'post_norm_w', 'w_in', 'q_norm_w', 'w_q_up', 'kv_norm_w', 'w_kv_up',
           'attn_out_norm_w', 'ssm_a_re', 'ssm_a_im', 'ssm_log_dt', 'ssm_b_re', 'ssm_b_im', 'ssm_c_re', 'ssm_c_im',
           'ssm_d', 'w_glu', 'b_glu', 'ssm_out_norm_w', 'w_out']
SHARDED = ['w_in', 'w_q_up', 'w_kv_up', 'w_glu', 'w_out', 'meta_tokens']

def _cols_in(w):
    return jnp.concatenate([w[:, 0:384], w[:, 416:D_IN], w[:, 384:416]], axis=1)


def _cols_in_inv(w):
    return jnp.concatenate([w[:, 0:384], w[:, D_IN - QK_ROPE:D_IN], w[:, 384:D_IN - QK_ROPE]], axis=1)


def _cols_q(w):
    t = w.reshape(w.shape[0], HEADS, QK_DIM)
    return jnp.concatenate([t[:, :, 0:64].reshape(-1, 512), t[:, :, 64:80].reshape(-1, 128),
                            t[:, :, 80:96].reshape(-1, 128)], axis=1)


def _cols_q_inv(w):
    r = w.shape[0]
    return jnp.concatenate([w[:, 0:512].reshape(r, HEADS, 64), w[:, 512:640].reshape(r, HEADS, 16),
                            w[:, 640:768].reshape(r, HEADS, 16)], axis=2).reshape(r, HEADS * QK_DIM)


def _cols_kv(w):
    t = w.reshape(w.shape[0], HEADS, 128)
    return jnp.concatenate([t[:, :, 0:64].reshape(-1, 512), t[:, :, 64:128].reshape(-1, 512)], axis=1)


def _cols_kv_inv(w):
    r = w.shape[0]
    return jnp.concatenate([w[:, 0:512].reshape(r, HEADS, 64), w[:, 512:1024].reshape(r, HEADS, 64)],
                           axis=2).reshape(r, HEADS * 128)


def _pick(n, cands):
    for c in cands:
        if n % c == 0:
            return c
    raise ValueError(f"no tile for {n}")


def _cparams(*sem):
    return pltpu.CompilerParams(dimension_semantics=sem, vmem_limit_bytes=VMEM_LIMIT_V7X)


def _mm(a, b):
    return jnp.dot(a.astype(BF16), b.astype(BF16), preferred_element_type=F32)


def _mm_nt(a, b):
    return lax.dot_general(a.astype(BF16), b.astype(BF16), (((1,), (1,)), ((), ())), preferred_element_type=F32)


def _mm_tn(a, b):
    return lax.dot_general(a.astype(BF16), b.astype(BF16), (((0,), (0,)), ((), ())), preferred_element_type=F32)


def _mm_exact(a, b):
    return jnp.dot(a, b, precision=lax.Precision.HIGHEST, preferred_element_type=F32)


def _rms(x):
    return lax.rsqrt(jnp.mean(x * x, axis=-1, keepdims=True) + EPS)


def _rms_bwd(dy, x, r, w):
    xh = x * r
    g = dy * w
    dx = r * (g - xh * jnp.mean(g * xh, axis=-1, keepdims=True))
    dw = jnp.sum(dy * xh, axis=0, keepdims=True)
    return dx, dw


def _sigmoid(z):
    return 1.0 / (1.0 + jnp.exp(-z))


def _silu_and_grad(z):
    s = _sigmoid(z)
    return z * s, s * (1.0 + z * (1.0 - s))


_GELU_C = math.sqrt(2.0 / math.pi)


def _gelu_and_grad(x):
    x2 = x * x
    t = jnp.tanh(_GELU_C * (x + 0.044715 * x * x2))
    val = 0.5 * x * (1.0 + t)
    grad = 0.5 * (1.0 + t) + 0.5 * x * (1.0 - t * t) * _GELU_C * (1.0 + 3.0 * 0.044715 * x2)
    return val, grad


def _acc(ref, val, first):
    @pl.when(first)
    def _():
        ref[...] = val

    @pl.when(jnp.logical_not(first))
    def _():
        ref[...] += val


def _rows_call(name, body, tr, row_ins, full_ins, row_outs, acc_outs):
    lp = row_ins[0].shape[0]
    in_specs = [pl.BlockSpec((tr, a.shape[1]), lambda i: (i, 0)) for a in row_ins]
    in_specs += [pl.BlockSpec(a.shape, lambda i, n=a.ndim: (0,) * n) for a in full_ins]
    out_specs = [pl.BlockSpec((tr, c), lambda i: (i, 0)) for c, _ in row_outs]
    out_specs += [pl.BlockSpec(s, lambda i, n=len(s): (0,) * n) for s, _ in acc_outs]
    out_shape = [jax.ShapeDtypeStruct((lp, c), dt) for c, dt in row_outs]
    out_shape += [jax.ShapeDtypeStruct(s, dt) for s, dt in acc_outs]
    return pl.pallas_call(
        body, name=name, grid=(lp // tr,), in_specs=in_specs, out_specs=out_specs, out_shape=out_shape,
        compiler_params=_cparams("arbitrary"))(*row_ins, *full_ins)


def _inproj(h, pre_w, w_in_b, tr):
    def body(h_ref, pw_ref, w_ref, ql, kvl, ag, su, sg, kr):
        x = h_ref[...]
        xn = x * _rms(x) * pw_ref[...]
        pr = _mm(xn, w_ref[...])
        ql[...] = pr[:, 0:256]
        kvl[...] = pr[:, 256:384]
        ag[...] = pr[:, 384:896]
        su[...] = pr[:, 896:1408]
        sg[...] = pr[:, 1408:1920]
        kr[...] = pr[:, 1920:1952]

    return _rows_call("inproj", body, tr, [h], [pre_w, w_in_b],
                      [(256, F32), (128, F32), (512, F32), (512, F32), (512, F32), (32, F32)], [])


def _qkv_up(ql, kvl, kr, cos8, sin8, c32, s32, qw, kvw, wq_b, wkv_b, p32, tr):
    def body(ql_ref, kvl_ref, kr_ref, cos_ref, sin_ref, c32_ref, s32_ref, qw_ref, kvw_ref, wq_ref, wkv_ref, p_ref,
             qn_o, qr1_o, qr2_o, kn_o, v_o, kr_o):
        x = ql_ref[...]
        q = _mm(x * _rms(x) * qw_ref[...], wq_ref[...]) * Q_SCALE
        r1, r2 = q[:, 512:640], q[:, 640:768]
        cs, sn = cos_ref[...], sin_ref[...]
        qn_o[...] = q[:, 0:512].astype(BF16)
        qr1_o[...] = (r1 * cs - r2 * sn).astype(BF16)
        qr2_o[...] = (r2 * cs + r1 * sn).astype(BF16)
        x = kvl_ref[...]
        kv = _mm(x * _rms(x) * kvw_ref[...], wkv_ref[...])
        kn_o[...] = kv[:, 0:512].astype(BF16)
        v_o[...] = kv[:, 512:1024].astype(BF16)
        x = kr_ref[...]
        kr_o[...] = (x * c32_ref[...] + _mm_exact(x, p_ref[...]) * s32_ref[...]).astype(BF16)

    return _rows_call("qkv_up", body, tr, [ql, kvl, kr, cos8, sin8, c32, s32], [qw, kvw, wq_b, wkv_b, p32],
                      [(512, BF16), (128, BF16), (128, BF16), (512, BF16), (512, BF16), (32, BF16)], [])


def _row_position(row, lseg):
    return (row & (N_SEG - 1)) * lseg + (row >> 3)


def _first_padded_tile(n_valid, lp, tile):
    lseg = lp // N_SEG
    t0 = n_valid - (N_SEG - 1) * lseg
    return (t0 * N_SEG + N_SEG - 1) // tile if n_valid < lp else lp // tile


def _attn_fwd(qt, k, vxt, n_valid):
    _, nq, _, tq = qt.shape
    _, nk, _, tk = vxt.shape
    lp = k.shape[1]
    lseg = lp // N_SEG
    n_plain = max(0, min(nk, _first_padded_tile(n_valid, lp, tk)))
    grp = ATTN_UNROLL
    sizes = [grp] * (nk // grp) + ([nk % grp] if nk % grp else [])
    n_loop = n_plain // grp

    def body(q_ref, k_ref, v_ref, o_ref, lse_ref, m_o, p_hbm, m_s, acc_s, p_buf, p_sem):
        h, i = pl.program_id(0), pl.program_id(1)
        m_s[...] = jnp.full(m_s.shape, -1e30, F32)
        acc_s[...] = jnp.zeros(acc_s.shape, F32)
        qq = q_ref[0, 0]

        def p_copy(c, slot):
            return pltpu.make_async_copy(p_buf.at[slot], p_hbm.at[h, i, c], p_sem.at[slot])

        def chunk(c, slot, padded):
            r0 = pl.multiple_of(c * tk, tk)
            st = _mm(k_ref[0, pl.ds(r0, tk), :], qq)
            if padded:
                row = r0 + lax.broadcasted_iota(jnp.int32, (tk, tq), 0)
                st = jnp.where(_row_position(row, lseg) < n_valid, st, -1e30)
            m_old = m_s[...]
            m_new = jnp.maximum(m_old, jnp.max(st, axis=0, keepdims=True))
            pt = jnp.exp2(st - m_new).astype(BF16)
            acc_s[...] = jnp.exp2(m_old - m_new) * acc_s[...] + _mm(v_ref[0, c], pt)
            m_s[...] = m_new
            m_o[0, 0, c] = m_new
            p_buf[slot] = pt

        def group(g, size, padded):
            base = (g % 2) * grp

            @pl.when(g >= 2)
            def _():
                for j in range(grp):
                    p_copy(0, base + j).wait()

            for j in range(size):
                chunk(g * grp + j, base + j, padded[j])
            for j in range(size):
                p_copy(g * grp + j, base + j).start()

        def plain(g, carry):
            group(g, grp, [False] * grp)
            return carry

        if n_loop:
            lax.fori_loop(0, n_loop, plain, 0)
        for g in range(n_loop, len(sizes)):
            group(jnp.int32(g), sizes[g], [g * grp + j >= n_plain for j in range(sizes[g])])
        for g in range(max(0, len(sizes) - 2), len(sizes)):
            for j in range(sizes[g]):
                p_copy(0, (g % 2) * grp + j).wait()
        acc = acc_s[...]
        l = acc[V_HEAD:V_HEAD + 1, :]
        o_ref[0] = acc[:V_HEAD, :] / l
        lse_ref[0, 0] = m_s[...] + jnp.log2(l)

    return pl.pallas_call(
        body, name="attn_fwd", grid=(HEADS, nq),
        in_specs=[pl.BlockSpec((1, 1, QK_DIM, tq), lambda h, i: (h, i, 0, 0)),
                  pl.BlockSpec((1, lp, QK_DIM), lambda h, i: (h, 0, 0)),
                  pl.BlockSpec((1, nk, LANES, tk), lambda h, i: (h, 0, 0, 0))],
        out_specs=[pl.BlockSpec((1, V_HEAD, tq), lambda h, i: (h, 0, i)),
                   pl.BlockSpec((1, 1, 1, tq), lambda h, i: (h, i, 0, 0)),
                   pl.BlockSpec((1, 1, nk, 1, tq), lambda h, i: (h, i, 0, 0, 0)),
                   pl.BlockSpec(memory_space=pl.ANY)],
        out_shape=[jax.ShapeDtypeStruct((HEADS, V_HEAD, lp), F32), jax.ShapeDtypeStruct((HEADS, nq, 1, tq), F32),
                   jax.ShapeDtypeStruct((HEADS, nq, nk, 1, tq), F32),
                   jax.ShapeDtypeStruct((HEADS, nq, nk, tk, tq), BF16)],
        scratch_shapes=[pltpu.VMEM((1, tq), F32), pltpu.VMEM((LANES, tq), F32),
                        pltpu.VMEM((2 * grp, tk, tq), BF16), pltpu.SemaphoreType.DMA((2 * grp,))],
        compiler_params=_cparams("arbitrary", "arbitrary"))(qt, k, vxt)


def _attn_post(o_flat, ag, aw, tr):
    def body(o_ref, g_ref, w_ref, ya):
        t = o_ref[...] * _silu_and_grad(g_ref[...])[0]
        ya[...] = t * _rms(t) * w_ref[...]

    return _rows_call("attn_post", body, tr, [o_flat, ag], [aw], [(512, F32)], [])[0]


def _scan_tiles(lp):
    lseg = lp // N_SEG
    tt = _pick(lseg, [208, 48, 32, 16, 8, 4, 2, 1])
    return lseg, tt, lseg // tt


def _cmul(ar, ai, br, bi):
    return ar * br - ai * bi, ar * bi + ai * br


N_COL_BLK = N_STATES // COL_BLK
CH_BLK = D_SSM // N_COL_BLK


def _scan_steps(tt, forward, bre_ref, bim_ref, ar, ai, carry, visit):
    def step(s, c):
        r0 = pl.multiple_of((s if forward else tt - 1 - s) * N_SEG, N_SEG)
        pr, pi = _cmul(ar, ai, c[0], c[1])
        xr = pr + bre_ref[pl.ds(r0, N_SEG), :]
        xi = pi + bim_ref[pl.ds(r0, N_SEG), :]
        return (xr, xi) + tuple(visit(r0, (xr, xi), (c[0], c[1]), c[2:]))

    return lax.fori_loop(0, tt, step, carry, unroll=4 if tt % 4 == 0 else 1)


def _segment_starts(lseg, forward, ar, ai, ere_ref, eim_ref, s_re, s_im):
    a1r, a1i = ar[0:1, :], ai[0:1, :]
    pr, pi = jnp.ones_like(a1r), jnp.zeros_like(a1i)
    br, bi = a1r, a1i
    n = lseg
    while n:
        if n & 1:
            pr, pi = _cmul(pr, pi, br, bi)
        n >>= 1
        if n:
            br, bi = _cmul(br, bi, br, bi)
    cr, ci = jnp.zeros_like(a1r), jnp.zeros_like(a1i)
    for j in (range(N_SEG) if forward else range(N_SEG - 1, -1, -1)):
        s_re[j:j + 1, :] = cr
        s_im[j:j + 1, :] = ci
        nr, ni = _cmul(pr, pi, cr, ci)
        cr = nr + ere_ref[j:j + 1, :]
        ci = ni + eim_ref[j:j + 1, :]


def _scan_specs(lp, forward):
    lseg, tt, nt = _scan_tiles(lp)

    def tile(t):
        return t if forward else nt - 1 - t

    rows = lambda w: pl.BlockSpec((tt * N_SEG, w), lambda cb, t: (tile(t), cb))
    proj = pl.BlockSpec((1, CH_BLK, COL_BLK), lambda cb, t: (cb, 0, 0))
    slab = pl.BlockSpec((N_SEG, COL_BLK), lambda cb, t: (0, cb))
    return lseg, tt, nt, rows, proj, slab


def _scan_ends(name, urows, wre4, wim4, ar8, ai8, forward):
    lp = urows.shape[0]
    lseg, tt, nt, rows, proj, slab = _scan_specs(lp, forward)

    def body(u_ref, wre_ref, wim_ref, ar_ref, ai_ref, ere_o, eim_o, bre_s, bim_s, cr_s, ci_s):
        t = pl.program_id(1)

        @pl.when(t == 0)
        def _():
            cr_s[...] = jnp.zeros(cr_s.shape, F32)
            ci_s[...] = jnp.zeros(ci_s.shape, F32)

        u = u_ref[...]
        bre_s[...] = _mm(u, wre_ref[0])
        bim_s[...] = _mm(u, wim_ref[0])
        cr, ci = _scan_steps(tt, forward, bre_s, bim_s, ar_ref[...], ai_ref[...], (cr_s[...], ci_s[...]),
                             lambda r0, x, x_prev, extra: ())
        cr_s[...] = cr
        ci_s[...] = ci

        @pl.when(t == nt - 1)
        def _():
            ere_o[...] = cr
            eim_o[...] = ci

    return pl.pallas_call(
        body, name=name, grid=(N_COL_BLK, nt), in_specs=[rows(CH_BLK), proj, proj, slab, slab],
        out_specs=[slab, slab], out_shape=[jax.ShapeDtypeStruct((N_SEG, N_STATES), F32)] * 2,
        scratch_shapes=[pltpu.VMEM((tt * N_SEG, COL_BLK), F32)] * 2 + [pltpu.VMEM((N_SEG, COL_BLK), F32)] * 2,
        compiler_params=_cparams("arbitrary", "arbitrary"))(urows, wre4, wim4, ar8, ai8)


def _scan_fwd(name, urows, wre4, wim4, ar8, ai8, ere, eim, cre4, cim4, forward):
    lp = urows.shape[0]
    lseg, tt, nt, rows, proj, slab = _scan_specs(lp, forward)

    def body(u_ref, wre_ref, wim_ref, ar_ref, ai_ref, ere_ref, eim_ref, cre_ref, cim_ref,
             xre_o, xim_o, y_o, bre_s, bim_s, cr_s, ci_s):
        ar, ai = ar_ref[...], ai_ref[...]

        @pl.when(pl.program_id(1) == 0)
        def _():
            _segment_starts(lseg, forward, ar, ai, ere_ref, eim_ref, cr_s, ci_s)

        u = u_ref[...]
        bre_s[...] = _mm(u, wre_ref[0])
        bim_s[...] = _mm(u, wim_ref[0])

        def visit(r0, x, x_prev, extra):
            xre_o[pl.ds(r0, N_SEG), :] = x[0]
            xim_o[pl.ds(r0, N_SEG), :] = x[1]
            return ()

        cr, ci = _scan_steps(tt, forward, bre_s, bim_s, ar, ai, (cr_s[...], ci_s[...]), visit)
        cr_s[...] = cr
        ci_s[...] = ci
        y_o[...] = _mm_nt(xre_o[...], cre_ref[0]) + _mm_nt(xim_o[...], cim_ref[0])

    return pl.pallas_call(
        body, name=name, grid=(N_COL_BLK, nt),
        in_specs=[rows(CH_BLK), proj, proj, slab, slab, slab, slab, proj, proj],
        out_specs=[rows(COL_BLK), rows(COL_BLK), rows(CH_BLK)],
        out_shape=[jax.ShapeDtypeStruct((lp, N_STATES), F32)] * 2 + [jax.ShapeDtypeStruct((lp, D_SSM), F32)],
        scratch_shapes=[pltpu.VMEM((tt * N_SEG, COL_BLK), F32)] * 2 + [pltpu.VMEM((N_SEG, COL_BLK), F32)] * 2,
        compiler_params=_cparams("arbitrary", "arbitrary"))(urows, wre4, wim4, ar8, ai8, ere, eim, cre4, cim4)


def _scan_bwd(name, dyrows, cre4, cim4, ar8, ai8, ere, eim, urows, wre4, wim4, xre, xim, forward):
    lp = urows.shape[0]
    lseg, tt, nt, rows, proj, slab = _scan_specs(lp, forward)

    def body(dy_ref, cre_ref, cim_ref, ar_ref, ai_ref, ere_ref, eim_ref, u_ref, wre_ref, wim_ref, xre_ref, xim_ref,
             du_o, dwre_o, dwim_o, dcre_o, dcim_o, dare_o, daim_o, bre_s, bim_s, gre_s, gim_s, cr_s, ci_s):
        t = pl.program_id(1)
        ar, ai = ar_ref[...], ai_ref[...]

        @pl.when(t == 0)
        def _():
            _segment_starts(lseg, forward, ar, ai, ere_ref, eim_ref, cr_s, ci_s)
            dare_o[...] = jnp.zeros(dare_o.shape, F32)
            daim_o[...] = jnp.zeros(daim_o.shape, F32)

        dy = dy_ref[...]
        bre_s[...] = _mm(dy, cre_ref[0])
        bim_s[...] = _mm(dy, cim_ref[0])

        def visit(r0, g, g_prev, sums):
            gre_s[pl.ds(r0, N_SEG), :] = g[0]
            gim_s[pl.ds(r0, N_SEG), :] = g[1]
            fr = xre_ref[pl.ds(r0, N_SEG), :]
            fi = xim_ref[pl.ds(r0, N_SEG), :]
            pr, pi = g_prev
            return sums[0] + fr * pr + fi * pi, sums[1] + fr * pi - fi * pr

        out = _scan_steps(tt, forward, bre_s, bim_s, ar, ai, (cr_s[...], ci_s[...], dare_o[...], daim_o[...]), visit)
        cr_s[...] = out[0]
        ci_s[...] = out[1]
        dare_o[...] = out[2]
        daim_o[...] = out[3]
        gre, gim = gre_s[...], gim_s[...]
        du_o[...] = _mm_nt(gre, wre_ref[0]) + _mm_nt(gim, wim_ref[0])
        u = u_ref[...]
        first = t == 0
        _acc(dwre_o, _mm_tn(u, gre)[None], first)
        _acc(dwim_o, _mm_tn(u, gim)[None], first)
        _acc(dcre_o, _mm_tn(dy, xre_ref[...])[None], first)
        _acc(dcim_o, _mm_tn(dy, xim_ref[...])[None], first)

    big = pltpu.VMEM((tt * N_SEG, COL_BLK), F32)
    small = pltpu.VMEM((N_SEG, COL_BLK), F32)
    return pl.pallas_call(
        body, name=name, grid=(N_COL_BLK, nt),
        in_specs=[rows(CH_BLK), proj, proj, slab, slab, slab, slab, rows(CH_BLK), proj, proj,
                  rows(COL_BLK), rows(COL_BLK)],
        out_specs=[rows(CH_BLK), proj, proj, proj, proj, slab, slab],
        out_shape=[jax.ShapeDtypeStruct((lp, D_SSM), F32)]
        + [jax.ShapeDtypeStruct((N_COL_BLK, CH_BLK, COL_BLK), F32)] * 4
        + [jax.ShapeDtypeStruct((N_SEG, N_STATES), F32)] * 2,
        scratch_shapes=[big, big, big, big, small, small],
        compiler_params=_cparams("arbitrary", "arbitrary"))(
            dyrows, cre4, cim4, ar8, ai8, ere, eim, urows, wre4, wim4, xre, xim)


def _ssm_post(yf, yb, u, sg, wglu_b, bglu, sw, dvec, tr):
    def body(yf_ref, yb_ref, u_ref, g_ref, w_ref, b_ref, sw_ref, d_ref, ypre_o, glu_o, ysn_o):
        ypre = yf_ref[...] + yb_ref[...] + d_ref[...] * u_ref[...]
        ypre_o[...] = ypre
        glu = _mm(_gelu_and_grad(ypre)[0], w_ref[...]) + b_ref[...]
        glu_o[...] = glu
        t = glu[:, :D_SSM] * _sigmoid(glu[:, D_SSM:]) * _silu_and_grad(g_ref[...])[0]
        ysn_o[...] = t * _rms(t) * sw_ref[...]

    return _rows_call("ssm_post", body, tr, [yf, yb, u, sg], [wglu_b, bglu, sw, dvec],
                      [(512, F32), (1024, F32), (512, F32)], [])


def _out_loss(ya, ysn, h, tgt, wo_b, post_w, n_valid, tr):
    lseg = h.shape[0] // N_SEG

    def body(ya_ref, ys_ref, h_ref, t_ref, w_ref, pw_ref, dy_o, dout_o, loss_o, dpw_o):
        i = pl.program_id(0)
        y = _mm(ya_ref[...], w_ref[0:D_ATTN, :]) + _mm(ys_ref[...], w_ref[D_ATTN:, :])
        r = _rms(y)
        pw = pw_ref[...]
        out = h_ref[...] + y * r * pw
        pos = _row_position(i * tr + lax.broadcasted_iota(jnp.int32, (tr, 1), 0), lseg)
        valid = jnp.logical_and(pos >= N_META, pos < n_valid)
        diff = jnp.where(valid, out - t_ref[...], 0.0)
        dout = diff * (1.0 / D_MODEL)
        dy, dpw = _rms_bwd(dout, y, r, pw)
        dy_o[...] = dy
        dout_o[...] = dout
        _acc(loss_o, 0.5 * jnp.sum(jnp.sum(diff * diff, axis=1, keepdims=True), axis=0, keepdims=True)
             * (1.0 / D_MODEL), i == 0)
        _acc(dpw_o, dpw, i == 0)

    return _rows_call("out_loss", body, tr, [ya, ysn, h, tgt], [wo_b, post_w], [(1024, F32), (1024, F32)],
                      [((1, 1), F32), ((1, D_MODEL), F32)])


def _out_bwd(dy, ya, ysn, o_flat, ag, wo_b, aw, head_sum, tr):
    def body(dy_ref, ya_ref, ys_ref, o_ref, g_ref, w_ref, aw_ref, hs_ref, do_o, dag_o, dysn_o, dl_o, dwo_o, daw_o):
        i = pl.program_id(0)
        dy = dy_ref[...]
        dcat = _mm_nt(dy, w_ref[...])
        cat = jnp.concatenate([ya_ref[...], ys_ref[...]], axis=1)
        _acc(dwo_o, _mm_tn(cat, dy), i == 0)
        dysn_o[...] = dcat[:, D_ATTN:]
        o = o_ref[...]
        sl, dsl = _silu_and_grad(g_ref[...])
        t = o * sl
        dt, daw = _rms_bwd(dcat[:, :D_ATTN], t, _rms(t), aw_ref[...])
        _acc(daw_o, daw, i == 0)
        do = dt * sl
        do_o[...] = do
        dag_o[...] = dt * o * dsl
        dl_o[...] = _mm_exact(do * o, hs_ref[...])

    return _rows_call("out_bwd", body, tr, [dy, ya, ysn, o_flat, ag], [wo_b, aw, head_sum],
                      [(512, F32), (512, F32), (512, F32), (HEADS, F32)],
                      [((D_MODEL, D_MODEL), F32), ((1, D_ATTN), F32)])


def _ssm_post_bwd(dysn, glu, sg, ypre, u, wglu_b, sw, dvec, tr):
    def body(d_ref, glu_ref, sg_ref, y_ref, u_ref, w_ref, sw_ref, dv_ref,
             dyp_o, dsg_o, dwg_o, dbg_o, dsw_o, dd_o):
        i = pl.program_id(0)
        glu = glu_ref[...]
        a, b = glu[:, :D_SSM], glu[:, D_SSM:]
        sb = _sigmoid(b)
        ys = a * sb
        sl, dsl = _silu_and_grad(sg_ref[...])
        t = ys * sl
        dt, dsw = _rms_bwd(d_ref[...], t, _rms(t), sw_ref[...])
        _acc(dsw_o, dsw, i == 0)
        dsg_o[...] = dt * ys * dsl
        dys = dt * sl
        dglu = jnp.concatenate([dys * sb, dys * a * sb * (1.0 - sb)], axis=1)
        _acc(dbg_o, jnp.sum(dglu, axis=0, keepdims=True), i == 0)
        gel, dgel = _gelu_and_grad(y_ref[...])
        _acc(dwg_o, _mm_tn(gel, dglu), i == 0)
        dyp = _mm_nt(dglu, w_ref[...]) * dgel
        dyp_o[...] = dyp
        _acc(dd_o, jnp.sum(dyp * u_ref[...], axis=0, keepdims=True), i == 0)

    return _rows_call("ssm_post_bwd", body, tr, [dysn, glu, sg, ypre, u], [wglu_b, sw, dvec],
                      [(512, F32), (512, F32)],
                      [((D_SSM, 2 * D_SSM), F32), ((1, 2 * D_SSM), F32), ((1, D_SSM), F32), ((1, D_SSM), F32)])


def _attn_bwd(qt, kt, v, dot, lse_t, delta_t, m_t, p_t, tk):
    _, nq, _, tq = qt.shape
    lp = v.shape[1]
    nk = lp // tk
    _, _, nk_f, tk_f, _ = p_t.shape
    per_f = tk_f // tk
    assert tk_f % tk == 0 and lse_t.shape == (HEADS, nq, 1, tq) and delta_t.shape == (HEADS, nq, 1, tq)
    grp = ATTN_BWD_UNROLL
    sizes = [grp] * (nq // grp) + ([nq % grp] if nq % grp else [])
    n_loop = max(0, nq // grp - 1)

    def body(q_ref, kt_ref, v_ref, do_ref, lse_ref, dl_ref, m_ref, p_hbm, dq_o, dk_o, dv_o, dk_s, dv_s, p_buf, p_sem):
        h, j = pl.program_id(0), pl.program_id(1)
        kc = j // per_f
        row0 = pl.multiple_of((j % per_f) * tk, tk)

        @pl.when(j == 0)
        def _():
            dq_o[...] = jnp.zeros(dq_o.shape, F32)

        dk_s[...] = jnp.zeros(dk_s.shape, F32)
        dv_s[...] = jnp.zeros(dv_s.shape, F32)
        kkt = kt_ref[0]
        vv = v_ref[0]

        def p_copy(c, slot):
            return pltpu.make_async_copy(p_hbm.at[h, c, kc, pl.ds(row0, tk), :], p_buf.at[slot], p_sem.at[slot])

        def fetch(g, size):
            for i in range(size):
                p_copy(g * grp + i, (g % 2) * grp + i).start()

        def chunk(c, slot):
            qq = q_ref[0, c]
            dd = do_ref[0, c]
            scale = jnp.exp2(m_ref[0, c, kc] - lse_ref[0, c])
            pt = p_buf[slot]
            dv_s[...] += _mm_nt(dd * scale, pt)
            dst = (pt.astype(F32) * (scale * (_mm(vv, dd) - dl_ref[0, c]))).astype(BF16)
            dk_s[...] += _mm_nt(qq, dst)
            dq_o[0, c] += _mm(kkt, dst)

        def group(g, size, next_size):
            if next_size:
                fetch(g + 1, next_size)
            for i in range(size):
                p_copy(0, (g % 2) * grp + i).wait()
            for i in range(size):
                chunk(g * grp + i, (g % 2) * grp + i)

        def full(g, carry):
            group(g, grp, grp)
            return carry

        fetch(0, sizes[0])
        if n_loop:
            lax.fori_loop(0, n_loop, full, 0)
        for g in range(n_loop, len(sizes)):
            group(jnp.int32(g), sizes[g], sizes[g + 1] if g + 1 < len(sizes) else 0)
        dk_o[0] = dk_s[...]
        dv_o[0] = dv_s[...]

    head = lambda w: pl.BlockSpec((1, nq, w, tq), lambda h, j: (h, 0, 0, 0))
    rows = lambda w: pl.BlockSpec((1, tk, w), lambda h, j: (h, j, 0))
    cols = lambda w: pl.BlockSpec((1, w, tk), lambda h, j: (h, 0, j))
    return pl.pallas_call(
        body, name="attn_bwd", grid=(HEADS, nk),
        in_specs=[head(QK_DIM), cols(QK_DIM), rows(V_HEAD), head(V_HEAD), head(1), head(1),
                  pl.BlockSpec((1, nq, nk_f, 1, tq), lambda h, j: (h, 0, 0, 0, 0)),
                  pl.BlockSpec(memory_space=pl.ANY)],
        out_specs=[head(QK_DIM), cols(QK_DIM), cols(V_HEAD)],
        out_shape=[jax.ShapeDtypeStruct((HEADS, nq, QK_DIM, tq), F32), jax.ShapeDtypeStruct((HEADS, QK_DIM, lp), F32),
                   jax.ShapeDtypeStruct((HEADS, V_HEAD, lp), F32)],
        scratch_shapes=[pltpu.VMEM((QK_DIM, tk), F32), pltpu.VMEM((V_HEAD, tk), F32),
                        pltpu.VMEM((2 * grp, tk, tq), BF16), pltpu.SemaphoreType.DMA((2 * grp,))],
        compiler_params=_cparams("arbitrary", "arbitrary"))(qt, kt, v, dot, lse_t, delta_t, m_t, p_t)


def _qkv_up_bwd(dqn, dr1, dr2, dkn, dv, dkr8, ql, kvl, cos8, sin8, c32, s32, qw, kvw, wq_b, wkv_b, p32, sum8, tr):
    def body(dqn_ref, dr1_ref, dr2_ref, dkn_ref, dv_ref, dkr_ref, ql_ref, kvl_ref, cos_ref, sin_ref, c32_ref,
             s32_ref, qw_ref, kvw_ref, wq_ref, wkv_ref, p_ref, s8_ref,
             dql_o, dkvl_o, dkrr_o, dwq_o, dwkv_o, dqw_o, dkvw_o):
        i = pl.program_id(0)
        cs, sn = cos_ref[...], sin_ref[...]
        d1, d2 = dr1_ref[...], dr2_ref[...]
        dq = jnp.concatenate([dqn_ref[...], d1 * cs + d2 * sn, d2 * cs - d1 * sn], axis=1) * (Q_SCALE / LOG2E)
        x = ql_ref[...]
        r = _rms(x)
        qw = qw_ref[...]
        _acc(dwq_o, _mm_tn(x * r * qw, dq), i == 0)
        dx, dw = _rms_bwd(_mm_nt(dq, wq_ref[...]), x, r, qw)
        dql_o[...] = dx
        _acc(dqw_o, dw, i == 0)
        dkv = jnp.concatenate([dkn_ref[...] * (1.0 / LOG2E), dv_ref[...]], axis=1)
        x = kvl_ref[...]
        r = _rms(x)
        kvw = kvw_ref[...]
        _acc(dwkv_o, _mm_tn(x * r * kvw, dkv), i == 0)
        dx, dw = _rms_bwd(_mm_nt(dkv, wkv_ref[...]), x, r, kvw)
        dkvl_o[...] = dx
        _acc(dkvw_o, dw, i == 0)
        dkr = _mm_exact(dkr_ref[...], s8_ref[...]) * (1.0 / LOG2E)
        dkrr_o[...] = dkr * c32_ref[...] + _mm_exact(dkr * s32_ref[...], p_ref[...])

    return _rows_call("qkv_up_bwd", body, tr, [dqn, dr1, dr2, dkn, dv, dkr8, ql, kvl, cos8, sin8, c32, s32],
                      [qw, kvw, wq_b, wkv_b, p32, sum8], [(256, F32), (128, F32), (32, F32)],
                      [((Q_LORA, 768), F32), ((KV_LORA, 1024), F32), ((1, Q_LORA), F32), ((1, KV_LORA), F32)])


def _inproj_bwd(dql, dkvl, dag, du_f, du_b, dypre, dsg, dkr, h, dout, pre_w, w_in_b, dvec, tr):
    def body(dql_ref, dkvl_ref, dag_ref, duf_ref, dub_ref, dyp_ref, dsg_ref, dkr_ref, h_ref, dout_ref,
             pw_ref, w_ref, dv_ref, dh_o, dwin_o, dpw_o):
        i = pl.program_id(0)
        du = duf_ref[...] + dub_ref[...] + dv_ref[...] * dyp_ref[...]
        dproj = jnp.concatenate([dql_ref[...], dkvl_ref[...], dag_ref[...], du, dsg_ref[...],
                                 dkr_ref[...], jnp.zeros((tr, D_IN_PAD - D_IN), F32)], axis=1)
        x = h_ref[...]
        r = _rms(x)
        pw = pw_ref[...]
        _acc(dwin_o, _mm_tn(x * r * pw, dproj), i == 0)
        dx, dw = _rms_bwd(_mm_nt(dproj, w_ref[...]), x, r, pw)
        _acc(dpw_o, dw, i == 0)
        dh_o[...] = dout_ref[...] + dx

    return _rows_call("inproj_bwd", body, tr, [dql, dkvl, dag, du_f, du_b, dypre, dsg, dkr, h, dout],
                      [pre_w, w_in_b, dvec], [(1024, F32)], [((D_MODEL, D_IN_PAD), F32), ((1, D_MODEL), F32)])


def _disc_terms(a_re, a_im, ldt):
    dt = jnp.exp(ldt)
    mag = jnp.exp(a_re * dt)
    th = a_im * dt
    cs, sn = jnp.cos(th), jnp.sin(th)
    abar_re, abar_im = mag * cs, mag * sn
    num_re, num_im = abar_re - 1.0, abar_im
    den = a_re * a_re + a_im * a_im
    coef_re = (num_re * a_re + num_im * a_im) / den
    coef_im = (num_im * a_re - num_re * a_im) / den
    return dt, mag, cs, sn, abar_re, abar_im, num_re, num_im, den, coef_re, coef_im


def _ssm_disc(a_re, a_im, ldt, bt_re, bt_im):
    def body(ar_ref, ai_ref, l_ref, br_ref, bi_ref, abr_o, abi_o, bbr_o, bbi_o):
        t = _disc_terms(ar_ref[...], ai_ref[...], l_ref[...])
        abr_o[...] = t[4]
        abi_o[...] = t[5]
        cr, ci = t[9], t[10]
        br, bi = br_ref[...], bi_ref[...]
        bbr_o[...] = cr * br - ci * bi
        bbi_o[...] = cr * bi + ci * br

    ng = a_re.shape[0]
    return pl.pallas_call(
        body, name="ssm_disc",
        out_shape=[jax.ShapeDtypeStruct((ng, 1, SSM_STATE), F32)] * 2
        + [jax.ShapeDtypeStruct((ng, SSM_GROUP, SSM_STATE), F32)] * 2)(a_re, a_im, ldt, bt_re, bt_im)


def _ssm_disc_bwd(a_re, a_im, ldt, bt_re, bt_im, da8_re, da8_im, dbb_re, dbb_im):
    def body(ar_ref, ai_ref, l_ref, br_ref, bi_ref, dar_ref, dai_ref, dbr_ref, dbi_ref,
             gar_o, gai_o, gl_o, gbr_o, gbi_o):
        a_re, a_im = ar_ref[...], ai_ref[...]
        dt, mag, cs, sn, abar_re, abar_im, num_re, num_im, den, cr, ci = _disc_terms(a_re, a_im, l_ref[...])
        br, bi = br_ref[...], bi_ref[...]
        dbr, dbi = dbr_ref[...], dbi_ref[...]
        gbr_o[...] = cr * dbr + ci * dbi
        gbi_o[...] = cr * dbi - ci * dbr
        dcr = jnp.sum(br * dbr + bi * dbi, axis=1, keepdims=True)
        dci = jnp.sum(br * dbi - bi * dbr, axis=1, keepdims=True)
        dnum_re = (dcr * a_re - dci * a_im) / den
        dnum_im = (dcr * a_im + dci * a_re) / den
        dden = -(dcr * cr + dci * ci) / den
        g_are = (dcr * num_re + dci * num_im) / den + dden * 2.0 * a_re
        g_aim = (dcr * num_im - dci * num_re) / den + dden * 2.0 * a_im
        d_abr = jnp.sum(dar_ref[...], axis=1, keepdims=True) + dnum_re
        d_abi = jnp.sum(dai_ref[...], axis=1, keepdims=True) + dnum_im
        dmag = d_abr * cs + d_abi * sn
        dth = d_abi * abar_re - d_abr * abar_im
        g_are = g_are + dmag * mag * dt
        g_aim = g_aim + dth * dt
        ddt = jnp.sum(dmag * mag * a_re + dth * a_im, axis=2, keepdims=True)
        gar_o[...] = g_are
        gai_o[...] = g_aim
        gl_o[...] = ddt * dt

    ng = a_re.shape[0]
    return pl.pallas_call(
        body, name="ssm_disc_bwd",
        out_shape=[jax.ShapeDtypeStruct((ng, 1, SSM_STATE), F32)] * 2 + [jax.ShapeDtypeStruct((ng, 1, 1), F32)]
        + [jax.ShapeDtypeStruct((ng, SSM_GROUP, SSM_STATE), F32)] * 2)(
            a_re, a_im, ldt, bt_re, bt_im, da8_re, da8_im, dbb_re, dbb_im)


def _exchange(name, per_peer, shared):
    parts = [a for a in (per_peer, shared) if a is not None]
    rp = per_peer.shape[1] if per_peer is not None else 0
    rs = shared.shape[0] if shared is not None else 0
    n = len(parts)

    def body(*refs):
        in_refs, out_ref, send_sems, recv_sems, local_sems = refs[:n], refs[n], refs[n + 1], refs[n + 2], refs[n + 3]
        x, y, c = lax.axis_index("x"), lax.axis_index("y"), lax.axis_index("c")
        me = 4 * x + 2 * y + c

        def pieces(peer):
            out = []
            if per_peer is not None:
                out.append((in_refs[0].at[peer], out_ref.at[me, pl.ds(0, rp), :]))
            if shared is not None:
                out.append((in_refs[-1], out_ref.at[me, pl.ds(rp, rs), :]))
            return out

        copies = []
        for k in range(1, N_DEV):
            px = 1 - x if (k >> 2) & 1 else x
            py = 1 - y if (k >> 1) & 1 else y
            pc = 1 - c if k & 1 else c
            for j, (src, dst) in enumerate(pieces(4 * px + 2 * py + pc)):
                s = (k - 1) * n + j
                copies.append(pltpu.make_async_remote_copy(
                    src_ref=src, dst_ref=dst, send_sem=send_sems.at[s], recv_sem=recv_sems.at[s],
                    device_id=(px, py, pc), device_id_type=pl.DeviceIdType.MESH))
        mine = [pltpu.make_async_copy(src, dst, local_sems.at[j]) for j, (src, dst) in enumerate(pieces(me))]
        for cp in mine + copies:
            cp.start()
        for cp in copies + mine:
            cp.wait()

    n_sem = (N_DEV - 1) * n
    return pl.pallas_call(
        body, name=name, out_shape=jax.ShapeDtypeStruct((N_DEV, rp + rs, LANES), F32),
        in_specs=[pl.BlockSpec(memory_space=pl.ANY)] * n, out_specs=pl.BlockSpec(memory_space=pl.ANY),
        scratch_shapes=[pltpu.SemaphoreType.DMA((n_sem,)), pltpu.SemaphoreType.DMA((n_sem,)),
                        pltpu.SemaphoreType.DMA((n,))])(*parts)


def _adamw(recv, w, m, v, tr):
    rows = w.shape[0]
    c1 = 1.0 - ADAM_B1 ** ADAM_STEP
    c2 = 1.0 - ADAM_B2 ** ADAM_STEP

    def body(r_ref, w_ref, m_ref, v_ref, g_o, d_o, m_o, v_o):
        g = r_ref[0]
        for k in range(1, N_DEV):
            g = g + r_ref[k]
        mm = ADAM_B1 * m_ref[...] + (1.0 - ADAM_B1) * g
        vv = ADAM_B2 * v_ref[...] + (1.0 - ADAM_B2) * (g * g)
        g_o[...] = g
        m_o[...] = mm
        v_o[...] = vv
        d_o[...] = -ADAM_LR * ((mm / c1) / (jnp.sqrt(vv / c2) + ADAM_EPS) + ADAM_WD * w_ref[...])

    spec = pl.BlockSpec((tr, LANES), lambda i: (i, 0))
    return pl.pallas_call(
        body, name="adamw", grid=(rows // tr,),
        in_specs=[pl.BlockSpec((N_DEV, tr, LANES), lambda i: (0, i, 0)), spec, spec, spec],
        out_specs=[spec] * 4, out_shape=[jax.ShapeDtypeStruct((rows, LANES), F32)] * 4,
        compiler_params=_cparams("arbitrary"))(recv, w, m, v)


def _to_rows(a):
    flat = a.reshape(-1)
    pad = (-flat.shape[0]) % LANES
    if pad:
        flat = jnp.concatenate([flat, jnp.zeros((pad,), flat.dtype)])
    return flat.reshape(-1, LANES)


def _n_rows(shape):
    return -(-int(np.prod(shape)) // LANES)


def _pack(arrays, total_rows):
    rows = [_to_rows(a) for a in arrays]
    used = sum(r.shape[0] for r in rows)
    if total_rows > used:
        rows.append(jnp.zeros((total_rows - used, LANES), F32))
    return jnp.concatenate(rows, axis=0)


def _unpack(buf, shapes):
    lead = buf.shape[:-2]
    out, r0 = [], 0
    for s in shapes:
        n = int(np.prod(s))
        nr = _n_rows(s)
        out.append(buf[..., r0:r0 + nr, :].reshape(lead + (-1,))[..., :n].reshape(lead + tuple(s)))
        r0 += nr
    return out


def _pack_per_device(arrays, total_rows):
    rows = []
    for a in arrays:
        flat = a.reshape(N_DEV, -1)
        pad = (-flat.shape[1]) % LANES
        if pad:
            flat = jnp.concatenate([flat, jnp.zeros((N_DEV, pad), flat.dtype)], axis=1)
        rows.append(flat.reshape(N_DEV, -1, LANES))
    used = sum(r.shape[1] for r in rows)
    if total_rows > used:
        rows.append(jnp.zeros((N_DEV, total_rows - used, LANES), F32))
    return jnp.concatenate(rows, axis=1)


def _shard_views(name, full):
    if name == 'w_out':
        return full.reshape(N_DEV, full.shape[0] // N_DEV, full.shape[1])
    r, ccols = full.shape
    return full.reshape(r, N_DEV, ccols // N_DEV).transpose(1, 0, 2)


def _from_shards(name, stacked):
    if name == 'w_out':
        return stacked.reshape(-1, stacked.shape[-1])
    n, r, cc = stacked.shape
    return stacked.transpose(1, 0, 2).reshape(r, n * cc)


GROUPS_PER_BLK = N_GROUPS // N_COL_BLK


def _block_diag(t):
    eye = jnp.eye(GROUPS_PER_BLK, dtype=t.dtype)
    t4 = t.reshape(N_COL_BLK, GROUPS_PER_BLK, SSM_GROUP, SSM_STATE)
    return (t4[:, :, :, None, :] * eye[None, :, None, :, None]).reshape(N_COL_BLK, CH_BLK, COL_BLK)


def _diag_blocks(mat4):
    eye = jnp.eye(GROUPS_PER_BLK, dtype=mat4.dtype)
    m6 = mat4.reshape(N_COL_BLK, GROUPS_PER_BLK, SSM_GROUP, GROUPS_PER_BLK, SSM_STATE)
    return (m6 * eye[None, :, None, :, None]).sum(axis=3).reshape(N_GROUPS, SSM_GROUP, SSM_STATE)


def _step(x, loss_target, wts, moms, vels):
    seq = x.shape[1]
    n_valid = N_META + seq
    lp = -(-n_valid // 256) * 256
    tr = _pick(lp, [640, 256])
    tr_mid = 256
    tq = _pick(lp, [1280, 256])
    tk = _pick(lp, [640, 256])

    shard_shapes = [wts[n].shape[-2:] for n in SHARDED]
    n_shard_rows = sum(_n_rows(s) for s in shard_shapes)
    gathered = _exchange("gather_weights", None,
                         _pack([wts[n].reshape(wts[n].shape[-2:]) for n in SHARDED], n_shard_rows))
    full = {n: _from_shards(n, a) for n, a in zip(SHARDED, _unpack(gathered, shard_shapes))}

    w_in_b = jnp.concatenate([_cols_in(full['w_in']), jnp.zeros((D_MODEL, D_IN_PAD - D_IN), F32)],
                             axis=1).astype(BF16)
    wq_b = _cols_q(full['w_q_up']).astype(BF16)
    wkv_b = _cols_kv(full['w_kv_up']).astype(BF16)
    wglu_b = full['w_glu'].astype(BF16)
    wo_b = full['w_out'].astype(BF16)
    pre_w, post_w = wts['pre_norm_w'], wts['post_norm_w']
    qw, kvw, aw, sw = wts['q_norm_w'], wts['kv_norm_w'], wts['attn_out_norm_w'], wts['ssm_out_norm_w']
    bglu, dvec = wts['b_glu'], wts['ssm_d']

    lseg = lp // N_SEG
    pos = _row_position(jnp.arange(lp, dtype=jnp.int32), lseg)
    inv = ROPE_THETA ** (-jnp.arange(HALF_ROPE, dtype=F32) / HALF_ROPE)
    ang = pos.astype(F32)[:, None] * inv[None, :]
    cos, sin = jnp.cos(ang), jnp.sin(ang)
    cos8, sin8 = jnp.tile(cos, (1, HEADS)), jnp.tile(sin, (1, HEADS))
    c32 = jnp.concatenate([cos, cos], axis=1)
    s32 = jnp.concatenate([-sin, sin], axis=1)
    p32 = jnp.asarray(np.roll(np.eye(QK_ROPE, dtype=np.float32), HALF_ROPE, axis=1))
    sum8 = jnp.asarray(np.tile(np.eye(QK_ROPE, dtype=np.float32), (HEADS, 1)))
    head_sum = jnp.asarray(np.repeat(np.eye(HEADS, dtype=np.float32), V_HEAD, axis=0))

    ng = 2 * N_GROUPS
    a_re3 = wts['ssm_a_re'].reshape(ng, 1, SSM_STATE)
    a_im3 = wts['ssm_a_im'].reshape(ng, 1, SSM_STATE)
    ldt3 = wts['ssm_log_dt'].reshape(ng, 1, 1)
    bt_re = wts['ssm_b_re'].reshape(2, N_GROUPS, SSM_STATE, SSM_GROUP).transpose(0, 1, 3, 2).reshape(
        ng, SSM_GROUP, SSM_STATE)
    bt_im = wts['ssm_b_im'].reshape(2, N_GROUPS, SSM_STATE, SSM_GROUP).transpose(0, 1, 3, 2).reshape(
        ng, SSM_GROUP, SSM_STATE)
    c_re = wts['ssm_c_re'].reshape(ng, SSM_GROUP, SSM_STATE)
    c_im = wts['ssm_c_im'].reshape(ng, SSM_GROUP, SSM_STATE)
    abar_re, abar_im, bbt_re, bbt_im = _ssm_disc(a_re3, a_im3, ldt3, bt_re, bt_im)

    def direction(t, d):
        return t[d * N_GROUPS:(d + 1) * N_GROUPS]

    def slab(t, d, sign=1.0):
        return jnp.broadcast_to(sign * direction(t, d).reshape(1, N_STATES), (N_SEG, N_STATES))

    w_re = [_block_diag(direction(bbt_re, d)).astype(BF16) for d in range(2)]
    w_im = [_block_diag(direction(bbt_im, d)).astype(BF16) for d in range(2)]
    cb_re = [_block_diag(direction(c_re, d)).astype(BF16) for d in range(2)]
    cb_im = [_block_diag(-direction(c_im, d)).astype(BF16) for d in range(2)]

    def to_rows(a):
        return a.reshape(N_SEG, lseg, a.shape[-1]).transpose(1, 0, 2).reshape(lp, a.shape[-1])

    def to_tokens(a):
        return a.reshape(lseg, N_SEG, a.shape[-1]).transpose(1, 0, 2).reshape(lp, a.shape[-1])

    pad = jnp.zeros((lp - n_valid, D_MODEL), F32)
    h = to_rows(jnp.concatenate([full['meta_tokens'], x[0], pad], axis=0))
    tgt = to_rows(jnp.concatenate([jnp.zeros((N_META, D_MODEL), F32), loss_target[0], pad], axis=0))

    ql, kvl, ag, su, sg, kr = _inproj(h, pre_w, w_in_b, tr)
    qn_b, qr1_b, qr2_b, kn_b, v_b, kr_b = _qkv_up(ql, kvl, kr, cos8, sin8, c32, s32, qw, kvw, wq_b, wkv_b, p32, tr)

    def heads(a, w):
        return a.reshape(lp, HEADS, w)

    nq, nk = lp // tq, lp // tk
    q_t = jnp.concatenate([heads(qn_b, 64), heads(qr1_b, 16), heads(qr2_b, 16)], axis=-1)
    k_t = jnp.concatenate([heads(kn_b, 64), jnp.broadcast_to(kr_b[:, None, :], (lp, HEADS, QK_ROPE))], axis=-1)
    v_t = heads(v_b, 64)
    vx_t = jnp.concatenate([v_t, jnp.ones((lp, HEADS, 1), BF16), jnp.zeros((lp, HEADS, LANES - V_HEAD - 1), BF16)],
                           axis=-1)
    qt4 = q_t.reshape(nq, tq, HEADS, QK_DIM).transpose(2, 0, 3, 1)
    tk_fwd = tk
    vxt4 = vx_t.reshape(lp // tk_fwd, tk_fwd, HEADS, LANES).transpose(2, 0, 3, 1)
    k_h = k_t.transpose(1, 0, 2)
    kt_h = k_t.transpose(1, 2, 0)
    v_h = v_t.transpose(1, 0, 2)
    ot_h, lse4, m_chunks, p_chunks = _attn_fwd(qt4, k_h, vxt4, n_valid)
    o_flat = ot_h.transpose(2, 0, 1).reshape(lp, D_ATTN)
    ya = _attn_post(o_flat, ag, aw, tr)

    xs, ys = [], []
    for d in range(2):
        ar8, ai8 = slab(abar_re, d), slab(abar_im, d)
        ere, eim = _scan_ends(f"scan{d}_ends", su, w_re[d], w_im[d], ar8, ai8, d == 0)
        x_re, x_im, y_d = _scan_fwd(f"scan{d}", su, w_re[d], w_im[d], ar8, ai8, ere, eim, cb_re[d], cb_im[d],
                                    d == 0)
        xs += [x_re, x_im]
        ys.append(y_d)
    ypre, glu, ysn = _ssm_post(ys[0], ys[1], su, sg, wglu_b, bglu, sw, dvec, tr)

    dy, dout, loss, d_post = _out_loss(ya, ysn, h, tgt, wo_b, post_w, n_valid, tr)

    do_flat, dag, dysn, delta8, d_wo, d_aw = _out_bwd(dy, ya, ysn, o_flat, ag, wo_b, aw, head_sum, tr)
    dypre, dsg, d_wglu, d_bglu, d_sw, d_dvec = _ssm_post_bwd(dysn, glu, sg, ypre, su, wglu_b, sw, dvec, tr)

    dus, d_ct, d_wb, d_a8 = [], [], [], []
    for d in range(2):
        ar8, ai8c = slab(abar_re, d), slab(abar_im, d, -1.0)
        ere, eim = _scan_ends(f"scan_adj{d}_ends", dypre, cb_re[d], cb_im[d], ar8, ai8c, d != 0)
        du_d, dw_re, dw_im, dc_re, dc_im, da_re, da_im = _scan_bwd(
            f"scan_adj{d}", dypre, cb_re[d], cb_im[d], ar8, ai8c, ere, eim, su, w_re[d], w_im[d],
            xs[2 * d], xs[2 * d + 1], d != 0)
        dus.append(du_d)
        d_ct.append((dc_re, dc_im))
        d_wb.append((dw_re, dw_im))
        d_a8.append((da_re, da_im))

    dot4 = do_flat.reshape(nq, tq, HEADS, V_HEAD).transpose(2, 0, 3, 1)
    dqt4, dkt_h, dvt_h = _attn_bwd(qt4, kt_h, v_h, dot4, lse4, delta8.T.reshape(HEADS, nq, 1, tq), m_chunks,
                                   p_chunks, tk)
    dq_t = dqt4.transpose(1, 3, 0, 2).reshape(lp, HEADS, QK_DIM)
    dk_t = dkt_h.transpose(2, 0, 1)
    dqn = dq_t[:, :, :64].reshape(lp, 512)
    dr1 = dq_t[:, :, 64:80].reshape(lp, 128)
    dr2 = dq_t[:, :, 80:96].reshape(lp, 128)
    dkn = dk_t[:, :, :64].reshape(lp, 512)
    dkr8 = dk_t[:, :, 64:].reshape(lp, HEADS * QK_ROPE)
    dvf = dvt_h.transpose(2, 0, 1).reshape(lp, 512)
    dql, dkvl, dkrr, d_wq, d_wkv, d_qw, d_kvw = _qkv_up_bwd(
        dqn, dr1, dr2, dkn, dvf, dkr8, ql, kvl, cos8, sin8, c32, s32, qw, kvw, wq_b, wkv_b, p32, sum8, tr)
    dh, d_win, d_pre = _inproj_bwd(dql, dkvl, dag, dus[0], dus[1], dypre, dsg, dkrr, h, dout, pre_w, w_in_b, dvec,
                                   tr_mid)
    dh = to_tokens(dh)

    def seg_sums(t):
        return t.reshape(N_SEG, N_GROUPS, SSM_STATE).transpose(1, 0, 2)

    da8_re = jnp.concatenate([seg_sums(d_a8[d][0]) for d in range(2)], axis=0)
    da8_im = jnp.concatenate([seg_sums(d_a8[d][1]) for d in range(2)], axis=0)
    dbb_re = jnp.concatenate([_diag_blocks(d_wb[d][0]) for d in range(2)], axis=0)
    dbb_im = jnp.concatenate([_diag_blocks(d_wb[d][1]) for d in range(2)], axis=0)
    g_are, g_aim, g_ldt, g_bt_re, g_bt_im = _ssm_disc_bwd(a_re3, a_im3, ldt3, bt_re, bt_im, da8_re, da8_im,
                                                          dbb_re, dbb_im)
    g_c_re = jnp.concatenate([_diag_blocks(d_ct[d][0]) for d in range(2)], axis=0)
    g_c_im = jnp.concatenate([-_diag_blocks(d_ct[d][1]) for d in range(2)], axis=0)

    def b_layout(t):
        return t.reshape(2, N_GROUPS, SSM_GROUP, SSM_STATE).transpose(0, 1, 3, 2)

    local = {
        'meta_tokens': dh[:N_META],
        'pre_norm_w': d_pre, 'post_norm_w': d_post,
        'w_in': _cols_in_inv(d_win[:, :D_IN]),
        'q_norm_w': d_qw, 'w_q_up': _cols_q_inv(d_wq),
        'kv_norm_w': d_kvw, 'w_kv_up': _cols_kv_inv(d_wkv),
        'attn_out_norm_w': d_aw,
        'ssm_a_re': g_are, 'ssm_a_im': g_aim, 'ssm_log_dt': g_ldt,
        'ssm_b_re': b_layout(g_bt_re), 'ssm_b_im': b_layout(g_bt_im), 'ssm_c_re': g_c_re, 'ssm_c_im': g_c_im,
        'ssm_d': d_dvec, 'w_glu': d_wglu, 'b_glu': d_bglu, 'ssm_out_norm_w': d_sw, 'w_out': d_wo,
    }

    replicated = [n for n in WEIGHTS if n not in SHARDED]
    order = SHARDED + replicated
    shapes = [wts[n].shape for n in order] + [(1, 1)]
    tr_adam = 512
    total_rows = -(-sum(_n_rows(s) for s in shapes) // tr_adam) * tr_adam
    recv = _exchange("exchange_grads",
                     _pack_per_device([_shard_views(n, local[n]) for n in SHARDED], n_shard_rows),
                     _pack([local[n] for n in replicated] + [loss], total_rows - n_shard_rows))
    zero = jnp.zeros((1, 1), F32)
    packed = [_pack([src[n] for n in order] + [zero], total_rows) for src in (wts, moms, vels)]
    g_p, d_p, m_p, v_p = _adamw(recv, *packed, tr_adam)
    sums = _unpack(g_p, shapes)
    grads = dict(zip(order, sums))
    deltas, new_m, new_v = (dict(zip(order, _unpack(b, shapes))) for b in (d_p, m_p, v_p))

    grad_x = dh[N_META:n_valid][None]
    return (sums[-1][0, 0], grad_x, *[grads[n] for n in WEIGHTS], *[deltas[n] for n in WEIGHTS],
            *[new_m[n] for n in WEIGHTS], *[new_v[n] for n in WEIGHTS])


def kernel(x, meta_tokens, pre_norm_w, post_norm_w, w_in, q_norm_w, w_q_up, kv_norm_w, w_kv_up, attn_out_norm_w, ssm_a_re, ssm_a_im, ssm_log_dt, ssm_b_re, ssm_b_im, ssm_c_re, ssm_c_im, ssm_d, w_glu, b_glu, ssm_out_norm_w, w_out, loss_target, m_meta_tokens, m_pre_norm_w, m_post_norm_w, m_w_in, m_q_norm_w, m_w_q_up, m_kv_norm_w, m_w_kv_up, m_attn_out_norm_w, m_ssm_a_re, m_ssm_a_im, m_ssm_log_dt, m_ssm_b_re, m_ssm_b_im, m_ssm_c_re, m_ssm_c_im, m_ssm_d, m_w_glu, m_b_glu, m_ssm_out_norm_w, m_w_out, v_meta_tokens, v_pre_norm_w, v_post_norm_w, v_w_in, v_q_norm_w, v_w_q_up, v_kv_norm_w, v_w_kv_up, v_attn_out_norm_w, v_ssm_a_re, v_ssm_a_im, v_ssm_log_dt, v_ssm_b_re, v_ssm_b_im, v_ssm_c_re, v_ssm_c_im, v_ssm_d, v_w_glu, v_b_glu, v_ssm_out_norm_w, v_w_out):
    wts = dict(zip(WEIGHTS, (meta_tokens, pre_norm_w, post_norm_w, w_in, q_norm_w, w_q_up, kv_norm_w, w_kv_up,
                             attn_out_norm_w, ssm_a_re, ssm_a_im, ssm_log_dt, ssm_b_re, ssm_b_im, ssm_c_re,
                             ssm_c_im, ssm_d, w_glu, b_glu, ssm_out_norm_w, w_out)))
    moms = dict(zip(WEIGHTS, (m_meta_tokens, m_pre_norm_w, m_post_norm_w, m_w_in, m_q_norm_w, m_w_q_up,
                              m_kv_norm_w, m_w_kv_up, m_attn_out_norm_w, m_ssm_a_re, m_ssm_a_im, m_ssm_log_dt,
                              m_ssm_b_re, m_ssm_b_im, m_ssm_c_re, m_ssm_c_im, m_ssm_d, m_w_glu, m_b_glu,
                              m_ssm_out_norm_w, m_w_out)))
    vels = dict(zip(WEIGHTS, (v_meta_tokens, v_pre_norm_w, v_post_norm_w, v_w_in, v_q_norm_w, v_w_q_up,
                              v_kv_norm_w, v_w_kv_up, v_attn_out_norm_w, v_ssm_a_re, v_ssm_a_im, v_ssm_log_dt,
                              v_ssm_b_re, v_ssm_b_im, v_ssm_c_re, v_ssm_c_im, v_ssm_d, v_w_glu, v_b_glu,
                              v_ssm_out_norm_w, v_w_out)))
    return _step(x, loss_target, wts, moms, vels)
```

```python
import functools
import math

import numpy as np
import jax
import jax.numpy as jnp
from jax import lax
from jax.experimental import pallas as pl
from jax.experimental.pallas import tpu as pltpu

F32 = jnp.float32
BF16 = jnp.bfloat16

D_MODEL = 1024
N_META = 16
EPS = 1e-6
HEADS = 8
QK_NOPE = 64
QK_ROPE = 32
HALF_ROPE = QK_ROPE // 2
QK_DIM = QK_NOPE + QK_ROPE
V_HEAD = 64
Q_LORA = 256
KV_LORA = 128
D_ATTN = HEADS * V_HEAD
D_SSM = 512
SSM_GROUP = 16
N_GROUPS = D_SSM // SSM_GROUP
SSM_STATE = 64
N_STATES = N_GROUPS * SSM_STATE
ROPE_THETA = 10000.0
D_IN = Q_LORA + KV_LORA + QK_ROPE + D_ATTN + 2 * D_SSM
D_IN_PAD = 2048
N_DEV = 8
N_SEG = 8
COL_BLK = 512
LANES = 128

ADAM_LR = 0.001
ADAM_B1 = 0.9
ADAM_B2 = 0.999
ADAM_EPS = 1e-08
ADAM_WD = 0.01
ADAM_STEP = 10

VMEM_LIMIT_V7X = 56 * 1024 * 1024
LOG2E = 1.0 / math.log(2.0)
Q_SCALE = LOG2E / math.sqrt(QK_DIM)
ATTN_UNROLL = 4
ATTN_BWD_UNROLL = 4

WEIGHTS = ['meta_tokens', 'pre_norm_w', 'post_norm_w', 'w_in', 'q_norm_w', 'w_q_up', 'kv_norm_w', 'w_kv_up',
           'attn_out_norm_w', 'ssm_a_re', 'ssm_a_im', 'ssm_log_dt', 'ssm_b_re', 'ssm_b_im', 'ssm_c_re', 'ssm_c_im',
           'ssm_d', 'w_glu', 'b_glu', 'ssm_out_norm_w', 'w_out']
SHARDED = ['w_in', 'w_q_up', 'w_kv_up', 'w_glu', 'w_out', 'meta_tokens']

def _cols_in(w):
    return jnp.concatenate([w[:, 0:384], w[:, 416:D_IN], w[:, 384:416]], axis=1)


def _cols_in_inv(w):
    return jnp.concatenate([w[:, 0:384], w[:, D_IN - QK_ROPE:D_IN], w[:, 384:D_IN - QK_ROPE]], axis=1)


def _cols_q(w):
    t = w.reshape(w.shape[0], HEADS, QK_DIM)
    return jnp.concatenate([t[:, :, 0:64].reshape(-1, 512), t[:, :, 64:80].reshape(-1, 128),
                            t[:, :, 80:96].reshape(-1, 128)], axis=1)


def _cols_q_inv(w):
    r = w.shape[0]
    return jnp.concatenate([w[:, 0:512].reshape(r, HEADS, 64), w[:, 512:640].reshape(r, HEADS, 16),
                            w[:, 640:768].reshape(r, HEADS, 16)], axis=2).reshape(r, HEADS * QK_DIM)


def _cols_kv(w):
    t = w.reshape(w.shape[0], HEADS, 128)
    return jnp.concatenate([t[:, :, 0:64].reshape(-1, 512), t[:, :, 64:128].reshape(-1, 512)], axis=1)


def _cols_kv_inv(w):
    r = w.shape[0]
    return jnp.concatenate([w[:, 0:512].reshape(r, HEADS, 64), w[:, 512:1024].reshape(r, HEADS, 64)],
                           axis=2).reshape(r, HEADS * 128)


def _pick(n, cands):
    for c in cands:
        if n % c == 0:
            return c
    raise ValueError(f"no tile for {n}")


def _cparams(*sem):
    return pltpu.CompilerParams(dimension_semantics=sem, vmem_limit_bytes=VMEM_LIMIT_V7X)


def _mm(a, b):
    return jnp.dot(a.astype(BF16), b.astype(BF16), preferred_element_type=F32)


def _mm_nt(a, b):
    return lax.dot_general(a.astype(BF16), b.astype(BF16), (((1,), (1,)), ((), ())), preferred_element_type=F32)


def _mm_tn(a, b):
    return lax.dot_general(a.astype(BF16), b.astype(BF16), (((0,), (0,)), ((), ())), preferred_element_type=F32)


def _mm_exact(a, b):
    return jnp.dot(a, b, precision=lax.Precision.HIGHEST, preferred_element_type=F32)


def _rms(x):
    return lax.rsqrt(jnp.mean(x * x, axis=-1, keepdims=True) + EPS)


def _rms_bwd(dy, x, r, w):
    xh = x * r
    g = dy * w
    dx = r * (g - xh * jnp.mean(g * xh, axis=-1, keepdims=True))
    dw = jnp.sum(dy * xh, axis=0, keepdims=True)
    return dx, dw


def _sigmoid(z):
    return 1.0 / (1.0 + jnp.exp(-z))


def _silu_and_grad(z):
    s = _sigmoid(z)
    return z * s, s * (1.0 + z * (1.0 - s))


_GELU_C = math.sqrt(2.0 / math.pi)


def _gelu_and_grad(x):
    x2 = x * x
    t = jnp.tanh(_GELU_C * (x + 0.044715 * x * x2))
    val = 0.5 * x * (1.0 + t)
    grad = 0.5 * (1.0 + t) + 0.5 * x * (1.0 - t * t) * _GELU_C * (1.0 + 3.0 * 0.044715 * x2)
    return val, grad


def _acc(ref, val, first):
    @pl.when(first)
    def _():
        ref[...] = val

    @pl.when(jnp.logical_not(first))
    def _():
        ref[...] += val


def _rows_call(name, body, tr, row_ins, full_ins, row_outs, acc_outs):
    lp = row_ins[0].shape[0]
    in_specs = [pl.BlockSpec((tr, a.shape[1]), lambda i: (i, 0)) for a in row_ins]
    in_specs += [pl.BlockSpec(a.shape, lambda i, n=a.ndim: (0,) * n) for a in full_ins]
    out_specs = [pl.BlockSpec((tr, c), lambda i: (i, 0)) for c, _ in row_outs]
    out_specs += [pl.BlockSpec(s, lambda i, n=len(s): (0,) * n) for s, _ in acc_outs]
    out_shape = [jax.ShapeDtypeStruct((lp, c), dt) for c, dt in row_outs]
    out_shape += [jax.ShapeDtypeStruct(s, dt) for s, dt in acc_outs]
    return pl.pallas_call(
        body, name=name, grid=(lp // tr,), in_specs=in_specs, out_specs=out_specs, out_shape=out_shape,
        compiler_params=_cparams("arbitrary"))(*row_ins, *full_ins)


def _inproj(h, pre_w, w_in_b, tr):
    def body(h_ref, pw_ref, w_ref, ql, kvl, ag, su, sg, kr):
        x = h_ref[...]
        xn = x * _rms(x) * pw_ref[...]
        pr = _mm(xn, w_ref[...])
        ql[...] = pr[:, 0:256]
        kvl[...] = pr[:, 256:384]
        ag[...] = pr[:, 384:896]
        su[...] = pr[:, 896:1408]
        sg[...] = pr[:, 1408:1920]
        kr[...] = pr[:, 1920:1952]

    return _rows_call("inproj", body, tr, [h], [pre_w, w_in_b],
                      [(256, F32), (128, F32), (512, F32), (512, F32), (512, F32), (32, F32)], [])


def _qkv_up(ql, kvl, kr, cos8, sin8, c32, s32, qw, kvw, wq_b, wkv_b, p32, tr):
    def body(ql_ref, kvl_ref, kr_ref, cos_ref, sin_ref, c32_ref, s32_ref, qw_ref, kvw_ref, wq_ref, wkv_ref, p_ref,
             qn_o, qr1_o, qr2_o, kn_o, v_o, kr_o):
        x = ql_ref[...]
        q = _mm(x * _rms(x) * qw_ref[...], wq_ref[...]) * Q_SCALE
        r1, r2 = q[:, 512:640], q[:, 640:768]
        cs, sn = cos_ref[...], sin_ref[...]
        qn_o[...] = q[:, 0:512].astype(BF16)
        qr1_o[...] = (r1 * cs - r2 * sn).astype(BF16)
        qr2_o[...] = (r2 * cs + r1 * sn).astype(BF16)
        x = kvl_ref[...]
        kv = _mm(x * _rms(x) * kvw_ref[...], wkv_ref[...])
        kn_o[...] = kv[:, 0:512].astype(BF16)
        v_o[...] = kv[:, 512:1024].astype(BF16)
        x = kr_ref[...]
        kr_o[...] = (x * c32_ref[...] + _mm_exact(x, p_ref[...]) * s32_ref[...]).astype(BF16)

    return _rows_call("qkv_up", body, tr, [ql, kvl, kr, cos8, sin8, c32, s32], [qw, kvw, wq_b, wkv_b, p32],
                      [(512, BF16), (128, BF16), (128, BF16), (512, BF16), (512, BF16), (32, BF16)], [])


def _row_position(row, lseg):
    return (row & (N_SEG - 1)) * lseg + (row >> 3)


def _first_padded_tile(n_valid, lp, tile):
    lseg = lp // N_SEG
    t0 = n_valid - (N_SEG - 1) * lseg
    return (t0 * N_SEG + N_SEG - 1) // tile if n_valid < lp else lp // tile


def _attn_fwd(qt, k, vxt, n_valid):
    _, nq, _, tq = qt.shape
    _, nk, _, tk = vxt.shape
    lp = k.shape[1]
    lseg = lp // N_SEG
    n_plain = max(0, min(nk, _first_padded_tile(n_valid, lp, tk)))

    def body(q_ref, k_ref, v_ref, o_ref, lse_ref, m_s, acc_s):
        m_s[...] = jnp.full(m_s.shape, -1e30, F32)
        acc_s[...] = jnp.zeros(acc_s.shape, F32)
        qq = q_ref[0, 0]

        def chunk(c, padded):
            r0 = pl.multiple_of(c * tk, tk)
            st = _mm(k_ref[0, pl.ds(r0, tk), :], qq)
            if padded:
                row = r0 + lax.broadcasted_iota(jnp.int32, (tk, tq), 0)
                st = jnp.where(_row_position(row, lseg) < n_valid, st, -1e30)
            m_old = m_s[...]
            m_new = jnp.maximum(m_old, jnp.max(st, axis=0, keepdims=True))
            pt = jnp.exp2(st - m_new)
            acc_s[...] = jnp.exp2(m_old - m_new) * acc_s[...] + _mm(v_ref[0, c], pt)
            m_s[...] = m_new

        def plain(c, carry):
            chunk(c, False)
            return carry

        n_loop = n_plain - n_plain % ATTN_UNROLL
        if n_loop:
            lax.fori_loop(0, n_loop, plain, 0, unroll=ATTN_UNROLL)
        for c in range(n_loop, nk):
            chunk(c, c >= n_plain)
        acc = acc_s[...]
        l = acc[V_HEAD:V_HEAD + 1, :]
        o_ref[0] = acc[:V_HEAD, :] / l
        lse_ref[0, 0] = m_s[...] + jnp.log2(l)

    return pl.pallas_call(
        body, name="attn_fwd", grid=(HEADS, nq),
        in_specs=[pl.BlockSpec((1, 1, QK_DIM, tq), lambda h, i: (h, i, 0, 0)),
                  pl.BlockSpec((1, lp, QK_DIM), lambda h, i: (h, 0, 0)),
                  pl.BlockSpec((1, nk, LANES, tk), lambda h, i: (h, 0, 0, 0))],
        out_specs=[pl.BlockSpec((1, V_HEAD, tq), lambda h, i: (h, 0, i)),
                   pl.BlockSpec((1, 1, 1, tq), lambda h, i: (h, i, 0, 0))],
        out_shape=[jax.ShapeDtypeStruct((HEADS, V_HEAD, lp), F32), jax.ShapeDtypeStruct((HEADS, nq, 1, tq), F32)],
        scratch_shapes=[pltpu.VMEM((1, tq), F32), pltpu.VMEM((LANES, tq), F32)],
        compiler_params=_cparams("arbitrary", "arbitrary"))(qt, k, vxt)


def _attn_post(o_flat, ag, aw, tr):
    def body(o_ref, g_ref, w_ref, ya):
        t = o_ref[...] * _silu_and_grad(g_ref[...])[0]
        ya[...] = t * _rms(t) * w_ref[...]

    return _rows_call("attn_post", body, tr, [o_flat, ag], [aw], [(512, F32)], [])[0]


def _scan_tiles(lp):
    lseg = lp // N_SEG
    tt = _pick(lseg, [208, 48, 32, 16, 8, 4, 2, 1])
    return lseg, tt, lseg // tt


def _cmul(ar, ai, br, bi):
    return ar * br - ai * bi, ar * bi + ai * br


N_COL_BLK = N_STATES // COL_BLK
CH_BLK = D_SSM // N_COL_BLK


def _scan_steps(tt, forward, bre_ref, bim_ref, ar, ai, carry, visit):
    def step(s, c):
        r0 = pl.multiple_of((s if forward else tt - 1 - s) * N_SEG, N_SEG)
        pr, pi = _cmul(ar, ai, c[0], c[1])
        xr = pr + bre_ref[pl.ds(r0, N_SEG), :]
        xi = pi + bim_ref[pl.ds(r0, N_SEG), :]
        return (xr, xi) + tuple(visit(r0, (xr, xi), (c[0], c[1]), c[2:]))

    return lax.fori_loop(0, tt, step, carry, unroll=4 if tt % 4 == 0 else 1)


def _segment_starts(lseg, forward, ar, ai, ere_ref, eim_ref, s_re, s_im):
    a1r, a1i = ar[0:1, :], ai[0:1, :]
    pr, pi = jnp.ones_like(a1r), jnp.zeros_like(a1i)
    br, bi = a1r, a1i
    n = lseg
    while n:
        if n & 1:
            pr, pi = _cmul(pr, pi, br, bi)
        n >>= 1
        if n:
            br, bi = _cmul(br, bi, br, bi)
    cr, ci = jnp.zeros_like(a1r), jnp.zeros_like(a1i)
    for j in (range(N_SEG) if forward else range(N_SEG - 1, -1, -1)):
        s_re[j:j + 1, :] = cr
        s_im[j:j + 1, :] = ci
        nr, ni = _cmul(pr, pi, cr, ci)
        cr = nr + ere_ref[j:j + 1, :]
        ci = ni + eim_ref[j:j + 1, :]


def _scan_specs(lp, forward):
    lseg, tt, nt = _scan_tiles(lp)

    def tile(t):
        return t if forward else nt - 1 - t

    rows = lambda w: pl.BlockSpec((tt * N_SEG, w), lambda cb, t: (tile(t), cb))
    proj = pl.BlockSpec((1, CH_BLK, COL_BLK), lambda cb, t: (cb, 0, 0))
    slab = pl.BlockSpec((N_SEG, COL_BLK), lambda cb, t: (0, cb))
    return lseg, tt, nt, rows, proj, slab


def _scan_ends(name, urows, wre4, wim4, ar8, ai8, forward):
    lp = urows.shape[0]
    lseg, tt, nt, rows, proj, slab = _scan_specs(lp, forward)

    def body(u_ref, wre_ref, wim_ref, ar_ref, ai_ref, ere_o, eim_o, bre_s, bim_s, cr_s, ci_s):
        t = pl.program_id(1)

        @pl.when(t == 0)
        def _():
            cr_s[...] = jnp.zeros(cr_s.shape, F32)
            ci_s[...] = jnp.zeros(ci_s.shape, F32)

        u = u_ref[...]
        bre_s[...] = _mm(u, wre_ref[0])
        bim_s[...] = _mm(u, wim_ref[0])
        cr, ci = _scan_steps(tt, forward, bre_s, bim_s, ar_ref[...], ai_ref[...], (cr_s[...], ci_s[...]),
                             lambda r0, x, x_prev, extra: ())
        cr_s[...] = cr
        ci_s[...] = ci

        @pl.when(t == nt - 1)
        def _():
            ere_o[...] = cr
            eim_o[...] = ci

    return pl.pallas_call(
        body, name=name, grid=(N_COL_BLK, nt), in_specs=[rows(CH_BLK), proj, proj, slab, slab],
        out_specs=[slab, slab], out_shape=[jax.ShapeDtypeStruct((N_SEG, N_STATES), F32)] * 2,
        scratch_shapes=[pltpu.VMEM((tt * N_SEG, COL_BLK), F32)] * 2 + [pltpu.VMEM((N_SEG, COL_BLK), F32)] * 2,
        compiler_params=_cparams("arbitrary", "arbitrary"))(urows, wre4, wim4, ar8, ai8)


def _scan_fwd(name, urows, wre4, wim4, ar8, ai8, ere, eim, cre4, cim4, forward):
    lp = urows.shape[0]
    lseg, tt, nt, rows, proj, slab = _scan_specs(lp, forward)

    def body(u_ref, wre_ref, wim_ref, ar_ref, ai_ref, ere_ref, eim_ref, cre_ref, cim_ref,
             xre_o, xim_o, y_o, bre_s, bim_s, cr_s, ci_s):
        ar, ai = ar_ref[...], ai_ref[...]

        @pl.when(pl.program_id(1) == 0)
        def _():
            _segment_starts(lseg, forward, ar, ai, ere_ref, eim_ref, cr_s, ci_s)

        u = u_ref[...]
        bre_s[...] = _mm(u, wre_ref[0])
        bim_s[...] = _mm(u, wim_ref[0])

        def visit(r0, x, x_prev, extra):
            xre_o[pl.ds(r0, N_SEG), :] = x[0]
            xim_o[pl.ds(r0, N_SEG), :] = x[1]
            return ()

        cr, ci = _scan_steps(tt, forward, bre_s, bim_s, ar, ai, (cr_s[...], ci_s[...]), visit)
        cr_s[...] = cr
        ci_s[...] = ci
        y_o[...] = _mm_nt(xre_o[...], cre_ref[0]) + _mm_nt(xim_o[...], cim_ref[0])

    return pl.pallas_call(
        body, name=name, grid=(N_COL_BLK, nt),
        in_specs=[rows(CH_BLK), proj, proj, slab, slab, slab, slab, proj, proj],
        out_specs=[rows(COL_BLK), rows(COL_BLK), rows(CH_BLK)],
        out_shape=[jax.ShapeDtypeStruct((lp, N_STATES), F32)] * 2 + [jax.ShapeDtypeStruct((lp, D_SSM), F32)],
        scratch_shapes=[pltpu.VMEM((tt * N_SEG, COL_BLK), F32)] * 2 + [pltpu.VMEM((N_SEG, COL_BLK), F32)] * 2,
        compiler_params=_cparams("arbitrary", "arbitrary"))(urows, wre4, wim4, ar8, ai8, ere, eim, cre4, cim4)


def _scan_bwd(name, dyrows, cre4, cim4, ar8, ai8, ere, eim, urows, wre4, wim4, xre, xim, forward):
    lp = urows.shape[0]
    lseg, tt, nt, rows, proj, slab = _scan_specs(lp, forward)

    def body(dy_ref, cre_ref, cim_ref, ar_ref, ai_ref, ere_ref, eim_ref, u_ref, wre_ref, wim_ref, xre_ref, xim_ref,
             du_o, dwre_o, dwim_o, dcre_o, dcim_o, dare_o, daim_o, bre_s, bim_s, gre_s, gim_s, cr_s, ci_s):
        t = pl.program_id(1)
        ar, ai = ar_ref[...], ai_ref[...]

        @pl.when(t == 0)
        def _():
            _segment_starts(lseg, forward, ar, ai, ere_ref, eim_ref, cr_s, ci_s)
            dare_o[...] = jnp.zeros(dare_o.shape, F32)
            daim_o[...] = jnp.zeros(daim_o.shape, F32)

        dy = dy_ref[...]
        bre_s[...] = _mm(dy, cre_ref[0])
        bim_s[...] = _mm(dy, cim_ref[0])

        def visit(r0, g, g_prev, sums):
            gre_s[pl.ds(r0, N_SEG), :] = g[0]
            gim_s[pl.ds(r0, N_SEG), :] = g[1]
            fr = xre_ref[pl.ds(r0, N_SEG), :]
            fi = xim_ref[pl.ds(r0, N_SEG), :]
            pr, pi = g_prev
            return sums[0] + fr * pr + fi * pi, sums[1] + fr * pi - fi * pr

        out = _scan_steps(tt, forward, bre_s, bim_s, ar, ai, (cr_s[...], ci_s[...], dare_o[...], daim_o[...]), visit)
        cr_s[...] = out[0]
        ci_s[...] = out[1]
        dare_o[...] = out[2]
        daim_o[...] = out[3]
        gre, gim = gre_s[...], gim_s[...]
        du_o[...] = _mm_nt(gre, wre_ref[0]) + _mm_nt(gim, wim_ref[0])
        u = u_ref[...]
        first = t == 0
        _acc(dwre_o, _mm_tn(u, gre)[None], first)
        _acc(dwim_o, _mm_tn(u, gim)[None], first)
        _acc(dcre_o, _mm_tn(dy, xre_ref[...])[None], first)
        _acc(dcim_o, _mm_tn(dy, xim_ref[...])[None], first)

    big = pltpu.VMEM((tt * N_SEG, COL_BLK), F32)
    small = pltpu.VMEM((N_SEG, COL_BLK), F32)
    return pl.pallas_call(
        body, name=name, grid=(N_COL_BLK, nt),
        in_specs=[rows(CH_BLK), proj, proj, slab, slab, slab, slab, rows(CH_BLK), proj, proj,
                  rows(COL_BLK), rows(COL_BLK)],
        out_specs=[rows(CH_BLK), proj, proj, proj, proj, slab, slab],
        out_shape=[jax.ShapeDtypeStruct((lp, D_SSM), F32)]
        + [jax.ShapeDtypeStruct((N_COL_BLK, CH_BLK, COL_BLK), F32)] * 4
        + [jax.ShapeDtypeStruct((N_SEG, N_STATES), F32)] * 2,
        scratch_shapes=[big, big, big, big, small, small],
        compiler_params=_cparams("arbitrary", "arbitrary"))(
            dyrows, cre4, cim4, ar8, ai8, ere, eim, urows, wre4, wim4, xre, xim)


def _ssm_post(yf, yb, u, sg, wglu_b, bglu, sw, dvec, tr):
    def body(yf_ref, yb_ref, u_ref, g_ref, w_ref, b_ref, sw_ref, d_ref, ypre_o, glu_o, ysn_o):
        ypre = yf_ref[...] + yb_ref[...] + d_ref[...] * u_ref[...]
        ypre_o[...] = ypre
        glu = _mm(_gelu_and_grad(ypre)[0], w_ref[...]) + b_ref[...]
        glu_o[...] = glu
        t = glu[:, :D_SSM] * _sigmoid(glu[:, D_SSM:]) * _silu_and_grad(g_ref[...])[0]
        ysn_o[...] = t * _rms(t) * sw_ref[...]

    return _rows_call("ssm_post", body, tr, [yf, yb, u, sg], [wglu_b, bglu, sw, dvec],
                      [(512, F32), (1024, F32), (512, F32)], [])


def _out_loss(ya, ysn, h, tgt, wo_b, post_w, n_valid, tr):
    lseg = h.shape[0] // N_SEG

    def body(ya_ref, ys_ref, h_ref, t_ref, w_ref, pw_ref, dy_o, dout_o, loss_o, dpw_o):
        i = pl.program_id(0)
        y = _mm(ya_ref[...], w_ref[0:D_ATTN, :]) + _mm(ys_ref[...], w_ref[D_ATTN:, :])
        r = _rms(y)
        pw = pw_ref[...]
        out = h_ref[...] + y * r * pw
        pos = _row_position(i * tr + lax.broadcasted_iota(jnp.int32, (tr, 1), 0), lseg)
        valid = jnp.logical_and(pos >= N_META, pos < n_valid)
        diff = jnp.where(valid, out - t_ref[...], 0.0)
        dout = diff * (1.0 / D_MODEL)
        dy, dpw = _rms_bwd(dout, y, r, pw)
        dy_o[...] = dy
        dout_o[...] = dout
        _acc(loss_o, 0.5 * jnp.sum(jnp.sum(diff * diff, axis=1, keepdims=True), axis=0, keepdims=True)
             * (1.0 / D_MODEL), i == 0)
        _acc(dpw_o, dpw, i == 0)

    return _rows_call("out_loss", body, tr, [ya, ysn, h, tgt], [wo_b, post_w], [(1024, F32), (1024, F32)],
                      [((1, 1), F32), ((1, D_MODEL), F32)])


def _out_bwd(dy, ya, ysn, o_flat, ag, wo_b, aw, head_sum, tr):
    def body(dy_ref, ya_ref, ys_ref, o_ref, g_ref, w_ref, aw_ref, hs_ref, do_o, dag_o, dysn_o, dl_o, dwo_o, daw_o):
        i = pl.program_id(0)
        dy = dy_ref[...]
        dcat = _mm_nt(dy, w_ref[...])
        cat = jnp.concatenate([ya_ref[...], ys_ref[...]], axis=1)
        _acc(dwo_o, _mm_tn(cat, dy), i == 0)
        dysn_o[...] = dcat[:, D_ATTN:]
        o = o_ref[...]
        sl, dsl = _silu_and_grad(g_ref[...])
        t = o * sl
        dt, daw = _rms_bwd(dcat[:, :D_ATTN], t, _rms(t), aw_ref[...])
        _acc(daw_o, daw, i == 0)
        do = dt * sl
        do_o[...] = do
        dag_o[...] = dt * o * dsl
        dl_o[...] = _mm_exact(do * o, hs_ref[...])

    return _rows_call("out_bwd", body, tr, [dy, ya, ysn, o_flat, ag], [wo_b, aw, head_sum],
                      [(512, F32), (512, F32), (512, F32), (HEADS, F32)],
                      [((D_MODEL, D_MODEL), F32), ((1, D_ATTN), F32)])


def _ssm_post_bwd(dysn, glu, sg, ypre, u, wglu_b, sw, dvec, tr):
    def body(d_ref, glu_ref, sg_ref, y_ref, u_ref, w_ref, sw_ref, dv_ref,
             dyp_o, dsg_o, dwg_o, dbg_o, dsw_o, dd_o):
        i = pl.program_id(0)
        glu = glu_ref[...]
        a, b = glu[:, :D_SSM], glu[:, D_SSM:]
        sb = _sigmoid(b)
        ys = a * sb
        sl, dsl = _silu_and_grad(sg_ref[...])
        t = ys * sl
        dt, dsw = _rms_bwd(d_ref[...], t, _rms(t), sw_ref[...])
        _acc(dsw_o, dsw, i == 0)
        dsg_o[...] = dt * ys * dsl
        dys = dt * sl
        dglu = jnp.concatenate([dys * sb, dys * a * sb * (1.0 - sb)], axis=1)
        _acc(dbg_o, jnp.sum(dglu, axis=0, keepdims=True), i == 0)
        gel, dgel = _gelu_and_grad(y_ref[...])
        _acc(dwg_o, _mm_tn(gel, dglu), i == 0)
        dyp = _mm_nt(dglu, w_ref[...]) * dgel
        dyp_o[...] = dyp
        _acc(dd_o, jnp.sum(dyp * u_ref[...], axis=0, keepdims=True), i == 0)

    return _rows_call("ssm_post_bwd", body, tr, [dysn, glu, sg, ypre, u], [wglu_b, sw, dvec],
                      [(512, F32), (512, F32)],
                      [((D_SSM, 2 * D_SSM), F32), ((1, 2 * D_SSM), F32), ((1, D_SSM), F32), ((1, D_SSM), F32)])


def _attn_bwd(qt, k, kt, v, dot, lse_t, delta_t, tk):
    _, nq, _, tq = qt.shape
    lp = k.shape[1]
    nk = lp // tk
    assert lse_t.shape == (HEADS, nq, 1, tq) and delta_t.shape == (HEADS, nq, 1, tq)

    def body(q_ref, k_ref, kt_ref, v_ref, do_ref, lse_ref, dl_ref, dq_o, dk_o, dv_o, dk_s, dv_s):
        @pl.when(pl.program_id(1) == 0)
        def _():
            dq_o[...] = jnp.zeros(dq_o.shape, F32)

        dk_s[...] = jnp.zeros(dk_s.shape, F32)
        dv_s[...] = jnp.zeros(dv_s.shape, F32)
        kk = k_ref[0]
        kkt = kt_ref[0]
        vv = v_ref[0]

        def chunk(c, carry):
            qq = q_ref[0, c]
            dd = do_ref[0, c]
            pt = jnp.exp2(_mm(kk, qq) - lse_ref[0, c])
            dv_s[...] += _mm_nt(dd, pt)
            dst = (pt * (_mm(vv, dd) - dl_ref[0, c])).astype(BF16)
            dk_s[...] += _mm_nt(qq, dst)
            dq_o[0, c] += _mm(kkt, dst)
            return carry

        n_loop = nq - nq % ATTN_BWD_UNROLL
        if n_loop:
            lax.fori_loop(0, n_loop, chunk, 0, unroll=ATTN_BWD_UNROLL)
        for c in range(n_loop, nq):
            chunk(c, 0)
        dk_o[0] = dk_s[...]
        dv_o[0] = dv_s[...]

    head = lambda w: pl.BlockSpec((1, nq, w, tq), lambda h, j: (h, 0, 0, 0))
    rows = lambda w: pl.BlockSpec((1, tk, w), lambda h, j: (h, j, 0))
    cols = lambda w: pl.BlockSpec((1, w, tk), lambda h, j: (h, 0, j))
    return pl.pallas_call(
        body, name="attn_bwd", grid=(HEADS, nk),
        in_specs=[head(QK_DIM), rows(QK_DIM), cols(QK_DIM), rows(V_HEAD), head(V_HEAD), head(1), head(1)],
        out_specs=[head(QK_DIM), cols(QK_DIM), cols(V_HEAD)],
        out_shape=[jax.ShapeDtypeStruct((HEADS, nq, QK_DIM, tq), F32), jax.ShapeDtypeStruct((HEADS, QK_DIM, lp), F32),
                   jax.ShapeDtypeStruct((HEADS, V_HEAD, lp), F32)],
        scratch_shapes=[pltpu.VMEM((QK_DIM, tk), F32), pltpu.VMEM((V_HEAD, tk), F32)],
        compiler_params=_cparams("arbitrary", "arbitrary"))(qt, k, kt, v, dot, lse_t, delta_t)


def _qkv_up_bwd(dqn, dr1, dr2, dkn, dv, dkr8, ql, kvl, cos8, sin8, c32, s32, qw, kvw, wq_b, wkv_b, p32, sum8, tr):
    def body(dqn_ref, dr1_ref, dr2_ref, dkn_ref, dv_ref, dkr_ref, ql_ref, kvl_ref, cos_ref, sin_ref, c32_ref,
             s32_ref, qw_ref, kvw_ref, wq_ref, wkv_ref, p_ref, s8_ref,
             dql_o, dkvl_o, dkrr_o, dwq_o, dwkv_o, dqw_o, dkvw_o):
        i = pl.program_id(0)
        cs, sn = cos_ref[...], sin_ref[...]
        d1, d2 = dr1_ref[...], dr2_ref[...]
        dq = jnp.concatenate([dqn_ref[...], d1 * cs + d2 * sn, d2 * cs - d1 * sn], axis=1) * (Q_SCALE / LOG2E)
        x = ql_ref[...]
        r = _rms(x)
        qw = qw_ref[...]
        _acc(dwq_o, _mm_tn(x * r * qw, dq), i == 0)
        dx, dw = _rms_bwd(_mm_nt(dq, wq_ref[...]), x, r, qw)
        dql_o[...] = dx
        _acc(dqw_o, dw, i == 0)
        dkv = jnp.concatenate([dkn_ref[...] * (1.0 / LOG2E), dv_ref[...]], axis=1)
        x = kvl_ref[...]
        r = _rms(x)
        kvw = kvw_ref[...]
        _acc(dwkv_o, _mm_tn(x * r * kvw, dkv), i == 0)
        dx, dw = _rms_bwd(_mm_nt(dkv, wkv_ref[...]), x, r, kvw)
        dkvl_o[...] = dx
        _acc(dkvw_o, dw, i == 0)
        dkr = _mm_exact(dkr_ref[...], s8_ref[...]) * (1.0 / LOG2E)
        dkrr_o[...] = dkr * c32_ref[...] + _mm_exact(dkr * s32_ref[...], p_ref[...])

    return _rows_call("qkv_up_bwd", body, tr, [dqn, dr1, dr2, dkn, dv, dkr8, ql, kvl, cos8, sin8, c32, s32],
                      [qw, kvw, wq_b, wkv_b, p32, sum8], [(256, F32), (128, F32), (32, F32)],
                      [((Q_LORA, 768), F32), ((KV_LORA, 1024), F32), ((1, Q_LORA), F32), ((1, KV_LORA), F32)])


def _inproj_bwd(dql, dkvl, dag, du_f, du_b, dypre, dsg, dkr, h, dout, pre_w, w_in_b, dvec, tr):
    def body(dql_ref, dkvl_ref, dag_ref, duf_ref, dub_ref, dyp_ref, dsg_ref, dkr_ref, h_ref, dout_ref,
             pw_ref, w_ref, dv_ref, dh_o, dwin_o, dpw_o):
        i = pl.program_id(0)
        du = duf_ref[...] + dub_ref[...] + dv_ref[...] * dyp_ref[...]
        dproj = jnp.concatenate([dql_ref[...], dkvl_ref[...], dag_ref[...], du, dsg_ref[...],
                                 dkr_ref[...], jnp.zeros((tr, D_IN_PAD - D_IN), F32)], axis=1)
        x = h_ref[...]
        r = _rms(x)
        pw = pw_ref[...]
        _acc(dwin_o, _mm_tn(x * r * pw, dproj), i == 0)
        dx, dw = _rms_bwd(_mm_nt(dproj, w_ref[...]), x, r, pw)
        _acc(dpw_o, dw, i == 0)
        dh_o[...] = dout_ref[...] + dx

    return _rows_call("inproj_bwd", body, tr, [dql, dkvl, dag, du_f, du_b, dypre, dsg, dkr, h, dout],
                      [pre_w, w_in_b, dvec], [(1024, F32)], [((D_MODEL, D_IN_PAD), F32), ((1, D_MODEL), F32)])


def _disc_terms(a_re, a_im, ldt):
    dt = jnp.exp(ldt)
    mag = jnp.exp(a_re * dt)
    th = a_im * dt
    cs, sn = jnp.cos(th), jnp.sin(th)
    abar_re, abar_im = mag * cs, mag * sn
    num_re, num_im = abar_re - 1.0, abar_im
    den = a_re * a_re + a_im * a_im
    coef_re = (num_re * a_re + num_im * a_im) / den
    coef_im = (num_im * a_re - num_re * a_im) / den
    return dt, mag, cs, sn, abar_re, abar_im, num_re, num_im, den, coef_re, coef_im


def _ssm_disc(a_re, a_im, ldt, bt_re, bt_im):
    def body(ar_ref, ai_ref, l_ref, br_ref, bi_ref, abr_o, abi_o, bbr_o, bbi_o):
        t = _disc_terms(ar_ref[...], ai_ref[...], l_ref[...])
        abr_o[...] = t[4]
        abi_o[...] = t[5]
        cr, ci = t[9], t[10]
        br, bi = br_ref[...], bi_ref[...]
        bbr_o[...] = cr * br - ci * bi
        bbi_o[...] = cr * bi + ci * br

    ng = a_re.shape[0]
    return pl.pallas_call(
        body, name="ssm_disc",
        out_shape=[jax.ShapeDtypeStruct((ng, 1, SSM_STATE), F32)] * 2
        + [jax.ShapeDtypeStruct((ng, SSM_GROUP, SSM_STATE), F32)] * 2)(a_re, a_im, ldt, bt_re, bt_im)


def _ssm_disc_bwd(a_re, a_im, ldt, bt_re, bt_im, da8_re, da8_im, dbb_re, dbb_im):
    def body(ar_ref, ai_ref, l_ref, br_ref, bi_ref, dar_ref, dai_ref, dbr_ref, dbi_ref,
             gar_o, gai_o, gl_o, gbr_o, gbi_o):
        a_re, a_im = ar_ref[...], ai_ref[...]
        dt, mag, cs, sn, abar_re, abar_im, num_re, num_im, den, cr, ci = _disc_terms(a_re, a_im, l_ref[...])
        br, bi = br_ref[...], bi_ref[...]
        dbr, dbi = dbr_ref[...], dbi_ref[...]
        gbr_o[...] = cr * dbr + ci * dbi
        gbi_o[...] = cr * dbi - ci * dbr
        dcr = jnp.sum(br * dbr + bi * dbi, axis=1, keepdims=True)
        dci = jnp.sum(br * dbi - bi * dbr, axis=1, keepdims=True)
        dnum_re = (dcr * a_re - dci * a_im) / den
        dnum_im = (dcr * a_im + dci * a_re) / den
        dden = -(dcr * cr + dci * ci) / den
        g_are = (dcr * num_re + dci * num_im) / den + dden * 2.0 * a_re
        g_aim = (dcr * num_im - dci * num_re) / den + dden * 2.0 * a_im
        d_abr = jnp.sum(dar_ref[...], axis=1, keepdims=True) + dnum_re
        d_abi = jnp.sum(dai_ref[...], axis=1, keepdims=True) + dnum_im
        dmag = d_abr * cs + d_abi * sn
        dth = d_abi * abar_re - d_abr * abar_im
        g_are = g_are + dmag * mag * dt
        g_aim = g_aim + dth * dt
        ddt = jnp.sum(dmag * mag * a_re + dth * a_im, axis=2, keepdims=True)
        gar_o[...] = g_are
        gai_o[...] = g_aim
        gl_o[...] = ddt * dt

    ng = a_re.shape[0]
    return pl.pallas_call(
        body, name="ssm_disc_bwd",
        out_shape=[jax.ShapeDtypeStruct((ng, 1, SSM_STATE), F32)] * 2 + [jax.ShapeDtypeStruct((ng, 1, 1), F32)]
        + [jax.ShapeDtypeStruct((ng, SSM_GROUP, SSM_STATE), F32)] * 2)(
            a_re, a_im, ldt, bt_re, bt_im, da8_re, da8_im, dbb_re, dbb_im)


def _exchange(name, per_peer, shared):
    parts = [a for a in (per_peer, shared) if a is not None]
    rp = per_peer.shape[1] if per_peer is not None else 0
    rs = shared.shape[0] if shared is not None else 0
    n = len(parts)

    def body(*refs):
        in_refs, out_ref, send_sems, recv_sems, local_sems = refs[:n], refs[n], refs[n + 1], refs[n + 2], refs[n + 3]
        x, y, c = lax.axis_index("x"), lax.axis_index("y"), lax.axis_index("c")
        me = 4 * x + 2 * y + c

        def pieces(peer):
            out = []
            if per_peer is not None:
                out.append((in_refs[0].at[peer], out_ref.at[me, pl.ds(0, rp), :]))
            if shared is not None:
                out.append((in_refs[-1], out_ref.at[me, pl.ds(rp, rs), :]))
            return out

        copies = []
        for k in range(1, N_DEV):
            px = 1 - x if (k >> 2) & 1 else x
            py = 1 - y if (k >> 1) & 1 else y
            pc = 1 - c if k & 1 else c
            for j, (src, dst) in enumerate(pieces(4 * px + 2 * py + pc)):
                s = (k - 1) * n + j
                copies.append(pltpu.make_async_remote_copy(
                    src_ref=src, dst_ref=dst, send_sem=send_sems.at[s], recv_sem=recv_sems.at[s],
                    device_id=(px, py, pc), device_id_type=pl.DeviceIdType.MESH))
        mine = [pltpu.make_async_copy(src, dst, local_sems.at[j]) for j, (src, dst) in enumerate(pieces(me))]
        for cp in mine + copies:
            cp.start()
        for cp in copies + mine:
            cp.wait()

    n_sem = (N_DEV - 1) * n
    return pl.pallas_call(
        body, name=name, out_shape=jax.ShapeDtypeStruct((N_DEV, rp + rs, LANES), F32),
        in_specs=[pl.BlockSpec(memory_space=pl.ANY)] * n, out_specs=pl.BlockSpec(memory_space=pl.ANY),
        scratch_shapes=[pltpu.SemaphoreType.DMA((n_sem,)), pltpu.SemaphoreType.DMA((n_sem,)),
                        pltpu.SemaphoreType.DMA((n,))])(*parts)


def _adamw(recv, w, m, v, tr):
    rows = w.shape[0]
    c1 = 1.0 - ADAM_B1 ** ADAM_STEP
    c2 = 1.0 - ADAM_B2 ** ADAM_STEP

    def body(r_ref, w_ref, m_ref, v_ref, g_o, d_o, m_o, v_o):
        g = r_ref[0]
        for k in range(1, N_DEV):
            g = g + r_ref[k]
        mm = ADAM_B1 * m_ref[...] + (1.0 - ADAM_B1) * g
        vv = ADAM_B2 * v_ref[...] + (1.0 - ADAM_B2) * (g * g)
        g_o[...] = g
        m_o[...] = mm
        v_o[...] = vv
        d_o[...] = -ADAM_LR * ((mm / c1) / (jnp.sqrt(vv / c2) + ADAM_EPS) + ADAM_WD * w_ref[...])

    spec = pl.BlockSpec((tr, LANES), lambda i: (i, 0))
    return pl.pallas_call(
        body, name="adamw", grid=(rows // tr,),
        in_specs=[pl.BlockSpec((N_DEV, tr, LANES), lambda i: (0, i, 0)), spec, spec, spec],
        out_specs=[spec] * 4, out_shape=[jax.ShapeDtypeStruct((rows, LANES), F32)] * 4,
        compiler_params=_cparams("arbitrary"))(recv, w, m, v)


def _to_rows(a):
    flat = a.reshape(-1)
    pad = (-flat.shape[0]) % LANES
    if pad:
        flat = jnp.concatenate([flat, jnp.zeros((pad,), flat.dtype)])
    return flat.reshape(-1, LANES)


def _n_rows(shape):
    return -(-int(np.prod(shape)) // LANES)


def _pack(arrays, total_rows):
    rows = [_to_rows(a) for a in arrays]
    used = sum(r.shape[0] for r in rows)
    if total_rows > used:
        rows.append(jnp.zeros((total_rows - used, LANES), F32))
    return jnp.concatenate(rows, axis=0)


def _unpack(buf, shapes):
    lead = buf.shape[:-2]
    out, r0 = [], 0
    for s in shapes:
        n = int(np.prod(s))
        nr = _n_rows(s)
        out.append(buf[..., r0:r0 + nr, :].reshape(lead + (-1,))[..., :n].reshape(lead + tuple(s)))
        r0 += nr
    return out


def _pack_per_device(arrays, total_rows):
    rows = []
    for a in arrays:
        flat = a.reshape(N_DEV, -1)
        pad = (-flat.shape[1]) % LANES
        if pad:
            flat = jnp.concatenate([flat, jnp.zeros((N_DEV, pad), flat.dtype)], axis=1)
        rows.append(flat.reshape(N_DEV, -1, LANES))
    used = sum(r.shape[1] for r in rows)
    if total_rows > used:
        rows.append(jnp.zeros((N_DEV, total_rows - used, LANES), F32))
    return jnp.concatenate(rows, axis=1)


def _shard_views(name, full):
    if name == 'w_out':
        return full.reshape(N_DEV, full.shape[0] // N_DEV, full.shape[1])
    r, ccols = full.shape
    return full.reshape(r, N_DEV, ccols // N_DEV).transpose(1, 0, 2)


def _from_shards(name, stacked):
    if name == 'w_out':
        return stacked.reshape(-1, stacked.shape[-1])
    n, r, cc = stacked.shape
    return stacked.transpose(1, 0, 2).reshape(r, n * cc)


GROUPS_PER_BLK = N_GROUPS // N_COL_BLK


def _block_diag(t):
    eye = jnp.eye(GROUPS_PER_BLK, dtype=t.dtype)
    t4 = t.reshape(N_COL_BLK, GROUPS_PER_BLK, SSM_GROUP, SSM_STATE)
    return (t4[:, :, :, None, :] * eye[None, :, None, :, None]).reshape(N_COL_BLK, CH_BLK, COL_BLK)


def _diag_blocks(mat4):
    eye = jnp.eye(GROUPS_PER_BLK, dtype=mat4.dtype)
    m6 = mat4.reshape(N_COL_BLK, GROUPS_PER_BLK, SSM_GROUP, GROUPS_PER_BLK, SSM_STATE)
    return (m6 * eye[None, :, None, :, None]).sum(axis=3).reshape(N_GROUPS, SSM_GROUP, SSM_STATE)


def _step(x, loss_target, wts, moms, vels):
    seq = x.shape[1]
    n_valid = N_META + seq
    lp = -(-n_valid // 256) * 256
    tr = _pick(lp, [640, 256])
    tr_mid = 256
    tq = _pick(lp, [1280, 256])
    tk = _pick(lp, [640, 256])

    shard_shapes = [wts[n].shape[-2:] for n in SHARDED]
    n_shard_rows = sum(_n_rows(s) for s in shard_shapes)
    gathered = _exchange("gather_weights", None,
                         _pack([wts[n].reshape(wts[n].shape[-2:]) for n in SHARDED], n_shard_rows))
    full = {n: _from_shards(n, a) for n, a in zip(SHARDED, _unpack(gathered, shard_shapes))}

    w_in_b = jnp.concatenate([_cols_in(full['w_in']), jnp.zeros((D_MODEL, D_IN_PAD - D_IN), F32)],
                             axis=1).astype(BF16)
    wq_b = _cols_q(full['w_q_up']).astype(BF16)
    wkv_b = _cols_kv(full['w_kv_up']).astype(BF16)
    wglu_b = full['w_glu'].astype(BF16)
    wo_b = full['w_out'].astype(BF16)
    pre_w, post_w = wts['pre_norm_w'], wts['post_norm_w']
    qw, kvw, aw, sw = wts['q_norm_w'], wts['kv_norm_w'], wts['attn_out_norm_w'], wts['ssm_out_norm_w']
    bglu, dvec = wts['b_glu'], wts['ssm_d']

    lseg = lp // N_SEG
    pos = _row_position(jnp.arange(lp, dtype=jnp.int32), lseg)
    inv = ROPE_THETA ** (-jnp.arange(HALF_ROPE, dtype=F32) / HALF_ROPE)
    ang = pos.astype(F32)[:, None] * inv[None, :]
    cos, sin = jnp.cos(ang), jnp.sin(ang)
    cos8, sin8 = jnp.tile(cos, (1, HEADS)), jnp.tile(sin, (1, HEADS))
    c32 = jnp.concatenate([cos, cos], axis=1)
    s32 = jnp.concatenate([-sin, sin], axis=1)
    p32 = jnp.asarray(np.roll(np.eye(QK_ROPE, dtype=np.float32), HALF_ROPE, axis=1))
    sum8 = jnp.asarray(np.tile(np.eye(QK_ROPE, dtype=np.float32), (HEADS, 1)))
    head_sum = jnp.asarray(np.repeat(np.eye(HEADS, dtype=np.float32), V_HEAD, axis=0))

    ng = 2 * N_GROUPS
    a_re3 = wts['ssm_a_re'].reshape(ng, 1, SSM_STATE)
    a_im3 = wts['ssm_a_im'].reshape(ng, 1, SSM_STATE)
    ldt3 = wts['ssm_log_dt'].reshape(ng, 1, 1)
    bt_re = wts['ssm_b_re'].reshape(2, N_GROUPS, SSM_STATE, SSM_GROUP).transpose(0, 1, 3, 2).reshape(
        ng, SSM_GROUP, SSM_STATE)
    bt_im = wts['ssm_b_im'].reshape(2, N_GROUPS, SSM_STATE, SSM_GROUP).transpose(0, 1, 3, 2).reshape(
        ng, SSM_GROUP, SSM_STATE)
    c_re = wts['ssm_c_re'].reshape(ng, SSM_GROUP, SSM_STATE)
    c_im = wts['ssm_c_im'].reshape(ng, SSM_GROUP, SSM_STATE)
    abar_re, abar_im, bbt_re, bbt_im = _ssm_disc(a_re3, a_im3, ldt3, bt_re, bt_im)

    def direction(t, d):
        return t[d * N_GROUPS:(d + 1) * N_GROUPS]

    def slab(t, d, sign=1.0):
        return jnp.broadcast_to(sign * direction(t, d).reshape(1, N_STATES), (N_SEG, N_STATES))

    w_re = [_block_diag(direction(bbt_re, d)).astype(BF16) for d in range(2)]
    w_im = [_block_diag(direction(bbt_im, d)).astype(BF16) for d in range(2)]
    cb_re = [_block_diag(direction(c_re, d)).astype(BF16) for d in range(2)]
    cb_im = [_block_diag(-direction(c_im, d)).astype(BF16) for d in range(2)]

    def to_rows(head, body):
        segs = []
        for j in range(N_SEG):
            a, b = j * lseg, (j + 1) * lseg
            parts = []
            if a < N_META:
                parts.append(head[a:min(b, N_META)])
            if max(a, N_META) < min(b, n_valid):
                parts.append(body[max(a, N_META) - N_META:min(b, n_valid) - N_META])
            if b > n_valid:
                parts.append(jnp.zeros((b - max(a, n_valid), D_MODEL), F32))
            segs.append(parts[0] if len(parts) == 1 else jnp.concatenate(parts, axis=0))
        return jnp.stack(segs, axis=1).reshape(lp, D_MODEL)

    def body_tokens(a):
        a3 = a.reshape(lseg, N_SEG, D_MODEL)
        parts = [a3[max(j * lseg, N_META) - j * lseg:min((j + 1) * lseg, n_valid) - j * lseg, j]
                 for j in range(N_SEG) if max(j * lseg, N_META) < min((j + 1) * lseg, n_valid)]
        return jnp.concatenate(parts, axis=0)

    h = to_rows(full['meta_tokens'], x[0])
    tgt = to_rows(jnp.zeros((N_META, D_MODEL), F32), loss_target[0])

    ql, kvl, ag, su, sg, kr = _inproj(h, pre_w, w_in_b, tr)
    qn_b, qr1_b, qr2_b, kn_b, v_b, kr_b = _qkv_up(ql, kvl, kr, cos8, sin8, c32, s32, qw, kvw, wq_b, wkv_b, p32, tr)

    def heads(a, w):
        return a.reshape(lp, HEADS, w)

    nq, nk = lp // tq, lp // tk
    q_t = jnp.concatenate([heads(qn_b, 64), heads(qr1_b, 16), heads(qr2_b, 16)], axis=-1)
    k_t = jnp.concatenate([heads(kn_b, 64), jnp.broadcast_to(kr_b[:, None, :], (lp, HEADS, QK_ROPE))], axis=-1)
    v_t = heads(v_b, 64)
    vx_t = jnp.concatenate([v_t, jnp.ones((lp, HEADS, 1), BF16), jnp.zeros((lp, HEADS, LANES - V_HEAD - 1), BF16)],
                           axis=-1)
    qt4 = q_t.reshape(nq, tq, HEADS, QK_DIM).transpose(2, 0, 3, 1)
    tk_fwd = _pick(lp, [1280, 256])
    vxt4 = vx_t.reshape(lp // tk_fwd, tk_fwd, HEADS, LANES).transpose(2, 0, 3, 1)
    k_h = k_t.transpose(1, 0, 2)
    kt_h = k_t.transpose(1, 2, 0)
    v_h = v_t.transpose(1, 0, 2)
    ot_h, lse4 = _attn_fwd(qt4, k_h, vxt4, n_valid)
    o_flat = ot_h.transpose(2, 0, 1).reshape(lp, D_ATTN)
    ya = _attn_post(o_flat, ag, aw, tr)

    xs, ys = [], []
    for d in range(2):
        ar8, ai8 = slab(abar_re, d), slab(abar_im, d)
        ere, eim = _scan_ends(f"scan{d}_ends", su, w_re[d], w_im[d], ar8, ai8, d == 0)
        x_re, x_im, y_d = _scan_fwd(f"scan{d}", su, w_re[d], w_im[d], ar8, ai8, ere, eim, cb_re[d], cb_im[d],
                                    d == 0)
        xs += [x_re, x_im]
        ys.append(y_d)
    ypre, glu, ysn = _ssm_post(ys[0], ys[1], su, sg, wglu_b, bglu, sw, dvec, tr)

    dy, dout, loss, d_post = _out_loss(ya, ysn, h, tgt, wo_b, post_w, n_valid, tr)

    do_flat, dag, dysn, delta8, d_wo, d_aw = _out_bwd(dy, ya, ysn, o_flat, ag, wo_b, aw, head_sum, tr)
    dypre, dsg, d_wglu, d_bglu, d_sw, d_dvec = _ssm_post_bwd(dysn, glu, sg, ypre, su, wglu_b, sw, dvec, tr)

    dus, d_ct, d_wb, d_a8 = [], [], [], []
    for d in range(2):
        ar8, ai8c = slab(abar_re, d), slab(abar_im, d, -1.0)
        ere, eim = _scan_ends(f"scan_adj{d}_ends", dypre, cb_re[d], cb_im[d], ar8, ai8c, d != 0)
        du_d, dw_re, dw_im, dc_re, dc_im, da_re, da_im = _scan_bwd(
            f"scan_adj{d}", dypre, cb_re[d], cb_im[d], ar8, ai8c, ere, eim, su, w_re[d], w_im[d],
            xs[2 * d], xs[2 * d + 1], d != 0)
        dus.append(du_d)
        d_ct.append((dc_re, dc_im))
        d_wb.append((dw_re, dw_im))
        d_a8.append((da_re, da_im))

    dot4 = do_flat.astype(BF16).reshape(nq, tq, HEADS, V_HEAD).transpose(2, 0, 3, 1)
    dqt4, dkt_h, dvt_h = _attn_bwd(qt4, k_h, kt_h, v_h, dot4, lse4, delta8.T.reshape(HEADS, nq, 1, tq), tk)
    dq_t = dqt4.transpose(1, 3, 0, 2).reshape(lp, HEADS, QK_DIM)
    dk_t = dkt_h.transpose(2, 0, 1)
    dqn = dq_t[:, :, :64].reshape(lp, 512)
    dr1 = dq_t[:, :, 64:80].reshape(lp, 128)
    dr2 = dq_t[:, :, 80:96].reshape(lp, 128)
    dkn = dk_t[:, :, :64].reshape(lp, 512)
    dkr8 = dk_t[:, :, 64:].reshape(lp, HEADS * QK_ROPE)
    dvf = dvt_h.transpose(2, 0, 1).reshape(lp, 512)
    dql, dkvl, dkrr, d_wq, d_wkv, d_qw, d_kvw = _qkv_up_bwd(
        dqn, dr1, dr2, dkn, dvf, dkr8, ql, kvl, cos8, sin8, c32, s32, qw, kvw, wq_b, wkv_b, p32, sum8, tr)
    dh, d_win, d_pre = _inproj_bwd(dql, dkvl, dag, dus[0], dus[1], dypre, dsg, dkrr, h, dout, pre_w, w_in_b, dvec,
                                   tr_mid)
    assert N_META <= lseg
    d_meta = dh.reshape(lseg, N_SEG, D_MODEL)[:N_META, 0]

    def seg_sums(t):
        return t.reshape(N_SEG, N_GROUPS, SSM_STATE).transpose(1, 0, 2)

    da8_re = jnp.concatenate([seg_sums(d_a8[d][0]) for d in range(2)], axis=0)
    da8_im = jnp.concatenate([seg_sums(d_a8[d][1]) for d in range(2)], axis=0)
    dbb_re = jnp.concatenate([_diag_blocks(d_wb[d][0]) for d in range(2)], axis=0)
    dbb_im = jnp.concatenate([_diag_blocks(d_wb[d][1]) for d in range(2)], axis=0)
    g_are, g_aim, g_ldt, g_bt_re, g_bt_im = _ssm_disc_bwd(a_re3, a_im3, ldt3, bt_re, bt_im, da8_re, da8_im,
                                                          dbb_re, dbb_im)
    g_c_re = jnp.concatenate([_diag_blocks(d_ct[d][0]) for d in range(2)], axis=0)
    g_c_im = jnp.concatenate([-_diag_blocks(d_ct[d][1]) for d in range(2)], axis=0)

    def b_layout(t):
        return t.reshape(2, N_GROUPS, SSM_GROUP, SSM_STATE).transpose(0, 1, 3, 2)

    local = {
        'meta_tokens': d_meta,
        'pre_norm_w': d_pre, 'post_norm_w': d_post,
        'w_in': _cols_in_inv(d_win[:, :D_IN]),
        'q_norm_w': d_qw, 'w_q_up': _cols_q_inv(d_wq),
        'kv_norm_w': d_kvw, 'w_kv_up': _cols_kv_inv(d_wkv),
        'attn_out_norm_w': d_aw,
        'ssm_a_re': g_are, 'ssm_a_im': g_aim, 'ssm_log_dt': g_ldt,
        'ssm_b_re': b_layout(g_bt_re), 'ssm_b_im': b_layout(g_bt_im), 'ssm_c_re': g_c_re, 'ssm_c_im': g_c_im,
        'ssm_d': d_dvec, 'w_glu': d_wglu, 'b_glu': d_bglu, 'ssm_out_norm_w': d_sw, 'w_out': d_wo,
    }

    replicated = [n for n in WEIGHTS if n not in SHARDED]
    order = SHARDED + replicated
    shapes = [wts[n].shape for n in order] + [(1, 1)]
    tr_adam = 512
    total_rows = -(-sum(_n_rows(s) for s in shapes) // tr_adam) * tr_adam
    recv = _exchange("exchange_grads",
                     _pack_per_device([_shard_views(n, local[n]) for n in SHARDED], n_shard_rows),
                     _pack([local[n] for n in replicated] + [loss], total_rows - n_shard_rows))
    zero = jnp.zeros((1, 1), F32)
    packed = [_pack([src[n] for n in order] + [zero], total_rows) for src in (wts, moms, vels)]
    g_p, d_p, m_p, v_p = _adamw(recv, *packed, tr_adam)
    sums = _unpack(g_p, shapes)
    grads = dict(zip(order, sums))
    deltas, new_m, new_v = (dict(zip(order, _unpack(b, shapes))) for b in (d_p, m_p, v_p))

    grad_x = body_tokens(dh)[None]
    return (sums[-1][0, 0], grad_x, *[grads[n] for n in WEIGHTS], *[deltas[n] for n in WEIGHTS],
            *[new_m[n] for n in WEIGHTS], *[new_v[n] for n in WEIGHTS])


def kernel(x, meta_tokens, pre_norm_w, post_norm_w, w_in, q_norm_w, w_q_up, kv_norm_w, w_kv_up, attn_out_norm_w, ssm_a_re, ssm_a_im, ssm_log_dt, ssm_b_re, ssm_b_im, ssm_c_re, ssm_c_im, ssm_d, w_glu, b_glu, ssm_out_norm_w, w_out, loss_target, m_meta_tokens, m_pre_norm_w, m_post_norm_w, m_w_in, m_q_norm_w, m_w_q_up, m_kv_norm_w, m_w_kv_up, m_attn_out_norm_w, m_ssm_a_re, m_ssm_a_im, m_ssm_log_dt, m_ssm_b_re, m_ssm_b_im, m_ssm_c_re, m_ssm_c_im, m_ssm_d, m_w_glu, m_b_glu, m_ssm_out_norm_w, m_w_out, v_meta_tokens, v_pre_norm_w, v_post_norm_w, v_w_in, v_q_norm_w, v_w_q_up, v_kv_norm_w, v_w_kv_up, v_attn_out_norm_w, v_ssm_a_re, v_ssm_a_im, v_ssm_log_dt, v_ssm_b_re, v_ssm_b_im, v_ssm_c_re, v_ssm_c_im, v_ssm_d, v_w_glu, v_b_glu, v_ssm_out_norm_w, v_w_out):
    wts = dict(zip(WEIGHTS, (meta_tokens, pre_norm_w, post_norm_w, w_in, q_norm_w, w_q_up, kv_norm_w, w_kv_up,
                             attn_out_norm_w, ssm_a_re, ssm_a_im, ssm_log_dt, ssm_b_re, ssm_b_im, ssm_c_re,
                             ssm_c_im, ssm_d, w_glu, b_glu, ssm_out_norm_w, w_out)))
    moms = dict(zip(WEIGHTS, (m_meta_tokens, m_pre_norm_w, m_post_norm_w, m_w_in, m_q_norm_w, m_w_q_up,
                              m_kv_norm_w, m_w_kv_up, m_attn_out_norm_w, m_ssm_a_re, m_ssm_a_im, m_ssm_log_dt,
                              m_ssm_b_re, m_ssm_b_im, m_ssm_c_re, m_ssm_c_im, m_ssm_d, m_w_glu, m_b_glu,
                              m_ssm_out_norm_w, m_w_out)))
    vels = dict(zip(WEIGHTS, (v_meta_tokens, v_pre_norm_w, v_post_norm_w, v_w_in, v_q_norm_w, v_w_q_up,
                              v_kv_norm_w, v_w_kv_up, v_attn_out_norm_w, v_ssm_a_re, v_ssm_a_im, v_ssm_log_dt,
                              v_ssm_b_re, v_ssm_b_im, v_ssm_c_re, v_ssm_c_im, v_ssm_d, v_w_glu, v_b_glu,
                              v_ssm_out_norm_w, v_w_out)))
    return _step(x, loss_target, wts, moms, vels)
```

```python
import functools
import math

import numpy as np
import jax
import jax.numpy as jnp
from jax import lax
from jax.experimental import pallas as pl
from jax.experimental.pallas import tpu as pltpu

F32 = jnp.float32
BF16 = jnp.bfloat16

D_MODEL = 1024
N_META = 16
EPS = 1e-6
HEADS = 8
QK_NOPE = 64
QK_ROPE = 32
HALF_ROPE = QK_ROPE // 2
QK_DIM = QK_NOPE + QK_ROPE
V_HEAD = 64
Q_LORA = 256
KV_LORA = 128
D_ATTN = HEADS * V_HEAD
D_SSM = 512
SSM_GROUP = 16
N_GROUPS = D_SSM // SSM_GROUP
SSM_STATE = 64
N_STATES = N_GROUPS * SSM_STATE
ROPE_THETA = 10000.0
D_IN = Q_LORA + KV_LORA + QK_ROPE + D_ATTN + 2 * D_SSM
D_IN_PAD = 2048
N_DEV = 8
N_SEG = 8
COL_BLK = 512
LANES = 128

ADAM_LR = 0.001
ADAM_B1 = 0.9
ADAM_B2 = 0.999
ADAM_EPS = 1e-08
ADAM_WD = 0.01
ADAM_STEP = 10

VMEM_LIMIT_V7X = 56 * 1024 * 1024
LOG2E = 1.0 / math.log(2.0)
Q_SCALE = LOG2E / math.sqrt(QK_DIM)
ATTN_UNROLL = 4
ATTN_BWD_UNROLL = 4

WEIGHTS = ['meta_tokens', 'pre_norm_w', 'post_norm_w', 'w_in', 'q_norm_w', 'w_q_up', 'kv_norm_w', 'w_kv_up',
           'attn_out_norm_w', 'ssm_a_re', 'ssm_a_im', 'ssm_log_dt', 'ssm_b_re', 'ssm_b_im', 'ssm_c_re', 'ssm_c_im',
           'ssm_d', 'w_glu', 'b_glu', 'ssm_out_norm_w', 'w_out']
SHARDED = ['w_in', 'w_q_up', 'w_kv_up', 'w_glu', 'w_out', 'meta_tokens']

def _cols_in(w):
    return jnp.concatenate([w[:, 0:384], w[:, 416:D_IN], w[:, 384:416]], axis=1)


def _cols_in_inv(w):
    return jnp.concatenate([w[:, 0:384], w[:, D_IN - QK_ROPE:D_IN], w[:, 384:D_IN - QK_ROPE]], axis=1)


def _cols_q(w):
    t = w.reshape(w.shape[0], HEADS, QK_DIM)
    return jnp.concatenate([t[:, :, 0:64].reshape(-1, 512), t[:, :, 64:80].reshape(-1, 128),
                            t[:, :, 80:96].reshape(-1, 128)], axis=1)


def _cols_q_inv(w):
    r = w.shape[0]
    return jnp.concatenate([w[:, 0:512].reshape(r, HEADS, 64), w[:, 512:640].reshape(r, HEADS, 16),
                            w[:, 640:768].reshape(r, HEADS, 16)], axis=2).reshape(r, HEADS * QK_DIM)


def _cols_kv(w):
    t = w.reshape(w.shape[0], HEADS, 128)
    return jnp.concatenate([t[:, :, 0:64].reshape(-1, 512), t[:, :, 64:128].reshape(-1, 512)], axis=1)


def _cols_kv_inv(w):
    r = w.shape[0]
    return jnp.concatenate([w[:, 0:512].reshape(r, HEADS, 64), w[:, 512:1024].reshape(r, HEADS, 64)],
                           axis=2).reshape(r, HEADS * 128)


def _pick(n, cands):
    for c in cands:
        if n % c == 0:
            return c
    raise ValueError(f"no tile for {n}")


def _cparams(*sem):
    return pltpu.CompilerParams(dimension_semantics=sem, vmem_limit_bytes=VMEM_LIMIT_V7X)


def _mm(a, b):
    return jnp.dot(a.astype(BF16), b.astype(BF16), preferred_element_type=F32)


def _mm_nt(a, b):
    return lax.dot_general(a.astype(BF16), b.astype(BF16), (((1,), (1,)), ((), ())), preferred_element_type=F32)


def _mm_tn(a, b):
    return lax.dot_general(a.astype(BF16), b.astype(BF16), (((0,), (0,)), ((), ())), preferred_element_type=F32)


def _mm_exact(a, b):
    return jnp.dot(a, b, precision=lax.Precision.HIGHEST, preferred_element_type=F32)


def _rms(x):
    return lax.rsqrt(jnp.mean(x * x, axis=-1, keepdims=True) + EPS)


def _rms_bwd(dy, x, r, w):
    xh = x * r
    g = dy * w
    dx = r * (g - xh * jnp.mean(g * xh, axis=-1, keepdims=True))
    dw = jnp.sum(dy * xh, axis=0, keepdims=True)
    return dx, dw


def _sigmoid(z):
    return 1.0 / (1.0 + jnp.exp(-z))


def _silu_and_grad(z):
    s = _sigmoid(z)
    return z * s, s * (1.0 + z * (1.0 - s))


_GELU_C = math.sqrt(2.0 / math.pi)


def _gelu_and_grad(x):
    x2 = x * x
    t = jnp.tanh(_GELU_C * (x + 0.044715 * x * x2))
    val = 0.5 * x * (1.0 + t)
    grad = 0.5 * (1.0 + t) + 0.5 * x * (1.0 - t * t) * _GELU_C * (1.0 + 3.0 * 0.044715 * x2)
    return val, grad


def _acc(ref, val, first):
    @pl.when(first)
    def _():
        ref[...] = val

    @pl.when(jnp.logical_not(first))
    def _():
        ref[...] += val


def _rows_call(name, body, tr, row_ins, full_ins, row_outs, acc_outs):
    lp = row_ins[0].shape[0]
    in_specs = [pl.BlockSpec((tr, a.shape[1]), lambda i: (i, 0)) for a in row_ins]
    in_specs += [pl.BlockSpec(a.shape, lambda i, n=a.ndim: (0,) * n) for a in full_ins]
    out_specs = [pl.BlockSpec((tr, c), lambda i: (i, 0)) for c, _ in row_outs]
    out_specs += [pl.BlockSpec(s, lambda i, n=len(s): (0,) * n) for s, _ in acc_outs]
    out_shape = [jax.ShapeDtypeStruct((lp, c), dt) for c, dt in row_outs]
    out_shape += [jax.ShapeDtypeStruct(s, dt) for s, dt in acc_outs]
    return pl.pallas_call(
        body, name=name, grid=(lp // tr,), in_specs=in_specs, out_specs=out_specs, out_shape=out_shape,
        compiler_params=_cparams("arbitrary"))(*row_ins, *full_ins)


def _inproj(h, pre_w, w_in_b, tr):
    def body(h_ref, pw_ref, w_ref, ql, kvl, ag, su, sg, kr):
        x = h_ref[...]
        xn = x * _rms(x) * pw_ref[...]
        pr = _mm(xn, w_ref[...])
        ql[...] = pr[:, 0:256]
        kvl[...] = pr[:, 256:384]
        ag[...] = pr[:, 384:896]
        su[...] = pr[:, 896:1408]
        sg[...] = pr[:, 1408:1920]
        kr[...] = pr[:, 1920:1952]

    return _rows_call("inproj", body, tr, [h], [pre_w, w_in_b],
                      [(256, F32), (128, F32), (512, F32), (512, F32), (512, F32), (32, F32)], [])


def _qkv_up(ql, kvl, kr, cos8, sin8, c32, s32, qw, kvw, wq_b, wkv_b, p32, tr):
    def body(ql_ref, kvl_ref, kr_ref, cos_ref, sin_ref, c32_ref, s32_ref, qw_ref, kvw_ref, wq_ref, wkv_ref, p_ref,
             qn_o, qr1_o, qr2_o, kn_o, v_o, kr_o):
        x = ql_ref[...]
        q = _mm(x * _rms(x) * qw_ref[...], wq_ref[...]) * Q_SCALE
        r1, r2 = q[:, 512:640], q[:, 640:768]
        cs, sn = cos_ref[...], sin_ref[...]
        qn_o[...] = q[:, 0:512].astype(BF16)
        qr1_o[...] = (r1 * cs - r2 * sn).astype(BF16)
        qr2_o[...] = (r2 * cs + r1 * sn).astype(BF16)
        x = kvl_ref[...]
        kv = _mm(x * _rms(x) * kvw_ref[...], wkv_ref[...])
        kn_o[...] = kv[:, 0:512].astype(BF16)
        v_o[...] = kv[:, 512:1024].astype(BF16)
        x = kr_ref[...]
        kr_o[...] = (x * c32_ref[...] + _mm_exact(x, p_ref[...]) * s32_ref[...]).astype(BF16)

    return _rows_call("qkv_up", body, tr, [ql, kvl, kr, cos8, sin8, c32, s32], [qw, kvw, wq_b, wkv_b, p32],
                      [(512, BF16), (128, BF16), (128, BF16), (512, BF16), (512, BF16), (32, BF16)], [])


def _row_position(row, lseg):
    return (row & (N_SEG - 1)) * lseg + (row >> 3)


def _first_padded_tile(n_valid, lp, tile):
    lseg = lp // N_SEG
    t0 = n_valid - (N_SEG - 1) * lseg
    return (t0 * N_SEG + N_SEG - 1) // tile if n_valid < lp else lp // tile


def _attn_fwd(qt, k, vxt, n_valid):
    _, nq, _, tq = qt.shape
    _, nk, _, tk = vxt.shape
    lp = k.shape[1]
    lseg = lp // N_SEG
    n_plain = max(0, min(nk, _first_padded_tile(n_valid, lp, tk)))

    def body(q_ref, k_ref, v_ref, o_ref, lse_ref, m_s, acc_s):
        m_s[...] = jnp.full(m_s.shape, -1e30, F32)
        acc_s[...] = jnp.zeros(acc_s.shape, F32)
        qq = q_ref[0, 0]

        def chunk(c, padded):
            r0 = pl.multiple_of(c * tk, tk)
            st = _mm(k_ref[0, pl.ds(r0, tk), :], qq)
            if padded:
                row = r0 + lax.broadcasted_iota(jnp.int32, (tk, tq), 0)
                st = jnp.where(_row_position(row, lseg) < n_valid, st, -1e30)
            m_old = m_s[...]
            m_new = jnp.maximum(m_old, jnp.max(st, axis=0, keepdims=True))
            pt = jnp.exp2(st - m_new)
            acc_s[...] = jnp.exp2(m_old - m_new) * acc_s[...] + _mm(v_ref[0, c], pt)
            m_s[...] = m_new

        def plain(c, carry):
            chunk(c, False)
            return carry

        n_loop = n_plain - n_plain % ATTN_UNROLL
        if n_loop:
            lax.fori_loop(0, n_loop, plain, 0, unroll=ATTN_UNROLL)
        for c in range(n_loop, nk):
            chunk(c, c >= n_plain)
        acc = acc_s[...]
        l = acc[V_HEAD:V_HEAD + 1, :]
        o_ref[0] = acc[:V_HEAD, :] / l
        lse_ref[0, 0] = m_s[...] + jnp.log2(l)

    return pl.pallas_call(
        body, name="attn_fwd", grid=(HEADS, nq),
        in_specs=[pl.BlockSpec((1, 1, QK_DIM, tq), lambda h, i: (h, i, 0, 0)),
                  pl.BlockSpec((1, lp, QK_DIM), lambda h, i: (h, 0, 0)),
                  pl.BlockSpec((1, nk, LANES, tk), lambda h, i: (h, 0, 0, 0))],
        out_specs=[pl.BlockSpec((1, V_HEAD, tq), lambda h, i: (h, 0, i)),
                   pl.BlockSpec((1, 1, 1, tq), lambda h, i: (h, i, 0, 0))],
        out_shape=[jax.ShapeDtypeStruct((HEADS, V_HEAD, lp), F32), jax.ShapeDtypeStruct((HEADS, nq, 1, tq), F32)],
        scratch_shapes=[pltpu.VMEM((1, tq), F32), pltpu.VMEM((LANES, tq), F32)],
        compiler_params=_cparams("arbitrary", "arbitrary"))(qt, k, vxt)


def _attn_post(o_flat, ag, aw, tr):
    def body(o_ref, g_ref, w_ref, ya):
        t = o_ref[...] * _silu_and_grad(g_ref[...])[0]
        ya[...] = t * _rms(t) * w_ref[...]

    return _rows_call("attn_post", body, tr, [o_flat, ag], [aw], [(512, F32)], [])[0]


def _scan_tiles(lp):
    lseg = lp // N_SEG
    tt = _pick(lseg, [208, 48, 32, 16, 8, 4, 2, 1])
    return lseg, tt, lseg // tt


def _cmul(ar, ai, br, bi):
    return ar * br - ai * bi, ar * bi + ai * br


SCAN_STEPS_PER_ITER = 4
N_COL_BLK = N_STATES // COL_BLK
CH_BLK = D_SSM // N_COL_BLK


def _scan_steps(tt, forward, bre_ref, bim_ref, ar, ai, carry, visit):
    def step(s, c):
        r0 = pl.multiple_of((s if forward else tt - 1 - s) * N_SEG, N_SEG)
        pr, pi = _cmul(ar, ai, c[0], c[1])
        xr = pr + bre_ref[pl.ds(r0, N_SEG), :]
        xi = pi + bim_ref[pl.ds(r0, N_SEG), :]
        return (xr, xi) + tuple(visit(r0, (xr, xi), (c[0], c[1]), c[2:]))

    per = SCAN_STEPS_PER_ITER if tt % SCAN_STEPS_PER_ITER == 0 else 1

    def steps(it, c):
        for u in range(per):
            c = step(it * per + u, c)
        return c

    return lax.fori_loop(0, tt // per, steps, carry)


def _segment_starts(lseg, forward, ar, ai, ere_ref, eim_ref, s_re, s_im):
    a1r, a1i = ar[0:1, :], ai[0:1, :]
    pr, pi = jnp.ones_like(a1r), jnp.zeros_like(a1i)
    br, bi = a1r, a1i
    n = lseg
    while n:
        if n & 1:
            pr, pi = _cmul(pr, pi, br, bi)
        n >>= 1
        if n:
            br, bi = _cmul(br, bi, br, bi)
    cr, ci = jnp.zeros_like(a1r), jnp.zeros_like(a1i)
    for j in (range(N_SEG) if forward else range(N_SEG - 1, -1, -1)):
        s_re[j:j + 1, :] = cr
        s_im[j:j + 1, :] = ci
        nr, ni = _cmul(pr, pi, cr, ci)
        cr = nr + ere_ref[j:j + 1, :]
        ci = ni + eim_ref[j:j + 1, :]


def _scan_specs(lp, forward):
    lseg, tt, nt = _scan_tiles(lp)

    def tile(t):
        return t if forward else nt - 1 - t

    rows = lambda w: pl.BlockSpec((tt * N_SEG, w), lambda cb, t: (tile(t), cb))
    proj = pl.BlockSpec((1, CH_BLK, COL_BLK), lambda cb, t: (cb, 0, 0))
    slab = pl.BlockSpec((N_SEG, COL_BLK), lambda cb, t: (0, cb))
    return lseg, tt, nt, rows, proj, slab


def _scan_ends(name, urows, wre4, wim4, ar8, ai8, forward):
    lp = urows.shape[0]
    lseg, tt, nt, rows, proj, slab = _scan_specs(lp, forward)

    def body(u_ref, wre_ref, wim_ref, ar_ref, ai_ref, ere_o, eim_o, bre_s, bim_s, cr_s, ci_s):
        t = pl.program_id(1)

        @pl.when(t == 0)
        def _():
            cr_s[...] = jnp.zeros(cr_s.shape, F32)
            ci_s[...] = jnp.zeros(ci_s.shape, F32)

        u = u_ref[...]
        bre_s[...] = _mm(u, wre_ref[0])
        bim_s[...] = _mm(u, wim_ref[0])
        cr, ci = _scan_steps(tt, forward, bre_s, bim_s, ar_ref[...], ai_ref[...], (cr_s[...], ci_s[...]),
                             lambda r0, x, x_prev, extra: ())
        cr_s[...] = cr
        ci_s[...] = ci

        @pl.when(t == nt - 1)
        def _():
            ere_o[...] = cr
            eim_o[...] = ci

    return pl.pallas_call(
        body, name=name, grid=(N_COL_BLK, nt), in_specs=[rows(CH_BLK), proj, proj, slab, slab],
        out_specs=[slab, slab], out_shape=[jax.ShapeDtypeStruct((N_SEG, N_STATES), F32)] * 2,
        scratch_shapes=[pltpu.VMEM((tt * N_SEG, COL_BLK), F32)] * 2 + [pltpu.VMEM((N_SEG, COL_BLK), F32)] * 2,
        compiler_params=_cparams("arbitrary", "arbitrary"))(urows, wre4, wim4, ar8, ai8)


def _scan_fwd(name, urows, wre4, wim4, ar8, ai8, ere, eim, cre4, cim4, forward):
    lp = urows.shape[0]
    lseg, tt, nt, rows, proj, slab = _scan_specs(lp, forward)

    def body(u_ref, wre_ref, wim_ref, ar_ref, ai_ref, ere_ref, eim_ref, cre_ref, cim_ref,
             xre_o, xim_o, y_o, bre_s, bim_s, cr_s, ci_s):
        ar, ai = ar_ref[...], ai_ref[...]

        @pl.when(pl.program_id(1) == 0)
        def _():
            _segment_starts(lseg, forward, ar, ai, ere_ref, eim_ref, cr_s, ci_s)

        u = u_ref[...]
        bre_s[...] = _mm(u, wre_ref[0])
        bim_s[...] = _mm(u, wim_ref[0])

        def visit(r0, x, x_prev, extra):
            xre_o[pl.ds(r0, N_SEG), :] = x[0]
            xim_o[pl.ds(r0, N_SEG), :] = x[1]
            return ()

        cr, ci = _scan_steps(tt, forward, bre_s, bim_s, ar, ai, (cr_s[...], ci_s[...]), visit)
        cr_s[...] = cr
        ci_s[...] = ci
        y_o[...] = _mm_nt(xre_o[...], cre_ref[0]) + _mm_nt(xim_o[...], cim_ref[0])

    return pl.pallas_call(
        body, name=name, grid=(N_COL_BLK, nt),
        in_specs=[rows(CH_BLK), proj, proj, slab, slab, slab, slab, proj, proj],
        out_specs=[rows(COL_BLK), rows(COL_BLK), rows(CH_BLK)],
        out_shape=[jax.ShapeDtypeStruct((lp, N_STATES), F32)] * 2 + [jax.ShapeDtypeStruct((lp, D_SSM), F32)],
        scratch_shapes=[pltpu.VMEM((tt * N_SEG, COL_BLK), F32)] * 2 + [pltpu.VMEM((N_SEG, COL_BLK), F32)] * 2,
        compiler_params=_cparams("arbitrary", "arbitrary"))(urows, wre4, wim4, ar8, ai8, ere, eim, cre4, cim4)


def _scan_bwd(name, dyrows, cre4, cim4, ar8, ai8, ere, eim, urows, wre4, wim4, xre, xim, forward):
    lp = urows.shape[0]
    lseg, tt, nt, rows, proj, slab = _scan_specs(lp, forward)

    def body(dy_ref, cre_ref, cim_ref, ar_ref, ai_ref, ere_ref, eim_ref, u_ref, wre_ref, wim_ref, xre_ref, xim_ref,
             du_o, dwre_o, dwim_o, dcre_o, dcim_o, dare_o, daim_o, bre_s, bim_s, gre_s, gim_s, cr_s, ci_s):
        t = pl.program_id(1)
        ar, ai = ar_ref[...], ai_ref[...]

        @pl.when(t == 0)
        def _():
            _segment_starts(lseg, forward, ar, ai, ere_ref, eim_ref, cr_s, ci_s)
            dare_o[...] = jnp.zeros(dare_o.shape, F32)
            daim_o[...] = jnp.zeros(daim_o.shape, F32)

        dy = dy_ref[...]
        bre_s[...] = _mm(dy, cre_ref[0])
        bim_s[...] = _mm(dy, cim_ref[0])

        def visit(r0, g, g_prev, sums):
            gre_s[pl.ds(r0, N_SEG), :] = g[0]
            gim_s[pl.ds(r0, N_SEG), :] = g[1]
            fr = xre_ref[pl.ds(r0, N_SEG), :]
            fi = xim_ref[pl.ds(r0, N_SEG), :]
            pr, pi = g_prev
            return sums[0] + fr * pr + fi * pi, sums[1] + fr * pi - fi * pr

        out = _scan_steps(tt, forward, bre_s, bim_s, ar, ai, (cr_s[...], ci_s[...], dare_o[...], daim_o[...]), visit)
        cr_s[...] = out[0]
        ci_s[...] = out[1]
        dare_o[...] = out[2]
        daim_o[...] = out[3]
        gre, gim = gre_s[...], gim_s[...]
        du_o[...] = _mm_nt(gre, wre_ref[0]) + _mm_nt(gim, wim_ref[0])
        u = u_ref[...]
        first = t == 0
        _acc(dwre_o, _mm_tn(u, gre)[None], first)
        _acc(dwim_o, _mm_tn(u, gim)[None], first)
        _acc(dcre_o, _mm_tn(dy, xre_ref[...])[None], first)
        _acc(dcim_o, _mm_tn(dy, xim_ref[...])[None], first)

    big = pltpu.VMEM((tt * N_SEG, COL_BLK), F32)
    small = pltpu.VMEM((N_SEG, COL_BLK), F32)
    return pl.pallas_call(
        body, name=name, grid=(N_COL_BLK, nt),
        in_specs=[rows(CH_BLK), proj, proj, slab, slab, slab, slab, rows(CH_BLK), proj, proj,
                  rows(COL_BLK), rows(COL_BLK)],
        out_specs=[rows(CH_BLK), proj, proj, proj, proj, slab, slab],
        out_shape=[jax.ShapeDtypeStruct((lp, D_SSM), F32)]
        + [jax.ShapeDtypeStruct((N_COL_BLK, CH_BLK, COL_BLK), F32)] * 4
        + [jax.ShapeDtypeStruct((N_SEG, N_STATES), F32)] * 2,
        scratch_shapes=[big, big, big, big, small, small],
        compiler_params=_cparams("arbitrary", "arbitrary"))(
            dyrows, cre4, cim4, ar8, ai8, ere, eim, urows, wre4, wim4, xre, xim)


def _ssm_post(yf, yb, u, sg, wglu_b, bglu, sw, dvec, tr):
    def body(yf_ref, yb_ref, u_ref, g_ref, w_ref, b_ref, sw_ref, d_ref, ypre_o, glu_o, ysn_o):
        ypre = yf_ref[...] + yb_ref[...] + d_ref[...] * u_ref[...]
        ypre_o[...] = ypre
        glu = _mm(_gelu_and_grad(ypre)[0], w_ref[...]) + b_ref[...]
        glu_o[...] = glu
        t = glu[:, :D_SSM] * _sigmoid(glu[:, D_SSM:]) * _silu_and_grad(g_ref[...])[0]
        ysn_o[...] = t * _rms(t) * sw_ref[...]

    return _rows_call("ssm_post", body, tr, [yf, yb, u, sg], [wglu_b, bglu, sw, dvec],
                      [(512, F32), (1024, F32), (512, F32)], [])


def _out_loss(ya, ysn, h, tgt, wo_b, post_w, n_valid, tr):
    lseg = h.shape[0] // N_SEG

    def body(ya_ref, ys_ref, h_ref, t_ref, w_ref, pw_ref, dy_o, dout_o, loss_o, dpw_o):
        i = pl.program_id(0)
        y = _mm(ya_ref[...], w_ref[0:D_ATTN, :]) + _mm(ys_ref[...], w_ref[D_ATTN:, :])
        r = _rms(y)
        pw = pw_ref[...]
        out = h_ref[...] + y * r * pw
        pos = _row_position(i * tr + lax.broadcasted_iota(jnp.int32, (tr, 1), 0), lseg)
        valid = jnp.logical_and(pos >= N_META, pos < n_valid)
        diff = jnp.where(valid, out - t_ref[...], 0.0)
        dout = diff * (1.0 / D_MODEL)
        dy, dpw = _rms_bwd(dout, y, r, pw)
        dy_o[...] = dy
        dout_o[...] = dout
        _acc(loss_o, 0.5 * jnp.sum(jnp.sum(diff * diff, axis=1, keepdims=True), axis=0, keepdims=True)
             * (1.0 / D_MODEL), i == 0)
        _acc(dpw_o, dpw, i == 0)

    return _rows_call("out_loss", body, tr, [ya, ysn, h, tgt], [wo_b, post_w], [(1024, F32), (1024, F32)],
                      [((1, 1), F32), ((1, D_MODEL), F32)])


def _out_bwd(dy, ya, ysn, o_flat, ag, wo_b, aw, head_sum, tr):
    def body(dy_ref, ya_ref, ys_ref, o_ref, g_ref, w_ref, aw_ref, hs_ref, do_o, dag_o, dysn_o, dl_o, dwo_o, daw_o):
        i = pl.program_id(0)
        dy = dy_ref[...]
        dcat = _mm_nt(dy, w_ref[...])
        cat = jnp.concatenate([ya_ref[...], ys_ref[...]], axis=1)
        _acc(dwo_o, _mm_tn(cat, dy), i == 0)
        dysn_o[...] = dcat[:, D_ATTN:]
        o = o_ref[...]
        sl, dsl = _silu_and_grad(g_ref[...])
        t = o * sl
        dt, daw = _rms_bwd(dcat[:, :D_ATTN], t, _rms(t), aw_ref[...])
        _acc(daw_o, daw, i == 0)
        do = dt * sl
        do_o[...] = do
        dag_o[...] = dt * o * dsl
        dl_o[...] = _mm_exact(do * o, hs_ref[...])

    return _rows_call("out_bwd", body, tr, [dy, ya, ysn, o_flat, ag], [wo_b, aw, head_sum],
                      [(512, F32), (512, F32), (512, F32), (HEADS, F32)],
                      [((D_MODEL, D_MODEL), F32), ((1, D_ATTN), F32)])


def _ssm_post_bwd(dysn, glu, sg, ypre, u, wglu_b, sw, dvec, tr):
    def body(d_ref, glu_ref, sg_ref, y_ref, u_ref, w_ref, sw_ref, dv_ref,
             dyp_o, dsg_o, dwg_o, dbg_o, dsw_o, dd_o):
        i = pl.program_id(0)
        glu = glu_ref[...]
        a, b = glu[:, :D_SSM], glu[:, D_SSM:]
        sb = _sigmoid(b)
        ys = a * sb
        sl, dsl = _silu_and_grad(sg_ref[...])
        t = ys * sl
        dt, dsw = _rms_bwd(d_ref[...], t, _rms(t), sw_ref[...])
        _acc(dsw_o, dsw, i == 0)
        dsg_o[...] = dt * ys * dsl
        dys = dt * sl
        dglu = jnp.concatenate([dys * sb, dys * a * sb * (1.0 - sb)], axis=1)
        _acc(dbg_o, jnp.sum(dglu, axis=0, keepdims=True), i == 0)
        gel, dgel = _gelu_and_grad(y_ref[...])
        _acc(dwg_o, _mm_tn(gel, dglu), i == 0)
        dyp = _mm_nt(dglu, w_ref[...]) * dgel
        dyp_o[...] = dyp
        _acc(dd_o, jnp.sum(dyp * u_ref[...], axis=0, keepdims=True), i == 0)

    return _rows_call("ssm_post_bwd", body, tr, [dysn, glu, sg, ypre, u], [wglu_b, sw, dvec],
                      [(512, F32), (512, F32)],
                      [((D_SSM, 2 * D_SSM), F32), ((1, 2 * D_SSM), F32), ((1, D_SSM), F32), ((1, D_SSM), F32)])


def _attn_bwd(qt, k, kt, v, dot, lse_t, delta_t, tk):
    _, nq, _, tq = qt.shape
    lp = k.shape[1]
    nk = lp // tk
    assert lse_t.shape == (HEADS, nq, 1, tq) and delta_t.shape == (HEADS, nq, 1, tq)

    def body(q_ref, k_ref, kt_ref, v_ref, do_ref, lse_ref, dl_ref, dq_o, dk_o, dv_o, dk_s, dv_s):
        @pl.when(pl.program_id(1) == 0)
        def _():
            dq_o[...] = jnp.zeros(dq_o.shape, F32)

        dk_s[...] = jnp.zeros(dk_s.shape, F32)
        dv_s[...] = jnp.zeros(dv_s.shape, F32)
        kk = k_ref[0]
        kkt = kt_ref[0]
        vv = v_ref[0]

        def chunk(c, carry):
            qq = q_ref[0, c]
            dd = do_ref[0, c]
            pt = jnp.exp2(_mm(kk, qq) - lse_ref[0, c])
            dv_s[...] += _mm_nt(dd, pt)
            dst = (pt * (_mm(vv, dd) - dl_ref[0, c])).astype(BF16)
            dk_s[...] += _mm_nt(qq, dst)
            dq_o[0, c] += _mm(kkt, dst)
            return carry

        n_loop = nq - nq % ATTN_BWD_UNROLL
        if n_loop:
            lax.fori_loop(0, n_loop, chunk, 0, unroll=ATTN_BWD_UNROLL)
        for c in range(n_loop, nq):
            chunk(c, 0)
        dk_o[0] = dk_s[...]
        dv_o[0] = dv_s[...]

    head = lambda w: pl.BlockSpec((1, nq, w, tq), lambda h, j: (h, 0, 0, 0))
    rows = lambda w: pl.BlockSpec((1, tk, w), lambda h, j: (h, j, 0))
    cols = lambda w: pl.BlockSpec((1, w, tk), lambda h, j: (h, 0, j))
    return pl.pallas_call(
        body, name="attn_bwd", grid=(HEADS, nk),
        in_specs=[head(QK_DIM), rows(QK_DIM), cols(QK_DIM), rows(V_HEAD), head(V_HEAD), head(1), head(1)],
        out_specs=[head(QK_DIM), cols(QK_DIM), cols(V_HEAD)],
        out_shape=[jax.ShapeDtypeStruct((HEADS, nq, QK_DIM, tq), F32), jax.ShapeDtypeStruct((HEADS, QK_DIM, lp), F32),
                   jax.ShapeDtypeStruct((HEADS, V_HEAD, lp), F32)],
        scratch_shapes=[pltpu.VMEM((QK_DIM, tk), F32), pltpu.VMEM((V_HEAD, tk), F32)],
        compiler_params=_cparams("arbitrary", "arbitrary"))(qt, k, kt, v, dot, lse_t, delta_t)


def _qkv_up_bwd(dqn, dr1, dr2, dkn, dv, dkr8, ql, kvl, cos8, sin8, c32, s32, qw, kvw, wq_b, wkv_b, p32, sum8, tr):
    def body(dqn_ref, dr1_ref, dr2_ref, dkn_ref, dv_ref, dkr_ref, ql_ref, kvl_ref, cos_ref, sin_ref, c32_ref,
             s32_ref, qw_ref, kvw_ref, wq_ref, wkv_ref, p_ref, s8_ref,
             dql_o, dkvl_o, dkrr_o, dwq_o, dwkv_o, dqw_o, dkvw_o):
        i = pl.program_id(0)
        cs, sn = cos_ref[...], sin_ref[...]
        d1, d2 = dr1_ref[...], dr2_ref[...]
        dq = jnp.concatenate([dqn_ref[...], d1 * cs + d2 * sn, d2 * cs - d1 * sn], axis=1) * (Q_SCALE / LOG2E)
        x = ql_ref[...]
        r = _rms(x)
        qw = qw_ref[...]
        _acc(dwq_o, _mm_tn(x * r * qw, dq), i == 0)
        dx, dw = _rms_bwd(_mm_nt(dq, wq_ref[...]), x, r, qw)
        dql_o[...] = dx
        _acc(dqw_o, dw, i == 0)
        dkv = jnp.concatenate([dkn_ref[...] * (1.0 / LOG2E), dv_ref[...]], axis=1)
        x = kvl_ref[...]
        r = _rms(x)
        kvw = kvw_ref[...]
        _acc(dwkv_o, _mm_tn(x * r * kvw, dkv), i == 0)
        dx, dw = _rms_bwd(_mm_nt(dkv, wkv_ref[...]), x, r, kvw)
        dkvl_o[...] = dx
        _acc(dkvw_o, dw, i == 0)
        dkr = _mm_exact(dkr_ref[...], s8_ref[...]) * (1.0 / LOG2E)
        dkrr_o[...] = dkr * c32_ref[...] + _mm_exact(dkr * s32_ref[...], p_ref[...])

    return _rows_call("qkv_up_bwd", body, tr, [dqn, dr1, dr2, dkn, dv, dkr8, ql, kvl, cos8, sin8, c32, s32],
                      [qw, kvw, wq_b, wkv_b, p32, sum8], [(256, F32), (128, F32), (32, F32)],
                      [((Q_LORA, 768), F32), ((KV_LORA, 1024), F32), ((1, Q_LORA), F32), ((1, KV_LORA), F32)])


def _inproj_bwd(dql, dkvl, dag, du_f, du_b, dypre, dsg, dkr, h, dout, pre_w, w_in_b, dvec, tr):
    def body(dql_ref, dkvl_ref, dag_ref, duf_ref, dub_ref, dyp_ref, dsg_ref, dkr_ref, h_ref, dout_ref,
             pw_ref, w_ref, dv_ref, dh_o, dwin_o, dpw_o):
        i = pl.program_id(0)
        du = duf_ref[...] + dub_ref[...] + dv_ref[...] * dyp_ref[...]
        dproj = jnp.concatenate([dql_ref[...], dkvl_ref[...], dag_ref[...], du, dsg_ref[...],
                                 dkr_ref[...], jnp.zeros((tr, D_IN_PAD - D_IN), F32)], axis=1)
        x = h_ref[...]
        r = _rms(x)
        pw = pw_ref[...]
        _acc(dwin_o, _mm_tn(x * r * pw, dproj), i == 0)
        dx, dw = _rms_bwd(_mm_nt(dproj, w_ref[...]), x, r, pw)
        _acc(dpw_o, dw, i == 0)
        dh_o[...] = dout_ref[...] + dx

    return _rows_call("inproj_bwd", body, tr, [dql, dkvl, dag, du_f, du_b, dypre, dsg, dkr, h, dout],
                      [pre_w, w_in_b, dvec], [(1024, F32)], [((D_MODEL, D_IN_PAD), F32), ((1, D_MODEL), F32)])


def _disc_terms(a_re, a_im, ldt):
    dt = jnp.exp(ldt)
    mag = jnp.exp(a_re * dt)
    th = a_im * dt
    cs, sn = jnp.cos(th), jnp.sin(th)
    abar_re, abar_im = mag * cs, mag * sn
    num_re, num_im = abar_re - 1.0, abar_im
    den = a_re * a_re + a_im * a_im
    coef_re = (num_re * a_re + num_im * a_im) / den
    coef_im = (num_im * a_re - num_re * a_im) / den
    return dt, mag, cs, sn, abar_re, abar_im, num_re, num_im, den, coef_re, coef_im


def _ssm_disc(a_re, a_im, ldt, bt_re, bt_im):
    def body(ar_ref, ai_ref, l_ref, br_ref, bi_ref, abr_o, abi_o, bbr_o, bbi_o):
        t = _disc_terms(ar_ref[...], ai_ref[...], l_ref[...])
        abr_o[...] = t[4]
        abi_o[...] = t[5]
        cr, ci = t[9], t[10]
        br, bi = br_ref[...], bi_ref[...]
        bbr_o[...] = cr * br - ci * bi
        bbi_o[...] = cr * bi + ci * br

    ng = a_re.shape[0]
    return pl.pallas_call(
        body, name="ssm_disc",
        out_shape=[jax.ShapeDtypeStruct((ng, 1, SSM_STATE), F32)] * 2
        + [jax.ShapeDtypeStruct((ng, SSM_GROUP, SSM_STATE), F32)] * 2)(a_re, a_im, ldt, bt_re, bt_im)


def _ssm_disc_bwd(a_re, a_im, ldt, bt_re, bt_im, da8_re, da8_im, dbb_re, dbb_im):
    def body(ar_ref, ai_ref, l_ref, br_ref, bi_ref, dar_ref, dai_ref, dbr_ref, dbi_ref,
             gar_o, gai_o, gl_o, gbr_o, gbi_o):
        a_re, a_im = ar_ref[...], ai_ref[...]
        dt, mag, cs, sn, abar_re, abar_im, num_re, num_im, den, cr, ci = _disc_terms(a_re, a_im, l_ref[...])
        br, bi = br_ref[...], bi_ref[...]
        dbr, dbi = dbr_ref[...], dbi_ref[...]
        gbr_o[...] = cr * dbr + ci * dbi
        gbi_o[...] = cr * dbi - ci * dbr
        dcr = jnp.sum(br * dbr + bi * dbi, axis=1, keepdims=True)
        dci = jnp.sum(br * dbi - bi * dbr, axis=1, keepdims=True)
        dnum_re = (dcr * a_re - dci * a_im) / den
        dnum_im = (dcr * a_im + dci * a_re) / den
        dden = -(dcr * cr + dci * ci) / den
        g_are = (dcr * num_re + dci * num_im) / den + dden * 2.0 * a_re
        g_aim = (dcr * num_im - dci * num_re) / den + dden * 2.0 * a_im
        d_abr = jnp.sum(dar_ref[...], axis=1, keepdims=True) + dnum_re
        d_abi = jnp.sum(dai_ref[...], axis=1, keepdims=True) + dnum_im
        dmag = d_abr * cs + d_abi * sn
        dth = d_abi * abar_re - d_abr * abar_im
        g_are = g_are + dmag * mag * dt
        g_aim = g_aim + dth * dt
        ddt = jnp.sum(dmag * mag * a_re + dth * a_im, axis=2, keepdims=True)
        gar_o[...] = g_are
        gai_o[...] = g_aim
        gl_o[...] = ddt * dt

    ng = a_re.shape[0]
    return pl.pallas_call(
        body, name="ssm_disc_bwd",
        out_shape=[jax.ShapeDtypeStruct((ng, 1, SSM_STATE), F32)] * 2 + [jax.ShapeDtypeStruct((ng, 1, 1), F32)]
        + [jax.ShapeDtypeStruct((ng, SSM_GROUP, SSM_STATE), F32)] * 2)(
            a_re, a_im, ldt, bt_re, bt_im, da8_re, da8_im, dbb_re, dbb_im)


def _exchange(name, per_peer, shared):
    parts = [a for a in (per_peer, shared) if a is not None]
    rp = per_peer.shape[1] if per_peer is not None else 0
    rs = shared.shape[0] if shared is not None else 0
    n = len(parts)

    def body(*refs):
        in_refs, out_ref, send_sems, recv_sems, local_sems = refs[:n], refs[n], refs[n + 1], refs[n + 2], refs[n + 3]
        x, y, c = lax.axis_index("x"), lax.axis_index("y"), lax.axis_index("c")
        me = 4 * x + 2 * y + c

        def pieces(peer):
            out = []
            if per_peer is not None:
                out.append((in_refs[0].at[peer], out_ref.at[me, pl.ds(0, rp), :]))
            if shared is not None:
                out.append((in_refs[-1], out_ref.at[me, pl.ds(rp, rs), :]))
            return out

        copies = []
        for k in range(1, N_DEV):
            px = 1 - x if (k >> 2) & 1 else x
            py = 1 - y if (k >> 1) & 1 else y
            pc = 1 - c if k & 1 else c
            for j, (src, dst) in enumerate(pieces(4 * px + 2 * py + pc)):
                s = (k - 1) * n + j
                copies.append(pltpu.make_async_remote_copy(
                    src_ref=src, dst_ref=dst, send_sem=send_sems.at[s], recv_sem=recv_sems.at[s],
                    device_id=(px, py, pc), device_id_type=pl.DeviceIdType.MESH))
        mine = [pltpu.make_async_copy(src, dst, local_sems.at[j]) for j, (src, dst) in enumerate(pieces(me))]
        for cp in mine + copies:
            cp.start()
        for cp in copies + mine:
            cp.wait()

    n_sem = (N_DEV - 1) * n
    return pl.pallas_call(
        body, name=name, out_shape=jax.ShapeDtypeStruct((N_DEV, rp + rs, LANES), F32),
        in_specs=[pl.BlockSpec(memory_space=pl.ANY)] * n, out_specs=pl.BlockSpec(memory_space=pl.ANY),
        scratch_shapes=[pltpu.SemaphoreType.DMA((n_sem,)), pltpu.SemaphoreType.DMA((n_sem,)),
                        pltpu.SemaphoreType.DMA((n,))])(*parts)


def _adamw(recv, w, m, v, tr):
    rows = w.shape[0]
    c1 = 1.0 - ADAM_B1 ** ADAM_STEP
    c2 = 1.0 - ADAM_B2 ** ADAM_STEP

    def body(r_ref, w_ref, m_ref, v_ref, g_o, d_o, m_o, v_o):
        g = r_ref[0]
        for k in range(1, N_DEV):
            g = g + r_ref[k]
        mm = ADAM_B1 * m_ref[...] + (1.0 - ADAM_B1) * g
        vv = ADAM_B2 * v_ref[...] + (1.0 - ADAM_B2) * (g * g)
        g_o[...] = g
        m_o[...] = mm
        v_o[...] = vv
        d_o[...] = -ADAM_LR * ((mm / c1) / (jnp.sqrt(vv / c2) + ADAM_EPS) + ADAM_WD * w_ref[...])

    spec = pl.BlockSpec((tr, LANES), lambda i: (i, 0))
    return pl.pallas_call(
        body, name="adamw", grid=(rows // tr,),
        in_specs=[pl.BlockSpec((N_DEV, tr, LANES), lambda i: (0, i, 0)), spec, spec, spec],
        out_specs=[spec] * 4, out_shape=[jax.ShapeDtypeStruct((rows, LANES), F32)] * 4,
        compiler_params=_cparams("arbitrary"))(recv, w, m, v)


def _to_rows(a):
    flat = a.reshape(-1)
    pad = (-flat.shape[0]) % LANES
    if pad:
        flat = jnp.concatenate([flat, jnp.zeros((pad,), flat.dtype)])
    return flat.reshape(-1, LANES)


def _n_rows(shape):
    return -(-int(np.prod(shape)) // LANES)


def _pack(arrays, total_rows):
    rows = [_to_rows(a) for a in arrays]
    used = sum(r.shape[0] for r in rows)
    if total_rows > used:
        rows.append(jnp.zeros((total_rows - used, LANES), F32))
    return jnp.concatenate(rows, axis=0)


def _unpack(buf, shapes):
    lead = buf.shape[:-2]
    out, r0 = [], 0
    for s in shapes:
        n = int(np.prod(s))
        nr = _n_rows(s)
        out.append(buf[..., r0:r0 + nr, :].reshape(lead + (-1,))[..., :n].reshape(lead + tuple(s)))
        r0 += nr
    return out


def _pack_per_device(arrays, total_rows):
    rows = []
    for a in arrays:
        flat = a.reshape(N_DEV, -1)
        pad = (-flat.shape[1]) % LANES
        if pad:
            flat = jnp.concatenate([flat, jnp.zeros((N_DEV, pad), flat.dtype)], axis=1)
        rows.append(flat.reshape(N_DEV, -1, LANES))
    used = sum(r.shape[1] for r in rows)
    if total_rows > used:
        rows.append(jnp.zeros((N_DEV, total_rows - used, LANES), F32))
    return jnp.concatenate(rows, axis=1)


def _shard_views(name, full):
    if name == 'w_out':
        return full.reshape(N_DEV, full.shape[0] // N_DEV, full.shape[1])
    r, ccols = full.shape
    return full.reshape(r, N_DEV, ccols // N_DEV).transpose(1, 0, 2)


def _from_shards(name, stacked):
    if name == 'w_out':
        return stacked.reshape(-1, stacked.shape[-1])
    n, r, cc = stacked.shape
    return stacked.transpose(1, 0, 2).reshape(r, n * cc)


GROUPS_PER_BLK = N_GROUPS // N_COL_BLK


def _block_diag(t):
    eye = jnp.eye(GROUPS_PER_BLK, dtype=t.dtype)
    t4 = t.reshape(N_COL_BLK, GROUPS_PER_BLK, SSM_GROUP, SSM_STATE)
    return (t4[:, :, :, None, :] * eye[None, :, None, :, None]).reshape(N_COL_BLK, CH_BLK, COL_BLK)


def _diag_blocks(mat4):
    eye = jnp.eye(GROUPS_PER_BLK, dtype=mat4.dtype)
    m6 = mat4.reshape(N_COL_BLK, GROUPS_PER_BLK, SSM_GROUP, GROUPS_PER_BLK, SSM_STATE)
    return (m6 * eye[None, :, None, :, None]).sum(axis=3).reshape(N_GROUPS, SSM_GROUP, SSM_STATE)


def _step(x, loss_target, wts, moms, vels):
    seq = x.shape[1]
    n_valid = N_META + seq
    lp = -(-n_valid // 256) * 256
    tr = _pick(lp, [640, 256])
    tr_mid = 256
    tq = _pick(lp, [1280, 256])
    tk = _pick(lp, [640, 256])

    shard_shapes = [wts[n].shape[-2:] for n in SHARDED]
    n_shard_rows = sum(_n_rows(s) for s in shard_shapes)
    gathered = _exchange("gather_weights", None,
                         _pack([wts[n].reshape(wts[n].shape[-2:]) for n in SHARDED], n_shard_rows))
    full = {n: _from_shards(n, a) for n, a in zip(SHARDED, _unpack(gathered, shard_shapes))}

    w_in_b = jnp.concatenate([_cols_in(full['w_in']), jnp.zeros((D_MODEL, D_IN_PAD - D_IN), F32)],
                             axis=1).astype(BF16)
    wq_b = _cols_q(full['w_q_up']).astype(BF16)
    wkv_b = _cols_kv(full['w_kv_up']).astype(BF16)
    wglu_b = full['w_glu'].astype(BF16)
    wo_b = full['w_out'].astype(BF16)
    pre_w, post_w = wts['pre_norm_w'], wts['post_norm_w']
    qw, kvw, aw, sw = wts['q_norm_w'], wts['kv_norm_w'], wts['attn_out_norm_w'], wts['ssm_out_norm_w']
    bglu, dvec = wts['b_glu'], wts['ssm_d']

    lseg = lp // N_SEG
    pos = _row_position(jnp.arange(lp, dtype=jnp.int32), lseg)
    inv = ROPE_THETA ** (-jnp.arange(HALF_ROPE, dtype=F32) / HALF_ROPE)
    ang = pos.astype(F32)[:, None] * inv[None, :]
    cos, sin = jnp.cos(ang), jnp.sin(ang)
    cos8, sin8 = jnp.tile(cos, (1, HEADS)), jnp.tile(sin, (1, HEADS))
    c32 = jnp.concatenate([cos, cos], axis=1)
    s32 = jnp.concatenate([-sin, sin], axis=1)
    p32 = jnp.asarray(np.roll(np.eye(QK_ROPE, dtype=np.float32), HALF_ROPE, axis=1))
    sum8 = jnp.asarray(np.tile(np.eye(QK_ROPE, dtype=np.float32), (HEADS, 1)))
    head_sum = jnp.asarray(np.repeat(np.eye(HEADS, dtype=np.float32), V_HEAD, axis=0))

    ng = 2 * N_GROUPS
    a_re3 = wts['ssm_a_re'].reshape(ng, 1, SSM_STATE)
    a_im3 = wts['ssm_a_im'].reshape(ng, 1, SSM_STATE)
    ldt3 = wts['ssm_log_dt'].reshape(ng, 1, 1)
    bt_re = wts['ssm_b_re'].reshape(2, N_GROUPS, SSM_STATE, SSM_GROUP).transpose(0, 1, 3, 2).reshape(
        ng, SSM_GROUP, SSM_STATE)
    bt_im = wts['ssm_b_im'].reshape(2, N_GROUPS, SSM_STATE, SSM_GROUP).transpose(0, 1, 3, 2).reshape(
        ng, SSM_GROUP, SSM_STATE)
    c_re = wts['ssm_c_re'].reshape(ng, SSM_GROUP, SSM_STATE)
    c_im = wts['ssm_c_im'].reshape(ng, SSM_GROUP, SSM_STATE)
    abar_re, abar_im, bbt_re, bbt_im = _ssm_disc(a_re3, a_im3, ldt3, bt_re, bt_im)

    def direction(t, d):
        return t[d * N_GROUPS:(d + 1) * N_GROUPS]

    def slab(t, d, sign=1.0):
        return jnp.broadcast_to(sign * direction(t, d).reshape(1, N_STATES), (N_SEG, N_STATES))

    w_re = [_block_diag(direction(bbt_re, d)).astype(BF16) for d in range(2)]
    w_im = [_block_diag(direction(bbt_im, d)).astype(BF16) for d in range(2)]
    cb_re = [_block_diag(direction(c_re, d)).astype(BF16) for d in range(2)]
    cb_im = [_block_diag(-direction(c_im, d)).astype(BF16) for d in range(2)]

    def to_rows(a):
        return a.reshape(N_SEG, lseg, a.shape[-1]).transpose(1, 0, 2).reshape(lp, a.shape[-1])

    def to_tokens(a):
        return a.reshape(lseg, N_SEG, a.shape[-1]).transpose(1, 0, 2).reshape(lp, a.shape[-1])

    pad = jnp.zeros((lp - n_valid, D_MODEL), F32)
    h = to_rows(jnp.concatenate([full['meta_tokens'], x[0], pad], axis=0))
    tgt = to_rows(jnp.concatenate([jnp.zeros((N_META, D_MODEL), F32), loss_target[0], pad], axis=0))

    ql, kvl, ag, su, sg, kr = _inproj(h, pre_w, w_in_b, tr)
    qn_b, qr1_b, qr2_b, kn_b, v_b, kr_b = _qkv_up(ql, kvl, kr, cos8, sin8, c32, s32, qw, kvw, wq_b, wkv_b, p32, tr)

    def heads(a, w):
        return a.reshape(lp, HEADS, w)

    nq, nk = lp // tq, lp // tk
    q_t = jnp.concatenate([heads(qn_b, 64), heads(qr1_b, 16), heads(qr2_b, 16)], axis=-1)
    k_t = jnp.concatenate([heads(kn_b, 64), jnp.broadcast_to(kr_b[:, None, :], (lp, HEADS, QK_ROPE))], axis=-1)
    v_t = heads(v_b, 64)
    vx_t = jnp.concatenate([v_t, jnp.ones((lp, HEADS, 1), BF16), jnp.zeros((lp, HEADS, LANES - V_HEAD - 1), BF16)],
                           axis=-1)
    qt4 = q_t.reshape(nq, tq, HEADS, QK_DIM).transpose(2, 0, 3, 1)
    tk_fwd = _pick(lp, [1280, 256])
    vxt4 = vx_t.reshape(lp // tk_fwd, tk_fwd, HEADS, LANES).transpose(2, 0, 3, 1)
    k_h = k_t.transpose(1, 0, 2)
    kt_h = k_t.transpose(1, 2, 0)
    v_h = v_t.transpose(1, 0, 2)
    ot_h, lse4 = _attn_fwd(qt4, k_h, vxt4, n_valid)
    o_flat = ot_h.transpose(2, 0, 1).reshape(lp, D_ATTN)
    ya = _attn_post(o_flat, ag, aw, tr)

    xs, ys = [], []
    for d in range(2):
        ar8, ai8 = slab(abar_re, d), slab(abar_im, d)
        ere, eim = _scan_ends(f"scan{d}_ends", su, w_re[d], w_im[d], ar8, ai8, d == 0)
        x_re, x_im, y_d = _scan_fwd(f"scan{d}", su, w_re[d], w_im[d], ar8, ai8, ere, eim, cb_re[d], cb_im[d],
                                    d == 0)
        xs += [x_re, x_im]
        ys.append(y_d)
    ypre, glu, ysn = _ssm_post(ys[0], ys[1], su, sg, wglu_b, bglu, sw, dvec, tr)

    dy, dout, loss, d_post = _out_loss(ya, ysn, h, tgt, wo_b, post_w, n_valid, tr)

    do_flat, dag, dysn, delta8, d_wo, d_aw = _out_bwd(dy, ya, ysn, o_flat, ag, wo_b, aw, head_sum, tr)
    dypre, dsg, d_wglu, d_bglu, d_sw, d_dvec = _ssm_post_bwd(dysn, glu, sg, ypre, su, wglu_b, sw, dvec, tr)

    dus, d_ct, d_wb, d_a8 = [], [], [], []
    for d in range(2):
        ar8, ai8c = slab(abar_re, d), slab(abar_im, d, -1.0)
        ere, eim = _scan_ends(f"scan_adj{d}_ends", dypre, cb_re[d], cb_im[d], ar8, ai8c, d != 0)
        du_d, dw_re, dw_im, dc_re, dc_im, da_re, da_im = _scan_bwd(
            f"scan_adj{d}", dypre, cb_re[d], cb_im[d], ar8, ai8c, ere, eim, su, w_re[d], w_im[d],
            xs[2 * d], xs[2 * d + 1], d != 0)
        dus.append(du_d)
        d_ct.append((dc_re, dc_im))
        d_wb.append((dw_re, dw_im))
        d_a8.append((da_re, da_im))

    dot4 = do_flat.astype(BF16).reshape(nq, tq, HEADS, V_HEAD).transpose(2, 0, 3, 1)
    dqt4, dkt_h, dvt_h = _attn_bwd(qt4, k_h, kt_h, v_h, dot4, lse4, delta8.T.reshape(HEADS, nq, 1, tq), tk)
    dq_t = dqt4.transpose(1, 3, 0, 2).reshape(lp, HEADS, QK_DIM)
    dk_t = dkt_h.transpose(2, 0, 1)
    dqn = dq_t[:, :, :64].reshape(lp, 512)
    dr1 = dq_t[:, :, 64:80].reshape(lp, 128)
    dr2 = dq_t[:, :, 80:96].reshape(lp, 128)
    dkn = dk_t[:, :, :64].reshape(lp, 512)
    dkr8 = dk_t[:, :, 64:].reshape(lp, HEADS * QK_ROPE)
    dvf = dvt_h.transpose(2, 0, 1).reshape(lp, 512)
    dql, dkvl, dkrr, d_wq, d_wkv, d_qw, d_kvw = _qkv_up_bwd(
        dqn, dr1, dr2, dkn, dvf, dkr8, ql, kvl, cos8, sin8, c32, s32, qw, kvw, wq_b, wkv_b, p32, sum8, tr)
    dh, d_win, d_pre = _inproj_bwd(dql, dkvl, dag, dus[0], dus[1], dypre, dsg, dkrr, h, dout, pre_w, w_in_b, dvec,
                                   tr_mid)
    dh = to_tokens(dh)

    def seg_sums(t):
        return t.reshape(N_SEG, N_GROUPS, SSM_STATE).transpose(1, 0, 2)

    da8_re = jnp.concatenate([seg_sums(d_a8[d][0]) for d in range(2)], axis=0)
    da8_im = jnp.concatenate([seg_sums(d_a8[d][1]) for d in range(2)], axis=0)
    dbb_re = jnp.concatenate([_diag_blocks(d_wb[d][0]) for d in range(2)], axis=0)
    dbb_im = jnp.concatenate([_diag_blocks(d_wb[d][1]) for d in range(2)], axis=0)
    g_are, g_aim, g_ldt, g_bt_re, g_bt_im = _ssm_disc_bwd(a_re3, a_im3, ldt3, bt_re, bt_im, da8_re, da8_im,
                                                          dbb_re, dbb_im)
    g_c_re = jnp.concatenate([_diag_blocks(d_ct[d][0]) for d in range(2)], axis=0)
    g_c_im = jnp.concatenate([-_diag_blocks(d_ct[d][1]) for d in range(2)], axis=0)

    def b_layout(t):
        return t.reshape(2, N_GROUPS, SSM_GROUP, SSM_STATE).transpose(0, 1, 3, 2)

    local = {
        'meta_tokens': dh[:N_META],
        'pre_norm_w': d_pre, 'post_norm_w': d_post,
        'w_in': _cols_in_inv(d_win[:, :D_IN]),
        'q_norm_w': d_qw, 'w_q_up': _cols_q_inv(d_wq),
        'kv_norm_w': d_kvw, 'w_kv_up': _cols_kv_inv(d_wkv),
        'attn_out_norm_w': d_aw,
        'ssm_a_re': g_are, 'ssm_a_im': g_aim, 'ssm_log_dt': g_ldt,
        'ssm_b_re': b_layout(g_bt_re), 'ssm_b_im': b_layout(g_bt_im), 'ssm_c_re': g_c_re, 'ssm_c_im': g_c_im,
        'ssm_d': d_dvec, 'w_glu': d_wglu, 'b_glu': d_bglu, 'ssm_out_norm_w': d_sw, 'w_out': d_wo,
    }

    replicated = [n for n in WEIGHTS if n not in SHARDED]
    order = SHARDED + replicated
    shapes = [wts[n].shape for n in order] + [(1, 1)]
    tr_adam = 512
    total_rows = -(-sum(_n_rows(s) for s in shapes) // tr_adam) * tr_adam
    recv = _exchange("exchange_grads",
                     _pack_per_device([_shard_views(n, local[n]) for n in SHARDED], n_shard_rows),
                     _pack([local[n] for n in replicated] + [loss], total_rows - n_shard_rows))
    zero = jnp.zeros((1, 1), F32)
    packed = [_pack([src[n] for n in order] + [zero], total_rows) for src in (wts, moms, vels)]
    g_p, d_p, m_p, v_p = _adamw(recv, *packed, tr_adam)
    sums = _unpack(g_p, shapes)
    grads = dict(zip(order, sums))
    deltas, new_m, new_v = (dict(zip(order, _unpack(b, shapes))) for b in (d_p, m_p, v_p))

    grad_x = dh[N_META:n_valid][None]
    return (sums[-1][0, 0], grad_x, *[grads[n] for n in WEIGHTS], *[deltas[n] for n in WEIGHTS],
            *[new_m[n] for n in WEIGHTS], *[new_v[n] for n in WEIGHTS])


def kernel(x, meta_tokens, pre_norm_w, post_norm_w, w_in, q_norm_w, w_q_up, kv_norm_w, w_kv_up, attn_out_norm_w, ssm_a_re, ssm_a_im, ssm_log_dt, ssm_b_re, ssm_b_im, ssm_c_re, ssm_c_im, ssm_d, w_glu, b_glu, ssm_out_norm_w, w_out, loss_target, m_meta_tokens, m_pre_norm_w, m_post_norm_w, m_w_in, m_q_norm_w, m_w_q_up, m_kv_norm_w, m_w_kv_up, m_attn_out_norm_w, m_ssm_a_re, m_ssm_a_im, m_ssm_log_dt, m_ssm_b_re, m_ssm_b_im, m_ssm_c_re, m_ssm_c_im, m_ssm_d, m_w_glu, m_b_glu, m_ssm_out_norm_w, m_w_out, v_meta_tokens, v_pre_norm_w, v_post_norm_w, v_w_in, v_q_norm_w, v_w_q_up, v_kv_norm_w, v_w_kv_up, v_attn_out_norm_w, v_ssm_a_re, v_ssm_a_im, v_ssm_log_dt, v_ssm_b_re, v_ssm_b_im, v_ssm_c_re, v_ssm_c_im, v_ssm_d, v_w_glu, v_b_glu, v_ssm_out_norm_w, v_w_out):
    wts = dict(zip(WEIGHTS, (meta_tokens, pre_norm_w, post_norm_w, w_in, q_norm_w, w_q_up, kv_norm_w, w_kv_up,
                             attn_out_norm_w, ssm_a_re, ssm_a_im, ssm_log_dt, ssm_b_re, ssm_b_im, ssm_c_re,
                             ssm_c_im, ssm_d, w_glu, b_glu, ssm_out_norm_w, w_out)))
    moms = dict(zip(WEIGHTS, (m_meta_tokens, m_pre_norm_w, m_post_norm_w, m_w_in, m_q_norm_w, m_w_q_up,
                              m_kv_norm_w, m_w_kv_up, m_attn_out_norm_w, m_ssm_a_re, m_ssm_a_im, m_ssm_log_dt,
                              m_ssm_b_re, m_ssm_b_im, m_ssm_c_re, m_ssm_c_im, m_ssm_d, m_w_glu, m_b_glu,
                              m_ssm_out_norm_w, m_w_out)))
    vels = dict(zip(WEIGHTS, (v_meta_tokens, v_pre_norm_w, v_post_norm_w, v_w_in, v_q_norm_w, v_w_q_up,
                              v_kv_norm_w, v_w_kv_up, v_attn_out_norm_w, v_ssm_a_re, v_ssm_a_im, v_ssm_log_dt,
                              v_ssm_b_re, v_ssm_b_im, v_ssm_c_re, v_ssm_c_im, v_ssm_d, v_w_glu, v_b_glu,
                              v_ssm_out_norm_w, v_w_out)))
    return _step(x, loss_target, wts, moms, vels)
```

```python
import functools
import math

import numpy as np
import jax
import jax.numpy as jnp
from jax import lax
from jax.experimental import pallas as pl
from jax.experimental.pallas import tpu as pltpu

F32 = jnp.float32
BF16 = jnp.bfloat16

D_MODEL = 1024
N_META = 16
EPS = 1e-6
HEADS = 8
QK_NOPE = 64
QK_ROPE = 32
HALF_ROPE = QK_ROPE // 2
QK_DIM = QK_NOPE + QK_ROPE
V_HEAD = 64
Q_LORA = 256
KV_LORA = 128
D_ATTN = HEADS * V_HEAD
D_SSM = 512
SSM_GROUP = 16
N_GROUPS = D_SSM // SSM_GROUP
SSM_STATE = 64
N_STATES = N_GROUPS * SSM_STATE
ROPE_THETA = 10000.0
D_IN = Q_LORA + KV_LORA + QK_ROPE + D_ATTN + 2 * D_SSM
D_IN_PAD = 2048
N_DEV = 8
N_SEG = 8
COL_BLK = 512
LANES = 128

ADAM_LR = 0.001
ADAM_B1 = 0.9
ADAM_B2 = 0.999
ADAM_EPS = 1e-08
ADAM_WD = 0.01
ADAM_STEP = 10

VMEM_LIMIT_V7X = 56 * 1024 * 1024
LOG2E = 1.0 / math.log(2.0)
Q_SCALE = LOG2E / math.sqrt(QK_DIM)
ATTN_UNROLL = 4
ATTN_BWD_UNROLL = 4

WEIGHTS = ['meta_tokens', 'pre_norm_w', 'post_norm_w', 'w_in', 'q_norm_w', 'w_q_up', 'kv_norm_w', 'w_kv_up',
           'attn_out_norm_w', 'ssm_a_re', 'ssm_a_im', 'ssm_log_dt', 'ssm_b_re', 'ssm_b_im', 'ssm_c_re', 'ssm_c_im',
           'ssm_d', 'w_glu', 'b_glu', 'ssm_out_norm_w', 'w_out']
SHARDED = ['w_in', 'w_q_up', 'w_kv_up', 'w_glu', 'w_out', 'meta_tokens']

def _cols_in(w):
    return jnp.concatenate([w[:, 0:384], w[:, 416:D_IN], w[:, 384:416]], axis=1)


def _cols_in_inv(w):
    return jnp.concatenate([w[:, 0:384], w[:, D_IN - QK_ROPE:D_IN], w[:, 384:D_IN - QK_ROPE]], axis=1)


def _cols_q(w):
    t = w.reshape(w.shape[0], HEADS, QK_DIM)
    return jnp.concatenate([t[:, :, 0:64].reshape(-1, 512), t[:, :, 64:80].reshape(-1, 128),
                            t[:, :, 80:96].reshape(-1, 128)], axis=1)


def _cols_q_inv(w):
    r = w.shape[0]
    return jnp.concatenate([w[:, 0:512].reshape(r, HEADS, 64), w[:, 512:640].reshape(r, HEADS, 16),
                            w[:, 640:768].reshape(r, HEADS, 16)], axis=2).reshape(r, HEADS * QK_DIM)


def _cols_kv(w):
    t = w.reshape(w.shape[0], HEADS, 128)
    return jnp.concatenate([t[:, :, 0:64].reshape(-1, 512), t[:, :, 64:128].reshape(-1, 512)], axis=1)


def _cols_kv_inv(w):
    r = w.shape[0]
    return jnp.concatenate([w[:, 0:512].reshape(r, HEADS, 64), w[:, 512:1024].reshape(r, HEADS, 64)],
                           axis=2).reshape(r, HEADS * 128)


def _pick(n, cands):
    for c in cands:
        if n % c == 0:
            return c
    raise ValueError(f"no tile for {n}")


def _cparams(*sem):
    return pltpu.CompilerParams(dimension_semantics=sem, vmem_limit_bytes=VMEM_LIMIT_V7X)


def _mm(a, b):
    return jnp.dot(a.astype(BF16), b.astype(BF16), preferred_element_type=F32)


def _mm_nt(a, b):
    return lax.dot_general(a.astype(BF16), b.astype(BF16), (((1,), (1,)), ((), ())), preferred_element_type=F32)


def _mm_tn(a, b):
    return lax.dot_general(a.astype(BF16), b.astype(BF16), (((0,), (0,)), ((), ())), preferred_element_type=F32)


def _mm_exact(a, b):
    return jnp.dot(a, b, precision=lax.Precision.HIGHEST, preferred_element_type=F32)


def _rms(x):
    return lax.rsqrt(jnp.mean(x * x, axis=-1, keepdims=True) + EPS)


def _rms_bwd(dy, x, r, w):
    xh = x * r
    g = dy * w
    dx = r * (g - xh * jnp.mean(g * xh, axis=-1, keepdims=True))
    dw = jnp.sum(dy * xh, axis=0, keepdims=True)
    return dx, dw


def _sigmoid(z):
    return 1.0 / (1.0 + jnp.exp(-z))


def _silu_and_grad(z):
    s = _sigmoid(z)
    return z * s, s * (1.0 + z * (1.0 - s))


_GELU_C = math.sqrt(2.0 / math.pi)


def _gelu_and_grad(x):
    x2 = x * x
    t = jnp.tanh(_GELU_C * (x + 0.044715 * x * x2))
    val = 0.5 * x * (1.0 + t)
    grad = 0.5 * (1.0 + t) + 0.5 * x * (1.0 - t * t) * _GELU_C * (1.0 + 3.0 * 0.044715 * x2)
    return val, grad


def _acc(ref, val, first):
    @pl.when(first)
    def _():
        ref[...] = val

    @pl.when(jnp.logical_not(first))
    def _():
        ref[...] += val


def _rows_call(name, body, tr, row_ins, full_ins, row_outs, acc_outs):
    lp = row_ins[0].shape[0]
    in_specs = [pl.BlockSpec((tr, a.shape[1]), lambda i: (i, 0)) for a in row_ins]
    in_specs += [pl.BlockSpec(a.shape, lambda i, n=a.ndim: (0,) * n) for a in full_ins]
    out_specs = [pl.BlockSpec((tr, c), lambda i: (i, 0)) for c, _ in row_outs]
    out_specs += [pl.BlockSpec(s, lambda i, n=len(s): (0,) * n) for s, _ in acc_outs]
    out_shape = [jax.ShapeDtypeStruct((lp, c), dt) for c, dt in row_outs]
    out_shape += [jax.ShapeDtypeStruct(s, dt) for s, dt in acc_outs]
    return pl.pallas_call(
        body, name=name, grid=(lp // tr,), in_specs=in_specs, out_specs=out_specs, out_shape=out_shape,
        compiler_params=_cparams("arbitrary"))(*row_ins, *full_ins)


def _inproj(h, pre_w, w_in_b, tr):
    def body(h_ref, pw_ref, w_ref, ql, kvl, ag, su, sg, kr):
        x = h_ref[...]
        xn = x * _rms(x) * pw_ref[...]
        pr = _mm(xn, w_ref[...])
        ql[...] = pr[:, 0:256]
        kvl[...] = pr[:, 256:384]
        ag[...] = pr[:, 384:896]
        su[...] = pr[:, 896:1408]
        sg[...] = pr[:, 1408:1920]
        kr[...] = pr[:, 1920:1952]

    return _rows_call("inproj", body, tr, [h], [pre_w, w_in_b],
                      [(256, F32), (128, F32), (512, F32), (512, F32), (512, F32), (32, F32)], [])


def _qkv_up(ql, kvl, kr, cos8, sin8, c32, s32, qw, kvw, wq_b, wkv_b, p32, tr):
    def body(ql_ref, kvl_ref, kr_ref, cos_ref, sin_ref, c32_ref, s32_ref, qw_ref, kvw_ref, wq_ref, wkv_ref, p_ref,
             qn_o, qr1_o, qr2_o, kn_o, v_o, kr_o):
        x = ql_ref[...]
        q = _mm(x * _rms(x) * qw_ref[...], wq_ref[...]) * Q_SCALE
        r1, r2 = q[:, 512:640], q[:, 640:768]
        cs, sn = cos_ref[...], sin_ref[...]
        qn_o[...] = q[:, 0:512].astype(BF16)
        qr1_o[...] = (r1 * cs - r2 * sn).astype(BF16)
        qr2_o[...] = (r2 * cs + r1 * sn).astype(BF16)
        x = kvl_ref[...]
        kv = _mm(x * _rms(x) * kvw_ref[...], wkv_ref[...])
        kn_o[...] = kv[:, 0:512].astype(BF16)
        v_o[...] = kv[:, 512:1024].astype(BF16)
        x = kr_ref[...]
        kr_o[...] = (x * c32_ref[...] + _mm_exact(x, p_ref[...]) * s32_ref[...]).astype(BF16)

    return _rows_call("qkv_up", body, tr, [ql, kvl, kr, cos8, sin8, c32, s32], [qw, kvw, wq_b, wkv_b, p32],
                      [(512, BF16), (128, BF16), (128, BF16), (512, BF16), (512, BF16), (32, BF16)], [])


def _row_position(row, lseg):
    return (row & (N_SEG - 1)) * lseg + (row >> 3)


def _first_padded_tile(n_valid, lp, tile):
    lseg = lp // N_SEG
    t0 = n_valid - (N_SEG - 1) * lseg
    return (t0 * N_SEG + N_SEG - 1) // tile if n_valid < lp else lp // tile


def _attn_fwd(qt, k, vxt, n_valid):
    _, nq, _, tq = qt.shape
    _, nk, _, tk = vxt.shape
    lp = k.shape[1]
    lseg = lp // N_SEG
    n_plain = max(0, min(nk, _first_padded_tile(n_valid, lp, tk)))

    def body(q_ref, k_ref, v_ref, o_ref, lse_ref, m_s, acc_s):
        m_s[...] = jnp.full(m_s.shape, -1e30, F32)
        acc_s[...] = jnp.zeros(acc_s.shape, F32)
        qq = q_ref[0, 0]

        def chunk(c, padded):
            r0 = pl.multiple_of(c * tk, tk)
            st = _mm(k_ref[0, pl.ds(r0, tk), :], qq)
            if padded:
                row = r0 + lax.broadcasted_iota(jnp.int32, (tk, tq), 0)
                st = jnp.where(_row_position(row, lseg) < n_valid, st, -1e30)
            m_old = m_s[...]
            m_new = jnp.maximum(m_old, jnp.max(st, axis=0, keepdims=True))
            pt = jnp.exp2(st - m_new)
            acc_s[...] = jnp.exp2(m_old - m_new) * acc_s[...] + _mm(v_ref[0, c], pt)
            m_s[...] = m_new

        def plain(c, carry):
            chunk(c, False)
            return carry

        n_loop = n_plain - n_plain % ATTN_UNROLL
        if n_loop:
            lax.fori_loop(0, n_loop, plain, 0, unroll=ATTN_UNROLL)
        for c in range(n_loop, nk):
            chunk(c, c >= n_plain)
        acc = acc_s[...]
        l = acc[V_HEAD:V_HEAD + 1, :]
        o_ref[0] = acc[:V_HEAD, :] / l
        lse_ref[0, 0] = m_s[...] + jnp.log2(l)

    return pl.pallas_call(
        body, name="attn_fwd", grid=(HEADS, nq),
        in_specs=[pl.BlockSpec((1, 1, QK_DIM, tq), lambda h, i: (h, i, 0, 0)),
                  pl.BlockSpec((1, lp, QK_DIM), lambda h, i: (h, 0, 0)),
                  pl.BlockSpec((1, nk, LANES, tk), lambda h, i: (h, 0, 0, 0))],
        out_specs=[pl.BlockSpec((1, V_HEAD, tq), lambda h, i: (h, 0, i)),
                   pl.BlockSpec((1, 1, 1, tq), lambda h, i: (h, i, 0, 0))],
        out_shape=[jax.ShapeDtypeStruct((HEADS, V_HEAD, lp), F32), jax.ShapeDtypeStruct((HEADS, nq, 1, tq), F32)],
        scratch_shapes=[pltpu.VMEM((1, tq), F32), pltpu.VMEM((LANES, tq), F32)],
        compiler_params=_cparams("arbitrary", "arbitrary"))(qt, k, vxt)


def _scan_tiles(lp, longest):
    lseg = lp // N_SEG
    tt = _pick(lseg, [longest, 48, 32, 16, 8, 4, 2, 1])
    return lseg, tt, lseg // tt


def _cmul(ar, ai, br, bi):
    return ar * br - ai * bi, ar * bi + ai * br


SCAN_STEPS_PER_ITER = 4
N_COL_BLK = N_STATES // COL_BLK
CH_BLK = D_SSM // N_COL_BLK


def _scan_steps(tt, forward, bre_ref, bim_ref, ar, ai, carry, visit):
    def step(s, c):
        r0 = pl.multiple_of((s if forward else tt - 1 - s) * N_SEG, N_SEG)
        pr, pi = _cmul(ar, ai, c[0], c[1])
        xr = pr + bre_ref[pl.ds(r0, N_SEG), :]
        xi = pi + bim_ref[pl.ds(r0, N_SEG), :]
        return (xr, xi) + tuple(visit(r0, (xr, xi), (c[0], c[1]), c[2:]))

    per = SCAN_STEPS_PER_ITER if tt % SCAN_STEPS_PER_ITER == 0 else 1

    def steps(it, c):
        for u in range(per):
            c = step(it * per + u, c)
        return c

    return lax.fori_loop(0, tt // per, steps, carry)


def _segment_starts(lseg, forward, ar, ai, ere_ref, eim_ref, s_re, s_im):
    a1r, a1i = ar[0:1, :], ai[0:1, :]
    pr, pi = jnp.ones_like(a1r), jnp.zeros_like(a1i)
    br, bi = a1r, a1i
    n = lseg
    while n:
        if n & 1:
            pr, pi = _cmul(pr, pi, br, bi)
        n >>= 1
        if n:
            br, bi = _cmul(br, bi, br, bi)
    cr, ci = jnp.zeros_like(a1r), jnp.zeros_like(a1i)
    for j in (range(N_SEG) if forward else range(N_SEG - 1, -1, -1)):
        s_re[j:j + 1, :] = cr
        s_im[j:j + 1, :] = ci
        nr, ni = _cmul(pr, pi, cr, ci)
        cr = nr + ere_ref[j:j + 1, :]
        ci = ni + eim_ref[j:j + 1, :]


def _scan_specs(lp, forward, longest=208):
    lseg, tt, nt = _scan_tiles(lp, longest)

    def tile(t):
        return t if forward else nt - 1 - t

    rows = lambda w: pl.BlockSpec((tt * N_SEG, w), lambda cb, t: (tile(t), cb))
    proj = pl.BlockSpec((1, CH_BLK, COL_BLK), lambda cb, t: (cb, 0, 0))
    slab = pl.BlockSpec((N_SEG, COL_BLK), lambda cb, t: (0, cb))
    return lseg, tt, nt, rows, proj, slab


def _scan_ends(name, urows, wre4, wim4, ar8, ai8, forward):
    lp = urows.shape[0]
    lseg, tt, nt, rows, proj, slab = _scan_specs(lp, forward, longest=520)

    def body(u_ref, wre_ref, wim_ref, ar_ref, ai_ref, ere_o, eim_o, bre_s, bim_s, cr_s, ci_s):
        t = pl.program_id(1)

        @pl.when(t == 0)
        def _():
            cr_s[...] = jnp.zeros(cr_s.shape, F32)
            ci_s[...] = jnp.zeros(ci_s.shape, F32)

        u = u_ref[...]
        bre_s[...] = _mm(u, wre_ref[0])
        bim_s[...] = _mm(u, wim_ref[0])
        cr, ci = _scan_steps(tt, forward, bre_s, bim_s, ar_ref[...], ai_ref[...], (cr_s[...], ci_s[...]),
                             lambda r0, x, x_prev, extra: ())
        cr_s[...] = cr
        ci_s[...] = ci

        @pl.when(t == nt - 1)
        def _():
            ere_o[...] = cr
            eim_o[...] = ci

    return pl.pallas_call(
        body, name=name, grid=(N_COL_BLK, nt), in_specs=[rows(CH_BLK), proj, proj, slab, slab],
        out_specs=[slab, slab], out_shape=[jax.ShapeDtypeStruct((N_SEG, N_STATES), F32)] * 2,
        scratch_shapes=[pltpu.VMEM((tt * N_SEG, COL_BLK), F32)] * 2 + [pltpu.VMEM((N_SEG, COL_BLK), F32)] * 2,
        compiler_params=_cparams("arbitrary", "arbitrary"))(urows, wre4, wim4, ar8, ai8)


def _scan_fwd(name, urows, wre4, wim4, ar8, ai8, ere, eim, cre4, cim4, forward):
    lp = urows.shape[0]
    lseg, tt, nt, rows, proj, slab = _scan_specs(lp, forward)

    def body(u_ref, wre_ref, wim_ref, ar_ref, ai_ref, ere_ref, eim_ref, cre_ref, cim_ref,
             xre_o, xim_o, y_o, bre_s, bim_s, cr_s, ci_s):
        ar, ai = ar_ref[...], ai_ref[...]

        @pl.when(pl.program_id(1) == 0)
        def _():
            _segment_starts(lseg, forward, ar, ai, ere_ref, eim_ref, cr_s, ci_s)

        u = u_ref[...]
        bre_s[...] = _mm(u, wre_ref[0])
        bim_s[...] = _mm(u, wim_ref[0])

        def visit(r0, x, x_prev, extra):
            xre_o[pl.ds(r0, N_SEG), :] = x[0]
            xim_o[pl.ds(r0, N_SEG), :] = x[1]
            return ()

        cr, ci = _scan_steps(tt, forward, bre_s, bim_s, ar, ai, (cr_s[...], ci_s[...]), visit)
        cr_s[...] = cr
        ci_s[...] = ci
        y_o[...] = _mm_nt(xre_o[...], cre_ref[0]) + _mm_nt(xim_o[...], cim_ref[0])

    return pl.pallas_call(
        body, name=name, grid=(N_COL_BLK, nt),
        in_specs=[rows(CH_BLK), proj, proj, slab, slab, slab, slab, proj, proj],
        out_specs=[rows(COL_BLK), rows(COL_BLK), rows(CH_BLK)],
        out_shape=[jax.ShapeDtypeStruct((lp, N_STATES), F32)] * 2 + [jax.ShapeDtypeStruct((lp, D_SSM), F32)],
        scratch_shapes=[pltpu.VMEM((tt * N_SEG, COL_BLK), F32)] * 2 + [pltpu.VMEM((N_SEG, COL_BLK), F32)] * 2,
        compiler_params=_cparams("arbitrary", "arbitrary"))(urows, wre4, wim4, ar8, ai8, ere, eim, cre4, cim4)


def _scan_bwd(name, dyrows, cre4, cim4, ar8, ai8, ere, eim, urows, wre4, wim4, xre, xim, forward):
    lp = urows.shape[0]
    lseg, tt, nt, rows, proj, slab = _scan_specs(lp, forward)

    def body(dy_ref, cre_ref, cim_ref, ar_ref, ai_ref, ere_ref, eim_ref, u_ref, wre_ref, wim_ref, xre_ref, xim_ref,
             du_o, dwre_o, dwim_o, dcre_o, dcim_o, dare_o, daim_o, bre_s, bim_s, gre_s, gim_s, cr_s, ci_s):
        t = pl.program_id(1)
        ar, ai = ar_ref[...], ai_ref[...]

        @pl.when(t == 0)
        def _():
            _segment_starts(lseg, forward, ar, ai, ere_ref, eim_ref, cr_s, ci_s)
            dare_o[...] = jnp.zeros(dare_o.shape, F32)
            daim_o[...] = jnp.zeros(daim_o.shape, F32)

        dy = dy_ref[...]
        bre_s[...] = _mm(dy, cre_ref[0])
        bim_s[...] = _mm(dy, cim_ref[0])

        def visit(r0, g, g_prev, sums):
            gre_s[pl.ds(r0, N_SEG), :] = g[0]
            gim_s[pl.ds(r0, N_SEG), :] = g[1]
            fr = xre_ref[pl.ds(r0, N_SEG), :]
            fi = xim_ref[pl.ds(r0, N_SEG), :]
            pr, pi = g_prev
            return sums[0] + fr * pr + fi * pi, sums[1] + fr * pi - fi * pr

        out = _scan_steps(tt, forward, bre_s, bim_s, ar, ai, (cr_s[...], ci_s[...], dare_o[...], daim_o[...]), visit)
        cr_s[...] = out[0]
        ci_s[...] = out[1]
        dare_o[...] = out[2]
        daim_o[...] = out[3]
        gre, gim = gre_s[...], gim_s[...]
        du_o[...] = _mm_nt(gre, wre_ref[0]) + _mm_nt(gim, wim_ref[0])
        u = u_ref[...]
        first = t == 0
        _acc(dwre_o, _mm_tn(u, gre)[None], first)
        _acc(dwim_o, _mm_tn(u, gim)[None], first)
        _acc(dcre_o, _mm_tn(dy, xre_ref[...])[None], first)
        _acc(dcim_o, _mm_tn(dy, xim_ref[...])[None], first)

    big = pltpu.VMEM((tt * N_SEG, COL_BLK), F32)
    small = pltpu.VMEM((N_SEG, COL_BLK), F32)
    return pl.pallas_call(
        body, name=name, grid=(N_COL_BLK, nt),
        in_specs=[rows(CH_BLK), proj, proj, slab, slab, slab, slab, rows(CH_BLK), proj, proj,
                  rows(COL_BLK), rows(COL_BLK)],
        out_specs=[rows(CH_BLK), proj, proj, proj, proj, slab, slab],
        out_shape=[jax.ShapeDtypeStruct((lp, D_SSM), F32)]
        + [jax.ShapeDtypeStruct((N_COL_BLK, CH_BLK, COL_BLK), F32)] * 4
        + [jax.ShapeDtypeStruct((N_SEG, N_STATES), F32)] * 2,
        scratch_shapes=[big, big, big, big, small, small],
        compiler_params=_cparams("arbitrary", "arbitrary"))(
            dyrows, cre4, cim4, ar8, ai8, ere, eim, urows, wre4, wim4, xre, xim)


def _ssm_post(yf, yb, u, sg, wglu_b, bglu, sw, dvec, tr):
    def body(yf_ref, yb_ref, u_ref, g_ref, w_ref, b_ref, sw_ref, d_ref, ypre_o, glu_o, ysn_o):
        ypre = yf_ref[...] + yb_ref[...] + d_ref[...] * u_ref[...]
        ypre_o[...] = ypre
        glu = _mm(_gelu_and_grad(ypre)[0], w_ref[...]) + b_ref[...]
        glu_o[...] = glu
        t = glu[:, :D_SSM] * _sigmoid(glu[:, D_SSM:]) * _silu_and_grad(g_ref[...])[0]
        ysn_o[...] = t * _rms(t) * sw_ref[...]

    return _rows_call("ssm_post", body, tr, [yf, yb, u, sg], [wglu_b, bglu, sw, dvec],
                      [(512, F32), (1024, F32), (512, F32)], [])


def _attn_gate_norm(o, gate, w):
    sl, dsl = _silu_and_grad(gate)
    t = o * sl
    r = _rms(t)
    return t * r * w, t, r, sl, dsl


def _out_loss(o_flat, ag, ysn, h, tgt, wo_b, post_w, aw, n_valid, tr):
    lseg = h.shape[0] // N_SEG

    def body(o_ref, g_ref, ys_ref, h_ref, t_ref, w_ref, pw_ref, aw_ref, dy_o, dout_o, loss_o, dpw_o):
        i = pl.program_id(0)
        ya = _attn_gate_norm(o_ref[...], g_ref[...], aw_ref[...])[0]
        y = _mm(ya, w_ref[0:D_ATTN, :]) + _mm(ys_ref[...], w_ref[D_ATTN:, :])
        r = _rms(y)
        pw = pw_ref[...]
        out = h_ref[...] + y * r * pw
        pos = _row_position(i * tr + lax.broadcasted_iota(jnp.int32, (tr, 1), 0), lseg)
        valid = jnp.logical_and(pos >= N_META, pos < n_valid)
        diff = jnp.where(valid, out - t_ref[...], 0.0)
        dout = diff * (1.0 / D_MODEL)
        dy, dpw = _rms_bwd(dout, y, r, pw)
        dy_o[...] = dy
        dout_o[...] = dout
        _acc(loss_o, 0.5 * jnp.sum(jnp.sum(diff * diff, axis=1, keepdims=True), axis=0, keepdims=True)
             * (1.0 / D_MODEL), i == 0)
        _acc(dpw_o, dpw, i == 0)

    return _rows_call("out_loss", body, tr, [o_flat, ag, ysn, h, tgt], [wo_b, post_w, aw],
                      [(1024, F32), (1024, F32)], [((1, 1), F32), ((1, D_MODEL), F32)])


def _out_bwd(dy, ysn, o_flat, ag, wo_b, aw, head_sum, tr):
    def body(dy_ref, ys_ref, o_ref, g_ref, w_ref, aw_ref, hs_ref, do_o, dag_o, dysn_o, dl_o, dwo_o, daw_o):
        i = pl.program_id(0)
        dy = dy_ref[...]
        dcat = _mm_nt(dy, w_ref[...])
        o = o_ref[...]
        aw = aw_ref[...]
        ya, t, r, sl, dsl = _attn_gate_norm(o, g_ref[...], aw)
        cat = jnp.concatenate([ya, ys_ref[...]], axis=1)
        _acc(dwo_o, _mm_tn(cat, dy), i == 0)
        dysn_o[...] = dcat[:, D_ATTN:]
        dt, daw = _rms_bwd(dcat[:, :D_ATTN], t, r, aw)
        _acc(daw_o, daw, i == 0)
        do = dt * sl
        do_o[...] = do
        dag_o[...] = dt * o * dsl
        dl_o[...] = _mm_exact(do * o, hs_ref[...])

    return _rows_call("out_bwd", body, tr, [dy, ysn, o_flat, ag], [wo_b, aw, head_sum],
                      [(512, F32), (512, F32), (512, F32), (HEADS, F32)],
                      [((D_MODEL, D_MODEL), F32), ((1, D_ATTN), F32)])


def _ssm_post_bwd(dysn, glu, sg, ypre, u, wglu_b, sw, dvec, tr):
    def body(d_ref, glu_ref, sg_ref, y_ref, u_ref, w_ref, sw_ref, dv_ref,
             dyp_o, dsg_o, dwg_o, dbg_o, dsw_o, dd_o):
        i = pl.program_id(0)
        glu = glu_ref[...]
        a, b = glu[:, :D_SSM], glu[:, D_SSM:]
        sb = _sigmoid(b)
        ys = a * sb
        sl, dsl = _silu_and_grad(sg_ref[...])
        t = ys * sl
        dt, dsw = _rms_bwd(d_ref[...], t, _rms(t), sw_ref[...])
        _acc(dsw_o, dsw, i == 0)
        dsg_o[...] = dt * ys * dsl
        dys = dt * sl
        dglu = jnp.concatenate([dys * sb, dys * a * sb * (1.0 - sb)], axis=1)
        _acc(dbg_o, jnp.sum(dglu, axis=0, keepdims=True), i == 0)
        gel, dgel = _gelu_and_grad(y_ref[...])
        _acc(dwg_o, _mm_tn(gel, dglu), i == 0)
        dyp = _mm_nt(dglu, w_ref[...]) * dgel
        dyp_o[...] = dyp
        _acc(dd_o, jnp.sum(dyp * u_ref[...], axis=0, keepdims=True), i == 0)

    return _rows_call("ssm_post_bwd", body, tr, [dysn, glu, sg, ypre, u], [wglu_b, sw, dvec],
                      [(512, F32), (512, F32)],
                      [((D_SSM, 2 * D_SSM), F32), ((1, 2 * D_SSM), F32), ((1, D_SSM), F32), ((1, D_SSM), F32)])


def _attn_bwd(qt, k, kt, v, dot, lse_t, delta_t, tk):
    _, nq, _, tq = qt.shape
    lp = k.shape[1]
    nk = lp // tk
    assert lse_t.shape == (HEADS, nq, 1, tq) and delta_t.shape == (HEADS, nq, 1, tq)

    def body(q_ref, k_ref, kt_ref, v_ref, do_ref, lse_ref, dl_ref, dq_o, dk_o, dv_o, dk_s, dv_s):
        @pl.when(pl.program_id(1) == 0)
        def _():
            dq_o[...] = jnp.zeros(dq_o.shape, F32)

        dk_s[...] = jnp.zeros(dk_s.shape, F32)
        dv_s[...] = jnp.zeros(dv_s.shape, F32)
        kk = k_ref[0]
        kkt = kt_ref[0]
        vv = v_ref[0]

        def chunk(c, carry):
            qq = q_ref[0, c]
            dd = do_ref[0, c]
            pt = jnp.exp2(_mm(kk, qq) - lse_ref[0, c])
            dv_s[...] += _mm_nt(dd, pt)
            dst = (pt * (_mm(vv, dd) - dl_ref[0, c])).astype(BF16)
            dk_s[...] += _mm_nt(qq, dst)
            dq_o[0, c] += _mm(kkt, dst)
            return carry

        n_loop = nq - nq % ATTN_BWD_UNROLL
        if n_loop:
            lax.fori_loop(0, n_loop, chunk, 0, unroll=ATTN_BWD_UNROLL)
        for c in range(n_loop, nq):
            chunk(c, 0)
        dk_o[0] = dk_s[...]
        dv_o[0] = dv_s[...]

    head = lambda w: pl.BlockSpec((1, nq, w, tq), lambda h, j: (h, 0, 0, 0))
    rows = lambda w: pl.BlockSpec((1, tk, w), lambda h, j: (h, j, 0))
    cols = lambda w: pl.BlockSpec((1, w, tk), lambda h, j: (h, 0, j))
    return pl.pallas_call(
        body, name="attn_bwd", grid=(HEADS, nk),
        in_specs=[head(QK_DIM), rows(QK_DIM), cols(QK_DIM), rows(V_HEAD), head(V_HEAD), head(1), head(1)],
        out_specs=[head(QK_DIM), cols(QK_DIM), cols(V_HEAD)],
        out_shape=[jax.ShapeDtypeStruct((HEADS, nq, QK_DIM, tq), F32), jax.ShapeDtypeStruct((HEADS, QK_DIM, lp), F32),
                   jax.ShapeDtypeStruct((HEADS, V_HEAD, lp), F32)],
        scratch_shapes=[pltpu.VMEM((QK_DIM, tk), F32), pltpu.VMEM((V_HEAD, tk), F32)],
        compiler_params=_cparams("arbitrary", "arbitrary"))(qt, k, kt, v, dot, lse_t, delta_t)


def _qkv_up_bwd(dqn, dr1, dr2, dkn, dv, dkr8, ql, kvl, cos8, sin8, c32, s32, qw, kvw, wq_b, wkv_b, p32, sum8, tr):
    def body(dqn_ref, dr1_ref, dr2_ref, dkn_ref, dv_ref, dkr_ref, ql_ref, kvl_ref, cos_ref, sin_ref, c32_ref,
             s32_ref, qw_ref, kvw_ref, wq_ref, wkv_ref, p_ref, s8_ref,
             dql_o, dkvl_o, dkrr_o, dwq_o, dwkv_o, dqw_o, dkvw_o):
        i = pl.program_id(0)
        cs, sn = cos_ref[...], sin_ref[...]
        d1, d2 = dr1_ref[...], dr2_ref[...]
        dq = jnp.concatenate([dqn_ref[...], d1 * cs + d2 * sn, d2 * cs - d1 * sn], axis=1) * (Q_SCALE / LOG2E)
        x = ql_ref[...]
        r = _rms(x)
        qw = qw_ref[...]
        _acc(dwq_o, _mm_tn(x * r * qw, dq), i == 0)
        dx, dw = _rms_bwd(_mm_nt(dq, wq_ref[...]), x, r, qw)
        dql_o[...] = dx
        _acc(dqw_o, dw, i == 0)
        dkv = jnp.concatenate([dkn_ref[...] * (1.0 / LOG2E), dv_ref[...]], axis=1)
        x = kvl_ref[...]
        r = _rms(x)
        kvw = kvw_ref[...]
        _acc(dwkv_o, _mm_tn(x * r * kvw, dkv), i == 0)
        dx, dw = _rms_bwd(_mm_nt(dkv, wkv_ref[...]), x, r, kvw)
        dkvl_o[...] = dx
        _acc(dkvw_o, dw, i == 0)
        dkr = _mm_exact(dkr_ref[...], s8_ref[...]) * (1.0 / LOG2E)
        dkrr_o[...] = dkr * c32_ref[...] + _mm_exact(dkr * s32_ref[...], p_ref[...])

    return _rows_call("qkv_up_bwd", body, tr, [dqn, dr1, dr2, dkn, dv, dkr8, ql, kvl, cos8, sin8, c32, s32],
                      [qw, kvw, wq_b, wkv_b, p32, sum8], [(256, F32), (128, F32), (32, F32)],
                      [((Q_LORA, 768), F32), ((KV_LORA, 1024), F32), ((1, Q_LORA), F32), ((1, KV_LORA), F32)])


def _inproj_bwd(dql, dkvl, dag, du_f, du_b, dypre, dsg, dkr, h, dout, pre_w, w_in_b, dvec, tr):
    def body(dql_ref, dkvl_ref, dag_ref, duf_ref, dub_ref, dyp_ref, dsg_ref, dkr_ref, h_ref, dout_ref,
             pw_ref, w_ref, dv_ref, dh_o, dwin_o, dpw_o):
        i = pl.program_id(0)
        du = duf_ref[...] + dub_ref[...] + dv_ref[...] * dyp_ref[...]
        dproj = jnp.concatenate([dql_ref[...], dkvl_ref[...], dag_ref[...], du, dsg_ref[...],
                                 dkr_ref[...], jnp.zeros((tr, D_IN_PAD - D_IN), F32)], axis=1)
        x = h_ref[...]
        r = _rms(x)
        pw = pw_ref[...]
        _acc(dwin_o, _mm_tn(x * r * pw, dproj), i == 0)
        dx, dw = _rms_bwd(_mm_nt(dproj, w_ref[...]), x, r, pw)
        _acc(dpw_o, dw, i == 0)
        dh_o[...] = dout_ref[...] + dx

    return _rows_call("inproj_bwd", body, tr, [dql, dkvl, dag, du_f, du_b, dypre, dsg, dkr, h, dout],
                      [pre_w, w_in_b, dvec], [(1024, F32)], [((D_MODEL, D_IN_PAD), F32), ((1, D_MODEL), F32)])


def _disc_terms(a_re, a_im, ldt):
    dt = jnp.exp(ldt)
    mag = jnp.exp(a_re * dt)
    th = a_im * dt
    cs, sn = jnp.cos(th), jnp.sin(th)
    abar_re, abar_im = mag * cs, mag * sn
    num_re, num_im = abar_re - 1.0, abar_im
    den = a_re * a_re + a_im * a_im
    coef_re = (num_re * a_re + num_im * a_im) / den
    coef_im = (num_im * a_re - num_re * a_im) / den
    return dt, mag, cs, sn, abar_re, abar_im, num_re, num_im, den, coef_re, coef_im


def _ssm_disc(a_re, a_im, ldt, bt_re, bt_im):
    def body(ar_ref, ai_ref, l_ref, br_ref, bi_ref, abr_o, abi_o, bbr_o, bbi_o):
        t = _disc_terms(ar_ref[...], ai_ref[...], l_ref[...])
        abr_o[...] = t[4]
        abi_o[...] = t[5]
        cr, ci = t[9], t[10]
        br, bi = br_ref[...], bi_ref[...]
        bbr_o[...] = cr * br - ci * bi
        bbi_o[...] = cr * bi + ci * br

    ng = a_re.shape[0]
    return pl.pallas_call(
        body, name="ssm_disc",
        out_shape=[jax.ShapeDtypeStruct((ng, 1, SSM_STATE), F32)] * 2
        + [jax.ShapeDtypeStruct((ng, SSM_GROUP, SSM_STATE), F32)] * 2)(a_re, a_im, ldt, bt_re, bt_im)


def _ssm_disc_bwd(a_re, a_im, ldt, bt_re, bt_im, da8_re, da8_im, dbb_re, dbb_im):
    def body(ar_ref, ai_ref, l_ref, br_ref, bi_ref, dar_ref, dai_ref, dbr_ref, dbi_ref,
             gar_o, gai_o, gl_o, gbr_o, gbi_o):
        a_re, a_im = ar_ref[...], ai_ref[...]
        dt, mag, cs, sn, abar_re, abar_im, num_re, num_im, den, cr, ci = _disc_terms(a_re, a_im, l_ref[...])
        br, bi = br_ref[...], bi_ref[...]
        dbr, dbi = dbr_ref[...], dbi_ref[...]
        gbr_o[...] = cr * dbr + ci * dbi
        gbi_o[...] = cr * dbi - ci * dbr
        dcr = jnp.sum(br * dbr + bi * dbi, axis=1, keepdims=True)
        dci = jnp.sum(br * dbi - bi * dbr, axis=1, keepdims=True)
        dnum_re = (dcr * a_re - dci * a_im) / den
        dnum_im = (dcr * a_im + dci * a_re) / den
        dden = -(dcr * cr + dci * ci) / den
        g_are = (dcr * num_re + dci * num_im) / den + dden * 2.0 * a_re
        g_aim = (dcr * num_im - dci * num_re) / den + dden * 2.0 * a_im
        d_abr = jnp.sum(dar_ref[...], axis=1, keepdims=True) + dnum_re
        d_abi = jnp.sum(dai_ref[...], axis=1, keepdims=True) + dnum_im
        dmag = d_abr * cs + d_abi * sn
        dth = d_abi * abar_re - d_abr * abar_im
        g_are = g_are + dmag * mag * dt
        g_aim = g_aim + dth * dt
        ddt = jnp.sum(dmag * mag * a_re + dth * a_im, axis=2, keepdims=True)
        gar_o[...] = g_are
        gai_o[...] = g_aim
        gl_o[...] = ddt * dt

    ng = a_re.shape[0]
    return pl.pallas_call(
        body, name="ssm_disc_bwd",
        out_shape=[jax.ShapeDtypeStruct((ng, 1, SSM_STATE), F32)] * 2 + [jax.ShapeDtypeStruct((ng, 1, 1), F32)]
        + [jax.ShapeDtypeStruct((ng, SSM_GROUP, SSM_STATE), F32)] * 2)(
            a_re, a_im, ldt, bt_re, bt_im, da8_re, da8_im, dbb_re, dbb_im)


def _exchange(name, per_peer, shared):
    parts = [a for a in (per_peer, shared) if a is not None]
    rp = per_peer.shape[1] if per_peer is not None else 0
    rs = shared.shape[0] if shared is not None else 0
    n = len(parts)

    def body(*refs):
        in_refs, out_ref, send_sems, recv_sems, local_sems = refs[:n], refs[n], refs[n + 1], refs[n + 2], refs[n + 3]
        x, y, c = lax.axis_index("x"), lax.axis_index("y"), lax.axis_index("c")
        me = 4 * x + 2 * y + c

        def pieces(peer):
            out = []
            if per_peer is not None:
                out.append((in_refs[0].at[peer], out_ref.at[me, pl.ds(0, rp), :]))
            if shared is not None:
                out.append((in_refs[-1], out_ref.at[me, pl.ds(rp, rs), :]))
            return out

        copies = []
        for k in range(1, N_DEV):
            px = 1 - x if (k >> 2) & 1 else x
            py = 1 - y if (k >> 1) & 1 else y
            pc = 1 - c if k & 1 else c
            for j, (src, dst) in enumerate(pieces(4 * px + 2 * py + pc)):
                s = (k - 1) * n + j
                copies.append(pltpu.make_async_remote_copy(
                    src_ref=src, dst_ref=dst, send_sem=send_sems.at[s], recv_sem=recv_sems.at[s],
                    device_id=(px, py, pc), device_id_type=pl.DeviceIdType.MESH))
        mine = [pltpu.make_async_copy(src, dst, local_sems.at[j]) for j, (src, dst) in enumerate(pieces(me))]
        for cp in mine + copies:
            cp.start()
        for cp in copies + mine:
            cp.wait()

    n_sem = (N_DEV - 1) * n
    return pl.pallas_call(
        body, name=name, out_shape=jax.ShapeDtypeStruct((N_DEV, rp + rs, LANES), F32),
        in_specs=[pl.BlockSpec(memory_space=pl.ANY)] * n, out_specs=pl.BlockSpec(memory_space=pl.ANY),
        scratch_shapes=[pltpu.SemaphoreType.DMA((n_sem,)), pltpu.SemaphoreType.DMA((n_sem,)),
                        pltpu.SemaphoreType.DMA((n,))])(*parts)


def _adamw(recv, w, m, v, tr):
    rows = w.shape[0]
    c1 = 1.0 - ADAM_B1 ** ADAM_STEP
    c2 = 1.0 - ADAM_B2 ** ADAM_STEP

    def body(r_ref, w_ref, m_ref, v_ref, g_o, d_o, m_o, v_o):
        g = r_ref[0]
        for k in range(1, N_DEV):
            g = g + r_ref[k]
        mm = ADAM_B1 * m_ref[...] + (1.0 - ADAM_B1) * g
        vv = ADAM_B2 * v_ref[...] + (1.0 - ADAM_B2) * (g * g)
        g_o[...] = g
        m_o[...] = mm
        v_o[...] = vv
        d_o[...] = -ADAM_LR * ((mm / c1) / (jnp.sqrt(vv / c2) + ADAM_EPS) + ADAM_WD * w_ref[...])

    spec = pl.BlockSpec((tr, LANES), lambda i: (i, 0))
    return pl.pallas_call(
        body, name="adamw", grid=(rows // tr,),
        in_specs=[pl.BlockSpec((N_DEV, tr, LANES), lambda i: (0, i, 0)), spec, spec, spec],
        out_specs=[spec] * 4, out_shape=[jax.ShapeDtypeStruct((rows, LANES), F32)] * 4,
        compiler_params=_cparams("arbitrary"))(recv, w, m, v)


def _to_rows(a):
    flat = a.reshape(-1)
    pad = (-flat.shape[0]) % LANES
    if pad:
        flat = jnp.concatenate([flat, jnp.zeros((pad,), flat.dtype)])
    return flat.reshape(-1, LANES)


def _n_rows(shape):
    return -(-int(np.prod(shape)) // LANES)


def _pack(arrays, total_rows):
    rows = [_to_rows(a) for a in arrays]
    used = sum(r.shape[0] for r in rows)
    if total_rows > used:
        rows.append(jnp.zeros((total_rows - used, LANES), F32))
    return jnp.concatenate(rows, axis=0)


def _unpack(buf, shapes):
    lead = buf.shape[:-2]
    out, r0 = [], 0
    for s in shapes:
        n = int(np.prod(s))
        nr = _n_rows(s)
        out.append(buf[..., r0:r0 + nr, :].reshape(lead + (-1,))[..., :n].reshape(lead + tuple(s)))
        r0 += nr
    return out


def _pack_per_device(arrays, total_rows):
    rows = []
    for a in arrays:
        flat = a.reshape(N_DEV, -1)
        pad = (-flat.shape[1]) % LANES
        if pad:
            flat = jnp.concatenate([flat, jnp.zeros((N_DEV, pad), flat.dtype)], axis=1)
        rows.append(flat.reshape(N_DEV, -1, LANES))
    used = sum(r.shape[1] for r in rows)
    if total_rows > used:
        rows.append(jnp.zeros((N_DEV, total_rows - used, LANES), F32))
    return jnp.concatenate(rows, axis=1)


def _shard_views(name, full):
    if name == 'w_out':
        return full.reshape(N_DEV, full.shape[0] // N_DEV, full.shape[1])
    r, ccols = full.shape
    return full.reshape(r, N_DEV, ccols // N_DEV).transpose(1, 0, 2)


def _from_shards(name, stacked):
    if name == 'w_out':
        return stacked.reshape(-1, stacked.shape[-1])
    n, r, cc = stacked.shape
    return stacked.transpose(1, 0, 2).reshape(r, n * cc)


GROUPS_PER_BLK = N_GROUPS // N_COL_BLK


def _block_diag(t):
    eye = jnp.eye(GROUPS_PER_BLK, dtype=t.dtype)
    t4 = t.reshape(N_COL_BLK, GROUPS_PER_BLK, SSM_GROUP, SSM_STATE)
    return (t4[:, :, :, None, :] * eye[None, :, None, :, None]).reshape(N_COL_BLK, CH_BLK, COL_BLK)


def _diag_blocks(mat4):
    eye = jnp.eye(GROUPS_PER_BLK, dtype=mat4.dtype)
    m6 = mat4.reshape(N_COL_BLK, GROUPS_PER_BLK, SSM_GROUP, GROUPS_PER_BLK, SSM_STATE)
    return (m6 * eye[None, :, None, :, None]).sum(axis=3).reshape(N_GROUPS, SSM_GROUP, SSM_STATE)


def _step(x, loss_target, wts, moms, vels):
    seq = x.shape[1]
    n_valid = N_META + seq
    lp = -(-n_valid // 256) * 256
    tr = _pick(lp, [640, 256])
    tr_mid = 256
    tq = _pick(lp, [1280, 256])
    tk = _pick(lp, [640, 256])

    shard_shapes = [wts[n].shape[-2:] for n in SHARDED]
    n_shard_rows = sum(_n_rows(s) for s in shard_shapes)
    gathered = _exchange("gather_weights", None,
                         _pack([wts[n].reshape(wts[n].shape[-2:]) for n in SHARDED], n_shard_rows))
    full = {n: _from_shards(n, a) for n, a in zip(SHARDED, _unpack(gathered, shard_shapes))}

    w_in_b = jnp.concatenate([_cols_in(full['w_in']), jnp.zeros((D_MODEL, D_IN_PAD - D_IN), F32)],
                             axis=1).astype(BF16)
    wq_b = _cols_q(full['w_q_up']).astype(BF16)
    wkv_b = _cols_kv(full['w_kv_up']).astype(BF16)
    wglu_b = full['w_glu'].astype(BF16)
    wo_b = full['w_out'].astype(BF16)
    pre_w, post_w = wts['pre_norm_w'], wts['post_norm_w']
    qw, kvw, aw, sw = wts['q_norm_w'], wts['kv_norm_w'], wts['attn_out_norm_w'], wts['ssm_out_norm_w']
    bglu, dvec = wts['b_glu'], wts['ssm_d']

    lseg = lp // N_SEG
    pos = _row_position(jnp.arange(lp, dtype=jnp.int32), lseg)
    inv = ROPE_THETA ** (-jnp.arange(HALF_ROPE, dtype=F32) / HALF_ROPE)
    ang = pos.astype(F32)[:, None] * inv[None, :]
    cos, sin = jnp.cos(ang), jnp.sin(ang)
    cos8, sin8 = jnp.tile(cos, (1, HEADS)), jnp.tile(sin, (1, HEADS))
    c32 = jnp.concatenate([cos, cos], axis=1)
    s32 = jnp.concatenate([-sin, sin], axis=1)
    p32 = jnp.asarray(np.roll(np.eye(QK_ROPE, dtype=np.float32), HALF_ROPE, axis=1))
    sum8 = jnp.asarray(np.tile(np.eye(QK_ROPE, dtype=np.float32), (HEADS, 1)))
    head_sum = jnp.asarray(np.repeat(np.eye(HEADS, dtype=np.float32), V_HEAD, axis=0))

    ng = 2 * N_GROUPS
    a_re3 = wts['ssm_a_re'].reshape(ng, 1, SSM_STATE)
    a_im3 = wts['ssm_a_im'].reshape(ng, 1, SSM_STATE)
    ldt3 = wts['ssm_log_dt'].reshape(ng, 1, 1)
    bt_re = wts['ssm_b_re'].reshape(2, N_GROUPS, SSM_STATE, SSM_GROUP).transpose(0, 1, 3, 2).reshape(
        ng, SSM_GROUP, SSM_STATE)
    bt_im = wts['ssm_b_im'].reshape(2, N_GROUPS, SSM_STATE, SSM_GROUP).transpose(0, 1, 3, 2).reshape(
        ng, SSM_GROUP, SSM_STATE)
    c_re = wts['ssm_c_re'].reshape(ng, SSM_GROUP, SSM_STATE)
    c_im = wts['ssm_c_im'].reshape(ng, SSM_GROUP, SSM_STATE)
    abar_re, abar_im, bbt_re, bbt_im = _ssm_disc(a_re3, a_im3, ldt3, bt_re, bt_im)

    def direction(t, d):
        return t[d * N_GROUPS:(d + 1) * N_GROUPS]

    def slab(t, d, sign=1.0):
        return jnp.broadcast_to(sign * direction(t, d).reshape(1, N_STATES), (N_SEG, N_STATES))

    w_re = [_block_diag(direction(bbt_re, d)).astype(BF16) for d in range(2)]
    w_im = [_block_diag(direction(bbt_im, d)).astype(BF16) for d in range(2)]
    cb_re = [_block_diag(direction(c_re, d)).astype(BF16) for d in range(2)]
    cb_im = [_block_diag(-direction(c_im, d)).astype(BF16) for d in range(2)]

    def to_rows(a):
        return a.reshape(N_SEG, lseg, a.shape[-1]).transpose(1, 0, 2).reshape(lp, a.shape[-1])

    def to_tokens(a):
        return a.reshape(lseg, N_SEG, a.shape[-1]).transpose(1, 0, 2).reshape(lp, a.shape[-1])

    pad = jnp.zeros((lp - n_valid, D_MODEL), F32)
    h = to_rows(jnp.concatenate([full['meta_tokens'], x[0], pad], axis=0))
    tgt = to_rows(jnp.concatenate([jnp.zeros((N_META, D_MODEL), F32), loss_target[0], pad], axis=0))

    ql, kvl, ag, su, sg, kr = _inproj(h, pre_w, w_in_b, tr)
    qn_b, qr1_b, qr2_b, kn_b, v_b, kr_b = _qkv_up(ql, kvl, kr, cos8, sin8, c32, s32, qw, kvw, wq_b, wkv_b, p32, tr)

    def heads(a, w):
        return a.reshape(lp, HEADS, w)

    nq, nk = lp // tq, lp // tk
    q_t = jnp.concatenate([heads(qn_b, 64), heads(qr1_b, 16), heads(qr2_b, 16)], axis=-1)
    k_t = jnp.concatenate([heads(kn_b, 64), jnp.broadcast_to(kr_b[:, None, :], (lp, HEADS, QK_ROPE))], axis=-1)
    v_t = heads(v_b, 64)
    vx_t = jnp.concatenate([v_t, jnp.ones((lp, HEADS, 1), BF16), jnp.zeros((lp, HEADS, LANES - V_HEAD - 1), BF16)],
                           axis=-1)
    qt4 = q_t.reshape(nq, tq, HEADS, QK_DIM).transpose(2, 0, 3, 1)
    tk_fwd = _pick(lp, [1280, 256])
    vxt4 = vx_t.reshape(lp // tk_fwd, tk_fwd, HEADS, LANES).transpose(2, 0, 3, 1)
    k_h = k_t.transpose(1, 0, 2)
    kt_h = k_t.transpose(1, 2, 0)
    v_h = v_t.transpose(1, 0, 2)
    ot_h, lse4 = _attn_fwd(qt4, k_h, vxt4, n_valid)
    o_flat = ot_h.transpose(2, 0, 1).reshape(lp, D_ATTN)

    xs, ys = [], []
    for d in range(2):
        ar8, ai8 = slab(abar_re, d), slab(abar_im, d)
        ere, eim = _scan_ends(f"scan{d}_ends", su, w_re[d], w_im[d], ar8, ai8, d == 0)
        x_re, x_im, y_d = _scan_fwd(f"scan{d}", su, w_re[d], w_im[d], ar8, ai8, ere, eim, cb_re[d], cb_im[d],
                                    d == 0)
        xs += [x_re, x_im]
        ys.append(y_d)
    ypre, glu, ysn = _ssm_post(ys[0], ys[1], su, sg, wglu_b, bglu, sw, dvec, tr)

    dy, dout, loss, d_post = _out_loss(o_flat, ag, ysn, h, tgt, wo_b, post_w, aw, n_valid, tr)

    do_flat, dag, dysn, delta8, d_wo, d_aw = _out_bwd(dy, ysn, o_flat, ag, wo_b, aw, head_sum, tr)
    dypre, dsg, d_wglu, d_bglu, d_sw, d_dvec = _ssm_post_bwd(dysn, glu, sg, ypre, su, wglu_b, sw, dvec, tr)

    dus, d_ct, d_wb, d_a8 = [], [], [], []
    for d in range(2):
        ar8, ai8c = slab(abar_re, d), slab(abar_im, d, -1.0)
        ere, eim = _scan_ends(f"scan_adj{d}_ends", dypre, cb_re[d], cb_im[d], ar8, ai8c, d != 0)
        du_d, dw_re, dw_im, dc_re, dc_im, da_re, da_im = _scan_bwd(
            f"scan_adj{d}", dypre, cb_re[d], cb_im[d], ar8, ai8c, ere, eim, su, w_re[d], w_im[d],
            xs[2 * d], xs[2 * d + 1], d != 0)
        dus.append(du_d)
        d_ct.append((dc_re, dc_im))
        d_wb.append((dw_re, dw_im))
        d_a8.append((da_re, da_im))

    dot4 = do_flat.astype(BF16).reshape(nq, tq, HEADS, V_HEAD).transpose(2, 0, 3, 1)
    dqt4, dkt_h, dvt_h = _attn_bwd(qt4, k_h, kt_h, v_h, dot4, lse4, delta8.T.reshape(HEADS, nq, 1, tq), tk)
    dq_t = dqt4.transpose(1, 3, 0, 2).reshape(lp, HEADS, QK_DIM)
    dk_t = dkt_h.transpose(2, 0, 1)
    dqn = dq_t[:, :, :64].reshape(lp, 512)
    dr1 = dq_t[:, :, 64:80].reshape(lp, 128)
    dr2 = dq_t[:, :, 80:96].reshape(lp, 128)
    dkn = dk_t[:, :, :64].reshape(lp, 512)
    dkr8 = dk_t[:, :, 64:].reshape(lp, HEADS * QK_ROPE)
    dvf = dvt_h.transpose(2, 0, 1).reshape(lp, 512)
    dql, dkvl, dkrr, d_wq, d_wkv, d_qw, d_kvw = _qkv_up_bwd(
        dqn, dr1, dr2, dkn, dvf, dkr8, ql, kvl, cos8, sin8, c32, s32, qw, kvw, wq_b, wkv_b, p32, sum8, tr)
    dh, d_win, d_pre = _inproj_bwd(dql, dkvl, dag, dus[0], dus[1], dypre, dsg, dkrr, h, dout, pre_w, w_in_b, dvec,
                                   tr_mid)
    dh = to_tokens(dh)

    def seg_sums(t):
        return t.reshape(N_SEG, N_GROUPS, SSM_STATE).transpose(1, 0, 2)

    da8_re = jnp.concatenate([seg_sums(d_a8[d][0]) for d in range(2)], axis=0)
    da8_im = jnp.concatenate([seg_sums(d_a8[d][1]) for d in range(2)], axis=0)
    dbb_re = jnp.concatenate([_diag_blocks(d_wb[d][0]) for d in range(2)], axis=0)
    dbb_im = jnp.concatenate([_diag_blocks(d_wb[d][1]) for d in range(2)], axis=0)
    g_are, g_aim, g_ldt, g_bt_re, g_bt_im = _ssm_disc_bwd(a_re3, a_im3, ldt3, bt_re, bt_im, da8_re, da8_im,
                                                          dbb_re, dbb_im)
    g_c_re = jnp.concatenate([_diag_blocks(d_ct[d][0]) for d in range(2)], axis=0)
    g_c_im = jnp.concatenate([-_diag_blocks(d_ct[d][1]) for d in range(2)], axis=0)

    def b_layout(t):
        return t.reshape(2, N_GROUPS, SSM_GROUP, SSM_STATE).transpose(0, 1, 3, 2)

    local = {
        'meta_tokens': dh[:N_META],
        'pre_norm_w': d_pre, 'post_norm_w': d_post,
        'w_in': _cols_in_inv(d_win[:, :D_IN]),
        'q_norm_w': d_qw, 'w_q_up': _cols_q_inv(d_wq),
        'kv_norm_w': d_kvw, 'w_kv_up': _cols_kv_inv(d_wkv),
        'attn_out_norm_w': d_aw,
        'ssm_a_re': g_are, 'ssm_a_im': g_aim, 'ssm_log_dt': g_ldt,
        'ssm_b_re': b_layout(g_bt_re), 'ssm_b_im': b_layout(g_bt_im), 'ssm_c_re': g_c_re, 'ssm_c_im': g_c_im,
        'ssm_d': d_dvec, 'w_glu': d_wglu, 'b_glu': d_bglu, 'ssm_out_norm_w': d_sw, 'w_out': d_wo,
    }

    replicated = [n for n in WEIGHTS if n not in SHARDED]
    order = SHARDED + replicated
    shapes = [wts[n].shape for n in order] + [(1, 1)]
    tr_adam = 512
    total_rows = -(-sum(_n_rows(s) for s in shapes) // tr_adam) * tr_adam
    recv = _exchange("exchange_grads",
                     _pack_per_device([_shard_views(n, local[n]) for n in SHARDED], n_shard_rows),
                     _pack([local[n] for n in replicated] + [loss], total_rows - n_shard_rows))
    zero = jnp.zeros((1, 1), F32)
    packed = [_pack([src[n] for n in order] + [zero], total_rows) for src in (wts, moms, vels)]
    g_p, d_p, m_p, v_p = _adamw(recv, *packed, tr_adam)
    sums = _unpack(g_p, shapes)
    grads = dict(zip(order, sums))
    deltas, new_m, new_v = (dict(zip(order, _unpack(b, shapes))) for b in (d_p, m_p, v_p))

    grad_x = dh[N_META:n_valid][None]
    return (sums[-1][0, 0], grad_x, *[grads[n] for n in WEIGHTS], *[deltas[n] for n in WEIGHTS],
            *[new_m[n] for n in WEIGHTS], *[new_v[n] for n in WEIGHTS])


def kernel(x, meta_tokens, pre_norm_w, post_norm_w, w_in, q_norm_w, w_q_up, kv_norm_w, w_kv_up, attn_out_norm_w, ssm_a_re, ssm_a_im, ssm_log_dt, ssm_b_re, ssm_b_im, ssm_c_re, ssm_c_im, ssm_d, w_glu, b_glu, ssm_out_norm_w, w_out, loss_target, m_meta_tokens, m_pre_norm_w, m_post_norm_w, m_w_in, m_q_norm_w, m_w_q_up, m_kv_norm_w, m_w_kv_up, m_attn_out_norm_w, m_ssm_a_re, m_ssm_a_im, m_ssm_log_dt, m_ssm_b_re, m_ssm_b_im, m_ssm_c_re, m_ssm_c_im, m_ssm_d, m_w_glu, m_b_glu, m_ssm_out_norm_w, m_w_out, v_meta_tokens, v_pre_norm_w, v_post_norm_w, v_w_in, v_q_norm_w, v_w_q_up, v_kv_norm_w, v_w_kv_up, v_attn_out_norm_w, v_ssm_a_re, v_ssm_a_im, v_ssm_log_dt, v_ssm_b_re, v_ssm_b_im, v_ssm_c_re, v_ssm_c_im, v_ssm_d, v_w_glu, v_b_glu, v_ssm_out_norm_w, v_w_out):
    wts = dict(zip(WEIGHTS, (meta_tokens, pre_norm_w, post_norm_w, w_in, q_norm_w, w_q_up, kv_norm_w, w_kv_up,
                             attn_out_norm_w, ssm_a_re, ssm_a_im, ssm_log_dt, ssm_b_re, ssm_b_im, ssm_c_re,
                             ssm_c_im, ssm_d, w_glu, b_glu, ssm_out_norm_w, w_out)))
    moms = dict(zip(WEIGHTS, (m_meta_tokens, m_pre_norm_w, m_post_norm_w, m_w_in, m_q_norm_w, m_w_q_up,
                              m_kv_norm_w, m_w_kv_up, m_attn_out_norm_w, m_ssm_a_re, m_ssm_a_im, m_ssm_log_dt,
                              m_ssm_b_re, m_ssm_b_im, m_ssm_c_re, m_ssm_c_im, m_ssm_d, m_w_glu, m_b_glu,
                              m_ssm_out_norm_w, m_w_out)))
    vels = dict(zip(WEIGHTS, (v_meta_tokens, v_pre_norm_w, v_post_norm_w, v_w_in, v_q_norm_w, v_w_q_up,
                              v_kv_norm_w, v_w_kv_up, v_attn_out_norm_w, v_ssm_a_re, v_ssm_a_im, v_ssm_log_dt,
                              v_ssm_b_re, v_ssm_b_im, v_ssm_c_re, v_ssm_c_im, v_ssm_d, v_w_glu, v_b_glu,
                              v_ssm_out_norm_w, v_w_out)))
    return _step(x, loss_target, wts, moms, vels)
```

```python
import functools
import math

import numpy as np
import jax
import jax.numpy as jnp
from jax import lax
from jax.experimental import pallas as pl
from jax.experimental.pallas import tpu as pltpu

F32 = jnp.float32
BF16 = jnp.bfloat16

D_MODEL = 1024
N_META = 16
EPS = 1e-6
HEADS = 8
QK_NOPE = 64
QK_ROPE = 32
HALF_ROPE = QK_ROPE // 2
QK_DIM = QK_NOPE + QK_ROPE
V_HEAD = 64
Q_LORA = 256
KV_LORA = 128
D_ATTN = HEADS * V_HEAD
D_SSM = 512
SSM_GROUP = 16
N_GROUPS = D_SSM // SSM_GROUP
SSM_STATE = 64
N_STATES = N_GROUPS * SSM_STATE
ROPE_THETA = 10000.0
D_IN = Q_LORA + KV_LORA + QK_ROPE + D_ATTN + 2 * D_SSM
D_IN_PAD = 2048
N_DEV = 8
N_SEG = 8
COL_BLK = 512
LANES = 128

ADAM_LR = 0.001
ADAM_B1 = 0.9
ADAM_B2 = 0.999
ADAM_EPS = 1e-08
ADAM_WD = 0.01
ADAM_STEP = 10

VMEM_LIMIT_V7X = 56 * 1024 * 1024
LOG2E = 1.0 / math.log(2.0)
Q_SCALE = LOG2E / math.sqrt(QK_DIM)
ATTN_UNROLL = 4
ATTN_BWD_UNROLL = 4

WEIGHTS = ['meta_tokens', 'pre_norm_w', 'post_norm_w', 'w_in', 'q_norm_w', 'w_q_up', 'kv_norm_w', 'w_kv_up',
           'attn_out_norm_w', 'ssm_a_re', 'ssm_a_im', 'ssm_log_dt', 'ssm_b_re', 'ssm_b_im', 'ssm_c_re', 'ssm_c_im',
           'ssm_d', 'w_glu', 'b_glu', 'ssm_out_norm_w', 'w_out']
SHARDED = ['w_in', 'w_q_up', 'w_kv_up', 'w_glu', 'w_out', 'meta_tokens']

def _cols_in(w):
    return jnp.concatenate([w[:, 0:384], w[:, 416:D_IN], w[:, 384:416]], axis=1)


def _cols_in_inv(w):
    return jnp.concatenate([w[:, 0:384], w[:, D_IN - QK_ROPE:D_IN], w[:, 384:D_IN - QK_ROPE]], axis=1)


def _cols_q(w):
    t = w.reshape(w.shape[0], HEADS, QK_DIM)
    return jnp.concatenate([t[:, :, 0:64].reshape(-1, 512), t[:, :, 64:80].reshape(-1, 128),
                            t[:, :, 80:96].reshape(-1, 128)], axis=1)


def _cols_q_inv(w):
    r = w.shape[0]
    return jnp.concatenate([w[:, 0:512].reshape(r, HEADS, 64), w[:, 512:640].reshape(r, HEADS, 16),
                            w[:, 640:768].reshape(r, HEADS, 16)], axis=2).reshape(r, HEADS * QK_DIM)


def _cols_kv(w):
    t = w.reshape(w.shape[0], HEADS, 128)
    return jnp.concatenate([t[:, :, 0:64].reshape(-1, 512), t[:, :, 64:128].reshape(-1, 512)], axis=1)


def _cols_kv_inv(w):
    r = w.shape[0]
    return jnp.concatenate([w[:, 0:512].reshape(r, HEADS, 64), w[:, 512:1024].reshape(r, HEADS, 64)],
                           axis=2).reshape(r, HEADS * 128)


def _pick(n, cands):
    for c in cands:
        if n % c == 0:
            return c
    raise ValueError(f"no tile for {n}")


def _cparams(*sem):
    return pltpu.CompilerParams(dimension_semantics=sem, vmem_limit_bytes=VMEM_LIMIT_V7X)


def _mm(a, b):
    return jnp.dot(a.astype(BF16), b.astype(BF16), preferred_element_type=F32)


def _mm_nt(a, b):
    return lax.dot_general(a.astype(BF16), b.astype(BF16), (((1,), (1,)), ((), ())), preferred_element_type=F32)


def _mm_tn(a, b):
    return lax.dot_general(a.astype(BF16), b.astype(BF16), (((0,), (0,)), ((), ())), preferred_element_type=F32)


def _mm_exact(a, b):
    return jnp.dot(a, b, precision=lax.Precision.HIGHEST, preferred_element_type=F32)


def _rms(x):
    return lax.rsqrt(jnp.mean(x * x, axis=-1, keepdims=True) + EPS)


def _rms_bwd(dy, x, r, w):
    xh = x * r
    g = dy * w
    dx = r * (g - xh * jnp.mean(g * xh, axis=-1, keepdims=True))
    dw = jnp.sum(dy * xh, axis=0, keepdims=True)
    return dx, dw


def _sigmoid(z):
    return 1.0 / (1.0 + jnp.exp(-z))


def _silu_and_grad(z):
    s = _sigmoid(z)
    return z * s, s * (1.0 + z * (1.0 - s))


_GELU_C = math.sqrt(2.0 / math.pi)


def _gelu_and_grad(x):
    x2 = x * x
    t = jnp.tanh(_GELU_C * (x + 0.044715 * x * x2))
    val = 0.5 * x * (1.0 + t)
    grad = 0.5 * (1.0 + t) + 0.5 * x * (1.0 - t * t) * _GELU_C * (1.0 + 3.0 * 0.044715 * x2)
    return val, grad


def _acc(ref, val, first):
    @pl.when(first)
    def _():
        ref[...] = val

    @pl.when(jnp.logical_not(first))
    def _():
        ref[...] += val


def _rows_call(name, body, tr, row_ins, full_ins, row_outs, acc_outs):
    lp = row_ins[0].shape[0]
    in_specs = [pl.BlockSpec((tr, a.shape[1]), lambda i: (i, 0)) for a in row_ins]
    in_specs += [pl.BlockSpec(a.shape, lambda i, n=a.ndim: (0,) * n) for a in full_ins]
    out_specs = [pl.BlockSpec((tr, c), lambda i: (i, 0)) for c, _ in row_outs]
    out_specs += [pl.BlockSpec(s, lambda i, n=len(s): (0,) * n) for s, _ in acc_outs]
    out_shape = [jax.ShapeDtypeStruct((lp, c), dt) for c, dt in row_outs]
    out_shape += [jax.ShapeDtypeStruct(s, dt) for s, dt in acc_outs]
    return pl.pallas_call(
        body, name=name, grid=(lp // tr,), in_specs=in_specs, out_specs=out_specs, out_shape=out_shape,
        compiler_params=_cparams("arbitrary"))(*row_ins, *full_ins)


def _inproj(h, pre_w, w_in_b, tr):
    def body(h_ref, pw_ref, w_ref, ql, kvl, ag, su, sg, kr):
        x = h_ref[...]
        xn = x * _rms(x) * pw_ref[...]
        pr = _mm(xn, w_ref[...])
        ql[...] = pr[:, 0:256]
        kvl[...] = pr[:, 256:384]
        ag[...] = pr[:, 384:896]
        su[...] = pr[:, 896:1408]
        sg[...] = pr[:, 1408:1920]
        kr[...] = pr[:, 1920:1952]

    return _rows_call("inproj", body, tr, [h], [pre_w, w_in_b],
                      [(256, F32), (128, F32), (512, F32), (512, F32), (512, F32), (32, F32)], [])


def _qkv_up(ql, kvl, kr, cos8, sin8, c32, s32, qw, kvw, wq_b, wkv_b, p32, tr):
    def body(ql_ref, kvl_ref, kr_ref, cos_ref, sin_ref, c32_ref, s32_ref, qw_ref, kvw_ref, wq_ref, wkv_ref, p_ref,
             qn_o, qr1_o, qr2_o, kn_o, v_o, kr_o):
        x = ql_ref[...]
        q = _mm(x * _rms(x) * qw_ref[...], wq_ref[...]) * Q_SCALE
        r1, r2 = q[:, 512:640], q[:, 640:768]
        cs, sn = cos_ref[...], sin_ref[...]
        qn_o[...] = q[:, 0:512].astype(BF16)
        qr1_o[...] = (r1 * cs - r2 * sn).astype(BF16)
        qr2_o[...] = (r2 * cs + r1 * sn).astype(BF16)
        x = kvl_ref[...]
        kv = _mm(x * _rms(x) * kvw_ref[...], wkv_ref[...])
        kn_o[...] = kv[:, 0:512].astype(BF16)
        v_o[...] = kv[:, 512:1024].astype(BF16)
        x = kr_ref[...]
        kr_o[...] = (x * c32_ref[...] + _mm_exact(x, p_ref[...]) * s32_ref[...]).astype(BF16)

    return _rows_call("qkv_up", body, tr, [ql, kvl, kr, cos8, sin8, c32, s32], [qw, kvw, wq_b, wkv_b, p32],
                      [(512, BF16), (128, BF16), (128, BF16), (512, BF16), (512, BF16), (32, BF16)], [])


def _row_position(row, lseg):
    return (row & (N_SEG - 1)) * lseg + (row >> 3)


def _first_padded_tile(n_valid, lp, tile):
    lseg = lp // N_SEG
    t0 = n_valid - (N_SEG - 1) * lseg
    return (t0 * N_SEG + N_SEG - 1) // tile if n_valid < lp else lp // tile


def _attn_fwd(qt, k, vxt, n_valid):
    _, nq, _, tq = qt.shape
    _, nk, _, tk = vxt.shape
    lp = k.shape[1]
    lseg = lp // N_SEG
    n_plain = max(0, min(nk, _first_padded_tile(n_valid, lp, tk)))

    def body(q_ref, k_ref, v_ref, o_ref, lse_ref, m_s, acc_s):
        m_s[...] = jnp.full(m_s.shape, -1e30, F32)
        acc_s[...] = jnp.zeros(acc_s.shape, F32)
        qq = q_ref[0, 0]

        def chunk(c, padded):
            r0 = pl.multiple_of(c * tk, tk)
            st = _mm(k_ref[0, pl.ds(r0, tk), :], qq)
            if padded:
                row = r0 + lax.broadcasted_iota(jnp.int32, (tk, tq), 0)
                st = jnp.where(_row_position(row, lseg) < n_valid, st, -1e30)
            m_old = m_s[...]
            m_new = jnp.maximum(m_old, jnp.max(st, axis=0, keepdims=True))
            pt = jnp.exp2(st - m_new)
            acc_s[...] = jnp.exp2(m_old - m_new) * acc_s[...] + _mm(v_ref[0, c], pt)
            m_s[...] = m_new

        def plain(c, carry):
            chunk(c, False)
            return carry

        n_loop = n_plain - n_plain % ATTN_UNROLL
        if n_loop:
            lax.fori_loop(0, n_loop, plain, 0, unroll=ATTN_UNROLL)
        for c in range(n_loop, nk):
            chunk(c, c >= n_plain)
        acc = acc_s[...]
        l = acc[V_HEAD:V_HEAD + 1, :]
        o_ref[0] = acc[:V_HEAD, :] / l
        lse_ref[0, 0] = m_s[...] + jnp.log2(l)

    return pl.pallas_call(
        body, name="attn_fwd", grid=(HEADS, nq),
        in_specs=[pl.BlockSpec((1, 1, QK_DIM, tq), lambda h, i: (h, i, 0, 0)),
                  pl.BlockSpec((1, lp, QK_DIM), lambda h, i: (h, 0, 0)),
                  pl.BlockSpec((1, nk, LANES, tk), lambda h, i: (h, 0, 0, 0))],
        out_specs=[pl.BlockSpec((1, V_HEAD, tq), lambda h, i: (h, 0, i)),
                   pl.BlockSpec((1, 1, 1, tq), lambda h, i: (h, i, 0, 0))],
        out_shape=[jax.ShapeDtypeStruct((HEADS, V_HEAD, lp), F32), jax.ShapeDtypeStruct((HEADS, nq, 1, tq), F32)],
        scratch_shapes=[pltpu.VMEM((1, tq), F32), pltpu.VMEM((LANES, tq), F32)],
        compiler_params=_cparams("arbitrary", "arbitrary"))(qt, k, vxt)


def _scan_tiles(lp, longest):
    lseg = lp // N_SEG
    tt = _pick(lseg, [longest, 48, 32, 16, 8, 4, 2, 1])
    return lseg, tt, lseg // tt


def _cmul(ar, ai, br, bi):
    return ar * br - ai * bi, ar * bi + ai * br


SCAN_STEPS_PER_ITER = 4
N_COL_BLK = N_STATES // COL_BLK
CH_BLK = D_SSM // N_COL_BLK


def _scan_steps(tt, forward, bre_ref, bim_ref, ar, ai, carry, visit):
    def step(s, c):
        r0 = pl.multiple_of((s if forward else tt - 1 - s) * N_SEG, N_SEG)
        pr, pi = _cmul(ar, ai, c[0], c[1])
        xr = pr + bre_ref[pl.ds(r0, N_SEG), :]
        xi = pi + bim_ref[pl.ds(r0, N_SEG), :]
        return (xr, xi) + tuple(visit(r0, (xr, xi), (c[0], c[1]), c[2:]))

    per = SCAN_STEPS_PER_ITER if tt % SCAN_STEPS_PER_ITER == 0 else 1

    def steps(it, c):
        for u in range(per):
            c = step(it * per + u, c)
        return c

    return lax.fori_loop(0, tt // per, steps, carry)


def _segment_starts(lseg, forward, ar, ai, ere_ref, eim_ref, s_re, s_im):
    a1r, a1i = ar[0:1, :], ai[0:1, :]
    pr, pi = jnp.ones_like(a1r), jnp.zeros_like(a1i)
    br, bi = a1r, a1i
    n = lseg
    while n:
        if n & 1:
            pr, pi = _cmul(pr, pi, br, bi)
        n >>= 1
        if n:
            br, bi = _cmul(br, bi, br, bi)
    cr, ci = jnp.zeros_like(a1r), jnp.zeros_like(a1i)
    for j in (range(N_SEG) if forward else range(N_SEG - 1, -1, -1)):
        s_re[j:j + 1, :] = cr
        s_im[j:j + 1, :] = ci
        nr, ni = _cmul(pr, pi, cr, ci)
        cr = nr + ere_ref[j:j + 1, :]
        ci = ni + eim_ref[j:j + 1, :]


def _scan_specs(lp, forward, longest=208):
    lseg, tt, nt = _scan_tiles(lp, longest)

    def tile(t):
        return t if forward else nt - 1 - t

    rows = lambda w: pl.BlockSpec((tt * N_SEG, w), lambda cb, t: (tile(t), cb))
    proj = pl.BlockSpec((1, CH_BLK, COL_BLK), lambda cb, t: (cb, 0, 0))
    slab = pl.BlockSpec((N_SEG, COL_BLK), lambda cb, t: (0, cb))
    return lseg, tt, nt, rows, proj, slab


def _scan_ends(name, urows, wre4, wim4, ar8, ai8, forward):
    lp = urows.shape[0]
    lseg, tt, nt, rows, proj, slab = _scan_specs(lp, forward, longest=520)

    def body(u_ref, wre_ref, wim_ref, ar_ref, ai_ref, ere_o, eim_o, bre_s, bim_s, cr_s, ci_s):
        t = pl.program_id(1)

        @pl.when(t == 0)
        def _():
            cr_s[...] = jnp.zeros(cr_s.shape, F32)
            ci_s[...] = jnp.zeros(ci_s.shape, F32)

        u = u_ref[...]
        bre_s[...] = _mm(u, wre_ref[0])
        bim_s[...] = _mm(u, wim_ref[0])
        cr, ci = _scan_steps(tt, forward, bre_s, bim_s, ar_ref[...], ai_ref[...], (cr_s[...], ci_s[...]),
                             lambda r0, x, x_prev, extra: ())
        cr_s[...] = cr
        ci_s[...] = ci

        @pl.when(t == nt - 1)
        def _():
            ere_o[...] = cr
            eim_o[...] = ci

    return pl.pallas_call(
        body, name=name, grid=(N_COL_BLK, nt), in_specs=[rows(CH_BLK), proj, proj, slab, slab],
        out_specs=[slab, slab], out_shape=[jax.ShapeDtypeStruct((N_SEG, N_STATES), F32)] * 2,
        scratch_shapes=[pltpu.VMEM((tt * N_SEG, COL_BLK), F32)] * 2 + [pltpu.VMEM((N_SEG, COL_BLK), F32)] * 2,
        compiler_params=_cparams("arbitrary", "arbitrary"))(urows, wre4, wim4, ar8, ai8)


def _scan_fwd(name, urows, wre4, wim4, ar8, ai8, ere, eim, cre4, cim4, forward):
    lp = urows.shape[0]
    lseg, tt, nt, rows, proj, slab = _scan_specs(lp, forward)

    def body(u_ref, wre_ref, wim_ref, ar_ref, ai_ref, ere_ref, eim_ref, cre_ref, cim_ref,
             xre_o, xim_o, y_o, bre_s, bim_s, cr_s, ci_s):
        ar, ai = ar_ref[...], ai_ref[...]

        @pl.when(pl.program_id(1) == 0)
        def _():
            _segment_starts(lseg, forward, ar, ai, ere_ref, eim_ref, cr_s, ci_s)

        u = u_ref[...]
        bre_s[...] = _mm(u, wre_ref[0])
        bim_s[...] = _mm(u, wim_ref[0])

        def visit(r0, x, x_prev, extra):
            xre_o[pl.ds(r0, N_SEG), :] = x[0]
            xim_o[pl.ds(r0, N_SEG), :] = x[1]
            return ()

        cr, ci = _scan_steps(tt, forward, bre_s, bim_s, ar, ai, (cr_s[...], ci_s[...]), visit)
        cr_s[...] = cr
        ci_s[...] = ci
        y_o[...] = _mm_nt(xre_o[...], cre_ref[0]) + _mm_nt(xim_o[...], cim_ref[0])

    return pl.pallas_call(
        body, name=name, grid=(N_COL_BLK, nt),
        in_specs=[rows(CH_BLK), proj, proj, slab, slab, slab, slab, proj, proj],
        out_specs=[rows(COL_BLK), rows(COL_BLK), rows(CH_BLK)],
        out_shape=[jax.ShapeDtypeStruct((lp, N_STATES), F32)] * 2 + [jax.ShapeDtypeStruct((lp, D_SSM), F32)],
        scratch_shapes=[pltpu.VMEM((tt * N_SEG, COL_BLK), F32)] * 2 + [pltpu.VMEM((N_SEG, COL_BLK), F32)] * 2,
        compiler_params=_cparams("arbitrary", "arbitrary"))(urows, wre4, wim4, ar8, ai8, ere, eim, cre4, cim4)


def _scan_bwd(name, dyrows, cre4, cim4, ar8, ai8, ere, eim, urows, wre4, wim4, xre, xim, forward):
    lp = urows.shape[0]
    lseg, tt, nt, rows, proj, slab = _scan_specs(lp, forward)

    def body(dy_ref, cre_ref, cim_ref, ar_ref, ai_ref, ere_ref, eim_ref, u_ref, wre_ref, wim_ref, xre_ref, xim_ref,
             du_o, dwre_o, dwim_o, dcre_o, dcim_o, dare_o, daim_o, bre_s, bim_s, gre_s, gim_s, cr_s, ci_s):
        t = pl.program_id(1)
        ar, ai = ar_ref[...], ai_ref[...]

        @pl.when(t == 0)
        def _():
            _segment_starts(lseg, forward, ar, ai, ere_ref, eim_ref, cr_s, ci_s)
            dare_o[...] = jnp.zeros(dare_o.shape, F32)
            daim_o[...] = jnp.zeros(daim_o.shape, F32)

        dy = dy_ref[...]
        bre_s[...] = _mm(dy, cre_ref[0])
        bim_s[...] = _mm(dy, cim_ref[0])

        def visit(r0, g, g_prev, sums):
            gre_s[pl.ds(r0, N_SEG), :] = g[0]
            gim_s[pl.ds(r0, N_SEG), :] = g[1]
            fr = xre_ref[pl.ds(r0, N_SEG), :]
            fi = xim_ref[pl.ds(r0, N_SEG), :]
            pr, pi = g_prev
            return sums[0] + fr * pr + fi * pi, sums[1] + fr * pi - fi * pr

        out = _scan_steps(tt, forward, bre_s, bim_s, ar, ai, (cr_s[...], ci_s[...], dare_o[...], daim_o[...]), visit)
        cr_s[...] = out[0]
        ci_s[...] = out[1]
        dare_o[...] = out[2]
        daim_o[...] = out[3]
        gre, gim = gre_s[...], gim_s[...]
        du_o[...] = _mm_nt(gre, wre_ref[0]) + _mm_nt(gim, wim_ref[0])
        u = u_ref[...]
        first = t == 0
        _acc(dwre_o, _mm_tn(u, gre)[None], first)
        _acc(dwim_o, _mm_tn(u, gim)[None], first)
        _acc(dcre_o, _mm_tn(dy, xre_ref[...])[None], first)
        _acc(dcim_o, _mm_tn(dy, xim_ref[...])[None], first)

    big = pltpu.VMEM((tt * N_SEG, COL_BLK), F32)
    small = pltpu.VMEM((N_SEG, COL_BLK), F32)
    return pl.pallas_call(
        body, name=name, grid=(N_COL_BLK, nt),
        in_specs=[rows(CH_BLK), proj, proj, slab, slab, slab, slab, rows(CH_BLK), proj, proj,
                  rows(COL_BLK), rows(COL_BLK)],
        out_specs=[rows(CH_BLK), proj, proj, proj, proj, slab, slab],
        out_shape=[jax.ShapeDtypeStruct((lp, D_SSM), F32)]
        + [jax.ShapeDtypeStruct((N_COL_BLK, CH_BLK, COL_BLK), F32)] * 4
        + [jax.ShapeDtypeStruct((N_SEG, N_STATES), F32)] * 2,
        scratch_shapes=[big, big, big, big, small, small],
        compiler_params=_cparams("arbitrary", "arbitrary"))(
            dyrows, cre4, cim4, ar8, ai8, ere, eim, urows, wre4, wim4, xre, xim)


def _ssm_post(yf, yb, u, sg, wglu_b, bglu, sw, dvec, tr):
    def body(yf_ref, yb_ref, u_ref, g_ref, w_ref, b_ref, sw_ref, d_ref, ypre_o, glu_o, ysn_o):
        ypre = yf_ref[...] + yb_ref[...] + d_ref[...] * u_ref[...]
        ypre_o[...] = ypre
        glu = _mm(_gelu_and_grad(ypre)[0], w_ref[...]) + b_ref[...]
        glu_o[...] = glu
        t = glu[:, :D_SSM] * _sigmoid(glu[:, D_SSM:]) * _silu_and_grad(g_ref[...])[0]
        ysn_o[...] = t * _rms(t) * sw_ref[...]

    return _rows_call("ssm_post", body, tr, [yf, yb, u, sg], [wglu_b, bglu, sw, dvec],
                      [(512, F32), (1024, F32), (512, F32)], [])


def _attn_gate_norm(o, gate, w):
    sl, dsl = _silu_and_grad(gate)
    t = o * sl
    r = _rms(t)
    return t * r * w, t, r, sl, dsl


def _out_loss(o_flat, ag, ysn, h, tgt, wo_b, post_w, aw, n_valid, tr):
    lseg = h.shape[0] // N_SEG

    def body(o_ref, g_ref, ys_ref, h_ref, t_ref, w_ref, pw_ref, aw_ref, dy_o, dout_o, loss_o, dpw_o):
        i = pl.program_id(0)
        ya = _attn_gate_norm(o_ref[...], g_ref[...], aw_ref[...])[0]
        y = _mm(ya, w_ref[0:D_ATTN, :]) + _mm(ys_ref[...], w_ref[D_ATTN:, :])
        r = _rms(y)
        pw = pw_ref[...]
        out = h_ref[...] + y * r * pw
        pos = _row_position(i * tr + lax.broadcasted_iota(jnp.int32, (tr, 1), 0), lseg)
        valid = jnp.logical_and(pos >= N_META, pos < n_valid)
        diff = jnp.where(valid, out - t_ref[...], 0.0)
        dout = diff * (1.0 / D_MODEL)
        dy, dpw = _rms_bwd(dout, y, r, pw)
        dy_o[...] = dy
        dout_o[...] = dout
        _acc(loss_o, 0.5 * jnp.sum(jnp.sum(diff * diff, axis=1, keepdims=True), axis=0, keepdims=True)
             * (1.0 / D_MODEL), i == 0)
        _acc(dpw_o, dpw, i == 0)

    return _rows_call("out_loss", body, tr, [o_flat, ag, ysn, h, tgt], [wo_b, post_w, aw],
                      [(1024, F32), (1024, F32)], [((1, 1), F32), ((1, D_MODEL), F32)])


def _out_bwd(dy, ysn, o_flat, ag, wo_b, aw, head_sum, tr):
    def body(dy_ref, ys_ref, o_ref, g_ref, w_ref, aw_ref, hs_ref, do_o, dag_o, dysn_o, dl_o, dwo_o, daw_o):
        i = pl.program_id(0)
        dy = dy_ref[...]
        dcat = _mm_nt(dy, w_ref[...])
        o = o_ref[...]
        aw = aw_ref[...]
        ya, t, r, sl, dsl = _attn_gate_norm(o, g_ref[...], aw)
        cat = jnp.concatenate([ya, ys_ref[...]], axis=1)
        _acc(dwo_o, _mm_tn(cat, dy), i == 0)
        dysn_o[...] = dcat[:, D_ATTN:]
        dt, daw = _rms_bwd(dcat[:, :D_ATTN], t, r, aw)
        _acc(daw_o, daw, i == 0)
        do = dt * sl
        do_o[...] = do
        dag_o[...] = dt * o * dsl
        dl_o[...] = _mm_exact(do * o, hs_ref[...])

    return _rows_call("out_bwd", body, tr, [dy, ysn, o_flat, ag], [wo_b, aw, head_sum],
                      [(512, F32), (512, F32), (512, F32), (HEADS, F32)],
                      [((D_MODEL, D_MODEL), F32), ((1, D_ATTN), F32)])


def _ssm_post_bwd(dysn, glu, sg, ypre, u, wglu_b, sw, dvec, tr):
    def body(d_ref, glu_ref, sg_ref, y_ref, u_ref, w_ref, sw_ref, dv_ref,
             dyp_o, dsg_o, dwg_o, dbg_o, dsw_o, dd_o):
        i = pl.program_id(0)
        glu = glu_ref[...]
        a, b = glu[:, :D_SSM], glu[:, D_SSM:]
        sb = _sigmoid(b)
        ys = a * sb
        sl, dsl = _silu_and_grad(sg_ref[...])
        t = ys * sl
        dt, dsw = _rms_bwd(d_ref[...], t, _rms(t), sw_ref[...])
        _acc(dsw_o, dsw, i == 0)
        dsg_o[...] = dt * ys * dsl
        dys = dt * sl
        dglu = jnp.concatenate([dys * sb, dys * a * sb * (1.0 - sb)], axis=1)
        _acc(dbg_o, jnp.sum(dglu, axis=0, keepdims=True), i == 0)
        gel, dgel = _gelu_and_grad(y_ref[...])
        _acc(dwg_o, _mm_tn(gel, dglu), i == 0)
        dyp = _mm_nt(dglu, w_ref[...]) * dgel
        dyp_o[...] = dyp
        _acc(dd_o, jnp.sum(dyp * u_ref[...], axis=0, keepdims=True), i == 0)

    return _rows_call("ssm_post_bwd", body, tr, [dysn, glu, sg, ypre, u], [wglu_b, sw, dvec],
                      [(512, F32), (512, F32)],
                      [((D_SSM, 2 * D_SSM), F32), ((1, 2 * D_SSM), F32), ((1, D_SSM), F32), ((1, D_SSM), F32)])


def _attn_bwd(qt, k, kt, v, dot, lse_t, delta_t, tk):
    _, nq, _, tq = qt.shape
    lp = k.shape[1]
    nk = lp // tk
    assert lse_t.shape == (HEADS, nq, 1, tq) and delta_t.shape == (HEADS, nq, 1, tq)

    def body(q_ref, k_ref, kt_ref, v_ref, do_ref, lse_ref, dl_ref, dq_o, dk_o, dv_o, dk_s, dv_s):
        @pl.when(pl.program_id(1) == 0)
        def _():
            dq_o[...] = jnp.zeros(dq_o.shape, F32)

        dk_s[...] = jnp.zeros(dk_s.shape, F32)
        dv_s[...] = jnp.zeros(dv_s.shape, F32)
        kk = k_ref[0]
        kkt = kt_ref[0]
        vv = v_ref[0]

        def chunk(c, carry):
            qq = q_ref[0, c]
            dd = do_ref[0, c]
            pt = jnp.exp2(_mm(kk, qq) - lse_ref[0, c])
            dv_s[...] += _mm_nt(dd, pt)
            dst = (pt * (_mm(vv, dd) - dl_ref[0, c])).astype(BF16)
            dk_s[...] += _mm_nt(qq, dst)
            dq_o[0, c] += _mm(kkt, dst)
            return carry

        n_loop = nq - nq % ATTN_BWD_UNROLL
        if n_loop:
            lax.fori_loop(0, n_loop, chunk, 0, unroll=ATTN_BWD_UNROLL)
        for c in range(n_loop, nq):
            chunk(c, 0)
        dk_o[0] = dk_s[...]
        dv_o[0] = dv_s[...]

    head = lambda w: pl.BlockSpec((1, nq, w, tq), lambda h, j: (h, 0, 0, 0))
    rows = lambda w: pl.BlockSpec((1, tk, w), lambda h, j: (h, j, 0))
    cols = lambda w: pl.BlockSpec((1, w, tk), lambda h, j: (h, 0, j))
    return pl.pallas_call(
        body, name="attn_bwd", grid=(HEADS, nk),
        in_specs=[head(QK_DIM), rows(QK_DIM), cols(QK_DIM), rows(V_HEAD), head(V_HEAD), head(1), head(1)],
        out_specs=[head(QK_DIM), cols(QK_DIM), cols(V_HEAD)],
        out_shape=[jax.ShapeDtypeStruct((HEADS, nq, QK_DIM, tq), F32), jax.ShapeDtypeStruct((HEADS, QK_DIM, lp), F32),
                   jax.ShapeDtypeStruct((HEADS, V_HEAD, lp), F32)],
        scratch_shapes=[pltpu.VMEM((QK_DIM, tk), F32), pltpu.VMEM((V_HEAD, tk), F32)],
        compiler_params=_cparams("arbitrary", "arbitrary"))(qt, k, kt, v, dot, lse_t, delta_t)


def _qkv_up_bwd(dqn, dr1, dr2, dkn, dv, dkr8, ql, kvl, cos8, sin8, c32, s32, qw, kvw, wq_b, wkv_b, p32, sum8, tr):
    def body(dqn_ref, dr1_ref, dr2_ref, dkn_ref, dv_ref, dkr_ref, ql_ref, kvl_ref, cos_ref, sin_ref, c32_ref,
             s32_ref, qw_ref, kvw_ref, wq_ref, wkv_ref, p_ref, s8_ref,
             dql_o, dkvl_o, dkrr_o, dwq_o, dwkv_o, dqw_o, dkvw_o):
        i = pl.program_id(0)
        cs, sn = cos_ref[...], sin_ref[...]
        d1, d2 = dr1_ref[...], dr2_ref[...]
        dq = jnp.concatenate([dqn_ref[...], d1 * cs + d2 * sn, d2 * cs - d1 * sn], axis=1) * (Q_SCALE / LOG2E)
        x = ql_ref[...]
        r = _rms(x)
        qw = qw_ref[...]
        _acc(dwq_o, _mm_tn(x * r * qw, dq), i == 0)
        dx, dw = _rms_bwd(_mm_nt(dq, wq_ref[...]), x, r, qw)
        dql_o[...] = dx
        _acc(dqw_o, dw, i == 0)
        dkv = jnp.concatenate([dkn_ref[...] * (1.0 / LOG2E), dv_ref[...]], axis=1)
        x = kvl_ref[...]
        r = _rms(x)
        kvw = kvw_ref[...]
        _acc(dwkv_o, _mm_tn(x * r * kvw, dkv), i == 0)
        dx, dw = _rms_bwd(_mm_nt(dkv, wkv_ref[...]), x, r, kvw)
        dkvl_o[...] = dx
        _acc(dkvw_o, dw, i == 0)
        dkr = _mm_exact(dkr_ref[...], s8_ref[...]) * (1.0 / LOG2E)
        dkrr_o[...] = dkr * c32_ref[...] + _mm_exact(dkr * s32_ref[...], p_ref[...])

    return _rows_call("qkv_up_bwd", body, tr, [dqn, dr1, dr2, dkn, dv, dkr8, ql, kvl, cos8, sin8, c32, s32],
                      [qw, kvw, wq_b, wkv_b, p32, sum8], [(256, F32), (128, F32), (32, F32)],
                      [((Q_LORA, 768), F32), ((KV_LORA, 1024), F32), ((1, Q_LORA), F32), ((1, KV_LORA), F32)])


def _inproj_bwd(dql, dkvl, dag, du_f, du_b, dypre, dsg, dkr, h, dout, pre_w, w_in_b, dvec, tr):
    def body(dql_ref, dkvl_ref, dag_ref, duf_ref, dub_ref, dyp_ref, dsg_ref, dkr_ref, h_ref, dout_ref,
             pw_ref, w_ref, dv_ref, dh_o, dwin_o, dpw_o):
        i = pl.program_id(0)
        du = duf_ref[...] + dub_ref[...] + dv_ref[...] * dyp_ref[...]
        dproj = jnp.concatenate([dql_ref[...], dkvl_ref[...], dag_ref[...], du, dsg_ref[...],
                                 dkr_ref[...], jnp.zeros((tr, D_IN_PAD - D_IN), F32)], axis=1)
        x = h_ref[...]
        r = _rms(x)
        pw = pw_ref[...]
        _acc(dwin_o, _mm_tn(x * r * pw, dproj), i == 0)
        dx, dw = _rms_bwd(_mm_nt(dproj, w_ref[...]), x, r, pw)
        _acc(dpw_o, dw, i == 0)
        dh_o[...] = dout_ref[...] + dx

    return _rows_call("inproj_bwd", body, tr, [dql, dkvl, dag, du_f, du_b, dypre, dsg, dkr, h, dout],
                      [pre_w, w_in_b, dvec], [(1024, F32)], [((D_MODEL, D_IN_PAD), F32), ((1, D_MODEL), F32)])


def _disc_terms(a_re, a_im, ldt):
    dt = jnp.exp(ldt)
    mag = jnp.exp(a_re * dt)
    th = a_im * dt
    cs, sn = jnp.cos(th), jnp.sin(th)
    abar_re, abar_im = mag * cs, mag * sn
    num_re, num_im = abar_re - 1.0, abar_im
    den = a_re * a_re + a_im * a_im
    coef_re = (num_re * a_re + num_im * a_im) / den
    coef_im = (num_im * a_re - num_re * a_im) / den
    return dt, mag, cs, sn, abar_re, abar_im, num_re, num_im, den, coef_re, coef_im


def _ssm_disc(a_re, a_im, ldt, bt_re, bt_im):
    def body(ar_ref, ai_ref, l_ref, br_ref, bi_ref, abr_o, abi_o, bbr_o, bbi_o):
        t = _disc_terms(ar_ref[...], ai_ref[...], l_ref[...])
        abr_o[...] = t[4]
        abi_o[...] = t[5]
        cr, ci = t[9], t[10]
        br, bi = br_ref[...], bi_ref[...]
        bbr_o[...] = cr * br - ci * bi
        bbi_o[...] = cr * bi + ci * br

    ng = a_re.shape[0]
    return pl.pallas_call(
        body, name="ssm_disc",
        out_shape=[jax.ShapeDtypeStruct((ng, 1, SSM_STATE), F32)] * 2
        + [jax.ShapeDtypeStruct((ng, SSM_GROUP, SSM_STATE), F32)] * 2)(a_re, a_im, ldt, bt_re, bt_im)


def _ssm_disc_bwd(a_re, a_im, ldt, bt_re, bt_im, da8_re, da8_im, dbb_re, dbb_im):
    def body(ar_ref, ai_ref, l_ref, br_ref, bi_ref, dar_ref, dai_ref, dbr_ref, dbi_ref,
             gar_o, gai_o, gl_o, gbr_o, gbi_o):
        a_re, a_im = ar_ref[...], ai_ref[...]
        dt, mag, cs, sn, abar_re, abar_im, num_re, num_im, den, cr, ci = _disc_terms(a_re, a_im, l_ref[...])
        br, bi = br_ref[...], bi_ref[...]
        dbr, dbi = dbr_ref[...], dbi_ref[...]
        gbr_o[...] = cr * dbr + ci * dbi
        gbi_o[...] = cr * dbi - ci * dbr
        dcr = jnp.sum(br * dbr + bi * dbi, axis=1, keepdims=True)
        dci = jnp.sum(br * dbi - bi * dbr, axis=1, keepdims=True)
        dnum_re = (dcr * a_re - dci * a_im) / den
        dnum_im = (dcr * a_im + dci * a_re) / den
        dden = -(dcr * cr + dci * ci) / den
        g_are = (dcr * num_re + dci * num_im) / den + dden * 2.0 * a_re
        g_aim = (dcr * num_im - dci * num_re) / den + dden * 2.0 * a_im
        d_abr = jnp.sum(dar_ref[...], axis=1, keepdims=True) + dnum_re
        d_abi = jnp.sum(dai_ref[...], axis=1, keepdims=True) + dnum_im
        dmag = d_abr * cs + d_abi * sn
        dth = d_abi * abar_re - d_abr * abar_im
        g_are = g_are + dmag * mag * dt
        g_aim = g_aim + dth * dt
        ddt = jnp.sum(dmag * mag * a_re + dth * a_im, axis=2, keepdims=True)
        gar_o[...] = g_are
        gai_o[...] = g_aim
        gl_o[...] = ddt * dt

    ng = a_re.shape[0]
    return pl.pallas_call(
        body, name="ssm_disc_bwd",
        out_shape=[jax.ShapeDtypeStruct((ng, 1, SSM_STATE), F32)] * 2 + [jax.ShapeDtypeStruct((ng, 1, 1), F32)]
        + [jax.ShapeDtypeStruct((ng, SSM_GROUP, SSM_STATE), F32)] * 2)(
            a_re, a_im, ldt, bt_re, bt_im, da8_re, da8_im, dbb_re, dbb_im)


def _exchange(name, per_peer, shared):
    parts = [a for a in (per_peer, shared) if a is not None]
    rp = per_peer.shape[1] if per_peer is not None else 0
    rs = shared.shape[0] if shared is not None else 0
    n = len(parts)

    def body(*refs):
        in_refs, out_ref, send_sems, recv_sems, local_sems = refs[:n], refs[n], refs[n + 1], refs[n + 2], refs[n + 3]
        x, y, c = lax.axis_index("x"), lax.axis_index("y"), lax.axis_index("c")
        me = 4 * x + 2 * y + c

        def pieces(peer):
            out = []
            if per_peer is not None:
                out.append((in_refs[0].at[peer], out_ref.at[me, pl.ds(0, rp), :]))
            if shared is not None:
                out.append((in_refs[-1], out_ref.at[me, pl.ds(rp, rs), :]))
            return out

        copies = []
        for k in range(1, N_DEV):
            px = 1 - x if (k >> 2) & 1 else x
            py = 1 - y if (k >> 1) & 1 else y
            pc = 1 - c if k & 1 else c
            for j, (src, dst) in enumerate(pieces(4 * px + 2 * py + pc)):
                s = (k - 1) * n + j
                copies.append(pltpu.make_async_remote_copy(
                    src_ref=src, dst_ref=dst, send_sem=send_sems.at[s], recv_sem=recv_sems.at[s],
                    device_id=(px, py, pc), device_id_type=pl.DeviceIdType.MESH))
        mine = [pltpu.make_async_copy(src, dst, local_sems.at[j]) for j, (src, dst) in enumerate(pieces(me))]
        for cp in mine + copies:
            cp.start()
        for cp in copies + mine:
            cp.wait()

    n_sem = (N_DEV - 1) * n
    return pl.pallas_call(
        body, name=name, out_shape=jax.ShapeDtypeStruct((N_DEV, rp + rs, LANES), F32),
        in_specs=[pl.BlockSpec(memory_space=pl.ANY)] * n, out_specs=pl.BlockSpec(memory_space=pl.ANY),
        scratch_shapes=[pltpu.SemaphoreType.DMA((n_sem,)), pltpu.SemaphoreType.DMA((n_sem,)),
                        pltpu.SemaphoreType.DMA((n,))])(*parts)


def _gather_two_level(name, buf):
    m_per, n = buf.shape

    def body(x_ref, out_ref, send_sems, recv_sems, local_sem):
        x, y, c = lax.axis_index("x"), lax.axis_index("y"), lax.axis_index("c")
        me, sibling = (x, y, c), (x, y, 1 - c)
        chips = [(1 - x, y), (x, 1 - y), (1 - x, 1 - y)]

        def rows(px, py, pc):
            return out_ref.at[pl.ds(pl.multiple_of((4 * px + 2 * py + pc) * m_per, 8), m_per), :]

        def copy(k, block, to, src=None):
            return pltpu.make_async_remote_copy(
                src_ref=rows(*block) if src is None else src, dst_ref=rows(*block), send_sem=send_sems.at[k],
                recv_sem=recv_sems.at[k], device_id=to, device_id_type=pl.DeviceIdType.MESH)

        mine = pltpu.make_async_copy(x_ref, rows(*me), local_sem)
        mine.start()
        first = [copy(0, me, sibling, src=x_ref)]
        first += [copy(1 + j, me, (*chip, c), src=x_ref) for j, chip in enumerate(chips)]
        for cp in first:
            cp.start()
        passed = [copy(4 + j, (*chip, c), sibling) for j, chip in enumerate(chips)]
        for j, chip in enumerate(chips):
            copy(1 + j, (*chip, c), me).wait_recv()
            passed[j].start()
        copy(0, sibling, me).wait_recv()
        for j, chip in enumerate(chips):
            copy(4 + j, (*chip, 1 - c), me).wait_recv()
        for cp in first + passed:
            cp.wait_send()
        mine.wait()

    assert m_per % 8 == 0
    out = pl.pallas_call(
        body, name=name, out_shape=jax.ShapeDtypeStruct((N_DEV * m_per, n), buf.dtype),
        in_specs=[pl.BlockSpec(memory_space=pltpu.VMEM)], out_specs=pl.BlockSpec(memory_space=pltpu.VMEM),
        scratch_shapes=[pltpu.SemaphoreType.DMA((N_DEV - 1,)), pltpu.SemaphoreType.DMA((N_DEV - 1,)),
                        pltpu.SemaphoreType.DMA(())],
        compiler_params=pltpu.CompilerParams(vmem_limit_bytes=VMEM_LIMIT_V7X))(buf)
    return out.reshape(N_DEV, m_per, n)


def _adamw(recv, w, m, v, tr):
    rows = w.shape[0]
    c1 = 1.0 - ADAM_B1 ** ADAM_STEP
    c2 = 1.0 - ADAM_B2 ** ADAM_STEP

    def body(r_ref, w_ref, m_ref, v_ref, g_o, d_o, m_o, v_o):
        g = r_ref[0]
        for k in range(1, N_DEV):
            g = g + r_ref[k]
        mm = ADAM_B1 * m_ref[...] + (1.0 - ADAM_B1) * g
        vv = ADAM_B2 * v_ref[...] + (1.0 - ADAM_B2) * (g * g)
        g_o[...] = g
        m_o[...] = mm
        v_o[...] = vv
        d_o[...] = -ADAM_LR * ((mm / c1) / (jnp.sqrt(vv / c2) + ADAM_EPS) + ADAM_WD * w_ref[...])

    spec = pl.BlockSpec((tr, LANES), lambda i: (i, 0))
    return pl.pallas_call(
        body, name="adamw", grid=(rows // tr,),
        in_specs=[pl.BlockSpec((N_DEV, tr, LANES), lambda i: (0, i, 0)), spec, spec, spec],
        out_specs=[spec] * 4, out_shape=[jax.ShapeDtypeStruct((rows, LANES), F32)] * 4,
        compiler_params=_cparams("arbitrary"))(recv, w, m, v)


def _to_rows(a):
    flat = a.reshape(-1)
    pad = (-flat.shape[0]) % LANES
    if pad:
        flat = jnp.concatenate([flat, jnp.zeros((pad,), flat.dtype)])
    return flat.reshape(-1, LANES)


def _n_rows(shape):
    return -(-int(np.prod(shape)) // LANES)


def _pack(arrays, total_rows):
    rows = [_to_rows(a) for a in arrays]
    used = sum(r.shape[0] for r in rows)
    if total_rows > used:
        rows.append(jnp.zeros((total_rows - used, LANES), F32))
    return jnp.concatenate(rows, axis=0)


def _unpack(buf, shapes):
    lead = buf.shape[:-2]
    out, r0 = [], 0
    for s in shapes:
        n = int(np.prod(s))
        nr = _n_rows(s)
        out.append(buf[..., r0:r0 + nr, :].reshape(lead + (-1,))[..., :n].reshape(lead + tuple(s)))
        r0 += nr
    return out


def _pack_per_device(arrays, total_rows):
    rows = []
    for a in arrays:
        flat = a.reshape(N_DEV, -1)
        pad = (-flat.shape[1]) % LANES
        if pad:
            flat = jnp.concatenate([flat, jnp.zeros((N_DEV, pad), flat.dtype)], axis=1)
        rows.append(flat.reshape(N_DEV, -1, LANES))
    used = sum(r.shape[1] for r in rows)
    if total_rows > used:
        rows.append(jnp.zeros((N_DEV, total_rows - used, LANES), F32))
    return jnp.concatenate(rows, axis=1)


def _shard_views(name, full):
    if name == 'w_out':
        return full.reshape(N_DEV, full.shape[0] // N_DEV, full.shape[1])
    r, ccols = full.shape
    return full.reshape(r, N_DEV, ccols // N_DEV).transpose(1, 0, 2)


def _from_shards(name, stacked):
    if name == 'w_out':
        return stacked.reshape(-1, stacked.shape[-1])
    n, r, cc = stacked.shape
    return stacked.transpose(1, 0, 2).reshape(r, n * cc)


GROUPS_PER_BLK = N_GROUPS // N_COL_BLK


def _block_diag(t):
    eye = jnp.eye(GROUPS_PER_BLK, dtype=t.dtype)
    t4 = t.reshape(N_COL_BLK, GROUPS_PER_BLK, SSM_GROUP, SSM_STATE)
    return (t4[:, :, :, None, :] * eye[None, :, None, :, None]).reshape(N_COL_BLK, CH_BLK, COL_BLK)


def _diag_blocks(mat4):
    eye = jnp.eye(GROUPS_PER_BLK, dtype=mat4.dtype)
    m6 = mat4.reshape(N_COL_BLK, GROUPS_PER_BLK, SSM_GROUP, GROUPS_PER_BLK, SSM_STATE)
    return (m6 * eye[None, :, None, :, None]).sum(axis=3).reshape(N_GROUPS, SSM_GROUP, SSM_STATE)


def _step(x, loss_target, wts, moms, vels):
    seq = x.shape[1]
    n_valid = N_META + seq
    lp = -(-n_valid // 256) * 256
    tr = _pick(lp, [640, 256])
    tr_mid = 256
    tq = _pick(lp, [1280, 256])
    tk = _pick(lp, [640, 256])

    shard_shapes = [wts[n].shape[-2:] for n in SHARDED]
    n_shard_rows = sum(_n_rows(s) for s in shard_shapes)
    gathered = _gather_two_level("gather_weights",
                                 _pack([wts[n].reshape(wts[n].shape[-2:]) for n in SHARDED], n_shard_rows))
    full = {n: _from_shards(n, a) for n, a in zip(SHARDED, _unpack(gathered, shard_shapes))}

    w_in_b = jnp.concatenate([_cols_in(full['w_in']), jnp.zeros((D_MODEL, D_IN_PAD - D_IN), F32)],
                             axis=1).astype(BF16)
    wq_b = _cols_q(full['w_q_up']).astype(BF16)
    wkv_b = _cols_kv(full['w_kv_up']).astype(BF16)
    wglu_b = full['w_glu'].astype(BF16)
    wo_b = full['w_out'].astype(BF16)
    pre_w, post_w = wts['pre_norm_w'], wts['post_norm_w']
    qw, kvw, aw, sw = wts['q_norm_w'], wts['kv_norm_w'], wts['attn_out_norm_w'], wts['ssm_out_norm_w']
    bglu, dvec = wts['b_glu'], wts['ssm_d']

    lseg = lp // N_SEG
    pos = _row_position(jnp.arange(lp, dtype=jnp.int32), lseg)
    inv = ROPE_THETA ** (-jnp.arange(HALF_ROPE, dtype=F32) / HALF_ROPE)
    ang = pos.astype(F32)[:, None] * inv[None, :]
    cos, sin = jnp.cos(ang), jnp.sin(ang)
    cos8, sin8 = jnp.tile(cos, (1, HEADS)), jnp.tile(sin, (1, HEADS))
    c32 = jnp.concatenate([cos, cos], axis=1)
    s32 = jnp.concatenate([-sin, sin], axis=1)
    p32 = jnp.asarray(np.roll(np.eye(QK_ROPE, dtype=np.float32), HALF_ROPE, axis=1))
    sum8 = jnp.asarray(np.tile(np.eye(QK_ROPE, dtype=np.float32), (HEADS, 1)))
    head_sum = jnp.asarray(np.repeat(np.eye(HEADS, dtype=np.float32), V_HEAD, axis=0))

    ng = 2 * N_GROUPS
    a_re3 = wts['ssm_a_re'].reshape(ng, 1, SSM_STATE)
    a_im3 = wts['ssm_a_im'].reshape(ng, 1, SSM_STATE)
    ldt3 = wts['ssm_log_dt'].reshape(ng, 1, 1)
    bt_re = wts['ssm_b_re'].reshape(2, N_GROUPS, SSM_STATE, SSM_GROUP).transpose(0, 1, 3, 2).reshape(
        ng, SSM_GROUP, SSM_STATE)
    bt_im = wts['ssm_b_im'].reshape(2, N_GROUPS, SSM_STATE, SSM_GROUP).transpose(0, 1, 3, 2).reshape(
        ng, SSM_GROUP, SSM_STATE)
    c_re = wts['ssm_c_re'].reshape(ng, SSM_GROUP, SSM_STATE)
    c_im = wts['ssm_c_im'].reshape(ng, SSM_GROUP, SSM_STATE)
    abar_re, abar_im, bbt_re, bbt_im = _ssm_disc(a_re3, a_im3, ldt3, bt_re, bt_im)

    def direction(t, d):
        return t[d * N_GROUPS:(d + 1) * N_GROUPS]

    def slab(t, d, sign=1.0):
        return jnp.broadcast_to(sign * direction(t, d).reshape(1, N_STATES), (N_SEG, N_STATES))

    w_re = [_block_diag(direction(bbt_re, d)).astype(BF16) for d in range(2)]
    w_im = [_block_diag(direction(bbt_im, d)).astype(BF16) for d in range(2)]
    cb_re = [_block_diag(direction(c_re, d)).astype(BF16) for d in range(2)]
    cb_im = [_block_diag(-direction(c_im, d)).astype(BF16) for d in range(2)]

    def to_rows(a):
        return a.reshape(N_SEG, lseg, a.shape[-1]).transpose(1, 0, 2).reshape(lp, a.shape[-1])

    def to_tokens(a):
        return a.reshape(lseg, N_SEG, a.shape[-1]).transpose(1, 0, 2).reshape(lp, a.shape[-1])

    pad = jnp.zeros((lp - n_valid, D_MODEL), F32)
    h = to_rows(jnp.concatenate([full['meta_tokens'], x[0], pad], axis=0))
    tgt = to_rows(jnp.concatenate([jnp.zeros((N_META, D_MODEL), F32), loss_target[0], pad], axis=0))

    ql, kvl, ag, su, sg, kr = _inproj(h, pre_w, w_in_b, tr)
    qn_b, qr1_b, qr2_b, kn_b, v_b, kr_b = _qkv_up(ql, kvl, kr, cos8, sin8, c32, s32, qw, kvw, wq_b, wkv_b, p32, tr)

    def heads(a, w):
        return a.reshape(lp, HEADS, w)

    nq, nk = lp // tq, lp // tk
    q_t = jnp.concatenate([heads(qn_b, 64), heads(qr1_b, 16), heads(qr2_b, 16)], axis=-1)
    k_t = jnp.concatenate([heads(kn_b, 64), jnp.broadcast_to(kr_b[:, None, :], (lp, HEADS, QK_ROPE))], axis=-1)
    v_t = heads(v_b, 64)
    vx_t = jnp.concatenate([v_t, jnp.ones((lp, HEADS, 1), BF16), jnp.zeros((lp, HEADS, LANES - V_HEAD - 1), BF16)],
                           axis=-1)
    qt4 = q_t.reshape(nq, tq, HEADS, QK_DIM).transpose(2, 0, 3, 1)
    tk_fwd = _pick(lp, [1280, 256])
    vxt4 = vx_t.reshape(lp // tk_fwd, tk_fwd, HEADS, LANES).transpose(2, 0, 3, 1)
    k_h = k_t.transpose(1, 0, 2)
    kt_h = k_t.transpose(1, 2, 0)
    v_h = v_t.transpose(1, 0, 2)
    ot_h, lse4 = _attn_fwd(qt4, k_h, vxt4, n_valid)
    o_flat = ot_h.transpose(2, 0, 1).reshape(lp, D_ATTN)

    xs, ys = [], []
    for d in range(2):
        ar8, ai8 = slab(abar_re, d), slab(abar_im, d)
        ere, eim = _scan_ends(f"scan{d}_ends", su, w_re[d], w_im[d], ar8, ai8, d == 0)
        x_re, x_im, y_d = _scan_fwd(f"scan{d}", su, w_re[d], w_im[d], ar8, ai8, ere, eim, cb_re[d], cb_im[d],
                                    d == 0)
        xs += [x_re, x_im]
        ys.append(y_d)
    ypre, glu, ysn = _ssm_post(ys[0], ys[1], su, sg, wglu_b, bglu, sw, dvec, tr)

    dy, dout, loss, d_post = _out_loss(o_flat, ag, ysn, h, tgt, wo_b, post_w, aw, n_valid, tr)

    do_flat, dag, dysn, delta8, d_wo, d_aw = _out_bwd(dy, ysn, o_flat, ag, wo_b, aw, head_sum, tr)
    dypre, dsg, d_wglu, d_bglu, d_sw, d_dvec = _ssm_post_bwd(dysn, glu, sg, ypre, su, wglu_b, sw, dvec, tr)

    dus, d_ct, d_wb, d_a8 = [], [], [], []
    for d in range(2):
        ar8, ai8c = slab(abar_re, d), slab(abar_im, d, -1.0)
        ere, eim = _scan_ends(f"scan_adj{d}_ends", dypre, cb_re[d], cb_im[d], ar8, ai8c, d != 0)
        du_d, dw_re, dw_im, dc_re, dc_im, da_re, da_im = _scan_bwd(
            f"scan_adj{d}", dypre, cb_re[d], cb_im[d], ar8, ai8c, ere, eim, su, w_re[d], w_im[d],
            xs[2 * d], xs[2 * d + 1], d != 0)
        dus.append(du_d)
        d_ct.append((dc_re, dc_im))
        d_wb.append((dw_re, dw_im))
        d_a8.append((da_re, da_im))

    dot4 = do_flat.astype(BF16).reshape(nq, tq, HEADS, V_HEAD).transpose(2, 0, 3, 1)
    dqt4, dkt_h, dvt_h = _attn_bwd(qt4, k_h, kt_h, v_h, dot4, lse4, delta8.T.reshape(HEADS, nq, 1, tq), tk)
    dq_t = dqt4.transpose(1, 3, 0, 2).reshape(lp, HEADS, QK_DIM)
    dk_t = dkt_h.transpose(2, 0, 1)
    dqn = dq_t[:, :, :64].reshape(lp, 512)
    dr1 = dq_t[:, :, 64:80].reshape(lp, 128)
    dr2 = dq_t[:, :, 80:96].reshape(lp, 128)
    dkn = dk_t[:, :, :64].reshape(lp, 512)
    dkr8 = dk_t[:, :, 64:].reshape(lp, HEADS * QK_ROPE)
    dvf = dvt_h.transpose(2, 0, 1).reshape(lp, 512)
    dql, dkvl, dkrr, d_wq, d_wkv, d_qw, d_kvw = _qkv_up_bwd(
        dqn, dr1, dr2, dkn, dvf, dkr8, ql, kvl, cos8, sin8, c32, s32, qw, kvw, wq_b, wkv_b, p32, sum8, tr)
    dh, d_win, d_pre = _inproj_bwd(dql, dkvl, dag, dus[0], dus[1], dypre, dsg, dkrr, h, dout, pre_w, w_in_b, dvec,
                                   tr_mid)
    dh = to_tokens(dh)

    def seg_sums(t):
        return t.reshape(N_SEG, N_GROUPS, SSM_STATE).transpose(1, 0, 2)

    da8_re = jnp.concatenate([seg_sums(d_a8[d][0]) for d in range(2)], axis=0)
    da8_im = jnp.concatenate([seg_sums(d_a8[d][1]) for d in range(2)], axis=0)
    dbb_re = jnp.concatenate([_diag_blocks(d_wb[d][0]) for d in range(2)], axis=0)
    dbb_im = jnp.concatenate([_diag_blocks(d_wb[d][1]) for d in range(2)], axis=0)
    g_are, g_aim, g_ldt, g_bt_re, g_bt_im = _ssm_disc_bwd(a_re3, a_im3, ldt3, bt_re, bt_im, da8_re, da8_im,
                                                          dbb_re, dbb_im)
    g_c_re = jnp.concatenate([_diag_blocks(d_ct[d][0]) for d in range(2)], axis=0)
    g_c_im = jnp.concatenate([-_diag_blocks(d_ct[d][1]) for d in range(2)], axis=0)

    def b_layout(t):
        return t.reshape(2, N_GROUPS, SSM_GROUP, SSM_STATE).transpose(0, 1, 3, 2)

    local = {
        'meta_tokens': dh[:N_META],
        'pre_norm_w': d_pre, 'post_norm_w': d_post,
        'w_in': _cols_in_inv(d_win[:, :D_IN]),
        'q_norm_w': d_qw, 'w_q_up': _cols_q_inv(d_wq),
        'kv_norm_w': d_kvw, 'w_kv_up': _cols_kv_inv(d_wkv),
        'attn_out_norm_w': d_aw,
        'ssm_a_re': g_are, 'ssm_a_im': g_aim, 'ssm_log_dt': g_ldt,
        'ssm_b_re': b_layout(g_bt_re), 'ssm_b_im': b_layout(g_bt_im), 'ssm_c_re': g_c_re, 'ssm_c_im': g_c_im,
        'ssm_d': d_dvec, 'w_glu': d_wglu, 'b_glu': d_bglu, 'ssm_out_norm_w': d_sw, 'w_out': d_wo,
    }

    replicated = [n for n in WEIGHTS if n not in SHARDED]
    order = SHARDED + replicated
    shapes = [wts[n].shape for n in order] + [(1, 1)]
    tr_adam = 512
    total_rows = -(-sum(_n_rows(s) for s in shapes) // tr_adam) * tr_adam
    recv = _exchange("exchange_grads",
                     _pack_per_device([_shard_views(n, local[n]) for n in SHARDED], n_shard_rows),
                     _pack([local[n] for n in replicated] + [loss], total_rows - n_shard_rows))
    zero = jnp.zeros((1, 1), F32)
    packed = [_pack([src[n] for n in order] + [zero], total_rows) for src in (wts, moms, vels)]
    g_p, d_p, m_p, v_p = _adamw(recv, *packed, tr_adam)
    sums = _unpack(g_p, shapes)
    grads = dict(zip(order, sums))
    deltas, new_m, new_v = (dict(zip(order, _unpack(b, shapes))) for b in (d_p, m_p, v_p))

    grad_x = dh[N_META:n_valid][None]
    return (sums[-1][0, 0], grad_x, *[grads[n] for n in WEIGHTS], *[deltas[n] for n in WEIGHTS],
            *[new_m[n] for n in WEIGHTS], *[new_v[n] for n in WEIGHTS])


def kernel(x, meta_tokens, pre_norm_w, post_norm_w, w_in, q_norm_w, w_q_up, kv_norm_w, w_kv_up, attn_out_norm_w, ssm_a_re, ssm_a_im, ssm_log_dt, ssm_b_re, ssm_b_im, ssm_c_re, ssm_c_im, ssm_d, w_glu, b_glu, ssm_out_norm_w, w_out, loss_target, m_meta_tokens, m_pre_norm_w, m_post_norm_w, m_w_in, m_q_norm_w, m_w_q_up, m_kv_norm_w, m_w_kv_up, m_attn_out_norm_w, m_ssm_a_re, m_ssm_a_im, m_ssm_log_dt, m_ssm_b_re, m_ssm_b_im, m_ssm_c_re, m_ssm_c_im, m_ssm_d, m_w_glu, m_b_glu, m_ssm_out_norm_w, m_w_out, v_meta_tokens, v_pre_norm_w, v_post_norm_w, v_w_in, v_q_norm_w, v_w_q_up, v_kv_norm_w, v_w_kv_up, v_attn_out_norm_w, v_ssm_a_re, v_ssm_a_im, v_ssm_log_dt, v_ssm_b_re, v_ssm_b_im, v_ssm_c_re, v_ssm_c_im, v_ssm_d, v_w_glu, v_b_glu, v_ssm_out_norm_w, v_w_out):
    wts = dict(zip(WEIGHTS, (meta_tokens, pre_norm_w, post_norm_w, w_in, q_norm_w, w_q_up, kv_norm_w, w_kv_up,
                             attn_out_norm_w, ssm_a_re, ssm_a_im, ssm_log_dt, ssm_b_re, ssm_b_im, ssm_c_re,
                             ssm_c_im, ssm_d, w_glu, b_glu, ssm_out_norm_w, w_out)))
    moms = dict(zip(WEIGHTS, (m_meta_tokens, m_pre_norm_w, m_post_norm_w, m_w_in, m_q_norm_w, m_w_q_up,
                              m_kv_norm_w, m_w_kv_up, m_attn_out_norm_w, m_ssm_a_re, m_ssm_a_im, m_ssm_log_dt,
                              m_ssm_b_re, m_ssm_b_im, m_ssm_c_re, m_ssm_c_im, m_ssm_d, m_w_glu, m_b_glu,
                              m_ssm_out_norm_w, m_w_out)))
    vels = dict(zip(WEIGHTS, (v_meta_tokens, v_pre_norm_w, v_post_norm_w, v_w_in, v_q_norm_w, v_w_q_up,
                              v_kv_norm_w, v_w_kv_up, v_attn_out_norm_w, v_ssm_a_re, v_ssm_a_im, v_ssm_log_dt,
                              v_ssm_b_re, v_ssm_b_im, v_ssm_c_re, v_ssm_c_im, v_ssm_d, v_w_glu, v_b_glu,
                              v_ssm_out_norm_w, v_w_out)))
    return _step(x, loss_target, wts, moms, vels)
```

```python
import functools
import math

import numpy as np
import jax
import jax.numpy as jnp
from jax import lax
from jax.experimental import pallas as pl
from jax.experimental.pallas import tpu as pltpu

F32 = jnp.float32
BF16 = jnp.bfloat16

D_MODEL = 1024
N_META = 16
EPS = 1e-6
HEADS = 8
QK_NOPE = 64
QK_ROPE = 32
HALF_ROPE = QK_ROPE // 2
QK_DIM = QK_NOPE + QK_ROPE
V_HEAD = 64
Q_LORA = 256
KV_LORA = 128
D_ATTN = HEADS * V_HEAD
D_SSM = 512
SSM_GROUP = 16
N_GROUPS = D_SSM // SSM_GROUP
SSM_STATE = 64
N_STATES = N_GROUPS * SSM_STATE
ROPE_THETA = 10000.0
D_IN = Q_LORA + KV_LORA + QK_ROPE + D_ATTN + 2 * D_SSM
D_IN_PAD = 2048
N_DEV = 8
N_SEG = 8
COL_BLK = 512
LANES = 128

ADAM_LR = 0.001
ADAM_B1 = 0.9
ADAM_B2 = 0.999
ADAM_EPS = 1e-08
ADAM_WD = 0.01
ADAM_STEP = 10

VMEM_LIMIT_V7X = 56 * 1024 * 1024
LOG2E = 1.0 / math.log(2.0)
Q_SCALE = LOG2E / math.sqrt(QK_DIM)
ATTN_UNROLL = 4
ATTN_BWD_UNROLL = 4

WEIGHTS = ['meta_tokens', 'pre_norm_w', 'post_norm_w', 'w_in', 'q_norm_w', 'w_q_up', 'kv_norm_w', 'w_kv_up',
           'attn_out_norm_w', 'ssm_a_re', 'ssm_a_im', 'ssm_log_dt', 'ssm_b_re', 'ssm_b_im', 'ssm_c_re', 'ssm_c_im',
           'ssm_d', 'w_glu', 'b_glu', 'ssm_out_norm_w', 'w_out']
SHARDED = ['w_in', 'w_q_up', 'w_kv_up', 'w_glu', 'w_out', 'meta_tokens']

def _cols_in(w):
    return jnp.concatenate([w[:, 0:384], w[:, 416:D_IN], w[:, 384:416]], axis=1)


def _cols_in_inv(w):
    return jnp.concatenate([w[:, 0:384], w[:, D_IN - QK_ROPE:D_IN], w[:, 384:D_IN - QK_ROPE]], axis=1)


def _cols_q(w):
    t = w.reshape(w.shape[0], HEADS, QK_DIM)
    return jnp.concatenate([t[:, :, 0:64].reshape(-1, 512), t[:, :, 64:80].reshape(-1, 128),
                            t[:, :, 80:96].reshape(-1, 128)], axis=1)


def _cols_q_inv(w):
    r = w.shape[0]
    return jnp.concatenate([w[:, 0:512].reshape(r, HEADS, 64), w[:, 512:640].reshape(r, HEADS, 16),
                            w[:, 640:768].reshape(r, HEADS, 16)], axis=2).reshape(r, HEADS * QK_DIM)


def _cols_kv(w):
    t = w.reshape(w.shape[0], HEADS, 128)
    return jnp.concatenate([t[:, :, 0:64].reshape(-1, 512), t[:, :, 64:128].reshape(-1, 512)], axis=1)


def _cols_kv_inv(w):
    r = w.shape[0]
    return jnp.concatenate([w[:, 0:512].reshape(r, HEADS, 64), w[:, 512:1024].reshape(r, HEADS, 64)],
                           axis=2).reshape(r, HEADS * 128)


def _pick(n, cands):
    for c in cands:
        if n % c == 0:
            return c
    raise ValueError(f"no tile for {n}")


def _cparams(*sem):
    return pltpu.CompilerParams(dimension_semantics=sem, vmem_limit_bytes=VMEM_LIMIT_V7X)


def _mm(a, b):
    return jnp.dot(a.astype(BF16), b.astype(BF16), preferred_element_type=F32)


def _mm_nt(a, b):
    return lax.dot_general(a.astype(BF16), b.astype(BF16), (((1,), (1,)), ((), ())), preferred_element_type=F32)


def _mm_tn(a, b):
    return lax.dot_general(a.astype(BF16), b.astype(BF16), (((0,), (0,)), ((), ())), preferred_element_type=F32)


def _mm_exact(a, b):
    return jnp.dot(a, b, precision=lax.Precision.HIGHEST, preferred_element_type=F32)


def _rms(x):
    return lax.rsqrt(jnp.mean(x * x, axis=-1, keepdims=True) + EPS)


def _rms_bwd(dy, x, r, w):
    xh = x * r
    g = dy * w
    dx = r * (g - xh * jnp.mean(g * xh, axis=-1, keepdims=True))
    dw = jnp.sum(dy * xh, axis=0, keepdims=True)
    return dx, dw


def _sigmoid(z):
    return 1.0 / (1.0 + jnp.exp(-z))


def _silu_and_grad(z):
    s = _sigmoid(z)
    return z * s, s * (1.0 + z * (1.0 - s))


_GELU_C = math.sqrt(2.0 / math.pi)


def _gelu_and_grad(x):
    x2 = x * x
    t = jnp.tanh(_GELU_C * (x + 0.044715 * x * x2))
    val = 0.5 * x * (1.0 + t)
    grad = 0.5 * (1.0 + t) + 0.5 * x * (1.0 - t * t) * _GELU_C * (1.0 + 3.0 * 0.044715 * x2)
    return val, grad


def _acc(ref, val, first):
    @pl.when(first)
    def _():
        ref[...] = val

    @pl.when(jnp.logical_not(first))
    def _():
        ref[...] += val


def _rows_call(name, body, tr, row_ins, full_ins, row_outs, acc_outs):
    lp = row_ins[0].shape[0]
    in_specs = [pl.BlockSpec((tr, a.shape[1]), lambda i: (i, 0)) for a in row_ins]
    in_specs += [pl.BlockSpec(a.shape, lambda i, n=a.ndim: (0,) * n) for a in full_ins]
    out_specs = [pl.BlockSpec((tr, c), lambda i: (i, 0)) for c, _ in row_outs]
    out_specs += [pl.BlockSpec(s, lambda i, n=len(s): (0,) * n) for s, _ in acc_outs]
    out_shape = [jax.ShapeDtypeStruct((lp, c), dt) for c, dt in row_outs]
    out_shape += [jax.ShapeDtypeStruct(s, dt) for s, dt in acc_outs]
    return pl.pallas_call(
        body, name=name, grid=(lp // tr,), in_specs=in_specs, out_specs=out_specs, out_shape=out_shape,
        compiler_params=_cparams("arbitrary"))(*row_ins, *full_ins)


def _inproj(h, pre_w, w_in_b, tr):
    def body(h_ref, pw_ref, w_ref, ql, kvl, ag, su, sg, kr):
        x = h_ref[...]
        xn = x * _rms(x) * pw_ref[...]
        pr = _mm(xn, w_ref[...])
        ql[...] = pr[:, 0:256]
        kvl[...] = pr[:, 256:384]
        ag[...] = pr[:, 384:896]
        su[...] = pr[:, 896:1408]
        sg[...] = pr[:, 1408:1920]
        kr[...] = pr[:, 1920:1952]

    return _rows_call("inproj", body, tr, [h], [pre_w, w_in_b],
                      [(256, F32), (128, F32), (512, F32), (512, F32), (512, F32), (32, F32)], [])


def _qkv_up(ql, kvl, kr, cos8, sin8, c32, s32, qw, kvw, wq_b, wkv_b, p32, tr):
    def body(ql_ref, kvl_ref, kr_ref, cos_ref, sin_ref, c32_ref, s32_ref, qw_ref, kvw_ref, wq_ref, wkv_ref, p_ref,
             qn_o, qr1_o, qr2_o, kn_o, v_o, kr_o):
        x = ql_ref[...]
        q = _mm(x * _rms(x) * qw_ref[...], wq_ref[...]) * Q_SCALE
        r1, r2 = q[:, 512:640], q[:, 640:768]
        cs, sn = cos_ref[...], sin_ref[...]
        qn_o[...] = q[:, 0:512].astype(BF16)
        qr1_o[...] = (r1 * cs - r2 * sn).astype(BF16)
        qr2_o[...] = (r2 * cs + r1 * sn).astype(BF16)
        x = kvl_ref[...]
        kv = _mm(x * _rms(x) * kvw_ref[...], wkv_ref[...])
        kn_o[...] = kv[:, 0:512].astype(BF16)
        v_o[...] = kv[:, 512:1024].astype(BF16)
        x = kr_ref[...]
        kr_o[...] = (x * c32_ref[...] + _mm_exact(x, p_ref[...]) * s32_ref[...]).astype(BF16)

    return _rows_call("qkv_up", body, tr, [ql, kvl, kr, cos8, sin8, c32, s32], [qw, kvw, wq_b, wkv_b, p32],
                      [(512, BF16), (128, BF16), (128, BF16), (512, BF16), (512, BF16), (32, BF16)], [])


def _row_position(row, lseg):
    return (row & (N_SEG - 1)) * lseg + (row >> 3)


def _first_padded_tile(n_valid, lp, tile):
    lseg = lp // N_SEG
    t0 = n_valid - (N_SEG - 1) * lseg
    return (t0 * N_SEG + N_SEG - 1) // tile if n_valid < lp else lp // tile


def _attn_fwd(qt, k, vxt, n_valid):
    _, nq, _, tq = qt.shape
    _, nk, _, tk = vxt.shape
    lp = k.shape[1]
    lseg = lp // N_SEG
    n_plain = max(0, min(nk, _first_padded_tile(n_valid, lp, tk)))

    def body(q_ref, k_ref, v_ref, o_ref, lse_ref, m_s, acc_s):
        m_s[...] = jnp.full(m_s.shape, -1e30, F32)
        acc_s[...] = jnp.zeros(acc_s.shape, F32)
        qq = q_ref[0, 0]

        def chunk(c, padded):
            r0 = pl.multiple_of(c * tk, tk)
            st = _mm(k_ref[0, pl.ds(r0, tk), :], qq)
            if padded:
                row = r0 + lax.broadcasted_iota(jnp.int32, (tk, tq), 0)
                st = jnp.where(_row_position(row, lseg) < n_valid, st, -1e30)
            m_old = m_s[...]
            m_new = jnp.maximum(m_old, jnp.max(st, axis=0, keepdims=True))
            pt = jnp.exp2(st - m_new)
            acc_s[...] = jnp.exp2(m_old - m_new) * acc_s[...] + _mm(v_ref[0, c], pt)
            m_s[...] = m_new

        def plain(c, carry):
            chunk(c, False)
            return carry

        n_loop = n_plain - n_plain % ATTN_UNROLL
        if n_loop:
            lax.fori_loop(0, n_loop, plain, 0, unroll=ATTN_UNROLL)
        for c in range(n_loop, nk):
            chunk(c, c >= n_plain)
        acc = acc_s[...]
        l = acc[V_HEAD:V_HEAD + 1, :]
        o_ref[0] = acc[:V_HEAD, :] / l
        lse_ref[0, 0] = m_s[...] + jnp.log2(l)

    return pl.pallas_call(
        body, name="attn_fwd", grid=(HEADS, nq),
        in_specs=[pl.BlockSpec((1, 1, QK_DIM, tq), lambda h, i: (h, i, 0, 0)),
                  pl.BlockSpec((1, lp, QK_DIM), lambda h, i: (h, 0, 0)),
                  pl.BlockSpec((1, nk, LANES, tk), lambda h, i: (h, 0, 0, 0))],
        out_specs=[pl.BlockSpec((1, V_HEAD, tq), lambda h, i: (h, 0, i)),
                   pl.BlockSpec((1, 1, 1, tq), lambda h, i: (h, i, 0, 0))],
        out_shape=[jax.ShapeDtypeStruct((HEADS, V_HEAD, lp), F32), jax.ShapeDtypeStruct((HEADS, nq, 1, tq), F32)],
        scratch_shapes=[pltpu.VMEM((1, tq), F32), pltpu.VMEM((LANES, tq), F32)],
        compiler_params=_cparams("arbitrary", "arbitrary"))(qt, k, vxt)


def _scan_tiles(lp, longest):
    lseg = lp // N_SEG
    tt = _pick(lseg, [longest, 48, 32, 16, 8, 4, 2, 1])
    return lseg, tt, lseg // tt


def _cmul(ar, ai, br, bi):
    return ar * br - ai * bi, ar * bi + ai * br


SCAN_STEPS_PER_ITER = 4
N_COL_BLK = N_STATES // COL_BLK
CH_BLK = D_SSM // N_COL_BLK


def _scan_steps(tt, forward, bre_ref, bim_ref, ar, ai, carry, visit):
    def step(s, c):
        r0 = pl.multiple_of((s if forward else tt - 1 - s) * N_SEG, N_SEG)
        pr, pi = _cmul(ar, ai, c[0], c[1])
        xr = pr + bre_ref[pl.ds(r0, N_SEG), :]
        xi = pi + bim_ref[pl.ds(r0, N_SEG), :]
        return (xr, xi) + tuple(visit(r0, (xr, xi), (c[0], c[1]), c[2:]))

    per = SCAN_STEPS_PER_ITER if tt % SCAN_STEPS_PER_ITER == 0 else 1

    def steps(it, c):
        for u in range(per):
            c = step(it * per + u, c)
        return c

    return lax.fori_loop(0, tt // per, steps, carry)


def _segment_starts(lseg, forward, ar, ai, ere_ref, eim_ref, s_re, s_im):
    a1r, a1i = ar[0:1, :], ai[0:1, :]
    pr, pi = jnp.ones_like(a1r), jnp.zeros_like(a1i)
    br, bi = a1r, a1i
    n = lseg
    while n:
        if n & 1:
            pr, pi = _cmul(pr, pi, br, bi)
        n >>= 1
        if n:
            br, bi = _cmul(br, bi, br, bi)
    cr, ci = jnp.zeros_like(a1r), jnp.zeros_like(a1i)
    for j in (range(N_SEG) if forward else range(N_SEG - 1, -1, -1)):
        s_re[j:j + 1, :] = cr
        s_im[j:j + 1, :] = ci
        nr, ni = _cmul(pr, pi, cr, ci)
        cr = nr + ere_ref[j:j + 1, :]
        ci = ni + eim_ref[j:j + 1, :]


def _scan_specs(lp, forward, longest=208):
    lseg, tt, nt = _scan_tiles(lp, longest)

    def tile(t):
        return t if forward else nt - 1 - t

    rows = lambda w: pl.BlockSpec((tt * N_SEG, w), lambda cb, t: (tile(t), cb))
    proj = pl.BlockSpec((1, CH_BLK, COL_BLK), lambda cb, t: (cb, 0, 0))
    slab = pl.BlockSpec((N_SEG, COL_BLK), lambda cb, t: (0, cb))
    return lseg, tt, nt, rows, proj, slab


def _scan_ends(name, urows, wre4, wim4, ar8, ai8, forward):
    lp = urows.shape[0]
    lseg, tt, nt, rows, proj, slab = _scan_specs(lp, forward, longest=520)

    def body(u_ref, wre_ref, wim_ref, ar_ref, ai_ref, ere_o, eim_o, bre_s, bim_s, cr_s, ci_s):
        t = pl.program_id(1)

        @pl.when(t == 0)
        def _():
            cr_s[...] = jnp.zeros(cr_s.shape, F32)
            ci_s[...] = jnp.zeros(ci_s.shape, F32)

        u = u_ref[...]
        bre_s[...] = _mm(u, wre_ref[0])
        bim_s[...] = _mm(u, wim_ref[0])
        cr, ci = _scan_steps(tt, forward, bre_s, bim_s, ar_ref[...], ai_ref[...], (cr_s[...], ci_s[...]),
                             lambda r0, x, x_prev, extra: ())
        cr_s[...] = cr
        ci_s[...] = ci

        @pl.when(t == nt - 1)
        def _():
            ere_o[...] = cr
            eim_o[...] = ci

    return pl.pallas_call(
        body, name=name, grid=(N_COL_BLK, nt), in_specs=[rows(CH_BLK), proj, proj, slab, slab],
        out_specs=[slab, slab], out_shape=[jax.ShapeDtypeStruct((N_SEG, N_STATES), F32)] * 2,
        scratch_shapes=[pltpu.VMEM((tt * N_SEG, COL_BLK), F32)] * 2 + [pltpu.VMEM((N_SEG, COL_BLK), F32)] * 2,
        compiler_params=_cparams("arbitrary", "arbitrary"))(urows, wre4, wim4, ar8, ai8)


def _scan_fwd(name, urows, wre4, wim4, ar8, ai8, ere, eim, cre4, cim4, forward):
    lp = urows.shape[0]
    lseg, tt, nt, rows, proj, slab = _scan_specs(lp, forward)

    def body(u_ref, wre_ref, wim_ref, ar_ref, ai_ref, ere_ref, eim_ref, cre_ref, cim_ref,
             xre_o, xim_o, y_o, bre_s, bim_s, cr_s, ci_s):
        ar, ai = ar_ref[...], ai_ref[...]

        @pl.when(pl.program_id(1) == 0)
        def _():
            _segment_starts(lseg, forward, ar, ai, ere_ref, eim_ref, cr_s, ci_s)

        u = u_ref[...]
        bre_s[...] = _mm(u, wre_ref[0])
        bim_s[...] = _mm(u, wim_ref[0])

        def visit(r0, x, x_prev, extra):
            xre_o[pl.ds(r0, N_SEG), :] = x[0]
            xim_o[pl.ds(r0, N_SEG), :] = x[1]
            return ()

        cr, ci = _scan_steps(tt, forward, bre_s, bim_s, ar, ai, (cr_s[...], ci_s[...]), visit)
        cr_s[...] = cr
        ci_s[...] = ci
        y_o[...] = _mm_nt(xre_o[...], cre_ref[0]) + _mm_nt(xim_o[...], cim_ref[0])

    return pl.pallas_call(
        body, name=name, grid=(N_COL_BLK, nt),
        in_specs=[rows(CH_BLK), proj, proj, slab, slab, slab, slab, proj, proj],
        out_specs=[rows(COL_BLK), rows(COL_BLK), rows(CH_BLK)],
        out_shape=[jax.ShapeDtypeStruct((lp, N_STATES), F32)] * 2 + [jax.ShapeDtypeStruct((lp, D_SSM), F32)],
        scratch_shapes=[pltpu.VMEM((tt * N_SEG, COL_BLK), F32)] * 2 + [pltpu.VMEM((N_SEG, COL_BLK), F32)] * 2,
        compiler_params=_cparams("arbitrary", "arbitrary"))(urows, wre4, wim4, ar8, ai8, ere, eim, cre4, cim4)


def _scan_bwd(name, dyrows, cre4, cim4, ar8, ai8, ere, eim, urows, wre4, wim4, xre, xim, forward):
    lp = urows.shape[0]
    lseg, tt, nt, rows, proj, slab = _scan_specs(lp, forward)

    def body(dy_ref, cre_ref, cim_ref, ar_ref, ai_ref, ere_ref, eim_ref, u_ref, wre_ref, wim_ref, xre_ref, xim_ref,
             du_o, dwre_o, dwim_o, dcre_o, dcim_o, dare_o, daim_o, bre_s, bim_s, gre_s, gim_s, cr_s, ci_s):
        t = pl.program_id(1)
        ar, ai = ar_ref[...], ai_ref[...]

        @pl.when(t == 0)
        def _():
            _segment_starts(lseg, forward, ar, ai, ere_ref, eim_ref, cr_s, ci_s)
            dare_o[...] = jnp.zeros(dare_o.shape, F32)
            daim_o[...] = jnp.zeros(daim_o.shape, F32)

        dy = dy_ref[...]
        bre_s[...] = _mm(dy, cre_ref[0])
        bim_s[...] = _mm(dy, cim_ref[0])

        def visit(r0, g, g_prev, sums):
            gre_s[pl.ds(r0, N_SEG), :] = g[0]
            gim_s[pl.ds(r0, N_SEG), :] = g[1]
            fr = xre_ref[pl.ds(r0, N_SEG), :]
            fi = xim_ref[pl.ds(r0, N_SEG), :]
            pr, pi = g_prev
            return sums[0] + fr * pr + fi * pi, sums[1] + fr * pi - fi * pr

        out = _scan_steps(tt, forward, bre_s, bim_s, ar, ai, (cr_s[...], ci_s[...], dare_o[...], daim_o[...]), visit)
        cr_s[...] = out[0]
        ci_s[...] = out[1]
        dare_o[...] = out[2]
        daim_o[...] = out[3]
        gre, gim = gre_s[...], gim_s[...]
        du_o[...] = _mm_nt(gre, wre_ref[0]) + _mm_nt(gim, wim_ref[0])
        u = u_ref[...]
        first = t == 0
        _acc(dwre_o, _mm_tn(u, gre)[None], first)
        _acc(dwim_o, _mm_tn(u, gim)[None], first)
        _acc(dcre_o, _mm_tn(dy, xre_ref[...])[None], first)
        _acc(dcim_o, _mm_tn(dy, xim_ref[...])[None], first)

    big = pltpu.VMEM((tt * N_SEG, COL_BLK), F32)
    small = pltpu.VMEM((N_SEG, COL_BLK), F32)
    return pl.pallas_call(
        body, name=name, grid=(N_COL_BLK, nt),
        in_specs=[rows(CH_BLK), proj, proj, slab, slab, slab, slab, rows(CH_BLK), proj, proj,
                  rows(COL_BLK), rows(COL_BLK)],
        out_specs=[rows(CH_BLK), proj, proj, proj, proj, slab, slab],
        out_shape=[jax.ShapeDtypeStruct((lp, D_SSM), F32)]
        + [jax.ShapeDtypeStruct((N_COL_BLK, CH_BLK, COL_BLK), F32)] * 4
        + [jax.ShapeDtypeStruct((N_SEG, N_STATES), F32)] * 2,
        scratch_shapes=[big, big, big, big, small, small],
        compiler_params=_cparams("arbitrary", "arbitrary"))(
            dyrows, cre4, cim4, ar8, ai8, ere, eim, urows, wre4, wim4, xre, xim)


def _ssm_post(yf, yb, u, sg, wglu_b, bglu, sw, dvec, tr):
    def body(yf_ref, yb_ref, u_ref, g_ref, w_ref, b_ref, sw_ref, d_ref, ypre_o, glu_o, ysn_o):
        ypre = yf_ref[...] + yb_ref[...] + d_ref[...] * u_ref[...]
        ypre_o[...] = ypre
        glu = _mm(_gelu_and_grad(ypre)[0], w_ref[...]) + b_ref[...]
        glu_o[...] = glu
        t = glu[:, :D_SSM] * _sigmoid(glu[:, D_SSM:]) * _silu_and_grad(g_ref[...])[0]
        ysn_o[...] = t * _rms(t) * sw_ref[...]

    return _rows_call("ssm_post", body, tr, [yf, yb, u, sg], [wglu_b, bglu, sw, dvec],
                      [(512, F32), (1024, F32), (512, F32)], [])


def _attn_gate_norm(o, gate, w):
    sl, dsl = _silu_and_grad(gate)
    t = o * sl
    r = _rms(t)
    return t * r * w, t, r, sl, dsl


def _out_loss(o_flat, ag, ysn, h, tgt, wo_b, post_w, aw, n_valid, tr):
    lseg = h.shape[0] // N_SEG

    def body(o_ref, g_ref, ys_ref, h_ref, t_ref, w_ref, pw_ref, aw_ref, dy_o, dout_o, loss_o, dpw_o):
        i = pl.program_id(0)
        ya = _attn_gate_norm(o_ref[...], g_ref[...], aw_ref[...])[0]
        y = _mm(ya, w_ref[0:D_ATTN, :]) + _mm(ys_ref[...], w_ref[D_ATTN:, :])
        r = _rms(y)
        pw = pw_ref[...]
        out = h_ref[...] + y * r * pw
        pos = _row_position(i * tr + lax.broadcasted_iota(jnp.int32, (tr, 1), 0), lseg)
        valid = jnp.logical_and(pos >= N_META, pos < n_valid)
        diff = jnp.where(valid, out - t_ref[...], 0.0)
        dout = diff * (1.0 / D_MODEL)
        dy, dpw = _rms_bwd(dout, y, r, pw)
        dy_o[...] = dy
        dout_o[...] = dout
        _acc(loss_o, 0.5 * jnp.sum(jnp.sum(diff * diff, axis=1, keepdims=True), axis=0, keepdims=True)
             * (1.0 / D_MODEL), i == 0)
        _acc(dpw_o, dpw, i == 0)

    return _rows_call("out_loss", body, tr, [o_flat, ag, ysn, h, tgt], [wo_b, post_w, aw],
                      [(1024, F32), (1024, F32)], [((1, 1), F32), ((1, D_MODEL), F32)])


def _out_bwd(dy, ysn, o_flat, ag, wo_b, aw, head_sum, tr):
    def body(dy_ref, ys_ref, o_ref, g_ref, w_ref, aw_ref, hs_ref, do_o, dag_o, dysn_o, dl_o, dwo_o, daw_o):
        i = pl.program_id(0)
        dy = dy_ref[...]
        dcat = _mm_nt(dy, w_ref[...])
        o = o_ref[...]
        aw = aw_ref[...]
        ya, t, r, sl, dsl = _attn_gate_norm(o, g_ref[...], aw)
        cat = jnp.concatenate([ya, ys_ref[...]], axis=1)
        _acc(dwo_o, _mm_tn(cat, dy), i == 0)
        dysn_o[...] = dcat[:, D_ATTN:]
        dt, daw = _rms_bwd(dcat[:, :D_ATTN], t, r, aw)
        _acc(daw_o, daw, i == 0)
        do = dt * sl
        do_o[...] = do
        dag_o[...] = dt * o * dsl
        dl_o[...] = _mm_exact(do * o, hs_ref[...])

    return _rows_call("out_bwd", body, tr, [dy, ysn, o_flat, ag], [wo_b, aw, head_sum],
                      [(512, F32), (512, F32), (512, F32), (HEADS, F32)],
                      [((D_MODEL, D_MODEL), F32), ((1, D_ATTN), F32)])


def _ssm_post_bwd(dysn, glu, sg, ypre, u, wglu_b, sw, dvec, tr):
    def body(d_ref, glu_ref, sg_ref, y_ref, u_ref, w_ref, sw_ref, dv_ref,
             dyp_o, dsg_o, dwg_o, dbg_o, dsw_o, dd_o):
        i = pl.program_id(0)
        glu = glu_ref[...]
        a, b = glu[:, :D_SSM], glu[:, D_SSM:]
        sb = _sigmoid(b)
        ys = a * sb
        sl, dsl = _silu_and_grad(sg_ref[...])
        t = ys * sl
        dt, dsw = _rms_bwd(d_ref[...], t, _rms(t), sw_ref[...])
        _acc(dsw_o, dsw, i == 0)
        dsg_o[...] = dt * ys * dsl
        dys = dt * sl
        dglu = jnp.concatenate([dys * sb, dys * a * sb * (1.0 - sb)], axis=1)
        _acc(dbg_o, jnp.sum(dglu, axis=0, keepdims=True), i == 0)
        gel, dgel = _gelu_and_grad(y_ref[...])
        _acc(dwg_o, _mm_tn(gel, dglu), i == 0)
        dyp = _mm_nt(dglu, w_ref[...]) * dgel
        dyp_o[...] = dyp
        _acc(dd_o, jnp.sum(dyp * u_ref[...], axis=0, keepdims=True), i == 0)

    return _rows_call("ssm_post_bwd", body, tr, [dysn, glu, sg, ypre, u], [wglu_b, sw, dvec],
                      [(512, F32), (512, F32)],
                      [((D_SSM, 2 * D_SSM), F32), ((1, 2 * D_SSM), F32), ((1, D_SSM), F32), ((1, D_SSM), F32)])


def _attn_bwd(qt, k, kt, v, dot, lse_t, delta_t, tk):
    _, nq, _, tq = qt.shape
    lp = k.shape[1]
    nk = lp // tk
    assert lse_t.shape == (HEADS, nq, 1, tq) and delta_t.shape == (HEADS, nq, 1, tq)

    def body(q_ref, k_ref, kt_ref, v_ref, do_ref, lse_ref, dl_ref, dq_o, dk_o, dv_o, dk_s, dv_s):
        @pl.when(pl.program_id(1) == 0)
        def _():
            dq_o[...] = jnp.zeros(dq_o.shape, F32)

        dk_s[...] = jnp.zeros(dk_s.shape, F32)
        dv_s[...] = jnp.zeros(dv_s.shape, F32)
        kk = k_ref[0]
        kkt = kt_ref[0]
        vv = v_ref[0]

        def chunk(c, carry):
            qq = q_ref[0, c]
            dd = do_ref[0, c]
            pt = jnp.exp2(_mm(kk, qq) - lse_ref[0, c])
            dv_s[...] += _mm_nt(dd, pt)
            dst = (pt * (_mm(vv, dd) - dl_ref[0, c])).astype(BF16)
            dk_s[...] += _mm_nt(qq, dst)
            dq_o[0, c] += _mm(kkt, dst)
            return carry

        n_loop = nq - nq % ATTN_BWD_UNROLL
        if n_loop:
            lax.fori_loop(0, n_loop, chunk, 0, unroll=ATTN_BWD_UNROLL)
        for c in range(n_loop, nq):
            chunk(c, 0)
        dk_o[0] = dk_s[...]
        dv_o[0] = dv_s[...]

    head = lambda w: pl.BlockSpec((1, nq, w, tq), lambda h, j: (h, 0, 0, 0))
    rows = lambda w: pl.BlockSpec((1, tk, w), lambda h, j: (h, j, 0))
    cols = lambda w: pl.BlockSpec((1, w, tk), lambda h, j: (h, 0, j))
    return pl.pallas_call(
        body, name="attn_bwd", grid=(HEADS, nk),
        in_specs=[head(QK_DIM), rows(QK_DIM), cols(QK_DIM), rows(V_HEAD), head(V_HEAD), head(1), head(1)],
        out_specs=[head(QK_DIM), cols(QK_DIM), cols(V_HEAD)],
        out_shape=[jax.ShapeDtypeStruct((HEADS, nq, QK_DIM, tq), F32), jax.ShapeDtypeStruct((HEADS, QK_DIM, lp), F32),
                   jax.ShapeDtypeStruct((HEADS, V_HEAD, lp), F32)],
        scratch_shapes=[pltpu.VMEM((QK_DIM, tk), F32), pltpu.VMEM((V_HEAD, tk), F32)],
        compiler_params=_cparams("arbitrary", "arbitrary"))(qt, k, kt, v, dot, lse_t, delta_t)


def _qkv_up_bwd(dqn, dr1, dr2, dkn, dv, dkr8, ql, kvl, cos8, sin8, c32, s32, qw, kvw, wq_b, wkv_b, p32, sum8, tr):
    def body(dqn_ref, dr1_ref, dr2_ref, dkn_ref, dv_ref, dkr_ref, ql_ref, kvl_ref, cos_ref, sin_ref, c32_ref,
             s32_ref, qw_ref, kvw_ref, wq_ref, wkv_ref, p_ref, s8_ref,
             dql_o, dkvl_o, dkrr_o, dwq_o, dwkv_o, dqw_o, dkvw_o):
        i = pl.program_id(0)
        cs, sn = cos_ref[...], sin_ref[...]
        d1, d2 = dr1_ref[...], dr2_ref[...]
        dq = jnp.concatenate([dqn_ref[...], d1 * cs + d2 * sn, d2 * cs - d1 * sn], axis=1) * (Q_SCALE / LOG2E)
        x = ql_ref[...]
        r = _rms(x)
        qw = qw_ref[...]
        _acc(dwq_o, _mm_tn(x * r * qw, dq), i == 0)
        dx, dw = _rms_bwd(_mm_nt(dq, wq_ref[...]), x, r, qw)
        dql_o[...] = dx
        _acc(dqw_o, dw, i == 0)
        dkv = jnp.concatenate([dkn_ref[...] * (1.0 / LOG2E), dv_ref[...]], axis=1)
        x = kvl_ref[...]
        r = _rms(x)
        kvw = kvw_ref[...]
        _acc(dwkv_o, _mm_tn(x * r * kvw, dkv), i == 0)
        dx, dw = _rms_bwd(_mm_nt(dkv, wkv_ref[...]), x, r, kvw)
        dkvl_o[...] = dx
        _acc(dkvw_o, dw, i == 0)
        dkr = _mm_exact(dkr_ref[...], s8_ref[...]) * (1.0 / LOG2E)
        dkrr_o[...] = dkr * c32_ref[...] + _mm_exact(dkr * s32_ref[...], p_ref[...])

    return _rows_call("qkv_up_bwd", body, tr, [dqn, dr1, dr2, dkn, dv, dkr8, ql, kvl, cos8, sin8, c32, s32],
                      [qw, kvw, wq_b, wkv_b, p32, sum8], [(256, F32), (128, F32), (32, F32)],
                      [((Q_LORA, 768), F32), ((KV_LORA, 1024), F32), ((1, Q_LORA), F32), ((1, KV_LORA), F32)])


def _inproj_bwd(dql, dkvl, dag, du_f, du_b, dypre, dsg, dkr, h, dout, pre_w, w_in_b, dvec, tr):
    def body(dql_ref, dkvl_ref, dag_ref, duf_ref, dub_ref, dyp_ref, dsg_ref, dkr_ref, h_ref, dout_ref,
             pw_ref, w_ref, dv_ref, dh_o, dwin_o, dpw_o):
        i = pl.program_id(0)
        du = duf_ref[...] + dub_ref[...] + dv_ref[...] * dyp_ref[...]
        dproj = jnp.concatenate([dql_ref[...], dkvl_ref[...], dag_ref[...], du, dsg_ref[...],
                                 dkr_ref[...], jnp.zeros((tr, D_IN_PAD - D_IN), F32)], axis=1)
        x = h_ref[...]
        r = _rms(x)
        pw = pw_ref[...]
        _acc(dwin_o, _mm_tn(x * r * pw, dproj), i == 0)
        dx, dw = _rms_bwd(_mm_nt(dproj, w_ref[...]), x, r, pw)
        _acc(dpw_o, dw, i == 0)
        dh_o[...] = dout_ref[...] + dx

    return _rows_call("inproj_bwd", body, tr, [dql, dkvl, dag, du_f, du_b, dypre, dsg, dkr, h, dout],
                      [pre_w, w_in_b, dvec], [(1024, F32)], [((D_MODEL, D_IN_PAD), F32), ((1, D_MODEL), F32)])


def _disc_terms(a_re, a_im, ldt):
    dt = jnp.exp(ldt)
    mag = jnp.exp(a_re * dt)
    th = a_im * dt
    cs, sn = jnp.cos(th), jnp.sin(th)
    abar_re, abar_im = mag * cs, mag * sn
    num_re, num_im = abar_re - 1.0, abar_im
    den = a_re * a_re + a_im * a_im
    coef_re = (num_re * a_re + num_im * a_im) / den
    coef_im = (num_im * a_re - num_re * a_im) / den
    return dt, mag, cs, sn, abar_re, abar_im, num_re, num_im, den, coef_re, coef_im


def _ssm_disc(a_re, a_im, ldt, bt_re, bt_im):
    def body(ar_ref, ai_ref, l_ref, br_ref, bi_ref, abr_o, abi_o, bbr_o, bbi_o):
        t = _disc_terms(ar_ref[...], ai_ref[...], l_ref[...])
        abr_o[...] = t[4]
        abi_o[...] = t[5]
        cr, ci = t[9], t[10]
        br, bi = br_ref[...], bi_ref[...]
        bbr_o[...] = cr * br - ci * bi
        bbi_o[...] = cr * bi + ci * br

    ng = a_re.shape[0]
    return pl.pallas_call(
        body, name="ssm_disc",
        out_shape=[jax.ShapeDtypeStruct((ng, 1, SSM_STATE), F32)] * 2
        + [jax.ShapeDtypeStruct((ng, SSM_GROUP, SSM_STATE), F32)] * 2)(a_re, a_im, ldt, bt_re, bt_im)


def _ssm_disc_bwd(a_re, a_im, ldt, bt_re, bt_im, da8_re, da8_im, dbb_re, dbb_im):
    def body(ar_ref, ai_ref, l_ref, br_ref, bi_ref, dar_ref, dai_ref, dbr_ref, dbi_ref,
             gar_o, gai_o, gl_o, gbr_o, gbi_o):
        a_re, a_im = ar_ref[...], ai_ref[...]
        dt, mag, cs, sn, abar_re, abar_im, num_re, num_im, den, cr, ci = _disc_terms(a_re, a_im, l_ref[...])
        br, bi = br_ref[...], bi_ref[...]
        dbr, dbi = dbr_ref[...], dbi_ref[...]
        gbr_o[...] = cr * dbr + ci * dbi
        gbi_o[...] = cr * dbi - ci * dbr
        dcr = jnp.sum(br * dbr + bi * dbi, axis=1, keepdims=True)
        dci = jnp.sum(br * dbi - bi * dbr, axis=1, keepdims=True)
        dnum_re = (dcr * a_re - dci * a_im) / den
        dnum_im = (dcr * a_im + dci * a_re) / den
        dden = -(dcr * cr + dci * ci) / den
        g_are = (dcr * num_re + dci * num_im) / den + dden * 2.0 * a_re
        g_aim = (dcr * num_im - dci * num_re) / den + dden * 2.0 * a_im
        d_abr = jnp.sum(dar_ref[...], axis=1, keepdims=True) + dnum_re
        d_abi = jnp.sum(dai_ref[...], axis=1, keepdims=True) + dnum_im
        dmag = d_abr * cs + d_abi * sn
        dth = d_abi * abar_re - d_abr * abar_im
        g_are = g_are + dmag * mag * dt
        g_aim = g_aim + dth * dt
        ddt = jnp.sum(dmag * mag * a_re + dth * a_im, axis=2, keepdims=True)
        gar_o[...] = g_are
        gai_o[...] = g_aim
        gl_o[...] = ddt * dt

    ng = a_re.shape[0]
    return pl.pallas_call(
        body, name="ssm_disc_bwd",
        out_shape=[jax.ShapeDtypeStruct((ng, 1, SSM_STATE), F32)] * 2 + [jax.ShapeDtypeStruct((ng, 1, 1), F32)]
        + [jax.ShapeDtypeStruct((ng, SSM_GROUP, SSM_STATE), F32)] * 2)(
            a_re, a_im, ldt, bt_re, bt_im, da8_re, da8_im, dbb_re, dbb_im)


def _exchange(name, per_peer, shared):
    rp, rs = per_peer.shape[1], shared.shape[0]
    n_direct = N_DEV - 1

    def body(peer_ref, shared_ref, out_ref, send_sems, recv_sems, local_sems):
        x, y, c = lax.axis_index("x"), lax.axis_index("y"), lax.axis_index("c")
        me = 4 * x + 2 * y + c
        sibling = (x, y, 1 - c)
        chips = [(1 - x, y), (x, 1 - y), (1 - x, 1 - y)]

        def index(px, py, pc):
            return 4 * px + 2 * py + pc

        def remote(src, dst, s, to):
            return pltpu.make_async_remote_copy(src_ref=src, dst_ref=dst, send_sem=send_sems.at[s],
                                                recv_sem=recv_sems.at[s], device_id=to,
                                                device_id_type=pl.DeviceIdType.MESH)

        direct = []
        for k in range(1, N_DEV):
            px = 1 - x if (k >> 2) & 1 else x
            py = 1 - y if (k >> 1) & 1 else y
            pc = 1 - c if k & 1 else c
            direct.append(remote(peer_ref.at[index(px, py, pc)], out_ref.at[me, pl.ds(0, rp), :], k - 1,
                                 (px, py, pc)))

        def block(i):
            return out_ref.at[i, pl.ds(rp, rs), :]

        def relay(k, i, to, src=None):
            return remote(block(i) if src is None else src, block(i), n_direct + k, to)

        mine = [pltpu.make_async_copy(peer_ref.at[me], out_ref.at[me, pl.ds(0, rp), :], local_sems.at[0]),
                pltpu.make_async_copy(shared_ref, block(me), local_sems.at[1])]
        first = [relay(0, me, sibling, src=shared_ref)]
        first += [relay(1 + j, me, (*chip, c), src=shared_ref) for j, chip in enumerate(chips)]
        for cp in mine + direct + first:
            cp.start()
        passed = [relay(4 + j, index(*chip, c), sibling) for j, chip in enumerate(chips)]
        for j, chip in enumerate(chips):
            relay(1 + j, index(*chip, c), (x, y, c)).wait_recv()
            passed[j].start()
        relay(0, index(*sibling), (x, y, c)).wait_recv()
        for j, chip in enumerate(chips):
            relay(4 + j, index(*chip, 1 - c), (x, y, c)).wait_recv()
        for cp in first + passed:
            cp.wait_send()
        for cp in direct + mine:
            cp.wait()

    n_sem = n_direct + N_DEV - 1
    return pl.pallas_call(
        body, name=name, out_shape=jax.ShapeDtypeStruct((N_DEV, rp + rs, LANES), F32),
        in_specs=[pl.BlockSpec(memory_space=pl.ANY)] * 2, out_specs=pl.BlockSpec(memory_space=pl.ANY),
        scratch_shapes=[pltpu.SemaphoreType.DMA((n_sem,)), pltpu.SemaphoreType.DMA((n_sem,)),
                        pltpu.SemaphoreType.DMA((2,))])(per_peer, shared)


def _gather_two_level(name, buf):
    m_per, n = buf.shape

    def body(x_ref, out_ref, send_sems, recv_sems, local_sem):
        x, y, c = lax.axis_index("x"), lax.axis_index("y"), lax.axis_index("c")
        me, sibling = (x, y, c), (x, y, 1 - c)
        chips = [(1 - x, y), (x, 1 - y), (1 - x, 1 - y)]

        def rows(px, py, pc):
            return out_ref.at[pl.ds(pl.multiple_of((4 * px + 2 * py + pc) * m_per, 8), m_per), :]

        def copy(k, block, to, src=None):
            return pltpu.make_async_remote_copy(
                src_ref=rows(*block) if src is None else src, dst_ref=rows(*block), send_sem=send_sems.at[k],
                recv_sem=recv_sems.at[k], device_id=to, device_id_type=pl.DeviceIdType.MESH)

        mine = pltpu.make_async_copy(x_ref, rows(*me), local_sem)
        mine.start()
        first = [copy(0, me, sibling, src=x_ref)]
        first += [copy(1 + j, me, (*chip, c), src=x_ref) for j, chip in enumerate(chips)]
        for cp in first:
            cp.start()
        passed = [copy(4 + j, (*chip, c), sibling) for j, chip in enumerate(chips)]
        for j, chip in enumerate(chips):
            copy(1 + j, (*chip, c), me).wait_recv()
            passed[j].start()
        copy(0, sibling, me).wait_recv()
        for j, chip in enumerate(chips):
            copy(4 + j, (*chip, 1 - c), me).wait_recv()
        for cp in first + passed:
            cp.wait_send()
        mine.wait()

    assert m_per % 8 == 0
    out = pl.pallas_call(
        body, name=name, out_shape=jax.ShapeDtypeStruct((N_DEV * m_per, n), buf.dtype),
        in_specs=[pl.BlockSpec(memory_space=pltpu.VMEM)], out_specs=pl.BlockSpec(memory_space=pltpu.VMEM),
        scratch_shapes=[pltpu.SemaphoreType.DMA((N_DEV - 1,)), pltpu.SemaphoreType.DMA((N_DEV - 1,)),
                        pltpu.SemaphoreType.DMA(())],
        compiler_params=pltpu.CompilerParams(vmem_limit_bytes=VMEM_LIMIT_V7X))(buf)
    return out.reshape(N_DEV, m_per, n)


def _adamw(recv, w, m, v, tr):
    rows = w.shape[0]
    c1 = 1.0 - ADAM_B1 ** ADAM_STEP
    c2 = 1.0 - ADAM_B2 ** ADAM_STEP

    def body(r_ref, w_ref, m_ref, v_ref, g_o, d_o, m_o, v_o):
        g = r_ref[0]
        for k in range(1, N_DEV):
            g = g + r_ref[k]
        mm = ADAM_B1 * m_ref[...] + (1.0 - ADAM_B1) * g
        vv = ADAM_B2 * v_ref[...] + (1.0 - ADAM_B2) * (g * g)
        g_o[...] = g
        m_o[...] = mm
        v_o[...] = vv
        d_o[...] = -ADAM_LR * ((mm / c1) / (jnp.sqrt(vv / c2) + ADAM_EPS) + ADAM_WD * w_ref[...])

    spec = pl.BlockSpec((tr, LANES), lambda i: (i, 0))
    return pl.pallas_call(
        body, name="adamw", grid=(rows // tr,),
        in_specs=[pl.BlockSpec((N_DEV, tr, LANES), lambda i: (0, i, 0)), spec, spec, spec],
        out_specs=[spec] * 4, out_shape=[jax.ShapeDtypeStruct((rows, LANES), F32)] * 4,
        compiler_params=_cparams("arbitrary"))(recv, w, m, v)


def _to_rows(a):
    flat = a.reshape(-1)
    pad = (-flat.shape[0]) % LANES
    if pad:
        flat = jnp.concatenate([flat, jnp.zeros((pad,), flat.dtype)])
    return flat.reshape(-1, LANES)


def _n_rows(shape):
    return -(-int(np.prod(shape)) // LANES)


def _pack(arrays, total_rows):
    rows = [_to_rows(a) for a in arrays]
    used = sum(r.shape[0] for r in rows)
    if total_rows > used:
        rows.append(jnp.zeros((total_rows - used, LANES), F32))
    return jnp.concatenate(rows, axis=0)


def _unpack(buf, shapes):
    lead = buf.shape[:-2]
    out, r0 = [], 0
    for s in shapes:
        n = int(np.prod(s))
        nr = _n_rows(s)
        out.append(buf[..., r0:r0 + nr, :].reshape(lead + (-1,))[..., :n].reshape(lead + tuple(s)))
        r0 += nr
    return out


def _pack_per_device(arrays, total_rows):
    rows = []
    for a in arrays:
        flat = a.reshape(N_DEV, -1)
        pad = (-flat.shape[1]) % LANES
        if pad:
            flat = jnp.concatenate([flat, jnp.zeros((N_DEV, pad), flat.dtype)], axis=1)
        rows.append(flat.reshape(N_DEV, -1, LANES))
    used = sum(r.shape[1] for r in rows)
    if total_rows > used:
        rows.append(jnp.zeros((N_DEV, total_rows - used, LANES), F32))
    return jnp.concatenate(rows, axis=1)


def _shard_views(name, full):
    if name == 'w_out':
        return full.reshape(N_DEV, full.shape[0] // N_DEV, full.shape[1])
    r, ccols = full.shape
    return full.reshape(r, N_DEV, ccols // N_DEV).transpose(1, 0, 2)


def _from_shards(name, stacked):
    if name == 'w_out':
        return stacked.reshape(-1, stacked.shape[-1])
    n, r, cc = stacked.shape
    return stacked.transpose(1, 0, 2).reshape(r, n * cc)


GROUPS_PER_BLK = N_GROUPS // N_COL_BLK


def _block_diag(t):
    eye = jnp.eye(GROUPS_PER_BLK, dtype=t.dtype)
    t4 = t.reshape(N_COL_BLK, GROUPS_PER_BLK, SSM_GROUP, SSM_STATE)
    return (t4[:, :, :, None, :] * eye[None, :, None, :, None]).reshape(N_COL_BLK, CH_BLK, COL_BLK)


def _diag_blocks(mat4):
    eye = jnp.eye(GROUPS_PER_BLK, dtype=mat4.dtype)
    m6 = mat4.reshape(N_COL_BLK, GROUPS_PER_BLK, SSM_GROUP, GROUPS_PER_BLK, SSM_STATE)
    return (m6 * eye[None, :, None, :, None]).sum(axis=3).reshape(N_GROUPS, SSM_GROUP, SSM_STATE)


def _step(x, loss_target, wts, moms, vels):
    seq = x.shape[1]
    n_valid = N_META + seq
    lp = -(-n_valid // 256) * 256
    tr = _pick(lp, [640, 256])
    tr_mid = 256
    tq = _pick(lp, [1280, 256])
    tk = _pick(lp, [640, 256])

    shard_shapes = [wts[n].shape[-2:] for n in SHARDED]
    n_shard_rows = sum(_n_rows(s) for s in shard_shapes)
    gathered = _gather_two_level("gather_weights",
                                 _pack([wts[n].reshape(wts[n].shape[-2:]) for n in SHARDED], n_shard_rows))
    full = {n: _from_shards(n, a) for n, a in zip(SHARDED, _unpack(gathered, shard_shapes))}

    w_in_b = jnp.concatenate([_cols_in(full['w_in']), jnp.zeros((D_MODEL, D_IN_PAD - D_IN), F32)],
                             axis=1).astype(BF16)
    wq_b = _cols_q(full['w_q_up']).astype(BF16)
    wkv_b = _cols_kv(full['w_kv_up']).astype(BF16)
    wglu_b = full['w_glu'].astype(BF16)
    wo_b = full['w_out'].astype(BF16)
    pre_w, post_w = wts['pre_norm_w'], wts['post_norm_w']
    qw, kvw, aw, sw = wts['q_norm_w'], wts['kv_norm_w'], wts['attn_out_norm_w'], wts['ssm_out_norm_w']
    bglu, dvec = wts['b_glu'], wts['ssm_d']

    lseg = lp // N_SEG
    pos = _row_position(jnp.arange(lp, dtype=jnp.int32), lseg)
    inv = ROPE_THETA ** (-jnp.arange(HALF_ROPE, dtype=F32) / HALF_ROPE)
    ang = pos.astype(F32)[:, None] * inv[None, :]
    cos, sin = jnp.cos(ang), jnp.sin(ang)
    cos8, sin8 = jnp.tile(cos, (1, HEADS)), jnp.tile(sin, (1, HEADS))
    c32 = jnp.concatenate([cos, cos], axis=1)
    s32 = jnp.concatenate([-sin, sin], axis=1)
    p32 = jnp.asarray(np.roll(np.eye(QK_ROPE, dtype=np.float32), HALF_ROPE, axis=1))
    sum8 = jnp.asarray(np.tile(np.eye(QK_ROPE, dtype=np.float32), (HEADS, 1)))
    head_sum = jnp.asarray(np.repeat(np.eye(HEADS, dtype=np.float32), V_HEAD, axis=0))

    ng = 2 * N_GROUPS
    a_re3 = wts['ssm_a_re'].reshape(ng, 1, SSM_STATE)
    a_im3 = wts['ssm_a_im'].reshape(ng, 1, SSM_STATE)
    ldt3 = wts['ssm_log_dt'].reshape(ng, 1, 1)
    bt_re = wts['ssm_b_re'].reshape(2, N_GROUPS, SSM_STATE, SSM_GROUP).transpose(0, 1, 3, 2).reshape(
        ng, SSM_GROUP, SSM_STATE)
    bt_im = wts['ssm_b_im'].reshape(2, N_GROUPS, SSM_STATE, SSM_GROUP).transpose(0, 1, 3, 2).reshape(
        ng, SSM_GROUP, SSM_STATE)
    c_re = wts['ssm_c_re'].reshape(ng, SSM_GROUP, SSM_STATE)
    c_im = wts['ssm_c_im'].reshape(ng, SSM_GROUP, SSM_STATE)
    abar_re, abar_im, bbt_re, bbt_im = _ssm_disc(a_re3, a_im3, ldt3, bt_re, bt_im)

    def direction(t, d):
        return t[d * N_GROUPS:(d + 1) * N_GROUPS]

    def slab(t, d, sign=1.0):
        return jnp.broadcast_to(sign * direction(t, d).reshape(1, N_STATES), (N_SEG, N_STATES))

    w_re = [_block_diag(direction(bbt_re, d)).astype(BF16) for d in range(2)]
    w_im = [_block_diag(direction(bbt_im, d)).astype(BF16) for d in range(2)]
    cb_re = [_block_diag(direction(c_re, d)).astype(BF16) for d in range(2)]
    cb_im = [_block_diag(-direction(c_im, d)).astype(BF16) for d in range(2)]

    def to_rows(a):
        return a.reshape(N_SEG, lseg, a.shape[-1]).transpose(1, 0, 2).reshape(lp, a.shape[-1])

    def to_tokens(a):
        return a.reshape(lseg, N_SEG, a.shape[-1]).transpose(1, 0, 2).reshape(lp, a.shape[-1])

    pad = jnp.zeros((lp - n_valid, D_MODEL), F32)
    h = to_rows(jnp.concatenate([full['meta_tokens'], x[0], pad], axis=0))
    tgt = to_rows(jnp.concatenate([jnp.zeros((N_META, D_MODEL), F32), loss_target[0], pad], axis=0))

    ql, kvl, ag, su, sg, kr = _inproj(h, pre_w, w_in_b, tr)
    qn_b, qr1_b, qr2_b, kn_b, v_b, kr_b = _qkv_up(ql, kvl, kr, cos8, sin8, c32, s32, qw, kvw, wq_b, wkv_b, p32, tr)

    def heads(a, w):
        return a.reshape(lp, HEADS, w)

    nq, nk = lp // tq, lp // tk
    q_t = jnp.concatenate([heads(qn_b, 64), heads(qr1_b, 16), heads(qr2_b, 16)], axis=-1)
    k_t = jnp.concatenate([heads(kn_b, 64), jnp.broadcast_to(kr_b[:, None, :], (lp, HEADS, QK_ROPE))], axis=-1)
    v_t = heads(v_b, 64)
    vx_t = jnp.concatenate([v_t, jnp.ones((lp, HEADS, 1), BF16), jnp.zeros((lp, HEADS, LANES - V_HEAD - 1), BF16)],
                           axis=-1)
    qt4 = q_t.reshape(nq, tq, HEADS, QK_DIM).transpose(2, 0, 3, 1)
    tk_fwd = _pick(lp, [1280, 256])
    vxt4 = vx_t.reshape(lp // tk_fwd, tk_fwd, HEADS, LANES).transpose(2, 0, 3, 1)
    k_h = k_t.transpose(1, 0, 2)
    kt_h = k_t.transpose(1, 2, 0)
    v_h = v_t.transpose(1, 0, 2)
    ot_h, lse4 = _attn_fwd(qt4, k_h, vxt4, n_valid)
    o_flat = ot_h.transpose(2, 0, 1).reshape(lp, D_ATTN)

    xs, ys = [], []
    for d in range(2):
        ar8, ai8 = slab(abar_re, d), slab(abar_im, d)
        ere, eim = _scan_ends(f"scan{d}_ends", su, w_re[d], w_im[d], ar8, ai8, d == 0)
        x_re, x_im, y_d = _scan_fwd(f"scan{d}", su, w_re[d], w_im[d], ar8, ai8, ere, eim, cb_re[d], cb_im[d],
                                    d == 0)
        xs += [x_re, x_im]
        ys.append(y_d)
    ypre, glu, ysn = _ssm_post(ys[0], ys[1], su, sg, wglu_b, bglu, sw, dvec, tr)

    dy, dout, loss, d_post = _out_loss(o_flat, ag, ysn, h, tgt, wo_b, post_w, aw, n_valid, tr)

    do_flat, dag, dysn, delta8, d_wo, d_aw = _out_bwd(dy, ysn, o_flat, ag, wo_b, aw, head_sum, tr)
    dypre, dsg, d_wglu, d_bglu, d_sw, d_dvec = _ssm_post_bwd(dysn, glu, sg, ypre, su, wglu_b, sw, dvec, tr)

    dus, d_ct, d_wb, d_a8 = [], [], [], []
    for d in range(2):
        ar8, ai8c = slab(abar_re, d), slab(abar_im, d, -1.0)
        ere, eim = _scan_ends(f"scan_adj{d}_ends", dypre, cb_re[d], cb_im[d], ar8, ai8c, d != 0)
        du_d, dw_re, dw_im, dc_re, dc_im, da_re, da_im = _scan_bwd(
            f"scan_adj{d}", dypre, cb_re[d], cb_im[d], ar8, ai8c, ere, eim, su, w_re[d], w_im[d],
            xs[2 * d], xs[2 * d + 1], d != 0)
        dus.append(du_d)
        d_ct.append((dc_re, dc_im))
        d_wb.append((dw_re, dw_im))
        d_a8.append((da_re, da_im))

    dot4 = do_flat.astype(BF16).reshape(nq, tq, HEADS, V_HEAD).transpose(2, 0, 3, 1)
    dqt4, dkt_h, dvt_h = _attn_bwd(qt4, k_h, kt_h, v_h, dot4, lse4, delta8.T.reshape(HEADS, nq, 1, tq), tk)
    dq_t = dqt4.transpose(1, 3, 0, 2).reshape(lp, HEADS, QK_DIM)
    dk_t = dkt_h.transpose(2, 0, 1)
    dqn = dq_t[:, :, :64].reshape(lp, 512)
    dr1 = dq_t[:, :, 64:80].reshape(lp, 128)
    dr2 = dq_t[:, :, 80:96].reshape(lp, 128)
    dkn = dk_t[:, :, :64].reshape(lp, 512)
    dkr8 = dk_t[:, :, 64:].reshape(lp, HEADS * QK_ROPE)
    dvf = dvt_h.transpose(2, 0, 1).reshape(lp, 512)
    dql, dkvl, dkrr, d_wq, d_wkv, d_qw, d_kvw = _qkv_up_bwd(
        dqn, dr1, dr2, dkn, dvf, dkr8, ql, kvl, cos8, sin8, c32, s32, qw, kvw, wq_b, wkv_b, p32, sum8, tr)
    dh, d_win, d_pre = _inproj_bwd(dql, dkvl, dag, dus[0], dus[1], dypre, dsg, dkrr, h, dout, pre_w, w_in_b, dvec,
                                   tr_mid)
    dh = to_tokens(dh)

    def seg_sums(t):
        return t.reshape(N_SEG, N_GROUPS, SSM_STATE).transpose(1, 0, 2)

    da8_re = jnp.concatenate([seg_sums(d_a8[d][0]) for d in range(2)], axis=0)
    da8_im = jnp.concatenate([seg_sums(d_a8[d][1]) for d in range(2)], axis=0)
    dbb_re = jnp.concatenate([_diag_blocks(d_wb[d][0]) for d in range(2)], axis=0)
    dbb_im = jnp.concatenate([_diag_blocks(d_wb[d][1]) for d in range(2)], axis=0)
    g_are, g_aim, g_ldt, g_bt_re, g_bt_im = _ssm_disc_bwd(a_re3, a_im3, ldt3, bt_re, bt_im, da8_re, da8_im,
                                                          dbb_re, dbb_im)
    g_c_re = jnp.concatenate([_diag_blocks(d_ct[d][0]) for d in range(2)], axis=0)
    g_c_im = jnp.concatenate([-_diag_blocks(d_ct[d][1]) for d in range(2)], axis=0)

    def b_layout(t):
        return t.reshape(2, N_GROUPS, SSM_GROUP, SSM_STATE).transpose(0, 1, 3, 2)

    local = {
        'meta_tokens': dh[:N_META],
        'pre_norm_w': d_pre, 'post_norm_w': d_post,
        'w_in': _cols_in_inv(d_win[:, :D_IN]),
        'q_norm_w': d_qw, 'w_q_up': _cols_q_inv(d_wq),
        'kv_norm_w': d_kvw, 'w_kv_up': _cols_kv_inv(d_wkv),
        'attn_out_norm_w': d_aw,
        'ssm_a_re': g_are, 'ssm_a_im': g_aim, 'ssm_log_dt': g_ldt,
        'ssm_b_re': b_layout(g_bt_re), 'ssm_b_im': b_layout(g_bt_im), 'ssm_c_re': g_c_re, 'ssm_c_im': g_c_im,
        'ssm_d': d_dvec, 'w_glu': d_wglu, 'b_glu': d_bglu, 'ssm_out_norm_w': d_sw, 'w_out': d_wo,
    }

    replicated = [n for n in WEIGHTS if n not in SHARDED]
    order = SHARDED + replicated
    shapes = [wts[n].shape for n in order] + [(1, 1)]
    tr_adam = 512
    total_rows = -(-sum(_n_rows(s) for s in shapes) // tr_adam) * tr_adam
    recv = _exchange("exchange_grads",
                     _pack_per_device([_shard_views(n, local[n]) for n in SHARDED], n_shard_rows),
                     _pack([local[n] for n in replicated] + [loss], total_rows - n_shard_rows))
    zero = jnp.zeros((1, 1), F32)
    packed = [_pack([src[n] for n in order] + [zero], total_rows) for src in (wts, moms, vels)]
    g_p, d_p, m_p, v_p = _adamw(recv, *packed, tr_adam)
    sums = _unpack(g_p, shapes)
    grads = dict(zip(order, sums))
    deltas, new_m, new_v = (dict(zip(order, _unpack(b, shapes))) for b in (d_p, m_p, v_p))

    grad_x = dh[N_META:n_valid][None]
    return (sums[-1][0, 0], grad_x, *[grads[n] for n in WEIGHTS], *[deltas[n] for n in WEIGHTS],
            *[new_m[n] for n in WEIGHTS], *[new_v[n] for n in WEIGHTS])


def kernel(x, meta_tokens, pre_norm_w, post_norm_w, w_in, q_norm_w, w_q_up, kv_norm_w, w_kv_up, attn_out_norm_w, ssm_a_re, ssm_a_im, ssm_log_dt, ssm_b_re, ssm_b_im, ssm_c_re, ssm_c_im, ssm_d, w_glu, b_glu, ssm_out_norm_w, w_out, loss_target, m_meta_tokens, m_pre_norm_w, m_post_norm_w, m_w_in, m_q_norm_w, m_w_q_up, m_kv_norm_w, m_w_kv_up, m_attn_out_norm_w, m_ssm_a_re, m_ssm_a_im, m_ssm_log_dt, m_ssm_b_re, m_ssm_b_im, m_ssm_c_re, m_ssm_c_im, m_ssm_d, m_w_glu, m_b_glu, m_ssm_out_norm_w, m_w_out, v_meta_tokens, v_pre_norm_w, v_post_norm_w, v_w_in, v_q_norm_w, v_w_q_up, v_kv_norm_w, v_w_kv_up, v_attn_out_norm_w, v_ssm_a_re, v_ssm_a_im, v_ssm_log_dt, v_ssm_b_re, v_ssm_b_im, v_ssm_c_re, v_ssm_c_im, v_ssm_d, v_w_glu, v_b_glu, v_ssm_out_norm_w, v_w_out):
    wts = dict(zip(WEIGHTS, (meta_tokens, pre_norm_w, post_norm_w, w_in, q_norm_w, w_q_up, kv_norm_w, w_kv_up,
                             attn_out_norm_w, ssm_a_re, ssm_a_im, ssm_log_dt, ssm_b_re, ssm_b_im, ssm_c_re,
                             ssm_c_im, ssm_d, w_glu, b_glu, ssm_out_norm_w, w_out)))
    moms = dict(zip(WEIGHTS, (m_meta_tokens, m_pre_norm_w, m_post_norm_w, m_w_in, m_q_norm_w, m_w_q_up,
                              m_kv_norm_w, m_w_kv_up, m_attn_out_norm_w, m_ssm_a_re, m_ssm_a_im, m_ssm_log_dt,
                              m_ssm_b_re, m_ssm_b_im, m_ssm_c_re, m_ssm_c_im, m_ssm_d, m_w_glu, m_b_glu,
                              m_ssm_out_norm_w, m_w_out)))
    vels = dict(zip(WEIGHTS, (v_meta_tokens, v_pre_norm_w, v_post_norm_w, v_w_in, v_q_norm_w, v_w_q_up,
                              v_kv_norm_w, v_w_kv_up, v_attn_out_norm_w, v_ssm_a_re, v_ssm_a_im, v_ssm_log_dt,
                              v_ssm_b_re, v_ssm_b_im, v_ssm_c_re, v_ssm_c_im, v_ssm_d, v_w_glu, v_b_glu,
                              v_ssm_out_norm_w, v_w_out)))
    return _step(x, loss_target, wts, moms, vels)
```

```python
import functools
import math

import numpy as np
import jax
import jax.numpy as jnp
from jax import lax
from jax.experimental import pallas as pl
from jax.experimental.pallas import tpu as pltpu

F32 = jnp.float32
BF16 = jnp.bfloat16

D_MODEL = 1024
N_META = 16
EPS = 1e-6
HEADS = 8
QK_NOPE = 64
QK_ROPE = 32
HALF_ROPE = QK_ROPE // 2
QK_DIM = QK_NOPE + QK_ROPE
V_HEAD = 64
Q_LORA = 256
KV_LORA = 128
D_ATTN = HEADS * V_HEAD
D_SSM = 512
SSM_GROUP = 16
N_GROUPS = D_SSM // SSM_GROUP
SSM_STATE = 64
N_STATES = N_GROUPS * SSM_STATE
ROPE_THETA = 10000.0
D_IN = Q_LORA + KV_LORA + QK_ROPE + D_ATTN + 2 * D_SSM
D_IN_PAD = 2048
N_DEV = 8
N_SEG = 8
COL_BLK = 512
LANES = 128

ADAM_LR = 0.001
ADAM_B1 = 0.9
ADAM_B2 = 0.999
ADAM_EPS = 1e-08
ADAM_WD = 0.01
ADAM_STEP = 10

VMEM_LIMIT_V7X = 56 * 1024 * 1024
LOG2E = 1.0 / math.log(2.0)
Q_SCALE = LOG2E / math.sqrt(QK_DIM)
ATTN_UNROLL = 4
ATTN_BWD_UNROLL = 4

WEIGHTS = ['meta_tokens', 'pre_norm_w', 'post_norm_w', 'w_in', 'q_norm_w', 'w_q_up', 'kv_norm_w', 'w_kv_up',
           'attn_out_norm_w', 'ssm_a_re', 'ssm_a_im', 'ssm_log_dt', 'ssm_b_re', 'ssm_b_im', 'ssm_c_re', 'ssm_c_im',
           'ssm_d', 'w_glu', 'b_glu', 'ssm_out_norm_w', 'w_out']
SHARDED = ['w_in', 'w_q_up', 'w_kv_up', 'w_glu', 'w_out', 'meta_tokens']

def _cols_in(w):
    return jnp.concatenate([w[:, 0:384], w[:, 416:D_IN], w[:, 384:416]], axis=1)


def _cols_in_inv(w):
    return jnp.concatenate([w[:, 0:384], w[:, D_IN - QK_ROPE:D_IN], w[:, 384:D_IN - QK_ROPE]], axis=1)


def _cols_q(w):
    t = w.reshape(w.shape[0], HEADS, QK_DIM)
    return jnp.concatenate([t[:, :, 0:64].reshape(-1, 512), t[:, :, 64:80].reshape(-1, 128),
                            t[:, :, 80:96].reshape(-1, 128)], axis=1)


def _cols_q_inv(w):
    r = w.shape[0]
    return jnp.concatenate([w[:, 0:512].reshape(r, HEADS, 64), w[:, 512:640].reshape(r, HEADS, 16),
                            w[:, 640:768].reshape(r, HEADS, 16)], axis=2).reshape(r, HEADS * QK_DIM)


def _cols_kv(w):
    t = w.reshape(w.shape[0], HEADS, 128)
    return jnp.concatenate([t[:, :, 0:64].reshape(-1, 512), t[:, :, 64:128].reshape(-1, 512)], axis=1)


def _cols_kv_inv(w):
    r = w.shape[0]
    return jnp.concatenate([w[:, 0:512].reshape(r, HEADS, 64), w[:, 512:1024].reshape(r, HEADS, 64)],
                           axis=2).reshape(r, HEADS * 128)


def _pick(n, cands):
    for c in cands:
        if n % c == 0:
            return c
    raise ValueError(f"no tile for {n}")


def _cparams(*sem):
    return pltpu.CompilerParams(dimension_semantics=sem, vmem_limit_bytes=VMEM_LIMIT_V7X)


def _mm(a, b):
    return jnp.dot(a.astype(BF16), b.astype(BF16), preferred_element_type=F32)


def _mm_nt(a, b):
    return lax.dot_general(a.astype(BF16), b.astype(BF16), (((1,), (1,)), ((), ())), preferred_element_type=F32)


def _mm_tn(a, b):
    return lax.dot_general(a.astype(BF16), b.astype(BF16), (((0,), (0,)), ((), ())), preferred_element_type=F32)


def _mm_exact(a, b):
    return jnp.dot(a, b, precision=lax.Precision.HIGHEST, preferred_element_type=F32)


def _rms(x):
    return lax.rsqrt(jnp.mean(x * x, axis=-1, keepdims=True) + EPS)


def _rms_bwd(dy, x, r, w):
    xh = x * r
    g = dy * w
    dx = r * (g - xh * jnp.mean(g * xh, axis=-1, keepdims=True))
    dw = jnp.sum(dy * xh, axis=0, keepdims=True)
    return dx, dw


def _sigmoid(z):
    return 1.0 / (1.0 + jnp.exp(-z))


def _silu_and_grad(z):
    s = _sigmoid(z)
    return z * s, s * (1.0 + z * (1.0 - s))


_GELU_C = math.sqrt(2.0 / math.pi)


def _gelu_and_grad(x):
    x2 = x * x
    t = jnp.tanh(_GELU_C * (x + 0.044715 * x * x2))
    val = 0.5 * x * (1.0 + t)
    grad = 0.5 * (1.0 + t) + 0.5 * x * (1.0 - t * t) * _GELU_C * (1.0 + 3.0 * 0.044715 * x2)
    return val, grad


def _acc(ref, val, first):
    @pl.when(first)
    def _():
        ref[...] = val

    @pl.when(jnp.logical_not(first))
    def _():
        ref[...] += val


def _rows_call(name, body, tr, row_ins, full_ins, row_outs, acc_outs):
    lp = row_ins[0].shape[0]
    in_specs = [pl.BlockSpec((tr, a.shape[1]), lambda i: (i, 0)) for a in row_ins]
    in_specs += [pl.BlockSpec(a.shape, lambda i, n=a.ndim: (0,) * n) for a in full_ins]
    out_specs = [pl.BlockSpec((tr, c), lambda i: (i, 0)) for c, _ in row_outs]
    out_specs += [pl.BlockSpec(s, lambda i, n=len(s): (0,) * n) for s, _ in acc_outs]
    out_shape = [jax.ShapeDtypeStruct((lp, c), dt) for c, dt in row_outs]
    out_shape += [jax.ShapeDtypeStruct(s, dt) for s, dt in acc_outs]
    return pl.pallas_call(
        body, name=name, grid=(lp // tr,), in_specs=in_specs, out_specs=out_specs, out_shape=out_shape,
        compiler_params=_cparams("arbitrary"))(*row_ins, *full_ins)


def _inproj(h, pre_w, w_in_b, tr):
    def body(h_ref, pw_ref, w_ref, ql, kvl, ag, su, sg, kr):
        x = h_ref[...]
        xn = x * _rms(x) * pw_ref[...]
        pr = _mm(xn, w_ref[...])
        ql[...] = pr[:, 0:256]
        kvl[...] = pr[:, 256:384]
        ag[...] = pr[:, 384:896]
        su[...] = pr[:, 896:1408]
        sg[...] = pr[:, 1408:1920]
        kr[...] = pr[:, 1920:1952]

    return _rows_call("inproj", body, tr, [h], [pre_w, w_in_b],
                      [(256, F32), (128, F32), (512, F32), (512, F32), (512, F32), (32, F32)], [])


def _qkv_up(ql, kvl, kr, cos8, sin8, c32, s32, qw, kvw, wq_b, wkv_b, p32, tr):
    def body(ql_ref, kvl_ref, kr_ref, cos_ref, sin_ref, c32_ref, s32_ref, qw_ref, kvw_ref, wq_ref, wkv_ref, p_ref,
             qn_o, qr1_o, qr2_o, kn_o, v_o, kr_o):
        x = ql_ref[...]
        q = _mm(x * _rms(x) * qw_ref[...], wq_ref[...]) * Q_SCALE
        r1, r2 = q[:, 512:640], q[:, 640:768]
        cs, sn = cos_ref[...], sin_ref[...]
        qn_o[...] = q[:, 0:512].astype(BF16)
        qr1_o[...] = (r1 * cs - r2 * sn).astype(BF16)
        qr2_o[...] = (r2 * cs + r1 * sn).astype(BF16)
        x = kvl_ref[...]
        kv = _mm(x * _rms(x) * kvw_ref[...], wkv_ref[...])
        kn_o[...] = kv[:, 0:512].astype(BF16)
        v_o[...] = kv[:, 512:1024].astype(BF16)
        x = kr_ref[...]
        kr_o[...] = (x * c32_ref[...] + _mm_exact(x, p_ref[...]) * s32_ref[...]).astype(BF16)

    return _rows_call("qkv_up", body, tr, [ql, kvl, kr, cos8, sin8, c32, s32], [qw, kvw, wq_b, wkv_b, p32],
                      [(512, BF16), (128, BF16), (128, BF16), (512, BF16), (512, BF16), (32, BF16)], [])


def _row_position(row, lseg):
    return (row & (N_SEG - 1)) * lseg + (row >> 3)


def _first_padded_tile(n_valid, lp, tile):
    lseg = lp // N_SEG
    t0 = n_valid - (N_SEG - 1) * lseg
    return (t0 * N_SEG + N_SEG - 1) // tile if n_valid < lp else lp // tile


def _attn_fwd(qt, k, vxt, n_valid):
    _, nq, _, tq = qt.shape
    _, nk, _, tk = vxt.shape
    lp = k.shape[1]
    lseg = lp // N_SEG
    n_plain = max(0, min(nk, _first_padded_tile(n_valid, lp, tk)))

    def body(q_ref, k_ref, v_ref, o_ref, lse_ref, m_s, acc_s):
        m_s[...] = jnp.full(m_s.shape, -1e30, F32)
        acc_s[...] = jnp.zeros(acc_s.shape, F32)
        qq = q_ref[0, 0]

        def chunk(c, padded):
            r0 = pl.multiple_of(c * tk, tk)
            st = _mm(k_ref[0, pl.ds(r0, tk), :], qq)
            if padded:
                row = r0 + lax.broadcasted_iota(jnp.int32, (tk, tq), 0)
                st = jnp.where(_row_position(row, lseg) < n_valid, st, -1e30)
            m_old = m_s[...]
            m_new = jnp.maximum(m_old, jnp.max(st, axis=0, keepdims=True))
            pt = jnp.exp2(st - m_new)
            acc_s[...] = jnp.exp2(m_old - m_new) * acc_s[...] + _mm(v_ref[0, c], pt)
            m_s[...] = m_new

        def plain(c, carry):
            chunk(c, False)
            return carry

        n_loop = n_plain - n_plain % ATTN_UNROLL
        if n_loop:
            lax.fori_loop(0, n_loop, plain, 0, unroll=ATTN_UNROLL)
        for c in range(n_loop, nk):
            chunk(c, c >= n_plain)
        acc = acc_s[...]
        l = acc[V_HEAD:V_HEAD + 1, :]
        o_ref[0] = acc[:V_HEAD, :] / l
        lse_ref[0, 0] = m_s[...] + jnp.log2(l)

    return pl.pallas_call(
        body, name="attn_fwd", grid=(HEADS, nq),
        in_specs=[pl.BlockSpec((1, 1, QK_DIM, tq), lambda h, i: (h, i, 0, 0)),
                  pl.BlockSpec((1, lp, QK_DIM), lambda h, i: (h, 0, 0)),
                  pl.BlockSpec((1, nk, LANES, tk), lambda h, i: (h, 0, 0, 0))],
        out_specs=[pl.BlockSpec((1, V_HEAD, tq), lambda h, i: (h, 0, i)),
                   pl.BlockSpec((1, 1, 1, tq), lambda h, i: (h, i, 0, 0))],
        out_shape=[jax.ShapeDtypeStruct((HEADS, V_HEAD, lp), F32), jax.ShapeDtypeStruct((HEADS, nq, 1, tq), F32)],
        scratch_shapes=[pltpu.VMEM((1, tq), F32), pltpu.VMEM((LANES, tq), F32)],
        compiler_params=_cparams("arbitrary", "arbitrary"))(qt, k, vxt)


def _scan_tiles(lp, longest):
    lseg = lp // N_SEG
    tt = _pick(lseg, [longest, 48, 32, 16, 8, 4, 2, 1])
    return lseg, tt, lseg // tt


def _cmul(ar, ai, br, bi):
    return ar * br - ai * bi, ar * bi + ai * br


SCAN_STEPS_PER_ITER = 4
N_COL_BLK = N_STATES // COL_BLK
CH_BLK = D_SSM // N_COL_BLK


def _scan_steps(tt, forward, bre_ref, bim_ref, ar, ai, carry, visit):
    def step(s, c):
        r0 = pl.multiple_of((s if forward else tt - 1 - s) * N_SEG, N_SEG)
        pr, pi = _cmul(ar, ai, c[0], c[1])
        xr = pr + bre_ref[pl.ds(r0, N_SEG), :]
        xi = pi + bim_ref[pl.ds(r0, N_SEG), :]
        return (xr, xi) + tuple(visit(r0, (xr, xi), (c[0], c[1]), c[2:]))

    per = SCAN_STEPS_PER_ITER if tt % SCAN_STEPS_PER_ITER == 0 else 1

    def steps(it, c):
        for u in range(per):
            c = step(it * per + u, c)
        return c

    return lax.fori_loop(0, tt // per, steps, carry)


def _segment_starts(lseg, forward, ar, ai, ere_ref, eim_ref, s_re, s_im):
    a1r, a1i = ar[0:1, :], ai[0:1, :]
    pr, pi = jnp.ones_like(a1r), jnp.zeros_like(a1i)
    br, bi = a1r, a1i
    n = lseg
    while n:
        if n & 1:
            pr, pi = _cmul(pr, pi, br, bi)
        n >>= 1
        if n:
            br, bi = _cmul(br, bi, br, bi)
    cr, ci = jnp.zeros_like(a1r), jnp.zeros_like(a1i)
    for j in (range(N_SEG) if forward else range(N_SEG - 1, -1, -1)):
        s_re[j:j + 1, :] = cr
        s_im[j:j + 1, :] = ci
        nr, ni = _cmul(pr, pi, cr, ci)
        cr = nr + ere_ref[j:j + 1, :]
        ci = ni + eim_ref[j:j + 1, :]


def _scan_specs(lp, forward, longest=208):
    lseg, tt, nt = _scan_tiles(lp, longest)

    def tile(t):
        return t if forward else nt - 1 - t

    rows = lambda w: pl.BlockSpec((tt * N_SEG, w), lambda cb, t: (tile(t), cb))
    proj = pl.BlockSpec((1, CH_BLK, COL_BLK), lambda cb, t: (cb, 0, 0))
    slab = pl.BlockSpec((N_SEG, COL_BLK), lambda cb, t: (0, cb))
    return lseg, tt, nt, rows, proj, slab


def _scan_ends(name, urows, wre4, wim4, ar8, ai8, forward):
    lp = urows.shape[0]
    lseg, tt, nt, rows, proj, slab = _scan_specs(lp, forward, longest=520)

    def body(u_ref, wre_ref, wim_ref, ar_ref, ai_ref, ere_o, eim_o, bre_s, bim_s, cr_s, ci_s):
        t = pl.program_id(1)

        @pl.when(t == 0)
        def _():
            cr_s[...] = jnp.zeros(cr_s.shape, F32)
            ci_s[...] = jnp.zeros(ci_s.shape, F32)

        u = u_ref[...]
        bre_s[...] = _mm(u, wre_ref[0])
        bim_s[...] = _mm(u, wim_ref[0])
        cr, ci = _scan_steps(tt, forward, bre_s, bim_s, ar_ref[...], ai_ref[...], (cr_s[...], ci_s[...]),
                             lambda r0, x, x_prev, extra: ())
        cr_s[...] = cr
        ci_s[...] = ci

        @pl.when(t == nt - 1)
        def _():
            ere_o[...] = cr
            eim_o[...] = ci

    return pl.pallas_call(
        body, name=name, grid=(N_COL_BLK, nt), in_specs=[rows(CH_BLK), proj, proj, slab, slab],
        out_specs=[slab, slab], out_shape=[jax.ShapeDtypeStruct((N_SEG, N_STATES), F32)] * 2,
        scratch_shapes=[pltpu.VMEM((tt * N_SEG, COL_BLK), F32)] * 2 + [pltpu.VMEM((N_SEG, COL_BLK), F32)] * 2,
        compiler_params=_cparams("arbitrary", "arbitrary"))(urows, wre4, wim4, ar8, ai8)


def _scan_fwd(name, urows, wre4, wim4, ar8, ai8, ere, eim, cre4, cim4, forward):
    lp = urows.shape[0]
    lseg, tt, nt, rows, proj, slab = _scan_specs(lp, forward)

    def body(u_ref, wre_ref, wim_ref, ar_ref, ai_ref, ere_ref, eim_ref, cre_ref, cim_ref,
             xre_o, xim_o, y_o, bre_s, bim_s, cr_s, ci_s):
        ar, ai = ar_ref[...], ai_ref[...]

        @pl.when(pl.program_id(1) == 0)
        def _():
            _segment_starts(lseg, forward, ar, ai, ere_ref, eim_ref, cr_s, ci_s)

        u = u_ref[...]
        bre_s[...] = _mm(u, wre_ref[0])
        bim_s[...] = _mm(u, wim_ref[0])

        def visit(r0, x, x_prev, extra):
            xre_o[pl.ds(r0, N_SEG), :] = x[0]
            xim_o[pl.ds(r0, N_SEG), :] = x[1]
            return ()

        cr, ci = _scan_steps(tt, forward, bre_s, bim_s, ar, ai, (cr_s[...], ci_s[...]), visit)
        cr_s[...] = cr
        ci_s[...] = ci
        y_o[...] = _mm_nt(xre_o[...], cre_ref[0]) + _mm_nt(xim_o[...], cim_ref[0])

    return pl.pallas_call(
        body, name=name, grid=(N_COL_BLK, nt),
        in_specs=[rows(CH_BLK), proj, proj, slab, slab, slab, slab, proj, proj],
        out_specs=[rows(COL_BLK), rows(COL_BLK), rows(CH_BLK)],
        out_shape=[jax.ShapeDtypeStruct((lp, N_STATES), F32)] * 2 + [jax.ShapeDtypeStruct((lp, D_SSM), F32)],
        scratch_shapes=[pltpu.VMEM((tt * N_SEG, COL_BLK), F32)] * 2 + [pltpu.VMEM((N_SEG, COL_BLK), F32)] * 2,
        compiler_params=_cparams("arbitrary", "arbitrary"))(urows, wre4, wim4, ar8, ai8, ere, eim, cre4, cim4)


def _scan_bwd(name, dyrows, cre4, cim4, ar8, ai8, ere, eim, urows, wre4, wim4, xre, xim, forward):
    lp = urows.shape[0]
    lseg, tt, nt, rows, proj, slab = _scan_specs(lp, forward)

    def body(dy_ref, cre_ref, cim_ref, ar_ref, ai_ref, ere_ref, eim_ref, u_ref, wre_ref, wim_ref, xre_ref, xim_ref,
             du_o, dwre_o, dwim_o, dcre_o, dcim_o, dare_o, daim_o, bre_s, bim_s, gre_s, gim_s, cr_s, ci_s):
        t = pl.program_id(1)
        ar, ai = ar_ref[...], ai_ref[...]

        @pl.when(t == 0)
        def _():
            _segment_starts(lseg, forward, ar, ai, ere_ref, eim_ref, cr_s, ci_s)
            dare_o[...] = jnp.zeros(dare_o.shape, F32)
            daim_o[...] = jnp.zeros(daim_o.shape, F32)

        dy = dy_ref[...]
        bre_s[...] = _mm(dy, cre_ref[0])
        bim_s[...] = _mm(dy, cim_ref[0])

        def visit(r0, g, g_prev, sums):
            gre_s[pl.ds(r0, N_SEG), :] = g[0]
            gim_s[pl.ds(r0, N_SEG), :] = g[1]
            fr = xre_ref[pl.ds(r0, N_SEG), :]
            fi = xim_ref[pl.ds(r0, N_SEG), :]
            pr, pi = g_prev
            return sums[0] + fr * pr + fi * pi, sums[1] + fr * pi - fi * pr

        out = _scan_steps(tt, forward, bre_s, bim_s, ar, ai, (cr_s[...], ci_s[...], dare_o[...], daim_o[...]), visit)
        cr_s[...] = out[0]
        ci_s[...] = out[1]
        dare_o[...] = out[2]
        daim_o[...] = out[3]
        gre, gim = gre_s[...], gim_s[...]
        du_o[...] = _mm_nt(gre, wre_ref[0]) + _mm_nt(gim, wim_ref[0])
        u = u_ref[...]
        first = t == 0
        _acc(dwre_o, _mm_tn(u, gre)[None], first)
        _acc(dwim_o, _mm_tn(u, gim)[None], first)
        _acc(dcre_o, _mm_tn(dy, xre_ref[...])[None], first)
        _acc(dcim_o, _mm_tn(dy, xim_ref[...])[None], first)

    big = pltpu.VMEM((tt * N_SEG, COL_BLK), F32)
    small = pltpu.VMEM((N_SEG, COL_BLK), F32)
    return pl.pallas_call(
        body, name=name, grid=(N_COL_BLK, nt),
        in_specs=[rows(CH_BLK), proj, proj, slab, slab, slab, slab, rows(CH_BLK), proj, proj,
                  rows(COL_BLK), rows(COL_BLK)],
        out_specs=[rows(CH_BLK), proj, proj, proj, proj, slab, slab],
        out_shape=[jax.ShapeDtypeStruct((lp, D_SSM), F32)]
        + [jax.ShapeDtypeStruct((N_COL_BLK, CH_BLK, COL_BLK), F32)] * 4
        + [jax.ShapeDtypeStruct((N_SEG, N_STATES), F32)] * 2,
        scratch_shapes=[big, big, big, big, small, small],
        compiler_params=_cparams("arbitrary", "arbitrary"))(
            dyrows, cre4, cim4, ar8, ai8, ere, eim, urows, wre4, wim4, xre, xim)


def _ssm_post(yf, yb, u, sg, wglu_b, bglu, sw, dvec, tr):
    def body(yf_ref, yb_ref, u_ref, g_ref, w_ref, b_ref, sw_ref, d_ref, ypre_o, glu_o, ysn_o):
        ypre = yf_ref[...] + yb_ref[...] + d_ref[...] * u_ref[...]
        ypre_o[...] = ypre
        glu = _mm(_gelu_and_grad(ypre)[0], w_ref[...]) + b_ref[...]
        glu_o[...] = glu
        t = glu[:, :D_SSM] * _sigmoid(glu[:, D_SSM:]) * _silu_and_grad(g_ref[...])[0]
        ysn_o[...] = t * _rms(t) * sw_ref[...]

    return _rows_call("ssm_post", body, tr, [yf, yb, u, sg], [wglu_b, bglu, sw, dvec],
                      [(512, F32), (1024, F32), (512, F32)], [])


def _attn_gate_norm(o, gate, w):
    sl, dsl = _silu_and_grad(gate)
    t = o * sl
    r = _rms(t)
    return t * r * w, t, r, sl, dsl


def _out_fwd_bwd(o_flat, ag, ysn, h, tgt, wo_b, post_w, aw, head_sum, n_valid, tr):
    lseg = h.shape[0] // N_SEG

    def body(o_ref, g_ref, ys_ref, h_ref, t_ref, w_ref, pw_ref, aw_ref, hs_ref,
             dout_o, do_o, dag_o, dysn_o, dl_o, loss_o, dpw_o, dwo_o, daw_o):
        i = pl.program_id(0)
        o = o_ref[...]
        aw = aw_ref[...]
        ya, t, rt, sl, dsl = _attn_gate_norm(o, g_ref[...], aw)
        cat = jnp.concatenate([ya, ys_ref[...]], axis=1)
        y = _mm(cat, w_ref[...])
        r = _rms(y)
        pw = pw_ref[...]
        out = h_ref[...] + y * r * pw
        pos = _row_position(i * tr + lax.broadcasted_iota(jnp.int32, (tr, 1), 0), lseg)
        valid = jnp.logical_and(pos >= N_META, pos < n_valid)
        diff = jnp.where(valid, out - t_ref[...], 0.0)
        dout = diff * (1.0 / D_MODEL)
        dout_o[...] = dout
        _acc(loss_o, 0.5 * jnp.sum(jnp.sum(diff * diff, axis=1, keepdims=True), axis=0, keepdims=True)
             * (1.0 / D_MODEL), i == 0)
        dy, dpw = _rms_bwd(dout, y, r, pw)
        _acc(dpw_o, dpw, i == 0)
        dcat = _mm_nt(dy, w_ref[...])
        _acc(dwo_o, _mm_tn(cat, dy), i == 0)
        dysn_o[...] = dcat[:, D_ATTN:]
        dt, daw = _rms_bwd(dcat[:, :D_ATTN], t, rt, aw)
        _acc(daw_o, daw, i == 0)
        do = dt * sl
        do_o[...] = do
        dag_o[...] = dt * o * dsl
        dl_o[...] = _mm_exact(do * o, hs_ref[...])

    return _rows_call("out_fwd_bwd", body, tr, [o_flat, ag, ysn, h, tgt], [wo_b, post_w, aw, head_sum],
                      [(1024, F32), (512, F32), (512, F32), (512, F32), (HEADS, F32)],
                      [((1, 1), F32), ((1, D_MODEL), F32), ((D_MODEL, D_MODEL), F32), ((1, D_ATTN), F32)])


def _ssm_post_bwd(dysn, glu, sg, ypre, u, wglu_b, sw, dvec, tr):
    def body(d_ref, glu_ref, sg_ref, y_ref, u_ref, w_ref, sw_ref, dv_ref,
             dyp_o, dsg_o, dwg_o, dbg_o, dsw_o, dd_o):
        i = pl.program_id(0)
        glu = glu_ref[...]
        a, b = glu[:, :D_SSM], glu[:, D_SSM:]
        sb = _sigmoid(b)
        ys = a * sb
        sl, dsl = _silu_and_grad(sg_ref[...])
        t = ys * sl
        dt, dsw = _rms_bwd(d_ref[...], t, _rms(t), sw_ref[...])
        _acc(dsw_o, dsw, i == 0)
        dsg_o[...] = dt * ys * dsl
        dys = dt * sl
        dglu = jnp.concatenate([dys * sb, dys * a * sb * (1.0 - sb)], axis=1)
        _acc(dbg_o, jnp.sum(dglu, axis=0, keepdims=True), i == 0)
        gel, dgel = _gelu_and_grad(y_ref[...])
        _acc(dwg_o, _mm_tn(gel, dglu), i == 0)
        dyp = _mm_nt(dglu, w_ref[...]) * dgel
        dyp_o[...] = dyp
        _acc(dd_o, jnp.sum(dyp * u_ref[...], axis=0, keepdims=True), i == 0)

    return _rows_call("ssm_post_bwd", body, tr, [dysn, glu, sg, ypre, u], [wglu_b, sw, dvec],
                      [(512, F32), (512, F32)],
                      [((D_SSM, 2 * D_SSM), F32), ((1, 2 * D_SSM), F32), ((1, D_SSM), F32), ((1, D_SSM), F32)])


def _attn_bwd(qt, k, kt, v, dot, lse_t, delta_t, tk):
    _, nq, _, tq = qt.shape
    lp = k.shape[1]
    nk = lp // tk
    assert lse_t.shape == (HEADS, nq, 1, tq) and delta_t.shape == (HEADS, nq, 1, tq)

    def body(q_ref, k_ref, kt_ref, v_ref, do_ref, lse_ref, dl_ref, dq_o, dk_o, dv_o, dk_s, dv_s):
        @pl.when(pl.program_id(1) == 0)
        def _():
            dq_o[...] = jnp.zeros(dq_o.shape, F32)

        dk_s[...] = jnp.zeros(dk_s.shape, F32)
        dv_s[...] = jnp.zeros(dv_s.shape, F32)
        kk = k_ref[0]
        kkt = kt_ref[0]
        vv = v_ref[0]

        def chunk(c, carry):
            qq = q_ref[0, c]
            dd = do_ref[0, c]
            pt = jnp.exp2(_mm(kk, qq) - lse_ref[0, c])
            dv_s[...] += _mm_nt(dd, pt)
            dst = (pt * (_mm(vv, dd) - dl_ref[0, c])).astype(BF16)
            dk_s[...] += _mm_nt(qq, dst)
            dq_o[0, c] += _mm(kkt, dst)
            return carry

        n_loop = nq - nq % ATTN_BWD_UNROLL
        if n_loop:
            lax.fori_loop(0, n_loop, chunk, 0, unroll=ATTN_BWD_UNROLL)
        for c in range(n_loop, nq):
            chunk(c, 0)
        dk_o[0] = dk_s[...]
        dv_o[0] = dv_s[...]

    head = lambda w: pl.BlockSpec((1, nq, w, tq), lambda h, j: (h, 0, 0, 0))
    rows = lambda w: pl.BlockSpec((1, tk, w), lambda h, j: (h, j, 0))
    cols = lambda w: pl.BlockSpec((1, w, tk), lambda h, j: (h, 0, j))
    return pl.pallas_call(
        body, name="attn_bwd", grid=(HEADS, nk),
        in_specs=[head(QK_DIM), rows(QK_DIM), cols(QK_DIM), rows(V_HEAD), head(V_HEAD), head(1), head(1)],
        out_specs=[head(QK_DIM), cols(QK_DIM), cols(V_HEAD)],
        out_shape=[jax.ShapeDtypeStruct((HEADS, nq, QK_DIM, tq), F32), jax.ShapeDtypeStruct((HEADS, QK_DIM, lp), F32),
                   jax.ShapeDtypeStruct((HEADS, V_HEAD, lp), F32)],
        scratch_shapes=[pltpu.VMEM((QK_DIM, tk), F32), pltpu.VMEM((V_HEAD, tk), F32)],
        compiler_params=_cparams("arbitrary", "arbitrary"))(qt, k, kt, v, dot, lse_t, delta_t)


def _qkv_up_bwd(dqn, dr1, dr2, dkn, dv, dkr8, ql, kvl, cos8, sin8, c32, s32, qw, kvw, wq_b, wkv_b, p32, sum8, tr):
    def body(dqn_ref, dr1_ref, dr2_ref, dkn_ref, dv_ref, dkr_ref, ql_ref, kvl_ref, cos_ref, sin_ref, c32_ref,
             s32_ref, qw_ref, kvw_ref, wq_ref, wkv_ref, p_ref, s8_ref,
             dql_o, dkvl_o, dkrr_o, dwq_o, dwkv_o, dqw_o, dkvw_o):
        i = pl.program_id(0)
        cs, sn = cos_ref[...], sin_ref[...]
        d1, d2 = dr1_ref[...], dr2_ref[...]
        dq = jnp.concatenate([dqn_ref[...], d1 * cs + d2 * sn, d2 * cs - d1 * sn], axis=1) * (Q_SCALE / LOG2E)
        x = ql_ref[...]
        r = _rms(x)
        qw = qw_ref[...]
        _acc(dwq_o, _mm_tn(x * r * qw, dq), i == 0)
        dx, dw = _rms_bwd(_mm_nt(dq, wq_ref[...]), x, r, qw)
        dql_o[...] = dx
        _acc(dqw_o, dw, i == 0)
        dkv = jnp.concatenate([dkn_ref[...] * (1.0 / LOG2E), dv_ref[...]], axis=1)
        x = kvl_ref[...]
        r = _rms(x)
        kvw = kvw_ref[...]
        _acc(dwkv_o, _mm_tn(x * r * kvw, dkv), i == 0)
        dx, dw = _rms_bwd(_mm_nt(dkv, wkv_ref[...]), x, r, kvw)
        dkvl_o[...] = dx
        _acc(dkvw_o, dw, i == 0)
        dkr = _mm_exact(dkr_ref[...], s8_ref[...]) * (1.0 / LOG2E)
        dkrr_o[...] = dkr * c32_ref[...] + _mm_exact(dkr * s32_ref[...], p_ref[...])

    return _rows_call("qkv_up_bwd", body, tr, [dqn, dr1, dr2, dkn, dv, dkr8, ql, kvl, cos8, sin8, c32, s32],
                      [qw, kvw, wq_b, wkv_b, p32, sum8], [(256, F32), (128, F32), (32, F32)],
                      [((Q_LORA, 768), F32), ((KV_LORA, 1024), F32), ((1, Q_LORA), F32), ((1, KV_LORA), F32)])


def _inproj_bwd(dql, dkvl, dag, du_f, du_b, dypre, dsg, dkr, h, dout, pre_w, w_in_b, dvec, tr):
    def body(dql_ref, dkvl_ref, dag_ref, duf_ref, dub_ref, dyp_ref, dsg_ref, dkr_ref, h_ref, dout_ref,
             pw_ref, w_ref, dv_ref, dh_o, dwin_o, dpw_o):
        i = pl.program_id(0)
        du = duf_ref[...] + dub_ref[...] + dv_ref[...] * dyp_ref[...]
        dproj = jnp.concatenate([dql_ref[...], dkvl_ref[...], dag_ref[...], du, dsg_ref[...],
                                 dkr_ref[...], jnp.zeros((tr, D_IN_PAD - D_IN), F32)], axis=1)
        x = h_ref[...]
        r = _rms(x)
        pw = pw_ref[...]
        _acc(dwin_o, _mm_tn(x * r * pw, dproj), i == 0)
        dx, dw = _rms_bwd(_mm_nt(dproj, w_ref[...]), x, r, pw)
        _acc(dpw_o, dw, i == 0)
        dh_o[...] = dout_ref[...] + dx

    return _rows_call("inproj_bwd", body, tr, [dql, dkvl, dag, du_f, du_b, dypre, dsg, dkr, h, dout],
                      [pre_w, w_in_b, dvec], [(1024, F32)], [((D_MODEL, D_IN_PAD), F32), ((1, D_MODEL), F32)])


def _disc_terms(a_re, a_im, ldt):
    dt = jnp.exp(ldt)
    mag = jnp.exp(a_re * dt)
    th = a_im * dt
    cs, sn = jnp.cos(th), jnp.sin(th)
    abar_re, abar_im = mag * cs, mag * sn
    num_re, num_im = abar_re - 1.0, abar_im
    den = a_re * a_re + a_im * a_im
    coef_re = (num_re * a_re + num_im * a_im) / den
    coef_im = (num_im * a_re - num_re * a_im) / den
    return dt, mag, cs, sn, abar_re, abar_im, num_re, num_im, den, coef_re, coef_im


def _ssm_disc(a_re, a_im, ldt, bt_re, bt_im):
    def body(ar_ref, ai_ref, l_ref, br_ref, bi_ref, abr_o, abi_o, bbr_o, bbi_o):
        t = _disc_terms(ar_ref[...], ai_ref[...], l_ref[...])
        abr_o[...] = t[4]
        abi_o[...] = t[5]
        cr, ci = t[9], t[10]
        br, bi = br_ref[...], bi_ref[...]
        bbr_o[...] = cr * br - ci * bi
        bbi_o[...] = cr * bi + ci * br

    ng = a_re.shape[0]
    return pl.pallas_call(
        body, name="ssm_disc",
        out_shape=[jax.ShapeDtypeStruct((ng, 1, SSM_STATE), F32)] * 2
        + [jax.ShapeDtypeStruct((ng, SSM_GROUP, SSM_STATE), F32)] * 2)(a_re, a_im, ldt, bt_re, bt_im)


def _ssm_disc_bwd(a_re, a_im, ldt, bt_re, bt_im, da8_re, da8_im, dbb_re, dbb_im):
    def body(ar_ref, ai_ref, l_ref, br_ref, bi_ref, dar_ref, dai_ref, dbr_ref, dbi_ref,
             gar_o, gai_o, gl_o, gbr_o, gbi_o):
        a_re, a_im = ar_ref[...], ai_ref[...]
        dt, mag, cs, sn, abar_re, abar_im, num_re, num_im, den, cr, ci = _disc_terms(a_re, a_im, l_ref[...])
        br, bi = br_ref[...], bi_ref[...]
        dbr, dbi = dbr_ref[...], dbi_ref[...]
        gbr_o[...] = cr * dbr + ci * dbi
        gbi_o[...] = cr * dbi - ci * dbr
        dcr = jnp.sum(br * dbr + bi * dbi, axis=1, keepdims=True)
        dci = jnp.sum(br * dbi - bi * dbr, axis=1, keepdims=True)
        dnum_re = (dcr * a_re - dci * a_im) / den
        dnum_im = (dcr * a_im + dci * a_re) / den
        dden = -(dcr * cr + dci * ci) / den
        g_are = (dcr * num_re + dci * num_im) / den + dden * 2.0 * a_re
        g_aim = (dcr * num_im - dci * num_re) / den + dden * 2.0 * a_im
        d_abr = jnp.sum(dar_ref[...], axis=1, keepdims=True) + dnum_re
        d_abi = jnp.sum(dai_ref[...], axis=1, keepdims=True) + dnum_im
        dmag = d_abr * cs + d_abi * sn
        dth = d_abi * abar_re - d_abr * abar_im
        g_are = g_are + dmag * mag * dt
        g_aim = g_aim + dth * dt
        ddt = jnp.sum(dmag * mag * a_re + dth * a_im, axis=2, keepdims=True)
        gar_o[...] = g_are
        gai_o[...] = g_aim
        gl_o[...] = ddt * dt

    ng = a_re.shape[0]
    return pl.pallas_call(
        body, name="ssm_disc_bwd",
        out_shape=[jax.ShapeDtypeStruct((ng, 1, SSM_STATE), F32)] * 2 + [jax.ShapeDtypeStruct((ng, 1, 1), F32)]
        + [jax.ShapeDtypeStruct((ng, SSM_GROUP, SSM_STATE), F32)] * 2)(
            a_re, a_im, ldt, bt_re, bt_im, da8_re, da8_im, dbb_re, dbb_im)


def _exchange(name, per_peer, shared):
    rp, rs = per_peer.shape[1], shared.shape[0]
    n_direct = N_DEV - 1

    def body(peer_ref, shared_ref, out_ref, send_sems, recv_sems, local_sems):
        x, y, c = lax.axis_index("x"), lax.axis_index("y"), lax.axis_index("c")
        me = 4 * x + 2 * y + c
        sibling = (x, y, 1 - c)
        chips = [(1 - x, y), (x, 1 - y), (1 - x, 1 - y)]

        def index(px, py, pc):
            return 4 * px + 2 * py + pc

        def remote(src, dst, s, to):
            return pltpu.make_async_remote_copy(src_ref=src, dst_ref=dst, send_sem=send_sems.at[s],
                                                recv_sem=recv_sems.at[s], device_id=to,
                                                device_id_type=pl.DeviceIdType.MESH)

        direct = []
        for k in range(1, N_DEV):
            px = 1 - x if (k >> 2) & 1 else x
            py = 1 - y if (k >> 1) & 1 else y
            pc = 1 - c if k & 1 else c
            direct.append(remote(peer_ref.at[index(px, py, pc)], out_ref.at[me, pl.ds(0, rp), :], k - 1,
                                 (px, py, pc)))

        def block(i):
            return out_ref.at[i, pl.ds(rp, rs), :]

        def relay(k, i, to, src=None):
            return remote(block(i) if src is None else src, block(i), n_direct + k, to)

        mine = [pltpu.make_async_copy(peer_ref.at[me], out_ref.at[me, pl.ds(0, rp), :], local_sems.at[0]),
                pltpu.make_async_copy(shared_ref, block(me), local_sems.at[1])]
        first = [relay(0, me, sibling, src=shared_ref)]
        first += [relay(1 + j, me, (*chip, c), src=shared_ref) for j, chip in enumerate(chips)]
        for cp in mine + direct + first:
            cp.start()
        passed = [relay(4 + j, index(*chip, c), sibling) for j, chip in enumerate(chips)]
        for j, chip in enumerate(chips):
            relay(1 + j, index(*chip, c), (x, y, c)).wait_recv()
            passed[j].start()
        relay(0, index(*sibling), (x, y, c)).wait_recv()
        for j, chip in enumerate(chips):
            relay(4 + j, index(*chip, 1 - c), (x, y, c)).wait_recv()
        for cp in first + passed:
            cp.wait_send()
        for cp in direct + mine:
            cp.wait()

    n_sem = n_direct + N_DEV - 1
    return pl.pallas_call(
        body, name=name, out_shape=jax.ShapeDtypeStruct((N_DEV, rp + rs, LANES), F32),
        in_specs=[pl.BlockSpec(memory_space=pl.ANY)] * 2, out_specs=pl.BlockSpec(memory_space=pl.ANY),
        scratch_shapes=[pltpu.SemaphoreType.DMA((n_sem,)), pltpu.SemaphoreType.DMA((n_sem,)),
                        pltpu.SemaphoreType.DMA((2,))])(per_peer, shared)


def _gather_two_level(name, buf):
    m_per, n = buf.shape

    def body(x_ref, out_ref, send_sems, recv_sems, local_sem):
        x, y, c = lax.axis_index("x"), lax.axis_index("y"), lax.axis_index("c")
        me, sibling = (x, y, c), (x, y, 1 - c)
        chips = [(1 - x, y), (x, 1 - y), (1 - x, 1 - y)]

        def rows(px, py, pc):
            return out_ref.at[pl.ds(pl.multiple_of((4 * px + 2 * py + pc) * m_per, 8), m_per), :]

        def copy(k, block, to, src=None):
            return pltpu.make_async_remote_copy(
                src_ref=rows(*block) if src is None else src, dst_ref=rows(*block), send_sem=send_sems.at[k],
                recv_sem=recv_sems.at[k], device_id=to, device_id_type=pl.DeviceIdType.MESH)

        mine = pltpu.make_async_copy(x_ref, rows(*me), local_sem)
        mine.start()
        first = [copy(0, me, sibling, src=x_ref)]
        first += [copy(1 + j, me, (*chip, c), src=x_ref) for j, chip in enumerate(chips)]
        for cp in first:
            cp.start()
        passed = [copy(4 + j, (*chip, c), sibling) for j, chip in enumerate(chips)]
        for j, chip in enumerate(chips):
            copy(1 + j, (*chip, c), me).wait_recv()
            passed[j].start()
        copy(0, sibling, me).wait_recv()
        for j, chip in enumerate(chips):
            copy(4 + j, (*chip, 1 - c), me).wait_recv()
        for cp in first + passed:
            cp.wait_send()
        mine.wait()

    assert m_per % 8 == 0
    out = pl.pallas_call(
        body, name=name, out_shape=jax.ShapeDtypeStruct((N_DEV * m_per, n), buf.dtype),
        in_specs=[pl.BlockSpec(memory_space=pltpu.VMEM)], out_specs=pl.BlockSpec(memory_space=pltpu.VMEM),
        scratch_shapes=[pltpu.SemaphoreType.DMA((N_DEV - 1,)), pltpu.SemaphoreType.DMA((N_DEV - 1,)),
                        pltpu.SemaphoreType.DMA(())],
        compiler_params=pltpu.CompilerParams(vmem_limit_bytes=VMEM_LIMIT_V7X))(buf)
    return out.reshape(N_DEV, m_per, n)


def _adamw(recv, w, m, v, tr):
    rows = w.shape[0]
    c1 = 1.0 - ADAM_B1 ** ADAM_STEP
    c2 = 1.0 - ADAM_B2 ** ADAM_STEP

    def body(r_ref, w_ref, m_ref, v_ref, g_o, d_o, m_o, v_o):
        g = r_ref[0]
        for k in range(1, N_DEV):
            g = g + r_ref[k]
        mm = ADAM_B1 * m_ref[...] + (1.0 - ADAM_B1) * g
        vv = ADAM_B2 * v_ref[...] + (1.0 - ADAM_B2) * (g * g)
        g_o[...] = g
        m_o[...] = mm
        v_o[...] = vv
        d_o[...] = -ADAM_LR * ((mm / c1) / (jnp.sqrt(vv / c2) + ADAM_EPS) + ADAM_WD * w_ref[...])

    spec = pl.BlockSpec((tr, LANES), lambda i: (i, 0))
    return pl.pallas_call(
        body, name="adamw", grid=(rows // tr,),
        in_specs=[pl.BlockSpec((N_DEV, tr, LANES), lambda i: (0, i, 0)), spec, spec, spec],
        out_specs=[spec] * 4, out_shape=[jax.ShapeDtypeStruct((rows, LANES), F32)] * 4,
        compiler_params=_cparams("arbitrary"))(recv, w, m, v)


def _to_rows(a):
    flat = a.reshape(-1)
    pad = (-flat.shape[0]) % LANES
    if pad:
        flat = jnp.concatenate([flat, jnp.zeros((pad,), flat.dtype)])
    return flat.reshape(-1, LANES)


def _n_rows(shape):
    return -(-int(np.prod(shape)) // LANES)


def _pack(arrays, total_rows):
    rows = [_to_rows(a) for a in arrays]
    used = sum(r.shape[0] for r in rows)
    if total_rows > used:
        rows.append(jnp.zeros((total_rows - used, LANES), F32))
    return jnp.concatenate(rows, axis=0)


def _unpack(buf, shapes):
    lead = buf.shape[:-2]
    out, r0 = [], 0
    for s in shapes:
        n = int(np.prod(s))
        nr = _n_rows(s)
        out.append(buf[..., r0:r0 + nr, :].reshape(lead + (-1,))[..., :n].reshape(lead + tuple(s)))
        r0 += nr
    return out


def _pack_per_device(arrays, total_rows):
    rows = []
    for a in arrays:
        flat = a.reshape(N_DEV, -1)
        pad = (-flat.shape[1]) % LANES
        if pad:
            flat = jnp.concatenate([flat, jnp.zeros((N_DEV, pad), flat.dtype)], axis=1)
        rows.append(flat.reshape(N_DEV, -1, LANES))
    used = sum(r.shape[1] for r in rows)
    if total_rows > used:
        rows.append(jnp.zeros((N_DEV, total_rows - used, LANES), F32))
    return jnp.concatenate(rows, axis=1)


def _shard_views(name, full):
    if name == 'w_out':
        return full.reshape(N_DEV, full.shape[0] // N_DEV, full.shape[1])
    r, ccols = full.shape
    return full.reshape(r, N_DEV, ccols // N_DEV).transpose(1, 0, 2)


def _from_shards(name, stacked):
    if name == 'w_out':
        return stacked.reshape(-1, stacked.shape[-1])
    n, r, cc = stacked.shape
    return stacked.transpose(1, 0, 2).reshape(r, n * cc)


GROUPS_PER_BLK = N_GROUPS // N_COL_BLK


def _block_diag(t):
    eye = jnp.eye(GROUPS_PER_BLK, dtype=t.dtype)
    t4 = t.reshape(N_COL_BLK, GROUPS_PER_BLK, SSM_GROUP, SSM_STATE)
    return (t4[:, :, :, None, :] * eye[None, :, None, :, None]).reshape(N_COL_BLK, CH_BLK, COL_BLK)


def _diag_blocks(mat4):
    eye = jnp.eye(GROUPS_PER_BLK, dtype=mat4.dtype)
    m6 = mat4.reshape(N_COL_BLK, GROUPS_PER_BLK, SSM_GROUP, GROUPS_PER_BLK, SSM_STATE)
    return (m6 * eye[None, :, None, :, None]).sum(axis=3).reshape(N_GROUPS, SSM_GROUP, SSM_STATE)


def _step(x, loss_target, wts, moms, vels):
    seq = x.shape[1]
    n_valid = N_META + seq
    lp = -(-n_valid // 256) * 256
    tr = _pick(lp, [640, 256])
    tr_mid = 256
    tq = _pick(lp, [1280, 256])
    tk = _pick(lp, [640, 256])

    shard_shapes = [wts[n].shape[-2:] for n in SHARDED]
    n_shard_rows = sum(_n_rows(s) for s in shard_shapes)
    gathered = _gather_two_level("gather_weights",
                                 _pack([wts[n].reshape(wts[n].shape[-2:]) for n in SHARDED], n_shard_rows))
    full = {n: _from_shards(n, a) for n, a in zip(SHARDED, _unpack(gathered, shard_shapes))}

    w_in_b = jnp.concatenate([_cols_in(full['w_in']), jnp.zeros((D_MODEL, D_IN_PAD - D_IN), F32)],
                             axis=1).astype(BF16)
    wq_b = _cols_q(full['w_q_up']).astype(BF16)
    wkv_b = _cols_kv(full['w_kv_up']).astype(BF16)
    wglu_b = full['w_glu'].astype(BF16)
    wo_b = full['w_out'].astype(BF16)
    pre_w, post_w = wts['pre_norm_w'], wts['post_norm_w']
    qw, kvw, aw, sw = wts['q_norm_w'], wts['kv_norm_w'], wts['attn_out_norm_w'], wts['ssm_out_norm_w']
    bglu, dvec = wts['b_glu'], wts['ssm_d']

    lseg = lp // N_SEG
    pos = _row_position(jnp.arange(lp, dtype=jnp.int32), lseg)
    inv = ROPE_THETA ** (-jnp.arange(HALF_ROPE, dtype=F32) / HALF_ROPE)
    ang = pos.astype(F32)[:, None] * inv[None, :]
    cos, sin = jnp.cos(ang), jnp.sin(ang)
    cos8, sin8 = jnp.tile(cos, (1, HEADS)), jnp.tile(sin, (1, HEADS))
    c32 = jnp.concatenate([cos, cos], axis=1)
    s32 = jnp.concatenate([-sin, sin], axis=1)
    p32 = jnp.asarray(np.roll(np.eye(QK_ROPE, dtype=np.float32), HALF_ROPE, axis=1))
    sum8 = jnp.asarray(np.tile(np.eye(QK_ROPE, dtype=np.float32), (HEADS, 1)))
    head_sum = jnp.asarray(np.repeat(np.eye(HEADS, dtype=np.float32), V_HEAD, axis=0))

    ng = 2 * N_GROUPS
    a_re3 = wts['ssm_a_re'].reshape(ng, 1, SSM_STATE)
    a_im3 = wts['ssm_a_im'].reshape(ng, 1, SSM_STATE)
    ldt3 = wts['ssm_log_dt'].reshape(ng, 1, 1)
    bt_re = wts['ssm_b_re'].reshape(2, N_GROUPS, SSM_STATE, SSM_GROUP).transpose(0, 1, 3, 2).reshape(
        ng, SSM_GROUP, SSM_STATE)
    bt_im = wts['ssm_b_im'].reshape(2, N_GROUPS, SSM_STATE, SSM_GROUP).transpose(0, 1, 3, 2).reshape(
        ng, SSM_GROUP, SSM_STATE)
    c_re = wts['ssm_c_re'].reshape(ng, SSM_GROUP, SSM_STATE)
    c_im = wts['ssm_c_im'].reshape(ng, SSM_GROUP, SSM_STATE)
    abar_re, abar_im, bbt_re, bbt_im = _ssm_disc(a_re3, a_im3, ldt3, bt_re, bt_im)

    def direction(t, d):
        return t[d * N_GROUPS:(d + 1) * N_GROUPS]

    def slab(t, d, sign=1.0):
        return jnp.broadcast_to(sign * direction(t, d).reshape(1, N_STATES), (N_SEG, N_STATES))

    w_re = [_block_diag(direction(bbt_re, d)).astype(BF16) for d in range(2)]
    w_im = [_block_diag(direction(bbt_im, d)).astype(BF16) for d in range(2)]
    cb_re = [_block_diag(direction(c_re, d)).astype(BF16) for d in range(2)]
    cb_im = [_block_diag(-direction(c_im, d)).astype(BF16) for d in range(2)]

    def to_rows(a):
        return a.reshape(N_SEG, lseg, a.shape[-1]).transpose(1, 0, 2).reshape(lp, a.shape[-1])

    def to_tokens(a):
        return a.reshape(lseg, N_SEG, a.shape[-1]).transpose(1, 0, 2).reshape(lp, a.shape[-1])

    pad = jnp.zeros((lp - n_valid, D_MODEL), F32)
    h = to_rows(jnp.concatenate([full['meta_tokens'], x[0], pad], axis=0))
    tgt = to_rows(jnp.concatenate([jnp.zeros((N_META, D_MODEL), F32), loss_target[0], pad], axis=0))

    ql, kvl, ag, su, sg, kr = _inproj(h, pre_w, w_in_b, tr)
    qn_b, qr1_b, qr2_b, kn_b, v_b, kr_b = _qkv_up(ql, kvl, kr, cos8, sin8, c32, s32, qw, kvw, wq_b, wkv_b, p32, tr)

    def heads(a, w):
        return a.reshape(lp, HEADS, w)

    nq, nk = lp // tq, lp // tk
    q_t = jnp.concatenate([heads(qn_b, 64), heads(qr1_b, 16), heads(qr2_b, 16)], axis=-1)
    k_t = jnp.concatenate([heads(kn_b, 64), jnp.broadcast_to(kr_b[:, None, :], (lp, HEADS, QK_ROPE))], axis=-1)
    v_t = heads(v_b, 64)
    vx_t = jnp.concatenate([v_t, jnp.ones((lp, HEADS, 1), BF16), jnp.zeros((lp, HEADS, LANES - V_HEAD - 1), BF16)],
                           axis=-1)
    qt4 = q_t.reshape(nq, tq, HEADS, QK_DIM).transpose(2, 0, 3, 1)
    tk_fwd = _pick(lp, [1280, 256])
    vxt4 = vx_t.reshape(lp // tk_fwd, tk_fwd, HEADS, LANES).transpose(2, 0, 3, 1)
    k_h = k_t.transpose(1, 0, 2)
    kt_h = k_t.transpose(1, 2, 0)
    v_h = v_t.transpose(1, 0, 2)
    ot_h, lse4 = _attn_fwd(qt4, k_h, vxt4, n_valid)
    o_flat = ot_h.transpose(2, 0, 1).reshape(lp, D_ATTN)

    xs, ys = [], []
    for d in range(2):
        ar8, ai8 = slab(abar_re, d), slab(abar_im, d)
        ere, eim = _scan_ends(f"scan{d}_ends", su, w_re[d], w_im[d], ar8, ai8, d == 0)
        x_re, x_im, y_d = _scan_fwd(f"scan{d}", su, w_re[d], w_im[d], ar8, ai8, ere, eim, cb_re[d], cb_im[d],
                                    d == 0)
        xs += [x_re, x_im]
        ys.append(y_d)
    ypre, glu, ysn = _ssm_post(ys[0], ys[1], su, sg, wglu_b, bglu, sw, dvec, tr)

    dout, do_flat, dag, dysn, delta8, loss, d_post, d_wo, d_aw = _out_fwd_bwd(
        o_flat, ag, ysn, h, tgt, wo_b, post_w, aw, head_sum, n_valid, _pick(lp, [320, 256]))
    dypre, dsg, d_wglu, d_bglu, d_sw, d_dvec = _ssm_post_bwd(dysn, glu, sg, ypre, su, wglu_b, sw, dvec, tr)

    dus, d_ct, d_wb, d_a8 = [], [], [], []
    for d in range(2):
        ar8, ai8c = slab(abar_re, d), slab(abar_im, d, -1.0)
        ere, eim = _scan_ends(f"scan_adj{d}_ends", dypre, cb_re[d], cb_im[d], ar8, ai8c, d != 0)
        du_d, dw_re, dw_im, dc_re, dc_im, da_re, da_im = _scan_bwd(
            f"scan_adj{d}", dypre, cb_re[d], cb_im[d], ar8, ai8c, ere, eim, su, w_re[d], w_im[d],
            xs[2 * d], xs[2 * d + 1], d != 0)
        dus.append(du_d)
        d_ct.append((dc_re, dc_im))
        d_wb.append((dw_re, dw_im))
        d_a8.append((da_re, da_im))

    dot4 = do_flat.astype(BF16).reshape(nq, tq, HEADS, V_HEAD).transpose(2, 0, 3, 1)
    dqt4, dkt_h, dvt_h = _attn_bwd(qt4, k_h, kt_h, v_h, dot4, lse4, delta8.T.reshape(HEADS, nq, 1, tq), tk)
    dq_t = dqt4.transpose(1, 3, 0, 2).reshape(lp, HEADS, QK_DIM)
    dk_t = dkt_h.transpose(2, 0, 1)
    dqn = dq_t[:, :, :64].reshape(lp, 512)
    dr1 = dq_t[:, :, 64:80].reshape(lp, 128)
    dr2 = dq_t[:, :, 80:96].reshape(lp, 128)
    dkn = dk_t[:, :, :64].reshape(lp, 512)
    dkr8 = dk_t[:, :, 64:].reshape(lp, HEADS * QK_ROPE)
    dvf = dvt_h.transpose(2, 0, 1).reshape(lp, 512)
    dql, dkvl, dkrr, d_wq, d_wkv, d_qw, d_kvw = _qkv_up_bwd(
        dqn, dr1, dr2, dkn, dvf, dkr8, ql, kvl, cos8, sin8, c32, s32, qw, kvw, wq_b, wkv_b, p32, sum8, tr)
    dh, d_win, d_pre = _inproj_bwd(dql, dkvl, dag, dus[0], dus[1], dypre, dsg, dkrr, h, dout, pre_w, w_in_b, dvec,
                                   tr_mid)
    dh = to_tokens(dh)

    def seg_sums(t):
        return t.reshape(N_SEG, N_GROUPS, SSM_STATE).transpose(1, 0, 2)

    da8_re = jnp.concatenate([seg_sums(d_a8[d][0]) for d in range(2)], axis=0)
    da8_im = jnp.concatenate([seg_sums(d_a8[d][1]) for d in range(2)], axis=0)
    dbb_re = jnp.concatenate([_diag_blocks(d_wb[d][0]) for d in range(2)], axis=0)
    dbb_im = jnp.concatenate([_diag_blocks(d_wb[d][1]) for d in range(2)], axis=0)
    g_are, g_aim, g_ldt, g_bt_re, g_bt_im = _ssm_disc_bwd(a_re3, a_im3, ldt3, bt_re, bt_im, da8_re, da8_im,
                                                          dbb_re, dbb_im)
    g_c_re = jnp.concatenate([_diag_blocks(d_ct[d][0]) for d in range(2)], axis=0)
    g_c_im = jnp.concatenate([-_diag_blocks(d_ct[d][1]) for d in range(2)], axis=0)

    def b_layout(t):
        return t.reshape(2, N_GROUPS, SSM_GROUP, SSM_STATE).transpose(0, 1, 3, 2)

    local = {
        'meta_tokens': dh[:N_META],
        'pre_norm_w': d_pre, 'post_norm_w': d_post,
        'w_in': _cols_in_inv(d_win[:, :D_IN]),
        'q_norm_w': d_qw, 'w_q_up': _cols_q_inv(d_wq),
        'kv_norm_w': d_kvw, 'w_kv_up': _cols_kv_inv(d_wkv),
        'attn_out_norm_w': d_aw,
        'ssm_a_re': g_are, 'ssm_a_im': g_aim, 'ssm_log_dt': g_ldt,
        'ssm_b_re': b_layout(g_bt_re), 'ssm_b_im': b_layout(g_bt_im), 'ssm_c_re': g_c_re, 'ssm_c_im': g_c_im,
        'ssm_d': d_dvec, 'w_glu': d_wglu, 'b_glu': d_bglu, 'ssm_out_norm_w': d_sw, 'w_out': d_wo,
    }

    replicated = [n for n in WEIGHTS if n not in SHARDED]
    order = SHARDED + replicated
    shapes = [wts[n].shape for n in order] + [(1, 1)]
    tr_adam = 512
    total_rows = -(-sum(_n_rows(s) for s in shapes) // tr_adam) * tr_adam
    recv = _exchange("exchange_grads",
                     _pack_per_device([_shard_views(n, local[n]) for n in SHARDED], n_shard_rows),
                     _pack([local[n] for n in replicated] + [loss], total_rows - n_shard_rows))
    zero = jnp.zeros((1, 1), F32)
    packed = [_pack([src[n] for n in order] + [zero], total_rows) for src in (wts, moms, vels)]
    g_p, d_p, m_p, v_p = _adamw(recv, *packed, tr_adam)
    sums = _unpack(g_p, shapes)
    grads = dict(zip(order, sums))
    deltas, new_m, new_v = (dict(zip(order, _unpack(b, shapes))) for b in (d_p, m_p, v_p))

    grad_x = dh[N_META:n_valid][None]
    return (sums[-1][0, 0], grad_x, *[grads[n] for n in WEIGHTS], *[deltas[n] for n in WEIGHTS],
            *[new_m[n] for n in WEIGHTS], *[new_v[n] for n in WEIGHTS])


def kernel(x, meta_tokens, pre_norm_w, post_norm_w, w_in, q_norm_w, w_q_up, kv_norm_w, w_kv_up, attn_out_norm_w, ssm_a_re, ssm_a_im, ssm_log_dt, ssm_b_re, ssm_b_im, ssm_c_re, ssm_c_im, ssm_d, w_glu, b_glu, ssm_out_norm_w, w_out, loss_target, m_meta_tokens, m_pre_norm_w, m_post_norm_w, m_w_in, m_q_norm_w, m_w_q_up, m_kv_norm_w, m_w_kv_up, m_attn_out_norm_w, m_ssm_a_re, m_ssm_a_im, m_ssm_log_dt, m_ssm_b_re, m_ssm_b_im, m_ssm_c_re, m_ssm_c_im, m_ssm_d, m_w_glu, m_b_glu, m_ssm_out_norm_w, m_w_out, v_meta_tokens, v_pre_norm_w, v_post_norm_w, v_w_in, v_q_norm_w, v_w_q_up, v_kv_norm_w, v_w_kv_up, v_attn_out_norm_w, v_ssm_a_re, v_ssm_a_im, v_ssm_log_dt, v_ssm_b_re, v_ssm_b_im, v_ssm_c_re, v_ssm_c_im, v_ssm_d, v_w_glu, v_b_glu, v_ssm_out_norm_w, v_w_out):
    wts = dict(zip(WEIGHTS, (meta_tokens, pre_norm_w, post_norm_w, w_in, q_norm_w, w_q_up, kv_norm_w, w_kv_up,
                             attn_out_norm_w, ssm_a_re, ssm_a_im, ssm_log_dt, ssm_b_re, ssm_b_im, ssm_c_re,
                             ssm_c_im, ssm_d, w_glu, b_glu, ssm_out_norm_w, w_out)))
    moms = dict(zip(WEIGHTS, (m_meta_tokens, m_pre_norm_w, m_post_norm_w, m_w_in, m_q_norm_w, m_w_q_up,
                              m_kv_norm_w, m_w_kv_up, m_attn_out_norm_w, m_ssm_a_re, m_ssm_a_im, m_ssm_log_dt,
                              m_ssm_b_re, m_ssm_b_im, m_ssm_c_re, m_ssm_c_im, m_ssm_d, m_w_glu, m_b_glu,
                              m_ssm_out_norm_w, m_w_out)))
    vels = dict(zip(WEIGHTS, (v_meta_tokens, v_pre_norm_w, v_post_norm_w, v_w_in, v_q_norm_w, v_w_q_up,
                              v_kv_norm_w, v_w_kv_up, v_attn_out_norm_w, v_ssm_a_re, v_ssm_a_im, v_ssm_log_dt,
                              v_ssm_b_re, v_ssm_b_im, v_ssm_c_re, v_ssm_c_im, v_ssm_d, v_w_glu, v_b_glu,
                              v_ssm_out_norm_w, v_w_out)))
    return _step(x, loss_target, wts, moms, vels)
```

```python
import functools
import math

import numpy as np
import jax
import jax.numpy as jnp
from jax import lax
from jax.experimental import pallas as pl
from jax.experimental.pallas import tpu as pltpu

F32 = jnp.float32
BF16 = jnp.bfloat16

D_MODEL = 1024
N_META = 16
EPS = 1e-6
HEADS = 8
QK_NOPE = 64
QK_ROPE = 32
HALF_ROPE = QK_ROPE // 2
QK_DIM = QK_NOPE + QK_ROPE
V_HEAD = 64
Q_LORA = 256
KV_LORA = 128
D_ATTN = HEADS * V_HEAD
D_SSM = 512
SSM_GROUP = 16
N_GROUPS = D_SSM // SSM_GROUP
SSM_STATE = 64
N_STATES = N_GROUPS * SSM_STATE
ROPE_THETA = 10000.0
D_IN = Q_LORA + KV_LORA + QK_ROPE + D_ATTN + 2 * D_SSM
D_IN_PAD = 2048
N_DEV = 8
N_SEG = 8
COL_BLK = 512
LANES = 128

ADAM_LR = 0.001
ADAM_B1 = 0.9
ADAM_B2 = 0.999
ADAM_EPS = 1e-08
ADAM_WD = 0.01
ADAM_STEP = 10

VMEM_LIMIT_V7X = 56 * 1024 * 1024
LOG2E = 1.0 / math.log(2.0)
Q_SCALE = LOG2E / math.sqrt(QK_DIM)
ATTN_UNROLL = 4
ATTN_BWD_UNROLL = 4

WEIGHTS = ['meta_tokens', 'pre_norm_w', 'post_norm_w', 'w_in', 'q_norm_w', 'w_q_up', 'kv_norm_w', 'w_kv_up',
           'attn_out_norm_w', 'ssm_a_re', 'ssm_a_im', 'ssm_log_dt', 'ssm_b_re', 'ssm_b_im', 'ssm_c_re', 'ssm_c_im',
           'ssm_d', 'w_glu', 'b_glu', 'ssm_out_norm_w', 'w_out']
SHARDED = ['w_in', 'w_q_up', 'w_kv_up', 'w_glu', 'w_out', 'meta_tokens']

def _cols_in(w):
    return jnp.concatenate([w[:, 0:384], w[:, 416:D_IN], w[:, 384:416]], axis=1)


def _cols_in_inv(w):
    return jnp.concatenate([w[:, 0:384], w[:, D_IN - QK_ROPE:D_IN], w[:, 384:D_IN - QK_ROPE]], axis=1)


def _cols_q(w):
    t = w.reshape(w.shape[0], HEADS, QK_DIM)
    return jnp.concatenate([t[:, :, 0:64].reshape(-1, 512), t[:, :, 64:80].reshape(-1, 128),
                            t[:, :, 80:96].reshape(-1, 128)], axis=1)


def _cols_q_inv(w):
    r = w.shape[0]
    return jnp.concatenate([w[:, 0:512].reshape(r, HEADS, 64), w[:, 512:640].reshape(r, HEADS, 16),
                            w[:, 640:768].reshape(r, HEADS, 16)], axis=2).reshape(r, HEADS * QK_DIM)


def _cols_kv(w):
    t = w.reshape(w.shape[0], HEADS, 128)
    return jnp.concatenate([t[:, :, 0:64].reshape(-1, 512), t[:, :, 64:128].reshape(-1, 512)], axis=1)


def _cols_kv_inv(w):
    r = w.shape[0]
    return jnp.concatenate([w[:, 0:512].reshape(r, HEADS, 64), w[:, 512:1024].reshape(r, HEADS, 64)],
                           axis=2).reshape(r, HEADS * 128)


def _pick(n, cands):
    for c in cands:
        if n % c == 0:
            return c
    raise ValueError(f"no tile for {n}")


def _cparams(*sem):
    return pltpu.CompilerParams(dimension_semantics=sem, vmem_limit_bytes=VMEM_LIMIT_V7X)


def _mm(a, b):
    return jnp.dot(a.astype(BF16), b.astype(BF16), preferred_element_type=F32)


def _mm_nt(a, b):
    return lax.dot_general(a.astype(BF16), b.astype(BF16), (((1,), (1,)), ((), ())), preferred_element_type=F32)


def _mm_tn(a, b):
    return lax.dot_general(a.astype(BF16), b.astype(BF16), (((0,), (0,)), ((), ())), preferred_element_type=F32)


def _mm_exact(a, b):
    return jnp.dot(a, b, precision=lax.Precision.HIGHEST, preferred_element_type=F32)


def _rms(x):
    return lax.rsqrt(jnp.mean(x * x, axis=-1, keepdims=True) + EPS)


def _rms_bwd(dy, x, r, w):
    xh = x * r
    g = dy * w
    dx = r * (g - xh * jnp.mean(g * xh, axis=-1, keepdims=True))
    dw = jnp.sum(dy * xh, axis=0, keepdims=True)
    return dx, dw


def _sigmoid(z):
    return 1.0 / (1.0 + jnp.exp(-z))


def _silu_and_grad(z):
    s = _sigmoid(z)
    return z * s, s * (1.0 + z * (1.0 - s))


_GELU_C = math.sqrt(2.0 / math.pi)


def _gelu_and_grad(x):
    x2 = x * x
    t = jnp.tanh(_GELU_C * (x + 0.044715 * x * x2))
    val = 0.5 * x * (1.0 + t)
    grad = 0.5 * (1.0 + t) + 0.5 * x * (1.0 - t * t) * _GELU_C * (1.0 + 3.0 * 0.044715 * x2)
    return val, grad


def _acc(ref, val, first):
    @pl.when(first)
    def _():
        ref[...] = val

    @pl.when(jnp.logical_not(first))
    def _():
        ref[...] += val


def _rows_call(name, body, tr, row_ins, full_ins, row_outs, acc_outs):
    lp = row_ins[0].shape[0]
    in_specs = [pl.BlockSpec((tr, a.shape[1]), lambda i: (i, 0)) for a in row_ins]
    in_specs += [pl.BlockSpec(a.shape, lambda i, n=a.ndim: (0,) * n) for a in full_ins]
    out_specs = [pl.BlockSpec((tr, c), lambda i: (i, 0)) for c, _ in row_outs]
    out_specs += [pl.BlockSpec(s, lambda i, n=len(s): (0,) * n) for s, _ in acc_outs]
    out_shape = [jax.ShapeDtypeStruct((lp, c), dt) for c, dt in row_outs]
    out_shape += [jax.ShapeDtypeStruct(s, dt) for s, dt in acc_outs]
    return pl.pallas_call(
        body, name=name, grid=(lp // tr,), in_specs=in_specs, out_specs=out_specs, out_shape=out_shape,
        compiler_params=_cparams("arbitrary"))(*row_ins, *full_ins)


def _inproj(h, pre_w, w_in_b, tr):
    def body(h_ref, pw_ref, w_ref, ql, kvl, ag, su, sg, kr):
        x = h_ref[...]
        xn = x * _rms(x) * pw_ref[...]
        pr = _mm(xn, w_ref[...])
        ql[...] = pr[:, 0:256]
        kvl[...] = pr[:, 256:384]
        ag[...] = pr[:, 384:896]
        su[...] = pr[:, 896:1408]
        sg[...] = pr[:, 1408:1920]
        kr[...] = pr[:, 1920:1952]

    return _rows_call("inproj", body, tr, [h], [pre_w, w_in_b],
                      [(256, F32), (128, F32), (512, F32), (512, F32), (512, F32), (32, F32)], [])


def _qkv_up(ql, kvl, kr, cos8, sin8, c32, s32, qw, kvw, wq_b, wkv_b, p32, tr):
    def body(ql_ref, kvl_ref, kr_ref, cos_ref, sin_ref, c32_ref, s32_ref, qw_ref, kvw_ref, wq_ref, wkv_ref, p_ref,
             qn_o, qr1_o, qr2_o, kn_o, v_o, kr_o):
        x = ql_ref[...]
        q = _mm(x * _rms(x) * qw_ref[...], wq_ref[...]) * Q_SCALE
        r1, r2 = q[:, 512:640], q[:, 640:768]
        cs, sn = cos_ref[...], sin_ref[...]
        qn_o[...] = q[:, 0:512].astype(BF16)
        qr1_o[...] = (r1 * cs - r2 * sn).astype(BF16)
        qr2_o[...] = (r2 * cs + r1 * sn).astype(BF16)
        x = kvl_ref[...]
        kv = _mm(x * _rms(x) * kvw_ref[...], wkv_ref[...])
        kn_o[...] = kv[:, 0:512].astype(BF16)
        v_o[...] = kv[:, 512:1024].astype(BF16)
        x = kr_ref[...]
        kr_o[...] = (x * c32_ref[...] + _mm_exact(x, p_ref[...]) * s32_ref[...]).astype(BF16)

    return _rows_call("qkv_up", body, tr, [ql, kvl, kr, cos8, sin8, c32, s32], [qw, kvw, wq_b, wkv_b, p32],
                      [(512, BF16), (128, BF16), (128, BF16), (512, BF16), (512, BF16), (32, BF16)], [])


def _row_position(row, lseg):
    return (row & (N_SEG - 1)) * lseg + (row >> 3)


def _first_padded_tile(n_valid, lp, tile):
    lseg = lp // N_SEG
    t0 = n_valid - (N_SEG - 1) * lseg
    return (t0 * N_SEG + N_SEG - 1) // tile if n_valid < lp else lp // tile


def _attn_fwd(qt, k, vxt, n_valid):
    _, nq, _, tq = qt.shape
    _, nk, _, tk = vxt.shape
    lp = k.shape[1]
    lseg = lp // N_SEG
    n_plain = max(0, min(nk, _first_padded_tile(n_valid, lp, tk)))

    def body(q_ref, k_ref, v_ref, o_ref, lse_ref, m_s, acc_s):
        m_s[...] = jnp.full(m_s.shape, -1e30, F32)
        acc_s[...] = jnp.zeros(acc_s.shape, F32)
        qq = q_ref[0, 0]

        def chunk(c, padded):
            r0 = pl.multiple_of(c * tk, tk)
            st = _mm(k_ref[0, pl.ds(r0, tk), :], qq)
            if padded:
                row = r0 + lax.broadcasted_iota(jnp.int32, (tk, tq), 0)
                st = jnp.where(_row_position(row, lseg) < n_valid, st, -1e30)
            m_old = m_s[...]
            m_new = jnp.maximum(m_old, jnp.max(st, axis=0, keepdims=True))
            pt = jnp.exp2(st - m_new)
            acc_s[...] = jnp.exp2(m_old - m_new) * acc_s[...] + _mm(v_ref[0, c], pt)
            m_s[...] = m_new

        def plain(c, carry):
            chunk(c, False)
            return carry

        n_loop = n_plain - n_plain % ATTN_UNROLL
        if n_loop:
            lax.fori_loop(0, n_loop, plain, 0, unroll=ATTN_UNROLL)
        for c in range(n_loop, nk):
            chunk(c, c >= n_plain)
        acc = acc_s[...]
        l = acc[V_HEAD:V_HEAD + 1, :]
        o_ref[0] = acc[:V_HEAD, :] / l
        lse_ref[0, 0] = m_s[...] + jnp.log2(l)

    return pl.pallas_call(
        body, name="attn_fwd", grid=(HEADS, nq),
        in_specs=[pl.BlockSpec((1, 1, QK_DIM, tq), lambda h, i: (h, i, 0, 0)),
                  pl.BlockSpec((1, lp, QK_DIM), lambda h, i: (h, 0, 0)),
                  pl.BlockSpec((1, nk, LANES, tk), lambda h, i: (h, 0, 0, 0))],
        out_specs=[pl.BlockSpec((1, V_HEAD, tq), lambda h, i: (h, 0, i)),
                   pl.BlockSpec((1, 1, 1, tq), lambda h, i: (h, i, 0, 0))],
        out_shape=[jax.ShapeDtypeStruct((HEADS, V_HEAD, lp), F32), jax.ShapeDtypeStruct((HEADS, nq, 1, tq), F32)],
        scratch_shapes=[pltpu.VMEM((1, tq), F32), pltpu.VMEM((LANES, tq), F32)],
        compiler_params=_cparams("arbitrary", "arbitrary"))(qt, k, vxt)


def _scan_tiles(lp, longest):
    lseg = lp // N_SEG
    tt = _pick(lseg, [longest, 48, 32, 16, 8, 4, 2, 1])
    return lseg, tt, lseg // tt


def _cmul(ar, ai, br, bi):
    return ar * br - ai * bi, ar * bi + ai * br


SCAN_STEPS_PER_ITER = 4
N_COL_BLK = N_STATES // COL_BLK
CH_BLK = D_SSM // N_COL_BLK


def _scan_steps(tt, forward, bre_ref, bim_ref, ar, ai, carry, visit):
    def step(s, c):
        r0 = pl.multiple_of((s if forward else tt - 1 - s) * N_SEG, N_SEG)
        pr, pi = _cmul(ar, ai, c[0], c[1])
        xr = pr + bre_ref[pl.ds(r0, N_SEG), :]
        xi = pi + bim_ref[pl.ds(r0, N_SEG), :]
        return (xr, xi) + tuple(visit(r0, (xr, xi), (c[0], c[1]), c[2:]))

    per = SCAN_STEPS_PER_ITER if tt % SCAN_STEPS_PER_ITER == 0 else 1

    def steps(it, c):
        for u in range(per):
            c = step(it * per + u, c)
        return c

    return lax.fori_loop(0, tt // per, steps, carry)


def _segment_starts(lseg, forward, ar, ai, ere_ref, eim_ref, s_re, s_im):
    a1r, a1i = ar[0:1, :], ai[0:1, :]
    pr, pi = jnp.ones_like(a1r), jnp.zeros_like(a1i)
    br, bi = a1r, a1i
    n = lseg
    while n:
        if n & 1:
            pr, pi = _cmul(pr, pi, br, bi)
        n >>= 1
        if n:
            br, bi = _cmul(br, bi, br, bi)
    cr, ci = jnp.zeros_like(a1r), jnp.zeros_like(a1i)
    for j in (range(N_SEG) if forward else range(N_SEG - 1, -1, -1)):
        s_re[j:j + 1, :] = cr
        s_im[j:j + 1, :] = ci
        nr, ni = _cmul(pr, pi, cr, ci)
        cr = nr + ere_ref[j:j + 1, :]
        ci = ni + eim_ref[j:j + 1, :]


def _scan_specs(lp, forward, longest=208):
    lseg, tt, nt = _scan_tiles(lp, longest)

    def tile(t):
        return t if forward else nt - 1 - t

    rows = lambda w: pl.BlockSpec((tt * N_SEG, w), lambda cb, t: (tile(t), cb))
    proj = pl.BlockSpec((1, CH_BLK, COL_BLK), lambda cb, t: (cb, 0, 0))
    slab = pl.BlockSpec((N_SEG, COL_BLK), lambda cb, t: (0, cb))
    return lseg, tt, nt, rows, proj, slab


def _scan_ends(name, urows, wre4, wim4, ar8, ai8, forward):
    lp = urows.shape[0]
    lseg, tt, nt, rows, proj, slab = _scan_specs(lp, forward, longest=520)

    def body(u_ref, wre_ref, wim_ref, ar_ref, ai_ref, ere_o, eim_o, bre_s, bim_s, cr_s, ci_s):
        t = pl.program_id(1)

        @pl.when(t == 0)
        def _():
            cr_s[...] = jnp.zeros(cr_s.shape, F32)
            ci_s[...] = jnp.zeros(ci_s.shape, F32)

        u = u_ref[...]
        bre_s[...] = _mm(u, wre_ref[0])
        bim_s[...] = _mm(u, wim_ref[0])
        cr, ci = _scan_steps(tt, forward, bre_s, bim_s, ar_ref[...], ai_ref[...], (cr_s[...], ci_s[...]),
                             lambda r0, x, x_prev, extra: ())
        cr_s[...] = cr
        ci_s[...] = ci

        @pl.when(t == nt - 1)
        def _():
            ere_o[...] = cr
            eim_o[...] = ci

    return pl.pallas_call(
        body, name=name, grid=(N_COL_BLK, nt), in_specs=[rows(CH_BLK), proj, proj, slab, slab],
        out_specs=[slab, slab], out_shape=[jax.ShapeDtypeStruct((N_SEG, N_STATES), F32)] * 2,
        scratch_shapes=[pltpu.VMEM((tt * N_SEG, COL_BLK), F32)] * 2 + [pltpu.VMEM((N_SEG, COL_BLK), F32)] * 2,
        compiler_params=_cparams("arbitrary", "arbitrary"))(urows, wre4, wim4, ar8, ai8)


def _scan_fwd(name, urows, wre4, wim4, ar8, ai8, ere, eim, cre4, cim4, forward):
    lp = urows.shape[0]
    lseg, tt, nt, rows, proj, slab = _scan_specs(lp, forward)

    def body(u_ref, wre_ref, wim_ref, ar_ref, ai_ref, ere_ref, eim_ref, cre_ref, cim_ref,
             xre_o, xim_o, y_o, bre_s, bim_s, cr_s, ci_s):
        ar, ai = ar_ref[...], ai_ref[...]

        @pl.when(pl.program_id(1) == 0)
        def _():
            _segment_starts(lseg, forward, ar, ai, ere_ref, eim_ref, cr_s, ci_s)

        u = u_ref[...]
        bre_s[...] = _mm(u, wre_ref[0])
        bim_s[...] = _mm(u, wim_ref[0])

        def visit(r0, x, x_prev, extra):
            xre_o[pl.ds(r0, N_SEG), :] = x[0]
            xim_o[pl.ds(r0, N_SEG), :] = x[1]
            return ()

        cr, ci = _scan_steps(tt, forward, bre_s, bim_s, ar, ai, (cr_s[...], ci_s[...]), visit)
        cr_s[...] = cr
        ci_s[...] = ci
        y_o[...] = _mm_nt(xre_o[...], cre_ref[0]) + _mm_nt(xim_o[...], cim_ref[0])

    return pl.pallas_call(
        body, name=name, grid=(N_COL_BLK, nt),
        in_specs=[rows(CH_BLK), proj, proj, slab, slab, slab, slab, proj, proj],
        out_specs=[rows(COL_BLK), rows(COL_BLK), rows(CH_BLK)],
        out_shape=[jax.ShapeDtypeStruct((lp, N_STATES), F32)] * 2 + [jax.ShapeDtypeStruct((lp, D_SSM), F32)],
        scratch_shapes=[pltpu.VMEM((tt * N_SEG, COL_BLK), F32)] * 2 + [pltpu.VMEM((N_SEG, COL_BLK), F32)] * 2,
        compiler_params=_cparams("arbitrary", "arbitrary"))(urows, wre4, wim4, ar8, ai8, ere, eim, cre4, cim4)


def _scan_bwd(name, dyrows, cre4, cim4, ar8, ai8, ere, eim, urows, wre4, wim4, xre, xim, forward):
    lp = urows.shape[0]
    lseg, tt, nt, rows, proj, slab = _scan_specs(lp, forward)

    def body(dy_ref, cre_ref, cim_ref, ar_ref, ai_ref, ere_ref, eim_ref, u_ref, wre_ref, wim_ref, xre_ref, xim_ref,
             du_o, dwre_o, dwim_o, dcre_o, dcim_o, dare_o, daim_o, bre_s, bim_s, gre_s, gim_s, cr_s, ci_s):
        t = pl.program_id(1)
        ar, ai = ar_ref[...], ai_ref[...]

        @pl.when(t == 0)
        def _():
            _segment_starts(lseg, forward, ar, ai, ere_ref, eim_ref, cr_s, ci_s)
            dare_o[...] = jnp.zeros(dare_o.shape, F32)
            daim_o[...] = jnp.zeros(daim_o.shape, F32)

        dy = dy_ref[...]
        bre_s[...] = _mm(dy, cre_ref[0])
        bim_s[...] = _mm(dy, cim_ref[0])

        def visit(r0, g, g_prev, sums):
            gre_s[pl.ds(r0, N_SEG), :] = g[0]
            gim_s[pl.ds(r0, N_SEG), :] = g[1]
            fr = xre_ref[pl.ds(r0, N_SEG), :]
            fi = xim_ref[pl.ds(r0, N_SEG), :]
            pr, pi = g_prev
            return sums[0] + fr * pr + fi * pi, sums[1] + fr * pi - fi * pr

        out = _scan_steps(tt, forward, bre_s, bim_s, ar, ai, (cr_s[...], ci_s[...], dare_o[...], daim_o[...]), visit)
        cr_s[...] = out[0]
        ci_s[...] = out[1]
        dare_o[...] = out[2]
        daim_o[...] = out[3]
        gre, gim = gre_s[...], gim_s[...]
        du_o[...] = _mm_nt(gre, wre_ref[0]) + _mm_nt(gim, wim_ref[0])
        u = u_ref[...]
        first = t == 0
        _acc(dwre_o, _mm_tn(u, gre)[None], first)
        _acc(dwim_o, _mm_tn(u, gim)[None], first)
        _acc(dcre_o, _mm_tn(dy, xre_ref[...])[None], first)
        _acc(dcim_o, _mm_tn(dy, xim_ref[...])[None], first)

    big = pltpu.VMEM((tt * N_SEG, COL_BLK), F32)
    small = pltpu.VMEM((N_SEG, COL_BLK), F32)
    return pl.pallas_call(
        body, name=name, grid=(N_COL_BLK, nt),
        in_specs=[rows(CH_BLK), proj, proj, slab, slab, slab, slab, rows(CH_BLK), proj, proj,
                  rows(COL_BLK), rows(COL_BLK)],
        out_specs=[rows(CH_BLK), proj, proj, proj, proj, slab, slab],
        out_shape=[jax.ShapeDtypeStruct((lp, D_SSM), F32)]
        + [jax.ShapeDtypeStruct((N_COL_BLK, CH_BLK, COL_BLK), F32)] * 4
        + [jax.ShapeDtypeStruct((N_SEG, N_STATES), F32)] * 2,
        scratch_shapes=[big, big, big, big, small, small],
        compiler_params=_cparams("arbitrary", "arbitrary"))(
            dyrows, cre4, cim4, ar8, ai8, ere, eim, urows, wre4, wim4, xre, xim)


def _ssm_post(yf, yb, u, sg, wglu_b, bglu, sw, dvec, tr):
    def body(yf_ref, yb_ref, u_ref, g_ref, w_ref, b_ref, sw_ref, d_ref, ypre_o, glu_o, ysn_o):
        ypre = yf_ref[...] + yb_ref[...] + d_ref[...] * u_ref[...]
        ypre_o[...] = ypre
        glu = _mm(_gelu_and_grad(ypre)[0], w_ref[...]) + b_ref[...]
        glu_o[...] = glu
        t = glu[:, :D_SSM] * _sigmoid(glu[:, D_SSM:]) * _silu_and_grad(g_ref[...])[0]
        ysn_o[...] = t * _rms(t) * sw_ref[...]

    return _rows_call("ssm_post", body, tr, [yf, yb, u, sg], [wglu_b, bglu, sw, dvec],
                      [(512, F32), (1024, F32), (512, F32)], [])


def _attn_gate_norm(o, gate, w):
    sl, dsl = _silu_and_grad(gate)
    t = o * sl
    r = _rms(t)
    return t * r * w, t, r, sl, dsl


def _out_fwd_bwd(o_flat, ag, ysn, h, tgt, wo_b, post_w, aw, head_sum, n_valid, tr):
    lseg = h.shape[0] // N_SEG

    def body(o_ref, g_ref, ys_ref, h_ref, t_ref, w_ref, pw_ref, aw_ref, hs_ref,
             dout_o, do_o, dag_o, dysn_o, dl_o, loss_o, dpw_o, dwo_o, daw_o):
        i = pl.program_id(0)
        o = o_ref[...]
        aw = aw_ref[...]
        ya, t, rt, sl, dsl = _attn_gate_norm(o, g_ref[...], aw)
        cat = jnp.concatenate([ya, ys_ref[...]], axis=1)
        y = _mm(cat, w_ref[...])
        r = _rms(y)
        pw = pw_ref[...]
        out = h_ref[...] + y * r * pw
        pos = _row_position(i * tr + lax.broadcasted_iota(jnp.int32, (tr, 1), 0), lseg)
        valid = jnp.logical_and(pos >= N_META, pos < n_valid)
        diff = jnp.where(valid, out - t_ref[...], 0.0)
        dout = diff * (1.0 / D_MODEL)
        dout_o[...] = dout
        _acc(loss_o, 0.5 * jnp.sum(jnp.sum(diff * diff, axis=1, keepdims=True), axis=0, keepdims=True)
             * (1.0 / D_MODEL), i == 0)
        dy, dpw = _rms_bwd(dout, y, r, pw)
        _acc(dpw_o, dpw, i == 0)
        dcat = _mm_nt(dy, w_ref[...])
        _acc(dwo_o, _mm_tn(cat, dy), i == 0)
        dysn_o[...] = dcat[:, D_ATTN:]
        dt, daw = _rms_bwd(dcat[:, :D_ATTN], t, rt, aw)
        _acc(daw_o, daw, i == 0)
        do = dt * sl
        do_o[...] = do
        dag_o[...] = dt * o * dsl
        dl_o[...] = _mm_exact(do * o, hs_ref[...])

    return _rows_call("out_fwd_bwd", body, tr, [o_flat, ag, ysn, h, tgt], [wo_b, post_w, aw, head_sum],
                      [(1024, F32), (512, F32), (512, F32), (512, F32), (HEADS, F32)],
                      [((1, 1), F32), ((1, D_MODEL), F32), ((D_MODEL, D_MODEL), F32), ((1, D_ATTN), F32)])


def _ssm_post_bwd(dysn, glu, sg, ypre, u, wglu_b, sw, dvec, tr):
    def body(d_ref, glu_ref, sg_ref, y_ref, u_ref, w_ref, sw_ref, dv_ref,
             dyp_o, dsg_o, dwg_o, dbg_o, dsw_o, dd_o):
        i = pl.program_id(0)
        glu = glu_ref[...]
        a, b = glu[:, :D_SSM], glu[:, D_SSM:]
        sb = _sigmoid(b)
        ys = a * sb
        sl, dsl = _silu_and_grad(sg_ref[...])
        t = ys * sl
        dt, dsw = _rms_bwd(d_ref[...], t, _rms(t), sw_ref[...])
        _acc(dsw_o, dsw, i == 0)
        dsg_o[...] = dt * ys * dsl
        dys = dt * sl
        dglu = jnp.concatenate([dys * sb, dys * a * sb * (1.0 - sb)], axis=1)
        _acc(dbg_o, jnp.sum(dglu, axis=0, keepdims=True), i == 0)
        gel, dgel = _gelu_and_grad(y_ref[...])
        _acc(dwg_o, _mm_tn(gel, dglu), i == 0)
        dyp = _mm_nt(dglu, w_ref[...]) * dgel
        dyp_o[...] = dyp
        _acc(dd_o, jnp.sum(dyp * u_ref[...], axis=0, keepdims=True), i == 0)

    return _rows_call("ssm_post_bwd", body, tr, [dysn, glu, sg, ypre, u], [wglu_b, sw, dvec],
                      [(512, F32), (512, F32)],
                      [((D_SSM, 2 * D_SSM), F32), ((1, 2 * D_SSM), F32), ((1, D_SSM), F32), ((1, D_SSM), F32)])


def _attn_bwd(qt, k, kt, v, dot, lse_t, delta_t, tk):
    _, nq, _, tq = qt.shape
    lp = k.shape[1]
    nk = lp // tk
    assert lse_t.shape == (HEADS, nq, 1, tq) and delta_t.shape == (HEADS, nq, 1, tq)

    def body(q_ref, k_ref, kt_ref, v_ref, do_ref, lse_ref, dl_ref, dq_o, dk_o, dv_o, dk_s, dv_s):
        @pl.when(pl.program_id(1) == 0)
        def _():
            dq_o[...] = jnp.zeros(dq_o.shape, F32)

        dk_s[...] = jnp.zeros(dk_s.shape, F32)
        dv_s[...] = jnp.zeros(dv_s.shape, F32)
        kk = k_ref[0]
        kkt = kt_ref[0]
        vv = v_ref[0]

        def chunk(c, carry):
            qq = q_ref[0, c]
            dd = do_ref[0, c]
            pt = jnp.exp2(_mm(kk, qq) - lse_ref[0, c])
            dv_s[...] += _mm_nt(dd, pt)
            dst = (pt * (_mm(vv, dd) - dl_ref[0, c])).astype(BF16)
            dk_s[...] += _mm_nt(qq, dst)
            dq_o[0, c] += _mm(kkt, dst)
            return carry

        n_loop = nq - nq % ATTN_BWD_UNROLL
        if n_loop:
            lax.fori_loop(0, n_loop, chunk, 0, unroll=ATTN_BWD_UNROLL)
        for c in range(n_loop, nq):
            chunk(c, 0)
        dk_o[0] = dk_s[...]
        dv_o[0] = dv_s[...]

    head = lambda w: pl.BlockSpec((1, nq, w, tq), lambda h, j: (h, 0, 0, 0))
    rows = lambda w: pl.BlockSpec((1, tk, w), lambda h, j: (h, j, 0))
    cols = lambda w: pl.BlockSpec((1, w, tk), lambda h, j: (h, 0, j))
    return pl.pallas_call(
        body, name="attn_bwd", grid=(HEADS, nk),
        in_specs=[head(QK_DIM), rows(QK_DIM), cols(QK_DIM), rows(V_HEAD), head(V_HEAD), head(1), head(1)],
        out_specs=[head(QK_DIM), cols(QK_DIM), cols(V_HEAD)],
        out_shape=[jax.ShapeDtypeStruct((HEADS, nq, QK_DIM, tq), F32), jax.ShapeDtypeStruct((HEADS, QK_DIM, lp), F32),
                   jax.ShapeDtypeStruct((HEADS, V_HEAD, lp), F32)],
        scratch_shapes=[pltpu.VMEM((QK_DIM, tk), F32), pltpu.VMEM((V_HEAD, tk), F32)],
        compiler_params=_cparams("arbitrary", "arbitrary"))(qt, k, kt, v, dot, lse_t, delta_t)


def _qkv_up_bwd(dqn, dr1, dr2, dkn, dv, dkr8, ql, kvl, cos8, sin8, c32, s32, qw, kvw, wq_b, wkv_b, p32, sum8, tr):
    def body(dqn_ref, dr1_ref, dr2_ref, dkn_ref, dv_ref, dkr_ref, ql_ref, kvl_ref, cos_ref, sin_ref, c32_ref,
             s32_ref, qw_ref, kvw_ref, wq_ref, wkv_ref, p_ref, s8_ref,
             dql_o, dkvl_o, dkrr_o, dwq_o, dwkv_o, dqw_o, dkvw_o):
        i = pl.program_id(0)
        cs, sn = cos_ref[...], sin_ref[...]
        d1, d2 = dr1_ref[...], dr2_ref[...]
        dq = jnp.concatenate([dqn_ref[...], d1 * cs + d2 * sn, d2 * cs - d1 * sn], axis=1) * (Q_SCALE / LOG2E)
        x = ql_ref[...]
        r = _rms(x)
        qw = qw_ref[...]
        _acc(dwq_o, _mm_tn(x * r * qw, dq), i == 0)
        dx, dw = _rms_bwd(_mm_nt(dq, wq_ref[...]), x, r, qw)
        dql_o[...] = dx
        _acc(dqw_o, dw, i == 0)
        dkv = jnp.concatenate([dkn_ref[...] * (1.0 / LOG2E), dv_ref[...]], axis=1)
        x = kvl_ref[...]
        r = _rms(x)
        kvw = kvw_ref[...]
        _acc(dwkv_o, _mm_tn(x * r * kvw, dkv), i == 0)
        dx, dw = _rms_bwd(_mm_nt(dkv, wkv_ref[...]), x, r, kvw)
        dkvl_o[...] = dx
        _acc(dkvw_o, dw, i == 0)
        dkr = _mm_exact(dkr_ref[...], s8_ref[...]) * (1.0 / LOG2E)
        dkrr_o[...] = dkr * c32_ref[...] + _mm_exact(dkr * s32_ref[...], p_ref[...])

    return _rows_call("qkv_up_bwd", body, tr, [dqn, dr1, dr2, dkn, dv, dkr8, ql, kvl, cos8, sin8, c32, s32],
                      [qw, kvw, wq_b, wkv_b, p32, sum8], [(256, F32), (128, F32), (32, F32)],
                      [((Q_LORA, 768), F32), ((KV_LORA, 1024), F32), ((1, Q_LORA), F32), ((1, KV_LORA), F32)])


def _inproj_bwd(dql, dkvl, dag, du_f, du_b, dypre, dsg, dkr, h, dout, pre_w, w_in_b, dvec, tr):
    def body(dql_ref, dkvl_ref, dag_ref, duf_ref, dub_ref, dyp_ref, dsg_ref, dkr_ref, h_ref, dout_ref,
             pw_ref, w_ref, dv_ref, dh_o, dwin_o, dpw_o):
        i = pl.program_id(0)
        du = duf_ref[...] + dub_ref[...] + dv_ref[...] * dyp_ref[...]
        dproj = jnp.concatenate([dql_ref[...], dkvl_ref[...], dag_ref[...], du, dsg_ref[...],
                                 dkr_ref[...], jnp.zeros((tr, D_IN_PAD - D_IN), F32)], axis=1)
        x = h_ref[...]
        r = _rms(x)
        pw = pw_ref[...]
        _acc(dwin_o, _mm_tn(x * r * pw, dproj), i == 0)
        dx, dw = _rms_bwd(_mm_nt(dproj, w_ref[...]), x, r, pw)
        _acc(dpw_o, dw, i == 0)
        dh_o[...] = dout_ref[...] + dx

    return _rows_call("inproj_bwd", body, tr, [dql, dkvl, dag, du_f, du_b, dypre, dsg, dkr, h, dout],
                      [pre_w, w_in_b, dvec], [(1024, F32)], [((D_MODEL, D_IN_PAD), F32), ((1, D_MODEL), F32)])


def _disc_terms(a_re, a_im, ldt):
    dt = jnp.exp(ldt)
    mag = jnp.exp(a_re * dt)
    th = a_im * dt
    cs, sn = jnp.cos(th), jnp.sin(th)
    abar_re, abar_im = mag * cs, mag * sn
    num_re, num_im = abar_re - 1.0, abar_im
    den = a_re * a_re + a_im * a_im
    coef_re = (num_re * a_re + num_im * a_im) / den
    coef_im = (num_im * a_re - num_re * a_im) / den
    return dt, mag, cs, sn, abar_re, abar_im, num_re, num_im, den, coef_re, coef_im


def _ssm_disc(a_re, a_im, ldt, bt_re, bt_im):
    def body(ar_ref, ai_ref, l_ref, br_ref, bi_ref, abr_o, abi_o, bbr_o, bbi_o):
        t = _disc_terms(ar_ref[...], ai_ref[...], l_ref[...])
        abr_o[...] = t[4]
        abi_o[...] = t[5]
        cr, ci = t[9], t[10]
        br, bi = br_ref[...], bi_ref[...]
        bbr_o[...] = cr * br - ci * bi
        bbi_o[...] = cr * bi + ci * br

    ng = a_re.shape[0]
    return pl.pallas_call(
        body, name="ssm_disc",
        out_shape=[jax.ShapeDtypeStruct((ng, 1, SSM_STATE), F32)] * 2
        + [jax.ShapeDtypeStruct((ng, SSM_GROUP, SSM_STATE), F32)] * 2)(a_re, a_im, ldt, bt_re, bt_im)


def _ssm_disc_bwd(a_re, a_im, ldt, bt_re, bt_im, da8_re, da8_im, dbb_re, dbb_im):
    def body(ar_ref, ai_ref, l_ref, br_ref, bi_ref, dar_ref, dai_ref, dbr_ref, dbi_ref,
             gar_o, gai_o, gl_o, gbr_o, gbi_o):
        a_re, a_im = ar_ref[...], ai_ref[...]
        dt, mag, cs, sn, abar_re, abar_im, num_re, num_im, den, cr, ci = _disc_terms(a_re, a_im, l_ref[...])
        br, bi = br_ref[...], bi_ref[...]
        dbr, dbi = dbr_ref[...], dbi_ref[...]
        gbr_o[...] = cr * dbr + ci * dbi
        gbi_o[...] = cr * dbi - ci * dbr
        dcr = jnp.sum(br * dbr + bi * dbi, axis=1, keepdims=True)
        dci = jnp.sum(br * dbi - bi * dbr, axis=1, keepdims=True)
        dnum_re = (dcr * a_re - dci * a_im) / den
        dnum_im = (dcr * a_im + dci * a_re) / den
        dden = -(dcr * cr + dci * ci) / den
        g_are = (dcr * num_re + dci * num_im) / den + dden * 2.0 * a_re
        g_aim = (dcr * num_im - dci * num_re) / den + dden * 2.0 * a_im
        d_abr = jnp.sum(dar_ref[...], axis=1, keepdims=True) + dnum_re
        d_abi = jnp.sum(dai_ref[...], axis=1, keepdims=True) + dnum_im
        dmag = d_abr * cs + d_abi * sn
        dth = d_abi * abar_re - d_abr * abar_im
        g_are = g_are + dmag * mag * dt
        g_aim = g_aim + dth * dt
        ddt = jnp.sum(dmag * mag * a_re + dth * a_im, axis=2, keepdims=True)
        gar_o[...] = g_are
        gai_o[...] = g_aim
        gl_o[...] = ddt * dt

    ng = a_re.shape[0]
    return pl.pallas_call(
        body, name="ssm_disc_bwd",
        out_shape=[jax.ShapeDtypeStruct((ng, 1, SSM_STATE), F32)] * 2 + [jax.ShapeDtypeStruct((ng, 1, 1), F32)]
        + [jax.ShapeDtypeStruct((ng, SSM_GROUP, SSM_STATE), F32)] * 2)(
            a_re, a_im, ldt, bt_re, bt_im, da8_re, da8_im, dbb_re, dbb_im)


def _exchange(name, per_peer, shared):
    rp, rs = per_peer.shape[1], shared.shape[0]
    n_direct = N_DEV - 1

    def body(peer_ref, shared_ref, out_ref, send_sems, recv_sems, local_sems):
        x, y, c = lax.axis_index("x"), lax.axis_index("y"), lax.axis_index("c")
        me = 4 * x + 2 * y + c
        sibling = (x, y, 1 - c)
        chips = [(1 - x, y), (x, 1 - y), (1 - x, 1 - y)]

        def index(px, py, pc):
            return 4 * px + 2 * py + pc

        def remote(src, dst, s, to):
            return pltpu.make_async_remote_copy(src_ref=src, dst_ref=dst, send_sem=send_sems.at[s],
                                                recv_sem=recv_sems.at[s], device_id=to,
                                                device_id_type=pl.DeviceIdType.MESH)

        direct = []
        for k in range(1, N_DEV):
            px = 1 - x if (k >> 2) & 1 else x
            py = 1 - y if (k >> 1) & 1 else y
            pc = 1 - c if k & 1 else c
            direct.append(remote(peer_ref.at[index(px, py, pc)], out_ref.at[me, pl.ds(0, rp), :], k - 1,
                                 (px, py, pc)))

        def block(i):
            return out_ref.at[i, pl.ds(rp, rs), :]

        def relay(k, i, to, src=None):
            return remote(block(i) if src is None else src, block(i), n_direct + k, to)

        mine = [pltpu.make_async_copy(peer_ref.at[me], out_ref.at[me, pl.ds(0, rp), :], local_sems.at[0]),
                pltpu.make_async_copy(shared_ref, block(me), local_sems.at[1])]
        first = [relay(0, me, sibling, src=shared_ref)]
        first += [relay(1 + j, me, (*chip, c), src=shared_ref) for j, chip in enumerate(chips)]
        for cp in mine + direct + first:
            cp.start()
        passed = [relay(4 + j, index(*chip, c), sibling) for j, chip in enumerate(chips)]
        for j, chip in enumerate(chips):
            relay(1 + j, index(*chip, c), (x, y, c)).wait_recv()
            passed[j].start()
        relay(0, index(*sibling), (x, y, c)).wait_recv()
        for j, chip in enumerate(chips):
            relay(4 + j, index(*chip, 1 - c), (x, y, c)).wait_recv()
        for cp in first + passed:
            cp.wait_send()
        for cp in direct + mine:
            cp.wait()

    n_sem = n_direct + N_DEV - 1
    return pl.pallas_call(
        body, name=name, out_shape=jax.ShapeDtypeStruct((N_DEV, rp + rs, LANES), F32),
        in_specs=[pl.BlockSpec(memory_space=pl.ANY)] * 2, out_specs=pl.BlockSpec(memory_space=pl.ANY),
        scratch_shapes=[pltpu.SemaphoreType.DMA((n_sem,)), pltpu.SemaphoreType.DMA((n_sem,)),
                        pltpu.SemaphoreType.DMA((2,))])(per_peer, shared)


def _gather_two_level(name, buf):
    m_per, n = buf.shape

    def body(x_ref, out_ref, send_sems, recv_sems, local_sem):
        x, y, c = lax.axis_index("x"), lax.axis_index("y"), lax.axis_index("c")
        me, sibling = (x, y, c), (x, y, 1 - c)
        chips = [(1 - x, y), (x, 1 - y), (1 - x, 1 - y)]

        def rows(px, py, pc):
            return out_ref.at[pl.ds(pl.multiple_of((4 * px + 2 * py + pc) * m_per, 8), m_per), :]

        def copy(k, block, to, src=None):
            return pltpu.make_async_remote_copy(
                src_ref=rows(*block) if src is None else src, dst_ref=rows(*block), send_sem=send_sems.at[k],
                recv_sem=recv_sems.at[k], device_id=to, device_id_type=pl.DeviceIdType.MESH)

        mine = pltpu.make_async_copy(x_ref, rows(*me), local_sem)
        mine.start()
        first = [copy(0, me, sibling, src=x_ref)]
        first += [copy(1 + j, me, (*chip, c), src=x_ref) for j, chip in enumerate(chips)]
        for cp in first:
            cp.start()
        passed = [copy(4 + j, (*chip, c), sibling) for j, chip in enumerate(chips)]
        for j, chip in enumerate(chips):
            copy(1 + j, (*chip, c), me).wait_recv()
            passed[j].start()
        copy(0, sibling, me).wait_recv()
        for j, chip in enumerate(chips):
            copy(4 + j, (*chip, 1 - c), me).wait_recv()
        for cp in first + passed:
            cp.wait_send()
        mine.wait()

    assert m_per % 8 == 0
    out = pl.pallas_call(
        body, name=name, out_shape=jax.ShapeDtypeStruct((N_DEV * m_per, n), buf.dtype),
        in_specs=[pl.BlockSpec(memory_space=pltpu.VMEM)], out_specs=pl.BlockSpec(memory_space=pltpu.VMEM),
        scratch_shapes=[pltpu.SemaphoreType.DMA((N_DEV - 1,)), pltpu.SemaphoreType.DMA((N_DEV - 1,)),
                        pltpu.SemaphoreType.DMA(())],
        compiler_params=pltpu.CompilerParams(vmem_limit_bytes=VMEM_LIMIT_V7X))(buf)
    return out.reshape(N_DEV, m_per, n)


def _adamw(recv, w, m, v, tr):
    rows = w.shape[0]
    c1 = 1.0 - ADAM_B1 ** ADAM_STEP
    c2 = 1.0 - ADAM_B2 ** ADAM_STEP

    def body(r_ref, w_ref, m_ref, v_ref, g_o, d_o, m_o, v_o):
        g = r_ref[0]
        for k in range(1, N_DEV):
            g = g + r_ref[k]
        mm = ADAM_B1 * m_ref[...] + (1.0 - ADAM_B1) * g
        vv = ADAM_B2 * v_ref[...] + (1.0 - ADAM_B2) * (g * g)
        g_o[...] = g
        m_o[...] = mm
        v_o[...] = vv
        d_o[...] = -ADAM_LR * ((mm / c1) / (jnp.sqrt(vv / c2) + ADAM_EPS) + ADAM_WD * w_ref[...])

    spec = pl.BlockSpec((tr, LANES), lambda i: (i, 0))
    return pl.pallas_call(
        body, name="adamw", grid=(rows // tr,),
        in_specs=[pl.BlockSpec((N_DEV, tr, LANES), lambda i: (0, i, 0)), spec, spec, spec],
        out_specs=[spec] * 4, out_shape=[jax.ShapeDtypeStruct((rows, LANES), F32)] * 4,
        compiler_params=_cparams("arbitrary"))(recv, w, m, v)


def _to_rows(a):
    flat = a.reshape(-1)
    pad = (-flat.shape[0]) % LANES
    if pad:
        flat = jnp.concatenate([flat, jnp.zeros((pad,), flat.dtype)])
    return flat.reshape(-1, LANES)


def _n_rows(shape):
    return -(-int(np.prod(shape)) // LANES)


def _pack(arrays, total_rows):
    rows = [_to_rows(a) for a in arrays]
    used = sum(r.shape[0] for r in rows)
    if total_rows > used:
        rows.append(jnp.zeros((total_rows - used, LANES), F32))
    return jnp.concatenate(rows, axis=0)


def _unpack(buf, shapes):
    lead = buf.shape[:-2]
    out, r0 = [], 0
    for s in shapes:
        n = int(np.prod(s))
        nr = _n_rows(s)
        out.append(buf[..., r0:r0 + nr, :].reshape(lead + (-1,))[..., :n].reshape(lead + tuple(s)))
        r0 += nr
    return out


def _pack_per_device(arrays, total_rows):
    rows = []
    for a in arrays:
        flat = a.reshape(N_DEV, -1)
        pad = (-flat.shape[1]) % LANES
        if pad:
            flat = jnp.concatenate([flat, jnp.zeros((N_DEV, pad), flat.dtype)], axis=1)
        rows.append(flat.reshape(N_DEV, -1, LANES))
    used = sum(r.shape[1] for r in rows)
    if total_rows > used:
        rows.append(jnp.zeros((N_DEV, total_rows - used, LANES), F32))
    return jnp.concatenate(rows, axis=1)


def _shard_views(name, full):
    if name == 'w_out':
        return full.reshape(N_DEV, full.shape[0] // N_DEV, full.shape[1])
    r, ccols = full.shape
    return full.reshape(r, N_DEV, ccols // N_DEV).transpose(1, 0, 2)


def _from_shards(name, stacked):
    if name == 'w_out':
        return stacked.reshape(-1, stacked.shape[-1])
    n, r, cc = stacked.shape
    return stacked.transpose(1, 0, 2).reshape(r, n * cc)


GROUPS_PER_BLK = N_GROUPS // N_COL_BLK


def _block_diag(t):
    eye = jnp.eye(GROUPS_PER_BLK, dtype=t.dtype)
    t4 = t.reshape(N_COL_BLK, GROUPS_PER_BLK, SSM_GROUP, SSM_STATE)
    return (t4[:, :, :, None, :] * eye[None, :, None, :, None]).reshape(N_COL_BLK, CH_BLK, COL_BLK)


def _diag_blocks(mat4):
    eye = jnp.eye(GROUPS_PER_BLK, dtype=mat4.dtype)
    m6 = mat4.reshape(N_COL_BLK, GROUPS_PER_BLK, SSM_GROUP, GROUPS_PER_BLK, SSM_STATE)
    return (m6 * eye[None, :, None, :, None]).sum(axis=3).reshape(N_GROUPS, SSM_GROUP, SSM_STATE)


def _step(x, loss_target, wts, moms, vels):
    seq = x.shape[1]
    n_valid = N_META + seq
    lp = -(-n_valid // 256) * 256
    tr = _pick(lp, [640, 256])
    tr_mid = 256
    tq = _pick(lp, [1280, 256])
    tk = _pick(lp, [640, 256])

    shard_shapes = [wts[n].shape[-2:] for n in SHARDED]
    n_shard_rows = sum(_n_rows(s) for s in shard_shapes)
    gathered = _gather_two_level("gather_weights",
                                 _pack([wts[n].reshape(wts[n].shape[-2:]) for n in SHARDED], n_shard_rows))
    full = {n: _from_shards(n, a) for n, a in zip(SHARDED, _unpack(gathered, shard_shapes))}

    w_in_b = jnp.concatenate([_cols_in(full['w_in']), jnp.zeros((D_MODEL, D_IN_PAD - D_IN), F32)],
                             axis=1).astype(BF16)
    wq_b = _cols_q(full['w_q_up']).astype(BF16)
    wkv_b = _cols_kv(full['w_kv_up']).astype(BF16)
    wglu_b = full['w_glu'].astype(BF16)
    wo_b = full['w_out'].astype(BF16)
    pre_w, post_w = wts['pre_norm_w'], wts['post_norm_w']
    qw, kvw, aw, sw = wts['q_norm_w'], wts['kv_norm_w'], wts['attn_out_norm_w'], wts['ssm_out_norm_w']
    bglu, dvec = wts['b_glu'], wts['ssm_d']

    lseg = lp // N_SEG
    pos = _row_position(jnp.arange(lp, dtype=jnp.int32), lseg)
    inv = ROPE_THETA ** (-jnp.arange(HALF_ROPE, dtype=F32) / HALF_ROPE)
    ang = pos.astype(F32)[:, None] * inv[None, :]
    cos, sin = jnp.cos(ang), jnp.sin(ang)
    cos8, sin8 = jnp.tile(cos, (1, HEADS)), jnp.tile(sin, (1, HEADS))
    c32 = jnp.concatenate([cos, cos], axis=1)
    s32 = jnp.concatenate([-sin, sin], axis=1)
    p32 = jnp.asarray(np.roll(np.eye(QK_ROPE, dtype=np.float32), HALF_ROPE, axis=1))
    sum8 = jnp.asarray(np.tile(np.eye(QK_ROPE, dtype=np.float32), (HEADS, 1)))
    head_sum = jnp.asarray(np.repeat(np.eye(HEADS, dtype=np.float32), V_HEAD, axis=0))

    ng = 2 * N_GROUPS
    a_re3 = wts['ssm_a_re'].reshape(ng, 1, SSM_STATE)
    a_im3 = wts['ssm_a_im'].reshape(ng, 1, SSM_STATE)
    ldt3 = wts['ssm_log_dt'].reshape(ng, 1, 1)
    bt_re = wts['ssm_b_re'].reshape(2, N_GROUPS, SSM_STATE, SSM_GROUP).transpose(0, 1, 3, 2).reshape(
        ng, SSM_GROUP, SSM_STATE)
    bt_im = wts['ssm_b_im'].reshape(2, N_GROUPS, SSM_STATE, SSM_GROUP).transpose(0, 1, 3, 2).reshape(
        ng, SSM_GROUP, SSM_STATE)
    c_re = wts['ssm_c_re'].reshape(ng, SSM_GROUP, SSM_STATE)
    c_im = wts['ssm_c_im'].reshape(ng, SSM_GROUP, SSM_STATE)
    abar_re, abar_im, bbt_re, bbt_im = _ssm_disc(a_re3, a_im3, ldt3, bt_re, bt_im)

    def direction(t, d):
        return t[d * N_GROUPS:(d + 1) * N_GROUPS]

    def slab(t, d, sign=1.0):
        return jnp.broadcast_to(sign * direction(t, d).reshape(1, N_STATES), (N_SEG, N_STATES))

    w_re = [_block_diag(direction(bbt_re, d)).astype(BF16) for d in range(2)]
    w_im = [_block_diag(direction(bbt_im, d)).astype(BF16) for d in range(2)]
    cb_re = [_block_diag(direction(c_re, d)).astype(BF16) for d in range(2)]
    cb_im = [_block_diag(-direction(c_im, d)).astype(BF16) for d in range(2)]

    def to_rows(a):
        return a.reshape(N_SEG, lseg, a.shape[-1]).transpose(1, 0, 2).reshape(lp, a.shape[-1])

    def to_tokens(a):
        return a.reshape(lseg, N_SEG, a.shape[-1]).transpose(1, 0, 2).reshape(lp, a.shape[-1])

    pad = jnp.zeros((lp - n_valid, D_MODEL), F32)
    h = to_rows(jnp.concatenate([full['meta_tokens'], x[0], pad], axis=0))
    tgt = to_rows(jnp.concatenate([jnp.zeros((N_META, D_MODEL), F32), loss_target[0], pad], axis=0))

    ql, kvl, ag, su, sg, kr = _inproj(h, pre_w, w_in_b, tr)
    qn_b, qr1_b, qr2_b, kn_b, v_b, kr_b = _qkv_up(ql, kvl, kr, cos8, sin8, c32, s32, qw, kvw, wq_b, wkv_b, p32, tr)

    def heads(a, w):
        return a.reshape(lp, HEADS, w)

    nq, nk = lp // tq, lp // tk
    q_t = jnp.concatenate([heads(qn_b, 64), heads(qr1_b, 16), heads(qr2_b, 16)], axis=-1)
    k_t = jnp.concatenate([heads(kn_b, 64), jnp.broadcast_to(kr_b[:, None, :], (lp, HEADS, QK_ROPE))], axis=-1)
    v_t = heads(v_b, 64)
    vx_t = jnp.concatenate([v_t, jnp.ones((lp, HEADS, 1), BF16), jnp.zeros((lp, HEADS, LANES - V_HEAD - 1), BF16)],
                           axis=-1)
    qt4 = q_t.reshape(nq, tq, HEADS, QK_DIM).transpose(2, 0, 3, 1)
    tk_fwd = _pick(lp, [1280, 256])
    vxt4 = vx_t.reshape(lp // tk_fwd, tk_fwd, HEADS, LANES).transpose(2, 0, 3, 1)
    k_h = k_t.transpose(1, 0, 2)
    kt_h = k_t.transpose(1, 2, 0)
    v_h = v_t.transpose(1, 0, 2)
    ot_h, lse4 = _attn_fwd(qt4, k_h, vxt4, n_valid)
    o_flat = ot_h.transpose(2, 0, 1).reshape(lp, D_ATTN)

    xs, ys = [], []
    for d in range(2):
        ar8, ai8 = slab(abar_re, d), slab(abar_im, d)
        ere, eim = _scan_ends(f"scan{d}_ends", su, w_re[d], w_im[d], ar8, ai8, d == 0)
        x_re, x_im, y_d = _scan_fwd(f"scan{d}", su, w_re[d], w_im[d], ar8, ai8, ere, eim, cb_re[d], cb_im[d],
                                    d == 0)
        xs += [x_re, x_im]
        ys.append(y_d)
    ypre, glu, ysn = _ssm_post(ys[0], ys[1], su, sg, wglu_b, bglu, sw, dvec, tr)

    dout, do_flat, dag, dysn, delta8, loss, d_post, d_wo, d_aw = _out_fwd_bwd(
        o_flat, ag, ysn, h, tgt, wo_b, post_w, aw, head_sum, n_valid, tr)
    dypre, dsg, d_wglu, d_bglu, d_sw, d_dvec = _ssm_post_bwd(dysn, glu, sg, ypre, su, wglu_b, sw, dvec, tr)

    dus, d_ct, d_wb, d_a8 = [], [], [], []
    for d in range(2):
        ar8, ai8c = slab(abar_re, d), slab(abar_im, d, -1.0)
        ere, eim = _scan_ends(f"scan_adj{d}_ends", dypre, cb_re[d], cb_im[d], ar8, ai8c, d != 0)
        du_d, dw_re, dw_im, dc_re, dc_im, da_re, da_im = _scan_bwd(
            f"scan_adj{d}", dypre, cb_re[d], cb_im[d], ar8, ai8c, ere, eim, su, w_re[d], w_im[d],
            xs[2 * d], xs[2 * d + 1], d != 0)
        dus.append(du_d)
        d_ct.append((dc_re, dc_im))
        d_wb.append((dw_re, dw_im))
        d_a8.append((da_re, da_im))

    dot4 = do_flat.astype(BF16).reshape(nq, tq, HEADS, V_HEAD).transpose(2, 0, 3, 1)
    dqt4, dkt_h, dvt_h = _attn_bwd(qt4, k_h, kt_h, v_h, dot4, lse4, delta8.T.reshape(HEADS, nq, 1, tq), tk)
    dq_t = dqt4.transpose(1, 3, 0, 2).reshape(lp, HEADS, QK_DIM)
    dk_t = dkt_h.transpose(2, 0, 1)
    dqn = dq_t[:, :, :64].reshape(lp, 512)
    dr1 = dq_t[:, :, 64:80].reshape(lp, 128)
    dr2 = dq_t[:, :, 80:96].reshape(lp, 128)
    dkn = dk_t[:, :, :64].reshape(lp, 512)
    dkr8 = dk_t[:, :, 64:].reshape(lp, HEADS * QK_ROPE)
    dvf = dvt_h.transpose(2, 0, 1).reshape(lp, 512)
    dql, dkvl, dkrr, d_wq, d_wkv, d_qw, d_kvw = _qkv_up_bwd(
        dqn, dr1, dr2, dkn, dvf, dkr8, ql, kvl, cos8, sin8, c32, s32, qw, kvw, wq_b, wkv_b, p32, sum8, tr)
    dh, d_win, d_pre = _inproj_bwd(dql, dkvl, dag, dus[0], dus[1], dypre, dsg, dkrr, h, dout, pre_w, w_in_b, dvec,
                                   tr_mid)
    dh = to_tokens(dh)

    def seg_sums(t):
        return t.reshape(N_SEG, N_GROUPS, SSM_STATE).transpose(1, 0, 2)

    da8_re = jnp.concatenate([seg_sums(d_a8[d][0]) for d in range(2)], axis=0)
    da8_im = jnp.concatenate([seg_sums(d_a8[d][1]) for d in range(2)], axis=0)
    dbb_re = jnp.concatenate([_diag_blocks(d_wb[d][0]) for d in range(2)], axis=0)
    dbb_im = jnp.concatenate([_diag_blocks(d_wb[d][1]) for d in range(2)], axis=0)
    g_are, g_aim, g_ldt, g_bt_re, g_bt_im = _ssm_disc_bwd(a_re3, a_im3, ldt3, bt_re, bt_im, da8_re, da8_im,
                                                          dbb_re, dbb_im)
    g_c_re = jnp.concatenate([_diag_blocks(d_ct[d][0]) for d in range(2)], axis=0)
    g_c_im = jnp.concatenate([-_diag_blocks(d_ct[d][1]) for d in range(2)], axis=0)

    def b_layout(t):
        return t.reshape(2, N_GROUPS, SSM_GROUP, SSM_STATE).transpose(0, 1, 3, 2)

    local = {
        'meta_tokens': dh[:N_META],
        'pre_norm_w': d_pre, 'post_norm_w': d_post,
        'w_in': _cols_in_inv(d_win[:, :D_IN]),
        'q_norm_w': d_qw, 'w_q_up': _cols_q_inv(d_wq),
        'kv_norm_w': d_kvw, 'w_kv_up': _cols_kv_inv(d_wkv),
        'attn_out_norm_w': d_aw,
        'ssm_a_re': g_are, 'ssm_a_im': g_aim, 'ssm_log_dt': g_ldt,
        'ssm_b_re': b_layout(g_bt_re), 'ssm_b_im': b_layout(g_bt_im), 'ssm_c_re': g_c_re, 'ssm_c_im': g_c_im,
        'ssm_d': d_dvec, 'w_glu': d_wglu, 'b_glu': d_bglu, 'ssm_out_norm_w': d_sw, 'w_out': d_wo,
    }

    replicated = [n for n in WEIGHTS if n not in SHARDED]
    order = SHARDED + replicated
    shapes = [wts[n].shape for n in order] + [(1, 1)]
    tr_adam = 512
    total_rows = -(-sum(_n_rows(s) for s in shapes) // tr_adam) * tr_adam
    recv = _exchange("exchange_grads",
                     _pack_per_device([_shard_views(n, local[n]) for n in SHARDED], n_shard_rows),
                     _pack([local[n] for n in replicated] + [loss], total_rows - n_shard_rows))
    zero = jnp.zeros((1, 1), F32)
    packed = [_pack([src[n] for n in order] + [zero], total_rows) for src in (wts, moms, vels)]
    g_p, d_p, m_p, v_p = _adamw(recv, *packed, tr_adam)
    sums = _unpack(g_p, shapes)
    grads = dict(zip(order, sums))
    deltas, new_m, new_v = (dict(zip(order, _unpack(b, shapes))) for b in (d_p, m_p, v_p))

    grad_x = dh[N_META:n_valid][None]
    return (sums[-1][0, 0], grad_x, *[grads[n] for n in WEIGHTS], *[deltas[n] for n in WEIGHTS],
            *[new_m[n] for n in WEIGHTS], *[new_v[n] for n in WEIGHTS])


def kernel(x, meta_tokens, pre_norm_w, post_norm_w, w_in, q_norm_w, w_q_up, kv_norm_w, w_kv_up, attn_out_norm_w, ssm_a_re, ssm_a_im, ssm_log_dt, ssm_b_re, ssm_b_im, ssm_c_re, ssm_c_im, ssm_d, w_glu, b_glu, ssm_out_norm_w, w_out, loss_target, m_meta_tokens, m_pre_norm_w, m_post_norm_w, m_w_in, m_q_norm_w, m_w_q_up, m_kv_norm_w, m_w_kv_up, m_attn_out_norm_w, m_ssm_a_re, m_ssm_a_im, m_ssm_log_dt, m_ssm_b_re, m_ssm_b_im, m_ssm_c_re, m_ssm_c_im, m_ssm_d, m_w_glu, m_b_glu, m_ssm_out_norm_w, m_w_out, v_meta_tokens, v_pre_norm_w, v_post_norm_w, v_w_in, v_q_norm_w, v_w_q_up, v_kv_norm_w, v_w_kv_up, v_attn_out_norm_w, v_ssm_a_re, v_ssm_a_im, v_ssm_log_dt, v_ssm_b_re, v_ssm_b_im, v_ssm_c_re, v_ssm_c_im, v_ssm_d, v_w_glu, v_b_glu, v_ssm_out_norm_w, v_w_out):
    wts = dict(zip(WEIGHTS, (meta_tokens, pre_norm_w, post_norm_w, w_in, q_norm_w, w_q_up, kv_norm_w, w_kv_up,
                             attn_out_norm_w, ssm_a_re, ssm_a_im, ssm_log_dt, ssm_b_re, ssm_b_im, ssm_c_re,
                             ssm_c_im, ssm_d, w_glu, b_glu, ssm_out_norm_w, w_out)))
    moms = dict(zip(WEIGHTS, (m_meta_tokens, m_pre_norm_w, m_post_norm_w, m_w_in, m_q_norm_w, m_w_q_up,
                              m_kv_norm_w, m_w_kv_up, m_attn_out_norm_w, m_ssm_a_re, m_ssm_a_im, m_ssm_log_dt,
                              m_ssm_b_re, m_ssm_b_im, m_ssm_c_re, m_ssm_c_im, m_ssm_d, m_w_glu, m_b_glu,
                              m_ssm_out_norm_w, m_w_out)))
    vels = dict(zip(WEIGHTS, (v_meta_tokens, v_pre_norm_w, v_post_norm_w, v_w_in, v_q_norm_w, v_w_q_up,
                              v_kv_norm_w, v_w_kv_up, v_attn_out_norm_w, v_ssm_a_re, v_ssm_a_im, v_ssm_log_dt,
                              v_ssm_b_re, v_ssm_b_im, v_ssm_c_re, v_ssm_c_im, v_ssm_d, v_w_glu, v_b_glu,
                              v_ssm_out_norm_w, v_w_out)))
    return _step(x, loss_target, wts, moms, vels)
```
